```python
import jax, jax.numpy as jnp
from jax import lax
import numpy as np

D_MODEL = 1024
BATCH = 8
SEQ = 4096
DEPTH = 2

CHUNK = 64
CONV_WIDTH = 3
D_CONV = 1024
D_GMLP = 1024
GMLP_BLOCK = 128
N_GROUPS_GMLP = 8
D_POOL = 1024
POOL_WINDOWS = (2, 4, 8, 16)
POOL_GROUP = D_POOL // len(POOL_WINDOWS)
N_BRANCHES = 3
D_FF = 2816
D_IN = 3 * D_CONV + 2 * D_GMLP + D_POOL + N_BRANCHES * D_MODEL
ALPHA = (2 * DEPTH) ** 0.25
BETA = (8 * DEPTH) ** -0.25
LN_EPS = 1e-5

kernel_name = "hybrid_conv_gmlp_pool_deepnorm_adaln"


def layer_norm(x, g, b):
    xf = x.astype(jnp.float32)
    mu = jnp.mean(xf, axis=-1, keepdims=True)
    var = jnp.mean(jnp.square(xf - mu), axis=-1, keepdims=True)
    y = (xf - mu) * lax.rsqrt(var + LN_EPS)
    return (y * g.astype(jnp.float32) + b.astype(jnp.float32)).astype(x.dtype)


def causal_dwconv(x, w):
    k, ch = w.shape
    return lax.conv_general_dilated(
        x, w[:, None, :].astype(x.dtype), window_strides=(1,), padding=[(k - 1, 0)],
        dimension_numbers=("NWC", "WIO", "NWC"), feature_group_count=ch)


def spatial_gating(u, v, ln_g, ln_b, w_s, b_s):
    bn, s, _ = v.shape
    v = layer_norm(v, ln_g, ln_b)
    vb = v.reshape(bn, s // GMLP_BLOCK, GMLP_BLOCK, N_GROUPS_GMLP, D_GMLP // N_GROUPS_GMLP)
    pos = jnp.arange(GMLP_BLOCK)
    allowed = (pos[None, :] // CHUNK) <= (pos[:, None] // CHUNK)
    w = jnp.where(allowed[None], w_s, jnp.zeros_like(w_s))
    mixed = jnp.einsum("gij,bnjgc->bnigc", w, vb) + b_s.T[None, None, :, :, None]
    return u * mixed.reshape(bn, s, D_GMLP)


def multiscale_pool(p, w_pool, scale):
    s = p.shape[1]
    pf = p.astype(jnp.float32)
    cs = jnp.cumsum(pf, axis=1)
    t = jnp.arange(1, s + 1, dtype=jnp.float32)
    outs = []
    for k, win in enumerate(POOL_WINDOWS):
        lo, hi = k * POOL_GROUP, (k + 1) * POOL_GROUP
        csk = cs[..., lo:hi]
        prev = jnp.pad(csk, ((0, 0), (win, 0), (0, 0)))[:, :s]
        mean = (csk - prev) / jnp.minimum(t, float(win))[None, :, None]
        d = (mean - pf[..., lo:hi]).astype(p.dtype)
        outs.append(d @ w_pool[k])
    return jnp.concatenate(outs, axis=-1) * scale


def _fwd_setup_inputs(seed: int = 0) -> dict:
    key = jax.random.key(seed)
    ks = jax.random.split(key, 26)
    f32 = jnp.float32

    def nrm(k, shape, s):
        return jax.random.normal(k, shape, f32) * s

    L = DEPTH
    w_in_scale = D_MODEL ** -0.5
    return {
        "x": nrm(ks[0], (BATCH, SEQ, D_MODEL), 1.0),
        "c": nrm(ks[1], (BATCH, D_MODEL), 1.0),
        "w_ada": nrm(ks[2], (L, D_MODEL, 6 * D_MODEL), 0.5 * D_MODEL ** -0.5),
        "b_ada": nrm(ks[3], (L, 6 * D_MODEL), 0.02),
        "w_in": nrm(ks[4], (L, D_MODEL, D_IN), w_in_scale),
        "b_in": nrm(ks[5], (L, D_IN), 0.01),
        "conv_a": nrm(ks[6], (L, CONV_WIDTH, D_CONV), 0.5),
        "w_a_out": nrm(ks[7], (L, D_CONV, D_MODEL), D_CONV ** -0.5),
        "ln_v_g": 1.0 + nrm(ks[8], (L, D_GMLP), 0.02),
        "ln_v_b": nrm(ks[9], (L, D_GMLP), 0.02),
        "w_spatial": nrm(ks[10], (L, N_GROUPS_GMLP, GMLP_BLOCK, GMLP_BLOCK), 0.5 * GMLP_BLOCK ** -0.5),
        "b_spatial": 1.0 + nrm(ks[11], (L, N_GROUPS_GMLP, GMLP_BLOCK), 0.02),
        "w_b_out": nrm(ks[12], (L, D_GMLP, D_MODEL), D_GMLP ** -0.5),
        "w_pool": nrm(ks[13], (L, len(POOL_WINDOWS), POOL_GROUP, POOL_GROUP), POOL_GROUP ** -0.5),
        "pool_scale": 1.0 + nrm(ks[14], (L, D_POOL), 0.02),
        "w_o": nrm(ks[15], (L, D_MODEL, D_MODEL), BETA * D_MODEL ** -0.5),
        "ln1_g": 1.0 + nrm(ks[16], (L, D_MODEL), 0.02),
        "ln1_b": nrm(ks[17], (L, D_MODEL), 0.02),
        "w_up": nrm(ks[18], (L, D_MODEL, 2 * D_FF), w_in_scale),
        "b_up": nrm(ks[19], (L, 2 * D_FF), 0.01),
        "conv_ffn": nrm(ks[20], (L, CONV_WIDTH, D_FF), 0.5),
        "conv_ffn_b": nrm(ks[21], (L, D_FF), 0.01),
        "w_down": nrm(ks[22], (L, D_FF, D_MODEL), BETA * D_FF ** -0.5),
        "ln2_g": 1.0 + nrm(ks[23], (L, D_MODEL), 0.02),
        "ln2_b": nrm(ks[24], (L, D_MODEL), 0.02),
    }


def _fwd_reference(x, c, w_ada, b_ada, w_in, b_in, conv_a, w_a_out, ln_v_g, ln_v_b,
              w_spatial, b_spatial, w_b_out, w_pool, pool_scale, w_o, ln1_g, ln1_b,
              w_up, b_up, conv_ffn, conv_ffn_b, w_down, ln2_g, ln2_b):
    split_points = [D_CONV, 2 * D_CONV, 3 * D_CONV, 3 * D_CONV + D_GMLP,
                    3 * D_CONV + 2 * D_GMLP, 3 * D_CONV + 2 * D_GMLP + D_POOL]
    c_act = jax.nn.silu(c)
    for l in range(DEPTH):
        ada = (c_act @ w_ada[l] + b_ada[l])[:, None, :]
        sh1, sc1, gt1, sh2, sc2, gt2 = jnp.split(ada, 6, axis=-1)

        h = x * (1.0 + sc1) + sh1
        z = h @ w_in[l] + b_in[l]
        zb, zc, zx, zu, zv, zp, zg = jnp.split(z, split_points, axis=-1)
        y_a = (zb * causal_dwconv(zc * zx, conv_a[l])) @ w_a_out[l]
        y_b = spatial_gating(jax.nn.gelu(zu), jax.nn.gelu(zv), ln_v_g[l], ln_v_b[l],
                             w_spatial[l], b_spatial[l]) @ w_b_out[l]
        y_c = multiscale_pool(zp, w_pool[l], pool_scale[l])
        g_a, g_b, g_c = jnp.split(jax.nn.sigmoid(zg), 3, axis=-1)
        merged = g_a * y_a + g_b * y_b + g_c * y_c
        x = layer_norm(ALPHA * x + gt1 * (merged @ w_o[l]), ln1_g[l], ln1_b[l])

        h = x * (1.0 + sc2) + sh2
        up_a, up_g = jnp.split(h @ w_up[l] + b_up[l], 2, axis=-1)
        f = jax.nn.gelu(causal_dwconv(up_a, conv_ffn[l]) + conv_ffn_b[l]) * up_g
        x = layer_norm(ALPHA * x + gt2 * (f @ w_down[l]), ln2_g[l], ln2_b[l])
    return x


import jax as _jax
import jax.numpy as _jnp

TWIN_FORMAT = 'train_step'
FWD_PARAMS = ['x', 'c', 'w_ada', 'b_ada', 'w_in', 'b_in', 'conv_a', 'w_a_out', 'ln_v_g', 'ln_v_b', 'w_spatial', 'b_spatial', 'w_b_out', 'w_pool', 'pool_scale', 'w_o', 'ln1_g', 'ln1_b', 'w_up', 'b_up', 'conv_ffn', 'conv_ffn_b', 'w_down', 'ln2_g', 'ln2_b']
TWIN_WEIGHTS = ['w_ada', 'b_ada', 'w_in', 'b_in', 'conv_a', 'w_a_out', 'ln_v_g', 'ln_v_b', 'w_spatial', 'b_spatial', 'w_b_out', 'w_pool', 'pool_scale', 'w_o', 'ln1_g', 'ln1_b', 'w_up', 'b_up', 'conv_ffn', 'conv_ffn_b', 'w_down', 'ln2_g', 'ln2_b']
TWIN_DIFF_INPUT = 'x'
TWIN_INPUTS = ['x', 'c', 'w_ada', 'b_ada', 'w_in', 'b_in', 'conv_a', 'w_a_out', 'ln_v_g', 'ln_v_b', 'w_spatial', 'b_spatial', 'w_b_out', 'w_pool', 'pool_scale', 'w_o', 'ln1_g', 'ln1_b', 'w_up', 'b_up', 'conv_ffn', 'conv_ffn_b', 'w_down', 'ln2_g', 'ln2_b', 'loss_target', 'm_w_ada', 'm_b_ada', 'm_w_in', 'm_b_in', 'm_conv_a', 'm_w_a_out', 'm_ln_v_g', 'm_ln_v_b', 'm_w_spatial', 'm_b_spatial', 'm_w_b_out', 'm_w_pool', 'm_pool_scale', 'm_w_o', 'm_ln1_g', 'm_ln1_b', 'm_w_up', 'm_b_up', 'm_conv_ffn', 'm_conv_ffn_b', 'm_w_down', 'm_ln2_g', 'm_ln2_b', 'v_w_ada', 'v_b_ada', 'v_w_in', 'v_b_in', 'v_conv_a', 'v_w_a_out', 'v_ln_v_g', 'v_ln_v_b', 'v_w_spatial', 'v_b_spatial', 'v_w_b_out', 'v_w_pool', 'v_pool_scale', 'v_w_o', 'v_ln1_g', 'v_ln1_b', 'v_w_up', 'v_b_up', 'v_conv_ffn', 'v_conv_ffn_b', 'v_w_down', 'v_ln2_g', 'v_ln2_b']
TWIN_OUTPUTS = ['loss', 'grad_x', 'grad_w_ada', 'grad_b_ada', 'grad_w_in', 'grad_b_in', 'grad_conv_a', 'grad_w_a_out', 'grad_ln_v_g', 'grad_ln_v_b', 'grad_w_spatial', 'grad_b_spatial', 'grad_w_b_out', 'grad_w_pool', 'grad_pool_scale', 'grad_w_o', 'grad_ln1_g', 'grad_ln1_b', 'grad_w_up', 'grad_b_up', 'grad_conv_ffn', 'grad_conv_ffn_b', 'grad_w_down', 'grad_ln2_g', 'grad_ln2_b', 'delta_w_ada', 'delta_b_ada', 'delta_w_in', 'delta_b_in', 'delta_conv_a', 'delta_w_a_out', 'delta_ln_v_g', 'delta_ln_v_b', 'delta_w_spatial', 'delta_b_spatial', 'delta_w_b_out', 'delta_w_pool', 'delta_pool_scale', 'delta_w_o', 'delta_ln1_g', 'delta_ln1_b', 'delta_w_up', 'delta_b_up', 'delta_conv_ffn', 'delta_conv_ffn_b', 'delta_w_down', 'delta_ln2_g', 'delta_ln2_b', 'new_m_w_ada', 'new_m_b_ada', 'new_m_w_in', 'new_m_b_in', 'new_m_conv_a', 'new_m_w_a_out', 'new_m_ln_v_g', 'new_m_ln_v_b', 'new_m_w_spatial', 'new_m_b_spatial', 'new_m_w_b_out', 'new_m_w_pool', 'new_m_pool_scale', 'new_m_w_o', 'new_m_ln1_g', 'new_m_ln1_b', 'new_m_w_up', 'new_m_b_up', 'new_m_conv_ffn', 'new_m_conv_ffn_b', 'new_m_w_down', 'new_m_ln2_g', 'new_m_ln2_b', 'new_v_w_ada', 'new_v_b_ada', 'new_v_w_in', 'new_v_b_in', 'new_v_conv_a', 'new_v_w_a_out', 'new_v_ln_v_g', 'new_v_ln_v_b', 'new_v_w_spatial', 'new_v_b_spatial', 'new_v_w_b_out', 'new_v_w_pool', 'new_v_pool_scale', 'new_v_w_o', 'new_v_ln1_g', 'new_v_ln1_b', 'new_v_w_up', 'new_v_b_up', 'new_v_conv_ffn', 'new_v_conv_ffn_b', 'new_v_w_down', 'new_v_ln2_g', 'new_v_ln2_b']
TWIN_LEAF_KINDS = {'loss': 'loss', 'grad_x': 'grad_x', 'grad_w_ada': 'grad_w', 'grad_b_ada': 'grad_w', 'grad_w_in': 'grad_w', 'grad_b_in': 'grad_w', 'grad_conv_a': 'grad_w', 'grad_w_a_out': 'grad_w', 'grad_ln_v_g': 'grad_w', 'grad_ln_v_b': 'grad_w', 'grad_w_spatial': 'grad_w', 'grad_b_spatial': 'grad_w', 'grad_w_b_out': 'grad_w', 'grad_w_pool': 'grad_w', 'grad_pool_scale': 'grad_w', 'grad_w_o': 'grad_w', 'grad_ln1_g': 'grad_w', 'grad_ln1_b': 'grad_w', 'grad_w_up': 'grad_w', 'grad_b_up': 'grad_w', 'grad_conv_ffn': 'grad_w', 'grad_conv_ffn_b': 'grad_w', 'grad_w_down': 'grad_w', 'grad_ln2_g': 'grad_w', 'grad_ln2_b': 'grad_w', 'delta_w_ada': 'delta_w', 'delta_b_ada': 'delta_w', 'delta_w_in': 'delta_w', 'delta_b_in': 'delta_w', 'delta_conv_a': 'delta_w', 'delta_w_a_out': 'delta_w', 'delta_ln_v_g': 'delta_w', 'delta_ln_v_b': 'delta_w', 'delta_w_spatial': 'delta_w', 'delta_b_spatial': 'delta_w', 'delta_w_b_out': 'delta_w', 'delta_w_pool': 'delta_w', 'delta_pool_scale': 'delta_w', 'delta_w_o': 'delta_w', 'delta_ln1_g': 'delta_w', 'delta_ln1_b': 'delta_w', 'delta_w_up': 'delta_w', 'delta_b_up': 'delta_w', 'delta_conv_ffn': 'delta_w', 'delta_conv_ffn_b': 'delta_w', 'delta_w_down': 'delta_w', 'delta_ln2_g': 'delta_w', 'delta_ln2_b': 'delta_w', 'new_m_w_ada': 'new_m', 'new_m_b_ada': 'new_m', 'new_m_w_in': 'new_m', 'new_m_b_in': 'new_m', 'new_m_conv_a': 'new_m', 'new_m_w_a_out': 'new_m', 'new_m_ln_v_g': 'new_m', 'new_m_ln_v_b': 'new_m', 'new_m_w_spatial': 'new_m', 'new_m_b_spatial': 'new_m', 'new_m_w_b_out': 'new_m', 'new_m_w_pool': 'new_m', 'new_m_pool_scale': 'new_m', 'new_m_w_o': 'new_m', 'new_m_ln1_g': 'new_m', 'new_m_ln1_b': 'new_m', 'new_m_w_up': 'new_m', 'new_m_b_up': 'new_m', 'new_m_conv_ffn': 'new_m', 'new_m_conv_ffn_b': 'new_m', 'new_m_w_down': 'new_m', 'new_m_ln2_g': 'new_m', 'new_m_ln2_b': 'new_m', 'new_v_w_ada': 'new_v', 'new_v_b_ada': 'new_v', 'new_v_w_in': 'new_v', 'new_v_b_in': 'new_v', 'new_v_conv_a': 'new_v', 'new_v_w_a_out': 'new_v', 'new_v_ln_v_g': 'new_v', 'new_v_ln_v_b': 'new_v', 'new_v_w_spatial': 'new_v', 'new_v_b_spatial': 'new_v', 'new_v_w_b_out': 'new_v', 'new_v_w_pool': 'new_v', 'new_v_pool_scale': 'new_v', 'new_v_w_o': 'new_v', 'new_v_ln1_g': 'new_v', 'new_v_ln1_b': 'new_v', 'new_v_w_up': 'new_v', 'new_v_b_up': 'new_v', 'new_v_conv_ffn': 'new_v', 'new_v_conv_ffn_b': 'new_v', 'new_v_w_down': 'new_v', 'new_v_ln2_g': 'new_v', 'new_v_ln2_b': 'new_v'}


def _forward(args):
    return _fwd_reference(*[args[k] for k in FWD_PARAMS])


def _output_shape():
    def fwd():
        inp = _fwd_setup_inputs(0)
        return _fwd_reference(*[inp[k] for k in FWD_PARAMS])
    out = _jax.eval_shape(fwd)
    return out.shape, out.dtype

N_MICROBATCH = 1
ADAM_LR = 0.001
ADAM_B1 = 0.9
ADAM_B2 = 0.999
ADAM_EPS = 1e-08
ADAM_WD = 0.01
ADAM_STEP = 10
PER_EXAMPLE_BATCH_AXIS = {'x': 0, 'c': 0, 'loss_target': 0}
SHARED_INPUTS = []
_WEIGHT_DTYPES = {'w_ada': _jnp.float32, 'b_ada': _jnp.float32, 'w_in': _jnp.float32, 'b_in': _jnp.float32, 'conv_a': _jnp.float32, 'w_a_out': _jnp.float32, 'ln_v_g': _jnp.float32, 'ln_v_b': _jnp.float32, 'w_spatial': _jnp.float32, 'b_spatial': _jnp.float32, 'w_b_out': _jnp.float32, 'w_pool': _jnp.float32, 'pool_scale': _jnp.float32, 'w_o': _jnp.float32, 'ln1_g': _jnp.float32, 'ln1_b': _jnp.float32, 'w_up': _jnp.float32, 'b_up': _jnp.float32, 'conv_ffn': _jnp.float32, 'conv_ffn_b': _jnp.float32, 'w_down': _jnp.float32, 'ln2_g': _jnp.float32, 'ln2_b': _jnp.float32}
MOMENT_SCALE = {'w_ada': 2.039412e-02, 'b_ada': 3.516200e-02, 'w_in': 8.453433e-03, 'b_in': 7.404279e-03, 'conv_a': 1.351756e-02, 'w_a_out': 1.171017e-02, 'ln_v_g': 3.312780e-03, 'ln_v_b': 3.191338e-03, 'w_spatial': 6.477592e-03, 'b_spatial': 7.562147e-03, 'w_b_out': 9.091347e-03, 'w_pool': 9.586785e-03, 'pool_scale': 9.756852e-03, 'w_o': 3.529475e-02, 'ln1_g': 1.178191e+00, 'ln1_b': 5.794743e-01, 'w_up': 7.904818e-03, 'b_up': 7.934066e-03, 'conv_ffn': 9.305587e-03, 'conv_ffn_b': 8.286220e-03, 'w_down': 2.588947e-02, 'ln2_g': 2.267394e+01, 'ln2_b': 9.879609e-01}


def _to_microbatches(a, axis):
    t = _jnp.moveaxis(a, axis, 0)
    t = t.reshape((N_MICROBATCH, t.shape[0] // N_MICROBATCH) + t.shape[1:])
    return _jnp.moveaxis(t, 1, axis + 1)


def setup_inputs(seed: int = 0) -> dict:
    inp = _fwd_setup_inputs(seed)
    key = _jax.random.fold_in(_jax.random.key(seed), 7919)
    shape, _ = _output_shape()
    out = dict(inp)
    out["loss_target"] = _jax.random.normal(_jax.random.fold_in(key, 0), shape, _jnp.float32)
    for i, name in enumerate(TWIN_WEIGHTS):
        w = inp[name].astype(_jnp.float32)
        if MOMENT_SCALE is None:
            s = _jnp.sqrt(_jnp.mean(_jnp.square(w)) + 1e-30)
        else:
            s = MOMENT_SCALE[name]
        km, kv = _jax.random.split(_jax.random.fold_in(key, i + 1))
        out[name] = w
        out["m_" + name] = s * _jax.random.normal(km, w.shape, _jnp.float32)
        out["v_" + name] = (s * s) * _jax.random.uniform(kv, w.shape, _jnp.float32, 0.5, 1.5)
    if N_MICROBATCH > 1:
        for name, axis in PER_EXAMPLE_BATCH_AXIS.items():
            out[name] = _to_microbatches(out[name], axis)
    return {'x': out['x'], 'c': out['c'], 'w_ada': out['w_ada'], 'b_ada': out['b_ada'], 'w_in': out['w_in'], 'b_in': out['b_in'], 'conv_a': out['conv_a'], 'w_a_out': out['w_a_out'], 'ln_v_g': out['ln_v_g'], 'ln_v_b': out['ln_v_b'], 'w_spatial': out['w_spatial'], 'b_spatial': out['b_spatial'], 'w_b_out': out['w_b_out'], 'w_pool': out['w_pool'], 'pool_scale': out['pool_scale'], 'w_o': out['w_o'], 'ln1_g': out['ln1_g'], 'ln1_b': out['ln1_b'], 'w_up': out['w_up'], 'b_up': out['b_up'], 'conv_ffn': out['conv_ffn'], 'conv_ffn_b': out['conv_ffn_b'], 'w_down': out['w_down'], 'ln2_g': out['ln2_g'], 'ln2_b': out['ln2_b'], 'loss_target': out['loss_target'], 'm_w_ada': out['m_w_ada'], 'm_b_ada': out['m_b_ada'], 'm_w_in': out['m_w_in'], 'm_b_in': out['m_b_in'], 'm_conv_a': out['m_conv_a'], 'm_w_a_out': out['m_w_a_out'], 'm_ln_v_g': out['m_ln_v_g'], 'm_ln_v_b': out['m_ln_v_b'], 'm_w_spatial': out['m_w_spatial'], 'm_b_spatial': out['m_b_spatial'], 'm_w_b_out': out['m_w_b_out'], 'm_w_pool': out['m_w_pool'], 'm_pool_scale': out['m_pool_scale'], 'm_w_o': out['m_w_o'], 'm_ln1_g': out['m_ln1_g'], 'm_ln1_b': out['m_ln1_b'], 'm_w_up': out['m_w_up'], 'm_b_up': out['m_b_up'], 'm_conv_ffn': out['m_conv_ffn'], 'm_conv_ffn_b': out['m_conv_ffn_b'], 'm_w_down': out['m_w_down'], 'm_ln2_g': out['m_ln2_g'], 'm_ln2_b': out['m_ln2_b'], 'v_w_ada': out['v_w_ada'], 'v_b_ada': out['v_b_ada'], 'v_w_in': out['v_w_in'], 'v_b_in': out['v_b_in'], 'v_conv_a': out['v_conv_a'], 'v_w_a_out': out['v_w_a_out'], 'v_ln_v_g': out['v_ln_v_g'], 'v_ln_v_b': out['v_ln_v_b'], 'v_w_spatial': out['v_w_spatial'], 'v_b_spatial': out['v_b_spatial'], 'v_w_b_out': out['v_w_b_out'], 'v_w_pool': out['v_w_pool'], 'v_pool_scale': out['v_pool_scale'], 'v_w_o': out['v_w_o'], 'v_ln1_g': out['v_ln1_g'], 'v_ln1_b': out['v_ln1_b'], 'v_w_up': out['v_w_up'], 'v_b_up': out['v_b_up'], 'v_conv_ffn': out['v_conv_ffn'], 'v_conv_ffn_b': out['v_conv_ffn_b'], 'v_w_down': out['v_w_down'], 'v_ln2_g': out['v_ln2_g'], 'v_ln2_b': out['v_ln2_b']}


def _loss(weights, diff, rest, loss_target):
    with _jax.named_scope("forward"):
        args = {**rest, TWIN_DIFF_INPUT: diff, **{k: w.astype(_WEIGHT_DTYPES[k]) for k, w in weights.items()}}
        y = _forward(args)
    with _jax.named_scope("loss_head"):
        err = _jnp.square(y.astype(_jnp.float32) - loss_target)
        return 0.5 * _jnp.sum(_jnp.mean(err, axis=-1)) if err.ndim else 0.5 * err


def _adamw(w, g, m, v):
    m = ADAM_B1 * m + (1.0 - ADAM_B1) * g
    v = ADAM_B2 * v + (1.0 - ADAM_B2) * _jnp.square(g)
    m_hat = m / (1.0 - ADAM_B1 ** ADAM_STEP)
    v_hat = v / (1.0 - ADAM_B2 ** ADAM_STEP)
    delta = -ADAM_LR * (m_hat / (_jnp.sqrt(v_hat) + ADAM_EPS) + ADAM_WD * w)
    return delta, m, v


def reference(x, c, w_ada, b_ada, w_in, b_in, conv_a, w_a_out, ln_v_g, ln_v_b, w_spatial, b_spatial, w_b_out, w_pool, pool_scale, w_o, ln1_g, ln1_b, w_up, b_up, conv_ffn, conv_ffn_b, w_down, ln2_g, ln2_b, loss_target, m_w_ada, m_b_ada, m_w_in, m_b_in, m_conv_a, m_w_a_out, m_ln_v_g, m_ln_v_b, m_w_spatial, m_b_spatial, m_w_b_out, m_w_pool, m_pool_scale, m_w_o, m_ln1_g, m_ln1_b, m_w_up, m_b_up, m_conv_ffn, m_conv_ffn_b, m_w_down, m_ln2_g, m_ln2_b, v_w_ada, v_b_ada, v_w_in, v_b_in, v_conv_a, v_w_a_out, v_ln_v_g, v_ln_v_b, v_w_spatial, v_b_spatial, v_w_b_out, v_w_pool, v_pool_scale, v_w_o, v_ln1_g, v_ln1_b, v_w_up, v_b_up, v_conv_ffn, v_conv_ffn_b, v_w_down, v_ln2_g, v_ln2_b):
    given = dict(x=x, c=c, w_ada=w_ada, b_ada=b_ada, w_in=w_in, b_in=b_in, conv_a=conv_a, w_a_out=w_a_out, ln_v_g=ln_v_g, ln_v_b=ln_v_b, w_spatial=w_spatial, b_spatial=b_spatial, w_b_out=w_b_out, w_pool=w_pool, pool_scale=pool_scale, w_o=w_o, ln1_g=ln1_g, ln1_b=ln1_b, w_up=w_up, b_up=b_up, conv_ffn=conv_ffn, conv_ffn_b=conv_ffn_b, w_down=w_down, ln2_g=ln2_g, ln2_b=ln2_b, loss_target=loss_target, m_w_ada=m_w_ada, m_b_ada=m_b_ada, m_w_in=m_w_in, m_b_in=m_b_in, m_conv_a=m_conv_a, m_w_a_out=m_w_a_out, m_ln_v_g=m_ln_v_g, m_ln_v_b=m_ln_v_b, m_w_spatial=m_w_spatial, m_b_spatial=m_b_spatial, m_w_b_out=m_w_b_out, m_w_pool=m_w_pool, m_pool_scale=m_pool_scale, m_w_o=m_w_o, m_ln1_g=m_ln1_g, m_ln1_b=m_ln1_b, m_w_up=m_w_up, m_b_up=m_b_up, m_conv_ffn=m_conv_ffn, m_conv_ffn_b=m_conv_ffn_b, m_w_down=m_w_down, m_ln2_g=m_ln2_g, m_ln2_b=m_ln2_b, v_w_ada=v_w_ada, v_b_ada=v_b_ada, v_w_in=v_w_in, v_b_in=v_b_in, v_conv_a=v_conv_a, v_w_a_out=v_w_a_out, v_ln_v_g=v_ln_v_g, v_ln_v_b=v_ln_v_b, v_w_spatial=v_w_spatial, v_b_spatial=v_b_spatial, v_w_b_out=v_w_b_out, v_w_pool=v_w_pool, v_pool_scale=v_pool_scale, v_w_o=v_w_o, v_ln1_g=v_ln1_g, v_ln1_b=v_ln1_b, v_w_up=v_w_up, v_b_up=v_b_up, v_conv_ffn=v_conv_ffn, v_conv_ffn_b=v_conv_ffn_b, v_w_down=v_w_down, v_ln2_g=v_ln2_g, v_ln2_b=v_ln2_b)
    weights = {n: given[n] for n in TWIN_WEIGHTS}
    shared = {n: given[n] for n in SHARED_INPUTS}
    per_example = {n: given[n] for n in ['x', 'c']}
    grad_fn = _jax.value_and_grad(_loss, argnums=(0, 1))

    def one_microbatch(ex, loss_target):
        ex = dict(ex)
        diff = ex.pop(TWIN_DIFF_INPUT)
        return grad_fn(weights, diff, {**shared, **ex}, loss_target)

    if N_MICROBATCH == 1:
        loss, (grad_w, grad_x) = one_microbatch(per_example, given["loss_target"])
    else:
        def body(carry, xs):
            loss_sum, grad_sum = carry
            l_k, (gw_k, gx_k) = one_microbatch(xs[0], xs[1])
            with _jax.named_scope("update"):
                return (loss_sum + l_k, _jax.tree.map(_jnp.add, grad_sum, gw_k)), gx_k

        init = (_jnp.zeros((), _jnp.float32), _jax.tree.map(_jnp.zeros_like, weights))
        (loss, grad_w), grad_x = _jax.lax.scan(body, init, (per_example, given["loss_target"]))
    with _jax.named_scope("update"):
        delta_w, new_m, new_v = {}, {}, {}
        for n in TWIN_WEIGHTS:
            delta_w[n], new_m[n], new_v[n] = _adamw(weights[n], grad_w[n], given["m_" + n], given["v_" + n])
    return (loss, grad_x, *[grad_w[n] for n in TWIN_WEIGHTS], *[delta_w[n] for n in TWIN_WEIGHTS],
            *[new_m[n] for n in TWIN_WEIGHTS], *[new_v[n] for n in TWIN_WEIGHTS])
```

```python
import functools
import math

import jax
import jax.numpy as jnp
from jax import lax
from jax.experimental import pallas as pl
from jax.experimental.pallas import tpu as pltpu

F32 = jnp.float32
BF16 = jnp.bfloat16

D_MODEL = 1024
D_Z = 9216
D_FF = 2816
N_GROUPS = 8
GBLK = 128
CHUNK = 64
POOL_WINDOWS = (2, 4, 8, 16)
POOL_GROUP = 256
POOL_HALO = 16
CONV_HALO = 8
DEPTH = 2
ALPHA = (2 * DEPTH) ** 0.25
LN_EPS = 1e-5
ADAM_LR, ADAM_B1, ADAM_B2, ADAM_EPS, ADAM_WD, ADAM_STEP = 0.001, 0.9, 0.999, 1e-08, 0.01, 10
N_CHIPS = 4
N_DEV = 8
FF_CHUNK = 704
MESH = pl.DeviceIdType.MESH
VMEM_LIMIT = 56 * 1024 * 1024


def _dot(a, b):
    return jnp.dot(a, b, preferred_element_type=F32)


def _dot_nt(a, b):
    return lax.dot_general(a, b, (((1,), (1,)), ((), ())), preferred_element_type=F32)


def _dot_tn(a, b):
    return lax.dot_general(a, b, (((0,), (0,)), ((), ())), preferred_element_type=F32)


_GELU_C = math.sqrt(2.0 / math.pi)


def _gelu_and_grad(x):
    x2 = x * x
    t = jnp.tanh(_GELU_C * (x + 0.044715 * x * x2))
    g = 0.5 * x * (1.0 + t)
    dg = 0.5 * (1.0 + t) + 0.5 * x * (1.0 - t * t) * (_GELU_C * (1.0 + 3 * 0.044715 * x2))
    return g, dg


def _gelu(x):
    return 0.5 * x * (1.0 + jnp.tanh(_GELU_C * (x + 0.044715 * x * x * x)))


def _sigmoid(x):
    return 1.0 / (1.0 + jnp.exp(-x))


def _ln_fwd(r):
    mu = jnp.mean(r, axis=-1, keepdims=True)
    xc = r - mu
    var = jnp.mean(xc * xc, axis=-1, keepdims=True)
    rstd = lax.rsqrt(var + LN_EPS)
    return xc * rstd, rstd


def _ln_bwd(dy, g, xhat, rstd):
    dxh = dy * g
    m1 = jnp.mean(dxh, axis=-1, keepdims=True)
    m2 = jnp.mean(dxh * xhat, axis=-1, keepdims=True)
    return rstd * (dxh - m1 - xhat * m2)


def _rows_before(ext, k, halo):
    return pltpu.roll(ext, k, 0)[halo:]


def _rows_after(ext, k, n):
    return pltpu.roll(ext, ext.shape[0] - k, 0)[:n]


def _colsum(v):
    return jnp.sum(v, axis=0, keepdims=True)


def _spatial_mask():
    i = lax.broadcasted_iota(jnp.int32, (GBLK, GBLK), 0)
    j = lax.broadcasted_iota(jnp.int32, (GBLK, GBLK), 1)
    return (j // CHUNK) <= (i // CHUNK)


def _params(sem):
    return pltpu.CompilerParams(dimension_semantics=sem, vmem_limit_bytes=VMEM_LIMIT)


def _const(shape):
    n = len(shape)
    return pl.BlockSpec(shape, lambda *_: (0,) * n)


def _resident(shape):
    n = len(shape)
    return pl.BlockSpec(shape, lambda *_: (0,) * n, pipeline_mode=pl.Buffered(1))


def _tile(ts, s):
    return min(ts, s)


def _mod_matmul(x, sc, sh, w, b, *, ts, tn, name):
    s, d = x.shape
    n = w.shape[1]
    ts = _tile(ts, s)

    def body(x_ref, sc_ref, sh_ref, w_ref, b_ref, o_ref, h_scr):
        @pl.when(pl.program_id(1) == 0)
        def _():
            h_scr[...] = (x_ref[...] * (1.0 + sc_ref[...]) + sh_ref[...]).astype(BF16)
        o_ref[...] = _dot(h_scr[...], w_ref[...]) + b_ref[...]

    return pl.pallas_call(
        body, name=name,
        grid=(s // ts, n // tn),
        in_specs=[pl.BlockSpec((ts, d), lambda i, j: (i, 0)), _const((1, d)), _const((1, d)),
                  pl.BlockSpec((d, tn), lambda i, j: (0, j)), pl.BlockSpec((1, tn), lambda i, j: (0, j))],
        out_specs=pl.BlockSpec((ts, tn), lambda i, j: (i, j)),
        out_shape=jax.ShapeDtypeStruct((s, n), F32),
        scratch_shapes=[pltpu.VMEM((ts, d), BF16)],
        compiler_params=_params(("arbitrary", "arbitrary")),
    )(x, sc, sh, w, b)


def _conv3(q, ext, cw):
    return cw[2:3] * q + cw[1:2] * _rows_before(ext, 1, CONV_HALO) + cw[0:1] * _rows_before(ext, 2, CONV_HALO)


def _mix_a_fwd(z, cw, w_out, *, ts, name):
    s = z.shape[0]
    d = D_MODEL
    ts = _tile(ts, s)

    def body(zb_ref, zc_ref, zx_ref, cw_ref, w_ref, a_ref, y_ref, carry):
        @pl.when(pl.program_id(0) == 0)
        def _():
            carry[...] = jnp.zeros_like(carry)
        q = zc_ref[...] * zx_ref[...]
        ext = jnp.concatenate([carry[...], q], axis=0)
        a = (zb_ref[...] * _conv3(q, ext, cw_ref[...])).astype(BF16)
        carry[...] = q[ts - CONV_HALO:]
        a_ref[...] = a
        y_ref[...] = _dot(a, w_ref[...])

    zspec = lambda k: pl.BlockSpec((ts, d), lambda i, k=k: (i, k))
    return pl.pallas_call(
        body, name=name, grid=(s // ts,),
        in_specs=[zspec(0), zspec(1), zspec(2), _const((3, d)), _const((d, d))],
        out_specs=[pl.BlockSpec((ts, d), lambda i: (i, 0))] * 2,
        out_shape=[jax.ShapeDtypeStruct((s, d), BF16), jax.ShapeDtypeStruct((s, d), F32)],
        scratch_shapes=[pltpu.VMEM((CONV_HALO, d), F32)],
        compiler_params=_params(("arbitrary",)),
    )(z, z, z, cw, w_out)


def _spatial_mix(vn_b, ws_ref, bst_ref, mixed_scr, ts):
    nblk = ts // GBLK
    mask = _spatial_mask()
    for g in range(N_GROUPS):
        cols = slice(g * GBLK, (g + 1) * GBLK)
        wm = jnp.where(mask, ws_ref[g], 0.0).astype(BF16)
        cat = jnp.concatenate([vn_b[n * GBLK:(n + 1) * GBLK, cols] for n in range(nblk)], axis=1)
        res = _dot(wm, cat) + bst_ref[:, g:g + 1]
        for n in range(nblk):
            mixed_scr[n * GBLK:(n + 1) * GBLK, cols] = res[:, n * GBLK:(n + 1) * GBLK]


def _mix_b_fwd(z, ln_g, ln_b, ws, bst, w_out, *, ts, name):
    s = z.shape[0]
    d = D_MODEL
    ts = _tile(ts, s)

    def body(zu_ref, zv_ref, g_ref, b_ref, ws_ref, bst_ref, w_ref, sg_ref, y_ref, mixed_scr):
        xhat, _ = _ln_fwd(_gelu(zv_ref[...]))
        vn = (xhat * g_ref[...] + b_ref[...]).astype(BF16)
        _spatial_mix(vn, ws_ref, bst_ref, mixed_scr, ts)
        sg = (_gelu(zu_ref[...]) * mixed_scr[...]).astype(BF16)
        sg_ref[...] = sg
        y_ref[...] = _dot(sg, w_ref[...])

    zspec = lambda k: pl.BlockSpec((ts, d), lambda i, k=k: (i, k))
    return pl.pallas_call(
        body, name=name, grid=(s // ts,),
        in_specs=[zspec(3), zspec(4), _const((1, d)), _const((1, d)), _const((N_GROUPS, GBLK, GBLK)),
                  _const((GBLK, N_GROUPS)), _const((d, d))],
        out_specs=[pl.BlockSpec((ts, d), lambda i: (i, 0))] * 2,
        out_shape=[jax.ShapeDtypeStruct((s, d), BF16), jax.ShapeDtypeStruct((s, d), F32)],
        scratch_shapes=[pltpu.VMEM((ts, d), F32)],
        compiler_params=_params(("arbitrary",)),
    )(z, z, ln_g, ln_b, ws, bst, w_out)


def _pool_denoms(tile_idx, ts):
    t1 = (tile_idx * ts + 1 + lax.broadcasted_iota(jnp.int32, (ts, 1), 0)).astype(F32)
    return [jnp.minimum(t1, float(w)) for w in POOL_WINDOWS]


def _pool_diff(p, ext, denoms, k):
    cols = slice(k * POOL_GROUP, (k + 1) * POOL_GROUP)
    acc = ext[:, cols]
    step = 1
    while step < POOL_WINDOWS[k]:
        acc = acc + pltpu.roll(acc, step, 0)
        step *= 2
    return acc[POOL_HALO:] / denoms[k] - p[:, cols]


def _mix_c_fwd(z, w_pool, scale, *, ts, name):
    s = z.shape[0]
    d = D_MODEL
    ts = _tile(ts, s)

    def body(zp_ref, w_ref, sc_ref, d_ref, y_ref, carry):
        i = pl.program_id(0)

        @pl.when(i == 0)
        def _():
            carry[...] = jnp.zeros_like(carry)
        p = zp_ref[...]
        ext = jnp.concatenate([carry[...], p], axis=0)
        carry[...] = p[ts - POOL_HALO:]
        denoms = _pool_denoms(i, ts)
        for k in range(len(POOL_WINDOWS)):
            cols = slice(k * POOL_GROUP, (k + 1) * POOL_GROUP)
            dk = _pool_diff(p, ext, denoms, k).astype(BF16)
            d_ref[:, cols] = dk
            y_ref[:, cols] = _dot(dk, w_ref[k]) * sc_ref[:, cols]

    return pl.pallas_call(
        body, name=name, grid=(s // ts,),
        in_specs=[pl.BlockSpec((ts, d), lambda i: (i, 5)), _const((4, POOL_GROUP, POOL_GROUP)), _const((1, d))],
        out_specs=[pl.BlockSpec((ts, d), lambda i: (i, 0))] * 2,
        out_shape=[jax.ShapeDtypeStruct((s, d), BF16), jax.ShapeDtypeStruct((s, d), F32)],
        scratch_shapes=[pltpu.VMEM((POOL_HALO, d), F32)],
        compiler_params=_params(("arbitrary",)),
    )(z, w_pool, scale)


def _mix_o_fwd(x, z, ya, yb, yc, w_o, gt, ln_g, ln_b, *, ts, name):
    s, d = x.shape
    ts = _tile(ts, s)

    def body(x_ref, ga_ref, gb_ref, gc_ref, ya_ref, yb_ref, yc_ref, w_ref, gt_ref, g_ref, b_ref,
             m_ref, o_ref, x1_ref):
        merged = (_sigmoid(ga_ref[...]) * ya_ref[...] + _sigmoid(gb_ref[...]) * yb_ref[...]
                  + _sigmoid(gc_ref[...]) * yc_ref[...]).astype(BF16)
        m_ref[...] = merged
        o = _dot(merged, w_ref[...])
        o_ref[...] = o
        xhat, _ = _ln_fwd(ALPHA * x_ref[...] + gt_ref[...] * o)
        x1_ref[...] = xhat * g_ref[...] + b_ref[...]

    row = pl.BlockSpec((ts, d), lambda i: (i, 0))
    zspec = lambda k: pl.BlockSpec((ts, d), lambda i, k=k: (i, k))
    return pl.pallas_call(
        body, name=name, grid=(s // ts,),
        in_specs=[row, zspec(6), zspec(7), zspec(8), row, row, row, _const((d, d)),
                  _const((1, d)), _const((1, d)), _const((1, d))],
        out_specs=[row] * 3,
        out_shape=[jax.ShapeDtypeStruct((s, d), BF16), jax.ShapeDtypeStruct((s, d), F32),
                   jax.ShapeDtypeStruct((s, d), F32)],
        compiler_params=_params(("arbitrary",)),
    )(x, z, z, z, ya, yb, yc, w_o, gt, ln_g, ln_b)


def _ffn_fwd(up, x1, cw, cb, w_down, gt, ln_g, ln_b, *, ts, name):
    s, d = x1.shape
    ts = _tile(ts, s)

    def body(up_ref, x1_ref, cw_ref, cb_ref, w_ref, gt_ref, g_ref, b_ref, f_ref, dn_ref, x2_ref, carry):
        @pl.when(pl.program_id(0) == 0)
        def _():
            carry[...] = jnp.zeros_like(carry)
        for c in range(D_FF // FF_CHUNK):
            ca = slice(c * FF_CHUNK, (c + 1) * FF_CHUNK)
            cg = slice(D_FF + c * FF_CHUNK, D_FF + (c + 1) * FF_CHUNK)
            ua = up_ref[:, ca]
            ext = jnp.concatenate([carry[:, ca], ua], axis=0)
            carry[:, ca] = ua[ts - CONV_HALO:]
            cf = _conv3(ua, ext, cw_ref[:, ca]) + cb_ref[:, ca]
            f_ref[:, ca] = (_gelu(cf) * up_ref[:, cg]).astype(BF16)
        dn = _dot(f_ref[...], w_ref[...])
        dn_ref[...] = dn
        xhat, _ = _ln_fwd(ALPHA * x1_ref[...] + gt_ref[...] * dn)
        x2_ref[...] = xhat * g_ref[...] + b_ref[...]

    row = pl.BlockSpec((ts, d), lambda i: (i, 0))
    return pl.pallas_call(
        body, name=name, grid=(s // ts,),
        in_specs=[pl.BlockSpec((ts, 2 * D_FF), lambda i: (i, 0)), row, _const((3, D_FF)), _const((1, D_FF)),
                  _resident((D_FF, d)), _const((1, d)), _const((1, d)), _const((1, d))],
        out_specs=[pl.BlockSpec((ts, D_FF), lambda i: (i, 0)), row, row],
        out_shape=[jax.ShapeDtypeStruct((s, D_FF), BF16), jax.ShapeDtypeStruct((s, d), F32),
                   jax.ShapeDtypeStruct((s, d), F32)],
        scratch_shapes=[pltpu.VMEM((CONV_HALO, D_FF), F32)],
        compiler_params=_params(("arbitrary",)),
    )(up, x1, cw, cb, w_down, gt, ln_g, ln_b)


def _loss_fwd(y, tgt, *, ts, name):
    s, d = y.shape
    ts = _tile(ts, s)

    def body(y_ref, t_ref, dy_ref, l_ref):
        @pl.when(pl.program_id(0) == 0)
        def _():
            l_ref[...] = jnp.zeros_like(l_ref)
        e = y_ref[...] - t_ref[...]
        dy_ref[...] = e / float(d)
        l_ref[...] += 0.5 * jnp.sum(jnp.mean(e * e, axis=-1, keepdims=True), axis=0, keepdims=True)

    row = pl.BlockSpec((ts, d), lambda i: (i, 0))
    return pl.pallas_call(
        body, name=name, grid=(s // ts,),
        in_specs=[row, row], out_specs=[row, _const((8, 128))],
        out_shape=[jax.ShapeDtypeStruct((s, d), F32), jax.ShapeDtypeStruct((8, 128), F32)],
        compiler_params=_params(("arbitrary",)),
    )(y, tgt)


def _rev(n_tiles):
    return lambda i: n_tiles - 1 - i


def _halo_spec(ts, n_tiles, halo, width, col):
    per = ts // halo
    return pl.BlockSpec((halo, width), lambda i: (jnp.maximum((n_tiles - 1 - i) * per - 1, 0), col))


def _ffn_bwd(dx2, x1, dn, up, cw, cb, w_down, w_up, gt, ln_g, sc, *, ts, name):
    s, d = x1.shape
    ts = _tile(ts, s)
    nt = s // ts
    rev = _rev(nt)

    def body(dx2_ref, x1_ref, dn_ref, up_ref, halo_ref, cw_ref, cb_ref, wd_ref, wu_ref, gt_ref, g_ref, sc_ref,
             ddn_ref, dup_ref, dx1_ref, redd_ref, redf_ref, dbup_ref, carry):
        i = pl.program_id(0)

        @pl.when(i == 0)
        def _():
            carry[...] = jnp.zeros_like(carry)
            redd_ref[...] = jnp.zeros_like(redd_ref)
            redf_ref[...] = jnp.zeros_like(redf_ref)
            dbup_ref[...] = jnp.zeros_like(dbup_ref)
        first_tile = i == nt - 1
        x1v, dnv, dyv = x1_ref[...], dn_ref[...], dx2_ref[...]
        xhat, rstd = _ln_fwd(ALPHA * x1v + gt_ref[...] * dnv)
        dr = _ln_bwd(dyv, g_ref[...], xhat, rstd)
        redd_ref[0:1, :] += _colsum(dyv * xhat)
        redd_ref[1:2, :] += _colsum(dyv)
        redd_ref[2:3, :] += _colsum(dr * dnv)
        ddn = (gt_ref[...] * dr).astype(BF16)
        ddn_ref[...] = ddn
        dh = jnp.zeros((ts, d), F32)
        for c in range(D_FF // FF_CHUNK):
            ca = slice(c * FF_CHUNK, (c + 1) * FF_CHUNK)
            cg = slice(D_FF + c * FF_CHUNK, D_FF + (c + 1) * FF_CHUNK)
            df = _dot_nt(ddn, wd_ref[ca, :])
            ua, ug = up_ref[:, ca], up_ref[:, cg]
            halo = jnp.where(first_tile, 0.0, halo_ref[:, ca])
            ext = jnp.concatenate([halo, ua], axis=0)
            u1, u2 = _rows_before(ext, 1, CONV_HALO), _rows_before(ext, 2, CONV_HALO)
            cwc = cw_ref[:, ca]
            gl, dgl = _gelu_and_grad(cwc[2:3] * ua + cwc[1:2] * u1 + cwc[0:1] * u2 + cb_ref[:, ca])
            dug = df * gl
            dcf = df * ug * dgl
            redf_ref[0:1, ca] += _colsum(dcf * u2)
            redf_ref[1:2, ca] += _colsum(dcf * u1)
            redf_ref[2:3, ca] += _colsum(dcf * ua)
            redf_ref[3:4, ca] += _colsum(dcf)
            extd = jnp.concatenate([dcf, carry[:, ca]], axis=0)
            carry[:, ca] = dcf[:CONV_HALO]
            dua = cwc[2:3] * dcf + cwc[1:2] * _rows_after(extd, 1, ts) + cwc[0:1] * _rows_after(extd, 2, ts)
            dbup_ref[0:1, ca] += _colsum(dua)
            dbup_ref[0:1, cg] += _colsum(dug)
            dua_b, dug_b = dua.astype(BF16), dug.astype(BF16)
            dup_ref[:, ca] = dua_b
            dup_ref[:, cg] = dug_b
            dh = dh + _dot_nt(dua_b, wu_ref[:, ca]) + _dot_nt(dug_b, wu_ref[:, cg])
        dx1_ref[...] = ALPHA * dr + dh * (1.0 + sc_ref[...])
        redd_ref[3:4, :] += _colsum(dh * x1v)
        redd_ref[4:5, :] += _colsum(dh)

    row = pl.BlockSpec((ts, d), lambda i: (rev(i), 0))
    return pl.pallas_call(
        body, name=name, grid=(nt,),
        in_specs=[row, row, row, pl.BlockSpec((ts, 2 * D_FF), lambda i: (rev(i), 0)),
                  _halo_spec(ts, nt, CONV_HALO, D_FF, 0), _const((3, D_FF)), _const((1, D_FF)),
                  _resident((D_FF, d)), _resident((d, 2 * D_FF)), _const((1, d)), _const((1, d)), _const((1, d))],
        out_specs=[row, pl.BlockSpec((ts, 2 * D_FF), lambda i: (rev(i), 0)), row,
                   _const((8, d)), _const((8, D_FF)), _const((8, 2 * D_FF))],
        out_shape=[jax.ShapeDtypeStruct((s, d), BF16), jax.ShapeDtypeStruct((s, 2 * D_FF), BF16),
                   jax.ShapeDtypeStruct((s, d), F32), jax.ShapeDtypeStruct((8, d), F32),
                   jax.ShapeDtypeStruct((8, D_FF), F32), jax.ShapeDtypeStruct((8, 2 * D_FF), F32)],
        scratch_shapes=[pltpu.VMEM((CONV_HALO, D_FF), F32)],
        compiler_params=_params(("arbitrary",)),
    )(dx2, x1, dn, up, up, cw, cb, w_down, w_up, gt, ln_g, sc)


def _grad_matmul(xa, dy, *, ts, tn, name, mod=None):
    s, k = xa.shape
    n = dy.shape[1]
    ts = _tile(ts, s)

    def body(*refs):
        if mod is None:
            xa_ref, dy_ref, o_ref = refs
            a = xa_ref[...]
        else:
            xa_ref, sc_ref, sh_ref, dy_ref, o_ref = refs
            a = (xa_ref[...] * (1.0 + sc_ref[...]) + sh_ref[...]).astype(BF16)

        @pl.when(pl.program_id(1) == 0)
        def _():
            o_ref[...] = jnp.zeros_like(o_ref)
        o_ref[...] += _dot_tn(a, dy_ref[...])

    xspec = pl.BlockSpec((ts, k), lambda j, t: (t, 0))
    dspec = pl.BlockSpec((ts, tn), lambda j, t: (t, j))
    in_specs = [xspec, dspec] if mod is None else [xspec, _const((1, k)), _const((1, k)), dspec]
    args = (xa, dy) if mod is None else (xa, mod[0], mod[1], dy)
    return pl.pallas_call(
        body, name=name, grid=(n // tn, s // ts),
        in_specs=in_specs, out_specs=pl.BlockSpec((k, tn), lambda j, t: (0, j)),
        out_shape=jax.ShapeDtypeStruct((k, n), F32),
        compiler_params=_params(("arbitrary", "arbitrary")),
    )(*args)


def _mix_o_bwd(dx1, x, o, z, ya, yb, yc, w_o, gt, ln_g, *, ts, name):
    s, d = x.shape
    ts = _tile(ts, s)

    def body(dx1_ref, x_ref, o_ref, ga_ref, gb_ref, gc_ref, ya_ref, yb_ref, yc_ref, w_ref, gt_ref, g_ref,
             do_ref, dxa_ref, dzg_ref, dya_ref, dyb_ref, dyc_ref, red_ref):
        @pl.when(pl.program_id(0) == 0)
        def _():
            red_ref[...] = jnp.zeros_like(red_ref)
        dyv, ov = dx1_ref[...], o_ref[...]
        xhat, rstd = _ln_fwd(ALPHA * x_ref[...] + gt_ref[...] * ov)
        dr = _ln_bwd(dyv, g_ref[...], xhat, rstd)
        red_ref[0:1, :] += _colsum(dyv * xhat)
        red_ref[1:2, :] += _colsum(dyv)
        red_ref[2:3, :] += _colsum(dr * ov)
        dxa_ref[...] = ALPHA * dr
        d_o = (gt_ref[...] * dr).astype(BF16)
        do_ref[...] = d_o
        dm = _dot_nt(d_o, w_ref[...])
        for k, (zg_ref, y_ref, dy_ref) in enumerate(((ga_ref, ya_ref, dya_ref), (gb_ref, yb_ref, dyb_ref),
                                                     (gc_ref, yc_ref, dyc_ref))):
            g = _sigmoid(zg_ref[...])
            dzg_ref[:, k * d:(k + 1) * d] = (dm * y_ref[...] * g * (1.0 - g)).astype(BF16)
            dy_ref[...] = (dm * g).astype(BF16)

    row = pl.BlockSpec((ts, d), lambda i: (i, 0))
    zspec = lambda k: pl.BlockSpec((ts, d), lambda i, k=k: (i, k))
    bf = jax.ShapeDtypeStruct((s, d), BF16)
    return pl.pallas_call(
        body, name=name, grid=(s // ts,),
        in_specs=[row, row, row, zspec(6), zspec(7), zspec(8), row, row, row, _const((d, d)),
                  _const((1, d)), _const((1, d))],
        out_specs=[row, row, pl.BlockSpec((ts, 3 * d), lambda i: (i, 0)), row, row, row, _const((8, d))],
        out_shape=[bf, jax.ShapeDtypeStruct((s, d), F32), jax.ShapeDtypeStruct((s, 3 * d), BF16), bf, bf, bf,
                   jax.ShapeDtypeStruct((8, d), F32)],
        compiler_params=_params(("arbitrary",)),
    )(dx1, x, o, z, z, z, ya, yb, yc, w_o, gt, ln_g)


def _mix_a_bwd(dya, z, cw, w_out, *, ts, name):
    s = z.shape[0]
    d = D_MODEL
    ts = _tile(ts, s)
    nt = s // ts
    rev = _rev(nt)

    def body(dya_ref, zb_ref, zc_ref, zx_ref, hc_ref, hx_ref, cw_ref, w_ref, dz_ref, red_ref, carry):
        i = pl.program_id(0)

        @pl.when(i == 0)
        def _():
            carry[...] = jnp.zeros_like(carry)
            red_ref[...] = jnp.zeros_like(red_ref)
        zb, zc, zx = zb_ref[...], zc_ref[...], zx_ref[...]
        q = zc * zx
        halo = jnp.where(i == nt - 1, 0.0, hc_ref[...] * hx_ref[...])
        ext = jnp.concatenate([halo, q], axis=0)
        q1, q2 = _rows_before(ext, 1, CONV_HALO), _rows_before(ext, 2, CONV_HALO)
        cwv = cw_ref[...]
        cv = cwv[2:3] * q + cwv[1:2] * q1 + cwv[0:1] * q2
        da = _dot_nt(dya_ref[...], w_ref[...])
        dcv = da * zb
        red_ref[0:1, :] += _colsum(dcv * q2)
        red_ref[1:2, :] += _colsum(dcv * q1)
        red_ref[2:3, :] += _colsum(dcv * q)
        extd = jnp.concatenate([dcv, carry[...]], axis=0)
        carry[...] = dcv[:CONV_HALO]
        dq = cwv[2:3] * dcv + cwv[1:2] * _rows_after(extd, 1, ts) + cwv[0:1] * _rows_after(extd, 2, ts)
        dz_ref[:, 0:d] = (da * cv).astype(BF16)
        dz_ref[:, d:2 * d] = (dq * zx).astype(BF16)
        dz_ref[:, 2 * d:3 * d] = (dq * zc).astype(BF16)

    zspec = lambda k: pl.BlockSpec((ts, d), lambda i, k=k: (rev(i), k))
    return pl.pallas_call(
        body, name=name, grid=(nt,),
        in_specs=[pl.BlockSpec((ts, d), lambda i: (rev(i), 0)), zspec(0), zspec(1), zspec(2),
                  _halo_spec(ts, nt, CONV_HALO, d, 1), _halo_spec(ts, nt, CONV_HALO, d, 2),
                  _const((3, d)), _const((d, d))],
        out_specs=[pl.BlockSpec((ts, 3 * d), lambda i: (rev(i), 0)), _const((8, d))],
        out_shape=[jax.ShapeDtypeStruct((s, 3 * d), BF16), jax.ShapeDtypeStruct((8, d), F32)],
        scratch_shapes=[pltpu.VMEM((CONV_HALO, d), F32)],
        compiler_params=_params(("arbitrary",)),
    )(dya, z, z, z, z, z, cw, w_out)


def _mix_b_bwd(dyb, z, ln_g, ln_b, ws, bst, w_out, *, ts, name):
    s = z.shape[0]
    d = D_MODEL
    ts = _tile(ts, s)
    nblk = ts // GBLK

    def body(dyb_ref, zu_ref, zv_ref, g_ref, b_ref, ws_ref, bst_ref, w_ref,
             dz_ref, red_ref, dws_ref, dbst_ref, mixed_scr, dvn_scr):
        @pl.when(pl.program_id(0) == 0)
        def _():
            red_ref[...] = jnp.zeros_like(red_ref)
            dws_ref[...] = jnp.zeros_like(dws_ref)
            dbst_ref[...] = jnp.zeros_like(dbst_ref)
        u, du_dz = _gelu_and_grad(zu_ref[...])
        vg, dv_dz = _gelu_and_grad(zv_ref[...])
        xhat, rstd = _ln_fwd(vg)
        vn = (xhat * g_ref[...] + b_ref[...]).astype(BF16)
        _spatial_mix(vn, ws_ref, bst_ref, mixed_scr, ts)
        dsg = _dot_nt(dyb_ref[...], w_ref[...])
        dz_ref[:, 0:d] = (dsg * mixed_scr[...] * du_dz).astype(BF16)
        dmix = dsg * u
        mask = _spatial_mask()
        for g in range(N_GROUPS):
            cols = slice(g * GBLK, (g + 1) * GBLK)
            wm = jnp.where(mask, ws_ref[g], 0.0).astype(BF16)
            dm_cat = jnp.concatenate([dmix[n * GBLK:(n + 1) * GBLK, cols] for n in range(nblk)], axis=1)
            vn_cat = jnp.concatenate([vn[n * GBLK:(n + 1) * GBLK, cols] for n in range(nblk)], axis=1)
            dm_b = dm_cat.astype(BF16)
            dbst_ref[:, g:g + 1] += jnp.sum(dm_cat, axis=1, keepdims=True)
            dws_ref[g] += jnp.where(mask, _dot_nt(dm_b, vn_cat), 0.0)
            dvn_cat = _dot_tn(wm, dm_b)
            for n in range(nblk):
                dvn_scr[n * GBLK:(n + 1) * GBLK, cols] = dvn_cat[:, n * GBLK:(n + 1) * GBLK]
        dvn = dvn_scr[...]
        red_ref[0:1, :] += _colsum(dvn * xhat)
        red_ref[1:2, :] += _colsum(dvn)
        dz_ref[:, d:2 * d] = (_ln_bwd(dvn, g_ref[...], xhat, rstd) * dv_dz).astype(BF16)

    zspec = lambda k: pl.BlockSpec((ts, d), lambda i, k=k: (i, k))
    return pl.pallas_call(
        body, name=name, grid=(s // ts,),
        in_specs=[pl.BlockSpec((ts, d), lambda i: (i, 0)), zspec(3), zspec(4), _const((1, d)), _const((1, d)),
                  _const((N_GROUPS, GBLK, GBLK)), _const((GBLK, N_GROUPS)), _const((d, d))],
        out_specs=[pl.BlockSpec((ts, 2 * d), lambda i: (i, 0)), _const((8, d)),
                   _const((N_GROUPS, GBLK, GBLK)), _const((GBLK, N_GROUPS))],
        out_shape=[jax.ShapeDtypeStruct((s, 2 * d), BF16), jax.ShapeDtypeStruct((8, d), F32),
                   jax.ShapeDtypeStruct((N_GROUPS, GBLK, GBLK), F32), jax.ShapeDtypeStruct((GBLK, N_GROUPS), F32)],
        scratch_shapes=[pltpu.VMEM((ts, d), F32), pltpu.VMEM((ts, d), F32)],
        compiler_params=_params(("arbitrary",)),
    )(dyb, z, z, ln_g, ln_b, ws, bst, w_out)


def _mix_c_bwd(dyc, z, w_pool, scale, *, ts, name):
    s = z.shape[0]
    d = D_MODEL
    ts = _tile(ts, s)
    nt = s // ts
    rev = _rev(nt)

    def body(dyc_ref, zp_ref, halo_ref, w_ref, sc_ref, dz_ref, red_ref, dw_ref, carry):
        i = pl.program_id(0)

        @pl.when(i == 0)
        def _():
            carry[...] = jnp.zeros_like(carry)
            red_ref[...] = jnp.zeros_like(red_ref)
            dw_ref[...] = jnp.zeros_like(dw_ref)
        p = zp_ref[...]
        ext = jnp.concatenate([jnp.where(i == nt - 1, 0.0, halo_ref[...]), p], axis=0)
        denoms = _pool_denoms(rev(i), ts)
        dyv = dyc_ref[...].astype(F32)
        for k in range(len(POOL_WINDOWS)):
            cols = slice(k * POOL_GROUP, (k + 1) * POOL_GROUP)
            dk = _pool_diff(p, ext, denoms, k).astype(BF16)
            red_ref[0:1, cols] += _colsum(dyv[:, cols] * _dot(dk, w_ref[k]))
            dpre = (dyv[:, cols] * sc_ref[:, cols]).astype(BF16)
            dw_ref[k] += _dot_tn(dk, dpre)
            dd = _dot_nt(dpre, w_ref[k])
            e = dd / denoms[k]
            acc = jnp.concatenate([e, carry[:, cols]], axis=0)
            carry[:, cols] = e[:POOL_HALO]
            step = 1
            while step < POOL_WINDOWS[k]:
                acc = acc + pltpu.roll(acc, acc.shape[0] - step, 0)
                step *= 2
            dz_ref[:, cols] = (acc[:ts] - dd).astype(BF16)

    return pl.pallas_call(
        body, name=name, grid=(nt,),
        in_specs=[pl.BlockSpec((ts, d), lambda i: (rev(i), 0)), pl.BlockSpec((ts, d), lambda i: (rev(i), 5)),
                  _halo_spec(ts, nt, POOL_HALO, d, 5), _const((4, POOL_GROUP, POOL_GROUP)), _const((1, d))],
        out_specs=[pl.BlockSpec((ts, d), lambda i: (rev(i), 0)), _const((8, d)), _const((4, POOL_GROUP, POOL_GROUP))],
        out_shape=[jax.ShapeDtypeStruct((s, d), BF16), jax.ShapeDtypeStruct((8, d), F32),
                   jax.ShapeDtypeStruct((4, POOL_GROUP, POOL_GROUP), F32)],
        scratch_shapes=[pltpu.VMEM((POOL_HALO, d), F32)],
        compiler_params=_params(("arbitrary",)),
    )(dyc, z, z, w_pool, scale)


def _in_proj_bwd(dzs, ws, dxa, x, sc, *, ts, name):
    s, d = x.shape
    ts = _tile(ts, s)
    widths = [w.shape[1] for w in ws]
    np_ = len(ws)

    def body(*refs):
        dz_refs, w_refs = refs[:np_], refs[np_:2 * np_]
        dxa_ref, x_ref, sc_ref = refs[2 * np_:2 * np_ + 3]
        dx_ref, red_ref = refs[2 * np_ + 3:2 * np_ + 5]
        db_refs = refs[2 * np_ + 5:]

        @pl.when(pl.program_id(0) == 0)
        def _():
            red_ref[...] = jnp.zeros_like(red_ref)
            for r in db_refs:
                r[...] = jnp.zeros_like(r)
        dh = jnp.zeros((ts, d), F32)
        for dz_ref, w_ref, db_ref in zip(dz_refs, w_refs, db_refs):
            dz = dz_ref[...]
            db_ref[0:1, :] += _colsum(dz.astype(F32))
            dh = dh + _dot_nt(dz, w_ref[...])
        dx_ref[...] = dxa_ref[...] + dh * (1.0 + sc_ref[...])
        red_ref[0:1, :] += _colsum(dh * x_ref[...])
        red_ref[1:2, :] += _colsum(dh)

    row = pl.BlockSpec((ts, d), lambda i: (i, 0))
    return pl.pallas_call(
        body, name=name, grid=(s // ts,),
        in_specs=[pl.BlockSpec((ts, w), lambda i: (i, 0)) for w in widths] + [_resident((d, w)) for w in widths]
        + [row, row, _const((1, d))],
        out_specs=[row, _const((8, d))] + [_const((8, w)) for w in widths],
        out_shape=[jax.ShapeDtypeStruct((s, d), F32), jax.ShapeDtypeStruct((8, d), F32)]
        + [jax.ShapeDtypeStruct((8, w), F32) for w in widths],
        compiler_params=_params(("arbitrary",)),
    )(*dzs, *ws, dxa, x, sc)


def _ada_fwd(c_all, w_ada, b_ada, *, name):
    nl, d, n = w_ada.shape
    tn = n // 2

    def body(c_ref, w_ref, b_ref, o_ref):
        cv = c_ref[...]
        ca = (cv * _sigmoid(cv)).astype(BF16)
        o_ref[0] = _dot(ca, w_ref[0].astype(BF16)) + b_ref[0]

    return pl.pallas_call(
        body, name=name, grid=(nl, n // tn),
        in_specs=[_const((N_DEV, d)), pl.BlockSpec((1, d, tn), lambda l, j: (l, 0, j)),
                  pl.BlockSpec((1, 1, tn), lambda l, j: (l, 0, j))],
        out_specs=pl.BlockSpec((1, N_DEV, tn), lambda l, j: (l, 0, j)),
        out_shape=jax.ShapeDtypeStruct((nl, N_DEV, n), F32),
        compiler_params=_params(("arbitrary", "arbitrary")),
    )(c_all, w_ada, b_ada)


def _ada_bwd(c_all, dada, *, name):
    nl, nb, n = dada.shape
    d = c_all.shape[1]
    tn = n // 2

    def body(c_ref, g_ref, o_ref):
        cv = c_ref[...]
        ca = (cv * _sigmoid(cv)).astype(BF16)
        o_ref[0] = _dot_tn(ca, g_ref[0].astype(BF16))

    return pl.pallas_call(
        body, name=name, grid=(nl, n // tn),
        in_specs=[_const((nb, d)), pl.BlockSpec((1, nb, tn), lambda l, j: (l, 0, j))],
        out_specs=pl.BlockSpec((1, d, tn), lambda l, j: (l, 0, j)),
        out_shape=jax.ShapeDtypeStruct((nl, d, n), F32),
        compiler_params=_params(("arbitrary", "arbitrary")),
    )(c_all, dada)


def _adamw(w, g, m, v, *, name, g2=None, g_scale_rows=2048):
    r, c = w.shape
    tr = r if r <= g_scale_rows else g_scale_rows
    while r % tr:
        tr //= 2
    if c * tr * 4 > (2 << 20):
        tr = max(8, ((2 << 20) // (c * 4)) // 8 * 8)
        while r % tr:
            tr -= 8
    has_g2 = g2 is not None

    def body(*refs):
        if has_g2:
            w_ref, g_ref, g2_ref, m_ref, v_ref, go_ref, d_ref, mo_ref, vo_ref = refs
            gv = g_ref[...] + g2_ref[...]
            go_ref[...] = gv
        else:
            w_ref, g_ref, m_ref, v_ref, d_ref, mo_ref, vo_ref = refs
            gv = g_ref[...]
        mn = ADAM_B1 * m_ref[...] + (1.0 - ADAM_B1) * gv
        vn = ADAM_B2 * v_ref[...] + (1.0 - ADAM_B2) * (gv * gv)
        m_hat = mn / (1.0 - ADAM_B1 ** ADAM_STEP)
        v_hat = vn / (1.0 - ADAM_B2 ** ADAM_STEP)
        d_ref[...] = -ADAM_LR * (m_hat / (jnp.sqrt(v_hat) + ADAM_EPS) + ADAM_WD * w_ref[...])
        mo_ref[...] = mn
        vo_ref[...] = vn

    blk = pl.BlockSpec((tr, c), lambda i: (i, 0))
    n_in = 5 if has_g2 else 4
    n_out = 4 if has_g2 else 3
    args = (w, g, g2, m, v) if has_g2 else (w, g, m, v)
    return pl.pallas_call(
        body, name=name, grid=(r // tr,),
        in_specs=[blk] * n_in, out_specs=[blk] * n_out,
        out_shape=[jax.ShapeDtypeStruct((r, c), F32)] * n_out,
        compiler_params=_params(("arbitrary",)),
    )(*args)


def _my_coords():
    return lax.axis_index("x"), lax.axis_index("y"), lax.axis_index("c")


def _all_gather8(x, *, name, with_sum=False):
    r, lanes = x.shape

    def body(x_ref, out_ref, *rest):
        if with_sum:
            sum_ref, send_sems, recv_sems, local_sem = rest
        else:
            send_sems, recv_sems, local_sem = rest
        mx, my, mc = _my_coords()
        me = 4 * mx + 2 * my + mc

        def peer(k):
            return (mx ^ ((k >> 2) & 1), my ^ ((k >> 1) & 1), mc ^ (k & 1))

        def copy(k, slot):
            return pltpu.make_async_remote_copy(
                src_ref=x_ref, dst_ref=out_ref.at[slot], send_sem=send_sems.at[k - 1], recv_sem=recv_sems.at[k - 1],
                device_id=peer(k), device_id_type=MESH)

        mine = pltpu.make_async_copy(x_ref, out_ref.at[me], local_sem)
        mine.start()
        sends = [copy(k, me) for k in range(1, N_DEV)]
        for cp in sends:
            cp.start()
        for k in range(1, N_DEV):
            copy(k, me ^ k).wait_recv()
        for cp in sends:
            cp.wait_send()
        mine.wait()
        if with_sum:
            acc = out_ref[0]
            for k in range(1, N_DEV):
                acc = acc + out_ref[k]
            sum_ref[...] = acc

    vmem = pl.BlockSpec(memory_space=pltpu.VMEM)
    out_shape = [jax.ShapeDtypeStruct((N_DEV, r, lanes), F32)]
    if with_sum:
        out_shape.append(jax.ShapeDtypeStruct((r, lanes), F32))
    res = pl.pallas_call(
        body, name=name, in_specs=[vmem], out_specs=[vmem] * len(out_shape), out_shape=out_shape,
        scratch_shapes=[pltpu.SemaphoreType.DMA((N_DEV - 1,)), pltpu.SemaphoreType.DMA((N_DEV - 1,)),
                        pltpu.SemaphoreType.DMA],
        compiler_params=pltpu.CompilerParams(vmem_limit_bytes=VMEM_LIMIT),
    )(x)
    return res if with_sum else res[0]


def _chip_peer(k):
    mx, my, mc = _my_coords()
    return (mx ^ ((k >> 1) & 1), my ^ (k & 1), mc)


def _chip_all_gather(x, *, name):
    def body(x_ref, out_ref, send_sems, recv_sems, local_sem):
        mx, my, _ = _my_coords()
        me = 2 * mx + my

        def copy(k, slot):
            return pltpu.make_async_remote_copy(
                src_ref=x_ref, dst_ref=out_ref.at[slot], send_sem=send_sems.at[k - 1], recv_sem=recv_sems.at[k - 1],
                device_id=_chip_peer(k), device_id_type=MESH)

        mine = pltpu.make_async_copy(x_ref, out_ref.at[me], local_sem)
        mine.start()
        sends = [copy(k, me) for k in range(1, N_CHIPS)]
        for cp in sends:
            cp.start()
        for k in range(1, N_CHIPS):
            copy(k, me ^ k).wait_recv()
        for cp in sends:
            cp.wait_send()
        mine.wait()

    hbm = pl.BlockSpec(memory_space=pl.ANY)
    return pl.pallas_call(
        body, name=name, in_specs=[hbm], out_specs=hbm,
        out_shape=jax.ShapeDtypeStruct((N_CHIPS,) + x.shape, x.dtype),
        scratch_shapes=[pltpu.SemaphoreType.DMA((N_CHIPS - 1,)), pltpu.SemaphoreType.DMA((N_CHIPS - 1,)),
                        pltpu.SemaphoreType.DMA],
    )(x)


def _chip_exchange(chunks, *, name):
    _, r, c = chunks.shape

    def body(x_ref, out_ref, send_sems, recv_sems):
        mx, my, _ = _my_coords()
        me = 2 * mx + my

        def copy(k):
            return pltpu.make_async_remote_copy(
                src_ref=x_ref.at[me ^ k], dst_ref=out_ref.at[k - 1], send_sem=send_sems.at[k - 1],
                recv_sem=recv_sems.at[k - 1], device_id=_chip_peer(k), device_id_type=MESH)

        sends = [copy(k) for k in range(1, N_CHIPS)]
        for cp in sends:
            cp.start()
        for cp in sends:
            cp.wait_recv()
        for cp in sends:
            cp.wait_send()

    hbm = pl.BlockSpec(memory_space=pl.ANY)
    return pl.pallas_call(
        body, name=name, in_specs=[hbm], out_specs=hbm,
        out_shape=jax.ShapeDtypeStruct((N_CHIPS - 1, r, c), chunks.dtype),
        scratch_shapes=[pltpu.SemaphoreType.DMA((N_CHIPS - 1,)), pltpu.SemaphoreType.DMA((N_CHIPS - 1,))],
    )(chunks)


def _sibling_swap(x, *, name):
    def body(x_ref, out_ref, send_sem, recv_sem):
        mx, my, mc = _my_coords()
        cp = pltpu.make_async_remote_copy(src_ref=x_ref, dst_ref=out_ref, send_sem=send_sem, recv_sem=recv_sem,
                                          device_id=(mx, my, 1 - mc), device_id_type=MESH)
        cp.start()
        cp.wait_recv()
        cp.wait_send()

    hbm = pl.BlockSpec(memory_space=pl.ANY)
    return pl.pallas_call(
        body, name=name, in_specs=[hbm], out_specs=hbm, out_shape=jax.ShapeDtypeStruct(x.shape, x.dtype),
        scratch_shapes=[pltpu.SemaphoreType.DMA, pltpu.SemaphoreType.DMA],
    )(x)


def _sum4(own, recv, *, name):
    r, c = own.shape
    tr = 512
    while r % tr:
        tr //= 2

    def body(own_ref, recv_ref, o_ref):
        acc = own_ref[...]
        for k in range(N_CHIPS - 1):
            acc = acc + recv_ref[k].astype(F32)
        o_ref[...] = acc

    return pl.pallas_call(
        body, name=name, grid=(r // tr,),
        in_specs=[pl.BlockSpec((tr, c), lambda i: (i, 0)), pl.BlockSpec((N_CHIPS - 1, tr, c), lambda i: (0, i, 0))],
        out_specs=pl.BlockSpec((tr, c), lambda i: (i, 0)), out_shape=jax.ShapeDtypeStruct((r, c), F32),
        compiler_params=_params(("arbitrary",)),
    )(own, recv)


def _layer_fwd(x, ada, p, l):
    sh1, sc1, gt1, sh2, sc2, gt2 = ada
    n = f"l{l}"
    z = _mod_matmul(x, sc1, sh1, p["w_in"], p["b_in"], ts=512, tn=1024, name=f"{n}_in_proj")
    a, ya = _mix_a_fwd(z, p["conv_a"], p["w_a_out"], ts=256, name=f"{n}_mix_a")
    sg, yb = _mix_b_fwd(z, p["ln_v_g"], p["ln_v_b"], p["w_spatial"], p["b_spatial_t"], p["w_b_out"], ts=256,
                        name=f"{n}_mix_b")
    dpool, yc = _mix_c_fwd(z, p["w_pool"], p["pool_scale"], ts=256, name=f"{n}_mix_c")
    merged, o, x1 = _mix_o_fwd(x, z, ya, yb, yc, p["w_o"], gt1, p["ln1_g"], p["ln1_b"], ts=256, name=f"{n}_mix_o")
    up = _mod_matmul(x1, sc2, sh2, p["w_up"], p["b_up"], ts=512, tn=1408, name=f"{n}_up_proj")
    f, dn, x2 = _ffn_fwd(up, x1, p["conv_ffn"], p["conv_ffn_b"], p["w_down"], gt2, p["ln2_g"], p["ln2_b"], ts=256,
                         name=f"{n}_ffn")
    saved = dict(x=x, z=z, a=a, ya=ya, sg=sg, yb=yb, dpool=dpool, yc=yc, merged=merged, o=o, x1=x1, up=up, f=f, dn=dn)
    return x2, saved


def _layer_bwd(dx2, ada, p, sv, l):
    sh1, sc1, gt1, sh2, sc2, gt2 = ada
    n = f"l{l}"
    d = D_MODEL
    ddn, dup, dx1, red_d, red_f, dbup = _ffn_bwd(
        dx2, sv["x1"], sv["dn"], sv["up"], p["conv_ffn"], p["conv_ffn_b"], p["w_down"], p["w_up"], gt2, p["ln2_g"],
        sc2, ts=256, name=f"{n}_ffn_bwd")
    g = {}
    g["w_down"] = _grad_matmul(sv["f"], ddn, ts=512, tn=512, name=f"{n}_dw_down")
    g["w_up"] = _grad_matmul(sv["x1"], dup, ts=512, tn=1408, name=f"{n}_dw_up", mod=(sc2, sh2))
    g["ln2_g"], g["ln2_b"] = red_d[0], red_d[1]
    g["conv_ffn"], g["conv_ffn_b"], g["b_up"] = red_f[0:3], red_f[3], dbup[0]

    d_o, dxa, dzg, dya, dyb, dyc, red_o = _mix_o_bwd(
        dx1, sv["x"], sv["o"], sv["z"], sv["ya"], sv["yb"], sv["yc"], p["w_o"], gt1, p["ln1_g"], ts=256,
        name=f"{n}_mix_o_bwd")
    g["w_o"] = _grad_matmul(sv["merged"], d_o, ts=512, tn=1024, name=f"{n}_dw_o")
    g["ln1_g"], g["ln1_b"] = red_o[0], red_o[1]

    dza, red_a = _mix_a_bwd(dya, sv["z"], p["conv_a"], p["w_a_out"], ts=256, name=f"{n}_mix_a_bwd")
    g["w_a_out"] = _grad_matmul(sv["a"], dya, ts=512, tn=1024, name=f"{n}_dw_a_out")
    g["conv_a"] = red_a[0:3]

    dzb, red_b, dws, dbst = _mix_b_bwd(dyb, sv["z"], p["ln_v_g"], p["ln_v_b"], p["w_spatial"], p["b_spatial_t"],
                                       p["w_b_out"], ts=256, name=f"{n}_mix_b_bwd")
    g["w_b_out"] = _grad_matmul(sv["sg"], dyb, ts=512, tn=1024, name=f"{n}_dw_b_out")
    g["ln_v_g"], g["ln_v_b"], g["w_spatial"], g["b_spatial"] = red_b[0], red_b[1], dws, dbst.T

    dzc, red_c, dwp = _mix_c_bwd(dyc, sv["z"], p["w_pool"], p["pool_scale"], ts=256, name=f"{n}_mix_c_bwd")
    g["pool_scale"], g["w_pool"] = red_c[0], dwp

    parts = (dza, dzb, dzc, dzg)
    w_in = p["w_in"]
    w_parts = (w_in[:, 0:3 * d], w_in[:, 3 * d:5 * d], w_in[:, 5 * d:6 * d], w_in[:, 6 * d:9 * d])
    res = _in_proj_bwd(parts, w_parts, dxa, sv["x"], sc1, ts=256, name=f"{n}_in_proj_bwd")
    dx, red_i = res[0], res[1]
    g["b_in"] = jnp.concatenate([r[0] for r in res[2:]])
    g["w_in"] = jnp.concatenate(
        [_grad_matmul(sv["x"], dzp, ts=512, tn=1024, name=f"{n}_dw_in{k}", mod=(sc1, sh1))
         for k, dzp in enumerate(parts)], axis=1)
    dada = jnp.stack([red_i[1], red_i[0], red_o[2], red_d[4], red_d[3], red_d[2]])
    return dx, g, dada


_COL_SHARDED = ("w_in", "w_up")
_ROW_SHARDED = ("w_a_out", "w_b_out", "w_o", "w_down")
_BIG = ("w_in", "w_a_out", "w_b_out", "w_pool", "w_o", "w_up", "w_down")
_SMALL_SHARDED = ("conv_a", "conv_ffn")
_SMALL_REPL = ("b_in", "ln_v_g", "ln_v_b", "w_spatial", "b_spatial", "pool_scale", "ln1_g", "ln1_b", "b_up",
               "conv_ffn_b", "ln2_g", "ln2_b")
_WEIGHTS = ("w_ada", "b_ada", "w_in", "b_in", "conv_a", "w_a_out", "ln_v_g", "ln_v_b", "w_spatial", "b_spatial",
            "w_b_out", "w_pool", "pool_scale", "w_o", "ln1_g", "ln1_b", "w_up", "b_up", "conv_ffn", "conv_ffn_b",
            "w_down", "ln2_g", "ln2_b")
_PACK_LANES = 1024


def _chip_slice(name, g, j, static):
    if name in _COL_SHARDED or name in _SMALL_SHARDED:
        ax = g.ndim - 1
    elif name in _ROW_SHARDED:
        ax = 1
    else:
        ax = 2
    w = g.shape[ax] // N_CHIPS
    if static:
        return lax.slice_in_dim(g, j * w, (j + 1) * w, axis=ax)
    return lax.dynamic_slice_in_dim(g, j * w, w, axis=ax)


def _unshard(name, stacked):
    if name in _COL_SHARDED or name in _SMALL_SHARDED:
        ax = stacked.ndim - 2
    elif name in _ROW_SHARDED:
        ax = 1
    else:
        ax = 2
    return jnp.concatenate([stacked[j] for j in range(N_CHIPS)], axis=ax)


def _pack(arrs, dtype):
    parts = []
    for a in arrs:
        flat = a.reshape(-1).astype(dtype)
        pad = (-flat.shape[0]) % (8 * _PACK_LANES)
        parts.append(jnp.pad(flat, (0, pad)) if pad else flat)
    return jnp.concatenate(parts).reshape(-1, _PACK_LANES)


def _unpack(buf, shapes):
    lead = buf.shape[:-2]
    flat = buf.reshape(lead + (-1,))
    out, off = [], 0
    for shp in shapes:
        n = math.prod(shp)
        out.append(flat[..., off:off + n].reshape(lead + tuple(shp)))
        off += n + ((-n) % (8 * _PACK_LANES))
    return out


def _pack_small(arrs):
    parts = []
    for a in arrs:
        flat = a.reshape(-1).astype(F32)
        pad = (-flat.shape[0]) % 128
        parts.append(jnp.pad(flat, (0, pad)) if pad else flat)
    flat = jnp.concatenate(parts)
    pad = (-flat.shape[0]) % 1024
    if pad:
        flat = jnp.pad(flat, (0, pad))
    return flat.reshape(-1, 128)


def _unpack_small(buf, shapes):
    lead = buf.shape[:-2]
    flat = buf.reshape(lead + (-1,))
    out, off = [], 0
    for shp in shapes:
        n = math.prod(shp)
        out.append(flat[..., off:off + n].reshape(lead + tuple(shp)))
        off += n + ((-n) % 128)
    return out


def _as2d(a):
    return a.reshape(-1, a.shape[-1])


def kernel(x, c, w_ada, b_ada, w_in, b_in, conv_a, w_a_out, ln_v_g, ln_v_b, w_spatial, b_spatial, w_b_out, w_pool, pool_scale, w_o, ln1_g, ln1_b, w_up, b_up, conv_ffn, conv_ffn_b, w_down, ln2_g, ln2_b, loss_target, m_w_ada, m_b_ada, m_w_in, m_b_in, m_conv_a, m_w_a_out, m_ln_v_g, m_ln_v_b, m_w_spatial, m_b_spatial, m_w_b_out, m_w_pool, m_pool_scale, m_w_o, m_ln1_g, m_ln1_b, m_w_up, m_b_up, m_conv_ffn, m_conv_ffn_b, m_w_down, m_ln2_g, m_ln2_b, v_w_ada, v_b_ada, v_w_in, v_b_in, v_conv_a, v_w_a_out, v_ln_v_g, v_ln_v_b, v_w_spatial, v_b_spatial, v_w_b_out, v_w_pool, v_pool_scale, v_w_o, v_ln1_g, v_ln1_b, v_w_up, v_b_up, v_conv_ffn, v_conv_ffn_b, v_w_down, v_ln2_g, v_ln2_b):
    args = locals()
    w = {k: args[k] for k in _WEIGHTS}
    m = {k: args["m_" + k] for k in _WEIGHTS}
    v = {k: args["v_" + k] for k in _WEIGHTS}
    d = D_MODEL
    mx, my, mc = _my_coords()
    chip = 2 * mx + my
    me = 4 * mx + 2 * my + mc

    small_shapes = [c.shape, conv_a.shape, conv_ffn.shape]
    small_all = _all_gather8(_pack_small([c, conv_a, conv_ffn]), name="gather_small")
    c_all, conv_a_st, conv_ffn_st = _unpack_small(small_all, small_shapes)
    c_all = c_all.reshape(N_DEV, d)
    full = {"conv_a": _unshard("conv_a", conv_a_st[0::2]), "conv_ffn": _unshard("conv_ffn", conv_ffn_st[0::2])}
    big_shapes = [w[k].shape for k in _BIG]
    big_all = _chip_all_gather(_pack([w[k] for k in _BIG], BF16), name="gather_weights")
    for k, st in zip(_BIG, _unpack(big_all, big_shapes)):
        full[k] = _unshard(k, st)

    n_ada = w_ada.shape[2]
    b_ada_mine = lax.dynamic_slice_in_dim(b_ada, chip * n_ada, n_ada, axis=1)
    ada_part = _ada_fwd(c_all, w_ada, b_ada_mine.reshape(DEPTH, 1, n_ada), name="ada_fwd")
    ada_all = _all_gather8(_pack_small([ada_part]), name="gather_ada")
    ada_st = _unpack_small(ada_all, [ada_part.shape])[0][0::2]
    ada_rows = jnp.concatenate([ada_st[j] for j in range(N_CHIPS)], axis=-1)
    ada_mine = lax.dynamic_index_in_dim(ada_rows, me, axis=1, keepdims=False)

    def layer_params(l):
        p = {k: full[k][l] for k in _BIG + _SMALL_SHARDED}
        for k in ("b_in", "ln_v_g", "ln_v_b", "pool_scale", "ln1_g", "ln1_b", "b_up", "conv_ffn_b", "ln2_g", "ln2_b"):
            p[k] = w[k][l].reshape(1, -1)
        p["w_spatial"] = w_spatial[l]
        p["b_spatial_t"] = b_spatial[l].T
        return p

    xs = x[0]
    saved, adas, params = [], [], []
    for l in range(DEPTH):
        ada = [ada_mine[l, k * d:(k + 1) * d].reshape(1, d) for k in range(6)]
        p = layer_params(l)
        xs, sv = _layer_fwd(xs, ada, p, l)
        saved.append(sv), adas.append(ada), params.append(p)
    dx, loss_blk = _loss_fwd(xs, loss_target[0], ts=512, name="loss")
    grads, dadas = [None] * DEPTH, [None] * DEPTH
    for l in reversed(range(DEPTH)):
        dx, grads[l], dadas[l] = _layer_bwd(dx, adas[l], params[l], saved[l], l)
    gfull = {k: jnp.stack([grads[l][k] for l in range(DEPTH)]) for k in grads[0]}
    dada = jnp.stack(dadas).reshape(DEPTH, 6 * d)

    small_names = _SMALL_REPL + _SMALL_SHARDED
    small_g = [gfull[k] for k in small_names] + [dada, loss_blk[0:1, 0:1]]
    small_g_shapes = [a.shape for a in small_g]
    sg_all, sg_sum = _all_gather8(_pack_small(small_g), name="reduce_small", with_sum=True)
    summed = _unpack_small(sg_sum, small_g_shapes)
    gsum = dict(zip(small_names, summed[:len(small_names)]))
    gsum["b_ada"], loss = summed[-2], summed[-1][0, 0]
    dada_all = _unpack_small(sg_all, small_g_shapes)[-2]
    for k in _SMALL_SHARDED:
        gsum[k] = _chip_slice(k, gsum[k], chip, static=False)

    dada_cols = lax.dynamic_slice_in_dim(dada_all, chip * n_ada, n_ada, axis=2)
    dada_cols = jnp.pad(jnp.swapaxes(dada_cols, 0, 1), ((0, 0), (0, N_DEV), (0, 0)))
    gsum["w_ada"] = _ada_bwd(jnp.pad(c_all, ((0, N_DEV), (0, 0))), dada_cols, name="ada_bwd")

    chunks = jnp.stack([_pack([_chip_slice(k, gfull[k], j, static=True) for k in _BIG], BF16)
                        for j in range(N_CHIPS)])
    own = _pack([_chip_slice(k, gfull[k], chip, static=False) for k in _BIG], F32)
    recv = _chip_exchange(chunks, name="scatter_grads")
    part = _sum4(own, recv, name="sum_grads")
    part_sib = _sibling_swap(part, name="swap_grads")
    part_big = dict(zip(_BIG, _unpack(part, big_shapes)))
    sib_big = dict(zip(_BIG, _unpack(part_sib, big_shapes)))

    out_g, out_d, out_m, out_v = {}, {}, {}, {}
    for k in _WEIGHTS:
        shp = w[k].shape
        if k in _BIG:
            res = _adamw(_as2d(w[k]), _as2d(part_big[k]), _as2d(m[k]), _as2d(v[k]), g2=_as2d(sib_big[k]),
                         name=f"adamw_{k}")
            gk, dk, mk, vk = res
        else:
            gk = gsum[k].reshape(shp)
            dk, mk, vk = _adamw(_as2d(w[k]), _as2d(gk), _as2d(m[k]), _as2d(v[k]), name=f"adamw_{k}")
        out_g[k], out_d[k], out_m[k], out_v[k] = gk.reshape(shp), dk.reshape(shp), mk.reshape(shp), vk.reshape(shp)

    return (loss, dx[None], *[out_g[k] for k in _WEIGHTS], *[out_d[k] for k in _WEIGHTS],
            *[out_m[k] for k in _WEIGHTS], *[out_v[k] for k in _WEIGHTS])
```

```python
import math
from typing import Callable, NamedTuple

import jax
import jax.numpy as jnp
from jax import lax
from jax.experimental import pallas as pl
from jax.experimental.pallas import tpu as pltpu

F32 = jnp.float32
BF16 = jnp.bfloat16

D_MODEL = 1024
D_Z = 9216
D_FF = 2816
N_GROUPS = 8
GBLK = 128
CHUNK = 64
POOL_WINDOWS = (2, 4, 8, 16)
POOL_GROUP = 256
POOL_HALO = 16
CONV_HALO = 8
DEPTH = 2
ALPHA = (2 * DEPTH) ** 0.25
LN_EPS = 1e-5
ADAM_LR, ADAM_B1, ADAM_B2, ADAM_EPS, ADAM_WD, ADAM_STEP = 0.001, 0.9, 0.999, 1e-08, 0.01, 10
N_CHIPS = 4
N_DEV = 8
FF_CHUNK = 704
MESH = pl.DeviceIdType.MESH
VMEM_LIMIT = 56 * 1024 * 1024
HBM = pl.BlockSpec(memory_space=pl.ANY)


def _dot(a, b):
    return jnp.dot(a, b, preferred_element_type=F32)


def _dot_nt(a, b):
    return lax.dot_general(a, b, (((1,), (1,)), ((), ())), preferred_element_type=F32)


def _dot_tn(a, b):
    return lax.dot_general(a, b, (((0,), (0,)), ((), ())), preferred_element_type=F32)


_GELU_C = math.sqrt(2.0 / math.pi)


def _gelu_and_grad(x):
    x2 = x * x
    t = jnp.tanh(_GELU_C * (x + 0.044715 * x * x2))
    g = 0.5 * x * (1.0 + t)
    dg = 0.5 * (1.0 + t) + 0.5 * x * (1.0 - t * t) * (_GELU_C * (1.0 + 3 * 0.044715 * x2))
    return g, dg


def _gelu(x):
    return 0.5 * x * (1.0 + jnp.tanh(_GELU_C * (x + 0.044715 * x * x * x)))


def _sigmoid(x):
    return 1.0 / (1.0 + jnp.exp(-x))


def _ln_fwd(r):
    mu = jnp.mean(r, axis=-1, keepdims=True)
    xc = r - mu
    var = jnp.mean(xc * xc, axis=-1, keepdims=True)
    rstd = lax.rsqrt(var + LN_EPS)
    return xc * rstd, rstd


def _ln_bwd(dy, g, xhat, rstd):
    dxh = dy * g
    m1 = jnp.mean(dxh, axis=-1, keepdims=True)
    m2 = jnp.mean(dxh * xhat, axis=-1, keepdims=True)
    return rstd * (dxh - m1 - xhat * m2)


def _rows_before(ext, k, halo):
    return pltpu.roll(ext, k, 0)[halo:]


def _rows_after(ext, k, n):
    return pltpu.roll(ext, ext.shape[0] - k, 0)[:n]


def _colsum(v):
    return jnp.sum(v, axis=0, keepdims=True)


def _spatial_mask():
    i = lax.broadcasted_iota(jnp.int32, (GBLK, GBLK), 0)
    j = lax.broadcasted_iota(jnp.int32, (GBLK, GBLK), 1)
    return (j // CHUNK) <= (i // CHUNK)


def _const(shape):
    n = len(shape)
    return pl.BlockSpec(shape, lambda *_: (0,) * n)


def _resident(shape):
    n = len(shape)
    return pl.BlockSpec(shape, lambda *_: (0,) * n, pipeline_mode=pl.Buffered(1))


def _tile(ts, s):
    return min(ts, s)


class _Comm(NamedTuple):
    srcs: tuple
    dsts: tuple
    n_remote: int
    n_local: int
    build: Callable


def _my_coords():
    return lax.axis_index("x"), lax.axis_index("y"), lax.axis_index("c")


def _chip_peer(k):
    mx, my, mc = _my_coords()
    return (mx ^ ((k >> 1) & 1), my ^ (k & 1), mc)


def _sem_scratch(comm):
    return [pltpu.SemaphoreType.DMA((max(comm.n_remote, 1),)), pltpu.SemaphoreType.DMA((max(comm.n_remote, 1),)),
            pltpu.SemaphoreType.DMA((max(comm.n_local, 1),))]


def _run(body, *, name, grid, in_specs, out_specs, out_shape, args, scratch_shapes=(), comm=None, aliases=None):
    sem = ("arbitrary",) * len(grid)
    cparams = pltpu.CompilerParams(dimension_semantics=sem, vmem_limit_bytes=VMEM_LIMIT)
    kwargs = {} if aliases is None else {"input_output_aliases": aliases}
    if comm is None:
        return pl.pallas_call(body, name=name, grid=grid, in_specs=in_specs, out_specs=out_specs, out_shape=out_shape,
                              scratch_shapes=list(scratch_shapes), compiler_params=cparams, **kwargs)(*args)
    n_in, n_cs, n_out, n_cd, n_scr = len(in_specs), len(comm.srcs), len(out_specs), len(comm.dsts), len(scratch_shapes)
    total = math.prod(grid)
    mid_step = min(total - 1, int(total * 0.7))

    def wrapped(*refs):
        ins, refs = refs[:n_in], refs[n_in:]
        csrc, refs = refs[:n_cs], refs[n_cs:]
        outs, refs = refs[:n_out], refs[n_out:]
        cdst, refs = refs[:n_cd], refs[n_cd:]
        scr, sems = refs[:n_scr], refs[n_scr:]
        step = pl.program_id(0)
        for ax in range(1, len(grid)):
            step = step * grid[ax] + pl.program_id(ax)
        first, mid, last = comm.build(csrc, cdst, *sems)
        pl.when(step == 0)(first)
        if mid is not None:
            pl.when(step == mid_step)(mid)
        body(*ins, *outs, *scr)
        pl.when(step == total - 1)(last)

    res = pl.pallas_call(
        wrapped, name=name, grid=grid, in_specs=list(in_specs) + [HBM] * n_cs, out_specs=list(out_specs) + [HBM] * n_cd,
        out_shape=list(out_shape) + list(comm.dsts), scratch_shapes=list(scratch_shapes) + _sem_scratch(comm),
        compiler_params=cparams, **kwargs)(*args, *comm.srcs)
    return res[:n_out], res[n_out:]


def _comm_call(comm, *, name):
    def body(*refs):
        n_cs, n_cd = len(comm.srcs), len(comm.dsts)
        first, mid, last = comm.build(refs[:n_cs], refs[n_cs:n_cs + n_cd], *refs[n_cs + n_cd:])
        first()
        if mid is not None:
            mid()
        last()

    return pl.pallas_call(body, name=name, in_specs=[HBM] * len(comm.srcs), out_specs=[HBM] * len(comm.dsts),
                          out_shape=list(comm.dsts), scratch_shapes=_sem_scratch(comm))(*comm.srcs)


def _gather_comm(shards, layer):
    dsts = tuple(jax.ShapeDtypeStruct((N_CHIPS,) + s.shape[1:], s.dtype) for s in shards)
    nw = len(shards)

    def build(srcs, outs, send_sems, recv_sems, local_sems):
        mx, my, mc = _my_coords()
        me = 2 * mx + my
        sibling = (mx, my, 1 - mc)

        def rdma(src, dst, idx, peer):
            return pltpu.make_async_remote_copy(src_ref=src, dst_ref=dst, send_sem=send_sems.at[idx],
                                                recv_sem=recv_sems.at[idx], device_id=peer, device_id_type=MESH)

        def ici(w, k, slot):
            return rdma(srcs[w].at[layer, mc], outs[w].at[slot, mc], 6 * w + k - 1, _chip_peer(k))

        def fwd(w, k, half):
            return rdma(outs[w].at[me ^ k, mc], outs[w].at[me ^ k, half], 6 * w + 2 + k, sibling)

        def own(w):
            return pltpu.make_async_copy(srcs[w].at[layer], outs[w].at[me], local_sems.at[w])

        def first():
            for w in range(nw):
                own(w).start()
                for k in range(1, N_CHIPS):
                    ici(w, k, me).start()

        def mid():
            for w in range(nw):
                for k in range(1, N_CHIPS):
                    ici(w, k, me ^ k).wait_recv()
                    fwd(w, k, mc).start()

        def last():
            for w in range(nw):
                for k in range(1, N_CHIPS):
                    fwd(w, k, 1 - mc).wait_recv()
                    ici(w, k, me).wait_send()
                    fwd(w, k, mc).wait_send()
                own(w).wait()

        return first, mid, last

    return _Comm(tuple(shards), dsts, 6 * nw, nw, build)


def _scatter_comm(g_bf16, g_f32):
    nw = len(g_bf16)
    dsts = tuple(jax.ShapeDtypeStruct((N_CHIPS - 1,) + g.shape[1:], BF16) for g in g_bf16) + tuple(
        jax.ShapeDtypeStruct(g.shape[1:], F32) for g in g_f32)

    def build(srcs, outs, send_sems, recv_sems, local_sems):
        mx, my, _ = _my_coords()
        me = 2 * mx + my

        def copies():
            remote = [pltpu.make_async_remote_copy(
                src_ref=srcs[w].at[me ^ k], dst_ref=outs[w].at[k - 1], send_sem=send_sems.at[3 * w + k - 1],
                recv_sem=recv_sems.at[3 * w + k - 1], device_id=_chip_peer(k), device_id_type=MESH)
                for w in range(nw) for k in range(1, N_CHIPS)]
            local = [pltpu.make_async_copy(srcs[nw + w].at[me], outs[nw + w], local_sems.at[w]) for w in range(nw)]
            return remote, local

        def first():
            remote, local = copies()
            for cp in remote + local:
                cp.start()

        def last():
            remote, local = copies()
            for cp in remote:
                cp.wait_recv()
            for cp in remote:
                cp.wait_send()
            for cp in local:
                cp.wait()

        return first, None, last

    return _Comm(tuple(g_bf16) + tuple(g_f32), dsts, 3 * nw, nw, build)


def _swap_comm(parts):
    dsts = tuple(jax.ShapeDtypeStruct(p.shape, p.dtype) for p in parts)

    def build(srcs, outs, send_sems, recv_sems, local_sems):
        mx, my, mc = _my_coords()

        def copies():
            return [pltpu.make_async_remote_copy(
                src_ref=srcs[w], dst_ref=outs[w], send_sem=send_sems.at[w], recv_sem=recv_sems.at[w],
                device_id=(mx, my, 1 - mc), device_id_type=MESH) for w in range(len(parts))]

        def first():
            for cp in copies():
                cp.start()

        def last():
            for cp in copies():
                cp.wait_recv()
            for cp in copies():
                cp.wait_send()

        return first, None, last

    return _Comm(tuple(parts), dsts, len(parts), 0, build)


def _all_gather8(x, *, name, with_sum=False):
    r, lanes = x.shape

    def body(x_ref, out_ref, *rest):
        if with_sum:
            sum_ref, send_sems, recv_sems, local_sem = rest
        else:
            send_sems, recv_sems, local_sem = rest
        mx, my, mc = _my_coords()
        me = 4 * mx + 2 * my + mc

        def peer(k):
            return (mx ^ ((k >> 2) & 1), my ^ ((k >> 1) & 1), mc ^ (k & 1))

        def copy(k, slot):
            return pltpu.make_async_remote_copy(
                src_ref=x_ref, dst_ref=out_ref.at[slot], send_sem=send_sems.at[k - 1], recv_sem=recv_sems.at[k - 1],
                device_id=peer(k), device_id_type=MESH)

        mine = pltpu.make_async_copy(x_ref, out_ref.at[me], local_sem)
        mine.start()
        sends = [copy(k, me) for k in range(1, N_DEV)]
        for cp in sends:
            cp.start()
        for k in range(1, N_DEV):
            copy(k, me ^ k).wait_recv()
        for cp in sends:
            cp.wait_send()
        mine.wait()
        if with_sum:
            acc = out_ref[0]
            for k in range(1, N_DEV):
                acc = acc + out_ref[k]
            sum_ref[...] = acc

    vmem = pl.BlockSpec(memory_space=pltpu.VMEM)
    out_shape = [jax.ShapeDtypeStruct((N_DEV, r, lanes), F32)]
    if with_sum:
        out_shape.append(jax.ShapeDtypeStruct((r, lanes), F32))
    res = pl.pallas_call(
        body, name=name, in_specs=[vmem], out_specs=[vmem] * len(out_shape), out_shape=out_shape,
        scratch_shapes=[pltpu.SemaphoreType.DMA((N_DEV - 1,)), pltpu.SemaphoreType.DMA((N_DEV - 1,)),
                        pltpu.SemaphoreType.DMA],
        compiler_params=pltpu.CompilerParams(vmem_limit_bytes=VMEM_LIMIT),
    )(x)
    return res if with_sum else res[0]


def _mod_matmul(x, sc, sh, w4, b, *, ts, tn, name, comm=None):
    s, d = x.shape
    wd = w4.shape[2]
    n = N_CHIPS * wd
    per = wd // tn
    ts = _tile(ts, s)

    def body(x_ref, sc_ref, sh_ref, w_ref, b_ref, o_ref, h_scr):
        @pl.when(pl.program_id(1) == 0)
        def _():
            h_scr[...] = (x_ref[...] * (1.0 + sc_ref[...]) + sh_ref[...]).astype(BF16)
        o_ref[...] = _dot(h_scr[...], w_ref[0]) + b_ref[...]

    return _run(
        body, name=name, grid=(s // ts, n // tn),
        in_specs=[pl.BlockSpec((ts, d), lambda i, j: (i, 0)), _const((1, d)), _const((1, d)),
                  pl.BlockSpec((1, d, tn), lambda i, j: (j // per, 0, j % per)),
                  pl.BlockSpec((1, tn), lambda i, j: (0, j))],
        out_specs=[pl.BlockSpec((ts, tn), lambda i, j: (i, j))],
        out_shape=[jax.ShapeDtypeStruct((s, n), F32)],
        scratch_shapes=[pltpu.VMEM((ts, d), BF16)],
        args=(x, sc, sh, w4, b), comm=comm)


def _conv3(q, ext, cw):
    return cw[2:3] * q + cw[1:2] * _rows_before(ext, 1, CONV_HALO) + cw[0:1] * _rows_before(ext, 2, CONV_HALO)


def _mix_a_fwd(z, cw, w_out, *, ts, name):
    s = z.shape[0]
    d = D_MODEL
    ts = _tile(ts, s)

    def body(zb_ref, zc_ref, zx_ref, cw_ref, w_ref, a_ref, y_ref, carry):
        @pl.when(pl.program_id(0) == 0)
        def _():
            carry[...] = jnp.zeros_like(carry)
        q = zc_ref[...] * zx_ref[...]
        ext = jnp.concatenate([carry[...], q], axis=0)
        a = (zb_ref[...] * _conv3(q, ext, cw_ref[...])).astype(BF16)
        carry[...] = q[ts - CONV_HALO:]
        a_ref[...] = a
        y_ref[...] = _dot(a, w_ref[...])

    zspec = lambda k: pl.BlockSpec((ts, d), lambda i, k=k: (i, k))
    return _run(
        body, name=name, grid=(s // ts,),
        in_specs=[zspec(0), zspec(1), zspec(2), _const((3, d)), _const((d, d))],
        out_specs=[pl.BlockSpec((ts, d), lambda i: (i, 0))] * 2,
        out_shape=[jax.ShapeDtypeStruct((s, d), BF16), jax.ShapeDtypeStruct((s, d), F32)],
        scratch_shapes=[pltpu.VMEM((CONV_HALO, d), F32)],
        args=(z, z, z, cw, w_out))


def _spatial_mix(vn_b, ws_ref, bst_ref, mixed_scr, ts):
    nblk = ts // GBLK
    mask = _spatial_mask()
    for g in range(N_GROUPS):
        cols = slice(g * GBLK, (g + 1) * GBLK)
        wm = jnp.where(mask, ws_ref[g], 0.0).astype(BF16)
        cat = jnp.concatenate([vn_b[n * GBLK:(n + 1) * GBLK, cols] for n in range(nblk)], axis=1)
        res = _dot(wm, cat) + bst_ref[:, g:g + 1]
        for n in range(nblk):
            mixed_scr[n * GBLK:(n + 1) * GBLK, cols] = res[:, n * GBLK:(n + 1) * GBLK]


def _mix_b_fwd(z, ln_g, ln_b, ws, bst, w_out, *, ts, name):
    s = z.shape[0]
    d = D_MODEL
    ts = _tile(ts, s)

    def body(zu_ref, zv_ref, g_ref, b_ref, ws_ref, bst_ref, w_ref, sg_ref, y_ref, mixed_scr):
        xhat, _ = _ln_fwd(_gelu(zv_ref[...]))
        vn = (xhat * g_ref[...] + b_ref[...]).astype(BF16)
        _spatial_mix(vn, ws_ref, bst_ref, mixed_scr, ts)
        sg = (_gelu(zu_ref[...]) * mixed_scr[...]).astype(BF16)
        sg_ref[...] = sg
        y_ref[...] = _dot(sg, w_ref[...])

    zspec = lambda k: pl.BlockSpec((ts, d), lambda i, k=k: (i, k))
    return _run(
        body, name=name, grid=(s // ts,),
        in_specs=[zspec(3), zspec(4), _const((1, d)), _const((1, d)), _const((N_GROUPS, GBLK, GBLK)),
                  _const((GBLK, N_GROUPS)), _const((d, d))],
        out_specs=[pl.BlockSpec((ts, d), lambda i: (i, 0))] * 2,
        out_shape=[jax.ShapeDtypeStruct((s, d), BF16), jax.ShapeDtypeStruct((s, d), F32)],
        scratch_shapes=[pltpu.VMEM((ts, d), F32)],
        args=(z, z, ln_g, ln_b, ws, bst, w_out))


def _pool_denoms(tile_idx, ts):
    t1 = (tile_idx * ts + 1 + lax.broadcasted_iota(jnp.int32, (ts, 1), 0)).astype(F32)
    return [jnp.minimum(t1, float(w)) for w in POOL_WINDOWS]


def _pool_diff(p, ext, denoms, k):
    cols = slice(k * POOL_GROUP, (k + 1) * POOL_GROUP)
    acc = ext[:, cols]
    step = 1
    while step < POOL_WINDOWS[k]:
        acc = acc + pltpu.roll(acc, step, 0)
        step *= 2
    return acc[POOL_HALO:] / denoms[k] - p[:, cols]


def _mix_c_fwd(z, w_pool, scale, *, ts, name):
    s = z.shape[0]
    d = D_MODEL
    ts = _tile(ts, s)

    def body(zp_ref, w_ref, sc_ref, d_ref, y_ref, carry):
        i = pl.program_id(0)

        @pl.when(i == 0)
        def _():
            carry[...] = jnp.zeros_like(carry)
        p = zp_ref[...]
        ext = jnp.concatenate([carry[...], p], axis=0)
        carry[...] = p[ts - POOL_HALO:]
        denoms = _pool_denoms(i, ts)
        for k in range(len(POOL_WINDOWS)):
            cols = slice(k * POOL_GROUP, (k + 1) * POOL_GROUP)
            dk = _pool_diff(p, ext, denoms, k).astype(BF16)
            d_ref[:, cols] = dk
            y_ref[:, cols] = _dot(dk, w_ref[k]) * sc_ref[:, cols]

    return _run(
        body, name=name, grid=(s // ts,),
        in_specs=[pl.BlockSpec((ts, d), lambda i: (i, 5)), _const((4, POOL_GROUP, POOL_GROUP)), _const((1, d))],
        out_specs=[pl.BlockSpec((ts, d), lambda i: (i, 0))] * 2,
        out_shape=[jax.ShapeDtypeStruct((s, d), BF16), jax.ShapeDtypeStruct((s, d), F32)],
        scratch_shapes=[pltpu.VMEM((POOL_HALO, d), F32)],
        args=(z, w_pool, scale))


def _mix_o_fwd(x, z, ya, yb, yc, w_o, gt, ln_g, ln_b, *, ts, name, comm=None):
    s, d = x.shape
    ts = _tile(ts, s)

    def body(x_ref, ga_ref, gb_ref, gc_ref, ya_ref, yb_ref, yc_ref, w_ref, gt_ref, g_ref, b_ref,
             m_ref, o_ref, x1_ref):
        merged = (_sigmoid(ga_ref[...]) * ya_ref[...] + _sigmoid(gb_ref[...]) * yb_ref[...]
                  + _sigmoid(gc_ref[...]) * yc_ref[...]).astype(BF16)
        m_ref[...] = merged
        o = _dot(merged, w_ref[...])
        o_ref[...] = o
        xhat, _ = _ln_fwd(ALPHA * x_ref[...] + gt_ref[...] * o)
        x1_ref[...] = xhat * g_ref[...] + b_ref[...]

    row = pl.BlockSpec((ts, d), lambda i: (i, 0))
    zspec = lambda k: pl.BlockSpec((ts, d), lambda i, k=k: (i, k))
    return _run(
        body, name=name, grid=(s // ts,),
        in_specs=[row, zspec(6), zspec(7), zspec(8), row, row, row, _const((d, d)),
                  _const((1, d)), _const((1, d)), _const((1, d))],
        out_specs=[row] * 3,
        out_shape=[jax.ShapeDtypeStruct((s, d), BF16), jax.ShapeDtypeStruct((s, d), F32),
                   jax.ShapeDtypeStruct((s, d), F32)],
        args=(x, z, z, z, ya, yb, yc, w_o, gt, ln_g, ln_b), comm=comm)


def _ffn_fwd(up, x1, cw, cb, w_down, gt, ln_g, ln_b, *, ts, name):
    s, d = x1.shape
    ts = _tile(ts, s)

    def body(up_ref, x1_ref, cw_ref, cb_ref, w_ref, gt_ref, g_ref, b_ref, f_ref, dn_ref, x2_ref, carry):
        @pl.when(pl.program_id(0) == 0)
        def _():
            carry[...] = jnp.zeros_like(carry)
        for c in range(D_FF // FF_CHUNK):
            ca = slice(c * FF_CHUNK, (c + 1) * FF_CHUNK)
            cg = slice(D_FF + c * FF_CHUNK, D_FF + (c + 1) * FF_CHUNK)
            ua = up_ref[:, ca]
            ext = jnp.concatenate([carry[:, ca], ua], axis=0)
            carry[:, ca] = ua[ts - CONV_HALO:]
            cf = _conv3(ua, ext, cw_ref[:, ca]) + cb_ref[:, ca]
            f_ref[:, ca] = (_gelu(cf) * up_ref[:, cg]).astype(BF16)
        dn = _dot(f_ref[...], w_ref[...])
        dn_ref[...] = dn
        xhat, _ = _ln_fwd(ALPHA * x1_ref[...] + gt_ref[...] * dn)
        x2_ref[...] = xhat * g_ref[...] + b_ref[...]

    row = pl.BlockSpec((ts, d), lambda i: (i, 0))
    return _run(
        body, name=name, grid=(s // ts,),
        in_specs=[pl.BlockSpec((ts, 2 * D_FF), lambda i: (i, 0)), row, _const((3, D_FF)), _const((1, D_FF)),
                  _resident((D_FF, d)), _const((1, d)), _const((1, d)), _const((1, d))],
        out_specs=[pl.BlockSpec((ts, D_FF), lambda i: (i, 0)), row, row],
        out_shape=[jax.ShapeDtypeStruct((s, D_FF), BF16), jax.ShapeDtypeStruct((s, d), F32),
                   jax.ShapeDtypeStruct((s, d), F32)],
        scratch_shapes=[pltpu.VMEM((CONV_HALO, D_FF), F32)],
        args=(up, x1, cw, cb, w_down, gt, ln_g, ln_b))


def _loss_fwd(y, tgt, *, ts, name):
    s, d = y.shape
    ts = _tile(ts, s)

    def body(y_ref, t_ref, dy_ref, l_ref):
        @pl.when(pl.program_id(0) == 0)
        def _():
            l_ref[...] = jnp.zeros_like(l_ref)
        e = y_ref[...] - t_ref[...]
        dy_ref[...] = e / float(d)
        l_ref[...] += 0.5 * jnp.sum(jnp.mean(e * e, axis=-1, keepdims=True), axis=0, keepdims=True)

    row = pl.BlockSpec((ts, d), lambda i: (i, 0))
    return _run(body, name=name, grid=(s // ts,), in_specs=[row, row], out_specs=[row, _const((8, 128))],
                out_shape=[jax.ShapeDtypeStruct((s, d), F32), jax.ShapeDtypeStruct((8, 128), F32)], args=(y, tgt))


def _rev(n_tiles):
    return lambda i: n_tiles - 1 - i


def _halo_spec(ts, n_tiles, halo, width, col):
    per = ts // halo
    return pl.BlockSpec((halo, width), lambda i: (jnp.maximum((n_tiles - 1 - i) * per - 1, 0), col))


def _ffn_bwd(dx2, x1, dn, up, cw, cb, w_down, w_up4, gt, ln_g, sc, *, ts, name, comm=None):
    s, d = x1.shape
    ts = _tile(ts, s)
    nt = s // ts
    rev = _rev(nt)
    wd = w_up4.shape[2]

    def w_up_cols(wu_ref, start):
        return wu_ref[start // wd, :, start % wd:start % wd + FF_CHUNK]

    def body(dx2_ref, x1_ref, dn_ref, up_ref, halo_ref, cw_ref, cb_ref, wd_ref, wu_ref, gt_ref, g_ref, sc_ref,
             ddn_ref, dup_ref, dx1_ref, redd_ref, redf_ref, dbup_ref, carry):
        i = pl.program_id(0)

        @pl.when(i == 0)
        def _():
            carry[...] = jnp.zeros_like(carry)
            redd_ref[...] = jnp.zeros_like(redd_ref)
            redf_ref[...] = jnp.zeros_like(redf_ref)
            dbup_ref[...] = jnp.zeros_like(dbup_ref)
        first_tile = i == nt - 1
        x1v, dnv, dyv = x1_ref[...], dn_ref[...], dx2_ref[...]
        xhat, rstd = _ln_fwd(ALPHA * x1v + gt_ref[...] * dnv)
        dr = _ln_bwd(dyv, g_ref[...], xhat, rstd)
        redd_ref[0:1, :] += _colsum(dyv * xhat)
        redd_ref[1:2, :] += _colsum(dyv)
        redd_ref[2:3, :] += _colsum(dr * dnv)
        ddn = (gt_ref[...] * dr).astype(BF16)
        ddn_ref[...] = ddn
        dh = jnp.zeros((ts, d), F32)
        for c in range(D_FF // FF_CHUNK):
            ca = slice(c * FF_CHUNK, (c + 1) * FF_CHUNK)
            cg = slice(D_FF + c * FF_CHUNK, D_FF + (c + 1) * FF_CHUNK)
            df = _dot_nt(ddn, wd_ref[ca, :])
            ua, ug = up_ref[:, ca], up_ref[:, cg]
            halo = jnp.where(first_tile, 0.0, halo_ref[:, ca])
            ext = jnp.concatenate([halo, ua], axis=0)
            u1, u2 = _rows_before(ext, 1, CONV_HALO), _rows_before(ext, 2, CONV_HALO)
            cwc = cw_ref[:, ca]
            gl, dgl = _gelu_and_grad(cwc[2:3] * ua + cwc[1:2] * u1 + cwc[0:1] * u2 + cb_ref[:, ca])
            dug = df * gl
            dcf = df * ug * dgl
            redf_ref[0:1, ca] += _colsum(dcf * u2)
            redf_ref[1:2, ca] += _colsum(dcf * u1)
            redf_ref[2:3, ca] += _colsum(dcf * ua)
            redf_ref[3:4, ca] += _colsum(dcf)
            extd = jnp.concatenate([dcf, carry[:, ca]], axis=0)
            carry[:, ca] = dcf[:CONV_HALO]
            dua = cwc[2:3] * dcf + cwc[1:2] * _rows_after(extd, 1, ts) + cwc[0:1] * _rows_after(extd, 2, ts)
            dbup_ref[0:1, ca] += _colsum(dua)
            dbup_ref[0:1, cg] += _colsum(dug)
            dua_b, dug_b = dua.astype(BF16), dug.astype(BF16)
            dup_ref[:, ca] = dua_b
            dup_ref[:, cg] = dug_b
            dh = dh + _dot_nt(dua_b, w_up_cols(wu_ref, c * FF_CHUNK)) + _dot_nt(dug_b, w_up_cols(wu_ref, D_FF + c * FF_CHUNK))
        dx1_ref[...] = ALPHA * dr + dh * (1.0 + sc_ref[...])
        redd_ref[3:4, :] += _colsum(dh * x1v)
        redd_ref[4:5, :] += _colsum(dh)

    row = pl.BlockSpec((ts, d), lambda i: (rev(i), 0))
    return _run(
        body, name=name, grid=(nt,),
        in_specs=[row, row, row, pl.BlockSpec((ts, 2 * D_FF), lambda i: (rev(i), 0)),
                  _halo_spec(ts, nt, CONV_HALO, D_FF, 0), _const((3, D_FF)), _const((1, D_FF)),
                  _resident((D_FF, d)), _resident((N_CHIPS, d, wd)), _const((1, d)), _const((1, d)), _const((1, d))],
        out_specs=[row, pl.BlockSpec((ts, 2 * D_FF), lambda i: (rev(i), 0)), row,
                   _const((8, d)), _const((8, D_FF)), _const((8, 2 * D_FF))],
        out_shape=[jax.ShapeDtypeStruct((s, d), BF16), jax.ShapeDtypeStruct((s, 2 * D_FF), BF16),
                   jax.ShapeDtypeStruct((s, d), F32), jax.ShapeDtypeStruct((8, d), F32),
                   jax.ShapeDtypeStruct((8, D_FF), F32), jax.ShapeDtypeStruct((8, 2 * D_FF), F32)],
        scratch_shapes=[pltpu.VMEM((CONV_HALO, D_FF), F32)],
        args=(dx2, x1, dn, up, up, cw, cb, w_down, w_up4, gt, ln_g, sc), comm=comm)


def _grad_matmul(xa, dy, *, ts, tn, name, mod=None, by_chip=False, comm=None):
    s, k = xa.shape
    n = dy.shape[1]
    ts = _tile(ts, s)
    nt = s // ts

    def body(*refs):
        if mod is None:
            xa_ref, dy_ref, o_ref, ob_ref = refs
            a = xa_ref[...]
        else:
            xa_ref, sc_ref, sh_ref, dy_ref, o_ref, ob_ref = refs
            a = (xa_ref[...] * (1.0 + sc_ref[...]) + sh_ref[...]).astype(BF16)
        t = pl.program_id(1)

        @pl.when(t == 0)
        def _():
            o_ref[...] = jnp.zeros_like(o_ref)
        o_ref[...] += _dot_tn(a, dy_ref[...]).reshape(o_ref.shape)

        @pl.when(t == nt - 1)
        def _():
            ob_ref[...] = o_ref[...].astype(BF16)

    xspec = pl.BlockSpec((ts, k), lambda j, t: (t, 0))
    dspec = pl.BlockSpec((ts, tn), lambda j, t: (t, j))
    in_specs = [xspec, dspec] if mod is None else [xspec, _const((1, k)), _const((1, k)), dspec]
    args = (xa, dy) if mod is None else (xa, mod[0], mod[1], dy)
    if by_chip:
        per = n // N_CHIPS // tn
        ospec = pl.BlockSpec((1, k, tn), lambda j, t: (j // per, 0, j % per))
        shape = (N_CHIPS, k, n // N_CHIPS)
    else:
        ospec = pl.BlockSpec((k, tn), lambda j, t: (0, j))
        shape = (k, n)
    return _run(body, name=name, grid=(n // tn, nt), in_specs=in_specs, out_specs=[ospec, ospec],
                out_shape=[jax.ShapeDtypeStruct(shape, F32), jax.ShapeDtypeStruct(shape, BF16)], args=args, comm=comm)


def _mix_o_bwd(dx1, x, o, z, ya, yb, yc, w_o, gt, ln_g, *, ts, name, comm=None):
    s, d = x.shape
    ts = _tile(ts, s)

    def body(dx1_ref, x_ref, o_ref, ga_ref, gb_ref, gc_ref, ya_ref, yb_ref, yc_ref, w_ref, gt_ref, g_ref,
             do_ref, dxa_ref, dzg_ref, dya_ref, dyb_ref, dyc_ref, red_ref):
        @pl.when(pl.program_id(0) == 0)
        def _():
            red_ref[...] = jnp.zeros_like(red_ref)
        dyv, ov = dx1_ref[...], o_ref[...]
        xhat, rstd = _ln_fwd(ALPHA * x_ref[...] + gt_ref[...] * ov)
        dr = _ln_bwd(dyv, g_ref[...], xhat, rstd)
        red_ref[0:1, :] += _colsum(dyv * xhat)
        red_ref[1:2, :] += _colsum(dyv)
        red_ref[2:3, :] += _colsum(dr * ov)
        dxa_ref[...] = ALPHA * dr
        d_o = (gt_ref[...] * dr).astype(BF16)
        do_ref[...] = d_o
        dm = _dot_nt(d_o, w_ref[...])
        for k, (zg_ref, y_ref, dy_ref) in enumerate(((ga_ref, ya_ref, dya_ref), (gb_ref, yb_ref, dyb_ref),
                                                     (gc_ref, yc_ref, dyc_ref))):
            g = _sigmoid(zg_ref[...])
            dzg_ref[:, k * d:(k + 1) * d] = (dm * y_ref[...] * g * (1.0 - g)).astype(BF16)
            dy_ref[...] = (dm * g).astype(BF16)

    row = pl.BlockSpec((ts, d), lambda i: (i, 0))
    zspec = lambda k: pl.BlockSpec((ts, d), lambda i, k=k: (i, k))
    bf = jax.ShapeDtypeStruct((s, d), BF16)
    return _run(
        body, name=name, grid=(s // ts,),
        in_specs=[row, row, row, zspec(6), zspec(7), zspec(8), row, row, row, _const((d, d)),
                  _const((1, d)), _const((1, d))],
        out_specs=[row, row, pl.BlockSpec((ts, 3 * d), lambda i: (i, 2)), row, row, row, _const((8, d))],
        out_shape=[bf, jax.ShapeDtypeStruct((s, d), F32), jax.ShapeDtypeStruct((s, D_Z), BF16), bf, bf, bf,
                   jax.ShapeDtypeStruct((8, d), F32)],
        args=(dx1, x, o, z, z, z, ya, yb, yc, w_o, gt, ln_g), comm=comm)


def _mix_a_bwd(dya, z, dz, cw, w_out, *, ts, name, comm=None):
    s = z.shape[0]
    d = D_MODEL
    ts = _tile(ts, s)
    nt = s // ts
    rev = _rev(nt)

    def body(dya_ref, zb_ref, zc_ref, zx_ref, hc_ref, hx_ref, cw_ref, w_ref, dz_in, dz_ref, red_ref, carry):
        i = pl.program_id(0)

        @pl.when(i == 0)
        def _():
            carry[...] = jnp.zeros_like(carry)
            red_ref[...] = jnp.zeros_like(red_ref)
        zb, zc, zx = zb_ref[...], zc_ref[...], zx_ref[...]
        q = zc * zx
        halo = jnp.where(i == nt - 1, 0.0, hc_ref[...] * hx_ref[...])
        ext = jnp.concatenate([halo, q], axis=0)
        q1, q2 = _rows_before(ext, 1, CONV_HALO), _rows_before(ext, 2, CONV_HALO)
        cwv = cw_ref[...]
        cv = cwv[2:3] * q + cwv[1:2] * q1 + cwv[0:1] * q2
        da = _dot_nt(dya_ref[...], w_ref[...])
        dcv = da * zb
        red_ref[0:1, :] += _colsum(dcv * q2)
        red_ref[1:2, :] += _colsum(dcv * q1)
        red_ref[2:3, :] += _colsum(dcv * q)
        extd = jnp.concatenate([dcv, carry[...]], axis=0)
        carry[...] = dcv[:CONV_HALO]
        dq = cwv[2:3] * dcv + cwv[1:2] * _rows_after(extd, 1, ts) + cwv[0:1] * _rows_after(extd, 2, ts)
        dz_ref[:, 0:d] = (da * cv).astype(BF16)
        dz_ref[:, d:2 * d] = (dq * zx).astype(BF16)
        dz_ref[:, 2 * d:3 * d] = (dq * zc).astype(BF16)

    zspec = lambda k: pl.BlockSpec((ts, d), lambda i, k=k: (rev(i), k))
    return _run(
        body, name=name, grid=(nt,),
        in_specs=[pl.BlockSpec((ts, d), lambda i: (rev(i), 0)), zspec(0), zspec(1), zspec(2),
                  _halo_spec(ts, nt, CONV_HALO, d, 1), _halo_spec(ts, nt, CONV_HALO, d, 2),
                  _const((3, d)), _const((d, d)), HBM],
        out_specs=[pl.BlockSpec((ts, 3 * d), lambda i: (rev(i), 0)), _const((8, d))],
        out_shape=[jax.ShapeDtypeStruct((s, D_Z), BF16), jax.ShapeDtypeStruct((8, d), F32)],
        scratch_shapes=[pltpu.VMEM((CONV_HALO, d), F32)],
        args=(dya, z, z, z, z, z, cw, w_out, dz), aliases={8: 0}, comm=comm)


def _mix_b_bwd(dyb, z, dz, ln_g, ln_b, ws, bst, w_out, *, ts, name, comm=None):
    s = z.shape[0]
    d = D_MODEL
    ts = _tile(ts, s)
    nblk = ts // GBLK

    def body(dyb_ref, zu_ref, zv_ref, g_ref, b_ref, ws_ref, bst_ref, w_ref, dz_in,
             dz_ref, red_ref, dws_ref, dbst_ref, mixed_scr, dvn_scr, dzv_scr):
        @pl.when((pl.program_id(0) == 0) & (pl.program_id(1) == 0))
        def _():
            red_ref[...] = jnp.zeros_like(red_ref)
            dws_ref[...] = jnp.zeros_like(dws_ref)
            dbst_ref[...] = jnp.zeros_like(dbst_ref)

        @pl.when(pl.program_id(1) == 0)
        def _():
            u, du_dz = _gelu_and_grad(zu_ref[...])
            vg, dv_dz = _gelu_and_grad(zv_ref[...])
            xhat, rstd = _ln_fwd(vg)
            vn = (xhat * g_ref[...] + b_ref[...]).astype(BF16)
            _spatial_mix(vn, ws_ref, bst_ref, mixed_scr, ts)
            dsg = _dot_nt(dyb_ref[...], w_ref[...])
            dz_ref[...] = (dsg * mixed_scr[...] * du_dz).astype(BF16)
            dmix = dsg * u
            mask = _spatial_mask()
            for g in range(N_GROUPS):
                cols = slice(g * GBLK, (g + 1) * GBLK)
                wm = jnp.where(mask, ws_ref[g], 0.0).astype(BF16)
                dm_cat = jnp.concatenate([dmix[n * GBLK:(n + 1) * GBLK, cols] for n in range(nblk)], axis=1)
                vn_cat = jnp.concatenate([vn[n * GBLK:(n + 1) * GBLK, cols] for n in range(nblk)], axis=1)
                dm_b = dm_cat.astype(BF16)
                dbst_ref[:, g:g + 1] += jnp.sum(dm_cat, axis=1, keepdims=True)
                dws_ref[g] += jnp.where(mask, _dot_nt(dm_b, vn_cat), 0.0)
                dvn_cat = _dot_tn(wm, dm_b)
                for n in range(nblk):
                    dvn_scr[n * GBLK:(n + 1) * GBLK, cols] = dvn_cat[:, n * GBLK:(n + 1) * GBLK]
            dvn = dvn_scr[...]
            red_ref[0:1, :] += _colsum(dvn * xhat)
            red_ref[1:2, :] += _colsum(dvn)
            dzv_scr[...] = (_ln_bwd(dvn, g_ref[...], xhat, rstd) * dv_dz).astype(BF16)

        @pl.when(pl.program_id(1) == 1)
        def _():
            dz_ref[...] = dzv_scr[...]

    zspec = lambda k: pl.BlockSpec((ts, d), lambda i, h, k=k: (i, k))
    return _run(
        body, name=name, grid=(s // ts, 2),
        in_specs=[pl.BlockSpec((ts, d), lambda i, h: (i, 0)), zspec(3), zspec(4), _const((1, d)), _const((1, d)),
                  _const((N_GROUPS, GBLK, GBLK)), _const((GBLK, N_GROUPS)), _const((d, d)), HBM],
        out_specs=[pl.BlockSpec((ts, d), lambda i, h: (i, 3 + h)), _const((8, d)),
                   _const((N_GROUPS, GBLK, GBLK)), _const((GBLK, N_GROUPS))],
        out_shape=[jax.ShapeDtypeStruct((s, D_Z), BF16), jax.ShapeDtypeStruct((8, d), F32),
                   jax.ShapeDtypeStruct((N_GROUPS, GBLK, GBLK), F32), jax.ShapeDtypeStruct((GBLK, N_GROUPS), F32)],
        scratch_shapes=[pltpu.VMEM((ts, d), F32), pltpu.VMEM((ts, d), F32), pltpu.VMEM((ts, d), BF16)],
        args=(dyb, z, z, ln_g, ln_b, ws, bst, w_out, dz), aliases={8: 0}, comm=comm)


def _mix_c_bwd(dyc, z, dz, w_pool, scale, *, ts, name):
    s = z.shape[0]
    d = D_MODEL
    ts = _tile(ts, s)
    nt = s // ts
    rev = _rev(nt)

    def body(dyc_ref, zp_ref, halo_ref, w_ref, sc_ref, dz_in, dz_ref, red_ref, dw_ref, carry):
        i = pl.program_id(0)

        @pl.when(i == 0)
        def _():
            carry[...] = jnp.zeros_like(carry)
            red_ref[...] = jnp.zeros_like(red_ref)
            dw_ref[...] = jnp.zeros_like(dw_ref)
        p = zp_ref[...]
        ext = jnp.concatenate([jnp.where(i == nt - 1, 0.0, halo_ref[...]), p], axis=0)
        denoms = _pool_denoms(rev(i), ts)
        dyv = dyc_ref[...].astype(F32)
        for k in range(len(POOL_WINDOWS)):
            cols = slice(k * POOL_GROUP, (k + 1) * POOL_GROUP)
            dk = _pool_diff(p, ext, denoms, k).astype(BF16)
            red_ref[0:1, cols] += _colsum(dyv[:, cols] * _dot(dk, w_ref[k]))
            dpre = (dyv[:, cols] * sc_ref[:, cols]).astype(BF16)
            dw_ref[k] += _dot_tn(dk, dpre)
            dd = _dot_nt(dpre, w_ref[k])
            e = dd / denoms[k]
            acc = jnp.concatenate([e, carry[:, cols]], axis=0)
            carry[:, cols] = e[:POOL_HALO]
            step = 1
            while step < POOL_WINDOWS[k]:
                acc = acc + pltpu.roll(acc, acc.shape[0] - step, 0)
                step *= 2
            dz_ref[:, cols] = (acc[:ts] - dd).astype(BF16)

    return _run(
        body, name=name, grid=(nt,),
        in_specs=[pl.BlockSpec((ts, d), lambda i: (rev(i), 0)), pl.BlockSpec((ts, d), lambda i: (rev(i), 5)),
                  _halo_spec(ts, nt, POOL_HALO, d, 5), _const((4, POOL_GROUP, POOL_GROUP)), _const((1, d)), HBM],
        out_specs=[pl.BlockSpec((ts, d), lambda i: (rev(i), 5)), _const((8, d)), _const((4, POOL_GROUP, POOL_GROUP))],
        out_shape=[jax.ShapeDtypeStruct((s, D_Z), BF16), jax.ShapeDtypeStruct((8, d), F32),
                   jax.ShapeDtypeStruct((4, POOL_GROUP, POOL_GROUP), F32)],
        scratch_shapes=[pltpu.VMEM((POOL_HALO, d), F32)],
        args=(dyc, z, z, w_pool, scale, dz), aliases={5: 0})


def _in_proj_bwd(dz, w4, dxa, x, sc, *, ts, name, comm=None):
    s, d = x.shape
    ts = _tile(ts, s)
    wd = w4.shape[2]

    def body(dz_ref, w_ref, dxa_ref, x_ref, sc_ref, dx_ref, red_ref, db_ref):
        @pl.when(pl.program_id(0) == 0)
        def _():
            red_ref[...] = jnp.zeros_like(red_ref)
            db_ref[...] = jnp.zeros_like(db_ref)
        dh = jnp.zeros((ts, d), F32)
        for j in range(N_CHIPS):
            dzj = dz_ref[:, j * wd:(j + 1) * wd]
            db_ref[0:1, j * wd:(j + 1) * wd] += _colsum(dzj.astype(F32))
            dh = dh + _dot_nt(dzj, w_ref[j])
        dx_ref[...] = dxa_ref[...] + dh * (1.0 + sc_ref[...])
        red_ref[0:1, :] += _colsum(dh * x_ref[...])
        red_ref[1:2, :] += _colsum(dh)

    row = pl.BlockSpec((ts, d), lambda i: (i, 0))
    return _run(
        body, name=name, grid=(s // ts,),
        in_specs=[pl.BlockSpec((ts, D_Z), lambda i: (i, 0)), _resident((N_CHIPS, d, wd)), row, row, _const((1, d))],
        out_specs=[row, _const((8, d)), _const((8, D_Z))],
        out_shape=[jax.ShapeDtypeStruct((s, d), F32), jax.ShapeDtypeStruct((8, d), F32),
                   jax.ShapeDtypeStruct((8, D_Z), F32)],
        args=(dz, w4, dxa, x, sc), comm=comm)


def _ada_fwd(c_all, w_ada, b_ada, *, name):
    nl, d, n = w_ada.shape
    tn = n // 2

    def body(c_ref, w_ref, b_ref, o_ref):
        cv = c_ref[...]
        ca = (cv * _sigmoid(cv)).astype(BF16)
        o_ref[0] = _dot(ca, w_ref[0].astype(BF16)) + b_ref[0]

    return _run(
        body, name=name, grid=(nl, n // tn),
        in_specs=[_const((N_DEV, d)), pl.BlockSpec((1, d, tn), lambda l, j: (l, 0, j)),
                  pl.BlockSpec((1, 1, tn), lambda l, j: (l, 0, j))],
        out_specs=[pl.BlockSpec((1, N_DEV, tn), lambda l, j: (l, 0, j))],
        out_shape=[jax.ShapeDtypeStruct((nl, N_DEV, n), F32)], args=(c_all, w_ada, b_ada))[0]


def _ada_bwd(c_all, dada, *, name):
    nl, nb, n = dada.shape
    d = c_all.shape[1]
    tn = n // 2

    def body(c_ref, g_ref, o_ref):
        cv = c_ref[...]
        ca = (cv * _sigmoid(cv)).astype(BF16)
        o_ref[0] = _dot_tn(ca, g_ref[0].astype(BF16))

    return _run(
        body, name=name, grid=(nl, n // tn),
        in_specs=[_const((nb, d)), pl.BlockSpec((1, nb, tn), lambda l, j: (l, 0, j))],
        out_specs=[pl.BlockSpec((1, d, tn), lambda l, j: (l, 0, j))],
        out_shape=[jax.ShapeDtypeStruct((nl, d, n), F32)], args=(c_all, dada))[0]


def _sum4(own, recv, *, name):
    r, c = own.shape
    tr = _row_tile(r, c, 2)

    def body(own_ref, recv_ref, o_ref):
        acc = own_ref[...]
        for k in range(N_CHIPS - 1):
            acc = acc + recv_ref[k].astype(F32)
        o_ref[...] = acc

    return _run(
        body, name=name, grid=(r // tr,),
        in_specs=[pl.BlockSpec((tr, c), lambda i: (i, 0)), pl.BlockSpec((N_CHIPS - 1, tr, c), lambda i: (0, i, 0))],
        out_specs=[pl.BlockSpec((tr, c), lambda i: (i, 0))], out_shape=[jax.ShapeDtypeStruct((r, c), F32)],
        args=(own, recv))[0]


def _row_tile(r, c, mib):
    limit = max(8, (mib << 20) // (4 * c))
    if r <= limit:
        return r
    best = 8
    for t in range(8, limit + 1, 8):
        if r % t == 0:
            best = t
    return best


def _adam_math(w, g, m, v):
    mn = ADAM_B1 * m + (1.0 - ADAM_B1) * g
    vn = ADAM_B2 * v + (1.0 - ADAM_B2) * (g * g)
    m_hat = mn / (1.0 - ADAM_B1 ** ADAM_STEP)
    v_hat = vn / (1.0 - ADAM_B2 ** ADAM_STEP)
    return -ADAM_LR * (m_hat / (jnp.sqrt(v_hat) + ADAM_EPS) + ADAM_WD * w), mn, vn


def _adamw(w, g, m, v, *, name):
    r, c = w.shape
    tr = _row_tile(r, c, 2)

    def body(w_ref, g_ref, m_ref, v_ref, d_ref, mo_ref, vo_ref):
        d_ref[...], mo_ref[...], vo_ref[...] = _adam_math(w_ref[...], g_ref[...], m_ref[...], v_ref[...])

    blk = pl.BlockSpec((tr, c), lambda i: (i, 0))
    return _run(body, name=name, grid=(r // tr,), in_specs=[blk] * 4, out_specs=[blk] * 3,
                out_shape=[jax.ShapeDtypeStruct((r, c), F32)] * 3, args=(w, g, m, v))


def _adamw_sharded(w, m, v, parts, *, name):
    nl, r, c = w.shape
    tr = _row_tile(r, c, 1)
    nt = r // tr

    def body(w_ref, m_ref, v_ref, a0, b0, a1, b1, g_ref, d_ref, mo_ref, vo_ref):
        g = jnp.where(pl.program_id(0) == 0, a0[...] + b0[...], a1[...] + b1[...])
        g_ref[0] = g
        d_ref[0], mo_ref[0], vo_ref[0] = _adam_math(w_ref[0], g, m_ref[0], v_ref[0])

    blk = pl.BlockSpec((1, tr, c), lambda l, i: (l, i, 0))
    part0 = pl.BlockSpec((tr, c), lambda l, i: (jnp.where(l == 0, i, nt - 1), 0))
    part1 = pl.BlockSpec((tr, c), lambda l, i: (jnp.where(l == 1, i, 0), 0))
    return _run(body, name=name, grid=(nl, nt), in_specs=[blk] * 3 + [part0, part0, part1, part1],
                out_specs=[blk] * 4, out_shape=[jax.ShapeDtypeStruct((nl, r, c), F32)] * 4,
                args=(w, m, v, parts[0][0], parts[0][1], parts[1][0], parts[1][1]))


_BIG = ("w_in", "w_a_out", "w_b_out", "w_pool", "w_o", "w_up", "w_down")
_COL_SHARDED = ("w_in", "w_up")
_SMALL_SHARDED = ("conv_a", "conv_ffn")
_SMALL_REPL = ("b_in", "ln_v_g", "ln_v_b", "w_spatial", "b_spatial", "pool_scale", "ln1_g", "ln1_b", "b_up",
               "conv_ffn_b", "ln2_g", "ln2_b")
_WEIGHTS = ("w_ada", "b_ada", "w_in", "b_in", "conv_a", "w_a_out", "ln_v_g", "ln_v_b", "w_spatial", "b_spatial",
            "w_b_out", "w_pool", "pool_scale", "w_o", "ln1_g", "ln1_b", "w_up", "b_up", "conv_ffn", "conv_ffn_b",
            "w_down", "ln2_g", "ln2_b")


def _shard3(a):
    return a.reshape(a.shape[0], -1, a.shape[-1])


def _use_gathered(name, g):
    g = g.reshape(N_CHIPS, -1, g.shape[-1])
    if name in _COL_SHARDED:
        return g
    if name == "w_pool":
        return g.reshape(N_CHIPS, 4, POOL_GROUP // N_CHIPS, POOL_GROUP).transpose(1, 0, 2, 3).reshape(
            4, POOL_GROUP, POOL_GROUP)
    return g.reshape(-1, g.shape[-1])


def _grad_by_chip(name, g):
    if name in _COL_SHARDED:
        return g
    if name == "w_pool":
        return g.reshape(4, N_CHIPS, POOL_GROUP // N_CHIPS, POOL_GROUP).transpose(1, 0, 2, 3).reshape(
            N_CHIPS, POOL_GROUP, POOL_GROUP)
    return g.reshape(N_CHIPS, -1, g.shape[-1])


def _pack_small(arrs):
    parts = []
    for a in arrs:
        flat = a.reshape(-1).astype(F32)
        pad = (-flat.shape[0]) % 128
        parts.append(jnp.pad(flat, (0, pad)) if pad else flat)
    flat = jnp.concatenate(parts)
    pad = (-flat.shape[0]) % 1024
    if pad:
        flat = jnp.pad(flat, (0, pad))
    return flat.reshape(-1, 128)


def _unpack_small(buf, shapes):
    lead = buf.shape[:-2]
    flat = buf.reshape(lead + (-1,))
    out, off = [], 0
    for shp in shapes:
        n = math.prod(shp)
        out.append(flat[..., off:off + n].reshape(lead + tuple(shp)))
        off += n + ((-n) % 128)
    return out


def _as2d(a):
    return a.reshape(-1, a.shape[-1])


def _layer_fwd(x, ada, p, l, comms):
    sh1, sc1, gt1, sh2, sc2, gt2 = ada
    n = f"l{l}"
    got = {}

    def carried(key, res):
        if comms.get(key) is None:
            return res
        got[key] = res[1]
        return res[0]

    z, = carried("in_proj", _mod_matmul(x, sc1, sh1, p["w_in"], p["b_in"], ts=512, tn=1152, name=f"{n}_in_proj",
                                        comm=comms.get("in_proj")))
    a, ya = _mix_a_fwd(z, p["conv_a"], p["w_a_out"], ts=256, name=f"{n}_mix_a")
    sg, yb = _mix_b_fwd(z, p["ln_v_g"], p["ln_v_b"], p["w_spatial"], p["b_spatial_t"], p["w_b_out"], ts=256,
                        name=f"{n}_mix_b")
    dpool, yc = _mix_c_fwd(z, p["w_pool"], p["pool_scale"], ts=256, name=f"{n}_mix_c")
    merged, o, x1 = carried("mix_o", _mix_o_fwd(x, z, ya, yb, yc, p["w_o"], gt1, p["ln1_g"], p["ln1_b"], ts=256,
                                                name=f"{n}_mix_o", comm=comms.get("mix_o")))
    up, = carried("up_proj", _mod_matmul(x1, sc2, sh2, p["w_up"], p["b_up"], ts=512, tn=1408, name=f"{n}_up_proj",
                                         comm=comms.get("up_proj")))
    f, dn, x2 = _ffn_fwd(up, x1, p["conv_ffn"], p["conv_ffn_b"], p["w_down"], gt2, p["ln2_g"], p["ln2_b"], ts=256,
                         name=f"{n}_ffn")
    saved = dict(x=x, z=z, a=a, ya=ya, sg=sg, yb=yb, dpool=dpool, yc=yc, merged=merged, o=o, x1=x1, up=up, f=f, dn=dn)
    return x2, saved, got


class _Scatter:
    def __init__(self):
        self.ready = {}
        self.parts = {}

    def add(self, layer, name, g_f32, g_bf16):
        self.ready[(layer, name)] = (_grad_by_chip(name, g_f32), _grad_by_chip(name, g_bf16))

    def comm(self, keys):
        return _scatter_comm([self.ready[k][1] for k in keys], [self.ready[k][0] for k in keys])

    def landed(self, keys, res, tag):
        nw = len(keys)
        for i, k in enumerate(keys):
            self.parts[k] = _sum4(res[nw + i], res[i], name=f"sum_l{k[0]}_{k[1]}")
            del self.ready[k]


def _layer_bwd(dx2, ada, p, sv, l, sc_, riders):
    sh1, sc1, gt1, sh2, sc2, gt2 = ada
    n = f"l{l}"

    def carried(key, fn):
        keys = riders.get(key)
        if not keys:
            return fn(None)
        outs, res = fn(sc_.comm(keys))
        sc_.landed(keys, res, key)
        return outs

    ddn, dup, dx1, red_d, red_f, dbup = carried("ffn_bwd", lambda cm: _ffn_bwd(
        dx2, sv["x1"], sv["dn"], sv["up"], p["conv_ffn"], p["conv_ffn_b"], p["w_down"], p["w_up"], gt2, p["ln2_g"],
        sc2, ts=256, name=f"{n}_ffn_bwd", comm=cm))
    g = {}
    sc_.add(l, "w_down", *_grad_matmul(sv["f"], ddn, ts=512, tn=512, name=f"{n}_dw_down"))
    sc_.add(l, "w_up", *carried("dw_up", lambda cm: _grad_matmul(
        sv["x1"], dup, ts=512, tn=1408, name=f"{n}_dw_up", mod=(sc2, sh2), by_chip=True, comm=cm)))
    g["ln2_g"], g["ln2_b"] = red_d[0], red_d[1]
    g["conv_ffn"], g["conv_ffn_b"], g["b_up"] = red_f[0:3], red_f[3], dbup[0]

    d_o, dxa, dz, dya, dyb, dyc, red_o = carried("mix_o_bwd", lambda cm: _mix_o_bwd(
        dx1, sv["x"], sv["o"], sv["z"], sv["ya"], sv["yb"], sv["yc"], p["w_o"], gt1, p["ln1_g"], ts=256,
        name=f"{n}_mix_o_bwd", comm=cm))
    sc_.add(l, "w_o", *_grad_matmul(sv["merged"], d_o, ts=512, tn=1024, name=f"{n}_dw_o"))
    g["ln1_g"], g["ln1_b"] = red_o[0], red_o[1]

    dz, red_a = carried("mix_a_bwd", lambda cm: _mix_a_bwd(dya, sv["z"], dz, p["conv_a"], p["w_a_out"], ts=256,
                                                           name=f"{n}_mix_a_bwd", comm=cm))
    sc_.add(l, "w_a_out", *_grad_matmul(sv["a"], dya, ts=512, tn=1024, name=f"{n}_dw_a_out"))
    g["conv_a"] = red_a[0:3]

    dz, red_b, dws, dbst = carried("mix_b_bwd", lambda cm: _mix_b_bwd(
        dyb, sv["z"], dz, p["ln_v_g"], p["ln_v_b"], p["w_spatial"], p["b_spatial_t"], p["w_b_out"], ts=256,
        name=f"{n}_mix_b_bwd", comm=cm))
    sc_.add(l, "w_b_out", *_grad_matmul(sv["sg"], dyb, ts=512, tn=1024, name=f"{n}_dw_b_out"))
    g["ln_v_g"], g["ln_v_b"], g["w_spatial"], g["b_spatial"] = red_b[0], red_b[1], dws, dbst.T

    dz, red_c, dwp = _mix_c_bwd(dyc, sv["z"], dz, p["w_pool"], p["pool_scale"], ts=256, name=f"{n}_mix_c_bwd")
    g["pool_scale"] = red_c[0]
    sc_.add(l, "w_pool", dwp, dwp.astype(BF16))

    dx, red_i, dbin = carried("in_proj_bwd", lambda cm: _in_proj_bwd(dz, p["w_in"], dxa, sv["x"], sc1, ts=256,
                                                                     name=f"{n}_in_proj_bwd", comm=cm))
    g["b_in"] = dbin[0]
    sc_.add(l, "w_in", *_grad_matmul(sv["x"], dz, ts=512, tn=1152, name=f"{n}_dw_in", mod=(sc1, sh1), by_chip=True))
    dada = jnp.stack([red_i[1], red_i[0], red_o[2], red_d[4], red_d[3], red_d[2]])
    return dx, g, dada


_LATE = ("w_a_out", "w_b_out", "w_pool", "w_o")
_FFN = ("w_up", "w_down")


def kernel(x, c, w_ada, b_ada, w_in, b_in, conv_a, w_a_out, ln_v_g, ln_v_b, w_spatial, b_spatial, w_b_out, w_pool, pool_scale, w_o, ln1_g, ln1_b, w_up, b_up, conv_ffn, conv_ffn_b, w_down, ln2_g, ln2_b, loss_target, m_w_ada, m_b_ada, m_w_in, m_b_in, m_conv_a, m_w_a_out, m_ln_v_g, m_ln_v_b, m_w_spatial, m_b_spatial, m_w_b_out, m_w_pool, m_pool_scale, m_w_o, m_ln1_g, m_ln1_b, m_w_up, m_b_up, m_conv_ffn, m_conv_ffn_b, m_w_down, m_ln2_g, m_ln2_b, v_w_ada, v_b_ada, v_w_in, v_b_in, v_conv_a, v_w_a_out, v_ln_v_g, v_ln_v_b, v_w_spatial, v_b_spatial, v_w_b_out, v_w_pool, v_pool_scale, v_w_o, v_ln1_g, v_ln1_b, v_w_up, v_b_up, v_conv_ffn, v_conv_ffn_b, v_w_down, v_ln2_g, v_ln2_b):
    args = locals()
    w = {k: args[k] for k in _WEIGHTS}
    m = {k: args["m_" + k] for k in _WEIGHTS}
    v = {k: args["v_" + k] for k in _WEIGHTS}
    d = D_MODEL
    mx, my, mc = _my_coords()
    chip = 2 * mx + my
    me = 4 * mx + 2 * my + mc

    small_shapes = [c.shape, conv_a.shape, conv_ffn.shape]
    small_all = _all_gather8(_pack_small([c, conv_a, conv_ffn]), name="gather_small")
    c_all, conv_a_st, conv_ffn_st = _unpack_small(small_all, small_shapes)
    c_all = c_all.reshape(N_DEV, d)
    conv_full = {"conv_a": jnp.concatenate([conv_a_st[2 * j] for j in range(N_CHIPS)], axis=-1),
                 "conv_ffn": jnp.concatenate([conv_ffn_st[2 * j] for j in range(N_CHIPS)], axis=-1)}

    halves = {}
    for k in _BIG:
        s3 = _shard3(w[k]).astype(BF16)
        halves[k] = s3.reshape(s3.shape[0], 2, s3.shape[1] // 2, s3.shape[2])
    gathered = [dict(zip(_BIG, _comm_call(_gather_comm([halves[k] for k in _BIG], 0), name="gather_l0"))), {}]

    n_ada = w_ada.shape[2]
    b_ada_mine = lax.dynamic_slice_in_dim(b_ada, chip * n_ada, n_ada, axis=1)
    ada_part = _ada_fwd(c_all, w_ada, b_ada_mine.reshape(DEPTH, 1, n_ada), name="ada_fwd")
    ada_all = _all_gather8(_pack_small([ada_part]), name="gather_ada")
    ada_st = _unpack_small(ada_all, [ada_part.shape])[0][0::2]
    ada_rows = jnp.concatenate([ada_st[j] for j in range(N_CHIPS)], axis=-1)
    ada_mine = lax.dynamic_index_in_dim(ada_rows, me, axis=1, keepdims=False)

    def layer_params(l):
        p = {k: _use_gathered(k, gathered[l][k]) for k in _BIG}
        for k in _SMALL_SHARDED:
            p[k] = conv_full[k][l]
        for k in ("b_in", "ln_v_g", "ln_v_b", "pool_scale", "ln1_g", "ln1_b", "b_up", "conv_ffn_b", "ln2_g", "ln2_b"):
            p[k] = w[k][l].reshape(1, -1)
        p["w_spatial"] = w_spatial[l]
        p["b_spatial_t"] = b_spatial[l].T
        return p

    riders_fwd = {"in_proj": ("w_in",), "mix_o": _LATE, "up_proj": _FFN}
    xs = x[0]
    saved, adas, params = [], [], []
    for l in range(DEPTH):
        ada = [ada_mine[l, k * d:(k + 1) * d].reshape(1, d) for k in range(6)]
        p = layer_params(l)
        comms = {key: _gather_comm([halves[k] for k in names], 1) for key, names in riders_fwd.items()} if l == 0 else {}
        xs, sv, got = _layer_fwd(xs, ada, p, l, comms)
        for key, res in got.items():
            gathered[1].update(zip(riders_fwd[key], res))
        saved.append(sv), adas.append(ada), params.append(p)
    dx, loss_blk = _loss_fwd(xs, loss_target[0], ts=512, name="loss")

    sc_ = _Scatter()
    grads, dadas = [None] * DEPTH, [None] * DEPTH
    dx, grads[1], dadas[1] = _layer_bwd(dx, adas[1], params[1], saved[1], 1, sc_, {})
    riders_bwd = {"ffn_bwd": [(1, "w_in")], "dw_up": [(1, k) for k in _FFN], "mix_o_bwd": [(1, k) for k in _LATE],
                  "mix_a_bwd": [(0, "w_down")], "mix_b_bwd": [(0, "w_up")], "in_proj_bwd": [(0, k) for k in _LATE]}
    dx, grads[0], dadas[0] = _layer_bwd(dx, adas[0], params[0], saved[0], 0, sc_, riders_bwd)
    tail = [(0, "w_in")]
    sc_.landed(tail, _comm_call(sc_.comm(tail), name="scatter_tail"), "tail")
    dada = jnp.stack(dadas).reshape(DEPTH, 6 * d)

    small_names = _SMALL_REPL + _SMALL_SHARDED
    small_g = [jnp.stack([grads[l][k] for l in range(DEPTH)]) for k in small_names] + [dada, loss_blk[0:1, 0:1]]
    small_g_shapes = [a.shape for a in small_g]
    sg_all, sg_sum = _all_gather8(_pack_small(small_g), name="reduce_small", with_sum=True)
    summed = _unpack_small(sg_sum, small_g_shapes)
    gsum = dict(zip(small_names, summed[:len(small_names)]))
    gsum["b_ada"], loss = summed[-2], summed[-1][0, 0]
    dada_all = _unpack_small(sg_all, small_g_shapes)[-2]
    for k in _SMALL_SHARDED:
        wd = gsum[k].shape[-1] // N_CHIPS
        gsum[k] = lax.dynamic_slice_in_dim(gsum[k], chip * wd, wd, axis=gsum[k].ndim - 1)

    dada_cols = lax.dynamic_slice_in_dim(dada_all, chip * n_ada, n_ada, axis=2)
    dada_cols = jnp.pad(jnp.swapaxes(dada_cols, 0, 1), ((0, 0), (0, N_DEV), (0, 0)))
    gsum["w_ada"] = _ada_bwd(jnp.pad(c_all, ((0, N_DEV), (0, 0))), dada_cols, name="ada_bwd")

    keys = [(l, k) for l in range(DEPTH) for k in _BIG]
    theirs = dict(zip(keys, _comm_call(_swap_comm([sc_.parts[k] for k in keys]), name="swap_grads")))
    out_g, out_d, out_m, out_v = {}, {}, {}, {}
    for k in _WEIGHTS:
        shp = w[k].shape
        if k in _BIG:
            parts = [(sc_.parts[(l, k)], theirs[(l, k)]) for l in range(DEPTH)]
            res = _adamw_sharded(_shard3(w[k]), _shard3(m[k]), _shard3(v[k]), parts, name=f"adamw_{k}")
        else:
            gk = gsum[k].reshape(shp)
            res = [gk] + list(_adamw(_as2d(w[k]), _as2d(gk), _as2d(m[k]), _as2d(v[k]), name=f"adamw_{k}"))
        out_g[k], out_d[k], out_m[k], out_v[k] = [r.reshape(shp) for r in res]

    return (loss, dx[None], *[out_g[k] for k in _WEIGHTS], *[out_d[k] for k in _WEIGHTS],
            *[out_m[k] for k in _WEIGHTS], *[out_v[k] for k in _WEIGHTS])
```

```python
import math
from typing import Callable, NamedTuple

import jax
import jax.numpy as jnp
from jax import lax
from jax.experimental import pallas as pl
from jax.experimental.pallas import tpu as pltpu

F32 = jnp.float32
BF16 = jnp.bfloat16

D_MODEL = 1024
D_Z = 9216
D_FF = 2816
N_GROUPS = 8
GBLK = 128
CHUNK = 64
POOL_WINDOWS = (2, 4, 8, 16)
POOL_GROUP = 256
POOL_HALO = 16
CONV_HALO = 8
DEPTH = 2
ALPHA = (2 * DEPTH) ** 0.25
LN_EPS = 1e-5
ADAM_LR, ADAM_B1, ADAM_B2, ADAM_EPS, ADAM_WD, ADAM_STEP = 0.001, 0.9, 0.999, 1e-08, 0.01, 10
N_CHIPS = 4
N_DEV = 8
FF_CHUNK = 704
MESH = pl.DeviceIdType.MESH
VMEM_LIMIT = 56 * 1024 * 1024
HBM = pl.BlockSpec(memory_space=pl.ANY)


def _dot(a, b):
    return jnp.dot(a, b, preferred_element_type=F32)


def _dot_nt(a, b):
    return lax.dot_general(a, b, (((1,), (1,)), ((), ())), preferred_element_type=F32)


def _dot_tn(a, b):
    return lax.dot_general(a, b, (((0,), (0,)), ((), ())), preferred_element_type=F32)


_GELU_C = math.sqrt(2.0 / math.pi)


def _gelu_and_grad(x):
    x2 = x * x
    t = jnp.tanh(_GELU_C * (x + 0.044715 * x * x2))
    g = 0.5 * x * (1.0 + t)
    dg = 0.5 * (1.0 + t) + 0.5 * x * (1.0 - t * t) * (_GELU_C * (1.0 + 3 * 0.044715 * x2))
    return g, dg


def _gelu(x):
    return 0.5 * x * (1.0 + jnp.tanh(_GELU_C * (x + 0.044715 * x * x * x)))


def _sigmoid(x):
    return 1.0 / (1.0 + jnp.exp(-x))


def _ln_fwd(r):
    mu = jnp.mean(r, axis=-1, keepdims=True)
    xc = r - mu
    var = jnp.mean(xc * xc, axis=-1, keepdims=True)
    rstd = lax.rsqrt(var + LN_EPS)
    return xc * rstd, rstd


def _ln_bwd(dy, g, xhat, rstd):
    dxh = dy * g
    m1 = jnp.mean(dxh, axis=-1, keepdims=True)
    m2 = jnp.mean(dxh * xhat, axis=-1, keepdims=True)
    return rstd * (dxh - m1 - xhat * m2)


def _rows_before(ext, k, halo):
    return pltpu.roll(ext, k, 0)[halo:]


def _rows_after(ext, k, n):
    return pltpu.roll(ext, ext.shape[0] - k, 0)[:n]


def _colsum(v):
    return jnp.sum(v, axis=0, keepdims=True)


def _spatial_mask():
    i = lax.broadcasted_iota(jnp.int32, (GBLK, GBLK), 0)
    j = lax.broadcasted_iota(jnp.int32, (GBLK, GBLK), 1)
    return (j // CHUNK) <= (i // CHUNK)


def _const(shape):
    n = len(shape)
    return pl.BlockSpec(shape, lambda *_: (0,) * n)


def _resident(shape):
    n = len(shape)
    return pl.BlockSpec(shape, lambda *_: (0,) * n, pipeline_mode=pl.Buffered(1))


def _tile(ts, s):
    return min(ts, s)


class _Comm(NamedTuple):
    srcs: tuple
    dsts: tuple
    n_remote: int
    n_local: int
    build: Callable


def _my_coords():
    return lax.axis_index("x"), lax.axis_index("y"), lax.axis_index("c")


def _chip_peer(k):
    mx, my, mc = _my_coords()
    return (mx ^ ((k >> 1) & 1), my ^ (k & 1), mc)


def _sem_scratch(comm):
    return [pltpu.SemaphoreType.DMA((max(comm.n_remote, 1),)), pltpu.SemaphoreType.DMA((max(comm.n_remote, 1),)),
            pltpu.SemaphoreType.DMA((max(comm.n_local, 1),))]


def _run(body, *, name, grid, in_specs, out_specs, out_shape, args, scratch_shapes=(), comm=None, aliases=None):
    sem = ("arbitrary",) * len(grid)
    cparams = pltpu.CompilerParams(dimension_semantics=sem, vmem_limit_bytes=VMEM_LIMIT)
    kwargs = {} if aliases is None else {"input_output_aliases": aliases}
    if comm is None:
        return pl.pallas_call(body, name=name, grid=grid, in_specs=in_specs, out_specs=out_specs, out_shape=out_shape,
                              scratch_shapes=list(scratch_shapes), compiler_params=cparams, **kwargs)(*args)
    n_in, n_cs, n_out, n_cd, n_scr = len(in_specs), len(comm.srcs), len(out_specs), len(comm.dsts), len(scratch_shapes)
    total = math.prod(grid)
    mid_step = min(total - 1, int(total * 0.7))

    def wrapped(*refs):
        ins, refs = refs[:n_in], refs[n_in:]
        csrc, refs = refs[:n_cs], refs[n_cs:]
        outs, refs = refs[:n_out], refs[n_out:]
        cdst, refs = refs[:n_cd], refs[n_cd:]
        scr, sems = refs[:n_scr], refs[n_scr:]
        step = pl.program_id(0)
        for ax in range(1, len(grid)):
            step = step * grid[ax] + pl.program_id(ax)
        first, mid, last = comm.build(csrc, cdst, *sems, 0, 0)
        pl.when(step == 0)(first)
        if mid is not None:
            pl.when(step == mid_step)(mid)
        body(*ins, *outs, *scr)
        pl.when(step == total - 1)(last)

    res = pl.pallas_call(
        wrapped, name=name, grid=grid, in_specs=list(in_specs) + [HBM] * n_cs, out_specs=list(out_specs) + [HBM] * n_cd,
        out_shape=list(out_shape) + list(comm.dsts), scratch_shapes=list(scratch_shapes) + _sem_scratch(comm),
        compiler_params=cparams, **kwargs)(*args, *comm.srcs)
    return res[:n_out], res[n_out:]


def _comm_call(comm, *, name):
    def body(*refs):
        n_cs, n_cd = len(comm.srcs), len(comm.dsts)
        first, mid, last = comm.build(refs[:n_cs], refs[n_cs:n_cs + n_cd], *refs[n_cs + n_cd:], 0, 0)
        first()
        if mid is not None:
            mid()
        last()

    return pl.pallas_call(body, name=name, in_specs=[HBM] * len(comm.srcs), out_specs=[HBM] * len(comm.dsts),
                          out_shape=list(comm.dsts), scratch_shapes=_sem_scratch(comm))(*comm.srcs)


def _gather_comm(shards, layer):
    dsts = tuple(jax.ShapeDtypeStruct((N_CHIPS,) + s.shape[1:], s.dtype) for s in shards)
    nw = len(shards)

    def build(srcs, outs, send_sems, recv_sems, local_sems, r0, l0):
        mx, my, mc = _my_coords()
        me = 2 * mx + my
        sibling = (mx, my, 1 - mc)

        def rdma(src, dst, idx, peer):
            return pltpu.make_async_remote_copy(src_ref=src, dst_ref=dst, send_sem=send_sems.at[r0 + idx],
                                                recv_sem=recv_sems.at[r0 + idx], device_id=peer, device_id_type=MESH)

        def ici(w, k, slot):
            return rdma(srcs[w].at[layer, mc], outs[w].at[slot, mc], 6 * w + k - 1, _chip_peer(k))

        def fwd(w, k, half):
            return rdma(outs[w].at[me ^ k, mc], outs[w].at[me ^ k, half], 6 * w + 2 + k, sibling)

        def own(w):
            return pltpu.make_async_copy(srcs[w].at[layer], outs[w].at[me], local_sems.at[l0 + w])

        def first():
            for w in range(nw):
                own(w).start()
                for k in range(1, N_CHIPS):
                    ici(w, k, me).start()

        def mid():
            for w in range(nw):
                for k in range(1, N_CHIPS):
                    ici(w, k, me ^ k).wait_recv()
                    fwd(w, k, mc).start()

        def last():
            for w in range(nw):
                for k in range(1, N_CHIPS):
                    fwd(w, k, 1 - mc).wait_recv()
                    ici(w, k, me).wait_send()
                    fwd(w, k, mc).wait_send()
                own(w).wait()

        return first, mid, last

    return _Comm(tuple(shards), dsts, 6 * nw, nw, build)


def _scatter_comm(g_bf16, g_f32):
    nw = len(g_bf16)
    dsts = tuple(jax.ShapeDtypeStruct((N_CHIPS - 1,) + g.shape[1:], BF16) for g in g_bf16) + tuple(
        jax.ShapeDtypeStruct(g.shape[1:], F32) for g in g_f32)

    def build(srcs, outs, send_sems, recv_sems, local_sems, r0, l0):
        mx, my, _ = _my_coords()
        me = 2 * mx + my

        def copies():
            remote = [pltpu.make_async_remote_copy(
                src_ref=srcs[w].at[me ^ k], dst_ref=outs[w].at[k - 1], send_sem=send_sems.at[r0 + 3 * w + k - 1],
                recv_sem=recv_sems.at[r0 + 3 * w + k - 1], device_id=_chip_peer(k), device_id_type=MESH)
                for w in range(nw) for k in range(1, N_CHIPS)]
            local = [pltpu.make_async_copy(srcs[nw + w].at[me], outs[nw + w], local_sems.at[l0 + w]) for w in range(nw)]
            return remote, local

        def first():
            remote, local = copies()
            for cp in remote + local:
                cp.start()

        def last():
            remote, local = copies()
            for cp in remote:
                cp.wait_recv()
            for cp in remote:
                cp.wait_send()
            for cp in local:
                cp.wait()

        return first, None, last

    return _Comm(tuple(g_bf16) + tuple(g_f32), dsts, 3 * nw, nw, build)


def _swap_comm(parts):
    dsts = tuple(jax.ShapeDtypeStruct(p.shape, p.dtype) for p in parts)

    def build(srcs, outs, send_sems, recv_sems, local_sems, r0, l0):
        mx, my, mc = _my_coords()

        def copies():
            return [pltpu.make_async_remote_copy(
                src_ref=srcs[w], dst_ref=outs[w], send_sem=send_sems.at[r0 + w], recv_sem=recv_sems.at[r0 + w],
                device_id=(mx, my, 1 - mc), device_id_type=MESH) for w in range(len(parts))]

        def first():
            for cp in copies():
                cp.start()

        def last():
            for cp in copies():
                cp.wait_recv()
            for cp in copies():
                cp.wait_send()

        return first, None, last

    return _Comm(tuple(parts), dsts, len(parts), 0, build)


def _merge(comms):
    comms = list(comms)
    if len(comms) == 1:
        return comms[0]

    def build(srcs, outs, send_sems, recv_sems, local_sems, r0, l0):
        phases, s0, d0 = [], 0, 0
        for cm in comms:
            phases.append(cm.build(srcs[s0:s0 + len(cm.srcs)], outs[d0:d0 + len(cm.dsts)], send_sems, recv_sems,
                                   local_sems, r0, l0))
            s0, d0, r0, l0 = s0 + len(cm.srcs), d0 + len(cm.dsts), r0 + cm.n_remote, l0 + cm.n_local

        def run(idx):
            fns = [ph[idx] for ph in phases if ph[idx] is not None]
            if not fns:
                return None

            def go():
                for fn in fns:
                    fn()
            return go

        return run(0), run(1), run(2)

    return _Comm(sum((cm.srcs for cm in comms), ()), sum((cm.dsts for cm in comms), ()),
                 sum(cm.n_remote for cm in comms), sum(cm.n_local for cm in comms), build)


def _split(comms, res):
    out, d0 = [], 0
    for cm in comms:
        out.append(list(res[d0:d0 + len(cm.dsts)]))
        d0 += len(cm.dsts)
    return out


def _all_reduce_small(x, *, name):
    r, lanes = x.shape

    def body(x_ref, out_ref, sib_ref, slots_ref, send_sems, recv_sems):
        mx, my, mc = _my_coords()
        me = 2 * mx + my
        swap = pltpu.make_async_remote_copy(src_ref=x_ref, dst_ref=sib_ref, send_sem=send_sems.at[0],
                                            recv_sem=recv_sems.at[0], device_id=(mx, my, 1 - mc), device_id_type=MESH)
        swap.start()
        swap.wait_recv()
        swap.wait_send()
        slots_ref[me] = x_ref[...] + sib_ref[...]

        def copy(k, slot):
            return pltpu.make_async_remote_copy(
                src_ref=slots_ref.at[me], dst_ref=slots_ref.at[slot], send_sem=send_sems.at[k], recv_sem=recv_sems.at[k],
                device_id=_chip_peer(k), device_id_type=MESH)

        sends = [copy(k, me) for k in range(1, N_CHIPS)]
        for cp in sends:
            cp.start()
        for k in range(1, N_CHIPS):
            copy(k, me ^ k).wait_recv()
        for cp in sends:
            cp.wait_send()
        acc = slots_ref[0]
        for j in range(1, N_CHIPS):
            acc = acc + slots_ref[j]
        out_ref[...] = acc

    vmem = pl.BlockSpec(memory_space=pltpu.VMEM)
    return pl.pallas_call(
        body, name=name, in_specs=[vmem], out_specs=vmem, out_shape=jax.ShapeDtypeStruct((r, lanes), F32),
        scratch_shapes=[pltpu.VMEM((r, lanes), F32), pltpu.VMEM((N_CHIPS, r, lanes), F32),
                        pltpu.SemaphoreType.DMA((N_CHIPS,)), pltpu.SemaphoreType.DMA((N_CHIPS,))],
        compiler_params=pltpu.CompilerParams(vmem_limit_bytes=VMEM_LIMIT),
    )(x)


def _all_gather8(x, *, name, with_sum=False):
    r, lanes = x.shape

    def body(x_ref, out_ref, *rest):
        if with_sum:
            sum_ref, send_sems, recv_sems, local_sem = rest
        else:
            send_sems, recv_sems, local_sem = rest
        mx, my, mc = _my_coords()
        me = 4 * mx + 2 * my + mc

        def peer(k):
            return (mx ^ ((k >> 2) & 1), my ^ ((k >> 1) & 1), mc ^ (k & 1))

        def copy(k, slot):
            return pltpu.make_async_remote_copy(
                src_ref=x_ref, dst_ref=out_ref.at[slot], send_sem=send_sems.at[k - 1], recv_sem=recv_sems.at[k - 1],
                device_id=peer(k), device_id_type=MESH)

        mine = pltpu.make_async_copy(x_ref, out_ref.at[me], local_sem)
        mine.start()
        sends = [copy(k, me) for k in range(1, N_DEV)]
        for cp in sends:
            cp.start()
        for k in range(1, N_DEV):
            copy(k, me ^ k).wait_recv()
        for cp in sends:
            cp.wait_send()
        mine.wait()
        if with_sum:
            acc = out_ref[0]
            for k in range(1, N_DEV):
                acc = acc + out_ref[k]
            sum_ref[...] = acc

    vmem = pl.BlockSpec(memory_space=pltpu.VMEM)
    out_shape = [jax.ShapeDtypeStruct((N_DEV, r, lanes), F32)]
    if with_sum:
        out_shape.append(jax.ShapeDtypeStruct((r, lanes), F32))
    res = pl.pallas_call(
        body, name=name, in_specs=[vmem], out_specs=[vmem] * len(out_shape), out_shape=out_shape,
        scratch_shapes=[pltpu.SemaphoreType.DMA((N_DEV - 1,)), pltpu.SemaphoreType.DMA((N_DEV - 1,)),
                        pltpu.SemaphoreType.DMA],
        compiler_params=pltpu.CompilerParams(vmem_limit_bytes=VMEM_LIMIT),
    )(x)
    return res if with_sum else res[0]


def _mod_matmul(x, sc, sh, w4, b, *, ts, tn, name, comm=None):
    s, d = x.shape
    wd = w4.shape[2]
    n = N_CHIPS * wd
    per = wd // tn
    ts = _tile(ts, s)

    def body(x_ref, sc_ref, sh_ref, w_ref, b_ref, o_ref, h_scr):
        @pl.when(pl.program_id(1) == 0)
        def _():
            h_scr[...] = (x_ref[...] * (1.0 + sc_ref[...]) + sh_ref[...]).astype(BF16)
        o_ref[...] = _dot(h_scr[...], w_ref[0]) + b_ref[...]

    return _run(
        body, name=name, grid=(s // ts, n // tn),
        in_specs=[pl.BlockSpec((ts, d), lambda i, j: (i, 0)), _const((1, d)), _const((1, d)),
                  pl.BlockSpec((1, d, tn), lambda i, j: (j // per, 0, j % per)),
                  pl.BlockSpec((1, tn), lambda i, j: (0, j))],
        out_specs=[pl.BlockSpec((ts, tn), lambda i, j: (i, j))],
        out_shape=[jax.ShapeDtypeStruct((s, n), F32)],
        scratch_shapes=[pltpu.VMEM((ts, d), BF16)],
        args=(x, sc, sh, w4, b), comm=comm)


def _conv3(q, ext, cw):
    return cw[2:3] * q + cw[1:2] * _rows_before(ext, 1, CONV_HALO) + cw[0:1] * _rows_before(ext, 2, CONV_HALO)


def _mix_a_fwd(z, cw, w_out, *, ts, name):
    s = z.shape[0]
    d = D_MODEL
    ts = _tile(ts, s)

    def body(zb_ref, zc_ref, zx_ref, cw_ref, w_ref, a_ref, y_ref, carry):
        @pl.when(pl.program_id(0) == 0)
        def _():
            carry[...] = jnp.zeros_like(carry)
        q = zc_ref[...] * zx_ref[...]
        ext = jnp.concatenate([carry[...], q], axis=0)
        a = (zb_ref[...] * _conv3(q, ext, cw_ref[...])).astype(BF16)
        carry[...] = q[ts - CONV_HALO:]
        a_ref[...] = a
        y_ref[...] = _dot(a, w_ref[...])

    zspec = lambda k: pl.BlockSpec((ts, d), lambda i, k=k: (i, k))
    return _run(
        body, name=name, grid=(s // ts,),
        in_specs=[zspec(0), zspec(1), zspec(2), _const((3, d)), _const((d, d))],
        out_specs=[pl.BlockSpec((ts, d), lambda i: (i, 0))] * 2,
        out_shape=[jax.ShapeDtypeStruct((s, d), BF16), jax.ShapeDtypeStruct((s, d), F32)],
        scratch_shapes=[pltpu.VMEM((CONV_HALO, d), F32)],
        args=(z, z, z, cw, w_out))


def _spatial_mix(vn_b, ws_ref, bst_ref, mixed_scr, ts):
    nblk = ts // GBLK
    mask = _spatial_mask()
    for g in range(N_GROUPS):
        cols = slice(g * GBLK, (g + 1) * GBLK)
        wm = jnp.where(mask, ws_ref[g], 0.0).astype(BF16)
        cat = jnp.concatenate([vn_b[n * GBLK:(n + 1) * GBLK, cols] for n in range(nblk)], axis=1)
        res = _dot(wm, cat) + bst_ref[:, g:g + 1]
        for n in range(nblk):
            mixed_scr[n * GBLK:(n + 1) * GBLK, cols] = res[:, n * GBLK:(n + 1) * GBLK]


def _mix_b_fwd(z, ln_g, ln_b, ws, bst, w_out, *, ts, name, comm=None):
    s = z.shape[0]
    d = D_MODEL
    ts = _tile(ts, s)

    def body(zu_ref, zv_ref, g_ref, b_ref, ws_ref, bst_ref, w_ref, sg_ref, y_ref, mixed_scr):
        xhat, _ = _ln_fwd(_gelu(zv_ref[...]))
        vn = (xhat * g_ref[...] + b_ref[...]).astype(BF16)
        _spatial_mix(vn, ws_ref, bst_ref, mixed_scr, ts)
        sg = (_gelu(zu_ref[...]) * mixed_scr[...]).astype(BF16)
        sg_ref[...] = sg
        y_ref[...] = _dot(sg, w_ref[...])

    zspec = lambda k: pl.BlockSpec((ts, d), lambda i, k=k: (i, k))
    return _run(
        body, name=name, grid=(s // ts,),
        in_specs=[zspec(3), zspec(4), _const((1, d)), _const((1, d)), _const((N_GROUPS, GBLK, GBLK)),
                  _const((GBLK, N_GROUPS)), _const((d, d))],
        out_specs=[pl.BlockSpec((ts, d), lambda i: (i, 0))] * 2,
        out_shape=[jax.ShapeDtypeStruct((s, d), BF16), jax.ShapeDtypeStruct((s, d), F32)],
        scratch_shapes=[pltpu.VMEM((ts, d), F32)],
        args=(z, z, ln_g, ln_b, ws, bst, w_out), comm=comm)


def _pool_denoms(tile_idx, ts):
    t1 = (tile_idx * ts + 1 + lax.broadcasted_iota(jnp.int32, (ts, 1), 0)).astype(F32)
    return [jnp.minimum(t1, float(w)) for w in POOL_WINDOWS]


def _pool_diff(p, ext, denoms, k):
    cols = slice(k * POOL_GROUP, (k + 1) * POOL_GROUP)
    acc = ext[:, cols]
    step = 1
    while step < POOL_WINDOWS[k]:
        acc = acc + pltpu.roll(acc, step, 0)
        step *= 2
    return acc[POOL_HALO:] / denoms[k] - p[:, cols]


def _mix_c_fwd(z, w_pool, scale, *, ts, name):
    s = z.shape[0]
    d = D_MODEL
    ts = _tile(ts, s)

    def body(zp_ref, w_ref, sc_ref, d_ref, y_ref, carry):
        i = pl.program_id(0)

        @pl.when(i == 0)
        def _():
            carry[...] = jnp.zeros_like(carry)
        p = zp_ref[...]
        ext = jnp.concatenate([carry[...], p], axis=0)
        carry[...] = p[ts - POOL_HALO:]
        denoms = _pool_denoms(i, ts)
        for k in range(len(POOL_WINDOWS)):
            cols = slice(k * POOL_GROUP, (k + 1) * POOL_GROUP)
            dk = _pool_diff(p, ext, denoms, k).astype(BF16)
            d_ref[:, cols] = dk
            y_ref[:, cols] = _dot(dk, w_ref[k]) * sc_ref[:, cols]

    return _run(
        body, name=name, grid=(s // ts,),
        in_specs=[pl.BlockSpec((ts, d), lambda i: (i, 5)), _const((4, POOL_GROUP, POOL_GROUP)), _const((1, d))],
        out_specs=[pl.BlockSpec((ts, d), lambda i: (i, 0))] * 2,
        out_shape=[jax.ShapeDtypeStruct((s, d), BF16), jax.ShapeDtypeStruct((s, d), F32)],
        scratch_shapes=[pltpu.VMEM((POOL_HALO, d), F32)],
        args=(z, w_pool, scale))


def _mix_o_fwd(x, z, ya, yb, yc, w_o, gt, ln_g, ln_b, *, ts, name, comm=None):
    s, d = x.shape
    ts = _tile(ts, s)

    def body(x_ref, ga_ref, gb_ref, gc_ref, ya_ref, yb_ref, yc_ref, w_ref, gt_ref, g_ref, b_ref,
             m_ref, o_ref, x1_ref):
        merged = (_sigmoid(ga_ref[...]) * ya_ref[...] + _sigmoid(gb_ref[...]) * yb_ref[...]
                  + _sigmoid(gc_ref[...]) * yc_ref[...]).astype(BF16)
        m_ref[...] = merged
        o = _dot(merged, w_ref[...])
        o_ref[...] = o
        xhat, _ = _ln_fwd(ALPHA * x_ref[...] + gt_ref[...] * o)
        x1_ref[...] = xhat * g_ref[...] + b_ref[...]

    row = pl.BlockSpec((ts, d), lambda i: (i, 0))
    zspec = lambda k: pl.BlockSpec((ts, d), lambda i, k=k: (i, k))
    return _run(
        body, name=name, grid=(s // ts,),
        in_specs=[row, zspec(6), zspec(7), zspec(8), row, row, row, _const((d, d)),
                  _const((1, d)), _const((1, d)), _const((1, d))],
        out_specs=[row] * 3,
        out_shape=[jax.ShapeDtypeStruct((s, d), BF16), jax.ShapeDtypeStruct((s, d), F32),
                   jax.ShapeDtypeStruct((s, d), F32)],
        args=(x, z, z, z, ya, yb, yc, w_o, gt, ln_g, ln_b), comm=comm)


def _ffn_fwd(up, x1, cw, cb, w_down, gt, ln_g, ln_b, *, ts, name, comm=None):
    s, d = x1.shape
    ts = _tile(ts, s)

    def body(up_ref, x1_ref, cw_ref, cb_ref, w_ref, gt_ref, g_ref, b_ref, f_ref, dn_ref, x2_ref, carry):
        @pl.when(pl.program_id(0) == 0)
        def _():
            carry[...] = jnp.zeros_like(carry)
        for c in range(D_FF // FF_CHUNK):
            ca = slice(c * FF_CHUNK, (c + 1) * FF_CHUNK)
            cg = slice(D_FF + c * FF_CHUNK, D_FF + (c + 1) * FF_CHUNK)
            ua = up_ref[:, ca]
            ext = jnp.concatenate([carry[:, ca], ua], axis=0)
            carry[:, ca] = ua[ts - CONV_HALO:]
            cf = _conv3(ua, ext, cw_ref[:, ca]) + cb_ref[:, ca]
            f_ref[:, ca] = (_gelu(cf) * up_ref[:, cg]).astype(BF16)
        dn = _dot(f_ref[...], w_ref[...])
        dn_ref[...] = dn
        xhat, _ = _ln_fwd(ALPHA * x1_ref[...] + gt_ref[...] * dn)
        x2_ref[...] = xhat * g_ref[...] + b_ref[...]

    row = pl.BlockSpec((ts, d), lambda i: (i, 0))
    return _run(
        body, name=name, grid=(s // ts,),
        in_specs=[pl.BlockSpec((ts, 2 * D_FF), lambda i: (i, 0)), row, _const((3, D_FF)), _const((1, D_FF)),
                  _resident((D_FF, d)), _const((1, d)), _const((1, d)), _const((1, d))],
        out_specs=[pl.BlockSpec((ts, D_FF), lambda i: (i, 0)), row, row],
        out_shape=[jax.ShapeDtypeStruct((s, D_FF), BF16), jax.ShapeDtypeStruct((s, d), F32),
                   jax.ShapeDtypeStruct((s, d), F32)],
        scratch_shapes=[pltpu.VMEM((CONV_HALO, D_FF), F32)],
        args=(up, x1, cw, cb, w_down, gt, ln_g, ln_b), comm=comm)


def _loss_fwd(y, tgt, *, ts, name):
    s, d = y.shape
    ts = _tile(ts, s)

    def body(y_ref, t_ref, dy_ref, l_ref):
        @pl.when(pl.program_id(0) == 0)
        def _():
            l_ref[...] = jnp.zeros_like(l_ref)
        e = y_ref[...] - t_ref[...]
        dy_ref[...] = e / float(d)
        l_ref[...] += 0.5 * jnp.sum(jnp.mean(e * e, axis=-1, keepdims=True), axis=0, keepdims=True)

    row = pl.BlockSpec((ts, d), lambda i: (i, 0))
    return _run(body, name=name, grid=(s // ts,), in_specs=[row, row], out_specs=[row, _const((8, 128))],
                out_shape=[jax.ShapeDtypeStruct((s, d), F32), jax.ShapeDtypeStruct((8, 128), F32)], args=(y, tgt))


def _rev(n_tiles):
    return lambda i: n_tiles - 1 - i


def _halo_spec(ts, n_tiles, halo, width, col):
    per = ts // halo
    return pl.BlockSpec((halo, width), lambda i: (jnp.maximum((n_tiles - 1 - i) * per - 1, 0), col))


def _ffn_bwd(dx2, x1, dn, up, cw, cb, w_down, w_up4, gt, ln_g, sc, *, ts, name, comm=None):
    s, d = x1.shape
    ts = _tile(ts, s)
    nt = s // ts
    rev = _rev(nt)
    wd = w_up4.shape[2]

    def w_up_cols(wu_ref, start):
        return wu_ref[start // wd, :, start % wd:start % wd + FF_CHUNK]

    def body(dx2_ref, x1_ref, dn_ref, up_ref, halo_ref, cw_ref, cb_ref, wd_ref, wu_ref, gt_ref, g_ref, sc_ref,
             ddn_ref, dup_ref, dx1_ref, redd_ref, redf_ref, dbup_ref, carry):
        i = pl.program_id(0)

        @pl.when(i == 0)
        def _():
            carry[...] = jnp.zeros_like(carry)
            redd_ref[...] = jnp.zeros_like(redd_ref)
            redf_ref[...] = jnp.zeros_like(redf_ref)
            dbup_ref[...] = jnp.zeros_like(dbup_ref)
        first_tile = i == nt - 1
        x1v, dnv, dyv = x1_ref[...], dn_ref[...], dx2_ref[...]
        xhat, rstd = _ln_fwd(ALPHA * x1v + gt_ref[...] * dnv)
        dr = _ln_bwd(dyv, g_ref[...], xhat, rstd)
        redd_ref[0:1, :] += _colsum(dyv * xhat)
        redd_ref[1:2, :] += _colsum(dyv)
        redd_ref[2:3, :] += _colsum(dr * dnv)
        ddn = (gt_ref[...] * dr).astype(BF16)
        ddn_ref[...] = ddn
        dh = jnp.zeros((ts, d), F32)
        for c in range(D_FF // FF_CHUNK):
            ca = slice(c * FF_CHUNK, (c + 1) * FF_CHUNK)
            cg = slice(D_FF + c * FF_CHUNK, D_FF + (c + 1) * FF_CHUNK)
            df = _dot_nt(ddn, wd_ref[ca, :])
            ua, ug = up_ref[:, ca], up_ref[:, cg]
            halo = jnp.where(first_tile, 0.0, halo_ref[:, ca])
            ext = jnp.concatenate([halo, ua], axis=0)
            u1, u2 = _rows_before(ext, 1, CONV_HALO), _rows_before(ext, 2, CONV_HALO)
            cwc = cw_ref[:, ca]
            gl, dgl = _gelu_and_grad(cwc[2:3] * ua + cwc[1:2] * u1 + cwc[0:1] * u2 + cb_ref[:, ca])
            dug = df * gl
            dcf = df * ug * dgl
            redf_ref[0:1, ca] += _colsum(dcf * u2)
            redf_ref[1:2, ca] += _colsum(dcf * u1)
            redf_ref[2:3, ca] += _colsum(dcf * ua)
            redf_ref[3:4, ca] += _colsum(dcf)
            extd = jnp.concatenate([dcf, carry[:, ca]], axis=0)
            carry[:, ca] = dcf[:CONV_HALO]
            dua = cwc[2:3] * dcf + cwc[1:2] * _rows_after(extd, 1, ts) + cwc[0:1] * _rows_after(extd, 2, ts)
            dbup_ref[0:1, ca] += _colsum(dua)
            dbup_ref[0:1, cg] += _colsum(dug)
            dua_b, dug_b = dua.astype(BF16), dug.astype(BF16)
            dup_ref[:, ca] = dua_b
            dup_ref[:, cg] = dug_b
            dh = dh + _dot_nt(dua_b, w_up_cols(wu_ref, c * FF_CHUNK)) + _dot_nt(dug_b, w_up_cols(wu_ref, D_FF + c * FF_CHUNK))
        dx1_ref[...] = ALPHA * dr + dh * (1.0 + sc_ref[...])
        redd_ref[3:4, :] += _colsum(dh * x1v)
        redd_ref[4:5, :] += _colsum(dh)

    row = pl.BlockSpec((ts, d), lambda i: (rev(i), 0))
    return _run(
        body, name=name, grid=(nt,),
        in_specs=[row, row, row, pl.BlockSpec((ts, 2 * D_FF), lambda i: (rev(i), 0)),
                  _halo_spec(ts, nt, CONV_HALO, D_FF, 0), _const((3, D_FF)), _const((1, D_FF)),
                  _resident((D_FF, d)), _resident((N_CHIPS, d, wd)), _const((1, d)), _const((1, d)), _const((1, d))],
        out_specs=[row, pl.BlockSpec((ts, 2 * D_FF), lambda i: (rev(i), 0)), row,
                   _const((8, d)), _const((8, D_FF)), _const((8, 2 * D_FF))],
        out_shape=[jax.ShapeDtypeStruct((s, d), BF16), jax.ShapeDtypeStruct((s, 2 * D_FF), BF16),
                   jax.ShapeDtypeStruct((s, d), F32), jax.ShapeDtypeStruct((8, d), F32),
                   jax.ShapeDtypeStruct((8, D_FF), F32), jax.ShapeDtypeStruct((8, 2 * D_FF), F32)],
        scratch_shapes=[pltpu.VMEM((CONV_HALO, D_FF), F32)],
        args=(dx2, x1, dn, up, up, cw, cb, w_down, w_up4, gt, ln_g, sc), comm=comm)


def _grad_matmul(xa, dy, *, ts, tn, name, mod=None, by_chip=False, comm=None):
    s, k = xa.shape
    n = dy.shape[1]
    ts = _tile(ts, s)
    nt = s // ts

    def body(*refs):
        if mod is None:
            xa_ref, dy_ref, o_ref, ob_ref = refs
            a = xa_ref[...]
        else:
            xa_ref, sc_ref, sh_ref, dy_ref, o_ref, ob_ref = refs
            a = (xa_ref[...] * (1.0 + sc_ref[...]) + sh_ref[...]).astype(BF16)
        t = pl.program_id(1)

        @pl.when(t == 0)
        def _():
            o_ref[...] = jnp.zeros_like(o_ref)
        o_ref[...] += _dot_tn(a, dy_ref[...]).reshape(o_ref.shape)

        @pl.when(t == nt - 1)
        def _():
            ob_ref[...] = o_ref[...].astype(BF16)

    xspec = pl.BlockSpec((ts, k), lambda j, t: (t, 0))
    dspec = pl.BlockSpec((ts, tn), lambda j, t: (t, j))
    in_specs = [xspec, dspec] if mod is None else [xspec, _const((1, k)), _const((1, k)), dspec]
    args = (xa, dy) if mod is None else (xa, mod[0], mod[1], dy)
    if by_chip:
        per = n // N_CHIPS // tn
        ospec = pl.BlockSpec((1, k, tn), lambda j, t: (j // per, 0, j % per))
        shape = (N_CHIPS, k, n // N_CHIPS)
    else:
        ospec = pl.BlockSpec((k, tn), lambda j, t: (0, j))
        shape = (k, n)
    return _run(body, name=name, grid=(n // tn, nt), in_specs=in_specs, out_specs=[ospec, ospec],
                out_shape=[jax.ShapeDtypeStruct(shape, F32), jax.ShapeDtypeStruct(shape, BF16)], args=args, comm=comm)


def _mix_o_bwd(dx1, x, o, z, ya, yb, yc, w_o, gt, ln_g, *, ts, name, comm=None):
    s, d = x.shape
    ts = _tile(ts, s)

    def body(dx1_ref, x_ref, o_ref, ga_ref, gb_ref, gc_ref, ya_ref, yb_ref, yc_ref, w_ref, gt_ref, g_ref,
             do_ref, dxa_ref, dzg_ref, dya_ref, dyb_ref, dyc_ref, red_ref):
        @pl.when(pl.program_id(0) == 0)
        def _():
            red_ref[...] = jnp.zeros_like(red_ref)
        dyv, ov = dx1_ref[...], o_ref[...]
        xhat, rstd = _ln_fwd(ALPHA * x_ref[...] + gt_ref[...] * ov)
        dr = _ln_bwd(dyv, g_ref[...], xhat, rstd)
        red_ref[0:1, :] += _colsum(dyv * xhat)
        red_ref[1:2, :] += _colsum(dyv)
        red_ref[2:3, :] += _colsum(dr * ov)
        dxa_ref[...] = ALPHA * dr
        d_o = (gt_ref[...] * dr).astype(BF16)
        do_ref[...] = d_o
        dm = _dot_nt(d_o, w_ref[...])
        for k, (zg_ref, y_ref, dy_ref) in enumerate(((ga_ref, ya_ref, dya_ref), (gb_ref, yb_ref, dyb_ref),
                                                     (gc_ref, yc_ref, dyc_ref))):
            g = _sigmoid(zg_ref[...])
            dzg_ref[:, k * d:(k + 1) * d] = (dm * y_ref[...] * g * (1.0 - g)).astype(BF16)
            dy_ref[...] = (dm * g).astype(BF16)

    row = pl.BlockSpec((ts, d), lambda i: (i, 0))
    zspec = lambda k: pl.BlockSpec((ts, d), lambda i, k=k: (i, k))
    bf = jax.ShapeDtypeStruct((s, d), BF16)
    return _run(
        body, name=name, grid=(s // ts,),
        in_specs=[row, row, row, zspec(6), zspec(7), zspec(8), row, row, row, _const((d, d)),
                  _const((1, d)), _const((1, d))],
        out_specs=[row, row, pl.BlockSpec((ts, 3 * d), lambda i: (i, 2)), row, row, row, _const((8, d))],
        out_shape=[bf, jax.ShapeDtypeStruct((s, d), F32), jax.ShapeDtypeStruct((s, D_Z), BF16), bf, bf, bf,
                   jax.ShapeDtypeStruct((8, d), F32)],
        args=(dx1, x, o, z, z, z, ya, yb, yc, w_o, gt, ln_g), comm=comm)


def _mix_a_bwd(dya, z, dz, cw, w_out, *, ts, name, comm=None):
    s = z.shape[0]
    d = D_MODEL
    ts = _tile(ts, s)
    nt = s // ts
    rev = _rev(nt)

    def body(dya_ref, zb_ref, zc_ref, zx_ref, hc_ref, hx_ref, cw_ref, w_ref, dz_in, dz_ref, red_ref, carry):
        i = pl.program_id(0)

        @pl.when(i == 0)
        def _():
            carry[...] = jnp.zeros_like(carry)
            red_ref[...] = jnp.zeros_like(red_ref)
        zb, zc, zx = zb_ref[...], zc_ref[...], zx_ref[...]
        q = zc * zx
        halo = jnp.where(i == nt - 1, 0.0, hc_ref[...] * hx_ref[...])
        ext = jnp.concatenate([halo, q], axis=0)
        q1, q2 = _rows_before(ext, 1, CONV_HALO), _rows_before(ext, 2, CONV_HALO)
        cwv = cw_ref[...]
        cv = cwv[2:3] * q + cwv[1:2] * q1 + cwv[0:1] * q2
        da = _dot_nt(dya_ref[...], w_ref[...])
        dcv = da * zb
        red_ref[0:1, :] += _colsum(dcv * q2)
        red_ref[1:2, :] += _colsum(dcv * q1)
        red_ref[2:3, :] += _colsum(dcv * q)
        extd = jnp.concatenate([dcv, carry[...]], axis=0)
        carry[...] = dcv[:CONV_HALO]
        dq = cwv[2:3] * dcv + cwv[1:2] * _rows_after(extd, 1, ts) + cwv[0:1] * _rows_after(extd, 2, ts)
        dz_ref[:, 0:d] = (da * cv).astype(BF16)
        dz_ref[:, d:2 * d] = (dq * zx).astype(BF16)
        dz_ref[:, 2 * d:3 * d] = (dq * zc).astype(BF16)

    zspec = lambda k: pl.BlockSpec((ts, d), lambda i, k=k: (rev(i), k))
    return _run(
        body, name=name, grid=(nt,),
        in_specs=[pl.BlockSpec((ts, d), lambda i: (rev(i), 0)), zspec(0), zspec(1), zspec(2),
                  _halo_spec(ts, nt, CONV_HALO, d, 1), _halo_spec(ts, nt, CONV_HALO, d, 2),
                  _const((3, d)), _const((d, d)), HBM],
        out_specs=[pl.BlockSpec((ts, 3 * d), lambda i: (rev(i), 0)), _const((8, d))],
        out_shape=[jax.ShapeDtypeStruct((s, D_Z), BF16), jax.ShapeDtypeStruct((8, d), F32)],
        scratch_shapes=[pltpu.VMEM((CONV_HALO, d), F32)],
        args=(dya, z, z, z, z, z, cw, w_out, dz), aliases={8: 0}, comm=comm)


def _mix_b_bwd(dyb, z, dz, ln_g, ln_b, ws, bst, w_out, *, ts, name, comm=None):
    s = z.shape[0]
    d = D_MODEL
    ts = _tile(ts, s)
    nblk = ts // GBLK

    def body(dyb_ref, zu_ref, zv_ref, g_ref, b_ref, ws_ref, bst_ref, w_ref, dz_in,
             dz_ref, red_ref, dws_ref, dbst_ref, mixed_scr, dvn_scr, dzv_scr):
        @pl.when((pl.program_id(0) == 0) & (pl.program_id(1) == 0))
        def _():
            red_ref[...] = jnp.zeros_like(red_ref)
            dws_ref[...] = jnp.zeros_like(dws_ref)
            dbst_ref[...] = jnp.zeros_like(dbst_ref)

        @pl.when(pl.program_id(1) == 0)
        def _():
            u, du_dz = _gelu_and_grad(zu_ref[...])
            vg, dv_dz = _gelu_and_grad(zv_ref[...])
            xhat, rstd = _ln_fwd(vg)
            vn = (xhat * g_ref[...] + b_ref[...]).astype(BF16)
            _spatial_mix(vn, ws_ref, bst_ref, mixed_scr, ts)
            dsg = _dot_nt(dyb_ref[...], w_ref[...])
            dz_ref[...] = (dsg * mixed_scr[...] * du_dz).astype(BF16)
            dmix = dsg * u
            mask = _spatial_mask()
            for g in range(N_GROUPS):
                cols = slice(g * GBLK, (g + 1) * GBLK)
                wm = jnp.where(mask, ws_ref[g], 0.0).astype(BF16)
                dm_cat = jnp.concatenate([dmix[n * GBLK:(n + 1) * GBLK, cols] for n in range(nblk)], axis=1)
                vn_cat = jnp.concatenate([vn[n * GBLK:(n + 1) * GBLK, cols] for n in range(nblk)], axis=1)
                dm_b = dm_cat.astype(BF16)
                dbst_ref[:, g:g + 1] += jnp.sum(dm_cat, axis=1, keepdims=True)
                dws_ref[g] += jnp.where(mask, _dot_nt(dm_b, vn_cat), 0.0)
                dvn_cat = _dot_tn(wm, dm_b)
                for n in range(nblk):
                    dvn_scr[n * GBLK:(n + 1) * GBLK, cols] = dvn_cat[:, n * GBLK:(n + 1) * GBLK]
            dvn = dvn_scr[...]
            red_ref[0:1, :] += _colsum(dvn * xhat)
            red_ref[1:2, :] += _colsum(dvn)
            dzv_scr[...] = (_ln_bwd(dvn, g_ref[...], xhat, rstd) * dv_dz).astype(BF16)

        @pl.when(pl.program_id(1) == 1)
        def _():
            dz_ref[...] = dzv_scr[...]

    zspec = lambda k: pl.BlockSpec((ts, d), lambda i, h, k=k: (i, k))
    return _run(
        body, name=name, grid=(s // ts, 2),
        in_specs=[pl.BlockSpec((ts, d), lambda i, h: (i, 0)), zspec(3), zspec(4), _const((1, d)), _const((1, d)),
                  _const((N_GROUPS, GBLK, GBLK)), _const((GBLK, N_GROUPS)), _const((d, d)), HBM],
        out_specs=[pl.BlockSpec((ts, d), lambda i, h: (i, 3 + h)), _const((8, d)),
                   _const((N_GROUPS, GBLK, GBLK)), _const((GBLK, N_GROUPS))],
        out_shape=[jax.ShapeDtypeStruct((s, D_Z), BF16), jax.ShapeDtypeStruct((8, d), F32),
                   jax.ShapeDtypeStruct((N_GROUPS, GBLK, GBLK), F32), jax.ShapeDtypeStruct((GBLK, N_GROUPS), F32)],
        scratch_shapes=[pltpu.VMEM((ts, d), F32), pltpu.VMEM((ts, d), F32), pltpu.VMEM((ts, d), BF16)],
        args=(dyb, z, z, ln_g, ln_b, ws, bst, w_out, dz), aliases={8: 0}, comm=comm)


def _mix_c_bwd(dyc, z, dz, w_pool, scale, *, ts, name):
    s = z.shape[0]
    d = D_MODEL
    ts = _tile(ts, s)
    nt = s // ts
    rev = _rev(nt)

    def body(dyc_ref, zp_ref, halo_ref, w_ref, sc_ref, dz_in, dz_ref, red_ref, dw_ref, carry):
        i = pl.program_id(0)

        @pl.when(i == 0)
        def _():
            carry[...] = jnp.zeros_like(carry)
            red_ref[...] = jnp.zeros_like(red_ref)
            dw_ref[...] = jnp.zeros_like(dw_ref)
        p = zp_ref[...]
        ext = jnp.concatenate([jnp.where(i == nt - 1, 0.0, halo_ref[...]), p], axis=0)
        denoms = _pool_denoms(rev(i), ts)
        dyv = dyc_ref[...].astype(F32)
        for k in range(len(POOL_WINDOWS)):
            cols = slice(k * POOL_GROUP, (k + 1) * POOL_GROUP)
            dk = _pool_diff(p, ext, denoms, k).astype(BF16)
            red_ref[0:1, cols] += _colsum(dyv[:, cols] * _dot(dk, w_ref[k]))
            dpre = (dyv[:, cols] * sc_ref[:, cols]).astype(BF16)
            dw_ref[k] += _dot_tn(dk, dpre)
            dd = _dot_nt(dpre, w_ref[k])
            e = dd / denoms[k]
            acc = jnp.concatenate([e, carry[:, cols]], axis=0)
            carry[:, cols] = e[:POOL_HALO]
            step = 1
            while step < POOL_WINDOWS[k]:
                acc = acc + pltpu.roll(acc, acc.shape[0] - step, 0)
                step *= 2
            dz_ref[:, cols] = (acc[:ts] - dd).astype(BF16)

    return _run(
        body, name=name, grid=(nt,),
        in_specs=[pl.BlockSpec((ts, d), lambda i: (rev(i), 0)), pl.BlockSpec((ts, d), lambda i: (rev(i), 5)),
                  _halo_spec(ts, nt, POOL_HALO, d, 5), _const((4, POOL_GROUP, POOL_GROUP)), _const((1, d)), HBM],
        out_specs=[pl.BlockSpec((ts, d), lambda i: (rev(i), 5)), _const((8, d)), _const((4, POOL_GROUP, POOL_GROUP))],
        out_shape=[jax.ShapeDtypeStruct((s, D_Z), BF16), jax.ShapeDtypeStruct((8, d), F32),
                   jax.ShapeDtypeStruct((4, POOL_GROUP, POOL_GROUP), F32)],
        scratch_shapes=[pltpu.VMEM((POOL_HALO, d), F32)],
        args=(dyc, z, z, w_pool, scale, dz), aliases={5: 0})


def _in_proj_bwd(dz, w4, dxa, x, sc, *, ts, name, comm=None):
    s, d = x.shape
    ts = _tile(ts, s)
    wd = w4.shape[2]

    def body(dz_ref, w_ref, dxa_ref, x_ref, sc_ref, dx_ref, red_ref, db_ref):
        @pl.when(pl.program_id(0) == 0)
        def _():
            red_ref[...] = jnp.zeros_like(red_ref)
            db_ref[...] = jnp.zeros_like(db_ref)
        dh = jnp.zeros((ts, d), F32)
        for j in range(N_CHIPS):
            dzj = dz_ref[:, j * wd:(j + 1) * wd]
            db_ref[0:1, j * wd:(j + 1) * wd] += _colsum(dzj.astype(F32))
            dh = dh + _dot_nt(dzj, w_ref[j])
        dx_ref[...] = dxa_ref[...] + dh * (1.0 + sc_ref[...])
        red_ref[0:1, :] += _colsum(dh * x_ref[...])
        red_ref[1:2, :] += _colsum(dh)

    row = pl.BlockSpec((ts, d), lambda i: (i, 0))
    return _run(
        body, name=name, grid=(s // ts,),
        in_specs=[pl.BlockSpec((ts, D_Z), lambda i: (i, 0)), _resident((N_CHIPS, d, wd)), row, row, _const((1, d))],
        out_specs=[row, _const((8, d)), _const((8, D_Z))],
        out_shape=[jax.ShapeDtypeStruct((s, d), F32), jax.ShapeDtypeStruct((8, d), F32),
                   jax.ShapeDtypeStruct((8, D_Z), F32)],
        args=(dz, w4, dxa, x, sc), comm=comm)


def _ada_fwd(c_all, w_ada, b_ada, *, name):
    nl, d, n = w_ada.shape
    tn = n // 2

    def body(c_ref, w_ref, b_ref, o_ref):
        cv = c_ref[...]
        ca = (cv * _sigmoid(cv)).astype(BF16)
        o_ref[0] = _dot(ca, w_ref[0].astype(BF16)) + b_ref[0]

    return _run(
        body, name=name, grid=(nl, n // tn),
        in_specs=[_const((N_DEV, d)), pl.BlockSpec((1, d, tn), lambda l, j: (l, 0, j)),
                  pl.BlockSpec((1, 1, tn), lambda l, j: (l, 0, j))],
        out_specs=[pl.BlockSpec((1, N_DEV, tn), lambda l, j: (l, 0, j))],
        out_shape=[jax.ShapeDtypeStruct((nl, N_DEV, n), F32)], args=(c_all, w_ada, b_ada))[0]


def _ada_bwd(c_all, dada, *, name):
    nl, nb, n = dada.shape
    d = c_all.shape[1]
    tn = n // 2

    def body(c_ref, g_ref, o_ref):
        cv = c_ref[...]
        ca = (cv * _sigmoid(cv)).astype(BF16)
        o_ref[0] = _dot_tn(ca, g_ref[0].astype(BF16))

    return _run(
        body, name=name, grid=(nl, n // tn),
        in_specs=[_const((nb, d)), pl.BlockSpec((1, nb, tn), lambda l, j: (l, 0, j))],
        out_specs=[pl.BlockSpec((1, d, tn), lambda l, j: (l, 0, j))],
        out_shape=[jax.ShapeDtypeStruct((nl, d, n), F32)], args=(c_all, dada))[0]


def _sum4(own, recv, *, name):
    r, c = own.shape
    tr = _row_tile(r, c, 2)

    def body(own_ref, recv_ref, o_ref):
        acc = own_ref[...]
        for k in range(N_CHIPS - 1):
            acc = acc + recv_ref[k].astype(F32)
        o_ref[...] = acc

    return _run(
        body, name=name, grid=(r // tr,),
        in_specs=[pl.BlockSpec((tr, c), lambda i: (i, 0)), pl.BlockSpec((N_CHIPS - 1, tr, c), lambda i: (0, i, 0))],
        out_specs=[pl.BlockSpec((tr, c), lambda i: (i, 0))], out_shape=[jax.ShapeDtypeStruct((r, c), F32)],
        args=(own, recv))[0]


def _row_tile(r, c, mib):
    limit = max(8, (mib << 20) // (4 * c))
    if r <= limit:
        return r
    best = 8
    for t in range(8, limit + 1, 8):
        if r % t == 0:
            best = t
    return best


def _adam_math(w, g, m, v):
    mn = ADAM_B1 * m + (1.0 - ADAM_B1) * g
    vn = ADAM_B2 * v + (1.0 - ADAM_B2) * (g * g)
    m_hat = mn / (1.0 - ADAM_B1 ** ADAM_STEP)
    v_hat = vn / (1.0 - ADAM_B2 ** ADAM_STEP)
    return -ADAM_LR * (m_hat / (jnp.sqrt(v_hat) + ADAM_EPS) + ADAM_WD * w), mn, vn


def _adamw(w, g, m, v, *, name):
    r, c = w.shape
    tr = _row_tile(r, c, 2)

    def body(w_ref, g_ref, m_ref, v_ref, d_ref, mo_ref, vo_ref):
        d_ref[...], mo_ref[...], vo_ref[...] = _adam_math(w_ref[...], g_ref[...], m_ref[...], v_ref[...])

    blk = pl.BlockSpec((tr, c), lambda i: (i, 0))
    return _run(body, name=name, grid=(r // tr,), in_specs=[blk] * 4, out_specs=[blk] * 3,
                out_shape=[jax.ShapeDtypeStruct((r, c), F32)] * 3, args=(w, g, m, v))


def _adamw_sharded(w, m, v, parts, *, name):
    nl, r, c = w.shape
    tr = _row_tile(r, c, 1)
    nt = r // tr

    def body(w_ref, m_ref, v_ref, a0, b0, a1, b1, g_ref, d_ref, mo_ref, vo_ref):
        g = jnp.where(pl.program_id(0) == 0, a0[...] + b0[...], a1[...] + b1[...])
        g_ref[0] = g
        d_ref[0], mo_ref[0], vo_ref[0] = _adam_math(w_ref[0], g, m_ref[0], v_ref[0])

    blk = pl.BlockSpec((1, tr, c), lambda l, i: (l, i, 0))
    part0 = pl.BlockSpec((tr, c), lambda l, i: (jnp.where(l == 0, i, nt - 1), 0))
    part1 = pl.BlockSpec((tr, c), lambda l, i: (jnp.where(l == 1, i, 0), 0))
    return _run(body, name=name, grid=(nl, nt), in_specs=[blk] * 3 + [part0, part0, part1, part1],
                out_specs=[blk] * 4, out_shape=[jax.ShapeDtypeStruct((nl, r, c), F32)] * 4,
                args=(w, m, v, parts[0][0], parts[0][1], parts[1][0], parts[1][1]))


_BIG = ("w_in", "w_a_out", "w_b_out", "w_pool", "w_o", "w_up", "w_down")
_COL_SHARDED = ("w_in", "w_up")
_SMALL_SHARDED = ("conv_a", "conv_ffn")
_SMALL_REPL = ("b_in", "ln_v_g", "ln_v_b", "w_spatial", "b_spatial", "pool_scale", "ln1_g", "ln1_b", "b_up",
               "conv_ffn_b", "ln2_g", "ln2_b")
_WEIGHTS = ("w_ada", "b_ada", "w_in", "b_in", "conv_a", "w_a_out", "ln_v_g", "ln_v_b", "w_spatial", "b_spatial",
            "w_b_out", "w_pool", "pool_scale", "w_o", "ln1_g", "ln1_b", "w_up", "b_up", "conv_ffn", "conv_ffn_b",
            "w_down", "ln2_g", "ln2_b")


def _shard3(a):
    return a.reshape(a.shape[0], -1, a.shape[-1])


def _use_gathered(name, g):
    g = g.reshape(N_CHIPS, -1, g.shape[-1])
    if name in _COL_SHARDED:
        return g
    if name == "w_pool":
        return g.reshape(N_CHIPS, 4, POOL_GROUP // N_CHIPS, POOL_GROUP).transpose(1, 0, 2, 3).reshape(
            4, POOL_GROUP, POOL_GROUP)
    return g.reshape(-1, g.shape[-1])


def _grad_by_chip(name, g):
    if name in _COL_SHARDED:
        return g
    if name == "w_pool":
        return g.reshape(4, N_CHIPS, POOL_GROUP // N_CHIPS, POOL_GROUP).transpose(1, 0, 2, 3).reshape(
            N_CHIPS, POOL_GROUP, POOL_GROUP)
    return g.reshape(N_CHIPS, -1, g.shape[-1])


def _pack_small(arrs):
    parts = []
    for a in arrs:
        flat = a.reshape(-1).astype(F32)
        pad = (-flat.shape[0]) % 128
        parts.append(jnp.pad(flat, (0, pad)) if pad else flat)
    flat = jnp.concatenate(parts)
    pad = (-flat.shape[0]) % 1024
    if pad:
        flat = jnp.pad(flat, (0, pad))
    return flat.reshape(-1, 128)


def _unpack_small(buf, shapes):
    lead = buf.shape[:-2]
    flat = buf.reshape(lead + (-1,))
    out, off = [], 0
    for shp in shapes:
        n = math.prod(shp)
        out.append(flat[..., off:off + n].reshape(lead + tuple(shp)))
        off += n + ((-n) % 128)
    return out


def _as2d(a):
    return a.reshape(-1, a.shape[-1])


_LATE = ("w_a_out", "w_b_out", "w_pool", "w_o")


class _Traffic:
    def __init__(self, halves, plan):
        self.halves = halves
        self.plan = plan
        self.gathered = {}
        self.ready = {}
        self.parts = {}
        self.theirs = {}

    def weight(self, layer, name):
        return self.gathered[(layer, name)]

    def add_grad(self, layer, name, g_f32, g_bf16):
        self.ready[(layer, name)] = (_grad_by_chip(name, g_f32), _grad_by_chip(name, g_bf16))

    def _comm(self, job):
        if job[0] == "gather":
            return _gather_comm([self.halves[k] for k in job[2]], job[1])
        if job[0] == "scatter":
            return _scatter_comm([self.ready[k][1] for k in job[1]], [self.ready[k][0] for k in job[1]])
        return _swap_comm([self.parts[k] for k in job[1]])

    def _done(self, job, res):
        if job[0] == "gather":
            for k, r in zip(job[2], res):
                self.gathered[(job[1], k)] = _use_gathered(k, r)
        elif job[0] == "scatter":
            nw = len(job[1])
            for i, k in enumerate(job[1]):
                self.parts[k] = _sum4(res[nw + i], res[i], name=f"sum_l{k[0]}_{k[1]}")
                del self.ready[k]
        else:
            self.theirs.update(zip(job[1], res))

    def run(self, name, fn):
        jobs = self.plan.get(name)
        if not jobs:
            return fn(None)
        comms = [self._comm(j) for j in jobs]
        outs, res = fn(_merge(comms))
        for job, r in zip(jobs, _split(comms, res)):
            self._done(job, r)
        return outs

    def alone(self, name):
        jobs = self.plan[name]
        comms = [self._comm(j) for j in jobs]
        for job, r in zip(jobs, _split(comms, _comm_call(_merge(comms), name=name))):
            self._done(job, r)


def _layer_fwd(x, ada, p, l, tr):
    sh1, sc1, gt1, sh2, sc2, gt2 = ada
    n = f"l{l}"
    z, = tr.run(f"{n}_in_proj", lambda cm: _mod_matmul(
        x, sc1, sh1, tr.weight(l, "w_in"), p["b_in"], ts=1024, tn=2304, name=f"{n}_in_proj", comm=cm))
    a, ya = _mix_a_fwd(z, p["conv_a"], tr.weight(l, "w_a_out"), ts=256, name=f"{n}_mix_a")
    sg, yb = tr.run(f"{n}_mix_b", lambda cm: _mix_b_fwd(
        z, p["ln_v_g"], p["ln_v_b"], p["w_spatial"], p["b_spatial_t"], tr.weight(l, "w_b_out"), ts=256,
        name=f"{n}_mix_b", comm=cm))
    dpool, yc = _mix_c_fwd(z, tr.weight(l, "w_pool"), p["pool_scale"], ts=256, name=f"{n}_mix_c")
    merged, o, x1 = tr.run(f"{n}_mix_o", lambda cm: _mix_o_fwd(
        x, z, ya, yb, yc, tr.weight(l, "w_o"), gt1, p["ln1_g"], p["ln1_b"], ts=256, name=f"{n}_mix_o", comm=cm))
    up, = tr.run(f"{n}_up_proj", lambda cm: _mod_matmul(
        x1, sc2, sh2, tr.weight(l, "w_up"), p["b_up"], ts=1024, tn=1408, name=f"{n}_up_proj", comm=cm))
    f, dn, x2 = tr.run(f"{n}_ffn", lambda cm: _ffn_fwd(
        up, x1, p["conv_ffn"], p["conv_ffn_b"], tr.weight(l, "w_down"), gt2, p["ln2_g"], p["ln2_b"], ts=256,
        name=f"{n}_ffn", comm=cm))
    saved = dict(x=x, z=z, a=a, ya=ya, sg=sg, yb=yb, dpool=dpool, yc=yc, merged=merged, o=o, x1=x1, up=up, f=f, dn=dn)
    return x2, saved


def _layer_bwd(dx2, ada, p, sv, l, tr, dw_in_first):
    sh1, sc1, gt1, sh2, sc2, gt2 = ada
    n = f"l{l}"
    ddn, dup, dx1, red_d, red_f, dbup = tr.run(f"{n}_ffn_bwd", lambda cm: _ffn_bwd(
        dx2, sv["x1"], sv["dn"], sv["up"], p["conv_ffn"], p["conv_ffn_b"], tr.weight(l, "w_down"),
        tr.weight(l, "w_up"), gt2, p["ln2_g"], sc2, ts=256, name=f"{n}_ffn_bwd", comm=cm))
    g = {}
    tr.add_grad(l, "w_down", *_grad_matmul(sv["f"], ddn, ts=512, tn=512, name=f"{n}_dw_down"))
    tr.add_grad(l, "w_up", *tr.run(f"{n}_dw_up", lambda cm: _grad_matmul(
        sv["x1"], dup, ts=512, tn=1408, name=f"{n}_dw_up", mod=(sc2, sh2), by_chip=True, comm=cm)))
    g["ln2_g"], g["ln2_b"] = red_d[0], red_d[1]
    g["conv_ffn"], g["conv_ffn_b"], g["b_up"] = red_f[0:3], red_f[3], dbup[0]

    d_o, dxa, dz, dya, dyb, dyc, red_o = tr.run(f"{n}_mix_o_bwd", lambda cm: _mix_o_bwd(
        dx1, sv["x"], sv["o"], sv["z"], sv["ya"], sv["yb"], sv["yc"], tr.weight(l, "w_o"), gt1, p["ln1_g"], ts=256,
        name=f"{n}_mix_o_bwd", comm=cm))
    tr.add_grad(l, "w_o", *_grad_matmul(sv["merged"], d_o, ts=512, tn=1024, name=f"{n}_dw_o"))
    g["ln1_g"], g["ln1_b"] = red_o[0], red_o[1]

    dz, red_a = tr.run(f"{n}_mix_a_bwd", lambda cm: _mix_a_bwd(
        dya, sv["z"], dz, p["conv_a"], tr.weight(l, "w_a_out"), ts=256, name=f"{n}_mix_a_bwd", comm=cm))
    tr.add_grad(l, "w_a_out", *_grad_matmul(sv["a"], dya, ts=512, tn=1024, name=f"{n}_dw_a_out"))
    g["conv_a"] = red_a[0:3]

    dz, red_b, dws, dbst = tr.run(f"{n}_mix_b_bwd", lambda cm: _mix_b_bwd(
        dyb, sv["z"], dz, p["ln_v_g"], p["ln_v_b"], p["w_spatial"], p["b_spatial_t"], tr.weight(l, "w_b_out"), ts=256,
        name=f"{n}_mix_b_bwd", comm=cm))
    tr.add_grad(l, "w_b_out", *_grad_matmul(sv["sg"], dyb, ts=512, tn=1024, name=f"{n}_dw_b_out"))
    g["ln_v_g"], g["ln_v_b"], g["w_spatial"], g["b_spatial"] = red_b[0], red_b[1], dws, dbst.T

    dz, red_c, dwp = _mix_c_bwd(dyc, sv["z"], dz, tr.weight(l, "w_pool"), p["pool_scale"], ts=256,
                                name=f"{n}_mix_c_bwd")
    g["pool_scale"] = red_c[0]
    tr.add_grad(l, "w_pool", dwp, dwp.astype(BF16))

    def dw_in():
        tr.add_grad(l, "w_in", *tr.run(f"{n}_dw_in", lambda cm: _grad_matmul(
            sv["x"], dz, ts=512, tn=2304, name=f"{n}_dw_in", mod=(sc1, sh1), by_chip=True, comm=cm)))

    if dw_in_first:
        dw_in()
    dx, red_i, dbin = tr.run(f"{n}_in_proj_bwd", lambda cm: _in_proj_bwd(
        dz, tr.weight(l, "w_in"), dxa, sv["x"], sc1, ts=256, name=f"{n}_in_proj_bwd", comm=cm))
    if not dw_in_first:
        dw_in()
    g["b_in"] = dbin[0]
    dada = jnp.stack([red_i[1], red_i[0], red_o[2], red_d[4], red_d[3], red_d[2]])
    return dx, g, dada


def _traffic_plan():
    late0 = [(0, k) for k in _LATE]
    late1 = [(1, k) for k in _LATE]
    return {
        "gather_l0": [("gather", 0, ("w_in",) + _LATE)],
        "l0_in_proj": [("gather", 1, ("w_in",))],
        "l0_mix_b": [("gather", 0, ("w_down",))],
        "l0_mix_o": [("gather", 0, ("w_up",))],
        "l0_up_proj": [("gather", 1, _LATE)],
        "l0_ffn": [("gather", 1, ("w_up",))],
        "l1_in_proj": [("gather", 1, ("w_down",))],
        "l1_mix_o_bwd": [("scatter", [(1, "w_down")])],
        "l1_in_proj_bwd": [("scatter", late1)],
        "l1_dw_in": [("scatter", [(1, "w_up")])],
        "l0_ffn_bwd": [("scatter", [(1, "w_in")])],
        "l0_dw_up": [("swap", [(1, k) for k in _BIG])],
        "l0_mix_o_bwd": [("scatter", [(0, "w_down")])],
        "l0_dw_in": [("scatter", [(0, "w_up")] + late0)],
        "l0_in_proj_bwd": [("scatter", [(0, "w_in")]), ("swap", [(0, k) for k in _BIG if k != "w_in"])],
        "swap_tail": [("swap", [(0, "w_in")])],
    }


def kernel(x, c, w_ada, b_ada, w_in, b_in, conv_a, w_a_out, ln_v_g, ln_v_b, w_spatial, b_spatial, w_b_out, w_pool, pool_scale, w_o, ln1_g, ln1_b, w_up, b_up, conv_ffn, conv_ffn_b, w_down, ln2_g, ln2_b, loss_target, m_w_ada, m_b_ada, m_w_in, m_b_in, m_conv_a, m_w_a_out, m_ln_v_g, m_ln_v_b, m_w_spatial, m_b_spatial, m_w_b_out, m_w_pool, m_pool_scale, m_w_o, m_ln1_g, m_ln1_b, m_w_up, m_b_up, m_conv_ffn, m_conv_ffn_b, m_w_down, m_ln2_g, m_ln2_b, v_w_ada, v_b_ada, v_w_in, v_b_in, v_conv_a, v_w_a_out, v_ln_v_g, v_ln_v_b, v_w_spatial, v_b_spatial, v_w_b_out, v_w_pool, v_pool_scale, v_w_o, v_ln1_g, v_ln1_b, v_w_up, v_b_up, v_conv_ffn, v_conv_ffn_b, v_w_down, v_ln2_g, v_ln2_b):
    args = locals()
    w = {k: args[k] for k in _WEIGHTS}
    m = {k: args["m_" + k] for k in _WEIGHTS}
    v = {k: args["v_" + k] for k in _WEIGHTS}
    d = D_MODEL
    mx, my, mc = _my_coords()
    chip = 2 * mx + my
    me = 4 * mx + 2 * my + mc

    small_shapes = [c.shape, conv_a.shape, conv_ffn.shape]
    small_all = _all_gather8(_pack_small([c, conv_a, conv_ffn]), name="gather_small")
    c_all, conv_a_st, conv_ffn_st = _unpack_small(small_all, small_shapes)
    c_all = c_all.reshape(N_DEV, d)
    conv_full = {"conv_a": jnp.concatenate([conv_a_st[2 * j] for j in range(N_CHIPS)], axis=-1),
                 "conv_ffn": jnp.concatenate([conv_ffn_st[2 * j] for j in range(N_CHIPS)], axis=-1)}

    halves = {}
    for k in _BIG:
        s3 = _shard3(w[k]).astype(BF16)
        halves[k] = s3.reshape(s3.shape[0], 2, s3.shape[1] // 2, s3.shape[2])
    tr = _Traffic(halves, _traffic_plan())
    tr.alone("gather_l0")

    n_ada = w_ada.shape[2]
    b_ada_mine = lax.dynamic_slice_in_dim(b_ada, chip * n_ada, n_ada, axis=1)
    ada_part = _ada_fwd(c_all, w_ada, b_ada_mine.reshape(DEPTH, 1, n_ada), name="ada_fwd")
    ada_all = _all_gather8(_pack_small([ada_part]), name="gather_ada")
    ada_st = _unpack_small(ada_all, [ada_part.shape])[0][0::2]
    ada_rows = jnp.concatenate([ada_st[j] for j in range(N_CHIPS)], axis=-1)
    ada_mine = lax.dynamic_index_in_dim(ada_rows, me, axis=1, keepdims=False)

    def layer_params(l):
        p = {k: conv_full[k][l] for k in _SMALL_SHARDED}
        for k in ("b_in", "ln_v_g", "ln_v_b", "pool_scale", "ln1_g", "ln1_b", "b_up", "conv_ffn_b", "ln2_g", "ln2_b"):
            p[k] = w[k][l].reshape(1, -1)
        p["w_spatial"] = w_spatial[l]
        p["b_spatial_t"] = b_spatial[l].T
        return p

    xs = x[0]
    saved, adas, params = [], [], []
    for l in range(DEPTH):
        ada = [ada_mine[l, k * d:(k + 1) * d].reshape(1, d) for k in range(6)]
        p = layer_params(l)
        xs, sv = _layer_fwd(xs, ada, p, l, tr)
        saved.append(sv), adas.append(ada), params.append(p)
    dx, loss_blk = _loss_fwd(xs, loss_target[0], ts=512, name="loss")

    grads, dadas = [None] * DEPTH, [None] * DEPTH
    dx, grads[1], dadas[1] = _layer_bwd(dx, adas[1], params[1], saved[1], 1, tr, dw_in_first=False)
    dx, grads[0], dadas[0] = _layer_bwd(dx, adas[0], params[0], saved[0], 0, tr, dw_in_first=True)
    tr.alone("swap_tail")
    dada = jnp.stack(dadas).reshape(DEPTH, 6 * d)

    small_names = _SMALL_REPL + _SMALL_SHARDED
    small_g = [jnp.stack([grads[l][k] for l in range(DEPTH)]) for k in small_names]
    gsum = dict(zip(small_names, _unpack_small(_all_reduce_small(_pack_small(small_g), name="reduce_small"),
                                               [a.shape for a in small_g])))
    tail_g = [dada, loss_blk[0:1, 0:1]]
    tail_all, tail_sum = _all_gather8(_pack_small(tail_g), name="gather_dada", with_sum=True)
    gsum["b_ada"], loss_sum = _unpack_small(tail_sum, [a.shape for a in tail_g])
    loss = loss_sum[0, 0]
    dada_all = _unpack_small(tail_all, [a.shape for a in tail_g])[0]
    for k in _SMALL_SHARDED:
        wd = gsum[k].shape[-1] // N_CHIPS
        gsum[k] = lax.dynamic_slice_in_dim(gsum[k], chip * wd, wd, axis=gsum[k].ndim - 1)

    dada_cols = lax.dynamic_slice_in_dim(dada_all, chip * n_ada, n_ada, axis=2)
    dada_cols = jnp.pad(jnp.swapaxes(dada_cols, 0, 1), ((0, 0), (0, N_DEV), (0, 0)))
    gsum["w_ada"] = _ada_bwd(jnp.pad(c_all, ((0, N_DEV), (0, 0))), dada_cols, name="ada_bwd")

    out_g, out_d, out_m, out_v = {}, {}, {}, {}
    for k in _WEIGHTS:
        shp = w[k].shape
        if k in _BIG:
            parts = [(tr.parts[(l, k)], tr.theirs[(l, k)]) for l in range(DEPTH)]
            res = _adamw_sharded(_shard3(w[k]), _shard3(m[k]), _shard3(v[k]), parts, name=f"adamw_{k}")
        else:
            gk = gsum[k].reshape(shp)
            res = [gk] + list(_adamw(_as2d(w[k]), _as2d(gk), _as2d(m[k]), _as2d(v[k]), name=f"adamw_{k}"))
        out_g[k], out_d[k], out_m[k], out_v[k] = [r.reshape(shp) for r in res]

    return (loss, dx[None], *[out_g[k] for k in _WEIGHTS], *[out_d[k] for k in _WEIGHTS],
            *[out_m[k] for k in _WEIGHTS], *[out_v[k] for k in _WEIGHTS])
```

```python
import math
from typing import Callable, NamedTuple

import jax
import jax.numpy as jnp
from jax import lax
from jax.experimental import pallas as pl
from jax.experimental.pallas import tpu as pltpu

F32 = jnp.float32
BF16 = jnp.bfloat16

D_MODEL = 1024
D_Z = 9216
D_FF = 2816
N_GROUPS = 8
GBLK = 128
CHUNK = 64
POOL_WINDOWS = (2, 4, 8, 16)
POOL_GROUP = 256
POOL_HALO = 16
CONV_HALO = 8
DEPTH = 2
ALPHA = (2 * DEPTH) ** 0.25
LN_EPS = 1e-5
ADAM_LR, ADAM_B1, ADAM_B2, ADAM_EPS, ADAM_WD, ADAM_STEP = 0.001, 0.9, 0.999, 1e-08, 0.01, 10
N_CHIPS = 4
N_DEV = 8
FF_CHUNK = 704
MESH = pl.DeviceIdType.MESH
VMEM_LIMIT = 56 * 1024 * 1024
HBM = pl.BlockSpec(memory_space=pl.ANY)


def _dot(a, b):
    return jnp.dot(a, b, preferred_element_type=F32)


def _dot_nt(a, b):
    return lax.dot_general(a, b, (((1,), (1,)), ((), ())), preferred_element_type=F32)


def _dot_tn(a, b):
    return lax.dot_general(a, b, (((0,), (0,)), ((), ())), preferred_element_type=F32)


_GELU_C = math.sqrt(2.0 / math.pi)


def _gelu_and_grad(x):
    x2 = x * x
    t = jnp.tanh(_GELU_C * (x + 0.044715 * x * x2))
    g = 0.5 * x * (1.0 + t)
    dg = 0.5 * (1.0 + t) + 0.5 * x * (1.0 - t * t) * (_GELU_C * (1.0 + 3 * 0.044715 * x2))
    return g, dg


def _gelu(x):
    return 0.5 * x * (1.0 + jnp.tanh(_GELU_C * (x + 0.044715 * x * x * x)))


def _sigmoid(x):
    return 1.0 / (1.0 + jnp.exp(-x))


def _ln_fwd(r):
    mu = jnp.mean(r, axis=-1, keepdims=True)
    xc = r - mu
    var = jnp.mean(xc * xc, axis=-1, keepdims=True)
    rstd = lax.rsqrt(var + LN_EPS)
    return xc * rstd, rstd


def _ln_bwd(dy, g, xhat, rstd):
    dxh = dy * g
    m1 = jnp.mean(dxh, axis=-1, keepdims=True)
    m2 = jnp.mean(dxh * xhat, axis=-1, keepdims=True)
    return rstd * (dxh - m1 - xhat * m2)


def _rows_before(ext, k, halo):
    return pltpu.roll(ext, k, 0)[halo:]


def _rows_after(ext, k, n):
    return pltpu.roll(ext, ext.shape[0] - k, 0)[:n]


def _colsum(v):
    return jnp.sum(v, axis=0, keepdims=True)


def _spatial_mask():
    i = lax.broadcasted_iota(jnp.int32, (GBLK, GBLK), 0)
    j = lax.broadcasted_iota(jnp.int32, (GBLK, GBLK), 1)
    return (j // CHUNK) <= (i // CHUNK)


def _const(shape):
    n = len(shape)
    return pl.BlockSpec(shape, lambda *_: (0,) * n)


def _resident(shape):
    n = len(shape)
    return pl.BlockSpec(shape, lambda *_: (0,) * n, pipeline_mode=pl.Buffered(1))


def _tile(ts, s):
    return min(ts, s)


class _Comm(NamedTuple):
    srcs: tuple
    dsts: tuple
    n_remote: int
    n_local: int
    build: Callable


def _my_coords():
    return lax.axis_index("x"), lax.axis_index("y"), lax.axis_index("c")


def _chip_peer(k):
    mx, my, mc = _my_coords()
    return (mx ^ ((k >> 1) & 1), my ^ (k & 1), mc)


def _sem_scratch(comm):
    return [pltpu.SemaphoreType.DMA((max(comm.n_remote, 1),)), pltpu.SemaphoreType.DMA((max(comm.n_remote, 1),)),
            pltpu.SemaphoreType.DMA((max(comm.n_local, 1),))]


def _run(body, *, name, grid, in_specs, out_specs, out_shape, args, scratch_shapes=(), comm=None, aliases=None):
    sem = ("arbitrary",) * len(grid)
    cparams = pltpu.CompilerParams(dimension_semantics=sem, vmem_limit_bytes=VMEM_LIMIT)
    kwargs = {} if aliases is None else {"input_output_aliases": aliases}
    if comm is None:
        return pl.pallas_call(body, name=name, grid=grid, in_specs=in_specs, out_specs=out_specs, out_shape=out_shape,
                              scratch_shapes=list(scratch_shapes), compiler_params=cparams, **kwargs)(*args)
    n_in, n_cs, n_out, n_cd, n_scr = len(in_specs), len(comm.srcs), len(out_specs), len(comm.dsts), len(scratch_shapes)
    total = math.prod(grid)
    mid_step = min(total - 1, int(total * 0.7))

    def wrapped(*refs):
        ins, refs = refs[:n_in], refs[n_in:]
        csrc, refs = refs[:n_cs], refs[n_cs:]
        outs, refs = refs[:n_out], refs[n_out:]
        cdst, refs = refs[:n_cd], refs[n_cd:]
        scr, sems = refs[:n_scr], refs[n_scr:]
        step = pl.program_id(0)
        for ax in range(1, len(grid)):
            step = step * grid[ax] + pl.program_id(ax)
        first, mid, last = comm.build(csrc, cdst, *sems, 0, 0)
        pl.when(step == 0)(first)
        if mid is not None:
            pl.when(step == mid_step)(mid)
        body(*ins, *outs, *scr)
        pl.when(step == total - 1)(last)

    res = pl.pallas_call(
        wrapped, name=name, grid=grid, in_specs=list(in_specs) + [HBM] * n_cs, out_specs=list(out_specs) + [HBM] * n_cd,
        out_shape=list(out_shape) + list(comm.dsts), scratch_shapes=list(scratch_shapes) + _sem_scratch(comm),
        compiler_params=cparams, **kwargs)(*args, *comm.srcs)
    return res[:n_out], res[n_out:]


def _comm_call(comm, *, name):
    def body(*refs):
        n_cs, n_cd = len(comm.srcs), len(comm.dsts)
        first, mid, last = comm.build(refs[:n_cs], refs[n_cs:n_cs + n_cd], *refs[n_cs + n_cd:], 0, 0)
        first()
        if mid is not None:
            mid()
        last()

    return pl.pallas_call(body, name=name, in_specs=[HBM] * len(comm.srcs), out_specs=[HBM] * len(comm.dsts),
                          out_shape=list(comm.dsts), scratch_shapes=_sem_scratch(comm))(*comm.srcs)


def _gather_comm(shards, layer):
    dsts = tuple(jax.ShapeDtypeStruct((N_CHIPS,) + s.shape[1:], s.dtype) for s in shards)
    nw = len(shards)

    def build(srcs, outs, send_sems, recv_sems, local_sems, r0, l0):
        mx, my, mc = _my_coords()
        me = 2 * mx + my
        sibling = (mx, my, 1 - mc)

        def rdma(src, dst, idx, peer):
            return pltpu.make_async_remote_copy(src_ref=src, dst_ref=dst, send_sem=send_sems.at[r0 + idx],
                                                recv_sem=recv_sems.at[r0 + idx], device_id=peer, device_id_type=MESH)

        def ici(w, k, slot):
            return rdma(srcs[w].at[layer, mc], outs[w].at[slot, mc], 6 * w + k - 1, _chip_peer(k))

        def fwd(w, k, half):
            return rdma(outs[w].at[me ^ k, mc], outs[w].at[me ^ k, half], 6 * w + 2 + k, sibling)

        def own(w):
            return pltpu.make_async_copy(srcs[w].at[layer], outs[w].at[me], local_sems.at[l0 + w])

        def first():
            for w in range(nw):
                own(w).start()
                for k in range(1, N_CHIPS):
                    ici(w, k, me).start()

        def mid():
            for w in range(nw):
                for k in range(1, N_CHIPS):
                    ici(w, k, me ^ k).wait_recv()
                    fwd(w, k, mc).start()

        def last():
            for w in range(nw):
                for k in range(1, N_CHIPS):
                    fwd(w, k, 1 - mc).wait_recv()
                    ici(w, k, me).wait_send()
                    fwd(w, k, mc).wait_send()
                own(w).wait()

        return first, mid, last

    return _Comm(tuple(shards), dsts, 6 * nw, nw, build)


def _symmetric(make_remote, make_local, make_incoming=None):
    def first():
        for cp in make_remote() + make_local():
            cp.start()

    def last():
        for cp in (make_incoming or make_remote)():
            cp.wait_recv()
        for cp in make_remote():
            cp.wait_send()
        for cp in make_local():
            cp.wait()

    return first, None, last


def _presum_comm(g_bf16):
    nw = len(g_bf16)
    dsts = tuple(jax.ShapeDtypeStruct((N_CHIPS,) + g.shape[2:], BF16) for g in g_bf16)

    def build(srcs, outs, send_sems, recv_sems, local_sems, r0, l0):
        mx, my, mc = _my_coords()

        def remote():
            return [pltpu.make_async_remote_copy(
                src_ref=srcs[w].at[j, 1 - mc], dst_ref=outs[w].at[j], send_sem=send_sems.at[r0 + N_CHIPS * w + j],
                recv_sem=recv_sems.at[r0 + N_CHIPS * w + j], device_id=(mx, my, 1 - mc), device_id_type=MESH)
                for w in range(nw) for j in range(N_CHIPS)]

        return _symmetric(remote, lambda: [])

    return _Comm(tuple(g_bf16), dsts, N_CHIPS * nw, 0, build)


def _scatter_comm(h_bf16):
    nw = len(h_bf16)
    dsts = tuple(jax.ShapeDtypeStruct((N_CHIPS - 1,) + h.shape[1:], BF16) for h in h_bf16)

    def build(srcs, outs, send_sems, recv_sems, local_sems, r0, l0):
        mx, my, _ = _my_coords()
        me = 2 * mx + my

        def remote():
            return [pltpu.make_async_remote_copy(
                src_ref=srcs[w].at[me ^ k], dst_ref=outs[w].at[k - 1], send_sem=send_sems.at[r0 + 3 * w + k - 1],
                recv_sem=recv_sems.at[r0 + 3 * w + k - 1], device_id=_chip_peer(k), device_id_type=MESH)
                for w in range(nw) for k in range(1, N_CHIPS)]

        return _symmetric(remote, lambda: [])

    return _Comm(tuple(h_bf16), dsts, 3 * nw, 0, build)


def _join_comm(halves):
    nw = len(halves)
    dsts = tuple(jax.ShapeDtypeStruct((2,) + h.shape, h.dtype) for h in halves)

    def build(srcs, outs, send_sems, recv_sems, local_sems, r0, l0):
        mx, my, mc = _my_coords()

        def remote(half=mc):
            return [pltpu.make_async_remote_copy(
                src_ref=srcs[w], dst_ref=outs[w].at[half], send_sem=send_sems.at[r0 + w], recv_sem=recv_sems.at[r0 + w],
                device_id=(mx, my, 1 - mc), device_id_type=MESH) for w in range(nw)]

        def local():
            return [pltpu.make_async_copy(srcs[w], outs[w].at[mc], local_sems.at[l0 + w]) for w in range(nw)]

        return _symmetric(remote, local, lambda: remote(1 - mc))

    return _Comm(tuple(halves), dsts, nw, nw, build)


def _merge(comms):
    comms = list(comms)
    if len(comms) == 1:
        return comms[0]

    def build(srcs, outs, send_sems, recv_sems, local_sems, r0, l0):
        phases, s0, d0 = [], 0, 0
        for cm in comms:
            phases.append(cm.build(srcs[s0:s0 + len(cm.srcs)], outs[d0:d0 + len(cm.dsts)], send_sems, recv_sems,
                                   local_sems, r0, l0))
            s0, d0, r0, l0 = s0 + len(cm.srcs), d0 + len(cm.dsts), r0 + cm.n_remote, l0 + cm.n_local

        def run(idx):
            fns = [ph[idx] for ph in phases if ph[idx] is not None]
            if not fns:
                return None

            def go():
                for fn in fns:
                    fn()
            return go

        return run(0), run(1), run(2)

    return _Comm(sum((cm.srcs for cm in comms), ()), sum((cm.dsts for cm in comms), ()),
                 sum(cm.n_remote for cm in comms), sum(cm.n_local for cm in comms), build)


def _split(comms, res):
    out, d0 = [], 0
    for cm in comms:
        out.append(list(res[d0:d0 + len(cm.dsts)]))
        d0 += len(cm.dsts)
    return out


def _all_reduce_small(x, *, name):
    r, lanes = x.shape

    def body(x_ref, out_ref, sib_ref, slots_ref, send_sems, recv_sems):
        mx, my, mc = _my_coords()
        me = 2 * mx + my
        swap = pltpu.make_async_remote_copy(src_ref=x_ref, dst_ref=sib_ref, send_sem=send_sems.at[0],
                                            recv_sem=recv_sems.at[0], device_id=(mx, my, 1 - mc), device_id_type=MESH)
        swap.start()
        swap.wait_recv()
        swap.wait_send()
        slots_ref[me] = x_ref[...] + sib_ref[...]

        def copy(k, slot):
            return pltpu.make_async_remote_copy(
                src_ref=slots_ref.at[me], dst_ref=slots_ref.at[slot], send_sem=send_sems.at[k], recv_sem=recv_sems.at[k],
                device_id=_chip_peer(k), device_id_type=MESH)

        sends = [copy(k, me) for k in range(1, N_CHIPS)]
        for cp in sends:
            cp.start()
        for k in range(1, N_CHIPS):
            copy(k, me ^ k).wait_recv()
        for cp in sends:
            cp.wait_send()
        acc = slots_ref[0]
        for j in range(1, N_CHIPS):
            acc = acc + slots_ref[j]
        out_ref[...] = acc

    vmem = pl.BlockSpec(memory_space=pltpu.VMEM)
    return pl.pallas_call(
        body, name=name, in_specs=[vmem], out_specs=vmem, out_shape=jax.ShapeDtypeStruct((r, lanes), F32),
        scratch_shapes=[pltpu.VMEM((r, lanes), F32), pltpu.VMEM((N_CHIPS, r, lanes), F32),
                        pltpu.SemaphoreType.DMA((N_CHIPS,)), pltpu.SemaphoreType.DMA((N_CHIPS,))],
        compiler_params=pltpu.CompilerParams(vmem_limit_bytes=VMEM_LIMIT),
    )(x)


def _all_gather8(x, *, name, with_sum=False):
    r, lanes = x.shape

    def body(x_ref, out_ref, *rest):
        if with_sum:
            sum_ref, send_sems, recv_sems, local_sem = rest
        else:
            send_sems, recv_sems, local_sem = rest
        mx, my, mc = _my_coords()
        me = 4 * mx + 2 * my + mc

        def peer(k):
            return (mx ^ ((k >> 2) & 1), my ^ ((k >> 1) & 1), mc ^ (k & 1))

        def copy(k, slot):
            return pltpu.make_async_remote_copy(
                src_ref=x_ref, dst_ref=out_ref.at[slot], send_sem=send_sems.at[k - 1], recv_sem=recv_sems.at[k - 1],
                device_id=peer(k), device_id_type=MESH)

        mine = pltpu.make_async_copy(x_ref, out_ref.at[me], local_sem)
        mine.start()
        sends = [copy(k, me) for k in range(1, N_DEV)]
        for cp in sends:
            cp.start()
        for k in range(1, N_DEV):
            copy(k, me ^ k).wait_recv()
        for cp in sends:
            cp.wait_send()
        mine.wait()
        if with_sum:
            acc = out_ref[0]
            for k in range(1, N_DEV):
                acc = acc + out_ref[k]
            sum_ref[...] = acc

    vmem = pl.BlockSpec(memory_space=pltpu.VMEM)
    out_shape = [jax.ShapeDtypeStruct((N_DEV, r, lanes), F32)]
    if with_sum:
        out_shape.append(jax.ShapeDtypeStruct((r, lanes), F32))
    res = pl.pallas_call(
        body, name=name, in_specs=[vmem], out_specs=[vmem] * len(out_shape), out_shape=out_shape,
        scratch_shapes=[pltpu.SemaphoreType.DMA((N_DEV - 1,)), pltpu.SemaphoreType.DMA((N_DEV - 1,)),
                        pltpu.SemaphoreType.DMA],
        compiler_params=pltpu.CompilerParams(vmem_limit_bytes=VMEM_LIMIT),
    )(x)
    return res if with_sum else res[0]


def _mod_matmul(x, sc, sh, w4, b, *, ts, tn, name, comm=None):
    s, d = x.shape
    wd = w4.shape[2]
    n = N_CHIPS * wd
    per = wd // tn
    ts = _tile(ts, s)

    def body(x_ref, sc_ref, sh_ref, w_ref, b_ref, o_ref, h_scr):
        @pl.when(pl.program_id(1) == 0)
        def _():
            h_scr[...] = (x_ref[...] * (1.0 + sc_ref[...]) + sh_ref[...]).astype(BF16)
        o_ref[...] = _dot(h_scr[...], w_ref[0]) + b_ref[...]

    return _run(
        body, name=name, grid=(s // ts, n // tn),
        in_specs=[pl.BlockSpec((ts, d), lambda i, j: (i, 0)), _const((1, d)), _const((1, d)),
                  pl.BlockSpec((1, d, tn), lambda i, j: (j // per, 0, j % per)),
                  pl.BlockSpec((1, tn), lambda i, j: (0, j))],
        out_specs=[pl.BlockSpec((ts, tn), lambda i, j: (i, j))],
        out_shape=[jax.ShapeDtypeStruct((s, n), F32)],
        scratch_shapes=[pltpu.VMEM((ts, d), BF16)],
        args=(x, sc, sh, w4, b), comm=comm)


def _conv3(q, ext, cw):
    return cw[2:3] * q + cw[1:2] * _rows_before(ext, 1, CONV_HALO) + cw[0:1] * _rows_before(ext, 2, CONV_HALO)


def _mix_a_fwd(z, cw, w_out, *, ts, name):
    s = z.shape[0]
    d = D_MODEL
    ts = _tile(ts, s)

    def body(zb_ref, zc_ref, zx_ref, cw_ref, w_ref, a_ref, y_ref, carry):
        @pl.when(pl.program_id(0) == 0)
        def _():
            carry[...] = jnp.zeros_like(carry)
        q = zc_ref[...] * zx_ref[...]
        ext = jnp.concatenate([carry[...], q], axis=0)
        a = (zb_ref[...] * _conv3(q, ext, cw_ref[...])).astype(BF16)
        carry[...] = q[ts - CONV_HALO:]
        a_ref[...] = a
        y_ref[...] = _dot(a, w_ref[...])

    zspec = lambda k: pl.BlockSpec((ts, d), lambda i, k=k: (i, k))
    return _run(
        body, name=name, grid=(s // ts,),
        in_specs=[zspec(0), zspec(1), zspec(2), _const((3, d)), _const((d, d))],
        out_specs=[pl.BlockSpec((ts, d), lambda i: (i, 0))] * 2,
        out_shape=[jax.ShapeDtypeStruct((s, d), BF16), jax.ShapeDtypeStruct((s, d), F32)],
        scratch_shapes=[pltpu.VMEM((CONV_HALO, d), F32)],
        args=(z, z, z, cw, w_out))


def _spatial_mix(vn_b, ws_ref, bst_ref, mixed_scr, ts):
    nblk = ts // GBLK
    mask = _spatial_mask()
    for g in range(N_GROUPS):
        cols = slice(g * GBLK, (g + 1) * GBLK)
        wm = jnp.where(mask, ws_ref[g], 0.0).astype(BF16)
        cat = jnp.concatenate([vn_b[n * GBLK:(n + 1) * GBLK, cols] for n in range(nblk)], axis=1)
        res = _dot(wm, cat) + bst_ref[:, g:g + 1]
        for n in range(nblk):
            mixed_scr[n * GBLK:(n + 1) * GBLK, cols] = res[:, n * GBLK:(n + 1) * GBLK]


def _mix_b_fwd(z, ln_g, ln_b, ws, bst, w_out, *, ts, name, comm=None):
    s = z.shape[0]
    d = D_MODEL
    ts = _tile(ts, s)

    def body(zu_ref, zv_ref, g_ref, b_ref, ws_ref, bst_ref, w_ref, sg_ref, y_ref, mixed_scr):
        xhat, _ = _ln_fwd(_gelu(zv_ref[...]))
        vn = (xhat * g_ref[...] + b_ref[...]).astype(BF16)
        _spatial_mix(vn, ws_ref, bst_ref, mixed_scr, ts)
        sg = (_gelu(zu_ref[...]) * mixed_scr[...]).astype(BF16)
        sg_ref[...] = sg
        y_ref[...] = _dot(sg, w_ref[...])

    zspec = lambda k: pl.BlockSpec((ts, d), lambda i, k=k: (i, k))
    return _run(
        body, name=name, grid=(s // ts,),
        in_specs=[zspec(3), zspec(4), _const((1, d)), _const((1, d)), _const((N_GROUPS, GBLK, GBLK)),
                  _const((GBLK, N_GROUPS)), _const((d, d))],
        out_specs=[pl.BlockSpec((ts, d), lambda i: (i, 0))] * 2,
        out_shape=[jax.ShapeDtypeStruct((s, d), BF16), jax.ShapeDtypeStruct((s, d), F32)],
        scratch_shapes=[pltpu.VMEM((ts, d), F32)],
        args=(z, z, ln_g, ln_b, ws, bst, w_out), comm=comm)


def _pool_denoms(tile_idx, ts):
    t1 = (tile_idx * ts + 1 + lax.broadcasted_iota(jnp.int32, (ts, 1), 0)).astype(F32)
    return [jnp.minimum(t1, float(w)) for w in POOL_WINDOWS]


def _pool_diff(p, ext, denoms, k):
    cols = slice(k * POOL_GROUP, (k + 1) * POOL_GROUP)
    acc = ext[:, cols]
    step = 1
    while step < POOL_WINDOWS[k]:
        acc = acc + pltpu.roll(acc, step, 0)
        step *= 2
    return acc[POOL_HALO:] / denoms[k] - p[:, cols]


def _mix_c_fwd(z, w_pool, scale, *, ts, name):
    s = z.shape[0]
    d = D_MODEL
    ts = _tile(ts, s)

    def body(zp_ref, w_ref, sc_ref, d_ref, y_ref, carry):
        i = pl.program_id(0)

        @pl.when(i == 0)
        def _():
            carry[...] = jnp.zeros_like(carry)
        p = zp_ref[...]
        ext = jnp.concatenate([carry[...], p], axis=0)
        carry[...] = p[ts - POOL_HALO:]
        denoms = _pool_denoms(i, ts)
        for k in range(len(POOL_WINDOWS)):
            cols = slice(k * POOL_GROUP, (k + 1) * POOL_GROUP)
            dk = _pool_diff(p, ext, denoms, k).astype(BF16)
            d_ref[:, cols] = dk
            y_ref[:, cols] = _dot(dk, w_ref[k]) * sc_ref[:, cols]

    return _run(
        body, name=name, grid=(s // ts,),
        in_specs=[pl.BlockSpec((ts, d), lambda i: (i, 5)), _const((4, POOL_GROUP, POOL_GROUP)), _const((1, d))],
        out_specs=[pl.BlockSpec((ts, d), lambda i: (i, 0))] * 2,
        out_shape=[jax.ShapeDtypeStruct((s, d), BF16), jax.ShapeDtypeStruct((s, d), F32)],
        scratch_shapes=[pltpu.VMEM((POOL_HALO, d), F32)],
        args=(z, w_pool, scale))


def _mix_o_fwd(x, z, ya, yb, yc, w_o, gt, ln_g, ln_b, *, ts, name, comm=None):
    s, d = x.shape
    ts = _tile(ts, s)

    def body(x_ref, ga_ref, gb_ref, gc_ref, ya_ref, yb_ref, yc_ref, w_ref, gt_ref, g_ref, b_ref,
             m_ref, o_ref, x1_ref):
        merged = (_sigmoid(ga_ref[...]) * ya_ref[...] + _sigmoid(gb_ref[...]) * yb_ref[...]
                  + _sigmoid(gc_ref[...]) * yc_ref[...]).astype(BF16)
        m_ref[...] = merged
        o = _dot(merged, w_ref[...])
        o_ref[...] = o
        xhat, _ = _ln_fwd(ALPHA * x_ref[...] + gt_ref[...] * o)
        x1_ref[...] = xhat * g_ref[...] + b_ref[...]

    row = pl.BlockSpec((ts, d), lambda i: (i, 0))
    zspec = lambda k: pl.BlockSpec((ts, d), lambda i, k=k: (i, k))
    return _run(
        body, name=name, grid=(s // ts,),
        in_specs=[row, zspec(6), zspec(7), zspec(8), row, row, row, _const((d, d)),
                  _const((1, d)), _const((1, d)), _const((1, d))],
        out_specs=[row] * 3,
        out_shape=[jax.ShapeDtypeStruct((s, d), BF16), jax.ShapeDtypeStruct((s, d), F32),
                   jax.ShapeDtypeStruct((s, d), F32)],
        args=(x, z, z, z, ya, yb, yc, w_o, gt, ln_g, ln_b), comm=comm)


def _ffn_fwd(up, x1, cw, cb, w_down, gt, ln_g, ln_b, *, ts, name, comm=None):
    s, d = x1.shape
    ts = _tile(ts, s)

    def body(up_ref, x1_ref, cw_ref, cb_ref, w_ref, gt_ref, g_ref, b_ref, f_ref, dn_ref, x2_ref, carry):
        @pl.when(pl.program_id(0) == 0)
        def _():
            carry[...] = jnp.zeros_like(carry)
        for c in range(D_FF // FF_CHUNK):
            ca = slice(c * FF_CHUNK, (c + 1) * FF_CHUNK)
            cg = slice(D_FF + c * FF_CHUNK, D_FF + (c + 1) * FF_CHUNK)
            ua = up_ref[:, ca]
            ext = jnp.concatenate([carry[:, ca], ua], axis=0)
            carry[:, ca] = ua[ts - CONV_HALO:]
            cf = _conv3(ua, ext, cw_ref[:, ca]) + cb_ref[:, ca]
            f_ref[:, ca] = (_gelu(cf) * up_ref[:, cg]).astype(BF16)
        dn = _dot(f_ref[...], w_ref[...])
        dn_ref[...] = dn
        xhat, _ = _ln_fwd(ALPHA * x1_ref[...] + gt_ref[...] * dn)
        x2_ref[...] = xhat * g_ref[...] + b_ref[...]

    row = pl.BlockSpec((ts, d), lambda i: (i, 0))
    return _run(
        body, name=name, grid=(s // ts,),
        in_specs=[pl.BlockSpec((ts, 2 * D_FF), lambda i: (i, 0)), row, _const((3, D_FF)), _const((1, D_FF)),
                  _resident((D_FF, d)), _const((1, d)), _const((1, d)), _const((1, d))],
        out_specs=[pl.BlockSpec((ts, D_FF), lambda i: (i, 0)), row, row],
        out_shape=[jax.ShapeDtypeStruct((s, D_FF), BF16), jax.ShapeDtypeStruct((s, d), F32),
                   jax.ShapeDtypeStruct((s, d), F32)],
        scratch_shapes=[pltpu.VMEM((CONV_HALO, D_FF), F32)],
        args=(up, x1, cw, cb, w_down, gt, ln_g, ln_b), comm=comm)


def _loss_fwd(y, tgt, *, ts, name):
    s, d = y.shape
    ts = _tile(ts, s)

    def body(y_ref, t_ref, dy_ref, l_ref):
        @pl.when(pl.program_id(0) == 0)
        def _():
            l_ref[...] = jnp.zeros_like(l_ref)
        e = y_ref[...] - t_ref[...]
        dy_ref[...] = e / float(d)
        l_ref[...] += 0.5 * jnp.sum(jnp.mean(e * e, axis=-1, keepdims=True), axis=0, keepdims=True)

    row = pl.BlockSpec((ts, d), lambda i: (i, 0))
    return _run(body, name=name, grid=(s // ts,), in_specs=[row, row], out_specs=[row, _const((8, 128))],
                out_shape=[jax.ShapeDtypeStruct((s, d), F32), jax.ShapeDtypeStruct((8, 128), F32)], args=(y, tgt))


def _rev(n_tiles):
    return lambda i: n_tiles - 1 - i


def _halo_spec(ts, n_tiles, halo, width, col):
    per = ts // halo
    return pl.BlockSpec((halo, width), lambda i: (jnp.maximum((n_tiles - 1 - i) * per - 1, 0), col))


def _ffn_bwd(dx2, x1, dn, up, cw, cb, w_down, w_up4, gt, ln_g, sc, *, ts, name, comm=None):
    s, d = x1.shape
    ts = _tile(ts, s)
    nt = s // ts
    rev = _rev(nt)
    wd = w_up4.shape[2]

    def w_up_cols(wu_ref, start):
        return wu_ref[start // wd, :, start % wd:start % wd + FF_CHUNK]

    def body(dx2_ref, x1_ref, dn_ref, up_ref, halo_ref, cw_ref, cb_ref, wd_ref, wu_ref, gt_ref, g_ref, sc_ref,
             ddn_ref, dup_ref, dx1_ref, redd_ref, redf_ref, dbup_ref, carry):
        i = pl.program_id(0)

        @pl.when(i == 0)
        def _():
            carry[...] = jnp.zeros_like(carry)
            redd_ref[...] = jnp.zeros_like(redd_ref)
            redf_ref[...] = jnp.zeros_like(redf_ref)
            dbup_ref[...] = jnp.zeros_like(dbup_ref)
        first_tile = i == nt - 1
        x1v, dnv, dyv = x1_ref[...], dn_ref[...], dx2_ref[...]
        xhat, rstd = _ln_fwd(ALPHA * x1v + gt_ref[...] * dnv)
        dr = _ln_bwd(dyv, g_ref[...], xhat, rstd)
        redd_ref[0:1, :] += _colsum(dyv * xhat)
        redd_ref[1:2, :] += _colsum(dyv)
        redd_ref[2:3, :] += _colsum(dr * dnv)
        ddn = (gt_ref[...] * dr).astype(BF16)
        ddn_ref[...] = ddn
        dh = jnp.zeros((ts, d), F32)
        for c in range(D_FF // FF_CHUNK):
            ca = slice(c * FF_CHUNK, (c + 1) * FF_CHUNK)
            cg = slice(D_FF + c * FF_CHUNK, D_FF + (c + 1) * FF_CHUNK)
            df = _dot_nt(ddn, wd_ref[ca, :])
            ua, ug = up_ref[:, ca], up_ref[:, cg]
            halo = jnp.where(first_tile, 0.0, halo_ref[:, ca])
            ext = jnp.concatenate([halo, ua], axis=0)
            u1, u2 = _rows_before(ext, 1, CONV_HALO), _rows_before(ext, 2, CONV_HALO)
            cwc = cw_ref[:, ca]
            gl, dgl = _gelu_and_grad(cwc[2:3] * ua + cwc[1:2] * u1 + cwc[0:1] * u2 + cb_ref[:, ca])
            dug = df * gl
            dcf = df * ug * dgl
            redf_ref[0:1, ca] += _colsum(dcf * u2)
            redf_ref[1:2, ca] += _colsum(dcf * u1)
            redf_ref[2:3, ca] += _colsum(dcf * ua)
            redf_ref[3:4, ca] += _colsum(dcf)
            extd = jnp.concatenate([dcf, carry[:, ca]], axis=0)
            carry[:, ca] = dcf[:CONV_HALO]
            dua = cwc[2:3] * dcf + cwc[1:2] * _rows_after(extd, 1, ts) + cwc[0:1] * _rows_after(extd, 2, ts)
            dbup_ref[0:1, ca] += _colsum(dua)
            dbup_ref[0:1, cg] += _colsum(dug)
            dua_b, dug_b = dua.astype(BF16), dug.astype(BF16)
            dup_ref[:, ca] = dua_b
            dup_ref[:, cg] = dug_b
            dh = dh + _dot_nt(dua_b, w_up_cols(wu_ref, c * FF_CHUNK)) + _dot_nt(dug_b, w_up_cols(wu_ref, D_FF + c * FF_CHUNK))
        dx1_ref[...] = ALPHA * dr + dh * (1.0 + sc_ref[...])
        redd_ref[3:4, :] += _colsum(dh * x1v)
        redd_ref[4:5, :] += _colsum(dh)

    row = pl.BlockSpec((ts, d), lambda i: (rev(i), 0))
    return _run(
        body, name=name, grid=(nt,),
        in_specs=[row, row, row, pl.BlockSpec((ts, 2 * D_FF), lambda i: (rev(i), 0)),
                  _halo_spec(ts, nt, CONV_HALO, D_FF, 0), _const((3, D_FF)), _const((1, D_FF)),
                  _resident((D_FF, d)), _resident((N_CHIPS, d, wd)), _const((1, d)), _const((1, d)), _const((1, d))],
        out_specs=[row, pl.BlockSpec((ts, 2 * D_FF), lambda i: (rev(i), 0)), row,
                   _const((8, d)), _const((8, D_FF)), _const((8, 2 * D_FF))],
        out_shape=[jax.ShapeDtypeStruct((s, d), BF16), jax.ShapeDtypeStruct((s, 2 * D_FF), BF16),
                   jax.ShapeDtypeStruct((s, d), F32), jax.ShapeDtypeStruct((8, d), F32),
                   jax.ShapeDtypeStruct((8, D_FF), F32), jax.ShapeDtypeStruct((8, 2 * D_FF), F32)],
        scratch_shapes=[pltpu.VMEM((CONV_HALO, D_FF), F32)],
        args=(dx2, x1, dn, up, up, cw, cb, w_down, w_up4, gt, ln_g, sc), comm=comm)


def _grad_matmul(xa, dy, *, ts, tn, name, mod=None, by_chip=False, comm=None):
    s, k = xa.shape
    n = dy.shape[1]
    ts = _tile(ts, s)
    nt = s // ts

    def body(*refs):
        if mod is None:
            xa_ref, dy_ref, o_ref, ob_ref = refs
            a = xa_ref[...]
        else:
            xa_ref, sc_ref, sh_ref, dy_ref, o_ref, ob_ref = refs
            a = (xa_ref[...] * (1.0 + sc_ref[...]) + sh_ref[...]).astype(BF16)
        t = pl.program_id(1)

        @pl.when(t == 0)
        def _():
            o_ref[...] = jnp.zeros_like(o_ref)
        o_ref[...] += _dot_tn(a, dy_ref[...]).reshape(o_ref.shape)

        @pl.when(t == nt - 1)
        def _():
            ob_ref[...] = o_ref[...].astype(BF16)

    xspec = pl.BlockSpec((ts, k), lambda j, t: (t, 0))
    dspec = pl.BlockSpec((ts, tn), lambda j, t: (t, j))
    in_specs = [xspec, dspec] if mod is None else [xspec, _const((1, k)), _const((1, k)), dspec]
    args = (xa, dy) if mod is None else (xa, mod[0], mod[1], dy)
    if by_chip:
        per = n // N_CHIPS // tn
        ospec = pl.BlockSpec((1, k, tn), lambda j, t: (j // per, 0, j % per))
        shape = (N_CHIPS, k, n // N_CHIPS)
    else:
        ospec = pl.BlockSpec((k, tn), lambda j, t: (0, j))
        shape = (k, n)
    return _run(body, name=name, grid=(n // tn, nt), in_specs=in_specs, out_specs=[ospec, ospec],
                out_shape=[jax.ShapeDtypeStruct(shape, F32), jax.ShapeDtypeStruct(shape, BF16)], args=args, comm=comm)


def _mix_o_bwd(dx1, x, o, z, ya, yb, yc, w_o, gt, ln_g, *, ts, name, comm=None):
    s, d = x.shape
    ts = _tile(ts, s)

    def body(dx1_ref, x_ref, o_ref, ga_ref, gb_ref, gc_ref, ya_ref, yb_ref, yc_ref, w_ref, gt_ref, g_ref,
             do_ref, dxa_ref, dzg_ref, dya_ref, dyb_ref, dyc_ref, red_ref):
        @pl.when(pl.program_id(0) == 0)
        def _():
            red_ref[...] = jnp.zeros_like(red_ref)
        dyv, ov = dx1_ref[...], o_ref[...]
        xhat, rstd = _ln_fwd(ALPHA * x_ref[...] + gt_ref[...] * ov)
        dr = _ln_bwd(dyv, g_ref[...], xhat, rstd)
        red_ref[0:1, :] += _colsum(dyv * xhat)
        red_ref[1:2, :] += _colsum(dyv)
        red_ref[2:3, :] += _colsum(dr * ov)
        dxa_ref[...] = ALPHA * dr
        d_o = (gt_ref[...] * dr).astype(BF16)
        do_ref[...] = d_o
        dm = _dot_nt(d_o, w_ref[...])
        for k, (zg_ref, y_ref, dy_ref) in enumerate(((ga_ref, ya_ref, dya_ref), (gb_ref, yb_ref, dyb_ref),
                                                     (gc_ref, yc_ref, dyc_ref))):
            g = _sigmoid(zg_ref[...])
            dzg_ref[:, k * d:(k + 1) * d] = (dm * y_ref[...] * g * (1.0 - g)).astype(BF16)
            dy_ref[...] = (dm * g).astype(BF16)

    row = pl.BlockSpec((ts, d), lambda i: (i, 0))
    zspec = lambda k: pl.BlockSpec((ts, d), lambda i, k=k: (i, k))
    bf = jax.ShapeDtypeStruct((s, d), BF16)
    return _run(
        body, name=name, grid=(s // ts,),
        in_specs=[row, row, row, zspec(6), zspec(7), zspec(8), row, row, row, _const((d, d)),
                  _const((1, d)), _const((1, d))],
        out_specs=[row, row, pl.BlockSpec((ts, 3 * d), lambda i: (i, 2)), row, row, row, _const((8, d))],
        out_shape=[bf, jax.ShapeDtypeStruct((s, d), F32), jax.ShapeDtypeStruct((s, D_Z), BF16), bf, bf, bf,
                   jax.ShapeDtypeStruct((8, d), F32)],
        args=(dx1, x, o, z, z, z, ya, yb, yc, w_o, gt, ln_g), comm=comm)


def _mix_a_bwd(dya, z, dz, cw, w_out, *, ts, name, comm=None):
    s = z.shape[0]
    d = D_MODEL
    ts = _tile(ts, s)
    nt = s // ts
    rev = _rev(nt)

    def body(dya_ref, zb_ref, zc_ref, zx_ref, hc_ref, hx_ref, cw_ref, w_ref, dz_in, dz_ref, red_ref, carry):
        i = pl.program_id(0)

        @pl.when(i == 0)
        def _():
            carry[...] = jnp.zeros_like(carry)
            red_ref[...] = jnp.zeros_like(red_ref)
        zb, zc, zx = zb_ref[...], zc_ref[...], zx_ref[...]
        q = zc * zx
        halo = jnp.where(i == nt - 1, 0.0, hc_ref[...] * hx_ref[...])
        ext = jnp.concatenate([halo, q], axis=0)
        q1, q2 = _rows_before(ext, 1, CONV_HALO), _rows_before(ext, 2, CONV_HALO)
        cwv = cw_ref[...]
        cv = cwv[2:3] * q + cwv[1:2] * q1 + cwv[0:1] * q2
        da = _dot_nt(dya_ref[...], w_ref[...])
        dcv = da * zb
        red_ref[0:1, :] += _colsum(dcv * q2)
        red_ref[1:2, :] += _colsum(dcv * q1)
        red_ref[2:3, :] += _colsum(dcv * q)
        extd = jnp.concatenate([dcv, carry[...]], axis=0)
        carry[...] = dcv[:CONV_HALO]
        dq = cwv[2:3] * dcv + cwv[1:2] * _rows_after(extd, 1, ts) + cwv[0:1] * _rows_after(extd, 2, ts)
        dz_ref[:, 0:d] = (da * cv).astype(BF16)
        dz_ref[:, d:2 * d] = (dq * zx).astype(BF16)
        dz_ref[:, 2 * d:3 * d] = (dq * zc).astype(BF16)

    zspec = lambda k: pl.BlockSpec((ts, d), lambda i, k=k: (rev(i), k))
    return _run(
        body, name=name, grid=(nt,),
        in_specs=[pl.BlockSpec((ts, d), lambda i: (rev(i), 0)), zspec(0), zspec(1), zspec(2),
                  _halo_spec(ts, nt, CONV_HALO, d, 1), _halo_spec(ts, nt, CONV_HALO, d, 2),
                  _const((3, d)), _const((d, d)), HBM],
        out_specs=[pl.BlockSpec((ts, 3 * d), lambda i: (rev(i), 0)), _const((8, d))],
        out_shape=[jax.ShapeDtypeStruct((s, D_Z), BF16), jax.ShapeDtypeStruct((8, d), F32)],
        scratch_shapes=[pltpu.VMEM((CONV_HALO, d), F32)],
        args=(dya, z, z, z, z, z, cw, w_out, dz), aliases={8: 0}, comm=comm)


def _mix_b_bwd(dyb, z, dz, ln_g, ln_b, ws, bst, w_out, *, ts, name, comm=None):
    s = z.shape[0]
    d = D_MODEL
    ts = _tile(ts, s)
    nblk = ts // GBLK

    def body(dyb_ref, zu_ref, zv_ref, g_ref, b_ref, ws_ref, bst_ref, w_ref, dz_in,
             dz_ref, red_ref, dws_ref, dbst_ref, mixed_scr, dvn_scr, dzv_scr):
        @pl.when((pl.program_id(0) == 0) & (pl.program_id(1) == 0))
        def _():
            red_ref[...] = jnp.zeros_like(red_ref)
            dws_ref[...] = jnp.zeros_like(dws_ref)
            dbst_ref[...] = jnp.zeros_like(dbst_ref)

        @pl.when(pl.program_id(1) == 0)
        def _():
            u, du_dz = _gelu_and_grad(zu_ref[...])
            vg, dv_dz = _gelu_and_grad(zv_ref[...])
            xhat, rstd = _ln_fwd(vg)
            vn = (xhat * g_ref[...] + b_ref[...]).astype(BF16)
            _spatial_mix(vn, ws_ref, bst_ref, mixed_scr, ts)
            dsg = _dot_nt(dyb_ref[...], w_ref[...])
            dz_ref[...] = (dsg * mixed_scr[...] * du_dz).astype(BF16)
            dmix = dsg * u
            mask = _spatial_mask()
            for g in range(N_GROUPS):
                cols = slice(g * GBLK, (g + 1) * GBLK)
                wm = jnp.where(mask, ws_ref[g], 0.0).astype(BF16)
                dm_cat = jnp.concatenate([dmix[n * GBLK:(n + 1) * GBLK, cols] for n in range(nblk)], axis=1)
                vn_cat = jnp.concatenate([vn[n * GBLK:(n + 1) * GBLK, cols] for n in range(nblk)], axis=1)
                dm_b = dm_cat.astype(BF16)
                dbst_ref[:, g:g + 1] += jnp.sum(dm_cat, axis=1, keepdims=True)
                dws_ref[g] += jnp.where(mask, _dot_nt(dm_b, vn_cat), 0.0)
                dvn_cat = _dot_tn(wm, dm_b)
                for n in range(nblk):
                    dvn_scr[n * GBLK:(n + 1) * GBLK, cols] = dvn_cat[:, n * GBLK:(n + 1) * GBLK]
            dvn = dvn_scr[...]
            red_ref[0:1, :] += _colsum(dvn * xhat)
            red_ref[1:2, :] += _colsum(dvn)
            dzv_scr[...] = (_ln_bwd(dvn, g_ref[...], xhat, rstd) * dv_dz).astype(BF16)

        @pl.when(pl.program_id(1) == 1)
        def _():
            dz_ref[...] = dzv_scr[...]

    zspec = lambda k: pl.BlockSpec((ts, d), lambda i, h, k=k: (i, k))
    return _run(
        body, name=name, grid=(s // ts, 2),
        in_specs=[pl.BlockSpec((ts, d), lambda i, h: (i, 0)), zspec(3), zspec(4), _const((1, d)), _const((1, d)),
                  _const((N_GROUPS, GBLK, GBLK)), _const((GBLK, N_GROUPS)), _const((d, d)), HBM],
        out_specs=[pl.BlockSpec((ts, d), lambda i, h: (i, 3 + h)), _const((8, d)),
                   _const((N_GROUPS, GBLK, GBLK)), _const((GBLK, N_GROUPS))],
        out_shape=[jax.ShapeDtypeStruct((s, D_Z), BF16), jax.ShapeDtypeStruct((8, d), F32),
                   jax.ShapeDtypeStruct((N_GROUPS, GBLK, GBLK), F32), jax.ShapeDtypeStruct((GBLK, N_GROUPS), F32)],
        scratch_shapes=[pltpu.VMEM((ts, d), F32), pltpu.VMEM((ts, d), F32), pltpu.VMEM((ts, d), BF16)],
        args=(dyb, z, z, ln_g, ln_b, ws, bst, w_out, dz), aliases={8: 0}, comm=comm)


def _mix_c_bwd(dyc, z, dz, w_pool, scale, *, ts, name):
    s = z.shape[0]
    d = D_MODEL
    ts = _tile(ts, s)
    nt = s // ts
    rev = _rev(nt)

    def body(dyc_ref, zp_ref, halo_ref, w_ref, sc_ref, dz_in, dz_ref, red_ref, dw_ref, carry):
        i = pl.program_id(0)

        @pl.when(i == 0)
        def _():
            carry[...] = jnp.zeros_like(carry)
            red_ref[...] = jnp.zeros_like(red_ref)
            dw_ref[...] = jnp.zeros_like(dw_ref)
        p = zp_ref[...]
        ext = jnp.concatenate([jnp.where(i == nt - 1, 0.0, halo_ref[...]), p], axis=0)
        denoms = _pool_denoms(rev(i), ts)
        dyv = dyc_ref[...].astype(F32)
        for k in range(len(POOL_WINDOWS)):
            cols = slice(k * POOL_GROUP, (k + 1) * POOL_GROUP)
            dk = _pool_diff(p, ext, denoms, k).astype(BF16)
            red_ref[0:1, cols] += _colsum(dyv[:, cols] * _dot(dk, w_ref[k]))
            dpre = (dyv[:, cols] * sc_ref[:, cols]).astype(BF16)
            dw_ref[k] += _dot_tn(dk, dpre)
            dd = _dot_nt(dpre, w_ref[k])
            e = dd / denoms[k]
            acc = jnp.concatenate([e, carry[:, cols]], axis=0)
            carry[:, cols] = e[:POOL_HALO]
            step = 1
            while step < POOL_WINDOWS[k]:
                acc = acc + pltpu.roll(acc, acc.shape[0] - step, 0)
                step *= 2
            dz_ref[:, cols] = (acc[:ts] - dd).astype(BF16)

    return _run(
        body, name=name, grid=(nt,),
        in_specs=[pl.BlockSpec((ts, d), lambda i: (rev(i), 0)), pl.BlockSpec((ts, d), lambda i: (rev(i), 5)),
                  _halo_spec(ts, nt, POOL_HALO, d, 5), _const((4, POOL_GROUP, POOL_GROUP)), _const((1, d)), HBM],
        out_specs=[pl.BlockSpec((ts, d), lambda i: (rev(i), 5)), _const((8, d)), _const((4, POOL_GROUP, POOL_GROUP))],
        out_shape=[jax.ShapeDtypeStruct((s, D_Z), BF16), jax.ShapeDtypeStruct((8, d), F32),
                   jax.ShapeDtypeStruct((4, POOL_GROUP, POOL_GROUP), F32)],
        scratch_shapes=[pltpu.VMEM((POOL_HALO, d), F32)],
        args=(dyc, z, z, w_pool, scale, dz), aliases={5: 0})


def _in_proj_bwd(dz, w4, dxa, x, sc, *, ts, name, comm=None):
    s, d = x.shape
    ts = _tile(ts, s)
    wd = w4.shape[2]

    def body(dz_ref, w_ref, dxa_ref, x_ref, sc_ref, dx_ref, red_ref, db_ref):
        @pl.when(pl.program_id(0) == 0)
        def _():
            red_ref[...] = jnp.zeros_like(red_ref)
            db_ref[...] = jnp.zeros_like(db_ref)
        dh = jnp.zeros((ts, d), F32)
        for j in range(N_CHIPS):
            dzj = dz_ref[:, j * wd:(j + 1) * wd]
            db_ref[0:1, j * wd:(j + 1) * wd] += _colsum(dzj.astype(F32))
            dh = dh + _dot_nt(dzj, w_ref[j])
        dx_ref[...] = dxa_ref[...] + dh * (1.0 + sc_ref[...])
        red_ref[0:1, :] += _colsum(dh * x_ref[...])
        red_ref[1:2, :] += _colsum(dh)

    row = pl.BlockSpec((ts, d), lambda i: (i, 0))
    return _run(
        body, name=name, grid=(s // ts,),
        in_specs=[pl.BlockSpec((ts, D_Z), lambda i: (i, 0)), _resident((N_CHIPS, d, wd)), row, row, _const((1, d))],
        out_specs=[row, _const((8, d)), _const((8, D_Z))],
        out_shape=[jax.ShapeDtypeStruct((s, d), F32), jax.ShapeDtypeStruct((8, d), F32),
                   jax.ShapeDtypeStruct((8, D_Z), F32)],
        args=(dz, w4, dxa, x, sc), comm=comm)


def _ada_fwd(c_all, w_ada, b_ada, *, name):
    nl, d, n = w_ada.shape
    tn = n // 2

    def body(c_ref, w_ref, b_ref, o_ref):
        cv = c_ref[...]
        ca = (cv * _sigmoid(cv)).astype(BF16)
        o_ref[0] = _dot(ca, w_ref[0].astype(BF16)) + b_ref[0]

    return _run(
        body, name=name, grid=(nl, n // tn),
        in_specs=[_const((N_DEV, d)), pl.BlockSpec((1, d, tn), lambda l, j: (l, 0, j)),
                  pl.BlockSpec((1, 1, tn), lambda l, j: (l, 0, j))],
        out_specs=[pl.BlockSpec((1, N_DEV, tn), lambda l, j: (l, 0, j))],
        out_shape=[jax.ShapeDtypeStruct((nl, N_DEV, n), F32)], args=(c_all, w_ada, b_ada))[0]


def _ada_bwd(c_all, dada, *, name):
    nl, nb, n = dada.shape
    d = c_all.shape[1]
    tn = n // 2

    def body(c_ref, g_ref, o_ref):
        cv = c_ref[...]
        ca = (cv * _sigmoid(cv)).astype(BF16)
        o_ref[0] = _dot_tn(ca, g_ref[0].astype(BF16))

    return _run(
        body, name=name, grid=(nl, n // tn),
        in_specs=[_const((nb, d)), pl.BlockSpec((1, nb, tn), lambda l, j: (l, 0, j))],
        out_specs=[pl.BlockSpec((1, d, tn), lambda l, j: (l, 0, j))],
        out_shape=[jax.ShapeDtypeStruct((nl, d, n), F32)], args=(c_all, dada))[0]


def _sum4(own, recv, *, name):
    r, c = own.shape
    tr = _row_tile(r, c, 2)

    def body(own_ref, recv_ref, o_ref):
        acc = own_ref[...]
        for k in range(N_CHIPS - 1):
            acc = acc + recv_ref[k].astype(F32)
        o_ref[...] = acc

    return _run(
        body, name=name, grid=(r // tr,),
        in_specs=[pl.BlockSpec((tr, c), lambda i: (i, 0)), pl.BlockSpec((N_CHIPS - 1, tr, c), lambda i: (0, i, 0))],
        out_specs=[pl.BlockSpec((tr, c), lambda i: (i, 0))], out_shape=[jax.ShapeDtypeStruct((r, c), F32)],
        args=(own, recv))[0]


def _sum_halves(g_f32, theirs, *, name):
    _, _, rh, c = g_f32.shape
    tr = _row_tile(rh, c, 1)

    def body(g_ref, t_ref, hb_ref, own_ref):
        mx, my, mc = _my_coords()
        j = pl.program_id(1)
        h = jnp.where(mc == 0, g_ref[0, 0], g_ref[0, 1]) + t_ref[0].astype(F32)
        hb_ref[0] = h.astype(BF16)

        @pl.when(j == 2 * mx + my)
        def _():
            own_ref[...] = h

    return _run(
        body, name=name, grid=(rh // tr, N_CHIPS),
        in_specs=[pl.BlockSpec((1, 2, tr, c), lambda i, j: (j, 0, i, 0)), pl.BlockSpec((1, tr, c), lambda i, j: (j, i, 0))],
        out_specs=[pl.BlockSpec((1, tr, c), lambda i, j: (j, i, 0)), pl.BlockSpec((tr, c), lambda i, j: (i, 0))],
        out_shape=[jax.ShapeDtypeStruct((N_CHIPS, rh, c), BF16), jax.ShapeDtypeStruct((rh, c), F32)],
        args=(g_f32, theirs))


def _row_tile(r, c, mib):
    limit = max(8, (mib << 20) // (4 * c))
    if r <= limit:
        return r
    best = 8
    for t in range(8, limit + 1, 8):
        if r % t == 0:
            best = t
    return best


def _adam_math(w, g, m, v):
    mn = ADAM_B1 * m + (1.0 - ADAM_B1) * g
    vn = ADAM_B2 * v + (1.0 - ADAM_B2) * (g * g)
    m_hat = mn / (1.0 - ADAM_B1 ** ADAM_STEP)
    v_hat = vn / (1.0 - ADAM_B2 ** ADAM_STEP)
    return -ADAM_LR * (m_hat / (jnp.sqrt(v_hat) + ADAM_EPS) + ADAM_WD * w), mn, vn


def _adamw(w, g, m, v, *, name):
    r, c = w.shape
    tr = _row_tile(r, c, 2)

    def body(w_ref, g_ref, m_ref, v_ref, d_ref, mo_ref, vo_ref):
        d_ref[...], mo_ref[...], vo_ref[...] = _adam_math(w_ref[...], g_ref[...], m_ref[...], v_ref[...])

    blk = pl.BlockSpec((tr, c), lambda i: (i, 0))
    return _run(body, name=name, grid=(r // tr,), in_specs=[blk] * 4, out_specs=[blk] * 3,
                out_shape=[jax.ShapeDtypeStruct((r, c), F32)] * 3, args=(w, g, m, v))


def _adamw_sharded(w, m, v, grads, *, name, comm=None):
    nl, r, c = w.shape
    tr = _row_tile(r, c, 1)
    nt = r // tr

    def body(w_ref, m_ref, v_ref, g0_ref, g1_ref, g_ref, d_ref, mo_ref, vo_ref):
        g = jnp.where(pl.program_id(0) == 0, g0_ref[...], g1_ref[...])
        g_ref[0] = g
        d_ref[0], mo_ref[0], vo_ref[0] = _adam_math(w_ref[0], g, m_ref[0], v_ref[0])

    blk = pl.BlockSpec((1, tr, c), lambda l, i: (l, i, 0))
    part0 = pl.BlockSpec((tr, c), lambda l, i: (jnp.where(l == 0, i, nt - 1), 0))
    part1 = pl.BlockSpec((tr, c), lambda l, i: (jnp.where(l == 1, i, 0), 0))
    return _run(body, name=name, grid=(nl, nt), in_specs=[blk] * 3 + [part0, part1],
                out_specs=[blk] * 4, out_shape=[jax.ShapeDtypeStruct((nl, r, c), F32)] * 4,
                args=(w, m, v, grads[0], grads[1]), comm=comm)


_BIG = ("w_in", "w_a_out", "w_b_out", "w_pool", "w_o", "w_up", "w_down")
_COL_SHARDED = ("w_in", "w_up")
_SMALL_SHARDED = ("conv_a", "conv_ffn")
_SMALL_REPL = ("b_in", "ln_v_g", "ln_v_b", "w_spatial", "b_spatial", "pool_scale", "ln1_g", "ln1_b", "b_up",
               "conv_ffn_b", "ln2_g", "ln2_b")
_WEIGHTS = ("w_ada", "b_ada", "w_in", "b_in", "conv_a", "w_a_out", "ln_v_g", "ln_v_b", "w_spatial", "b_spatial",
            "w_b_out", "w_pool", "pool_scale", "w_o", "ln1_g", "ln1_b", "w_up", "b_up", "conv_ffn", "conv_ffn_b",
            "w_down", "ln2_g", "ln2_b")


def _shard3(a):
    return a.reshape(a.shape[0], -1, a.shape[-1])


def _use_gathered(name, g):
    g = g.reshape(N_CHIPS, -1, g.shape[-1])
    if name in _COL_SHARDED:
        return g
    if name == "w_pool":
        return g.reshape(N_CHIPS, 4, POOL_GROUP // N_CHIPS, POOL_GROUP).transpose(1, 0, 2, 3).reshape(
            4, POOL_GROUP, POOL_GROUP)
    return g.reshape(-1, g.shape[-1])


def _grad_by_chip(name, g):
    if name in _COL_SHARDED:
        return g
    if name == "w_pool":
        return g.reshape(4, N_CHIPS, POOL_GROUP // N_CHIPS, POOL_GROUP).transpose(1, 0, 2, 3).reshape(
            N_CHIPS, POOL_GROUP, POOL_GROUP)
    return g.reshape(N_CHIPS, -1, g.shape[-1])


def _pack_small(arrs):
    parts = []
    for a in arrs:
        flat = a.reshape(-1).astype(F32)
        pad = (-flat.shape[0]) % 128
        parts.append(jnp.pad(flat, (0, pad)) if pad else flat)
    flat = jnp.concatenate(parts)
    pad = (-flat.shape[0]) % 1024
    if pad:
        flat = jnp.pad(flat, (0, pad))
    return flat.reshape(-1, 128)


def _unpack_small(buf, shapes):
    lead = buf.shape[:-2]
    flat = buf.reshape(lead + (-1,))
    out, off = [], 0
    for shp in shapes:
        n = math.prod(shp)
        out.append(flat[..., off:off + n].reshape(lead + tuple(shp)))
        off += n + ((-n) % 128)
    return out


def _as2d(a):
    return a.reshape(-1, a.shape[-1])


_LATE = ("w_a_out", "w_b_out", "w_pool", "w_o")


class _Traffic:
    def __init__(self, halves, plan):
        self.halves = halves
        self.plan = plan
        self.gathered = {}
        self.ready = {}
        self.summed = {}
        self.half = {}
        self.final = {}

    def weight(self, layer, name):
        return self.gathered[(layer, name)]

    def add_grad(self, layer, name, g_f32, g_bf16):
        def halves(g):
            g = _grad_by_chip(name, g)
            return g.reshape(N_CHIPS, 2, g.shape[1] // 2, g.shape[2])
        self.ready[(layer, name)] = (halves(g_f32), halves(g_bf16))

    def _comm(self, job):
        if job[0] == "gather":
            return _gather_comm([self.halves[k] for k in job[2]], job[1])
        if job[0] == "presum":
            return _presum_comm([self.ready[k][1] for k in job[1]])
        if job[0] == "scatter":
            return _scatter_comm([self.summed[k][0] for k in job[1]])
        return _join_comm([self.half[k] for k in job[1]])

    def _done(self, job, res):
        if job[0] == "gather":
            for k, r in zip(job[2], res):
                self.gathered[(job[1], k)] = _use_gathered(k, r)
        elif job[0] == "presum":
            for k, r in zip(job[1], res):
                self.summed[k] = _sum_halves(self.ready.pop(k)[0], r, name=f"presum_l{k[0]}_{k[1]}")
        elif job[0] == "scatter":
            for k, r in zip(job[1], res):
                self.half[k] = _sum4(self.summed.pop(k)[1], r, name=f"sum_l{k[0]}_{k[1]}")
        else:
            for k, r in zip(job[1], res):
                self.final[k] = r.reshape(-1, r.shape[-1])

    def run(self, name, fn):
        jobs = self.plan.get(name)
        if not jobs:
            return fn(None)
        comms = [self._comm(j) for j in jobs]
        outs, res = fn(_merge(comms))
        for job, r in zip(jobs, _split(comms, res)):
            self._done(job, r)
        return outs

    def alone(self, name):
        jobs = self.plan[name]
        comms = [self._comm(j) for j in jobs]
        for job, r in zip(jobs, _split(comms, _comm_call(_merge(comms), name=name))):
            self._done(job, r)


def _layer_fwd(x, ada, p, l, tr):
    sh1, sc1, gt1, sh2, sc2, gt2 = ada
    n = f"l{l}"
    z, = tr.run(f"{n}_in_proj", lambda cm: _mod_matmul(
        x, sc1, sh1, tr.weight(l, "w_in"), p["b_in"], ts=1024, tn=2304, name=f"{n}_in_proj", comm=cm))
    a, ya = _mix_a_fwd(z, p["conv_a"], tr.weight(l, "w_a_out"), ts=256, name=f"{n}_mix_a")
    sg, yb = tr.run(f"{n}_mix_b", lambda cm: _mix_b_fwd(
        z, p["ln_v_g"], p["ln_v_b"], p["w_spatial"], p["b_spatial_t"], tr.weight(l, "w_b_out"), ts=256,
        name=f"{n}_mix_b", comm=cm))
    dpool, yc = _mix_c_fwd(z, tr.weight(l, "w_pool"), p["pool_scale"], ts=256, name=f"{n}_mix_c")
    merged, o, x1 = tr.run(f"{n}_mix_o", lambda cm: _mix_o_fwd(
        x, z, ya, yb, yc, tr.weight(l, "w_o"), gt1, p["ln1_g"], p["ln1_b"], ts=256, name=f"{n}_mix_o", comm=cm))
    up, = tr.run(f"{n}_up_proj", lambda cm: _mod_matmul(
        x1, sc2, sh2, tr.weight(l, "w_up"), p["b_up"], ts=1024, tn=1408, name=f"{n}_up_proj", comm=cm))
    f, dn, x2 = tr.run(f"{n}_ffn", lambda cm: _ffn_fwd(
        up, x1, p["conv_ffn"], p["conv_ffn_b"], tr.weight(l, "w_down"), gt2, p["ln2_g"], p["ln2_b"], ts=256,
        name=f"{n}_ffn", comm=cm))
    saved = dict(x=x, z=z, a=a, ya=ya, sg=sg, yb=yb, dpool=dpool, yc=yc, merged=merged, o=o, x1=x1, up=up, f=f, dn=dn)
    return x2, saved


def _layer_bwd(dx2, ada, p, sv, l, tr):
    sh1, sc1, gt1, sh2, sc2, gt2 = ada
    n = f"l{l}"
    ddn, dup, dx1, red_d, red_f, dbup = tr.run(f"{n}_ffn_bwd", lambda cm: _ffn_bwd(
        dx2, sv["x1"], sv["dn"], sv["up"], p["conv_ffn"], p["conv_ffn_b"], tr.weight(l, "w_down"),
        tr.weight(l, "w_up"), gt2, p["ln2_g"], sc2, ts=256, name=f"{n}_ffn_bwd", comm=cm))
    g = {}
    tr.add_grad(l, "w_down", *_grad_matmul(sv["f"], ddn, ts=512, tn=512, name=f"{n}_dw_down"))
    tr.add_grad(l, "w_up", *tr.run(f"{n}_dw_up", lambda cm: _grad_matmul(
        sv["x1"], dup, ts=512, tn=1408, name=f"{n}_dw_up", mod=(sc2, sh2), by_chip=True, comm=cm)))
    g["ln2_g"], g["ln2_b"] = red_d[0], red_d[1]
    g["conv_ffn"], g["conv_ffn_b"], g["b_up"] = red_f[0:3], red_f[3], dbup[0]

    d_o, dxa, dz, dya, dyb, dyc, red_o = tr.run(f"{n}_mix_o_bwd", lambda cm: _mix_o_bwd(
        dx1, sv["x"], sv["o"], sv["z"], sv["ya"], sv["yb"], sv["yc"], tr.weight(l, "w_o"), gt1, p["ln1_g"], ts=256,
        name=f"{n}_mix_o_bwd", comm=cm))
    tr.add_grad(l, "w_o", *_grad_matmul(sv["merged"], d_o, ts=512, tn=1024, name=f"{n}_dw_o"))
    g["ln1_g"], g["ln1_b"] = red_o[0], red_o[1]

    dz, red_a = tr.run(f"{n}_mix_a_bwd", lambda cm: _mix_a_bwd(
        dya, sv["z"], dz, p["conv_a"], tr.weight(l, "w_a_out"), ts=256, name=f"{n}_mix_a_bwd", comm=cm))
    tr.add_grad(l, "w_a_out", *_grad_matmul(sv["a"], dya, ts=512, tn=1024, name=f"{n}_dw_a_out"))
    g["conv_a"] = red_a[0:3]

    dz, red_b, dws, dbst = tr.run(f"{n}_mix_b_bwd", lambda cm: _mix_b_bwd(
        dyb, sv["z"], dz, p["ln_v_g"], p["ln_v_b"], p["w_spatial"], p["b_spatial_t"], tr.weight(l, "w_b_out"), ts=256,
        name=f"{n}_mix_b_bwd", comm=cm))
    tr.add_grad(l, "w_b_out", *_grad_matmul(sv["sg"], dyb, ts=512, tn=1024, name=f"{n}_dw_b_out"))
    g["ln_v_g"], g["ln_v_b"], g["w_spatial"], g["b_spatial"] = red_b[0], red_b[1], dws, dbst.T

    dz, red_c, dwp = _mix_c_bwd(dyc, sv["z"], dz, tr.weight(l, "w_pool"), p["pool_scale"], ts=256,
                                name=f"{n}_mix_c_bwd")
    g["pool_scale"] = red_c[0]
    tr.add_grad(l, "w_pool", dwp, dwp.astype(BF16))

    tr.add_grad(l, "w_in", *tr.run(f"{n}_dw_in", lambda cm: _grad_matmul(
        sv["x"], dz, ts=512, tn=2304, name=f"{n}_dw_in", mod=(sc1, sh1), by_chip=True, comm=cm)))
    if f"{n}_presum_tail" in tr.plan:
        tr.alone(f"{n}_presum_tail")
    dx, red_i, dbin = tr.run(f"{n}_in_proj_bwd", lambda cm: _in_proj_bwd(
        dz, tr.weight(l, "w_in"), dxa, sv["x"], sc1, ts=256, name=f"{n}_in_proj_bwd", comm=cm))
    g["b_in"] = dbin[0]
    dada = jnp.stack([red_i[1], red_i[0], red_o[2], red_d[4], red_d[3], red_d[2]])
    return dx, g, dada


def _traffic_plan():
    plan = {
        "gather_l0": [("gather", 0, ("w_in",) + _LATE)],
        "l0_in_proj": [("gather", 1, ("w_in",))],
        "l0_mix_b": [("gather", 0, ("w_down",))],
        "l0_mix_o": [("gather", 0, ("w_up",))],
        "l0_up_proj": [("gather", 1, _LATE)],
        "l0_ffn": [("gather", 1, ("w_down",))],
        "l1_in_proj": [("gather", 1, ("w_up",))],
    }
    for l in reversed(range(DEPTH)):
        late = [(l, k) for k in _LATE]
        plan.update({
            f"l{l}_dw_up": [("presum", [(l, "w_down")])],
            f"l{l}_mix_o_bwd": [("presum", [(l, "w_up")]), ("scatter", [(l, "w_down")])],
            f"l{l}_mix_b_bwd": [("scatter", [(l, "w_up")])],
            f"l{l}_dw_in": [("presum", late), ("join", [(l, "w_down"), (l, "w_up")])],
        })
    late0, late1 = [(0, k) for k in _LATE], [(1, k) for k in _LATE]
    plan["l1_in_proj_bwd"] = [("presum", [(1, "w_in")]), ("scatter", late1)]
    plan["l0_ffn_bwd"] = [("scatter", [(1, "w_in")]), ("join", late1)]
    plan["l0_dw_up"] = plan["l0_dw_up"] + [("join", [(1, "w_in")])]
    plan["l0_presum_tail"] = [("presum", [(0, "w_in")])]
    plan["l0_in_proj_bwd"] = [("scatter", [(0, "w_in")] + late0)]
    plan["adamw_w_up"] = [("join", [(0, "w_in")] + late0)]
    return plan


def kernel(x, c, w_ada, b_ada, w_in, b_in, conv_a, w_a_out, ln_v_g, ln_v_b, w_spatial, b_spatial, w_b_out, w_pool, pool_scale, w_o, ln1_g, ln1_b, w_up, b_up, conv_ffn, conv_ffn_b, w_down, ln2_g, ln2_b, loss_target, m_w_ada, m_b_ada, m_w_in, m_b_in, m_conv_a, m_w_a_out, m_ln_v_g, m_ln_v_b, m_w_spatial, m_b_spatial, m_w_b_out, m_w_pool, m_pool_scale, m_w_o, m_ln1_g, m_ln1_b, m_w_up, m_b_up, m_conv_ffn, m_conv_ffn_b, m_w_down, m_ln2_g, m_ln2_b, v_w_ada, v_b_ada, v_w_in, v_b_in, v_conv_a, v_w_a_out, v_ln_v_g, v_ln_v_b, v_w_spatial, v_b_spatial, v_w_b_out, v_w_pool, v_pool_scale, v_w_o, v_ln1_g, v_ln1_b, v_w_up, v_b_up, v_conv_ffn, v_conv_ffn_b, v_w_down, v_ln2_g, v_ln2_b):
    args = locals()
    w = {k: args[k] for k in _WEIGHTS}
    m = {k: args["m_" + k] for k in _WEIGHTS}
    v = {k: args["v_" + k] for k in _WEIGHTS}
    d = D_MODEL
    mx, my, mc = _my_coords()
    chip = 2 * mx + my
    me = 4 * mx + 2 * my + mc

    small_shapes = [c.shape, conv_a.shape, conv_ffn.shape]
    small_all = _all_gather8(_pack_small([c, conv_a, conv_ffn]), name="gather_small")
    c_all, conv_a_st, conv_ffn_st = _unpack_small(small_all, small_shapes)
    c_all = c_all.reshape(N_DEV, d)
    conv_full = {"conv_a": jnp.concatenate([conv_a_st[2 * j] for j in range(N_CHIPS)], axis=-1),
                 "conv_ffn": jnp.concatenate([conv_ffn_st[2 * j] for j in range(N_CHIPS)], axis=-1)}

    halves = {}
    for k in _BIG:
        s3 = _shard3(w[k]).astype(BF16)
        halves[k] = s3.reshape(s3.shape[0], 2, s3.shape[1] // 2, s3.shape[2])
    tr = _Traffic(halves, _traffic_plan())
    tr.alone("gather_l0")

    n_ada = w_ada.shape[2]
    b_ada_mine = lax.dynamic_slice_in_dim(b_ada, chip * n_ada, n_ada, axis=1)
    ada_part = _ada_fwd(c_all, w_ada, b_ada_mine.reshape(DEPTH, 1, n_ada), name="ada_fwd")
    ada_all = _all_gather8(_pack_small([ada_part]), name="gather_ada")
    ada_st = _unpack_small(ada_all, [ada_part.shape])[0][0::2]
    ada_rows = jnp.concatenate([ada_st[j] for j in range(N_CHIPS)], axis=-1)
    ada_mine = lax.dynamic_index_in_dim(ada_rows, me, axis=1, keepdims=False)

    def layer_params(l):
        p = {k: conv_full[k][l] for k in _SMALL_SHARDED}
        for k in ("b_in", "ln_v_g", "ln_v_b", "pool_scale", "ln1_g", "ln1_b", "b_up", "conv_ffn_b", "ln2_g", "ln2_b"):
            p[k] = w[k][l].reshape(1, -1)
        p["w_spatial"] = w_spatial[l]
        p["b_spatial_t"] = b_spatial[l].T
        return p

    xs = x[0]
    saved, adas, params = [], [], []
    for l in range(DEPTH):
        ada = [ada_mine[l, k * d:(k + 1) * d].reshape(1, d) for k in range(6)]
        p = layer_params(l)
        xs, sv = _layer_fwd(xs, ada, p, l, tr)
        saved.append(sv), adas.append(ada), params.append(p)
    dx, loss_blk = _loss_fwd(xs, loss_target[0], ts=512, name="loss")

    grads, dadas = [None] * DEPTH, [None] * DEPTH
    for l in reversed(range(DEPTH)):
        dx, grads[l], dadas[l] = _layer_bwd(dx, adas[l], params[l], saved[l], l, tr)
    dada = jnp.stack(dadas).reshape(DEPTH, 6 * d)

    small_names = _SMALL_REPL + _SMALL_SHARDED
    small_g = [jnp.stack([grads[l][k] for l in range(DEPTH)]) for k in small_names]
    gsum = dict(zip(small_names, _unpack_small(_all_reduce_small(_pack_small(small_g), name="reduce_small"),
                                               [a.shape for a in small_g])))
    tail_g = [dada, loss_blk[0:1, 0:1]]
    tail_all, tail_sum = _all_gather8(_pack_small(tail_g), name="gather_dada", with_sum=True)
    gsum["b_ada"], loss_sum = _unpack_small(tail_sum, [a.shape for a in tail_g])
    loss = loss_sum[0, 0]
    dada_all = _unpack_small(tail_all, [a.shape for a in tail_g])[0]
    for k in _SMALL_SHARDED:
        wd = gsum[k].shape[-1] // N_CHIPS
        gsum[k] = lax.dynamic_slice_in_dim(gsum[k], chip * wd, wd, axis=gsum[k].ndim - 1)

    dada_cols = lax.dynamic_slice_in_dim(dada_all, chip * n_ada, n_ada, axis=2)
    dada_cols = jnp.pad(jnp.swapaxes(dada_cols, 0, 1), ((0, 0), (0, N_DEV), (0, 0)))
    gsum["w_ada"] = _ada_bwd(jnp.pad(c_all, ((0, N_DEV), (0, 0))), dada_cols, name="ada_bwd")

    out_g, out_d, out_m, out_v = {}, {}, {}, {}
    for k in ("w_up",) + tuple(k for k in _WEIGHTS if k != "w_up"):
        shp = w[k].shape
        if k in _BIG:
            res = tr.run(f"adamw_{k}", lambda cm: _adamw_sharded(
                _shard3(w[k]), _shard3(m[k]), _shard3(v[k]), [tr.final[(l, k)] for l in range(DEPTH)],
                name=f"adamw_{k}", comm=cm))
        else:
            gk = gsum[k].reshape(shp)
            res = [gk] + list(_adamw(_as2d(w[k]), _as2d(gk), _as2d(m[k]), _as2d(v[k]), name=f"adamw_{k}"))
        out_g[k], out_d[k], out_m[k], out_v[k] = [r.reshape(shp) for r in res]

    return (loss, dx[None], *[out_g[k] for k in _WEIGHTS], *[out_d[k] for k in _WEIGHTS],
            *[out_m[k] for k in _WEIGHTS], *[out_v[k] for k in _WEIGHTS])
```

```python
import math
from typing import Callable, NamedTuple

import jax
import jax.numpy as jnp
from jax import lax
from jax.experimental import pallas as pl
from jax.experimental.pallas import tpu as pltpu

F32 = jnp.float32
BF16 = jnp.bfloat16

D_MODEL = 1024
D_Z = 9216
D_FF = 2816
N_GROUPS = 8
GBLK = 128
CHUNK = 64
POOL_WINDOWS = (2, 4, 8, 16)
POOL_GROUP = 256
POOL_HALO = 16
CONV_HALO = 8
DEPTH = 2
ALPHA = (2 * DEPTH) ** 0.25
LN_EPS = 1e-5
ADAM_LR, ADAM_B1, ADAM_B2, ADAM_EPS, ADAM_WD, ADAM_STEP = 0.001, 0.9, 0.999, 1e-08, 0.01, 10
N_CHIPS = 4
N_DEV = 8
FF_CHUNK = 1408
MESH = pl.DeviceIdType.MESH
VMEM_LIMIT = 56 * 1024 * 1024
HBM = pl.BlockSpec(memory_space=pl.ANY)


def _dot(a, b):
    return jnp.dot(a, b, preferred_element_type=F32)


def _dot_nt(a, b):
    return lax.dot_general(a, b, (((1,), (1,)), ((), ())), preferred_element_type=F32)


def _dot_tn(a, b):
    return lax.dot_general(a, b, (((0,), (0,)), ((), ())), preferred_element_type=F32)


_GELU_C = math.sqrt(2.0 / math.pi)


def _gelu_and_grad(x):
    x2 = x * x
    t = jnp.tanh(_GELU_C * (x + 0.044715 * x * x2))
    g = 0.5 * x * (1.0 + t)
    dg = 0.5 * (1.0 + t) + 0.5 * x * (1.0 - t * t) * (_GELU_C * (1.0 + 3 * 0.044715 * x2))
    return g, dg


def _gelu(x):
    return 0.5 * x * (1.0 + jnp.tanh(_GELU_C * (x + 0.044715 * x * x * x)))


def _sigmoid(x):
    return 1.0 / (1.0 + jnp.exp(-x))


def _ln_fwd(r):
    mu = jnp.mean(r, axis=-1, keepdims=True)
    xc = r - mu
    var = jnp.mean(xc * xc, axis=-1, keepdims=True)
    rstd = lax.rsqrt(var + LN_EPS)
    return xc * rstd, rstd


def _ln_bwd(dy, g, xhat, rstd):
    dxh = dy * g
    m1 = jnp.mean(dxh, axis=-1, keepdims=True)
    m2 = jnp.mean(dxh * xhat, axis=-1, keepdims=True)
    return rstd * (dxh - m1 - xhat * m2)


def _rows_before(ext, k, halo):
    return pltpu.roll(ext, k, 0)[halo:]


def _rows_after(ext, k, n):
    return pltpu.roll(ext, ext.shape[0] - k, 0)[:n]


def _colsum(v):
    return jnp.sum(v, axis=0, keepdims=True)


def _spatial_mask():
    i = lax.broadcasted_iota(jnp.int32, (GBLK, GBLK), 0)
    j = lax.broadcasted_iota(jnp.int32, (GBLK, GBLK), 1)
    return (j // CHUNK) <= (i // CHUNK)


def _const(shape):
    n = len(shape)
    return pl.BlockSpec(shape, lambda *_: (0,) * n)


def _resident(shape):
    n = len(shape)
    return pl.BlockSpec(shape, lambda *_: (0,) * n, pipeline_mode=pl.Buffered(1))


def _tile(ts, s):
    return min(ts, s)


class _Comm(NamedTuple):
    srcs: tuple
    dsts: tuple
    n_remote: int
    n_local: int
    build: Callable


def _my_coords():
    return lax.axis_index("x"), lax.axis_index("y"), lax.axis_index("c")


def _chip_peer(k):
    mx, my, mc = _my_coords()
    return (mx ^ ((k >> 1) & 1), my ^ (k & 1), mc)


def _sem_scratch(comm):
    return [pltpu.SemaphoreType.DMA((max(comm.n_remote, 1),)), pltpu.SemaphoreType.DMA((max(comm.n_remote, 1),)),
            pltpu.SemaphoreType.DMA((max(comm.n_local, 1),))]


def _run(body, *, name, grid, in_specs, out_specs, out_shape, args, scratch_shapes=(), comm=None, aliases=None):
    sem = ("arbitrary",) * len(grid)
    cparams = pltpu.CompilerParams(dimension_semantics=sem, vmem_limit_bytes=VMEM_LIMIT)
    kwargs = {} if aliases is None else {"input_output_aliases": aliases}
    if comm is None:
        return pl.pallas_call(body, name=name, grid=grid, in_specs=in_specs, out_specs=out_specs, out_shape=out_shape,
                              scratch_shapes=list(scratch_shapes), compiler_params=cparams, **kwargs)(*args)
    n_in, n_cs, n_out, n_cd, n_scr = len(in_specs), len(comm.srcs), len(out_specs), len(comm.dsts), len(scratch_shapes)
    total = math.prod(grid)
    mid_step = min(total - 1, int(total * 0.7))

    def wrapped(*refs):
        ins, refs = refs[:n_in], refs[n_in:]
        csrc, refs = refs[:n_cs], refs[n_cs:]
        outs, refs = refs[:n_out], refs[n_out:]
        cdst, refs = refs[:n_cd], refs[n_cd:]
        scr, sems = refs[:n_scr], refs[n_scr:]
        step = pl.program_id(0)
        for ax in range(1, len(grid)):
            step = step * grid[ax] + pl.program_id(ax)
        first, mid, last = comm.build(csrc, cdst, *sems, 0, 0)
        pl.when(step == 0)(first)
        if mid is not None:
            pl.when(step == mid_step)(mid)
        body(*ins, *outs, *scr)
        pl.when(step == total - 1)(last)

    res = pl.pallas_call(
        wrapped, name=name, grid=grid, in_specs=list(in_specs) + [HBM] * n_cs, out_specs=list(out_specs) + [HBM] * n_cd,
        out_shape=list(out_shape) + list(comm.dsts), scratch_shapes=list(scratch_shapes) + _sem_scratch(comm),
        compiler_params=cparams, **kwargs)(*args, *comm.srcs)
    return res[:n_out], res[n_out:]


def _comm_call(comm, *, name):
    def body(*refs):
        n_cs, n_cd = len(comm.srcs), len(comm.dsts)
        first, mid, last = comm.build(refs[:n_cs], refs[n_cs:n_cs + n_cd], *refs[n_cs + n_cd:], 0, 0)
        first()
        if mid is not None:
            mid()
        last()

    return pl.pallas_call(body, name=name, in_specs=[HBM] * len(comm.srcs), out_specs=[HBM] * len(comm.dsts),
                          out_shape=list(comm.dsts), scratch_shapes=_sem_scratch(comm))(*comm.srcs)


def _gather_comm(shards, layer):
    dsts = tuple(jax.ShapeDtypeStruct((N_CHIPS,) + s.shape[1:], s.dtype) for s in shards)
    nw = len(shards)

    def build(srcs, outs, send_sems, recv_sems, local_sems, r0, l0):
        mx, my, mc = _my_coords()
        me = 2 * mx + my
        sibling = (mx, my, 1 - mc)

        def rdma(src, dst, idx, peer):
            return pltpu.make_async_remote_copy(src_ref=src, dst_ref=dst, send_sem=send_sems.at[r0 + idx],
                                                recv_sem=recv_sems.at[r0 + idx], device_id=peer, device_id_type=MESH)

        def ici(w, k, slot):
            return rdma(srcs[w].at[layer, mc], outs[w].at[slot, mc], 6 * w + k - 1, _chip_peer(k))

        def fwd(w, k, half):
            return rdma(outs[w].at[me ^ k, mc], outs[w].at[me ^ k, half], 6 * w + 2 + k, sibling)

        def own(w):
            return pltpu.make_async_copy(srcs[w].at[layer], outs[w].at[me], local_sems.at[l0 + w])

        def first():
            for w in range(nw):
                own(w).start()
                for k in range(1, N_CHIPS):
                    ici(w, k, me).start()

        def mid():
            for w in range(nw):
                for k in range(1, N_CHIPS):
                    ici(w, k, me ^ k).wait_recv()
                    fwd(w, k, mc).start()

        def last():
            for w in range(nw):
                for k in range(1, N_CHIPS):
                    fwd(w, k, 1 - mc).wait_recv()
                    ici(w, k, me).wait_send()
                    fwd(w, k, mc).wait_send()
                own(w).wait()

        return first, mid, last

    return _Comm(tuple(shards), dsts, 6 * nw, nw, build)


def _symmetric(make_remote, make_local, make_incoming=None):
    def first():
        for cp in make_remote() + make_local():
            cp.start()

    def last():
        for cp in (make_incoming or make_remote)():
            cp.wait_recv()
        for cp in make_remote():
            cp.wait_send()
        for cp in make_local():
            cp.wait()

    return first, None, last


def _presum_comm(g_bf16):
    nw = len(g_bf16)
    dsts = tuple(jax.ShapeDtypeStruct((N_CHIPS,) + g.shape[2:], BF16) for g in g_bf16)

    def build(srcs, outs, send_sems, recv_sems, local_sems, r0, l0):
        mx, my, mc = _my_coords()

        def remote():
            return [pltpu.make_async_remote_copy(
                src_ref=srcs[w].at[j, 1 - mc], dst_ref=outs[w].at[j], send_sem=send_sems.at[r0 + N_CHIPS * w + j],
                recv_sem=recv_sems.at[r0 + N_CHIPS * w + j], device_id=(mx, my, 1 - mc), device_id_type=MESH)
                for w in range(nw) for j in range(N_CHIPS)]

        return _symmetric(remote, lambda: [])

    return _Comm(tuple(g_bf16), dsts, N_CHIPS * nw, 0, build)


def _scatter_comm(h_bf16):
    nw = len(h_bf16)
    dsts = tuple(jax.ShapeDtypeStruct((N_CHIPS - 1,) + h.shape[1:], BF16) for h in h_bf16)

    def build(srcs, outs, send_sems, recv_sems, local_sems, r0, l0):
        mx, my, _ = _my_coords()
        me = 2 * mx + my

        def remote():
            return [pltpu.make_async_remote_copy(
                src_ref=srcs[w].at[me ^ k], dst_ref=outs[w].at[k - 1], send_sem=send_sems.at[r0 + 3 * w + k - 1],
                recv_sem=recv_sems.at[r0 + 3 * w + k - 1], device_id=_chip_peer(k), device_id_type=MESH)
                for w in range(nw) for k in range(1, N_CHIPS)]

        return _symmetric(remote, lambda: [])

    return _Comm(tuple(h_bf16), dsts, 3 * nw, 0, build)


def _join_comm(halves):
    nw = len(halves)
    dsts = tuple(jax.ShapeDtypeStruct((2,) + h.shape, h.dtype) for h in halves)

    def build(srcs, outs, send_sems, recv_sems, local_sems, r0, l0):
        mx, my, mc = _my_coords()

        def remote(half=mc):
            return [pltpu.make_async_remote_copy(
                src_ref=srcs[w], dst_ref=outs[w].at[half], send_sem=send_sems.at[r0 + w], recv_sem=recv_sems.at[r0 + w],
                device_id=(mx, my, 1 - mc), device_id_type=MESH) for w in range(nw)]

        def local():
            return [pltpu.make_async_copy(srcs[w], outs[w].at[mc], local_sems.at[l0 + w]) for w in range(nw)]

        return _symmetric(remote, local, lambda: remote(1 - mc))

    return _Comm(tuple(halves), dsts, nw, nw, build)


def _merge(comms):
    comms = list(comms)
    if len(comms) == 1:
        return comms[0]

    def build(srcs, outs, send_sems, recv_sems, local_sems, r0, l0):
        phases, s0, d0 = [], 0, 0
        for cm in comms:
            phases.append(cm.build(srcs[s0:s0 + len(cm.srcs)], outs[d0:d0 + len(cm.dsts)], send_sems, recv_sems,
                                   local_sems, r0, l0))
            s0, d0, r0, l0 = s0 + len(cm.srcs), d0 + len(cm.dsts), r0 + cm.n_remote, l0 + cm.n_local

        def run(idx):
            fns = [ph[idx] for ph in phases if ph[idx] is not None]
            if not fns:
                return None

            def go():
                for fn in fns:
                    fn()
            return go

        return run(0), run(1), run(2)

    return _Comm(sum((cm.srcs for cm in comms), ()), sum((cm.dsts for cm in comms), ()),
                 sum(cm.n_remote for cm in comms), sum(cm.n_local for cm in comms), build)


def _split(comms, res):
    out, d0 = [], 0
    for cm in comms:
        out.append(list(res[d0:d0 + len(cm.dsts)]))
        d0 += len(cm.dsts)
    return out


def _all_reduce_small(x, *, name):
    r, lanes = x.shape

    def body(x_ref, out_ref, sib_ref, slots_ref, send_sems, recv_sems):
        mx, my, mc = _my_coords()
        me = 2 * mx + my
        swap = pltpu.make_async_remote_copy(src_ref=x_ref, dst_ref=sib_ref, send_sem=send_sems.at[0],
                                            recv_sem=recv_sems.at[0], device_id=(mx, my, 1 - mc), device_id_type=MESH)
        swap.start()
        swap.wait_recv()
        swap.wait_send()
        slots_ref[me] = x_ref[...] + sib_ref[...]

        def copy(k, slot):
            return pltpu.make_async_remote_copy(
                src_ref=slots_ref.at[me], dst_ref=slots_ref.at[slot], send_sem=send_sems.at[k], recv_sem=recv_sems.at[k],
                device_id=_chip_peer(k), device_id_type=MESH)

        sends = [copy(k, me) for k in range(1, N_CHIPS)]
        for cp in sends:
            cp.start()
        for k in range(1, N_CHIPS):
            copy(k, me ^ k).wait_recv()
        for cp in sends:
            cp.wait_send()
        acc = slots_ref[0]
        for j in range(1, N_CHIPS):
            acc = acc + slots_ref[j]
        out_ref[...] = acc

    vmem = pl.BlockSpec(memory_space=pltpu.VMEM)
    return pl.pallas_call(
        body, name=name, in_specs=[vmem], out_specs=vmem, out_shape=jax.ShapeDtypeStruct((r, lanes), F32),
        scratch_shapes=[pltpu.VMEM((r, lanes), F32), pltpu.VMEM((N_CHIPS, r, lanes), F32),
                        pltpu.SemaphoreType.DMA((N_CHIPS,)), pltpu.SemaphoreType.DMA((N_CHIPS,))],
        compiler_params=pltpu.CompilerParams(vmem_limit_bytes=VMEM_LIMIT),
    )(x)


def _all_gather8(x, *, name, with_sum=False):
    r, lanes = x.shape

    def body(x_ref, out_ref, *rest):
        if with_sum:
            sum_ref, send_sems, recv_sems, local_sem = rest
        else:
            send_sems, recv_sems, local_sem = rest
        mx, my, mc = _my_coords()
        me = 4 * mx + 2 * my + mc

        def peer(k):
            return (mx ^ ((k >> 2) & 1), my ^ ((k >> 1) & 1), mc ^ (k & 1))

        def copy(k, slot):
            return pltpu.make_async_remote_copy(
                src_ref=x_ref, dst_ref=out_ref.at[slot], send_sem=send_sems.at[k - 1], recv_sem=recv_sems.at[k - 1],
                device_id=peer(k), device_id_type=MESH)

        mine = pltpu.make_async_copy(x_ref, out_ref.at[me], local_sem)
        mine.start()
        sends = [copy(k, me) for k in range(1, N_DEV)]
        for cp in sends:
            cp.start()
        for k in range(1, N_DEV):
            copy(k, me ^ k).wait_recv()
        for cp in sends:
            cp.wait_send()
        mine.wait()
        if with_sum:
            acc = out_ref[0]
            for k in range(1, N_DEV):
                acc = acc + out_ref[k]
            sum_ref[...] = acc

    vmem = pl.BlockSpec(memory_space=pltpu.VMEM)
    out_shape = [jax.ShapeDtypeStruct((N_DEV, r, lanes), F32)]
    if with_sum:
        out_shape.append(jax.ShapeDtypeStruct((r, lanes), F32))
    res = pl.pallas_call(
        body, name=name, in_specs=[vmem], out_specs=[vmem] * len(out_shape), out_shape=out_shape,
        scratch_shapes=[pltpu.SemaphoreType.DMA((N_DEV - 1,)), pltpu.SemaphoreType.DMA((N_DEV - 1,)),
                        pltpu.SemaphoreType.DMA],
        compiler_params=pltpu.CompilerParams(vmem_limit_bytes=VMEM_LIMIT),
    )(x)
    return res if with_sum else res[0]


def _mod_matmul(x, sc, sh, w4, b, *, ts, tn, name, comm=None):
    s, d = x.shape
    wd = w4.shape[2]
    n = N_CHIPS * wd
    per = wd // tn
    ts = _tile(ts, s)

    def body(x_ref, sc_ref, sh_ref, w_ref, b_ref, o_ref, ht_ref, h_scr):
        @pl.when(pl.program_id(1) == 0)
        def _():
            h = x_ref[...] * (1.0 + sc_ref[...]) + sh_ref[...]
            h_scr[...] = h.astype(BF16)
            ht_ref[...] = h.T.astype(BF16)
        o_ref[...] = _dot(h_scr[...], w_ref[0]) + b_ref[...]

    return _run(
        body, name=name, grid=(s // ts, n // tn),
        in_specs=[pl.BlockSpec((ts, d), lambda i, j: (i, 0)), _const((1, d)), _const((1, d)),
                  pl.BlockSpec((1, d, tn), lambda i, j: (j // per, 0, j % per)),
                  pl.BlockSpec((1, tn), lambda i, j: (0, j))],
        out_specs=[pl.BlockSpec((ts, tn), lambda i, j: (i, j)), pl.BlockSpec((d, ts), lambda i, j: (0, i))],
        out_shape=[jax.ShapeDtypeStruct((s, n), F32), jax.ShapeDtypeStruct((d, s), BF16)],
        scratch_shapes=[pltpu.VMEM((ts, d), BF16)],
        args=(x, sc, sh, w4, b), comm=comm)


def _conv3(q, ext, cw):
    return cw[2:3] * q + cw[1:2] * _rows_before(ext, 1, CONV_HALO) + cw[0:1] * _rows_before(ext, 2, CONV_HALO)


def _mix_a_fwd(z, cw, w_out, *, ts, name):
    s = z.shape[0]
    d = D_MODEL
    ts = _tile(ts, s)

    def body(zb_ref, zc_ref, zx_ref, cw_ref, w_ref, a_ref, y_ref, carry):
        @pl.when(pl.program_id(0) == 0)
        def _():
            carry[...] = jnp.zeros_like(carry)
        q = zc_ref[...] * zx_ref[...]
        ext = jnp.concatenate([carry[...], q], axis=0)
        a = (zb_ref[...] * _conv3(q, ext, cw_ref[...])).astype(BF16)
        carry[...] = q[ts - CONV_HALO:]
        a_ref[...] = a
        y_ref[...] = _dot(a, w_ref[...])

    zspec = lambda k: pl.BlockSpec((ts, d), lambda i, k=k: (i, k))
    return _run(
        body, name=name, grid=(s // ts,),
        in_specs=[zspec(0), zspec(1), zspec(2), _const((3, d)), _const((d, d))],
        out_specs=[pl.BlockSpec((ts, d), lambda i: (i, 0))] * 2,
        out_shape=[jax.ShapeDtypeStruct((s, d), BF16), jax.ShapeDtypeStruct((s, d), F32)],
        scratch_shapes=[pltpu.VMEM((CONV_HALO, d), F32)],
        args=(z, z, z, cw, w_out))


def _spatial_mix(vn_b, ws_ref, bst_ref, mixed_scr, ts):
    nblk = ts // GBLK
    mask = _spatial_mask()
    for g in range(N_GROUPS):
        cols = slice(g * GBLK, (g + 1) * GBLK)
        wm = jnp.where(mask, ws_ref[g], 0.0).astype(BF16)
        cat = jnp.concatenate([vn_b[n * GBLK:(n + 1) * GBLK, cols] for n in range(nblk)], axis=1)
        res = _dot(wm, cat) + bst_ref[:, g:g + 1]
        for n in range(nblk):
            mixed_scr[n * GBLK:(n + 1) * GBLK, cols] = res[:, n * GBLK:(n + 1) * GBLK]


def _mix_b_fwd(z, ln_g, ln_b, ws, bst, w_out, *, ts, name, comm=None):
    s = z.shape[0]
    d = D_MODEL
    ts = _tile(ts, s)

    def body(zu_ref, zv_ref, g_ref, b_ref, ws_ref, bst_ref, w_ref, sg_ref, y_ref, mixed_scr):
        xhat, _ = _ln_fwd(_gelu(zv_ref[...]))
        vn = (xhat * g_ref[...] + b_ref[...]).astype(BF16)
        _spatial_mix(vn, ws_ref, bst_ref, mixed_scr, ts)
        sg = (_gelu(zu_ref[...]) * mixed_scr[...]).astype(BF16)
        sg_ref[...] = sg
        y_ref[...] = _dot(sg, w_ref[...])

    zspec = lambda k: pl.BlockSpec((ts, d), lambda i, k=k: (i, k))
    return _run(
        body, name=name, grid=(s // ts,),
        in_specs=[zspec(3), zspec(4), _const((1, d)), _const((1, d)), _const((N_GROUPS, GBLK, GBLK)),
                  _const((GBLK, N_GROUPS)), _const((d, d))],
        out_specs=[pl.BlockSpec((ts, d), lambda i: (i, 0))] * 2,
        out_shape=[jax.ShapeDtypeStruct((s, d), BF16), jax.ShapeDtypeStruct((s, d), F32)],
        scratch_shapes=[pltpu.VMEM((ts, d), F32)],
        args=(z, z, ln_g, ln_b, ws, bst, w_out), comm=comm)


def _pool_denoms(tile_idx, ts):
    t1 = (tile_idx * ts + 1 + lax.broadcasted_iota(jnp.int32, (ts, 1), 0)).astype(F32)
    return [jnp.minimum(t1, float(w)) for w in POOL_WINDOWS]


def _pool_diff(p, ext, denoms, k):
    cols = slice(k * POOL_GROUP, (k + 1) * POOL_GROUP)
    acc = ext[:, cols]
    step = 1
    while step < POOL_WINDOWS[k]:
        acc = acc + pltpu.roll(acc, step, 0)
        step *= 2
    return acc[POOL_HALO:] / denoms[k] - p[:, cols]


def _mix_c_fwd(z, w_pool, scale, *, ts, name):
    s = z.shape[0]
    d = D_MODEL
    ts = _tile(ts, s)

    def body(zp_ref, w_ref, sc_ref, d_ref, y_ref, carry):
        i = pl.program_id(0)

        @pl.when(i == 0)
        def _():
            carry[...] = jnp.zeros_like(carry)
        p = zp_ref[...]
        ext = jnp.concatenate([carry[...], p], axis=0)
        carry[...] = p[ts - POOL_HALO:]
        denoms = _pool_denoms(i, ts)
        for k in range(len(POOL_WINDOWS)):
            cols = slice(k * POOL_GROUP, (k + 1) * POOL_GROUP)
            dk = _pool_diff(p, ext, denoms, k).astype(BF16)
            d_ref[:, cols] = dk
            y_ref[:, cols] = _dot(dk, w_ref[k]) * sc_ref[:, cols]

    return _run(
        body, name=name, grid=(s // ts,),
        in_specs=[pl.BlockSpec((ts, d), lambda i: (i, 5)), _const((4, POOL_GROUP, POOL_GROUP)), _const((1, d))],
        out_specs=[pl.BlockSpec((ts, d), lambda i: (i, 0))] * 2,
        out_shape=[jax.ShapeDtypeStruct((s, d), BF16), jax.ShapeDtypeStruct((s, d), F32)],
        scratch_shapes=[pltpu.VMEM((POOL_HALO, d), F32)],
        args=(z, w_pool, scale))


def _mix_o_fwd(x, z, ya, yb, yc, w_o, gt, ln_g, ln_b, *, ts, name, comm=None):
    s, d = x.shape
    ts = _tile(ts, s)

    def body(x_ref, ga_ref, gb_ref, gc_ref, ya_ref, yb_ref, yc_ref, w_ref, gt_ref, g_ref, b_ref,
             m_ref, o_ref, x1_ref):
        merged = (_sigmoid(ga_ref[...]) * ya_ref[...] + _sigmoid(gb_ref[...]) * yb_ref[...]
                  + _sigmoid(gc_ref[...]) * yc_ref[...]).astype(BF16)
        m_ref[...] = merged
        o = _dot(merged, w_ref[...])
        o_ref[...] = o
        xhat, _ = _ln_fwd(ALPHA * x_ref[...] + gt_ref[...] * o)
        x1_ref[...] = xhat * g_ref[...] + b_ref[...]

    row = pl.BlockSpec((ts, d), lambda i: (i, 0))
    zspec = lambda k: pl.BlockSpec((ts, d), lambda i, k=k: (i, k))
    return _run(
        body, name=name, grid=(s // ts,),
        in_specs=[row, zspec(6), zspec(7), zspec(8), row, row, row, _const((d, d)),
                  _const((1, d)), _const((1, d)), _const((1, d))],
        out_specs=[row] * 3,
        out_shape=[jax.ShapeDtypeStruct((s, d), BF16), jax.ShapeDtypeStruct((s, d), F32),
                   jax.ShapeDtypeStruct((s, d), F32)],
        args=(x, z, z, z, ya, yb, yc, w_o, gt, ln_g, ln_b), comm=comm)


def _ffn_fwd(up, x1, cw, cb, w_down, gt, ln_g, ln_b, *, ts, name, comm=None):
    s, d = x1.shape
    ts = _tile(ts, s)

    def body(up_ref, x1_ref, cw_ref, cb_ref, w_ref, gt_ref, g_ref, b_ref, ft_ref, dn_ref, x2_ref, carry, f_ref):
        @pl.when(pl.program_id(0) == 0)
        def _():
            carry[...] = jnp.zeros_like(carry)
        for c in range(D_FF // FF_CHUNK):
            ca = slice(c * FF_CHUNK, (c + 1) * FF_CHUNK)
            cg = slice(D_FF + c * FF_CHUNK, D_FF + (c + 1) * FF_CHUNK)
            ua = up_ref[:, ca]
            ext = jnp.concatenate([carry[:, ca], ua], axis=0)
            carry[:, ca] = ua[ts - CONV_HALO:]
            cf = _conv3(ua, ext, cw_ref[:, ca]) + cb_ref[:, ca]
            f = _gelu(cf) * up_ref[:, cg]
            f_ref[:, ca] = f.astype(BF16)
            ft_ref[ca, :] = f.T.astype(BF16)
        dn = _dot(f_ref[...], w_ref[...])
        dn_ref[...] = dn
        xhat, _ = _ln_fwd(ALPHA * x1_ref[...] + gt_ref[...] * dn)
        x2_ref[...] = xhat * g_ref[...] + b_ref[...]

    row = pl.BlockSpec((ts, d), lambda i: (i, 0))
    return _run(
        body, name=name, grid=(s // ts,),
        in_specs=[pl.BlockSpec((ts, 2 * D_FF), lambda i: (i, 0)), row, _const((3, D_FF)), _const((1, D_FF)),
                  _resident((D_FF, d)), _const((1, d)), _const((1, d)), _const((1, d))],
        out_specs=[pl.BlockSpec((D_FF, ts), lambda i: (0, i)), row, row],
        out_shape=[jax.ShapeDtypeStruct((D_FF, s), BF16), jax.ShapeDtypeStruct((s, d), F32),
                   jax.ShapeDtypeStruct((s, d), F32)],
        scratch_shapes=[pltpu.VMEM((CONV_HALO, D_FF), F32), pltpu.VMEM((ts, D_FF), BF16)],
        args=(up, x1, cw, cb, w_down, gt, ln_g, ln_b), comm=comm)


def _loss_fwd(y, tgt, *, ts, name):
    s, d = y.shape
    ts = _tile(ts, s)

    def body(y_ref, t_ref, dy_ref, l_ref):
        @pl.when(pl.program_id(0) == 0)
        def _():
            l_ref[...] = jnp.zeros_like(l_ref)
        e = y_ref[...] - t_ref[...]
        dy_ref[...] = e / float(d)
        l_ref[...] += 0.5 * jnp.sum(jnp.mean(e * e, axis=-1, keepdims=True), axis=0, keepdims=True)

    row = pl.BlockSpec((ts, d), lambda i: (i, 0))
    return _run(body, name=name, grid=(s // ts,), in_specs=[row, row], out_specs=[row, _const((8, 128))],
                out_shape=[jax.ShapeDtypeStruct((s, d), F32), jax.ShapeDtypeStruct((8, 128), F32)], args=(y, tgt))


def _rev(n_tiles):
    return lambda i: n_tiles - 1 - i


def _halo_spec(ts, n_tiles, halo, width, col):
    per = ts // halo
    return pl.BlockSpec((halo, width), lambda i: (jnp.maximum((n_tiles - 1 - i) * per - 1, 0), col))


def _ffn_bwd(dx2, x1, dn, up, cw, cb, w_down, w_up4, gt, ln_g, sc, *, ts, name, comm=None):
    s, d = x1.shape
    ts = _tile(ts, s)
    nt = s // ts
    rev = _rev(nt)
    wd = w_up4.shape[2]

    def w_up_cols(wu_ref, start):
        return wu_ref[start // wd, :, start % wd:start % wd + FF_CHUNK]

    def body(dx2_ref, x1_ref, dn_ref, up_ref, halo_ref, cw_ref, cb_ref, wd_ref, wu_ref, gt_ref, g_ref, sc_ref,
             ddn_ref, dup_ref, dx1_ref, redd_ref, redf_ref, dbup_ref, carry):
        i = pl.program_id(0)

        @pl.when(i == 0)
        def _():
            carry[...] = jnp.zeros_like(carry)
            redd_ref[...] = jnp.zeros_like(redd_ref)
            redf_ref[...] = jnp.zeros_like(redf_ref)
            dbup_ref[...] = jnp.zeros_like(dbup_ref)
        first_tile = i == nt - 1
        x1v, dnv, dyv = x1_ref[...], dn_ref[...], dx2_ref[...]
        xhat, rstd = _ln_fwd(ALPHA * x1v + gt_ref[...] * dnv)
        dr = _ln_bwd(dyv, g_ref[...], xhat, rstd)
        redd_ref[0:1, :] += _colsum(dyv * xhat)
        redd_ref[1:2, :] += _colsum(dyv)
        redd_ref[2:3, :] += _colsum(dr * dnv)
        ddn = (gt_ref[...] * dr).astype(BF16)
        ddn_ref[...] = ddn
        dh = jnp.zeros((ts, d), F32)
        for c in range(D_FF // FF_CHUNK):
            ca = slice(c * FF_CHUNK, (c + 1) * FF_CHUNK)
            cg = slice(D_FF + c * FF_CHUNK, D_FF + (c + 1) * FF_CHUNK)
            df = _dot_nt(ddn, wd_ref[ca, :])
            ua, ug = up_ref[:, ca], up_ref[:, cg]
            halo = jnp.where(first_tile, 0.0, halo_ref[:, ca])
            ext = jnp.concatenate([halo, ua], axis=0)
            u1, u2 = _rows_before(ext, 1, CONV_HALO), _rows_before(ext, 2, CONV_HALO)
            cwc = cw_ref[:, ca]
            gl, dgl = _gelu_and_grad(cwc[2:3] * ua + cwc[1:2] * u1 + cwc[0:1] * u2 + cb_ref[:, ca])
            dug = df * gl
            dcf = df * ug * dgl
            redf_ref[0:1, ca] += _colsum(dcf * u2)
            redf_ref[1:2, ca] += _colsum(dcf * u1)
            redf_ref[2:3, ca] += _colsum(dcf * ua)
            redf_ref[3:4, ca] += _colsum(dcf)
            extd = jnp.concatenate([dcf, carry[:, ca]], axis=0)
            carry[:, ca] = dcf[:CONV_HALO]
            dua = cwc[2:3] * dcf + cwc[1:2] * _rows_after(extd, 1, ts) + cwc[0:1] * _rows_after(extd, 2, ts)
            dbup_ref[0:1, ca] += _colsum(dua)
            dbup_ref[0:1, cg] += _colsum(dug)
            dua_b, dug_b = dua.astype(BF16), dug.astype(BF16)
            dup_ref[:, ca] = dua_b
            dup_ref[:, cg] = dug_b
            dh = dh + _dot_nt(dua_b, w_up_cols(wu_ref, c * FF_CHUNK)) + _dot_nt(dug_b, w_up_cols(wu_ref, D_FF + c * FF_CHUNK))
        dx1_ref[...] = ALPHA * dr + dh * (1.0 + sc_ref[...])
        redd_ref[3:4, :] += _colsum(dh * x1v)
        redd_ref[4:5, :] += _colsum(dh)

    row = pl.BlockSpec((ts, d), lambda i: (rev(i), 0))
    return _run(
        body, name=name, grid=(nt,),
        in_specs=[row, row, row, pl.BlockSpec((ts, 2 * D_FF), lambda i: (rev(i), 0)),
                  _halo_spec(ts, nt, CONV_HALO, D_FF, 0), _const((3, D_FF)), _const((1, D_FF)),
                  _resident((D_FF, d)), _resident((N_CHIPS, d, wd)), _const((1, d)), _const((1, d)), _const((1, d))],
        out_specs=[row, pl.BlockSpec((ts, 2 * D_FF), lambda i: (rev(i), 0)), row,
                   _const((8, d)), _const((8, D_FF)), _const((8, 2 * D_FF))],
        out_shape=[jax.ShapeDtypeStruct((s, d), BF16), jax.ShapeDtypeStruct((s, 2 * D_FF), BF16),
                   jax.ShapeDtypeStruct((s, d), F32), jax.ShapeDtypeStruct((8, d), F32),
                   jax.ShapeDtypeStruct((8, D_FF), F32), jax.ShapeDtypeStruct((8, 2 * D_FF), F32)],
        scratch_shapes=[pltpu.VMEM((CONV_HALO, D_FF), F32)],
        args=(dx2, x1, dn, up, up, cw, cb, w_down, w_up4, gt, ln_g, sc), comm=comm)


def _grad_matmul(xa, dy, *, ts, tn, name, mod=None, by_chip=False, comm=None):
    s, k = xa.shape
    n = dy.shape[1]
    ts = _tile(ts, s)
    nt = s // ts

    def body(*refs):
        if mod is None:
            xa_ref, dy_ref, o_ref, ob_ref = refs
            a = xa_ref[...]
        else:
            xa_ref, sc_ref, sh_ref, dy_ref, o_ref, ob_ref = refs
            a = (xa_ref[...] * (1.0 + sc_ref[...]) + sh_ref[...]).astype(BF16)
        t = pl.program_id(1)

        @pl.when(t == 0)
        def _():
            o_ref[...] = jnp.zeros_like(o_ref)
        o_ref[...] += _dot_tn(a, dy_ref[...]).reshape(o_ref.shape)

        @pl.when(t == nt - 1)
        def _():
            ob_ref[...] = o_ref[...].astype(BF16)

    xspec = pl.BlockSpec((ts, k), lambda j, t: (t, 0))
    dspec = pl.BlockSpec((ts, tn), lambda j, t: (t, j))
    in_specs = [xspec, dspec] if mod is None else [xspec, _const((1, k)), _const((1, k)), dspec]
    args = (xa, dy) if mod is None else (xa, mod[0], mod[1], dy)
    if by_chip:
        per = n // N_CHIPS // tn
        ospec = pl.BlockSpec((1, k, tn), lambda j, t: (j // per, 0, j % per))
        shape = (N_CHIPS, k, n // N_CHIPS)
    else:
        ospec = pl.BlockSpec((k, tn), lambda j, t: (0, j))
        shape = (k, n)
    return _run(body, name=name, grid=(n // tn, nt), in_specs=in_specs, out_specs=[ospec, ospec],
                out_shape=[jax.ShapeDtypeStruct(shape, F32), jax.ShapeDtypeStruct(shape, BF16)], args=args, comm=comm)


def _grad_matmul_t(xt, dy, *, tk, tn, name, by_chip=False, comm=None):
    k, s = xt.shape
    n = dy.shape[1]

    def body(xt_ref, dy_ref, o_ref, ob_ref):
        o = _dot(xt_ref[...], dy_ref[...]).reshape(o_ref.shape)
        o_ref[...] = o
        ob_ref[...] = o.astype(BF16)

    if by_chip:
        assert tk == k
        per = n // N_CHIPS // tn
        ospec = pl.BlockSpec((1, k, tn), lambda j, i: (j // per, 0, j % per))
        shape = (N_CHIPS, k, n // N_CHIPS)
    else:
        ospec = pl.BlockSpec((tk, tn), lambda j, i: (i, j))
        shape = (k, n)
    xspec = _resident((k, s)) if tk == k else pl.BlockSpec((tk, s), lambda j, i: (i, 0))
    dspec = _resident((s, n)) if tn == n else pl.BlockSpec((s, tn), lambda j, i: (0, j))
    return _run(body, name=name, grid=(n // tn, k // tk), in_specs=[xspec, dspec], out_specs=[ospec, ospec],
                out_shape=[jax.ShapeDtypeStruct(shape, F32), jax.ShapeDtypeStruct(shape, BF16)], args=(xt, dy), comm=comm)


def _mix_o_bwd(dx1, x, o, z, ya, yb, yc, w_o, gt, ln_g, *, ts, name, comm=None):
    s, d = x.shape
    ts = _tile(ts, s)

    def body(dx1_ref, x_ref, o_ref, ga_ref, gb_ref, gc_ref, ya_ref, yb_ref, yc_ref, w_ref, gt_ref, g_ref,
             do_ref, dxa_ref, dzg_ref, dya_ref, dyb_ref, dyc_ref, red_ref):
        @pl.when(pl.program_id(0) == 0)
        def _():
            red_ref[...] = jnp.zeros_like(red_ref)
        dyv, ov = dx1_ref[...], o_ref[...]
        xhat, rstd = _ln_fwd(ALPHA * x_ref[...] + gt_ref[...] * ov)
        dr = _ln_bwd(dyv, g_ref[...], xhat, rstd)
        red_ref[0:1, :] += _colsum(dyv * xhat)
        red_ref[1:2, :] += _colsum(dyv)
        red_ref[2:3, :] += _colsum(dr * ov)
        dxa_ref[...] = ALPHA * dr
        d_o = (gt_ref[...] * dr).astype(BF16)
        do_ref[...] = d_o
        dm = _dot_nt(d_o, w_ref[...])
        for k, (zg_ref, y_ref, dy_ref) in enumerate(((ga_ref, ya_ref, dya_ref), (gb_ref, yb_ref, dyb_ref),
                                                     (gc_ref, yc_ref, dyc_ref))):
            g = _sigmoid(zg_ref[...])
            dzg_ref[:, k * d:(k + 1) * d] = (dm * y_ref[...] * g * (1.0 - g)).astype(BF16)
            dy_ref[...] = (dm * g).astype(BF16)

    row = pl.BlockSpec((ts, d), lambda i: (i, 0))
    zspec = lambda k: pl.BlockSpec((ts, d), lambda i, k=k: (i, k))
    bf = jax.ShapeDtypeStruct((s, d), BF16)
    return _run(
        body, name=name, grid=(s // ts,),
        in_specs=[row, row, row, zspec(6), zspec(7), zspec(8), row, row, row, _const((d, d)),
                  _const((1, d)), _const((1, d))],
        out_specs=[row, row, pl.BlockSpec((ts, 3 * d), lambda i: (i, 2)), row, row, row, _const((8, d))],
        out_shape=[bf, jax.ShapeDtypeStruct((s, d), F32), jax.ShapeDtypeStruct((s, D_Z), BF16), bf, bf, bf,
                   jax.ShapeDtypeStruct((8, d), F32)],
        args=(dx1, x, o, z, z, z, ya, yb, yc, w_o, gt, ln_g), comm=comm)


def _mix_a_bwd(dya, z, dz, cw, w_out, *, ts, name, comm=None):
    s = z.shape[0]
    d = D_MODEL
    ts = _tile(ts, s)
    nt = s // ts
    rev = _rev(nt)

    def body(dya_ref, zb_ref, zc_ref, zx_ref, hc_ref, hx_ref, cw_ref, w_ref, dz_in, dz_ref, red_ref, carry):
        i = pl.program_id(0)

        @pl.when(i == 0)
        def _():
            carry[...] = jnp.zeros_like(carry)
            red_ref[...] = jnp.zeros_like(red_ref)
        zb, zc, zx = zb_ref[...], zc_ref[...], zx_ref[...]
        q = zc * zx
        halo = jnp.where(i == nt - 1, 0.0, hc_ref[...] * hx_ref[...])
        ext = jnp.concatenate([halo, q], axis=0)
        q1, q2 = _rows_before(ext, 1, CONV_HALO), _rows_before(ext, 2, CONV_HALO)
        cwv = cw_ref[...]
        cv = cwv[2:3] * q + cwv[1:2] * q1 + cwv[0:1] * q2
        da = _dot_nt(dya_ref[...], w_ref[...])
        dcv = da * zb
        red_ref[0:1, :] += _colsum(dcv * q2)
        red_ref[1:2, :] += _colsum(dcv * q1)
        red_ref[2:3, :] += _colsum(dcv * q)
        extd = jnp.concatenate([dcv, carry[...]], axis=0)
        carry[...] = dcv[:CONV_HALO]
        dq = cwv[2:3] * dcv + cwv[1:2] * _rows_after(extd, 1, ts) + cwv[0:1] * _rows_after(extd, 2, ts)
        dz_ref[:, 0:d] = (da * cv).astype(BF16)
        dz_ref[:, d:2 * d] = (dq * zx).astype(BF16)
        dz_ref[:, 2 * d:3 * d] = (dq * zc).astype(BF16)

    zspec = lambda k: pl.BlockSpec((ts, d), lambda i, k=k: (rev(i), k))
    return _run(
        body, name=name, grid=(nt,),
        in_specs=[pl.BlockSpec((ts, d), lambda i: (rev(i), 0)), zspec(0), zspec(1), zspec(2),
                  _halo_spec(ts, nt, CONV_HALO, d, 1), _halo_spec(ts, nt, CONV_HALO, d, 2),
                  _const((3, d)), _const((d, d)), HBM],
        out_specs=[pl.BlockSpec((ts, 3 * d), lambda i: (rev(i), 0)), _const((8, d))],
        out_shape=[jax.ShapeDtypeStruct((s, D_Z), BF16), jax.ShapeDtypeStruct((8, d), F32)],
        scratch_shapes=[pltpu.VMEM((CONV_HALO, d), F32)],
        args=(dya, z, z, z, z, z, cw, w_out, dz), aliases={8: 0}, comm=comm)


def _mix_b_bwd(dyb, z, dz, ln_g, ln_b, ws, bst, w_out, *, ts, name, comm=None):
    s = z.shape[0]
    d = D_MODEL
    ts = _tile(ts, s)
    nblk = ts // GBLK

    def body(dyb_ref, zu_ref, zv_ref, g_ref, b_ref, ws_ref, bst_ref, w_ref, dz_in,
             dz_ref, red_ref, dws_ref, dbst_ref, mixed_scr, dvn_scr, dzv_scr):
        @pl.when((pl.program_id(0) == 0) & (pl.program_id(1) == 0))
        def _():
            red_ref[...] = jnp.zeros_like(red_ref)
            dws_ref[...] = jnp.zeros_like(dws_ref)
            dbst_ref[...] = jnp.zeros_like(dbst_ref)

        @pl.when(pl.program_id(1) == 0)
        def _():
            u, du_dz = _gelu_and_grad(zu_ref[...])
            vg, dv_dz = _gelu_and_grad(zv_ref[...])
            xhat, rstd = _ln_fwd(vg)
            vn = (xhat * g_ref[...] + b_ref[...]).astype(BF16)
            _spatial_mix(vn, ws_ref, bst_ref, mixed_scr, ts)
            dsg = _dot_nt(dyb_ref[...], w_ref[...])
            dz_ref[...] = (dsg * mixed_scr[...] * du_dz).astype(BF16)
            dmix = dsg * u
            mask = _spatial_mask()
            for g in range(N_GROUPS):
                cols = slice(g * GBLK, (g + 1) * GBLK)
                wm = jnp.where(mask, ws_ref[g], 0.0).astype(BF16)
                dm_cat = jnp.concatenate([dmix[n * GBLK:(n + 1) * GBLK, cols] for n in range(nblk)], axis=1)
                vn_cat = jnp.concatenate([vn[n * GBLK:(n + 1) * GBLK, cols] for n in range(nblk)], axis=1)
                dm_b = dm_cat.astype(BF16)
                dbst_ref[:, g:g + 1] += jnp.sum(dm_cat, axis=1, keepdims=True)
                dws_ref[g] += jnp.where(mask, _dot_nt(dm_b, vn_cat), 0.0)
                dvn_cat = _dot_tn(wm, dm_b)
                for n in range(nblk):
                    dvn_scr[n * GBLK:(n + 1) * GBLK, cols] = dvn_cat[:, n * GBLK:(n + 1) * GBLK]
            dvn = dvn_scr[...]
            red_ref[0:1, :] += _colsum(dvn * xhat)
            red_ref[1:2, :] += _colsum(dvn)
            dzv_scr[...] = (_ln_bwd(dvn, g_ref[...], xhat, rstd) * dv_dz).astype(BF16)

        @pl.when(pl.program_id(1) == 1)
        def _():
            dz_ref[...] = dzv_scr[...]

    zspec = lambda k: pl.BlockSpec((ts, d), lambda i, h, k=k: (i, k))
    return _run(
        body, name=name, grid=(s // ts, 2),
        in_specs=[pl.BlockSpec((ts, d), lambda i, h: (i, 0)), zspec(3), zspec(4), _const((1, d)), _const((1, d)),
                  _const((N_GROUPS, GBLK, GBLK)), _const((GBLK, N_GROUPS)), _const((d, d)), HBM],
        out_specs=[pl.BlockSpec((ts, d), lambda i, h: (i, 3 + h)), _const((8, d)),
                   _const((N_GROUPS, GBLK, GBLK)), _const((GBLK, N_GROUPS))],
        out_shape=[jax.ShapeDtypeStruct((s, D_Z), BF16), jax.ShapeDtypeStruct((8, d), F32),
                   jax.ShapeDtypeStruct((N_GROUPS, GBLK, GBLK), F32), jax.ShapeDtypeStruct((GBLK, N_GROUPS), F32)],
        scratch_shapes=[pltpu.VMEM((ts, d), F32), pltpu.VMEM((ts, d), F32), pltpu.VMEM((ts, d), BF16)],
        args=(dyb, z, z, ln_g, ln_b, ws, bst, w_out, dz), aliases={8: 0}, comm=comm)


def _mix_c_bwd(dyc, z, dz, w_pool, scale, *, ts, name):
    s = z.shape[0]
    d = D_MODEL
    ts = _tile(ts, s)
    nt = s // ts
    rev = _rev(nt)

    def body(dyc_ref, zp_ref, halo_ref, w_ref, sc_ref, dz_in, dz_ref, red_ref, dw_ref, carry):
        i = pl.program_id(0)

        @pl.when(i == 0)
        def _():
            carry[...] = jnp.zeros_like(carry)
            red_ref[...] = jnp.zeros_like(red_ref)
            dw_ref[...] = jnp.zeros_like(dw_ref)
        p = zp_ref[...]
        ext = jnp.concatenate([jnp.where(i == nt - 1, 0.0, halo_ref[...]), p], axis=0)
        denoms = _pool_denoms(rev(i), ts)
        dyv = dyc_ref[...].astype(F32)
        for k in range(len(POOL_WINDOWS)):
            cols = slice(k * POOL_GROUP, (k + 1) * POOL_GROUP)
            dk = _pool_diff(p, ext, denoms, k).astype(BF16)
            red_ref[0:1, cols] += _colsum(dyv[:, cols] * _dot(dk, w_ref[k]))
            dpre = (dyv[:, cols] * sc_ref[:, cols]).astype(BF16)
            dw_ref[k] += _dot_tn(dk, dpre)
            dd = _dot_nt(dpre, w_ref[k])
            e = dd / denoms[k]
            acc = jnp.concatenate([e, carry[:, cols]], axis=0)
            carry[:, cols] = e[:POOL_HALO]
            step = 1
            while step < POOL_WINDOWS[k]:
                acc = acc + pltpu.roll(acc, acc.shape[0] - step, 0)
                step *= 2
            dz_ref[:, cols] = (acc[:ts] - dd).astype(BF16)

    return _run(
        body, name=name, grid=(nt,),
        in_specs=[pl.BlockSpec((ts, d), lambda i: (rev(i), 0)), pl.BlockSpec((ts, d), lambda i: (rev(i), 5)),
                  _halo_spec(ts, nt, POOL_HALO, d, 5), _const((4, POOL_GROUP, POOL_GROUP)), _const((1, d)), HBM],
        out_specs=[pl.BlockSpec((ts, d), lambda i: (rev(i), 5)), _const((8, d)), _const((4, POOL_GROUP, POOL_GROUP))],
        out_shape=[jax.ShapeDtypeStruct((s, D_Z), BF16), jax.ShapeDtypeStruct((8, d), F32),
                   jax.ShapeDtypeStruct((4, POOL_GROUP, POOL_GROUP), F32)],
        scratch_shapes=[pltpu.VMEM((POOL_HALO, d), F32)],
        args=(dyc, z, z, w_pool, scale, dz), aliases={5: 0})


def _in_proj_bwd(dz, w4, dxa, x, sc, *, ts, name, comm=None):
    s, d = x.shape
    ts = _tile(ts, s)
    wd = w4.shape[2]

    def body(dz_ref, w_ref, dxa_ref, x_ref, sc_ref, dx_ref, red_ref, db_ref):
        @pl.when(pl.program_id(0) == 0)
        def _():
            red_ref[...] = jnp.zeros_like(red_ref)
            db_ref[...] = jnp.zeros_like(db_ref)
        dh = jnp.zeros((ts, d), F32)
        for j in range(N_CHIPS):
            dzj = dz_ref[:, j * wd:(j + 1) * wd]
            db_ref[0:1, j * wd:(j + 1) * wd] += _colsum(dzj.astype(F32))
            dh = dh + _dot_nt(dzj, w_ref[j])
        dx_ref[...] = dxa_ref[...] + dh * (1.0 + sc_ref[...])
        red_ref[0:1, :] += _colsum(dh * x_ref[...])
        red_ref[1:2, :] += _colsum(dh)

    row = pl.BlockSpec((ts, d), lambda i: (i, 0))
    return _run(
        body, name=name, grid=(s // ts,),
        in_specs=[pl.BlockSpec((ts, D_Z), lambda i: (i, 0)), _resident((N_CHIPS, d, wd)), row, row, _const((1, d))],
        out_specs=[row, _const((8, d)), _const((8, D_Z))],
        out_shape=[jax.ShapeDtypeStruct((s, d), F32), jax.ShapeDtypeStruct((8, d), F32),
                   jax.ShapeDtypeStruct((8, D_Z), F32)],
        args=(dz, w4, dxa, x, sc), comm=comm)


def _ada_fwd(c_all, w_ada, b_ada, *, name):
    nl, d, n = w_ada.shape
    tn = n // 2

    def body(c_ref, w_ref, b_ref, o_ref):
        cv = c_ref[...]
        ca = (cv * _sigmoid(cv)).astype(BF16)
        o_ref[0] = _dot(ca, w_ref[0].astype(BF16)) + b_ref[0]

    return _run(
        body, name=name, grid=(nl, n // tn),
        in_specs=[_const((N_DEV, d)), pl.BlockSpec((1, d, tn), lambda l, j: (l, 0, j)),
                  pl.BlockSpec((1, 1, tn), lambda l, j: (l, 0, j))],
        out_specs=[pl.BlockSpec((1, N_DEV, tn), lambda l, j: (l, 0, j))],
        out_shape=[jax.ShapeDtypeStruct((nl, N_DEV, n), F32)], args=(c_all, w_ada, b_ada))[0]


def _ada_bwd(c_all, dada, *, name):
    nl, nb, n = dada.shape
    d = c_all.shape[1]
    tn = n // 2

    def body(c_ref, g_ref, o_ref):
        cv = c_ref[...]
        ca = (cv * _sigmoid(cv)).astype(BF16)
        o_ref[0] = _dot_tn(ca, g_ref[0].astype(BF16))

    return _run(
        body, name=name, grid=(nl, n // tn),
        in_specs=[_const((nb, d)), pl.BlockSpec((1, nb, tn), lambda l, j: (l, 0, j))],
        out_specs=[pl.BlockSpec((1, d, tn), lambda l, j: (l, 0, j))],
        out_shape=[jax.ShapeDtypeStruct((nl, d, n), F32)], args=(c_all, dada))[0]


def _sum4(own, recv, *, name):
    r, c = own.shape
    tr = _row_tile(r, c, 2)

    def body(own_ref, recv_ref, o_ref):
        acc = own_ref[...]
        for k in range(N_CHIPS - 1):
            acc = acc + recv_ref[k].astype(F32)
        o_ref[...] = acc

    return _run(
        body, name=name, grid=(r // tr,),
        in_specs=[pl.BlockSpec((tr, c), lambda i: (i, 0)), pl.BlockSpec((N_CHIPS - 1, tr, c), lambda i: (0, i, 0))],
        out_specs=[pl.BlockSpec((tr, c), lambda i: (i, 0))], out_shape=[jax.ShapeDtypeStruct((r, c), F32)],
        args=(own, recv))[0]


def _sum_halves(g_f32, theirs, *, name):
    _, _, rh, c = g_f32.shape
    tr = _row_tile(rh, c, 1)

    def body(g_ref, t_ref, hb_ref, own_ref):
        mx, my, mc = _my_coords()
        j = pl.program_id(1)
        h = jnp.where(mc == 0, g_ref[0, 0], g_ref[0, 1]) + t_ref[0].astype(F32)
        hb_ref[0] = h.astype(BF16)

        @pl.when(j == 2 * mx + my)
        def _():
            own_ref[...] = h

    return _run(
        body, name=name, grid=(rh // tr, N_CHIPS),
        in_specs=[pl.BlockSpec((1, 2, tr, c), lambda i, j: (j, 0, i, 0)), pl.BlockSpec((1, tr, c), lambda i, j: (j, i, 0))],
        out_specs=[pl.BlockSpec((1, tr, c), lambda i, j: (j, i, 0)), pl.BlockSpec((tr, c), lambda i, j: (i, 0))],
        out_shape=[jax.ShapeDtypeStruct((N_CHIPS, rh, c), BF16), jax.ShapeDtypeStruct((rh, c), F32)],
        args=(g_f32, theirs))


def _row_tile(r, c, mib):
    limit = max(8, (mib << 20) // (4 * c))
    if r <= limit:
        return r
    best = 8
    for t in range(8, limit + 1, 8):
        if r % t == 0:
            best = t
    return best


def _adam_math(w, g, m, v):
    mn = ADAM_B1 * m + (1.0 - ADAM_B1) * g
    vn = ADAM_B2 * v + (1.0 - ADAM_B2) * (g * g)
    m_hat = mn / (1.0 - ADAM_B1 ** ADAM_STEP)
    v_hat = vn / (1.0 - ADAM_B2 ** ADAM_STEP)
    return -ADAM_LR * (m_hat / (jnp.sqrt(v_hat) + ADAM_EPS) + ADAM_WD * w), mn, vn


def _adamw(w, g, m, v, *, name):
    r, c = w.shape
    tr = _row_tile(r, c, 2)

    def body(w_ref, g_ref, m_ref, v_ref, d_ref, mo_ref, vo_ref):
        d_ref[...], mo_ref[...], vo_ref[...] = _adam_math(w_ref[...], g_ref[...], m_ref[...], v_ref[...])

    blk = pl.BlockSpec((tr, c), lambda i: (i, 0))
    return _run(body, name=name, grid=(r // tr,), in_specs=[blk] * 4, out_specs=[blk] * 3,
                out_shape=[jax.ShapeDtypeStruct((r, c), F32)] * 3, args=(w, g, m, v))


def _adamw_sharded(w, m, v, grads, *, name, comm=None):
    nl, r, c = w.shape
    tr = _row_tile(r, c, 1)
    nt = r // tr

    def body(w_ref, m_ref, v_ref, g0_ref, g1_ref, g_ref, d_ref, mo_ref, vo_ref):
        g = jnp.where(pl.program_id(0) == 0, g0_ref[...], g1_ref[...])
        g_ref[0] = g
        d_ref[0], mo_ref[0], vo_ref[0] = _adam_math(w_ref[0], g, m_ref[0], v_ref[0])

    blk = pl.BlockSpec((1, tr, c), lambda l, i: (l, i, 0))
    part0 = pl.BlockSpec((tr, c), lambda l, i: (jnp.where(l == 0, i, nt - 1), 0))
    part1 = pl.BlockSpec((tr, c), lambda l, i: (jnp.where(l == 1, i, 0), 0))
    return _run(body, name=name, grid=(nl, nt), in_specs=[blk] * 3 + [part0, part1],
                out_specs=[blk] * 4, out_shape=[jax.ShapeDtypeStruct((nl, r, c), F32)] * 4,
                args=(w, m, v, grads[0], grads[1]), comm=comm)


_BIG = ("w_in", "w_a_out", "w_b_out", "w_pool", "w_o", "w_up", "w_down")
_COL_SHARDED = ("w_in", "w_up")
_SMALL_SHARDED = ("conv_a", "conv_ffn")
_SMALL_REPL = ("b_in", "ln_v_g", "ln_v_b", "w_spatial", "b_spatial", "pool_scale", "ln1_g", "ln1_b", "b_up",
               "conv_ffn_b", "ln2_g", "ln2_b")
_WEIGHTS = ("w_ada", "b_ada", "w_in", "b_in", "conv_a", "w_a_out", "ln_v_g", "ln_v_b", "w_spatial", "b_spatial",
            "w_b_out", "w_pool", "pool_scale", "w_o", "ln1_g", "ln1_b", "w_up", "b_up", "conv_ffn", "conv_ffn_b",
            "w_down", "ln2_g", "ln2_b")


def _shard3(a):
    return a.reshape(a.shape[0], -1, a.shape[-1])


def _use_gathered(name, g):
    g = g.reshape(N_CHIPS, -1, g.shape[-1])
    if name in _COL_SHARDED:
        return g
    if name == "w_pool":
        return g.reshape(N_CHIPS, 4, POOL_GROUP // N_CHIPS, POOL_GROUP).transpose(1, 0, 2, 3).reshape(
            4, POOL_GROUP, POOL_GROUP)
    return g.reshape(-1, g.shape[-1])


def _grad_by_chip(name, g):
    if name in _COL_SHARDED:
        return g
    if name == "w_pool":
        return g.reshape(4, N_CHIPS, POOL_GROUP // N_CHIPS, POOL_GROUP).transpose(1, 0, 2, 3).reshape(
            N_CHIPS, POOL_GROUP, POOL_GROUP)
    return g.reshape(N_CHIPS, -1, g.shape[-1])


def _pack_small(arrs):
    parts = []
    for a in arrs:
        flat = a.reshape(-1).astype(F32)
        pad = (-flat.shape[0]) % 128
        parts.append(jnp.pad(flat, (0, pad)) if pad else flat)
    flat = jnp.concatenate(parts)
    pad = (-flat.shape[0]) % 1024
    if pad:
        flat = jnp.pad(flat, (0, pad))
    return flat.reshape(-1, 128)


def _unpack_small(buf, shapes):
    lead = buf.shape[:-2]
    flat = buf.reshape(lead + (-1,))
    out, off = [], 0
    for shp in shapes:
        n = math.prod(shp)
        out.append(flat[..., off:off + n].reshape(lead + tuple(shp)))
        off += n + ((-n) % 128)
    return out


def _as2d(a):
    return a.reshape(-1, a.shape[-1])


_LATE = ("w_a_out", "w_b_out", "w_pool", "w_o")


class _Traffic:
    def __init__(self, halves, plan):
        self.halves = halves
        self.plan = plan
        self.gathered = {}
        self.ready = {}
        self.summed = {}
        self.half = {}
        self.final = {}

    def weight(self, layer, name):
        return self.gathered[(layer, name)]

    def add_grad(self, layer, name, g_f32, g_bf16):
        def halves(g):
            g = _grad_by_chip(name, g)
            return g.reshape(N_CHIPS, 2, g.shape[1] // 2, g.shape[2])
        self.ready[(layer, name)] = (halves(g_f32), halves(g_bf16))

    def _comm(self, job):
        if job[0] == "gather":
            return _gather_comm([self.halves[k] for k in job[2]], job[1])
        if job[0] == "presum":
            return _presum_comm([self.ready[k][1] for k in job[1]])
        if job[0] == "scatter":
            return _scatter_comm([self.summed[k][0] for k in job[1]])
        return _join_comm([self.half[k] for k in job[1]])

    def _done(self, job, res):
        if job[0] == "gather":
            for k, r in zip(job[2], res):
                self.gathered[(job[1], k)] = _use_gathered(k, r)
        elif job[0] == "presum":
            for k, r in zip(job[1], res):
                self.summed[k] = _sum_halves(self.ready.pop(k)[0], r, name=f"presum_l{k[0]}_{k[1]}")
        elif job[0] == "scatter":
            for k, r in zip(job[1], res):
                self.half[k] = _sum4(self.summed.pop(k)[1], r, name=f"sum_l{k[0]}_{k[1]}")
        else:
            for k, r in zip(job[1], res):
                self.final[k] = r.reshape(-1, r.shape[-1])

    def run(self, name, fn):
        jobs = self.plan.get(name)
        if not jobs:
            return fn(None)
        comms = [self._comm(j) for j in jobs]
        outs, res = fn(_merge(comms))
        for job, r in zip(jobs, _split(comms, res)):
            self._done(job, r)
        return outs

    def alone(self, name):
        jobs = self.plan[name]
        comms = [self._comm(j) for j in jobs]
        for job, r in zip(jobs, _split(comms, _comm_call(_merge(comms), name=name))):
            self._done(job, r)


def _layer_fwd(x, ada, p, l, tr):
    sh1, sc1, gt1, sh2, sc2, gt2 = ada
    n = f"l{l}"
    z, ht = tr.run(f"{n}_in_proj", lambda cm: _mod_matmul(
        x, sc1, sh1, tr.weight(l, "w_in"), p["b_in"], ts=1024, tn=2304, name=f"{n}_in_proj", comm=cm))
    a, ya = _mix_a_fwd(z, p["conv_a"], tr.weight(l, "w_a_out"), ts=256, name=f"{n}_mix_a")
    sg, yb = tr.run(f"{n}_mix_b", lambda cm: _mix_b_fwd(
        z, p["ln_v_g"], p["ln_v_b"], p["w_spatial"], p["b_spatial_t"], tr.weight(l, "w_b_out"), ts=256,
        name=f"{n}_mix_b", comm=cm))
    dpool, yc = _mix_c_fwd(z, tr.weight(l, "w_pool"), p["pool_scale"], ts=256, name=f"{n}_mix_c")
    merged, o, x1 = tr.run(f"{n}_mix_o", lambda cm: _mix_o_fwd(
        x, z, ya, yb, yc, tr.weight(l, "w_o"), gt1, p["ln1_g"], p["ln1_b"], ts=256, name=f"{n}_mix_o", comm=cm))
    up, h2t = tr.run(f"{n}_up_proj", lambda cm: _mod_matmul(
        x1, sc2, sh2, tr.weight(l, "w_up"), p["b_up"], ts=1024, tn=1408, name=f"{n}_up_proj", comm=cm))
    ft, dn, x2 = tr.run(f"{n}_ffn", lambda cm: _ffn_fwd(
        up, x1, p["conv_ffn"], p["conv_ffn_b"], tr.weight(l, "w_down"), gt2, p["ln2_g"], p["ln2_b"], ts=256,
        name=f"{n}_ffn", comm=cm))
    saved = dict(x=x, z=z, ht=ht, a=a, ya=ya, sg=sg, yb=yb, dpool=dpool, yc=yc, merged=merged, o=o, x1=x1, h2t=h2t,
                 up=up, ft=ft, dn=dn)
    return x2, saved


def _layer_bwd(dx2, ada, p, sv, l, tr):
    sh1, sc1, gt1, sh2, sc2, gt2 = ada
    n = f"l{l}"
    ddn, dup, dx1, red_d, red_f, dbup = tr.run(f"{n}_ffn_bwd", lambda cm: _ffn_bwd(
        dx2, sv["x1"], sv["dn"], sv["up"], p["conv_ffn"], p["conv_ffn_b"], tr.weight(l, "w_down"),
        tr.weight(l, "w_up"), gt2, p["ln2_g"], sc2, ts=256, name=f"{n}_ffn_bwd", comm=cm))
    g = {}
    tr.add_grad(l, "w_down", *_grad_matmul_t(sv["ft"], ddn, tk=D_FF // N_CHIPS, tn=D_MODEL, name=f"{n}_dw_down"))
    tr.add_grad(l, "w_up", *tr.run(f"{n}_dw_up", lambda cm: _grad_matmul_t(
        sv["h2t"], dup, tk=D_MODEL, tn=FF_CHUNK, name=f"{n}_dw_up", by_chip=True, comm=cm)))
    g["ln2_g"], g["ln2_b"] = red_d[0], red_d[1]
    g["conv_ffn"], g["conv_ffn_b"], g["b_up"] = red_f[0:3], red_f[3], dbup[0]

    d_o, dxa, dz, dya, dyb, dyc, red_o = tr.run(f"{n}_mix_o_bwd", lambda cm: _mix_o_bwd(
        dx1, sv["x"], sv["o"], sv["z"], sv["ya"], sv["yb"], sv["yc"], tr.weight(l, "w_o"), gt1, p["ln1_g"], ts=256,
        name=f"{n}_mix_o_bwd", comm=cm))
    tr.add_grad(l, "w_o", *_grad_matmul(sv["merged"], d_o, ts=512, tn=1024, name=f"{n}_dw_o"))
    g["ln1_g"], g["ln1_b"] = red_o[0], red_o[1]

    dz, red_a = tr.run(f"{n}_mix_a_bwd", lambda cm: _mix_a_bwd(
        dya, sv["z"], dz, p["conv_a"], tr.weight(l, "w_a_out"), ts=256, name=f"{n}_mix_a_bwd", comm=cm))
    tr.add_grad(l, "w_a_out", *_grad_matmul(sv["a"], dya, ts=512, tn=1024, name=f"{n}_dw_a_out"))
    g["conv_a"] = red_a[0:3]

    dz, red_b, dws, dbst = tr.run(f"{n}_mix_b_bwd", lambda cm: _mix_b_bwd(
        dyb, sv["z"], dz, p["ln_v_g"], p["ln_v_b"], p["w_spatial"], p["b_spatial_t"], tr.weight(l, "w_b_out"), ts=256,
        name=f"{n}_mix_b_bwd", comm=cm))
    tr.add_grad(l, "w_b_out", *_grad_matmul(sv["sg"], dyb, ts=512, tn=1024, name=f"{n}_dw_b_out"))
    g["ln_v_g"], g["ln_v_b"], g["w_spatial"], g["b_spatial"] = red_b[0], red_b[1], dws, dbst.T

    dz, red_c, dwp = _mix_c_bwd(dyc, sv["z"], dz, tr.weight(l, "w_pool"), p["pool_scale"], ts=256,
                                name=f"{n}_mix_c_bwd")
    g["pool_scale"] = red_c[0]
    tr.add_grad(l, "w_pool", dwp, dwp.astype(BF16))

    tr.add_grad(l, "w_in", *tr.run(f"{n}_dw_in", lambda cm: _grad_matmul_t(
        sv["ht"], dz, tk=D_MODEL, tn=1152, name=f"{n}_dw_in", by_chip=True, comm=cm)))
    if f"{n}_presum_tail" in tr.plan:
        tr.alone(f"{n}_presum_tail")
    dx, red_i, dbin = tr.run(f"{n}_in_proj_bwd", lambda cm: _in_proj_bwd(
        dz, tr.weight(l, "w_in"), dxa, sv["x"], sc1, ts=256, name=f"{n}_in_proj_bwd", comm=cm))
    g["b_in"] = dbin[0]
    dada = jnp.stack([red_i[1], red_i[0], red_o[2], red_d[4], red_d[3], red_d[2]])
    return dx, g, dada


def _traffic_plan():
    plan = {
        "gather_l0": [("gather", 0, ("w_in",) + _LATE)],
        "l0_in_proj": [("gather", 1, ("w_in",))],
        "l0_mix_b": [("gather", 0, ("w_down",))],
        "l0_mix_o": [("gather", 0, ("w_up",))],
        "l0_up_proj": [("gather", 1, _LATE)],
        "l0_ffn": [("gather", 1, ("w_down",))],
        "l1_in_proj": [("gather", 1, ("w_up",))],
    }
    for l in reversed(range(DEPTH)):
        late = [(l, k) for k in _LATE]
        plan.update({
            f"l{l}_dw_up": [("presum", [(l, "w_down")])],
            f"l{l}_mix_o_bwd": [("presum", [(l, "w_up")]), ("scatter", [(l, "w_down")])],
            f"l{l}_mix_b_bwd": [("scatter", [(l, "w_up")])],
            f"l{l}_dw_in": [("presum", late), ("join", [(l, "w_down"), (l, "w_up")])],
        })
    late0, late1 = [(0, k) for k in _LATE], [(1, k) for k in _LATE]
    plan["l1_in_proj_bwd"] = [("presum", [(1, "w_in")]), ("scatter", late1)]
    plan["l0_ffn_bwd"] = [("scatter", [(1, "w_in")]), ("join", late1)]
    plan["l0_dw_up"] = plan["l0_dw_up"] + [("join", [(1, "w_in")])]
    plan["l0_presum_tail"] = [("presum", [(0, "w_in")])]
    plan["l0_in_proj_bwd"] = [("scatter", [(0, "w_in")] + late0)]
    plan["adamw_w_up"] = [("join", [(0, "w_in")] + late0)]
    return plan


def kernel(x, c, w_ada, b_ada, w_in, b_in, conv_a, w_a_out, ln_v_g, ln_v_b, w_spatial, b_spatial, w_b_out, w_pool, pool_scale, w_o, ln1_g, ln1_b, w_up, b_up, conv_ffn, conv_ffn_b, w_down, ln2_g, ln2_b, loss_target, m_w_ada, m_b_ada, m_w_in, m_b_in, m_conv_a, m_w_a_out, m_ln_v_g, m_ln_v_b, m_w_spatial, m_b_spatial, m_w_b_out, m_w_pool, m_pool_scale, m_w_o, m_ln1_g, m_ln1_b, m_w_up, m_b_up, m_conv_ffn, m_conv_ffn_b, m_w_down, m_ln2_g, m_ln2_b, v_w_ada, v_b_ada, v_w_in, v_b_in, v_conv_a, v_w_a_out, v_ln_v_g, v_ln_v_b, v_w_spatial, v_b_spatial, v_w_b_out, v_w_pool, v_pool_scale, v_w_o, v_ln1_g, v_ln1_b, v_w_up, v_b_up, v_conv_ffn, v_conv_ffn_b, v_w_down, v_ln2_g, v_ln2_b):
    args = locals()
    w = {k: args[k] for k in _WEIGHTS}
    m = {k: args["m_" + k] for k in _WEIGHTS}
    v = {k: args["v_" + k] for k in _WEIGHTS}
    d = D_MODEL
    mx, my, mc = _my_coords()
    chip = 2 * mx + my
    me = 4 * mx + 2 * my + mc

    small_shapes = [c.shape, conv_a.shape, conv_ffn.shape]
    small_all = _all_gather8(_pack_small([c, conv_a, conv_ffn]), name="gather_small")
    c_all, conv_a_st, conv_ffn_st = _unpack_small(small_all, small_shapes)
    c_all = c_all.reshape(N_DEV, d)
    conv_full = {"conv_a": jnp.concatenate([conv_a_st[2 * j] for j in range(N_CHIPS)], axis=-1),
                 "conv_ffn": jnp.concatenate([conv_ffn_st[2 * j] for j in range(N_CHIPS)], axis=-1)}

    halves = {}
    for k in _BIG:
        s3 = _shard3(w[k]).astype(BF16)
        halves[k] = s3.reshape(s3.shape[0], 2, s3.shape[1] // 2, s3.shape[2])
    tr = _Traffic(halves, _traffic_plan())
    tr.alone("gather_l0")

    n_ada = w_ada.shape[2]
    b_ada_mine = lax.dynamic_slice_in_dim(b_ada, chip * n_ada, n_ada, axis=1)
    ada_part = _ada_fwd(c_all, w_ada, b_ada_mine.reshape(DEPTH, 1, n_ada), name="ada_fwd")
    ada_all = _all_gather8(_pack_small([ada_part]), name="gather_ada")
    ada_st = _unpack_small(ada_all, [ada_part.shape])[0][0::2]
    ada_rows = jnp.concatenate([ada_st[j] for j in range(N_CHIPS)], axis=-1)
    ada_mine = lax.dynamic_index_in_dim(ada_rows, me, axis=1, keepdims=False)

    def layer_params(l):
        p = {k: conv_full[k][l] for k in _SMALL_SHARDED}
        for k in ("b_in", "ln_v_g", "ln_v_b", "pool_scale", "ln1_g", "ln1_b", "b_up", "conv_ffn_b", "ln2_g", "ln2_b"):
            p[k] = w[k][l].reshape(1, -1)
        p["w_spatial"] = w_spatial[l]
        p["b_spatial_t"] = b_spatial[l].T
        return p

    xs = x[0]
    saved, adas, params = [], [], []
    for l in range(DEPTH):
        ada = [ada_mine[l, k * d:(k + 1) * d].reshape(1, d) for k in range(6)]
        p = layer_params(l)
        xs, sv = _layer_fwd(xs, ada, p, l, tr)
        saved.append(sv), adas.append(ada), params.append(p)
    dx, loss_blk = _loss_fwd(xs, loss_target[0], ts=512, name="loss")

    grads, dadas = [None] * DEPTH, [None] * DEPTH
    for l in reversed(range(DEPTH)):
        dx, grads[l], dadas[l] = _layer_bwd(dx, adas[l], params[l], saved[l], l, tr)
    dada = jnp.stack(dadas).reshape(DEPTH, 6 * d)

    small_names = _SMALL_REPL + _SMALL_SHARDED
    small_g = [jnp.stack([grads[l][k] for l in range(DEPTH)]) for k in small_names]
    gsum = dict(zip(small_names, _unpack_small(_all_reduce_small(_pack_small(small_g), name="reduce_small"),
                                               [a.shape for a in small_g])))
    tail_g = [dada, loss_blk[0:1, 0:1]]
    tail_all, tail_sum = _all_gather8(_pack_small(tail_g), name="gather_dada", with_sum=True)
    gsum["b_ada"], loss_sum = _unpack_small(tail_sum, [a.shape for a in tail_g])
    loss = loss_sum[0, 0]
    dada_all = _unpack_small(tail_all, [a.shape for a in tail_g])[0]
    for k in _SMALL_SHARDED:
        wd = gsum[k].shape[-1] // N_CHIPS
        gsum[k] = lax.dynamic_slice_in_dim(gsum[k], chip * wd, wd, axis=gsum[k].ndim - 1)

    dada_cols = lax.dynamic_slice_in_dim(dada_all, chip * n_ada, n_ada, axis=2)
    dada_cols = jnp.pad(jnp.swapaxes(dada_cols, 0, 1), ((0, 0), (0, N_DEV), (0, 0)))
    gsum["w_ada"] = _ada_bwd(jnp.pad(c_all, ((0, N_DEV), (0, 0))), dada_cols, name="ada_bwd")

    out_g, out_d, out_m, out_v = {}, {}, {}, {}
    for k in ("w_up",) + tuple(k for k in _WEIGHTS if k != "w_up"):
        shp = w[k].shape
        if k in _BIG:
            res = tr.run(f"adamw_{k}", lambda cm: _adamw_sharded(
                _shard3(w[k]), _shard3(m[k]), _shard3(v[k]), [tr.final[(l, k)] for l in range(DEPTH)],
                name=f"adamw_{k}", comm=cm))
        else:
            gk = gsum[k].reshape(shp)
            res = [gk] + list(_adamw(_as2d(w[k]), _as2d(gk), _as2d(m[k]), _as2d(v[k]), name=f"adamw_{k}"))
        out_g[k], out_d[k], out_m[k], out_v[k] = [r.reshape(shp) for r in res]

    return (loss, dx[None], *[out_g[k] for k in _WEIGHTS], *[out_d[k] for k in _WEIGHTS],
            *[out_m[k] for k in _WEIGHTS], *[out_v[k] for k in _WEIGHTS])
```

```python
import math
from typing import Callable, NamedTuple

import jax
import jax.numpy as jnp
from jax import lax
from jax.experimental import pallas as pl
from jax.experimental.pallas import tpu as pltpu

F32 = jnp.float32
BF16 = jnp.bfloat16

D_MODEL = 1024
D_Z = 9216
D_FF = 2816
N_GROUPS = 8
GBLK = 128
CHUNK = 64
POOL_WINDOWS = (2, 4, 8, 16)
POOL_GROUP = 256
POOL_HALO = 16
CONV_HALO = 8
DEPTH = 2
ALPHA = (2 * DEPTH) ** 0.25
LN_EPS = 1e-5
ADAM_LR, ADAM_B1, ADAM_B2, ADAM_EPS, ADAM_WD, ADAM_STEP = 0.001, 0.9, 0.999, 1e-08, 0.01, 10
N_CHIPS = 4
N_DEV = 8
FF_CHUNK = 1408
MESH = pl.DeviceIdType.MESH
VMEM_LIMIT = 56 * 1024 * 1024
HBM = pl.BlockSpec(memory_space=pl.ANY)


def _dot(a, b):
    return jnp.dot(a, b, preferred_element_type=F32)


def _dot_nt(a, b):
    return lax.dot_general(a, b, (((1,), (1,)), ((), ())), preferred_element_type=F32)


def _dot_tn(a, b):
    return lax.dot_general(a, b, (((0,), (0,)), ((), ())), preferred_element_type=F32)


_GELU_C = math.sqrt(2.0 / math.pi)


def _gelu_and_grad(x):
    x2 = x * x
    t = jnp.tanh(_GELU_C * (x + 0.044715 * x * x2))
    g = 0.5 * x * (1.0 + t)
    dg = 0.5 * (1.0 + t) + 0.5 * x * (1.0 - t * t) * (_GELU_C * (1.0 + 3 * 0.044715 * x2))
    return g, dg


def _gelu(x):
    return 0.5 * x * (1.0 + jnp.tanh(_GELU_C * (x + 0.044715 * x * x * x)))


def _sigmoid(x):
    return 1.0 / (1.0 + jnp.exp(-x))


def _ln_fwd(r):
    mu = jnp.mean(r, axis=-1, keepdims=True)
    xc = r - mu
    var = jnp.mean(xc * xc, axis=-1, keepdims=True)
    rstd = lax.rsqrt(var + LN_EPS)
    return xc * rstd, rstd


def _ln_bwd(dy, g, xhat, rstd):
    dxh = dy * g
    m1 = jnp.mean(dxh, axis=-1, keepdims=True)
    m2 = jnp.mean(dxh * xhat, axis=-1, keepdims=True)
    return rstd * (dxh - m1 - xhat * m2)


def _rows_before(ext, k, halo):
    return pltpu.roll(ext, k, 0)[halo:]


def _rows_after(ext, k, n):
    return pltpu.roll(ext, ext.shape[0] - k, 0)[:n]


def _colsum(v):
    return jnp.sum(v, axis=0, keepdims=True)


def _spatial_mask():
    i = lax.broadcasted_iota(jnp.int32, (GBLK, GBLK), 0)
    j = lax.broadcasted_iota(jnp.int32, (GBLK, GBLK), 1)
    return (j // CHUNK) <= (i // CHUNK)


def _const(shape):
    n = len(shape)
    return pl.BlockSpec(shape, lambda *_: (0,) * n)


def _resident(shape):
    n = len(shape)
    return pl.BlockSpec(shape, lambda *_: (0,) * n, pipeline_mode=pl.Buffered(1))


def _tile(ts, s):
    return min(ts, s)


class _Comm(NamedTuple):
    srcs: tuple
    dsts: tuple
    n_remote: int
    n_local: int
    build: Callable
    alias: tuple = ()


def _my_coords():
    return lax.axis_index("x"), lax.axis_index("y"), lax.axis_index("c")


def _chip_peer(k):
    mx, my, mc = _my_coords()
    return (mx ^ ((k >> 1) & 1), my ^ (k & 1), mc)


def _sem_scratch(comm):
    return [pltpu.SemaphoreType.DMA((max(comm.n_remote, 1),)), pltpu.SemaphoreType.DMA((max(comm.n_remote, 1),)),
            pltpu.SemaphoreType.DMA((max(comm.n_local, 1),))]


def _run(body, *, name, grid, in_specs, out_specs, out_shape, args, scratch_shapes=(), comm=None, aliases=None):
    sem = ("arbitrary",) * len(grid)
    cparams = pltpu.CompilerParams(dimension_semantics=sem, vmem_limit_bytes=VMEM_LIMIT)
    aliases = dict(aliases or {})
    if comm is None:
        return pl.pallas_call(body, name=name, grid=grid, in_specs=in_specs, out_specs=out_specs, out_shape=out_shape,
                              scratch_shapes=list(scratch_shapes), compiler_params=cparams,
                              input_output_aliases=aliases)(*args)
    n_in, n_cs, n_out, n_cd, n_scr = len(in_specs), len(comm.srcs), len(out_specs), len(comm.dsts), len(scratch_shapes)
    aliases.update({n_in + si: n_out + di for si, di in comm.alias})
    kwargs = {"input_output_aliases": aliases}
    total = math.prod(grid)
    mid_step = min(total - 1, int(total * 0.7))

    def wrapped(*refs):
        ins, refs = refs[:n_in], refs[n_in:]
        csrc, refs = refs[:n_cs], refs[n_cs:]
        outs, refs = refs[:n_out], refs[n_out:]
        cdst, refs = refs[:n_cd], refs[n_cd:]
        scr, sems = refs[:n_scr], refs[n_scr:]
        step = pl.program_id(0)
        for ax in range(1, len(grid)):
            step = step * grid[ax] + pl.program_id(ax)
        first, mid, last = comm.build(csrc, cdst, *sems, 0, 0)
        pl.when(step == 0)(first)
        if mid is not None:
            pl.when(step == mid_step)(mid)
        body(*ins, *outs, *scr)
        pl.when(step == total - 1)(last)

    res = pl.pallas_call(
        wrapped, name=name, grid=grid, in_specs=list(in_specs) + [HBM] * n_cs, out_specs=list(out_specs) + [HBM] * n_cd,
        out_shape=list(out_shape) + list(comm.dsts), scratch_shapes=list(scratch_shapes) + _sem_scratch(comm),
        compiler_params=cparams, **kwargs)(*args, *comm.srcs)
    return res[:n_out], res[n_out:]


def _comm_call(comm, *, name):
    def body(*refs):
        n_cs, n_cd = len(comm.srcs), len(comm.dsts)
        first, mid, last = comm.build(refs[:n_cs], refs[n_cs:n_cs + n_cd], *refs[n_cs + n_cd:], 0, 0)
        first()
        if mid is not None:
            mid()
        last()

    return pl.pallas_call(body, name=name, in_specs=[HBM] * len(comm.srcs), out_specs=[HBM] * len(comm.dsts),
                          out_shape=list(comm.dsts), scratch_shapes=_sem_scratch(comm),
                          input_output_aliases=dict(comm.alias))(*comm.srcs)


def _gather_comm(bufs):
    dsts = tuple(jax.ShapeDtypeStruct(b.shape, b.dtype) for b in bufs)
    nw = len(bufs)

    def build(srcs, outs, send_sems, recv_sems, local_sems, r0, l0):
        mx, my, mc = _my_coords()
        me = 2 * mx + my
        sibling = (mx, my, 1 - mc)

        def rdma(src, dst, idx, peer):
            return pltpu.make_async_remote_copy(src_ref=src, dst_ref=dst, send_sem=send_sems.at[r0 + idx],
                                                recv_sem=recv_sems.at[r0 + idx], device_id=peer, device_id_type=MESH)

        def ici(w, k, slot):
            return rdma(outs[w].at[me, mc], outs[w].at[slot, mc], 6 * w + k - 1, _chip_peer(k))

        def fwd(w, k, half):
            return rdma(outs[w].at[me ^ k, mc], outs[w].at[me ^ k, half], 6 * w + 2 + k, sibling)

        def first():
            for w in range(nw):
                for k in range(1, N_CHIPS):
                    ici(w, k, me).start()

        def mid():
            for w in range(nw):
                for k in range(1, N_CHIPS):
                    ici(w, k, me ^ k).wait_recv()
                    fwd(w, k, mc).start()

        def last():
            for w in range(nw):
                for k in range(1, N_CHIPS):
                    fwd(w, k, 1 - mc).wait_recv()
                    ici(w, k, me).wait_send()
                    fwd(w, k, mc).wait_send()

        return first, mid, last

    return _Comm(tuple(bufs), dsts, 6 * nw, 0, build, tuple((w, w) for w in range(nw)))


def _symmetric(make_remote, make_local, make_incoming=None):
    def first():
        for cp in make_remote() + make_local():
            cp.start()

    def last():
        for cp in (make_incoming or make_remote)():
            cp.wait_recv()
        for cp in make_remote():
            cp.wait_send()
        for cp in make_local():
            cp.wait()

    return first, None, last


def _presum_comm(g_bf16):
    nw = len(g_bf16)
    dsts = tuple(jax.ShapeDtypeStruct((N_CHIPS,) + g.shape[2:], BF16) for g in g_bf16)

    def build(srcs, outs, send_sems, recv_sems, local_sems, r0, l0):
        mx, my, mc = _my_coords()

        def remote():
            return [pltpu.make_async_remote_copy(
                src_ref=srcs[w].at[j, 1 - mc], dst_ref=outs[w].at[j], send_sem=send_sems.at[r0 + N_CHIPS * w + j],
                recv_sem=recv_sems.at[r0 + N_CHIPS * w + j], device_id=(mx, my, 1 - mc), device_id_type=MESH)
                for w in range(nw) for j in range(N_CHIPS)]

        return _symmetric(remote, lambda: [])

    return _Comm(tuple(g_bf16), dsts, N_CHIPS * nw, 0, build)


def _scatter_comm(h_bf16):
    nw = len(h_bf16)
    dsts = tuple(jax.ShapeDtypeStruct((N_CHIPS - 1,) + h.shape[1:], BF16) for h in h_bf16)

    def build(srcs, outs, send_sems, recv_sems, local_sems, r0, l0):
        mx, my, _ = _my_coords()
        me = 2 * mx + my

        def remote():
            return [pltpu.make_async_remote_copy(
                src_ref=srcs[w].at[me ^ k], dst_ref=outs[w].at[k - 1], send_sem=send_sems.at[r0 + 3 * w + k - 1],
                recv_sem=recv_sems.at[r0 + 3 * w + k - 1], device_id=_chip_peer(k), device_id_type=MESH)
                for w in range(nw) for k in range(1, N_CHIPS)]

        return _symmetric(remote, lambda: [])

    return _Comm(tuple(h_bf16), dsts, 3 * nw, 0, build)


def _join_comm(bufs):
    nw = len(bufs)
    dsts = tuple(jax.ShapeDtypeStruct(b.shape, b.dtype) for b in bufs)

    def build(srcs, outs, send_sems, recv_sems, local_sems, r0, l0):
        mx, my, mc = _my_coords()

        def remote(half=mc):
            return [pltpu.make_async_remote_copy(
                src_ref=outs[w].at[mc], dst_ref=outs[w].at[half], send_sem=send_sems.at[r0 + w],
                recv_sem=recv_sems.at[r0 + w], device_id=(mx, my, 1 - mc), device_id_type=MESH) for w in range(nw)]

        return _symmetric(remote, lambda: [], lambda: remote(1 - mc))

    return _Comm(tuple(bufs), dsts, nw, 0, build, tuple((w, w) for w in range(nw)))


def _merge(comms):
    comms = list(comms)
    if len(comms) == 1:
        return comms[0]

    def build(srcs, outs, send_sems, recv_sems, local_sems, r0, l0):
        phases, s0, d0 = [], 0, 0
        for cm in comms:
            phases.append(cm.build(srcs[s0:s0 + len(cm.srcs)], outs[d0:d0 + len(cm.dsts)], send_sems, recv_sems,
                                   local_sems, r0, l0))
            s0, d0, r0, l0 = s0 + len(cm.srcs), d0 + len(cm.dsts), r0 + cm.n_remote, l0 + cm.n_local

        def run(idx):
            fns = [ph[idx] for ph in phases if ph[idx] is not None]
            if not fns:
                return None

            def go():
                for fn in fns:
                    fn()
            return go

        return run(0), run(1), run(2)

    alias, s0, d0 = [], 0, 0
    for cm in comms:
        alias += [(s0 + si, d0 + di) for si, di in cm.alias]
        s0, d0 = s0 + len(cm.srcs), d0 + len(cm.dsts)
    return _Comm(sum((cm.srcs for cm in comms), ()), sum((cm.dsts for cm in comms), ()),
                 sum(cm.n_remote for cm in comms), sum(cm.n_local for cm in comms), build, tuple(alias))


def _split(comms, res):
    out, d0 = [], 0
    for cm in comms:
        out.append(list(res[d0:d0 + len(cm.dsts)]))
        d0 += len(cm.dsts)
    return out


def _all_reduce_small(x, *, name):
    r, lanes = x.shape

    def body(x_ref, out_ref, sib_ref, slots_ref, send_sems, recv_sems):
        mx, my, mc = _my_coords()
        me = 2 * mx + my
        swap = pltpu.make_async_remote_copy(src_ref=x_ref, dst_ref=sib_ref, send_sem=send_sems.at[0],
                                            recv_sem=recv_sems.at[0], device_id=(mx, my, 1 - mc), device_id_type=MESH)
        swap.start()
        swap.wait_recv()
        swap.wait_send()
        slots_ref[me] = x_ref[...] + sib_ref[...]

        def copy(k, slot):
            return pltpu.make_async_remote_copy(
                src_ref=slots_ref.at[me], dst_ref=slots_ref.at[slot], send_sem=send_sems.at[k], recv_sem=recv_sems.at[k],
                device_id=_chip_peer(k), device_id_type=MESH)

        sends = [copy(k, me) for k in range(1, N_CHIPS)]
        for cp in sends:
            cp.start()
        for k in range(1, N_CHIPS):
            copy(k, me ^ k).wait_recv()
        for cp in sends:
            cp.wait_send()
        acc = slots_ref[0]
        for j in range(1, N_CHIPS):
            acc = acc + slots_ref[j]
        out_ref[...] = acc

    vmem = pl.BlockSpec(memory_space=pltpu.VMEM)
    return pl.pallas_call(
        body, name=name, in_specs=[vmem], out_specs=vmem, out_shape=jax.ShapeDtypeStruct((r, lanes), F32),
        scratch_shapes=[pltpu.VMEM((r, lanes), F32), pltpu.VMEM((N_CHIPS, r, lanes), F32),
                        pltpu.SemaphoreType.DMA((N_CHIPS,)), pltpu.SemaphoreType.DMA((N_CHIPS,))],
        compiler_params=pltpu.CompilerParams(vmem_limit_bytes=VMEM_LIMIT),
    )(x)


def _all_gather8(x, *, name, with_sum=False):
    r, lanes = x.shape

    def body(x_ref, out_ref, *rest):
        if with_sum:
            sum_ref, send_sems, recv_sems, local_sem = rest
        else:
            send_sems, recv_sems, local_sem = rest
        mx, my, mc = _my_coords()
        me = 4 * mx + 2 * my + mc

        def peer(k):
            return (mx ^ ((k >> 2) & 1), my ^ ((k >> 1) & 1), mc ^ (k & 1))

        def copy(k, slot):
            return pltpu.make_async_remote_copy(
                src_ref=x_ref, dst_ref=out_ref.at[slot], send_sem=send_sems.at[k - 1], recv_sem=recv_sems.at[k - 1],
                device_id=peer(k), device_id_type=MESH)

        mine = pltpu.make_async_copy(x_ref, out_ref.at[me], local_sem)
        mine.start()
        sends = [copy(k, me) for k in range(1, N_DEV)]
        for cp in sends:
            cp.start()
        for k in range(1, N_DEV):
            copy(k, me ^ k).wait_recv()
        for cp in sends:
            cp.wait_send()
        mine.wait()
        if with_sum:
            acc = out_ref[0]
            for k in range(1, N_DEV):
                acc = acc + out_ref[k]
            sum_ref[...] = acc

    vmem = pl.BlockSpec(memory_space=pltpu.VMEM)
    out_shape = [jax.ShapeDtypeStruct((N_DEV, r, lanes), F32)]
    if with_sum:
        out_shape.append(jax.ShapeDtypeStruct((r, lanes), F32))
    res = pl.pallas_call(
        body, name=name, in_specs=[vmem], out_specs=[vmem] * len(out_shape), out_shape=out_shape,
        scratch_shapes=[pltpu.SemaphoreType.DMA((N_DEV - 1,)), pltpu.SemaphoreType.DMA((N_DEV - 1,)),
                        pltpu.SemaphoreType.DMA],
        compiler_params=pltpu.CompilerParams(vmem_limit_bytes=VMEM_LIMIT),
    )(x)
    return res if with_sum else res[0]


def _mod_matmul(x, sc, sh, w4, b, *, ts, tn, name, comm=None):
    s, d = x.shape
    wd = w4.shape[2]
    n = N_CHIPS * wd
    per = wd // tn
    ts = _tile(ts, s)

    def body(x_ref, sc_ref, sh_ref, w_ref, b_ref, o_ref, ht_ref, h_scr):
        @pl.when(pl.program_id(1) == 0)
        def _():
            h = x_ref[...] * (1.0 + sc_ref[...]) + sh_ref[...]
            h_scr[...] = h.astype(BF16)
            ht_ref[...] = h.T.astype(BF16)
        o_ref[...] = _dot(h_scr[...], w_ref[0]) + b_ref[...]

    return _run(
        body, name=name, grid=(s // ts, n // tn),
        in_specs=[pl.BlockSpec((ts, d), lambda i, j: (i, 0)), _const((1, d)), _const((1, d)),
                  pl.BlockSpec((1, d, tn), lambda i, j: (j // per, 0, j % per)),
                  pl.BlockSpec((1, tn), lambda i, j: (0, j))],
        out_specs=[pl.BlockSpec((ts, tn), lambda i, j: (i, j)), pl.BlockSpec((d, ts), lambda i, j: (0, i))],
        out_shape=[jax.ShapeDtypeStruct((s, n), F32), jax.ShapeDtypeStruct((d, s), BF16)],
        scratch_shapes=[pltpu.VMEM((ts, d), BF16)],
        args=(x, sc, sh, w4, b), comm=comm)


def _conv3(q, ext, cw):
    return cw[2:3] * q + cw[1:2] * _rows_before(ext, 1, CONV_HALO) + cw[0:1] * _rows_before(ext, 2, CONV_HALO)


def _mix_a_fwd(z, cw, w_out, *, ts, name):
    s = z.shape[0]
    d = D_MODEL
    ts = _tile(ts, s)

    def body(zb_ref, zc_ref, zx_ref, cw_ref, w_ref, a_ref, y_ref, carry):
        @pl.when(pl.program_id(0) == 0)
        def _():
            carry[...] = jnp.zeros_like(carry)
        q = zc_ref[...] * zx_ref[...]
        ext = jnp.concatenate([carry[...], q], axis=0)
        a = (zb_ref[...] * _conv3(q, ext, cw_ref[...])).astype(BF16)
        carry[...] = q[ts - CONV_HALO:]
        a_ref[...] = a
        y_ref[...] = _dot(a, w_ref[...])

    zspec = lambda k: pl.BlockSpec((ts, d), lambda i, k=k: (i, k))
    return _run(
        body, name=name, grid=(s // ts,),
        in_specs=[zspec(0), zspec(1), zspec(2), _const((3, d)), _const((d, d))],
        out_specs=[pl.BlockSpec((ts, d), lambda i: (i, 0))] * 2,
        out_shape=[jax.ShapeDtypeStruct((s, d), BF16), jax.ShapeDtypeStruct((s, d), F32)],
        scratch_shapes=[pltpu.VMEM((CONV_HALO, d), F32)],
        args=(z, z, z, cw, w_out))


def _spatial_mix(vn_b, ws_ref, bst_ref, mixed_scr, ts):
    nblk = ts // GBLK
    mask = _spatial_mask()
    for g in range(N_GROUPS):
        cols = slice(g * GBLK, (g + 1) * GBLK)
        wm = jnp.where(mask, ws_ref[g], 0.0).astype(BF16)
        cat = jnp.concatenate([vn_b[n * GBLK:(n + 1) * GBLK, cols] for n in range(nblk)], axis=1)
        res = _dot(wm, cat) + bst_ref[:, g:g + 1]
        for n in range(nblk):
            mixed_scr[n * GBLK:(n + 1) * GBLK, cols] = res[:, n * GBLK:(n + 1) * GBLK]


def _mix_b_fwd(z, ln_g, ln_b, ws, bst, w_out, *, ts, name, comm=None):
    s = z.shape[0]
    d = D_MODEL
    ts = _tile(ts, s)

    def body(zu_ref, zv_ref, g_ref, b_ref, ws_ref, bst_ref, w_ref, sg_ref, y_ref, mixed_scr):
        xhat, _ = _ln_fwd(_gelu(zv_ref[...]))
        vn = (xhat * g_ref[...] + b_ref[...]).astype(BF16)
        _spatial_mix(vn, ws_ref, bst_ref, mixed_scr, ts)
        sg = (_gelu(zu_ref[...]) * mixed_scr[...]).astype(BF16)
        sg_ref[...] = sg
        y_ref[...] = _dot(sg, w_ref[...])

    zspec = lambda k: pl.BlockSpec((ts, d), lambda i, k=k: (i, k))
    return _run(
        body, name=name, grid=(s // ts,),
        in_specs=[zspec(3), zspec(4), _const((1, d)), _const((1, d)), _const((N_GROUPS, GBLK, GBLK)),
                  _const((GBLK, N_GROUPS)), _const((d, d))],
        out_specs=[pl.BlockSpec((ts, d), lambda i: (i, 0))] * 2,
        out_shape=[jax.ShapeDtypeStruct((s, d), BF16), jax.ShapeDtypeStruct((s, d), F32)],
        scratch_shapes=[pltpu.VMEM((ts, d), F32)],
        args=(z, z, ln_g, ln_b, ws, bst, w_out), comm=comm)


def _pool_denoms(tile_idx, ts):
    t1 = (tile_idx * ts + 1 + lax.broadcasted_iota(jnp.int32, (ts, 1), 0)).astype(F32)
    return [jnp.minimum(t1, float(w)) for w in POOL_WINDOWS]


def _pool_diff(p, ext, denoms, k):
    cols = slice(k * POOL_GROUP, (k + 1) * POOL_GROUP)
    acc = ext[:, cols]
    step = 1
    while step < POOL_WINDOWS[k]:
        acc = acc + pltpu.roll(acc, step, 0)
        step *= 2
    return acc[POOL_HALO:] / denoms[k] - p[:, cols]


def _mix_c_fwd(z, w_pool, scale, *, ts, name):
    s = z.shape[0]
    d = D_MODEL
    ts = _tile(ts, s)

    def body(zp_ref, w_ref, sc_ref, d_ref, y_ref, carry):
        i = pl.program_id(0)

        @pl.when(i == 0)
        def _():
            carry[...] = jnp.zeros_like(carry)
        p = zp_ref[...]
        ext = jnp.concatenate([carry[...], p], axis=0)
        carry[...] = p[ts - POOL_HALO:]
        denoms = _pool_denoms(i, ts)
        for k in range(len(POOL_WINDOWS)):
            cols = slice(k * POOL_GROUP, (k + 1) * POOL_GROUP)
            dk = _pool_diff(p, ext, denoms, k).astype(BF16)
            d_ref[:, cols] = dk
            y_ref[:, cols] = _dot(dk, w_ref[k]) * sc_ref[:, cols]

    return _run(
        body, name=name, grid=(s // ts,),
        in_specs=[pl.BlockSpec((ts, d), lambda i: (i, 5)), _const((4, POOL_GROUP, POOL_GROUP)), _const((1, d))],
        out_specs=[pl.BlockSpec((ts, d), lambda i: (i, 0))] * 2,
        out_shape=[jax.ShapeDtypeStruct((s, d), BF16), jax.ShapeDtypeStruct((s, d), F32)],
        scratch_shapes=[pltpu.VMEM((POOL_HALO, d), F32)],
        args=(z, w_pool, scale))


def _mix_o_fwd(x, z, ya, yb, yc, w_o, gt, ln_g, ln_b, *, ts, name, comm=None):
    s, d = x.shape
    ts = _tile(ts, s)

    def body(x_ref, ga_ref, gb_ref, gc_ref, ya_ref, yb_ref, yc_ref, w_ref, gt_ref, g_ref, b_ref,
             m_ref, o_ref, x1_ref):
        merged = (_sigmoid(ga_ref[...]) * ya_ref[...] + _sigmoid(gb_ref[...]) * yb_ref[...]
                  + _sigmoid(gc_ref[...]) * yc_ref[...]).astype(BF16)
        m_ref[...] = merged
        o = _dot(merged, w_ref[...])
        o_ref[...] = o
        xhat, _ = _ln_fwd(ALPHA * x_ref[...] + gt_ref[...] * o)
        x1_ref[...] = xhat * g_ref[...] + b_ref[...]

    row = pl.BlockSpec((ts, d), lambda i: (i, 0))
    zspec = lambda k: pl.BlockSpec((ts, d), lambda i, k=k: (i, k))
    return _run(
        body, name=name, grid=(s // ts,),
        in_specs=[row, zspec(6), zspec(7), zspec(8), row, row, row, _const((d, d)),
                  _const((1, d)), _const((1, d)), _const((1, d))],
        out_specs=[row] * 3,
        out_shape=[jax.ShapeDtypeStruct((s, d), BF16), jax.ShapeDtypeStruct((s, d), F32),
                   jax.ShapeDtypeStruct((s, d), F32)],
        args=(x, z, z, z, ya, yb, yc, w_o, gt, ln_g, ln_b), comm=comm)


def _ffn_fwd(up, x1, cw, cb, w_down, gt, ln_g, ln_b, *, ts, name, comm=None):
    s, d = x1.shape
    ts = _tile(ts, s)

    def body(up_ref, x1_ref, cw_ref, cb_ref, w_ref, gt_ref, g_ref, b_ref, ft_ref, dn_ref, x2_ref, carry, f_ref):
        @pl.when(pl.program_id(0) == 0)
        def _():
            carry[...] = jnp.zeros_like(carry)
        for c in range(D_FF // FF_CHUNK):
            ca = slice(c * FF_CHUNK, (c + 1) * FF_CHUNK)
            cg = slice(D_FF + c * FF_CHUNK, D_FF + (c + 1) * FF_CHUNK)
            ua = up_ref[:, ca]
            ext = jnp.concatenate([carry[:, ca], ua], axis=0)
            carry[:, ca] = ua[ts - CONV_HALO:]
            cf = _conv3(ua, ext, cw_ref[:, ca]) + cb_ref[:, ca]
            f = _gelu(cf) * up_ref[:, cg]
            f_ref[:, ca] = f.astype(BF16)
            ft_ref[ca, :] = f.T.astype(BF16)
        dn = _dot(f_ref[...], w_ref[...])
        dn_ref[...] = dn
        xhat, _ = _ln_fwd(ALPHA * x1_ref[...] + gt_ref[...] * dn)
        x2_ref[...] = xhat * g_ref[...] + b_ref[...]

    row = pl.BlockSpec((ts, d), lambda i: (i, 0))
    return _run(
        body, name=name, grid=(s // ts,),
        in_specs=[pl.BlockSpec((ts, 2 * D_FF), lambda i: (i, 0)), row, _const((3, D_FF)), _const((1, D_FF)),
                  _resident((D_FF, d)), _const((1, d)), _const((1, d)), _const((1, d))],
        out_specs=[pl.BlockSpec((D_FF, ts), lambda i: (0, i)), row, row],
        out_shape=[jax.ShapeDtypeStruct((D_FF, s), BF16), jax.ShapeDtypeStruct((s, d), F32),
                   jax.ShapeDtypeStruct((s, d), F32)],
        scratch_shapes=[pltpu.VMEM((CONV_HALO, D_FF), F32), pltpu.VMEM((ts, D_FF), BF16)],
        args=(up, x1, cw, cb, w_down, gt, ln_g, ln_b), comm=comm)


def _loss_fwd(y, tgt, *, ts, name):
    s, d = y.shape
    ts = _tile(ts, s)

    def body(y_ref, t_ref, dy_ref, l_ref):
        @pl.when(pl.program_id(0) == 0)
        def _():
            l_ref[...] = jnp.zeros_like(l_ref)
        e = y_ref[...] - t_ref[...]
        dy_ref[...] = e / float(d)
        l_ref[...] += 0.5 * jnp.sum(jnp.mean(e * e, axis=-1, keepdims=True), axis=0, keepdims=True)

    row = pl.BlockSpec((ts, d), lambda i: (i, 0))
    return _run(body, name=name, grid=(s // ts,), in_specs=[row, row], out_specs=[row, _const((8, 128))],
                out_shape=[jax.ShapeDtypeStruct((s, d), F32), jax.ShapeDtypeStruct((8, 128), F32)], args=(y, tgt))


def _rev(n_tiles):
    return lambda i: n_tiles - 1 - i


def _halo_spec(ts, n_tiles, halo, width, col):
    per = ts // halo
    return pl.BlockSpec((halo, width), lambda i: (jnp.maximum((n_tiles - 1 - i) * per - 1, 0), col))


def _ffn_bwd(dx2, x1, dn, up, cw, cb, w_down, w_up4, gt, ln_g, sc, *, ts, name, comm=None):
    s, d = x1.shape
    ts = _tile(ts, s)
    nt = s // ts
    rev = _rev(nt)
    wd = w_up4.shape[2]

    def w_up_cols(wu_ref, start):
        return wu_ref[start // wd, :, start % wd:start % wd + FF_CHUNK]

    def body(dx2_ref, x1_ref, dn_ref, up_ref, halo_ref, cw_ref, cb_ref, wd_ref, wu_ref, gt_ref, g_ref, sc_ref,
             ddn_ref, dup_ref, dx1_ref, redd_ref, redf_ref, dbup_ref, carry):
        i = pl.program_id(0)

        @pl.when(i == 0)
        def _():
            carry[...] = jnp.zeros_like(carry)
            redd_ref[...] = jnp.zeros_like(redd_ref)
            redf_ref[...] = jnp.zeros_like(redf_ref)
            dbup_ref[...] = jnp.zeros_like(dbup_ref)
        first_tile = i == nt - 1
        x1v, dnv, dyv = x1_ref[...], dn_ref[...], dx2_ref[...]
        xhat, rstd = _ln_fwd(ALPHA * x1v + gt_ref[...] * dnv)
        dr = _ln_bwd(dyv, g_ref[...], xhat, rstd)
        redd_ref[0:1, :] += _colsum(dyv * xhat)
        redd_ref[1:2, :] += _colsum(dyv)
        redd_ref[2:3, :] += _colsum(dr * dnv)
        ddn = (gt_ref[...] * dr).astype(BF16)
        ddn_ref[...] = ddn
        dh = jnp.zeros((ts, d), F32)
        for c in range(D_FF // FF_CHUNK):
            ca = slice(c * FF_CHUNK, (c + 1) * FF_CHUNK)
            cg = slice(D_FF + c * FF_CHUNK, D_FF + (c + 1) * FF_CHUNK)
            df = _dot_nt(ddn, wd_ref[ca, :])
            ua, ug = up_ref[:, ca], up_ref[:, cg]
            halo = jnp.where(first_tile, 0.0, halo_ref[:, ca])
            ext = jnp.concatenate([halo, ua], axis=0)
            u1, u2 = _rows_before(ext, 1, CONV_HALO), _rows_before(ext, 2, CONV_HALO)
            cwc = cw_ref[:, ca]
            gl, dgl = _gelu_and_grad(cwc[2:3] * ua + cwc[1:2] * u1 + cwc[0:1] * u2 + cb_ref[:, ca])
            dug = df * gl
            dcf = df * ug * dgl
            redf_ref[0:1, ca] += _colsum(dcf * u2)
            redf_ref[1:2, ca] += _colsum(dcf * u1)
            redf_ref[2:3, ca] += _colsum(dcf * ua)
            redf_ref[3:4, ca] += _colsum(dcf)
            extd = jnp.concatenate([dcf, carry[:, ca]], axis=0)
            carry[:, ca] = dcf[:CONV_HALO]
            dua = cwc[2:3] * dcf + cwc[1:2] * _rows_after(extd, 1, ts) + cwc[0:1] * _rows_after(extd, 2, ts)
            dbup_ref[0:1, ca] += _colsum(dua)
            dbup_ref[0:1, cg] += _colsum(dug)
            dua_b, dug_b = dua.astype(BF16), dug.astype(BF16)
            dup_ref[:, ca] = dua_b
            dup_ref[:, cg] = dug_b
            dh = dh + _dot_nt(dua_b, w_up_cols(wu_ref, c * FF_CHUNK)) + _dot_nt(dug_b, w_up_cols(wu_ref, D_FF + c * FF_CHUNK))
        dx1_ref[...] = ALPHA * dr + dh * (1.0 + sc_ref[...])
        redd_ref[3:4, :] += _colsum(dh * x1v)
        redd_ref[4:5, :] += _colsum(dh)

    row = pl.BlockSpec((ts, d), lambda i: (rev(i), 0))
    return _run(
        body, name=name, grid=(nt,),
        in_specs=[row, row, row, pl.BlockSpec((ts, 2 * D_FF), lambda i: (rev(i), 0)),
                  _halo_spec(ts, nt, CONV_HALO, D_FF, 0), _const((3, D_FF)), _const((1, D_FF)),
                  _resident((D_FF, d)), _resident((N_CHIPS, d, wd)), _const((1, d)), _const((1, d)), _const((1, d))],
        out_specs=[row, pl.BlockSpec((ts, 2 * D_FF), lambda i: (rev(i), 0)), row,
                   _const((8, d)), _const((8, D_FF)), _const((8, 2 * D_FF))],
        out_shape=[jax.ShapeDtypeStruct((s, d), BF16), jax.ShapeDtypeStruct((s, 2 * D_FF), BF16),
                   jax.ShapeDtypeStruct((s, d), F32), jax.ShapeDtypeStruct((8, d), F32),
                   jax.ShapeDtypeStruct((8, D_FF), F32), jax.ShapeDtypeStruct((8, 2 * D_FF), F32)],
        scratch_shapes=[pltpu.VMEM((CONV_HALO, D_FF), F32)],
        args=(dx2, x1, dn, up, up, cw, cb, w_down, w_up4, gt, ln_g, sc), comm=comm)


def _grad_matmul(xa, dy, *, ts, tn, name, mod=None, by_chip=False, comm=None):
    s, k = xa.shape
    n = dy.shape[1]
    ts = _tile(ts, s)
    nt = s // ts

    def body(*refs):
        if mod is None:
            xa_ref, dy_ref, o_ref, ob_ref = refs
            a = xa_ref[...]
        else:
            xa_ref, sc_ref, sh_ref, dy_ref, o_ref, ob_ref = refs
            a = (xa_ref[...] * (1.0 + sc_ref[...]) + sh_ref[...]).astype(BF16)
        t = pl.program_id(1)

        @pl.when(t == 0)
        def _():
            o_ref[...] = jnp.zeros_like(o_ref)
        o_ref[...] += _dot_tn(a, dy_ref[...]).reshape(o_ref.shape)

        @pl.when(t == nt - 1)
        def _():
            ob_ref[...] = o_ref[...].astype(BF16)

    xspec = pl.BlockSpec((ts, k), lambda j, t: (t, 0))
    dspec = pl.BlockSpec((ts, tn), lambda j, t: (t, j))
    in_specs = [xspec, dspec] if mod is None else [xspec, _const((1, k)), _const((1, k)), dspec]
    args = (xa, dy) if mod is None else (xa, mod[0], mod[1], dy)
    if by_chip:
        per = n // N_CHIPS // tn
        ospec = pl.BlockSpec((1, k, tn), lambda j, t: (j // per, 0, j % per))
        shape = (N_CHIPS, k, n // N_CHIPS)
    else:
        ospec = pl.BlockSpec((k, tn), lambda j, t: (0, j))
        shape = (k, n)
    return _run(body, name=name, grid=(n // tn, nt), in_specs=in_specs, out_specs=[ospec, ospec],
                out_shape=[jax.ShapeDtypeStruct(shape, F32), jax.ShapeDtypeStruct(shape, BF16)], args=args, comm=comm)


def _grad_matmul_t(xt, dy, *, tk, tn, name, by_chip=False, comm=None):
    k, s = xt.shape
    n = dy.shape[1]

    def body(xt_ref, dy_ref, o_ref, ob_ref):
        o = _dot(xt_ref[...], dy_ref[...]).reshape(o_ref.shape)
        o_ref[...] = o
        ob_ref[...] = o.astype(BF16)

    if by_chip:
        assert tk == k
        per = n // N_CHIPS // tn
        ospec = pl.BlockSpec((1, k, tn), lambda j, i: (j // per, 0, j % per))
        shape = (N_CHIPS, k, n // N_CHIPS)
    else:
        ospec = pl.BlockSpec((tk, tn), lambda j, i: (i, j))
        shape = (k, n)
    xspec = _resident((k, s)) if tk == k else pl.BlockSpec((tk, s), lambda j, i: (i, 0))
    dspec = _resident((s, n)) if tn == n else pl.BlockSpec((s, tn), lambda j, i: (0, j))
    return _run(body, name=name, grid=(n // tn, k // tk), in_specs=[xspec, dspec], out_specs=[ospec, ospec],
                out_shape=[jax.ShapeDtypeStruct(shape, F32), jax.ShapeDtypeStruct(shape, BF16)], args=(xt, dy), comm=comm)


def _mix_o_bwd(dx1, x, o, z, ya, yb, yc, w_o, gt, ln_g, *, ts, name, comm=None):
    s, d = x.shape
    ts = _tile(ts, s)

    def body(dx1_ref, x_ref, o_ref, ga_ref, gb_ref, gc_ref, ya_ref, yb_ref, yc_ref, w_ref, gt_ref, g_ref,
             do_ref, dxa_ref, dzg_ref, dya_ref, dyb_ref, dyc_ref, red_ref):
        @pl.when(pl.program_id(0) == 0)
        def _():
            red_ref[...] = jnp.zeros_like(red_ref)
        dyv, ov = dx1_ref[...], o_ref[...]
        xhat, rstd = _ln_fwd(ALPHA * x_ref[...] + gt_ref[...] * ov)
        dr = _ln_bwd(dyv, g_ref[...], xhat, rstd)
        red_ref[0:1, :] += _colsum(dyv * xhat)
        red_ref[1:2, :] += _colsum(dyv)
        red_ref[2:3, :] += _colsum(dr * ov)
        dxa_ref[...] = ALPHA * dr
        d_o = (gt_ref[...] * dr).astype(BF16)
        do_ref[...] = d_o
        dm = _dot_nt(d_o, w_ref[...])
        for k, (zg_ref, y_ref, dy_ref) in enumerate(((ga_ref, ya_ref, dya_ref), (gb_ref, yb_ref, dyb_ref),
                                                     (gc_ref, yc_ref, dyc_ref))):
            g = _sigmoid(zg_ref[...])
            dzg_ref[:, k * d:(k + 1) * d] = (dm * y_ref[...] * g * (1.0 - g)).astype(BF16)
            dy_ref[...] = (dm * g).astype(BF16)

    row = pl.BlockSpec((ts, d), lambda i: (i, 0))
    zspec = lambda k: pl.BlockSpec((ts, d), lambda i, k=k: (i, k))
    bf = jax.ShapeDtypeStruct((s, d), BF16)
    return _run(
        body, name=name, grid=(s // ts,),
        in_specs=[row, row, row, zspec(6), zspec(7), zspec(8), row, row, row, _const((d, d)),
                  _const((1, d)), _const((1, d))],
        out_specs=[row, row, pl.BlockSpec((ts, 3 * d), lambda i: (i, 2)), row, row, row, _const((8, d))],
        out_shape=[bf, jax.ShapeDtypeStruct((s, d), F32), jax.ShapeDtypeStruct((s, D_Z), BF16), bf, bf, bf,
                   jax.ShapeDtypeStruct((8, d), F32)],
        args=(dx1, x, o, z, z, z, ya, yb, yc, w_o, gt, ln_g), comm=comm)


def _mix_a_bwd(dya, z, dz, cw, w_out, *, ts, name, comm=None):
    s = z.shape[0]
    d = D_MODEL
    ts = _tile(ts, s)
    nt = s // ts
    rev = _rev(nt)

    def body(dya_ref, zb_ref, zc_ref, zx_ref, hc_ref, hx_ref, cw_ref, w_ref, dz_in, dz_ref, red_ref, carry):
        i = pl.program_id(0)

        @pl.when(i == 0)
        def _():
            carry[...] = jnp.zeros_like(carry)
            red_ref[...] = jnp.zeros_like(red_ref)
        zb, zc, zx = zb_ref[...], zc_ref[...], zx_ref[...]
        q = zc * zx
        halo = jnp.where(i == nt - 1, 0.0, hc_ref[...] * hx_ref[...])
        ext = jnp.concatenate([halo, q], axis=0)
        q1, q2 = _rows_before(ext, 1, CONV_HALO), _rows_before(ext, 2, CONV_HALO)
        cwv = cw_ref[...]
        cv = cwv[2:3] * q + cwv[1:2] * q1 + cwv[0:1] * q2
        da = _dot_nt(dya_ref[...], w_ref[...])
        dcv = da * zb
        red_ref[0:1, :] += _colsum(dcv * q2)
        red_ref[1:2, :] += _colsum(dcv * q1)
        red_ref[2:3, :] += _colsum(dcv * q)
        extd = jnp.concatenate([dcv, carry[...]], axis=0)
        carry[...] = dcv[:CONV_HALO]
        dq = cwv[2:3] * dcv + cwv[1:2] * _rows_after(extd, 1, ts) + cwv[0:1] * _rows_after(extd, 2, ts)
        dz_ref[:, 0:d] = (da * cv).astype(BF16)
        dz_ref[:, d:2 * d] = (dq * zx).astype(BF16)
        dz_ref[:, 2 * d:3 * d] = (dq * zc).astype(BF16)

    zspec = lambda k: pl.BlockSpec((ts, d), lambda i, k=k: (rev(i), k))
    return _run(
        body, name=name, grid=(nt,),
        in_specs=[pl.BlockSpec((ts, d), lambda i: (rev(i), 0)), zspec(0), zspec(1), zspec(2),
                  _halo_spec(ts, nt, CONV_HALO, d, 1), _halo_spec(ts, nt, CONV_HALO, d, 2),
                  _const((3, d)), _const((d, d)), HBM],
        out_specs=[pl.BlockSpec((ts, 3 * d), lambda i: (rev(i), 0)), _const((8, d))],
        out_shape=[jax.ShapeDtypeStruct((s, D_Z), BF16), jax.ShapeDtypeStruct((8, d), F32)],
        scratch_shapes=[pltpu.VMEM((CONV_HALO, d), F32)],
        args=(dya, z, z, z, z, z, cw, w_out, dz), aliases={8: 0}, comm=comm)


def _mix_b_bwd(dyb, z, dz, ln_g, ln_b, ws, bst, w_out, *, ts, name, comm=None):
    s = z.shape[0]
    d = D_MODEL
    ts = _tile(ts, s)
    nblk = ts // GBLK

    def body(dyb_ref, zu_ref, zv_ref, g_ref, b_ref, ws_ref, bst_ref, w_ref, dz_in,
             dz_ref, red_ref, dws_ref, dbst_ref, mixed_scr, dvn_scr, dzv_scr):
        @pl.when((pl.program_id(0) == 0) & (pl.program_id(1) == 0))
        def _():
            red_ref[...] = jnp.zeros_like(red_ref)
            dws_ref[...] = jnp.zeros_like(dws_ref)
            dbst_ref[...] = jnp.zeros_like(dbst_ref)

        @pl.when(pl.program_id(1) == 0)
        def _():
            u, du_dz = _gelu_and_grad(zu_ref[...])
            vg, dv_dz = _gelu_and_grad(zv_ref[...])
            xhat, rstd = _ln_fwd(vg)
            vn = (xhat * g_ref[...] + b_ref[...]).astype(BF16)
            _spatial_mix(vn, ws_ref, bst_ref, mixed_scr, ts)
            dsg = _dot_nt(dyb_ref[...], w_ref[...])
            dz_ref[...] = (dsg * mixed_scr[...] * du_dz).astype(BF16)
            dmix = dsg * u
            mask = _spatial_mask()
            for g in range(N_GROUPS):
                cols = slice(g * GBLK, (g + 1) * GBLK)
                wm = jnp.where(mask, ws_ref[g], 0.0).astype(BF16)
                dm_cat = jnp.concatenate([dmix[n * GBLK:(n + 1) * GBLK, cols] for n in range(nblk)], axis=1)
                vn_cat = jnp.concatenate([vn[n * GBLK:(n + 1) * GBLK, cols] for n in range(nblk)], axis=1)
                dm_b = dm_cat.astype(BF16)
                dbst_ref[:, g:g + 1] += jnp.sum(dm_cat, axis=1, keepdims=True)
                dws_ref[g] += jnp.where(mask, _dot_nt(dm_b, vn_cat), 0.0)
                dvn_cat = _dot_tn(wm, dm_b)
                for n in range(nblk):
                    dvn_scr[n * GBLK:(n + 1) * GBLK, cols] = dvn_cat[:, n * GBLK:(n + 1) * GBLK]
            dvn = dvn_scr[...]
            red_ref[0:1, :] += _colsum(dvn * xhat)
            red_ref[1:2, :] += _colsum(dvn)
            dzv_scr[...] = (_ln_bwd(dvn, g_ref[...], xhat, rstd) * dv_dz).astype(BF16)

        @pl.when(pl.program_id(1) == 1)
        def _():
            dz_ref[...] = dzv_scr[...]

    zspec = lambda k: pl.BlockSpec((ts, d), lambda i, h, k=k: (i, k))
    return _run(
        body, name=name, grid=(s // ts, 2),
        in_specs=[pl.BlockSpec((ts, d), lambda i, h: (i, 0)), zspec(3), zspec(4), _const((1, d)), _const((1, d)),
                  _const((N_GROUPS, GBLK, GBLK)), _const((GBLK, N_GROUPS)), _const((d, d)), HBM],
        out_specs=[pl.BlockSpec((ts, d), lambda i, h: (i, 3 + h)), _const((8, d)),
                   _const((N_GROUPS, GBLK, GBLK)), _const((GBLK, N_GROUPS))],
        out_shape=[jax.ShapeDtypeStruct((s, D_Z), BF16), jax.ShapeDtypeStruct((8, d), F32),
                   jax.ShapeDtypeStruct((N_GROUPS, GBLK, GBLK), F32), jax.ShapeDtypeStruct((GBLK, N_GROUPS), F32)],
        scratch_shapes=[pltpu.VMEM((ts, d), F32), pltpu.VMEM((ts, d), F32), pltpu.VMEM((ts, d), BF16)],
        args=(dyb, z, z, ln_g, ln_b, ws, bst, w_out, dz), aliases={8: 0}, comm=comm)


def _mix_c_bwd(dyc, z, dz, w_pool, scale, *, ts, name):
    s = z.shape[0]
    d = D_MODEL
    ts = _tile(ts, s)
    nt = s // ts
    rev = _rev(nt)

    def body(dyc_ref, zp_ref, halo_ref, w_ref, sc_ref, dz_in, dz_ref, red_ref, dw_ref, carry):
        i = pl.program_id(0)

        @pl.when(i == 0)
        def _():
            carry[...] = jnp.zeros_like(carry)
            red_ref[...] = jnp.zeros_like(red_ref)
            dw_ref[...] = jnp.zeros_like(dw_ref)
        p = zp_ref[...]
        ext = jnp.concatenate([jnp.where(i == nt - 1, 0.0, halo_ref[...]), p], axis=0)
        denoms = _pool_denoms(rev(i), ts)
        dyv = dyc_ref[...].astype(F32)
        for k in range(len(POOL_WINDOWS)):
            cols = slice(k * POOL_GROUP, (k + 1) * POOL_GROUP)
            dk = _pool_diff(p, ext, denoms, k).astype(BF16)
            red_ref[0:1, cols] += _colsum(dyv[:, cols] * _dot(dk, w_ref[k]))
            dpre = (dyv[:, cols] * sc_ref[:, cols]).astype(BF16)
            dw_ref[k] += _dot_tn(dk, dpre)
            dd = _dot_nt(dpre, w_ref[k])
            e = dd / denoms[k]
            acc = jnp.concatenate([e, carry[:, cols]], axis=0)
            carry[:, cols] = e[:POOL_HALO]
            step = 1
            while step < POOL_WINDOWS[k]:
                acc = acc + pltpu.roll(acc, acc.shape[0] - step, 0)
                step *= 2
            dz_ref[:, cols] = (acc[:ts] - dd).astype(BF16)

    return _run(
        body, name=name, grid=(nt,),
        in_specs=[pl.BlockSpec((ts, d), lambda i: (rev(i), 0)), pl.BlockSpec((ts, d), lambda i: (rev(i), 5)),
                  _halo_spec(ts, nt, POOL_HALO, d, 5), _const((4, POOL_GROUP, POOL_GROUP)), _const((1, d)), HBM],
        out_specs=[pl.BlockSpec((ts, d), lambda i: (rev(i), 5)), _const((8, d)), _const((4, POOL_GROUP, POOL_GROUP))],
        out_shape=[jax.ShapeDtypeStruct((s, D_Z), BF16), jax.ShapeDtypeStruct((8, d), F32),
                   jax.ShapeDtypeStruct((4, POOL_GROUP, POOL_GROUP), F32)],
        scratch_shapes=[pltpu.VMEM((POOL_HALO, d), F32)],
        args=(dyc, z, z, w_pool, scale, dz), aliases={5: 0})


def _in_proj_bwd(dz, w4, dxa, x, sc, *, ts, name, comm=None):
    s, d = x.shape
    ts = _tile(ts, s)
    wd = w4.shape[2]

    def body(dz_ref, w_ref, dxa_ref, x_ref, sc_ref, dx_ref, red_ref, db_ref):
        @pl.when(pl.program_id(0) == 0)
        def _():
            red_ref[...] = jnp.zeros_like(red_ref)
            db_ref[...] = jnp.zeros_like(db_ref)
        dh = jnp.zeros((ts, d), F32)
        for j in range(N_CHIPS):
            dzj = dz_ref[:, j * wd:(j + 1) * wd]
            db_ref[0:1, j * wd:(j + 1) * wd] += _colsum(dzj.astype(F32))
            dh = dh + _dot_nt(dzj, w_ref[j])
        dx_ref[...] = dxa_ref[...] + dh * (1.0 + sc_ref[...])
        red_ref[0:1, :] += _colsum(dh * x_ref[...])
        red_ref[1:2, :] += _colsum(dh)

    row = pl.BlockSpec((ts, d), lambda i: (i, 0))
    return _run(
        body, name=name, grid=(s // ts,),
        in_specs=[pl.BlockSpec((ts, D_Z), lambda i: (i, 0)), _resident((N_CHIPS, d, wd)), row, row, _const((1, d))],
        out_specs=[row, _const((8, d)), _const((8, D_Z))],
        out_shape=[jax.ShapeDtypeStruct((s, d), F32), jax.ShapeDtypeStruct((8, d), F32),
                   jax.ShapeDtypeStruct((8, D_Z), F32)],
        args=(dz, w4, dxa, x, sc), comm=comm)


def _ada_fwd(c_all, w_ada, b_ada, *, name):
    nl, d, n = w_ada.shape
    tn = n // 2

    def body(c_ref, w_ref, b_ref, o_ref):
        cv = c_ref[...]
        ca = (cv * _sigmoid(cv)).astype(BF16)
        o_ref[0] = _dot(ca, w_ref[0].astype(BF16)) + b_ref[0]

    return _run(
        body, name=name, grid=(nl, n // tn),
        in_specs=[_const((N_DEV, d)), pl.BlockSpec((1, d, tn), lambda l, j: (l, 0, j)),
                  pl.BlockSpec((1, 1, tn), lambda l, j: (l, 0, j))],
        out_specs=[pl.BlockSpec((1, N_DEV, tn), lambda l, j: (l, 0, j))],
        out_shape=[jax.ShapeDtypeStruct((nl, N_DEV, n), F32)], args=(c_all, w_ada, b_ada))[0]


def _ada_bwd(c_all, dada, *, name):
    nl, nb, n = dada.shape
    d = c_all.shape[1]
    tn = n // 2

    def body(c_ref, g_ref, o_ref):
        cv = c_ref[...]
        ca = (cv * _sigmoid(cv)).astype(BF16)
        o_ref[0] = _dot_tn(ca, g_ref[0].astype(BF16))

    return _run(
        body, name=name, grid=(nl, n // tn),
        in_specs=[_const((nb, d)), pl.BlockSpec((1, nb, tn), lambda l, j: (l, 0, j))],
        out_specs=[pl.BlockSpec((1, d, tn), lambda l, j: (l, 0, j))],
        out_shape=[jax.ShapeDtypeStruct((nl, d, n), F32)], args=(c_all, dada))[0]


def _sum4_into_half(own, recv, core, *, name):
    r, c = own.shape
    tr = _row_tile(r, c, 2)

    def body(core_ref, own_ref, recv_ref, o_ref):
        acc = own_ref[...]
        for k in range(N_CHIPS - 1):
            acc = acc + recv_ref[k].astype(F32)
        o_ref[0] = acc

    spec = pltpu.PrefetchScalarGridSpec(
        num_scalar_prefetch=1, grid=(r // tr,),
        in_specs=[pl.BlockSpec((tr, c), lambda i, core_ref: (i, 0)),
                  pl.BlockSpec((N_CHIPS - 1, tr, c), lambda i, core_ref: (0, i, 0))],
        out_specs=pl.BlockSpec((1, tr, c), lambda i, core_ref: (core_ref[0], i, 0)))
    return pl.pallas_call(
        body, name=name, grid_spec=spec, out_shape=jax.ShapeDtypeStruct((2, r, c), F32),
        compiler_params=pltpu.CompilerParams(dimension_semantics=("arbitrary",), vmem_limit_bytes=VMEM_LIMIT),
    )(core, own, recv)


def _cast_into_slots(shards, layer, chip, *, name):
    quarters = 4

    def body(chip_ref, *refs):
        ins, outs = refs[:len(shards)], refs[len(shards):]
        for i_ref, o_ref in zip(ins, outs):
            o_ref[0, 0] = i_ref[0].astype(BF16)

    in_specs, out_specs, out_shape = [], [], []
    for sh in shards:
        _, r, c = sh.shape
        in_specs.append(pl.BlockSpec((1, r // quarters, c), lambda t, chip_ref: (layer, t, 0)))
        out_specs.append(pl.BlockSpec((1, 1, r // quarters, c), lambda t, chip_ref: (chip_ref[0], t // 2, t % 2, 0)))
        out_shape.append(jax.ShapeDtypeStruct((N_CHIPS, 2, r // 2, c), BF16))
    spec = pltpu.PrefetchScalarGridSpec(num_scalar_prefetch=1, grid=(quarters,), in_specs=in_specs, out_specs=out_specs)
    return pl.pallas_call(
        body, name=name, grid_spec=spec, out_shape=out_shape,
        compiler_params=pltpu.CompilerParams(dimension_semantics=("arbitrary",), vmem_limit_bytes=VMEM_LIMIT),
    )(chip, *shards)


def _sum_halves(g_f32, theirs, *, name):
    _, _, rh, c = g_f32.shape
    tr = _row_tile(rh, c, 1)

    def body(g_ref, t_ref, hb_ref, own_ref):
        mx, my, mc = _my_coords()
        j = pl.program_id(1)
        h = jnp.where(mc == 0, g_ref[0, 0], g_ref[0, 1]) + t_ref[0].astype(F32)
        hb_ref[0] = h.astype(BF16)

        @pl.when(j == 2 * mx + my)
        def _():
            own_ref[...] = h

    return _run(
        body, name=name, grid=(rh // tr, N_CHIPS),
        in_specs=[pl.BlockSpec((1, 2, tr, c), lambda i, j: (j, 0, i, 0)), pl.BlockSpec((1, tr, c), lambda i, j: (j, i, 0))],
        out_specs=[pl.BlockSpec((1, tr, c), lambda i, j: (j, i, 0)), pl.BlockSpec((tr, c), lambda i, j: (i, 0))],
        out_shape=[jax.ShapeDtypeStruct((N_CHIPS, rh, c), BF16), jax.ShapeDtypeStruct((rh, c), F32)],
        args=(g_f32, theirs))


def _row_tile(r, c, mib):
    limit = max(8, (mib << 20) // (4 * c))
    if r <= limit:
        return r
    best = 8
    for t in range(8, limit + 1, 8):
        if r % t == 0:
            best = t
    return best


def _adam_math(w, g, m, v):
    mn = ADAM_B1 * m + (1.0 - ADAM_B1) * g
    vn = ADAM_B2 * v + (1.0 - ADAM_B2) * (g * g)
    m_hat = mn / (1.0 - ADAM_B1 ** ADAM_STEP)
    v_hat = vn / (1.0 - ADAM_B2 ** ADAM_STEP)
    return -ADAM_LR * (m_hat / (jnp.sqrt(v_hat) + ADAM_EPS) + ADAM_WD * w), mn, vn


def _adamw(w, g, m, v, *, name):
    r, c = w.shape
    tr = _row_tile(r, c, 2)

    def body(w_ref, g_ref, m_ref, v_ref, d_ref, mo_ref, vo_ref):
        d_ref[...], mo_ref[...], vo_ref[...] = _adam_math(w_ref[...], g_ref[...], m_ref[...], v_ref[...])

    blk = pl.BlockSpec((tr, c), lambda i: (i, 0))
    return _run(body, name=name, grid=(r // tr,), in_specs=[blk] * 4, out_specs=[blk] * 3,
                out_shape=[jax.ShapeDtypeStruct((r, c), F32)] * 3, args=(w, g, m, v))


def _adamw_sharded(w, m, v, grads, *, name, comm=None):
    nl, r, c = w.shape
    tr = _row_tile(r, c, 1)
    nt = r // tr

    def body(w_ref, m_ref, v_ref, g0_ref, g1_ref, g_ref, d_ref, mo_ref, vo_ref):
        g = jnp.where(pl.program_id(0) == 0, g0_ref[...], g1_ref[...])
        g_ref[0] = g
        d_ref[0], mo_ref[0], vo_ref[0] = _adam_math(w_ref[0], g, m_ref[0], v_ref[0])

    blk = pl.BlockSpec((1, tr, c), lambda l, i: (l, i, 0))
    part0 = pl.BlockSpec((tr, c), lambda l, i: (jnp.where(l == 0, i, nt - 1), 0))
    part1 = pl.BlockSpec((tr, c), lambda l, i: (jnp.where(l == 1, i, 0), 0))
    return _run(body, name=name, grid=(nl, nt), in_specs=[blk] * 3 + [part0, part1],
                out_specs=[blk] * 4, out_shape=[jax.ShapeDtypeStruct((nl, r, c), F32)] * 4,
                args=(w, m, v, grads[0], grads[1]), comm=comm)


_BIG = ("w_in", "w_a_out", "w_b_out", "w_pool", "w_o", "w_up", "w_down")
_COL_SHARDED = ("w_in", "w_up")
_SMALL_SHARDED = ("conv_a", "conv_ffn")
_SMALL_REPL = ("b_in", "ln_v_g", "ln_v_b", "w_spatial", "b_spatial", "pool_scale", "ln1_g", "ln1_b", "b_up",
               "conv_ffn_b", "ln2_g", "ln2_b")
_WEIGHTS = ("w_ada", "b_ada", "w_in", "b_in", "conv_a", "w_a_out", "ln_v_g", "ln_v_b", "w_spatial", "b_spatial",
            "w_b_out", "w_pool", "pool_scale", "w_o", "ln1_g", "ln1_b", "w_up", "b_up", "conv_ffn", "conv_ffn_b",
            "w_down", "ln2_g", "ln2_b")


def _shard3(a):
    return a.reshape(a.shape[0], -1, a.shape[-1])


def _use_gathered(name, g):
    g = g.reshape(N_CHIPS, -1, g.shape[-1])
    if name in _COL_SHARDED:
        return g
    if name == "w_pool":
        return g.reshape(N_CHIPS, 4, POOL_GROUP // N_CHIPS, POOL_GROUP).transpose(1, 0, 2, 3).reshape(
            4, POOL_GROUP, POOL_GROUP)
    return g.reshape(-1, g.shape[-1])


def _grad_by_chip(name, g):
    if name in _COL_SHARDED:
        return g
    if name == "w_pool":
        return g.reshape(4, N_CHIPS, POOL_GROUP // N_CHIPS, POOL_GROUP).transpose(1, 0, 2, 3).reshape(
            N_CHIPS, POOL_GROUP, POOL_GROUP)
    return g.reshape(N_CHIPS, -1, g.shape[-1])


def _pack_small(arrs):
    parts = []
    for a in arrs:
        flat = a.reshape(-1).astype(F32)
        pad = (-flat.shape[0]) % 128
        parts.append(jnp.pad(flat, (0, pad)) if pad else flat)
    flat = jnp.concatenate(parts)
    pad = (-flat.shape[0]) % 1024
    if pad:
        flat = jnp.pad(flat, (0, pad))
    return flat.reshape(-1, 128)


def _unpack_small(buf, shapes):
    lead = buf.shape[:-2]
    flat = buf.reshape(lead + (-1,))
    out, off = [], 0
    for shp in shapes:
        n = math.prod(shp)
        out.append(flat[..., off:off + n].reshape(lead + tuple(shp)))
        off += n + ((-n) % 128)
    return out


def _as2d(a):
    return a.reshape(-1, a.shape[-1])


_LATE = ("w_a_out", "w_b_out", "w_pool", "w_o")


class _Traffic:
    def __init__(self, slots, plan, core):
        self.slots = slots
        self.plan = plan
        self.core = core
        self.gathered = {}
        self.ready = {}
        self.summed = {}
        self.half = {}
        self.final = {}

    def weight(self, layer, name):
        return self.gathered[(layer, name)]

    def add_grad(self, layer, name, g_f32, g_bf16):
        def halves(g):
            g = _grad_by_chip(name, g)
            return g.reshape(N_CHIPS, 2, g.shape[1] // 2, g.shape[2])
        self.ready[(layer, name)] = (halves(g_f32), halves(g_bf16))

    def _comm(self, job):
        if job[0] == "gather":
            return _gather_comm([self.slots[(job[1], k)] for k in job[2]])
        if job[0] == "presum":
            return _presum_comm([self.ready[k][1] for k in job[1]])
        if job[0] == "scatter":
            return _scatter_comm([self.summed[k][0] for k in job[1]])
        return _join_comm([self.half[k] for k in job[1]])

    def _done(self, job, res):
        if job[0] == "gather":
            for k, r in zip(job[2], res):
                self.gathered[(job[1], k)] = _use_gathered(k, r)
        elif job[0] == "presum":
            for k, r in zip(job[1], res):
                self.summed[k] = _sum_halves(self.ready.pop(k)[0], r, name=f"presum_l{k[0]}_{k[1]}")
        elif job[0] == "scatter":
            for k, r in zip(job[1], res):
                self.half[k] = _sum4_into_half(self.summed.pop(k)[1], r, self.core, name=f"sum_l{k[0]}_{k[1]}")
        else:
            for k, r in zip(job[1], res):
                self.final[k] = r.reshape(-1, r.shape[-1])

    def run(self, name, fn):
        jobs = self.plan.get(name)
        if not jobs:
            return fn(None)
        comms = [self._comm(j) for j in jobs]
        outs, res = fn(_merge(comms))
        for job, r in zip(jobs, _split(comms, res)):
            self._done(job, r)
        return outs

    def alone(self, name):
        jobs = self.plan[name]
        comms = [self._comm(j) for j in jobs]
        for job, r in zip(jobs, _split(comms, _comm_call(_merge(comms), name=name))):
            self._done(job, r)


def _layer_fwd(x, ada, p, l, tr):
    sh1, sc1, gt1, sh2, sc2, gt2 = ada
    n = f"l{l}"
    z, ht = tr.run(f"{n}_in_proj", lambda cm: _mod_matmul(
        x, sc1, sh1, tr.weight(l, "w_in"), p["b_in"], ts=1024, tn=2304, name=f"{n}_in_proj", comm=cm))
    a, ya = _mix_a_fwd(z, p["conv_a"], tr.weight(l, "w_a_out"), ts=256, name=f"{n}_mix_a")
    sg, yb = tr.run(f"{n}_mix_b", lambda cm: _mix_b_fwd(
        z, p["ln_v_g"], p["ln_v_b"], p["w_spatial"], p["b_spatial_t"], tr.weight(l, "w_b_out"), ts=256,
        name=f"{n}_mix_b", comm=cm))
    dpool, yc = _mix_c_fwd(z, tr.weight(l, "w_pool"), p["pool_scale"], ts=256, name=f"{n}_mix_c")
    merged, o, x1 = tr.run(f"{n}_mix_o", lambda cm: _mix_o_fwd(
        x, z, ya, yb, yc, tr.weight(l, "w_o"), gt1, p["ln1_g"], p["ln1_b"], ts=256, name=f"{n}_mix_o", comm=cm))
    up, h2t = tr.run(f"{n}_up_proj", lambda cm: _mod_matmul(
        x1, sc2, sh2, tr.weight(l, "w_up"), p["b_up"], ts=1024, tn=1408, name=f"{n}_up_proj", comm=cm))
    ft, dn, x2 = tr.run(f"{n}_ffn", lambda cm: _ffn_fwd(
        up, x1, p["conv_ffn"], p["conv_ffn_b"], tr.weight(l, "w_down"), gt2, p["ln2_g"], p["ln2_b"], ts=256,
        name=f"{n}_ffn", comm=cm))
    saved = dict(x=x, z=z, ht=ht, a=a, ya=ya, sg=sg, yb=yb, dpool=dpool, yc=yc, merged=merged, o=o, x1=x1, h2t=h2t,
                 up=up, ft=ft, dn=dn)
    return x2, saved


def _layer_bwd(dx2, ada, p, sv, l, tr):
    sh1, sc1, gt1, sh2, sc2, gt2 = ada
    n = f"l{l}"
    ddn, dup, dx1, red_d, red_f, dbup = tr.run(f"{n}_ffn_bwd", lambda cm: _ffn_bwd(
        dx2, sv["x1"], sv["dn"], sv["up"], p["conv_ffn"], p["conv_ffn_b"], tr.weight(l, "w_down"),
        tr.weight(l, "w_up"), gt2, p["ln2_g"], sc2, ts=256, name=f"{n}_ffn_bwd", comm=cm))
    g = {}
    tr.add_grad(l, "w_down", *_grad_matmul_t(sv["ft"], ddn, tk=D_FF // N_CHIPS, tn=D_MODEL, name=f"{n}_dw_down"))
    tr.add_grad(l, "w_up", *tr.run(f"{n}_dw_up", lambda cm: _grad_matmul_t(
        sv["h2t"], dup, tk=D_MODEL, tn=FF_CHUNK, name=f"{n}_dw_up", by_chip=True, comm=cm)))
    g["ln2_g"], g["ln2_b"] = red_d[0], red_d[1]
    g["conv_ffn"], g["conv_ffn_b"], g["b_up"] = red_f[0:3], red_f[3], dbup[0]

    d_o, dxa, dz, dya, dyb, dyc, red_o = tr.run(f"{n}_mix_o_bwd", lambda cm: _mix_o_bwd(
        dx1, sv["x"], sv["o"], sv["z"], sv["ya"], sv["yb"], sv["yc"], tr.weight(l, "w_o"), gt1, p["ln1_g"], ts=256,
        name=f"{n}_mix_o_bwd", comm=cm))
    tr.add_grad(l, "w_o", *_grad_matmul(sv["merged"], d_o, ts=512, tn=1024, name=f"{n}_dw_o"))
    g["ln1_g"], g["ln1_b"] = red_o[0], red_o[1]

    dz, red_a = tr.run(f"{n}_mix_a_bwd", lambda cm: _mix_a_bwd(
        dya, sv["z"], dz, p["conv_a"], tr.weight(l, "w_a_out"), ts=256, name=f"{n}_mix_a_bwd", comm=cm))
    tr.add_grad(l, "w_a_out", *_grad_matmul(sv["a"], dya, ts=512, tn=1024, name=f"{n}_dw_a_out"))
    g["conv_a"] = red_a[0:3]

    dz, red_b, dws, dbst = tr.run(f"{n}_mix_b_bwd", lambda cm: _mix_b_bwd(
        dyb, sv["z"], dz, p["ln_v_g"], p["ln_v_b"], p["w_spatial"], p["b_spatial_t"], tr.weight(l, "w_b_out"), ts=256,
        name=f"{n}_mix_b_bwd", comm=cm))
    tr.add_grad(l, "w_b_out", *_grad_matmul(sv["sg"], dyb, ts=512, tn=1024, name=f"{n}_dw_b_out"))
    g["ln_v_g"], g["ln_v_b"], g["w_spatial"], g["b_spatial"] = red_b[0], red_b[1], dws, dbst.T

    dz, red_c, dwp = _mix_c_bwd(dyc, sv["z"], dz, tr.weight(l, "w_pool"), p["pool_scale"], ts=256,
                                name=f"{n}_mix_c_bwd")
    g["pool_scale"] = red_c[0]
    tr.add_grad(l, "w_pool", dwp, dwp.astype(BF16))

    tr.add_grad(l, "w_in", *tr.run(f"{n}_dw_in", lambda cm: _grad_matmul_t(
        sv["ht"], dz, tk=D_MODEL, tn=1152, name=f"{n}_dw_in", by_chip=True, comm=cm)))
    if f"{n}_presum_tail" in tr.plan:
        tr.alone(f"{n}_presum_tail")
    dx, red_i, dbin = tr.run(f"{n}_in_proj_bwd", lambda cm: _in_proj_bwd(
        dz, tr.weight(l, "w_in"), dxa, sv["x"], sc1, ts=256, name=f"{n}_in_proj_bwd", comm=cm))
    g["b_in"] = dbin[0]
    dada = jnp.stack([red_i[1], red_i[0], red_o[2], red_d[4], red_d[3], red_d[2]])
    return dx, g, dada


def _traffic_plan():
    plan = {
        "gather_l0": [("gather", 0, ("w_in",) + _LATE)],
        "l0_in_proj": [("gather", 1, ("w_in",))],
        "l0_mix_b": [("gather", 0, ("w_down",))],
        "l0_mix_o": [("gather", 0, ("w_up",))],
        "l0_up_proj": [("gather", 1, _LATE)],
        "l0_ffn": [("gather", 1, ("w_down",))],
        "l1_in_proj": [("gather", 1, ("w_up",))],
    }
    for l in reversed(range(DEPTH)):
        late = [(l, k) for k in _LATE]
        plan.update({
            f"l{l}_dw_up": [("presum", [(l, "w_down")])],
            f"l{l}_mix_o_bwd": [("presum", [(l, "w_up")]), ("scatter", [(l, "w_down")])],
            f"l{l}_mix_b_bwd": [("scatter", [(l, "w_up")])],
            f"l{l}_dw_in": [("presum", late), ("join", [(l, "w_down"), (l, "w_up")])],
        })
    late0, late1 = [(0, k) for k in _LATE], [(1, k) for k in _LATE]
    plan["l1_in_proj_bwd"] = [("presum", [(1, "w_in")]), ("scatter", late1)]
    plan["l0_ffn_bwd"] = [("scatter", [(1, "w_in")]), ("join", late1)]
    plan["l0_dw_up"] = plan["l0_dw_up"] + [("join", [(1, "w_in")])]
    plan["l0_presum_tail"] = [("presum", [(0, "w_in")])]
    plan["l0_in_proj_bwd"] = [("scatter", [(0, "w_in")] + late0)]
    plan["join_tail"] = [("join", [(0, "w_in")] + late0)]
    return plan


def kernel(x, c, w_ada, b_ada, w_in, b_in, conv_a, w_a_out, ln_v_g, ln_v_b, w_spatial, b_spatial, w_b_out, w_pool, pool_scale, w_o, ln1_g, ln1_b, w_up, b_up, conv_ffn, conv_ffn_b, w_down, ln2_g, ln2_b, loss_target, m_w_ada, m_b_ada, m_w_in, m_b_in, m_conv_a, m_w_a_out, m_ln_v_g, m_ln_v_b, m_w_spatial, m_b_spatial, m_w_b_out, m_w_pool, m_pool_scale, m_w_o, m_ln1_g, m_ln1_b, m_w_up, m_b_up, m_conv_ffn, m_conv_ffn_b, m_w_down, m_ln2_g, m_ln2_b, v_w_ada, v_b_ada, v_w_in, v_b_in, v_conv_a, v_w_a_out, v_ln_v_g, v_ln_v_b, v_w_spatial, v_b_spatial, v_w_b_out, v_w_pool, v_pool_scale, v_w_o, v_ln1_g, v_ln1_b, v_w_up, v_b_up, v_conv_ffn, v_conv_ffn_b, v_w_down, v_ln2_g, v_ln2_b):
    args = locals()
    w = {k: args[k] for k in _WEIGHTS}
    m = {k: args["m_" + k] for k in _WEIGHTS}
    v = {k: args["v_" + k] for k in _WEIGHTS}
    d = D_MODEL
    mx, my, mc = _my_coords()
    chip = 2 * mx + my
    me = 4 * mx + 2 * my + mc

    small_shapes = [c.shape, conv_a.shape, conv_ffn.shape]
    small_all = _all_gather8(_pack_small([c, conv_a, conv_ffn]), name="gather_small")
    c_all, conv_a_st, conv_ffn_st = _unpack_small(small_all, small_shapes)
    c_all = c_all.reshape(N_DEV, d)
    conv_full = {"conv_a": jnp.concatenate([conv_a_st[2 * j] for j in range(N_CHIPS)], axis=-1),
                 "conv_ffn": jnp.concatenate([conv_ffn_st[2 * j] for j in range(N_CHIPS)], axis=-1)}

    chip_idx = jnp.reshape(chip, (1,)).astype(jnp.int32)
    slots = {}
    for l in range(DEPTH):
        bufs = _cast_into_slots([_shard3(w[k]) for k in _BIG], l, chip_idx, name=f"cast_l{l}")
        slots.update({(l, k): b for k, b in zip(_BIG, bufs)})
    tr = _Traffic(slots, _traffic_plan(), jnp.reshape(mc, (1,)).astype(jnp.int32))
    tr.alone("gather_l0")

    n_ada = w_ada.shape[2]
    b_ada_mine = lax.dynamic_slice_in_dim(b_ada, chip * n_ada, n_ada, axis=1)
    ada_part = _ada_fwd(c_all, w_ada, b_ada_mine.reshape(DEPTH, 1, n_ada), name="ada_fwd")
    ada_all = _all_gather8(_pack_small([ada_part]), name="gather_ada")
    ada_st = _unpack_small(ada_all, [ada_part.shape])[0][0::2]
    ada_rows = jnp.concatenate([ada_st[j] for j in range(N_CHIPS)], axis=-1)
    ada_mine = lax.dynamic_index_in_dim(ada_rows, me, axis=1, keepdims=False)

    def layer_params(l):
        p = {k: conv_full[k][l] for k in _SMALL_SHARDED}
        for k in ("b_in", "ln_v_g", "ln_v_b", "pool_scale", "ln1_g", "ln1_b", "b_up", "conv_ffn_b", "ln2_g", "ln2_b"):
            p[k] = w[k][l].reshape(1, -1)
        p["w_spatial"] = w_spatial[l]
        p["b_spatial_t"] = b_spatial[l].T
        return p

    xs = x[0]
    saved, adas, params = [], [], []
    for l in range(DEPTH):
        ada = [ada_mine[l, k * d:(k + 1) * d].reshape(1, d) for k in range(6)]
        p = layer_params(l)
        xs, sv = _layer_fwd(xs, ada, p, l, tr)
        saved.append(sv), adas.append(ada), params.append(p)
    dx, loss_blk = _loss_fwd(xs, loss_target[0], ts=512, name="loss")

    grads, dadas = [None] * DEPTH, [None] * DEPTH
    for l in reversed(range(DEPTH)):
        dx, grads[l], dadas[l] = _layer_bwd(dx, adas[l], params[l], saved[l], l, tr)
    tr.alone("join_tail")
    dada = jnp.stack(dadas).reshape(DEPTH, 6 * d)

    small_names = _SMALL_REPL + _SMALL_SHARDED
    small_g = [jnp.stack([grads[l][k] for l in range(DEPTH)]) for k in small_names]
    gsum = dict(zip(small_names, _unpack_small(_all_reduce_small(_pack_small(small_g), name="reduce_small"),
                                               [a.shape for a in small_g])))
    tail_g = [dada, loss_blk[0:1, 0:1]]
    tail_all, tail_sum = _all_gather8(_pack_small(tail_g), name="gather_dada", with_sum=True)
    gsum["b_ada"], loss_sum = _unpack_small(tail_sum, [a.shape for a in tail_g])
    loss = loss_sum[0, 0]
    dada_all = _unpack_small(tail_all, [a.shape for a in tail_g])[0]
    for k in _SMALL_SHARDED:
        wd = gsum[k].shape[-1] // N_CHIPS
        gsum[k] = lax.dynamic_slice_in_dim(gsum[k], chip * wd, wd, axis=gsum[k].ndim - 1)

    dada_cols = lax.dynamic_slice_in_dim(dada_all, chip * n_ada, n_ada, axis=2)
    dada_cols = jnp.pad(jnp.swapaxes(dada_cols, 0, 1), ((0, 0), (0, N_DEV), (0, 0)))
    gsum["w_ada"] = _ada_bwd(jnp.pad(c_all, ((0, N_DEV), (0, 0))), dada_cols, name="ada_bwd")

    out_g, out_d, out_m, out_v = {}, {}, {}, {}
    for k in _WEIGHTS:
        shp = w[k].shape
        if k in _BIG:
            res = tr.run(f"adamw_{k}", lambda cm: _adamw_sharded(
                _shard3(w[k]), _shard3(m[k]), _shard3(v[k]), [tr.final[(l, k)] for l in range(DEPTH)],
                name=f"adamw_{k}", comm=cm))
        else:
            gk = gsum[k].reshape(shp)
            res = [gk] + list(_adamw(_as2d(w[k]), _as2d(gk), _as2d(m[k]), _as2d(v[k]), name=f"adamw_{k}"))
        out_g[k], out_d[k], out_m[k], out_v[k] = [r.reshape(shp) for r in res]

    return (loss, dx[None], *[out_g[k] for k in _WEIGHTS], *[out_d[k] for k in _WEIGHTS],
            *[out_m[k] for k in _WEIGHTS], *[out_v[k] for k in _WEIGHTS])
```

```python
import math
from typing import Callable, NamedTuple

import jax
import jax.numpy as jnp
from jax import lax
from jax.experimental import pallas as pl
from jax.experimental.pallas import tpu as pltpu

F32 = jnp.float32
BF16 = jnp.bfloat16

D_MODEL = 1024
D_Z = 9216
D_FF = 2816
N_GROUPS = 8
GBLK = 128
CHUNK = 64
POOL_WINDOWS = (2, 4, 8, 16)
POOL_GROUP = 256
POOL_HALO = 16
CONV_HALO = 8
HALO_ROWS = 16
DEPTH = 2
ALPHA = (2 * DEPTH) ** 0.25
LN_EPS = 1e-5
ADAM_LR, ADAM_B1, ADAM_B2, ADAM_EPS, ADAM_WD, ADAM_STEP = 0.001, 0.9, 0.999, 1e-08, 0.01, 10
N_CHIPS = 4
N_DEV = 8
FF_CHUNK = 1408
MESH = pl.DeviceIdType.MESH
VMEM_LIMIT = 56 * 1024 * 1024
HBM = pl.BlockSpec(memory_space=pl.ANY)


def _dot(a, b):
    return jnp.dot(a, b, preferred_element_type=F32)


def _dot_nt(a, b):
    return lax.dot_general(a, b, (((1,), (1,)), ((), ())), preferred_element_type=F32)


def _dot_tn(a, b):
    return lax.dot_general(a, b, (((0,), (0,)), ((), ())), preferred_element_type=F32)


_GELU_C = math.sqrt(2.0 / math.pi)


def _gelu_and_grad(x):
    x2 = x * x
    t = jnp.tanh(_GELU_C * (x + 0.044715 * x * x2))
    g = 0.5 * x * (1.0 + t)
    dg = 0.5 * (1.0 + t) + 0.5 * x * (1.0 - t * t) * (_GELU_C * (1.0 + 3 * 0.044715 * x2))
    return g, dg


def _gelu(x):
    return 0.5 * x * (1.0 + jnp.tanh(_GELU_C * (x + 0.044715 * x * x * x)))


def _sigmoid(x):
    return 1.0 / (1.0 + jnp.exp(-x))


def _ln_fwd(r):
    mu = jnp.mean(r, axis=-1, keepdims=True)
    xc = r - mu
    var = jnp.mean(xc * xc, axis=-1, keepdims=True)
    rstd = lax.rsqrt(var + LN_EPS)
    return xc * rstd, rstd


def _ln_bwd(dy, g, xhat, rstd):
    dxh = dy * g
    m1 = jnp.mean(dxh, axis=-1, keepdims=True)
    m2 = jnp.mean(dxh * xhat, axis=-1, keepdims=True)
    return rstd * (dxh - m1 - xhat * m2)


def _rows_before(ext, k, halo):
    return pltpu.roll(ext, k, 0)[halo:]


def _rows_after(ext, k, n):
    return pltpu.roll(ext, ext.shape[0] - k, 0)[:n]


def _colsum(v):
    return jnp.sum(v, axis=0, keepdims=True)


def _spatial_mask():
    i = lax.broadcasted_iota(jnp.int32, (GBLK, GBLK), 0)
    j = lax.broadcasted_iota(jnp.int32, (GBLK, GBLK), 1)
    return (j // CHUNK) <= (i // CHUNK)


def _const(shape):
    n = len(shape)
    return pl.BlockSpec(shape, lambda *_: (0,) * n)


def _resident(shape):
    n = len(shape)
    return pl.BlockSpec(shape, lambda *_: (0,) * n, pipeline_mode=pl.Buffered(1))


def _tile(ts, s):
    return min(ts, s)


class _Comm(NamedTuple):
    srcs: tuple
    dsts: tuple
    n_remote: int
    n_local: int
    build: Callable
    alias: tuple = ()


def _my_coords():
    return lax.axis_index("x"), lax.axis_index("y"), lax.axis_index("c")


def _chip_peer(k):
    mx, my, mc = _my_coords()
    return (mx ^ ((k >> 1) & 1), my ^ (k & 1), mc)


def _sem_scratch(comm):
    return [pltpu.SemaphoreType.DMA((max(comm.n_remote, 1),)), pltpu.SemaphoreType.DMA((max(comm.n_remote, 1),)),
            pltpu.SemaphoreType.DMA((max(comm.n_local, 1),))]


def _run(body, *, name, grid, in_specs, out_specs, out_shape, args, scratch_shapes=(), comm=None, aliases=None,
         prefetch=None):
    sem = ("arbitrary",) * len(grid)
    cparams = pltpu.CompilerParams(dimension_semantics=sem, vmem_limit_bytes=VMEM_LIMIT)
    aliases = dict(aliases or {})
    n_pre = 0 if prefetch is None else 1

    def call(fn, in_specs, out_specs, out_shape, scratch_shapes, args):
        if prefetch is None:
            return pl.pallas_call(fn, name=name, grid=grid, in_specs=in_specs, out_specs=out_specs, out_shape=out_shape,
                                  scratch_shapes=scratch_shapes, compiler_params=cparams,
                                  input_output_aliases=aliases)(*args)
        spec = pltpu.PrefetchScalarGridSpec(num_scalar_prefetch=1, grid=grid, in_specs=in_specs, out_specs=out_specs,
                                            scratch_shapes=scratch_shapes)
        return pl.pallas_call(fn, name=name, grid_spec=spec, out_shape=out_shape, compiler_params=cparams,
                              input_output_aliases={k + 1: v for k, v in aliases.items()})(prefetch, *args)

    if comm is None:
        return call(body, list(in_specs), list(out_specs), list(out_shape), list(scratch_shapes), args)
    n_in, n_cs, n_out, n_cd, n_scr = len(in_specs), len(comm.srcs), len(out_specs), len(comm.dsts), len(scratch_shapes)
    aliases.update({n_in + si: n_out + di for si, di in comm.alias})
    total = math.prod(grid)
    mid_step = min(total - 1, int(total * 0.7))

    def wrapped(*refs):
        pre, refs = refs[:n_pre], refs[n_pre:]
        ins, refs = refs[:n_in], refs[n_in:]
        csrc, refs = refs[:n_cs], refs[n_cs:]
        outs, refs = refs[:n_out], refs[n_out:]
        cdst, refs = refs[:n_cd], refs[n_cd:]
        scr, sems = refs[:n_scr], refs[n_scr:]
        step = pl.program_id(0)
        for ax in range(1, len(grid)):
            step = step * grid[ax] + pl.program_id(ax)
        first, mid, last = comm.build(csrc, cdst, *sems, 0, 0)
        pl.when(step == 0)(first)
        if mid is not None:
            pl.when(step == mid_step)(mid)
        body(*pre, *ins, *outs, *scr)
        pl.when(step == total - 1)(last)

    res = call(wrapped, list(in_specs) + [HBM] * n_cs, list(out_specs) + [HBM] * n_cd,
               list(out_shape) + list(comm.dsts), list(scratch_shapes) + _sem_scratch(comm), (*args, *comm.srcs))
    return res[:n_out], res[n_out:]


def _comm_call(comm, *, name):
    def body(*refs):
        n_cs, n_cd = len(comm.srcs), len(comm.dsts)
        first, mid, last = comm.build(refs[:n_cs], refs[n_cs:n_cs + n_cd], *refs[n_cs + n_cd:], 0, 0)
        first()
        if mid is not None:
            mid()
        last()

    return pl.pallas_call(body, name=name, in_specs=[HBM] * len(comm.srcs), out_specs=[HBM] * len(comm.dsts),
                          out_shape=list(comm.dsts), scratch_shapes=_sem_scratch(comm),
                          input_output_aliases=dict(comm.alias))(*comm.srcs)


def _gather_comm(bufs):
    dsts = tuple(jax.ShapeDtypeStruct(b.shape, b.dtype) for b in bufs)
    nw = len(bufs)

    def build(srcs, outs, send_sems, recv_sems, local_sems, r0, l0):
        mx, my, mc = _my_coords()
        me = 2 * mx + my
        sibling = (mx, my, 1 - mc)

        def rdma(src, dst, idx, peer):
            return pltpu.make_async_remote_copy(src_ref=src, dst_ref=dst, send_sem=send_sems.at[r0 + idx],
                                                recv_sem=recv_sems.at[r0 + idx], device_id=peer, device_id_type=MESH)

        def ici(w, k, slot):
            return rdma(outs[w].at[me, mc], outs[w].at[slot, mc], 6 * w + k - 1, _chip_peer(k))

        def fwd(w, k, half):
            return rdma(outs[w].at[me ^ k, mc], outs[w].at[me ^ k, half], 6 * w + 2 + k, sibling)

        def first():
            for w in range(nw):
                for k in range(1, N_CHIPS):
                    ici(w, k, me).start()

        def mid():
            for w in range(nw):
                for k in range(1, N_CHIPS):
                    ici(w, k, me ^ k).wait_recv()
                    fwd(w, k, mc).start()

        def last():
            for w in range(nw):
                for k in range(1, N_CHIPS):
                    fwd(w, k, 1 - mc).wait_recv()
                    ici(w, k, me).wait_send()
                    fwd(w, k, mc).wait_send()

        return first, mid, last

    return _Comm(tuple(bufs), dsts, 6 * nw, 0, build, tuple((w, w) for w in range(nw)))


def _symmetric(make_remote, make_local, make_incoming=None):
    def first():
        for cp in make_remote() + make_local():
            cp.start()

    def last():
        for cp in (make_incoming or make_remote)():
            cp.wait_recv()
        for cp in make_remote():
            cp.wait_send()
        for cp in make_local():
            cp.wait()

    return first, None, last


def _presum_comm(g_bf16):
    nw = len(g_bf16)
    dsts = tuple(jax.ShapeDtypeStruct((N_CHIPS,) + g.shape[2:], BF16) for g in g_bf16)

    def build(srcs, outs, send_sems, recv_sems, local_sems, r0, l0):
        mx, my, mc = _my_coords()

        def remote():
            return [pltpu.make_async_remote_copy(
                src_ref=srcs[w].at[j, 1 - mc], dst_ref=outs[w].at[j], send_sem=send_sems.at[r0 + N_CHIPS * w + j],
                recv_sem=recv_sems.at[r0 + N_CHIPS * w + j], device_id=(mx, my, 1 - mc), device_id_type=MESH)
                for w in range(nw) for j in range(N_CHIPS)]

        return _symmetric(remote, lambda: [])

    return _Comm(tuple(g_bf16), dsts, N_CHIPS * nw, 0, build)


def _scatter_comm(h_bf16):
    nw = len(h_bf16)
    dsts = tuple(jax.ShapeDtypeStruct((N_CHIPS - 1,) + h.shape[1:], BF16) for h in h_bf16)

    def build(srcs, outs, send_sems, recv_sems, local_sems, r0, l0):
        mx, my, _ = _my_coords()
        me = 2 * mx + my

        def remote():
            return [pltpu.make_async_remote_copy(
                src_ref=srcs[w].at[me ^ k], dst_ref=outs[w].at[k - 1], send_sem=send_sems.at[r0 + 3 * w + k - 1],
                recv_sem=recv_sems.at[r0 + 3 * w + k - 1], device_id=_chip_peer(k), device_id_type=MESH)
                for w in range(nw) for k in range(1, N_CHIPS)]

        return _symmetric(remote, lambda: [])

    return _Comm(tuple(h_bf16), dsts, 3 * nw, 0, build)


def _join_comm(bufs):
    nw = len(bufs)
    dsts = tuple(jax.ShapeDtypeStruct(b.shape, b.dtype) for b in bufs)

    def build(srcs, outs, send_sems, recv_sems, local_sems, r0, l0):
        mx, my, mc = _my_coords()

        def remote(half=mc):
            return [pltpu.make_async_remote_copy(
                src_ref=outs[w].at[mc], dst_ref=outs[w].at[half], send_sem=send_sems.at[r0 + w],
                recv_sem=recv_sems.at[r0 + w], device_id=(mx, my, 1 - mc), device_id_type=MESH) for w in range(nw)]

        return _symmetric(remote, lambda: [], lambda: remote(1 - mc))

    return _Comm(tuple(bufs), dsts, nw, 0, build, tuple((w, w) for w in range(nw)))


def _merge(comms):
    comms = list(comms)
    if len(comms) == 1:
        return comms[0]

    def build(srcs, outs, send_sems, recv_sems, local_sems, r0, l0):
        phases, s0, d0 = [], 0, 0
        for cm in comms:
            phases.append(cm.build(srcs[s0:s0 + len(cm.srcs)], outs[d0:d0 + len(cm.dsts)], send_sems, recv_sems,
                                   local_sems, r0, l0))
            s0, d0, r0, l0 = s0 + len(cm.srcs), d0 + len(cm.dsts), r0 + cm.n_remote, l0 + cm.n_local

        def run(idx):
            fns = [ph[idx] for ph in phases if ph[idx] is not None]
            if not fns:
                return None

            def go():
                for fn in fns:
                    fn()
            return go

        return run(0), run(1), run(2)

    alias, s0, d0 = [], 0, 0
    for cm in comms:
        alias += [(s0 + si, d0 + di) for si, di in cm.alias]
        s0, d0 = s0 + len(cm.srcs), d0 + len(cm.dsts)
    return _Comm(sum((cm.srcs for cm in comms), ()), sum((cm.dsts for cm in comms), ()),
                 sum(cm.n_remote for cm in comms), sum(cm.n_local for cm in comms), build, tuple(alias))


def _split(comms, res):
    out, d0 = [], 0
    for cm in comms:
        out.append(list(res[d0:d0 + len(cm.dsts)]))
        d0 += len(cm.dsts)
    return out


def _all_reduce_small(x, *, name):
    r, lanes = x.shape

    def body(x_ref, out_ref, sib_ref, slots_ref, send_sems, recv_sems):
        mx, my, mc = _my_coords()
        me = 2 * mx + my
        swap = pltpu.make_async_remote_copy(src_ref=x_ref, dst_ref=sib_ref, send_sem=send_sems.at[0],
                                            recv_sem=recv_sems.at[0], device_id=(mx, my, 1 - mc), device_id_type=MESH)
        swap.start()
        swap.wait_recv()
        swap.wait_send()
        slots_ref[me] = x_ref[...] + sib_ref[...]

        def copy(k, slot):
            return pltpu.make_async_remote_copy(
                src_ref=slots_ref.at[me], dst_ref=slots_ref.at[slot], send_sem=send_sems.at[k], recv_sem=recv_sems.at[k],
                device_id=_chip_peer(k), device_id_type=MESH)

        sends = [copy(k, me) for k in range(1, N_CHIPS)]
        for cp in sends:
            cp.start()
        for k in range(1, N_CHIPS):
            copy(k, me ^ k).wait_recv()
        for cp in sends:
            cp.wait_send()
        acc = slots_ref[0]
        for j in range(1, N_CHIPS):
            acc = acc + slots_ref[j]
        out_ref[...] = acc

    vmem = pl.BlockSpec(memory_space=pltpu.VMEM)
    return pl.pallas_call(
        body, name=name, in_specs=[vmem], out_specs=vmem, out_shape=jax.ShapeDtypeStruct((r, lanes), F32),
        scratch_shapes=[pltpu.VMEM((r, lanes), F32), pltpu.VMEM((N_CHIPS, r, lanes), F32),
                        pltpu.SemaphoreType.DMA((N_CHIPS,)), pltpu.SemaphoreType.DMA((N_CHIPS,))],
        compiler_params=pltpu.CompilerParams(vmem_limit_bytes=VMEM_LIMIT),
    )(x)


def _all_gather8(x, *, name, with_sum=False):
    r, lanes = x.shape

    def body(x_ref, out_ref, *rest):
        if with_sum:
            sum_ref, send_sems, recv_sems, local_sem = rest
        else:
            send_sems, recv_sems, local_sem = rest
        mx, my, mc = _my_coords()
        me = 4 * mx + 2 * my + mc

        def peer(k):
            return (mx ^ ((k >> 2) & 1), my ^ ((k >> 1) & 1), mc ^ (k & 1))

        def copy(k, slot):
            return pltpu.make_async_remote_copy(
                src_ref=x_ref, dst_ref=out_ref.at[slot], send_sem=send_sems.at[k - 1], recv_sem=recv_sems.at[k - 1],
                device_id=peer(k), device_id_type=MESH)

        mine = pltpu.make_async_copy(x_ref, out_ref.at[me], local_sem)
        mine.start()
        sends = [copy(k, me) for k in range(1, N_DEV)]
        for cp in sends:
            cp.start()
        for k in range(1, N_DEV):
            copy(k, me ^ k).wait_recv()
        for cp in sends:
            cp.wait_send()
        mine.wait()
        if with_sum:
            acc = out_ref[0]
            for k in range(1, N_DEV):
                acc = acc + out_ref[k]
            sum_ref[...] = acc

    vmem = pl.BlockSpec(memory_space=pltpu.VMEM)
    out_shape = [jax.ShapeDtypeStruct((N_DEV, r, lanes), F32)]
    if with_sum:
        out_shape.append(jax.ShapeDtypeStruct((r, lanes), F32))
    res = pl.pallas_call(
        body, name=name, in_specs=[vmem], out_specs=[vmem] * len(out_shape), out_shape=out_shape,
        scratch_shapes=[pltpu.SemaphoreType.DMA((N_DEV - 1,)), pltpu.SemaphoreType.DMA((N_DEV - 1,)),
                        pltpu.SemaphoreType.DMA],
        compiler_params=pltpu.CompilerParams(vmem_limit_bytes=VMEM_LIMIT),
    )(x)
    return res if with_sum else res[0]


def _mod_matmul(x, sc, sh, w4, b, *, ts, tn, name, comm=None):
    s, d = x.shape
    wd = w4.shape[2]
    n = N_CHIPS * wd
    per = wd // tn
    ts = _tile(ts, s)

    def body(x_ref, sc_ref, sh_ref, w_ref, b_ref, o_ref, ht_ref, h_scr):
        @pl.when(pl.program_id(1) == 0)
        def _():
            h = x_ref[...] * (1.0 + sc_ref[...]) + sh_ref[...]
            h_scr[...] = h.astype(BF16)
            ht_ref[...] = h.T.astype(BF16)
        o_ref[...] = (_dot(h_scr[...], w_ref[0]) + b_ref[...]).astype(BF16)

    return _run(
        body, name=name, grid=(s // ts, n // tn),
        in_specs=[pl.BlockSpec((ts, d), lambda i, j: (i, 0)), _const((1, d)), _const((1, d)),
                  pl.BlockSpec((1, d, tn), lambda i, j: (j // per, 0, j % per)),
                  pl.BlockSpec((1, tn), lambda i, j: (0, j))],
        out_specs=[pl.BlockSpec((ts, tn), lambda i, j: (i, j)), pl.BlockSpec((d, ts), lambda i, j: (0, i))],
        out_shape=[jax.ShapeDtypeStruct((s, n), BF16), jax.ShapeDtypeStruct((d, s), BF16)],
        scratch_shapes=[pltpu.VMEM((ts, d), BF16)],
        args=(x, sc, sh, w4, b), comm=comm)


def _conv3(q, ext, cw):
    return cw[2:3] * q + cw[1:2] * _rows_before(ext, 1, CONV_HALO) + cw[0:1] * _rows_before(ext, 2, CONV_HALO)


def _mix_a_fwd(z, cw, w_out, *, ts, name):
    s = z.shape[0]
    d = D_MODEL
    ts = _tile(ts, s)

    def body(zb_ref, zc_ref, zx_ref, cw_ref, w_ref, a_ref, y_ref, carry):
        @pl.when(pl.program_id(0) == 0)
        def _():
            carry[...] = jnp.zeros_like(carry)
        q = zc_ref[...].astype(F32) * zx_ref[...].astype(F32)
        ext = jnp.concatenate([carry[...], q], axis=0)
        a = (zb_ref[...].astype(F32) * _conv3(q, ext, cw_ref[...])).astype(BF16)
        carry[...] = q[ts - CONV_HALO:]
        a_ref[...] = a
        y_ref[...] = _dot(a, w_ref[...])

    zspec = lambda k: pl.BlockSpec((ts, d), lambda i, k=k: (i, k))
    return _run(
        body, name=name, grid=(s // ts,),
        in_specs=[zspec(0), zspec(1), zspec(2), _const((3, d)), _const((d, d))],
        out_specs=[pl.BlockSpec((ts, d), lambda i: (i, 0))] * 2,
        out_shape=[jax.ShapeDtypeStruct((s, d), BF16), jax.ShapeDtypeStruct((s, d), F32)],
        scratch_shapes=[pltpu.VMEM((CONV_HALO, d), F32)],
        args=(z, z, z, cw, w_out))


def _spatial_mix(vn_b, ws_ref, bst_ref, mixed_scr, ts):
    nblk = ts // GBLK
    mask = _spatial_mask()
    for g in range(N_GROUPS):
        cols = slice(g * GBLK, (g + 1) * GBLK)
        wm = jnp.where(mask, ws_ref[g], 0.0).astype(BF16)
        cat = jnp.concatenate([vn_b[n * GBLK:(n + 1) * GBLK, cols] for n in range(nblk)], axis=1)
        res = _dot(wm, cat) + bst_ref[:, g:g + 1]
        for n in range(nblk):
            mixed_scr[n * GBLK:(n + 1) * GBLK, cols] = res[:, n * GBLK:(n + 1) * GBLK]


def _mix_b_fwd(z, ln_g, ln_b, ws, bst, w_out, *, ts, name, comm=None):
    s = z.shape[0]
    d = D_MODEL
    ts = _tile(ts, s)

    def body(zu_ref, zv_ref, g_ref, b_ref, ws_ref, bst_ref, w_ref, sg_ref, y_ref, mixed_scr):
        xhat, _ = _ln_fwd(_gelu(zv_ref[...].astype(F32)))
        vn = (xhat * g_ref[...] + b_ref[...]).astype(BF16)
        _spatial_mix(vn, ws_ref, bst_ref, mixed_scr, ts)
        sg = (_gelu(zu_ref[...].astype(F32)) * mixed_scr[...]).astype(BF16)
        sg_ref[...] = sg
        y_ref[...] = _dot(sg, w_ref[...])

    zspec = lambda k: pl.BlockSpec((ts, d), lambda i, k=k: (i, k))
    return _run(
        body, name=name, grid=(s // ts,),
        in_specs=[zspec(3), zspec(4), _const((1, d)), _const((1, d)), _const((N_GROUPS, GBLK, GBLK)),
                  _const((GBLK, N_GROUPS)), _const((d, d))],
        out_specs=[pl.BlockSpec((ts, d), lambda i: (i, 0))] * 2,
        out_shape=[jax.ShapeDtypeStruct((s, d), BF16), jax.ShapeDtypeStruct((s, d), F32)],
        scratch_shapes=[pltpu.VMEM((ts, d), F32)],
        args=(z, z, ln_g, ln_b, ws, bst, w_out), comm=comm)


def _pool_denoms(tile_idx, ts):
    t1 = (tile_idx * ts + 1 + lax.broadcasted_iota(jnp.int32, (ts, 1), 0)).astype(F32)
    return [jnp.minimum(t1, float(w)) for w in POOL_WINDOWS]


def _pool_diff(p, ext, denoms, k):
    cols = slice(k * POOL_GROUP, (k + 1) * POOL_GROUP)
    acc = ext[:, cols]
    step = 1
    while step < POOL_WINDOWS[k]:
        acc = acc + pltpu.roll(acc, step, 0)
        step *= 2
    return acc[POOL_HALO:] / denoms[k] - p[:, cols]


def _mix_c_fwd(z, w_pool, scale, *, ts, name):
    s = z.shape[0]
    d = D_MODEL
    ts = _tile(ts, s)

    def body(zp_ref, w_ref, sc_ref, d_ref, y_ref, carry):
        i = pl.program_id(0)

        @pl.when(i == 0)
        def _():
            carry[...] = jnp.zeros_like(carry)
        p = zp_ref[...].astype(F32)
        ext = jnp.concatenate([carry[...], p], axis=0)
        carry[...] = p[ts - POOL_HALO:]
        denoms = _pool_denoms(i, ts)
        for k in range(len(POOL_WINDOWS)):
            cols = slice(k * POOL_GROUP, (k + 1) * POOL_GROUP)
            dk = _pool_diff(p, ext, denoms, k).astype(BF16)
            d_ref[:, cols] = dk
            y_ref[:, cols] = _dot(dk, w_ref[k]) * sc_ref[:, cols]

    return _run(
        body, name=name, grid=(s // ts,),
        in_specs=[pl.BlockSpec((ts, d), lambda i: (i, 5)), _const((4, POOL_GROUP, POOL_GROUP)), _const((1, d))],
        out_specs=[pl.BlockSpec((ts, d), lambda i: (i, 0))] * 2,
        out_shape=[jax.ShapeDtypeStruct((s, d), BF16), jax.ShapeDtypeStruct((s, d), F32)],
        scratch_shapes=[pltpu.VMEM((POOL_HALO, d), F32)],
        args=(z, w_pool, scale))


def _mix_o_fwd(x, z, ya, yb, yc, w_o, gt, ln_g, ln_b, *, ts, name, comm=None):
    s, d = x.shape
    ts = _tile(ts, s)

    def body(x_ref, ga_ref, gb_ref, gc_ref, ya_ref, yb_ref, yc_ref, w_ref, gt_ref, g_ref, b_ref,
             m_ref, o_ref, x1_ref):
        merged = (_sigmoid(ga_ref[...].astype(F32)) * ya_ref[...] + _sigmoid(gb_ref[...].astype(F32)) * yb_ref[...]
                  + _sigmoid(gc_ref[...].astype(F32)) * yc_ref[...]).astype(BF16)
        m_ref[...] = merged
        o = _dot(merged, w_ref[...])
        o_ref[...] = o
        xhat, _ = _ln_fwd(ALPHA * x_ref[...] + gt_ref[...] * o)
        x1_ref[...] = xhat * g_ref[...] + b_ref[...]

    row = pl.BlockSpec((ts, d), lambda i: (i, 0))
    zspec = lambda k: pl.BlockSpec((ts, d), lambda i, k=k: (i, k))
    return _run(
        body, name=name, grid=(s // ts,),
        in_specs=[row, zspec(6), zspec(7), zspec(8), row, row, row, _const((d, d)),
                  _const((1, d)), _const((1, d)), _const((1, d))],
        out_specs=[row] * 3,
        out_shape=[jax.ShapeDtypeStruct((s, d), BF16), jax.ShapeDtypeStruct((s, d), F32),
                   jax.ShapeDtypeStruct((s, d), F32)],
        args=(x, z, z, z, ya, yb, yc, w_o, gt, ln_g, ln_b), comm=comm)


def _ffn_fwd(up, x1, cw, cb, w_down, gt, ln_g, ln_b, *, ts, name, comm=None):
    s, d = x1.shape
    ts = _tile(ts, s)

    def body(up_ref, x1_ref, cw_ref, cb_ref, w_ref, gt_ref, g_ref, b_ref, ft_ref, dn_ref, x2_ref, carry, f_ref):
        @pl.when(pl.program_id(0) == 0)
        def _():
            carry[...] = jnp.zeros_like(carry)
        for c in range(D_FF // FF_CHUNK):
            ca = slice(c * FF_CHUNK, (c + 1) * FF_CHUNK)
            cg = slice(D_FF + c * FF_CHUNK, D_FF + (c + 1) * FF_CHUNK)
            ua = up_ref[:, ca].astype(F32)
            ext = jnp.concatenate([carry[:, ca], ua], axis=0)
            carry[:, ca] = ua[ts - CONV_HALO:]
            cf = _conv3(ua, ext, cw_ref[:, ca]) + cb_ref[:, ca]
            f = _gelu(cf) * up_ref[:, cg].astype(F32)
            f_ref[:, ca] = f.astype(BF16)
            ft_ref[ca, :] = f.T.astype(BF16)
        dn = _dot(f_ref[...], w_ref[...])
        dn_ref[...] = dn
        xhat, _ = _ln_fwd(ALPHA * x1_ref[...] + gt_ref[...] * dn)
        x2_ref[...] = xhat * g_ref[...] + b_ref[...]

    row = pl.BlockSpec((ts, d), lambda i: (i, 0))
    return _run(
        body, name=name, grid=(s // ts,),
        in_specs=[pl.BlockSpec((ts, 2 * D_FF), lambda i: (i, 0)), row, _const((3, D_FF)), _const((1, D_FF)),
                  _resident((D_FF, d)), _const((1, d)), _const((1, d)), _const((1, d))],
        out_specs=[pl.BlockSpec((D_FF, ts), lambda i: (0, i)), row, row],
        out_shape=[jax.ShapeDtypeStruct((D_FF, s), BF16), jax.ShapeDtypeStruct((s, d), F32),
                   jax.ShapeDtypeStruct((s, d), F32)],
        scratch_shapes=[pltpu.VMEM((CONV_HALO, D_FF), F32), pltpu.VMEM((ts, D_FF), BF16)],
        args=(up, x1, cw, cb, w_down, gt, ln_g, ln_b), comm=comm)


def _loss_fwd(y, tgt, *, ts, name):
    s, d = y.shape
    ts = _tile(ts, s)

    def body(y_ref, t_ref, dy_ref, l_ref):
        @pl.when(pl.program_id(0) == 0)
        def _():
            l_ref[...] = jnp.zeros_like(l_ref)
        e = y_ref[...] - t_ref[...]
        dy_ref[...] = e / float(d)
        l_ref[...] += 0.5 * jnp.sum(jnp.mean(e * e, axis=-1, keepdims=True), axis=0, keepdims=True)

    row = pl.BlockSpec((ts, d), lambda i: (i, 0))
    return _run(body, name=name, grid=(s // ts,), in_specs=[row, row], out_specs=[row, _const((8, 128))],
                out_shape=[jax.ShapeDtypeStruct((s, d), F32), jax.ShapeDtypeStruct((8, 128), F32)], args=(y, tgt))


def _rev(n_tiles):
    return lambda i: n_tiles - 1 - i


def _halo_spec(ts, n_tiles, halo, width, col):
    per = ts // halo
    return pl.BlockSpec((halo, width), lambda i: (jnp.maximum((n_tiles - 1 - i) * per - 1, 0), col))


def _ffn_bwd(dx2, x1, dn, up, cw, cb, w_down, w_up4, gt, ln_g, sc, *, ts, name, comm=None):
    s, d = x1.shape
    ts = _tile(ts, s)
    nt = s // ts
    rev = _rev(nt)
    wd = w_up4.shape[2]

    def w_up_cols(wu_ref, start):
        return wu_ref[start // wd, :, start % wd:start % wd + FF_CHUNK]

    def body(dx2_ref, x1_ref, dn_ref, up_ref, halo_ref, cw_ref, cb_ref, wd_ref, wu_ref, gt_ref, g_ref, sc_ref,
             ddn_ref, dup_ref, dx1_ref, redd_ref, redf_ref, dbup_ref, carry):
        i = pl.program_id(0)

        @pl.when(i == 0)
        def _():
            carry[...] = jnp.zeros_like(carry)
            redd_ref[...] = jnp.zeros_like(redd_ref)
            redf_ref[...] = jnp.zeros_like(redf_ref)
            dbup_ref[...] = jnp.zeros_like(dbup_ref)
        first_tile = i == nt - 1
        x1v, dnv, dyv = x1_ref[...], dn_ref[...], dx2_ref[...]
        xhat, rstd = _ln_fwd(ALPHA * x1v + gt_ref[...] * dnv)
        dr = _ln_bwd(dyv, g_ref[...], xhat, rstd)
        redd_ref[0:1, :] += _colsum(dyv * xhat)
        redd_ref[1:2, :] += _colsum(dyv)
        redd_ref[2:3, :] += _colsum(dr * dnv)
        ddn = (gt_ref[...] * dr).astype(BF16)
        ddn_ref[...] = ddn
        dh = jnp.zeros((ts, d), F32)
        for c in range(D_FF // FF_CHUNK):
            ca = slice(c * FF_CHUNK, (c + 1) * FF_CHUNK)
            cg = slice(D_FF + c * FF_CHUNK, D_FF + (c + 1) * FF_CHUNK)
            df = _dot_nt(ddn, wd_ref[ca, :])
            ua, ug = up_ref[:, ca].astype(F32), up_ref[:, cg].astype(F32)
            halo = jnp.where(first_tile, 0.0, halo_ref[:, ca].astype(F32)[HALO_ROWS - CONV_HALO:])
            ext = jnp.concatenate([halo, ua], axis=0)
            u1, u2 = _rows_before(ext, 1, CONV_HALO), _rows_before(ext, 2, CONV_HALO)
            cwc = cw_ref[:, ca]
            gl, dgl = _gelu_and_grad(cwc[2:3] * ua + cwc[1:2] * u1 + cwc[0:1] * u2 + cb_ref[:, ca])
            dug = df * gl
            dcf = df * ug * dgl
            redf_ref[0:1, ca] += _colsum(dcf * u2)
            redf_ref[1:2, ca] += _colsum(dcf * u1)
            redf_ref[2:3, ca] += _colsum(dcf * ua)
            redf_ref[3:4, ca] += _colsum(dcf)
            extd = jnp.concatenate([dcf, carry[:, ca]], axis=0)
            carry[:, ca] = dcf[:CONV_HALO]
            dua = cwc[2:3] * dcf + cwc[1:2] * _rows_after(extd, 1, ts) + cwc[0:1] * _rows_after(extd, 2, ts)
            dbup_ref[0:1, ca] += _colsum(dua)
            dbup_ref[0:1, cg] += _colsum(dug)
            dua_b, dug_b = dua.astype(BF16), dug.astype(BF16)
            dup_ref[:, ca] = dua_b
            dup_ref[:, cg] = dug_b
            dh = dh + _dot_nt(dua_b, w_up_cols(wu_ref, c * FF_CHUNK)) + _dot_nt(dug_b, w_up_cols(wu_ref, D_FF + c * FF_CHUNK))
        dx1_ref[...] = ALPHA * dr + dh * (1.0 + sc_ref[...])
        redd_ref[3:4, :] += _colsum(dh * x1v)
        redd_ref[4:5, :] += _colsum(dh)

    row = pl.BlockSpec((ts, d), lambda i: (rev(i), 0))
    return _run(
        body, name=name, grid=(nt,),
        in_specs=[row, row, row, pl.BlockSpec((ts, 2 * D_FF), lambda i: (rev(i), 0)),
                  _halo_spec(ts, nt, HALO_ROWS, D_FF, 0), _const((3, D_FF)), _const((1, D_FF)),
                  _resident((D_FF, d)), _resident((N_CHIPS, d, wd)), _const((1, d)), _const((1, d)), _const((1, d))],
        out_specs=[row, pl.BlockSpec((ts, 2 * D_FF), lambda i: (rev(i), 0)), row,
                   _const((8, d)), _const((8, D_FF)), _const((8, 2 * D_FF))],
        out_shape=[jax.ShapeDtypeStruct((s, d), BF16), jax.ShapeDtypeStruct((s, 2 * D_FF), BF16),
                   jax.ShapeDtypeStruct((s, d), F32), jax.ShapeDtypeStruct((8, d), F32),
                   jax.ShapeDtypeStruct((8, D_FF), F32), jax.ShapeDtypeStruct((8, 2 * D_FF), F32)],
        scratch_shapes=[pltpu.VMEM((CONV_HALO, D_FF), F32)],
        args=(dx2, x1, dn, up, up, cw, cb, w_down, w_up4, gt, ln_g, sc), comm=comm)


def _grad_matmul(xa, dy, *, ts, tn, name, mod=None, by_chip=False, comm=None):
    s, k = xa.shape
    n = dy.shape[1]
    ts = _tile(ts, s)
    nt = s // ts

    def body(*refs):
        if mod is None:
            xa_ref, dy_ref, o_ref, ob_ref = refs
            a = xa_ref[...]
        else:
            xa_ref, sc_ref, sh_ref, dy_ref, o_ref, ob_ref = refs
            a = (xa_ref[...] * (1.0 + sc_ref[...]) + sh_ref[...]).astype(BF16)
        t = pl.program_id(1)

        @pl.when(t == 0)
        def _():
            o_ref[...] = jnp.zeros_like(o_ref)
        o_ref[...] += _dot_tn(a, dy_ref[...]).reshape(o_ref.shape)

        @pl.when(t == nt - 1)
        def _():
            ob_ref[...] = o_ref[...].astype(BF16)

    xspec = pl.BlockSpec((ts, k), lambda j, t: (t, 0))
    dspec = pl.BlockSpec((ts, tn), lambda j, t: (t, j))
    in_specs = [xspec, dspec] if mod is None else [xspec, _const((1, k)), _const((1, k)), dspec]
    args = (xa, dy) if mod is None else (xa, mod[0], mod[1], dy)
    if by_chip:
        per = n // N_CHIPS // tn
        ospec = pl.BlockSpec((1, k, tn), lambda j, t: (j // per, 0, j % per))
        shape = (N_CHIPS, k, n // N_CHIPS)
    else:
        ospec = pl.BlockSpec((k, tn), lambda j, t: (0, j))
        shape = (k, n)
    return _run(body, name=name, grid=(n // tn, nt), in_specs=in_specs, out_specs=[ospec, ospec],
                out_shape=[jax.ShapeDtypeStruct(shape, F32), jax.ShapeDtypeStruct(shape, BF16)], args=args, comm=comm)


def _grad_matmul_t(xt, dy, *, tk, tn, name, by_chip=False, comm=None):
    k, s = xt.shape
    n = dy.shape[1]

    def body(xt_ref, dy_ref, o_ref, ob_ref):
        o = _dot(xt_ref[...], dy_ref[...]).reshape(o_ref.shape)
        o_ref[...] = o
        ob_ref[...] = o.astype(BF16)

    if by_chip:
        assert tk == k
        per = n // N_CHIPS // tn
        ospec = pl.BlockSpec((1, k, tn), lambda j, i: (j // per, 0, j % per))
        shape = (N_CHIPS, k, n // N_CHIPS)
    else:
        ospec = pl.BlockSpec((tk, tn), lambda j, i: (i, j))
        shape = (k, n)
    xspec = _resident((k, s)) if tk == k else pl.BlockSpec((tk, s), lambda j, i: (i, 0))
    dspec = _resident((s, n)) if tn == n else pl.BlockSpec((s, tn), lambda j, i: (0, j))
    return _run(body, name=name, grid=(n // tn, k // tk), in_specs=[xspec, dspec], out_specs=[ospec, ospec],
                out_shape=[jax.ShapeDtypeStruct(shape, F32), jax.ShapeDtypeStruct(shape, BF16)], args=(xt, dy), comm=comm)


def _mix_o_bwd(dx1, x, o, z, ya, yb, yc, w_o, gt, ln_g, *, ts, name, comm=None):
    s, d = x.shape
    ts = _tile(ts, s)

    def body(dx1_ref, x_ref, o_ref, ga_ref, gb_ref, gc_ref, ya_ref, yb_ref, yc_ref, w_ref, gt_ref, g_ref,
             do_ref, dxa_ref, dzg_ref, dya_ref, dyb_ref, dyc_ref, red_ref):
        @pl.when(pl.program_id(0) == 0)
        def _():
            red_ref[...] = jnp.zeros_like(red_ref)
        dyv, ov = dx1_ref[...], o_ref[...]
        xhat, rstd = _ln_fwd(ALPHA * x_ref[...] + gt_ref[...] * ov)
        dr = _ln_bwd(dyv, g_ref[...], xhat, rstd)
        red_ref[0:1, :] += _colsum(dyv * xhat)
        red_ref[1:2, :] += _colsum(dyv)
        red_ref[2:3, :] += _colsum(dr * ov)
        dxa_ref[...] = ALPHA * dr
        d_o = (gt_ref[...] * dr).astype(BF16)
        do_ref[...] = d_o
        dm = _dot_nt(d_o, w_ref[...])
        for k, (zg_ref, y_ref, dy_ref) in enumerate(((ga_ref, ya_ref, dya_ref), (gb_ref, yb_ref, dyb_ref),
                                                     (gc_ref, yc_ref, dyc_ref))):
            g = _sigmoid(zg_ref[...].astype(F32))
            dzg_ref[:, k * d:(k + 1) * d] = (dm * y_ref[...] * g * (1.0 - g)).astype(BF16)
            dy_ref[...] = (dm * g).astype(BF16)

    row = pl.BlockSpec((ts, d), lambda i: (i, 0))
    zspec = lambda k: pl.BlockSpec((ts, d), lambda i, k=k: (i, k))
    bf = jax.ShapeDtypeStruct((s, d), BF16)
    return _run(
        body, name=name, grid=(s // ts,),
        in_specs=[row, row, row, zspec(6), zspec(7), zspec(8), row, row, row, _const((d, d)),
                  _const((1, d)), _const((1, d))],
        out_specs=[row, row, pl.BlockSpec((ts, 3 * d), lambda i: (i, 2)), row, row, row, _const((8, d))],
        out_shape=[bf, jax.ShapeDtypeStruct((s, d), F32), jax.ShapeDtypeStruct((s, D_Z), BF16), bf, bf, bf,
                   jax.ShapeDtypeStruct((8, d), F32)],
        args=(dx1, x, o, z, z, z, ya, yb, yc, w_o, gt, ln_g), comm=comm)


def _mix_a_bwd(dya, z, dz, cw, w_out, *, ts, name, comm=None):
    s = z.shape[0]
    d = D_MODEL
    ts = _tile(ts, s)
    nt = s // ts
    rev = _rev(nt)

    def body(dya_ref, zb_ref, zc_ref, zx_ref, hc_ref, hx_ref, cw_ref, w_ref, dz_in, dz_ref, red_ref, carry):
        i = pl.program_id(0)

        @pl.when(i == 0)
        def _():
            carry[...] = jnp.zeros_like(carry)
            red_ref[...] = jnp.zeros_like(red_ref)
        zb, zc, zx = zb_ref[...].astype(F32), zc_ref[...].astype(F32), zx_ref[...].astype(F32)
        q = zc * zx
        halo = jnp.where(i == nt - 1, 0.0, (hc_ref[...].astype(F32) * hx_ref[...].astype(F32))[HALO_ROWS - CONV_HALO:])
        ext = jnp.concatenate([halo, q], axis=0)
        q1, q2 = _rows_before(ext, 1, CONV_HALO), _rows_before(ext, 2, CONV_HALO)
        cwv = cw_ref[...]
        cv = cwv[2:3] * q + cwv[1:2] * q1 + cwv[0:1] * q2
        da = _dot_nt(dya_ref[...], w_ref[...])
        dcv = da * zb
        red_ref[0:1, :] += _colsum(dcv * q2)
        red_ref[1:2, :] += _colsum(dcv * q1)
        red_ref[2:3, :] += _colsum(dcv * q)
        extd = jnp.concatenate([dcv, carry[...]], axis=0)
        carry[...] = dcv[:CONV_HALO]
        dq = cwv[2:3] * dcv + cwv[1:2] * _rows_after(extd, 1, ts) + cwv[0:1] * _rows_after(extd, 2, ts)
        dz_ref[:, 0:d] = (da * cv).astype(BF16)
        dz_ref[:, d:2 * d] = (dq * zx).astype(BF16)
        dz_ref[:, 2 * d:3 * d] = (dq * zc).astype(BF16)

    zspec = lambda k: pl.BlockSpec((ts, d), lambda i, k=k: (rev(i), k))
    return _run(
        body, name=name, grid=(nt,),
        in_specs=[pl.BlockSpec((ts, d), lambda i: (rev(i), 0)), zspec(0), zspec(1), zspec(2),
                  _halo_spec(ts, nt, HALO_ROWS, d, 1), _halo_spec(ts, nt, HALO_ROWS, d, 2),
                  _const((3, d)), _const((d, d)), HBM],
        out_specs=[pl.BlockSpec((ts, 3 * d), lambda i: (rev(i), 0)), _const((8, d))],
        out_shape=[jax.ShapeDtypeStruct((s, D_Z), BF16), jax.ShapeDtypeStruct((8, d), F32)],
        scratch_shapes=[pltpu.VMEM((CONV_HALO, d), F32)],
        args=(dya, z, z, z, z, z, cw, w_out, dz), aliases={8: 0}, comm=comm)


def _mix_b_bwd(dyb, z, dz, ln_g, ln_b, ws, bst, w_out, *, ts, name, comm=None):
    s = z.shape[0]
    d = D_MODEL
    ts = _tile(ts, s)
    nblk = ts // GBLK

    def body(dyb_ref, zu_ref, zv_ref, g_ref, b_ref, ws_ref, bst_ref, w_ref, dz_in,
             dz_ref, red_ref, dws_ref, dbst_ref, mixed_scr, dvn_scr, dzv_scr):
        @pl.when((pl.program_id(0) == 0) & (pl.program_id(1) == 0))
        def _():
            red_ref[...] = jnp.zeros_like(red_ref)
            dws_ref[...] = jnp.zeros_like(dws_ref)
            dbst_ref[...] = jnp.zeros_like(dbst_ref)

        @pl.when(pl.program_id(1) == 0)
        def _():
            u, du_dz = _gelu_and_grad(zu_ref[...].astype(F32))
            vg, dv_dz = _gelu_and_grad(zv_ref[...].astype(F32))
            xhat, rstd = _ln_fwd(vg)
            vn = (xhat * g_ref[...] + b_ref[...]).astype(BF16)
            _spatial_mix(vn, ws_ref, bst_ref, mixed_scr, ts)
            dsg = _dot_nt(dyb_ref[...], w_ref[...])
            dz_ref[...] = (dsg * mixed_scr[...] * du_dz).astype(BF16)
            dmix = dsg * u
            mask = _spatial_mask()
            for g in range(N_GROUPS):
                cols = slice(g * GBLK, (g + 1) * GBLK)
                wm = jnp.where(mask, ws_ref[g], 0.0).astype(BF16)
                dm_cat = jnp.concatenate([dmix[n * GBLK:(n + 1) * GBLK, cols] for n in range(nblk)], axis=1)
                vn_cat = jnp.concatenate([vn[n * GBLK:(n + 1) * GBLK, cols] for n in range(nblk)], axis=1)
                dm_b = dm_cat.astype(BF16)
                dbst_ref[:, g:g + 1] += jnp.sum(dm_cat, axis=1, keepdims=True)
                dws_ref[g] += jnp.where(mask, _dot_nt(dm_b, vn_cat), 0.0)
                dvn_cat = _dot_tn(wm, dm_b)
                for n in range(nblk):
                    dvn_scr[n * GBLK:(n + 1) * GBLK, cols] = dvn_cat[:, n * GBLK:(n + 1) * GBLK]
            dvn = dvn_scr[...]
            red_ref[0:1, :] += _colsum(dvn * xhat)
            red_ref[1:2, :] += _colsum(dvn)
            dzv_scr[...] = (_ln_bwd(dvn, g_ref[...], xhat, rstd) * dv_dz).astype(BF16)

        @pl.when(pl.program_id(1) == 1)
        def _():
            dz_ref[...] = dzv_scr[...]

    zspec = lambda k: pl.BlockSpec((ts, d), lambda i, h, k=k: (i, k))
    return _run(
        body, name=name, grid=(s // ts, 2),
        in_specs=[pl.BlockSpec((ts, d), lambda i, h: (i, 0)), zspec(3), zspec(4), _const((1, d)), _const((1, d)),
                  _const((N_GROUPS, GBLK, GBLK)), _const((GBLK, N_GROUPS)), _const((d, d)), HBM],
        out_specs=[pl.BlockSpec((ts, d), lambda i, h: (i, 3 + h)), _const((8, d)),
                   _const((N_GROUPS, GBLK, GBLK)), _const((GBLK, N_GROUPS))],
        out_shape=[jax.ShapeDtypeStruct((s, D_Z), BF16), jax.ShapeDtypeStruct((8, d), F32),
                   jax.ShapeDtypeStruct((N_GROUPS, GBLK, GBLK), F32), jax.ShapeDtypeStruct((GBLK, N_GROUPS), F32)],
        scratch_shapes=[pltpu.VMEM((ts, d), F32), pltpu.VMEM((ts, d), F32), pltpu.VMEM((ts, d), BF16)],
        args=(dyb, z, z, ln_g, ln_b, ws, bst, w_out, dz), aliases={8: 0}, comm=comm)


def _mix_c_bwd(dyc, z, dz, w_pool, scale, *, ts, name):
    s = z.shape[0]
    d = D_MODEL
    ts = _tile(ts, s)
    nt = s // ts
    rev = _rev(nt)

    def body(dyc_ref, zp_ref, halo_ref, w_ref, sc_ref, dz_in, dz_ref, red_ref, dw_ref, carry):
        i = pl.program_id(0)

        @pl.when(i == 0)
        def _():
            carry[...] = jnp.zeros_like(carry)
            red_ref[...] = jnp.zeros_like(red_ref)
            dw_ref[...] = jnp.zeros_like(dw_ref)
        p = zp_ref[...].astype(F32)
        ext = jnp.concatenate([jnp.where(i == nt - 1, 0.0, halo_ref[...].astype(F32)), p], axis=0)
        denoms = _pool_denoms(rev(i), ts)
        dyv = dyc_ref[...].astype(F32)
        for k in range(len(POOL_WINDOWS)):
            cols = slice(k * POOL_GROUP, (k + 1) * POOL_GROUP)
            dk = _pool_diff(p, ext, denoms, k).astype(BF16)
            red_ref[0:1, cols] += _colsum(dyv[:, cols] * _dot(dk, w_ref[k]))
            dpre = (dyv[:, cols] * sc_ref[:, cols]).astype(BF16)
            dw_ref[k] += _dot_tn(dk, dpre)
            dd = _dot_nt(dpre, w_ref[k])
            e = dd / denoms[k]
            acc = jnp.concatenate([e, carry[:, cols]], axis=0)
            carry[:, cols] = e[:POOL_HALO]
            step = 1
            while step < POOL_WINDOWS[k]:
                acc = acc + pltpu.roll(acc, acc.shape[0] - step, 0)
                step *= 2
            dz_ref[:, cols] = (acc[:ts] - dd).astype(BF16)

    return _run(
        body, name=name, grid=(nt,),
        in_specs=[pl.BlockSpec((ts, d), lambda i: (rev(i), 0)), pl.BlockSpec((ts, d), lambda i: (rev(i), 5)),
                  _halo_spec(ts, nt, POOL_HALO, d, 5), _const((4, POOL_GROUP, POOL_GROUP)), _const((1, d)), HBM],
        out_specs=[pl.BlockSpec((ts, d), lambda i: (rev(i), 5)), _const((8, d)), _const((4, POOL_GROUP, POOL_GROUP))],
        out_shape=[jax.ShapeDtypeStruct((s, D_Z), BF16), jax.ShapeDtypeStruct((8, d), F32),
                   jax.ShapeDtypeStruct((4, POOL_GROUP, POOL_GROUP), F32)],
        scratch_shapes=[pltpu.VMEM((POOL_HALO, d), F32)],
        args=(dyc, z, z, w_pool, scale, dz), aliases={5: 0})


def _in_proj_bwd(dz, w4, dxa, x, sc, *, ts, name, comm=None):
    s, d = x.shape
    ts = _tile(ts, s)
    wd = w4.shape[2]

    def body(dz_ref, w_ref, dxa_ref, x_ref, sc_ref, dx_ref, red_ref, db_ref):
        @pl.when(pl.program_id(0) == 0)
        def _():
            red_ref[...] = jnp.zeros_like(red_ref)
            db_ref[...] = jnp.zeros_like(db_ref)
        dh = jnp.zeros((ts, d), F32)
        for j in range(N_CHIPS):
            dzj = dz_ref[:, j * wd:(j + 1) * wd]
            db_ref[0:1, j * wd:(j + 1) * wd] += _colsum(dzj.astype(F32))
            dh = dh + _dot_nt(dzj, w_ref[j])
        dx_ref[...] = dxa_ref[...] + dh * (1.0 + sc_ref[...])
        red_ref[0:1, :] += _colsum(dh * x_ref[...])
        red_ref[1:2, :] += _colsum(dh)

    row = pl.BlockSpec((ts, d), lambda i: (i, 0))
    return _run(
        body, name=name, grid=(s // ts,),
        in_specs=[pl.BlockSpec((ts, D_Z), lambda i: (i, 0)), _resident((N_CHIPS, d, wd)), row, row, _const((1, d))],
        out_specs=[row, _const((8, d)), _const((8, D_Z))],
        out_shape=[jax.ShapeDtypeStruct((s, d), F32), jax.ShapeDtypeStruct((8, d), F32),
                   jax.ShapeDtypeStruct((8, D_Z), F32)],
        args=(dz, w4, dxa, x, sc), comm=comm)


def _ada_fwd(c_all, w_ada, b_ada, *, name):
    nl, d, n = w_ada.shape
    tn = n // 2

    def body(c_ref, w_ref, b_ref, o_ref):
        cv = c_ref[...]
        ca = (cv * _sigmoid(cv)).astype(BF16)
        o_ref[0] = _dot(ca, w_ref[0].astype(BF16)) + b_ref[0]

    return _run(
        body, name=name, grid=(nl, n // tn),
        in_specs=[_const((N_DEV, d)), pl.BlockSpec((1, d, tn), lambda l, j: (l, 0, j)),
                  pl.BlockSpec((1, 1, tn), lambda l, j: (l, 0, j))],
        out_specs=[pl.BlockSpec((1, N_DEV, tn), lambda l, j: (l, 0, j))],
        out_shape=[jax.ShapeDtypeStruct((nl, N_DEV, n), F32)], args=(c_all, w_ada, b_ada))[0]


def _ada_bwd(c_all, dada, *, name):
    nl, nb, n = dada.shape
    d = c_all.shape[1]
    tn = n // 2

    def body(c_ref, g_ref, o_ref):
        cv = c_ref[...]
        ca = (cv * _sigmoid(cv)).astype(BF16)
        o_ref[0] = _dot_tn(ca, g_ref[0].astype(BF16))

    return _run(
        body, name=name, grid=(nl, n // tn),
        in_specs=[_const((nb, d)), pl.BlockSpec((1, nb, tn), lambda l, j: (l, 0, j))],
        out_specs=[pl.BlockSpec((1, d, tn), lambda l, j: (l, 0, j))],
        out_shape=[jax.ShapeDtypeStruct((nl, d, n), F32)], args=(c_all, dada))[0]


def _sum4_into_half(own, recv, core, *, name):
    r, c = own.shape
    tr = _row_tile(r, c, 2)

    def body(core_ref, own_ref, recv_ref, o_ref):
        acc = own_ref[...]
        for k in range(N_CHIPS - 1):
            acc = acc + recv_ref[k].astype(F32)
        o_ref[0] = acc

    spec = pltpu.PrefetchScalarGridSpec(
        num_scalar_prefetch=1, grid=(r // tr,),
        in_specs=[pl.BlockSpec((tr, c), lambda i, core_ref: (i, 0)),
                  pl.BlockSpec((N_CHIPS - 1, tr, c), lambda i, core_ref: (0, i, 0))],
        out_specs=pl.BlockSpec((1, tr, c), lambda i, core_ref: (core_ref[0], i, 0)))
    return pl.pallas_call(
        body, name=name, grid_spec=spec, out_shape=jax.ShapeDtypeStruct((2, r, c), F32),
        compiler_params=pltpu.CompilerParams(dimension_semantics=("arbitrary",), vmem_limit_bytes=VMEM_LIMIT),
    )(core, own, recv)


def _cast_into_slots(shards, layer, chip, *, name):
    quarters = 4

    def body(chip_ref, *refs):
        ins, outs = refs[:len(shards)], refs[len(shards):]
        for i_ref, o_ref in zip(ins, outs):
            o_ref[0, 0] = i_ref[0].astype(BF16)

    in_specs, out_specs, out_shape = [], [], []
    for sh in shards:
        _, r, c = sh.shape
        in_specs.append(pl.BlockSpec((1, r // quarters, c), lambda t, chip_ref: (layer, t, 0)))
        out_specs.append(pl.BlockSpec((1, 1, r // quarters, c), lambda t, chip_ref: (chip_ref[0], t // 2, t % 2, 0)))
        out_shape.append(jax.ShapeDtypeStruct((N_CHIPS, 2, r // 2, c), BF16))
    spec = pltpu.PrefetchScalarGridSpec(num_scalar_prefetch=1, grid=(quarters,), in_specs=in_specs, out_specs=out_specs)
    return pl.pallas_call(
        body, name=name, grid_spec=spec, out_shape=out_shape,
        compiler_params=pltpu.CompilerParams(dimension_semantics=("arbitrary",), vmem_limit_bytes=VMEM_LIMIT),
    )(chip, *shards)


def _sum_halves(g_f32, theirs, place, *, name, comm=None):
    _, _, rh, c = g_f32.shape
    tr = _row_tile(rh, c, 2)

    def body(place_ref, g_ref, t_ref, hb_ref, own_ref):
        h = g_ref[0, 0] + t_ref[0].astype(F32)
        hb_ref[0] = h.astype(BF16)

        @pl.when(pl.program_id(1) == place_ref[1])
        def _():
            own_ref[...] = h

    return _run(
        body, name=name, grid=(rh // tr, N_CHIPS), prefetch=place,
        in_specs=[pl.BlockSpec((1, 1, tr, c), lambda i, j, place_ref: (j, place_ref[0], i, 0)),
                  pl.BlockSpec((1, tr, c), lambda i, j, place_ref: (j, i, 0))],
        out_specs=[pl.BlockSpec((1, tr, c), lambda i, j, place_ref: (j, i, 0)),
                   pl.BlockSpec((tr, c), lambda i, j, place_ref: (i, 0))],
        out_shape=[jax.ShapeDtypeStruct((N_CHIPS, rh, c), BF16), jax.ShapeDtypeStruct((rh, c), F32)],
        args=(g_f32, theirs), comm=comm)


def _row_tile(r, c, mib):
    limit = max(8, (mib << 20) // (4 * c))
    if r <= limit:
        return r
    best = 8
    for t in range(8, limit + 1, 8):
        if r % t == 0:
            best = t
    return best


def _adam_math(w, g, m, v):
    mn = ADAM_B1 * m + (1.0 - ADAM_B1) * g
    vn = ADAM_B2 * v + (1.0 - ADAM_B2) * (g * g)
    m_hat = mn / (1.0 - ADAM_B1 ** ADAM_STEP)
    v_hat = vn / (1.0 - ADAM_B2 ** ADAM_STEP)
    return -ADAM_LR * (m_hat / (jnp.sqrt(v_hat) + ADAM_EPS) + ADAM_WD * w), mn, vn


def _adamw(w, g, m, v, *, name):
    r, c = w.shape
    tr = _row_tile(r, c, 2)

    def body(w_ref, g_ref, m_ref, v_ref, d_ref, mo_ref, vo_ref):
        d_ref[...], mo_ref[...], vo_ref[...] = _adam_math(w_ref[...], g_ref[...], m_ref[...], v_ref[...])

    blk = pl.BlockSpec((tr, c), lambda i: (i, 0))
    return _run(body, name=name, grid=(r // tr,), in_specs=[blk] * 4, out_specs=[blk] * 3,
                out_shape=[jax.ShapeDtypeStruct((r, c), F32)] * 3, args=(w, g, m, v))


def _adamw_sharded(w, m, v, grads, *, name, comm=None):
    nl, r, c = w.shape
    tr = _row_tile(r, c, 1)
    nt = r // tr

    def body(w_ref, m_ref, v_ref, g0_ref, g1_ref, g_ref, d_ref, mo_ref, vo_ref):
        g = jnp.where(pl.program_id(0) == 0, g0_ref[...], g1_ref[...])
        g_ref[0] = g
        d_ref[0], mo_ref[0], vo_ref[0] = _adam_math(w_ref[0], g, m_ref[0], v_ref[0])

    blk = pl.BlockSpec((1, tr, c), lambda l, i: (l, i, 0))
    part0 = pl.BlockSpec((tr, c), lambda l, i: (jnp.where(l == 0, i, nt - 1), 0))
    part1 = pl.BlockSpec((tr, c), lambda l, i: (jnp.where(l == 1, i, 0), 0))
    return _run(body, name=name, grid=(nl, nt), in_specs=[blk] * 3 + [part0, part1],
                out_specs=[blk] * 4, out_shape=[jax.ShapeDtypeStruct((nl, r, c), F32)] * 4,
                args=(w, m, v, grads[0], grads[1]), comm=comm)


_BIG = ("w_in", "w_a_out", "w_b_out", "w_pool", "w_o", "w_up", "w_down")
_COL_SHARDED = ("w_in", "w_up")
_SMALL_SHARDED = ("conv_a", "conv_ffn")
_SMALL_REPL = ("b_in", "ln_v_g", "ln_v_b", "w_spatial", "b_spatial", "pool_scale", "ln1_g", "ln1_b", "b_up",
               "conv_ffn_b", "ln2_g", "ln2_b")
_WEIGHTS = ("w_ada", "b_ada", "w_in", "b_in", "conv_a", "w_a_out", "ln_v_g", "ln_v_b", "w_spatial", "b_spatial",
            "w_b_out", "w_pool", "pool_scale", "w_o", "ln1_g", "ln1_b", "w_up", "b_up", "conv_ffn", "conv_ffn_b",
            "w_down", "ln2_g", "ln2_b")


def _shard3(a):
    return a.reshape(a.shape[0], -1, a.shape[-1])


def _use_gathered(name, g):
    g = g.reshape(N_CHIPS, -1, g.shape[-1])
    if name in _COL_SHARDED:
        return g
    if name == "w_pool":
        return g.reshape(N_CHIPS, 4, POOL_GROUP // N_CHIPS, POOL_GROUP).transpose(1, 0, 2, 3).reshape(
            4, POOL_GROUP, POOL_GROUP)
    return g.reshape(-1, g.shape[-1])


def _grad_by_chip(name, g):
    if name in _COL_SHARDED:
        return g
    if name == "w_pool":
        return g.reshape(4, N_CHIPS, POOL_GROUP // N_CHIPS, POOL_GROUP).transpose(1, 0, 2, 3).reshape(
            N_CHIPS, POOL_GROUP, POOL_GROUP)
    return g.reshape(N_CHIPS, -1, g.shape[-1])


def _pack_small(arrs):
    parts = []
    for a in arrs:
        flat = a.reshape(-1).astype(F32)
        pad = (-flat.shape[0]) % 128
        parts.append(jnp.pad(flat, (0, pad)) if pad else flat)
    flat = jnp.concatenate(parts)
    pad = (-flat.shape[0]) % 1024
    if pad:
        flat = jnp.pad(flat, (0, pad))
    return flat.reshape(-1, 128)


def _unpack_small(buf, shapes):
    lead = buf.shape[:-2]
    flat = buf.reshape(lead + (-1,))
    out, off = [], 0
    for shp in shapes:
        n = math.prod(shp)
        out.append(flat[..., off:off + n].reshape(lead + tuple(shp)))
        off += n + ((-n) % 128)
    return out


def _as2d(a):
    return a.reshape(-1, a.shape[-1])


_LATE = ("w_a_out", "w_b_out", "w_pool", "w_o")


class _Traffic:
    def __init__(self, slots, plan, core, chip):
        self.slots = slots
        self.plan = plan
        self.core = core
        self.place = jnp.concatenate([core, chip])
        self.gathered = {}
        self.ready = {}
        self.summed = {}
        self.half = {}
        self.final = {}

    def weight(self, layer, name):
        return self.gathered[(layer, name)]

    def add_grad(self, layer, name, g_f32, g_bf16):
        def halves(g):
            g = _grad_by_chip(name, g)
            return g.reshape(N_CHIPS, 2, g.shape[1] // 2, g.shape[2])
        self.ready[(layer, name)] = (halves(g_f32), halves(g_bf16))

    def _comm(self, job):
        if job[0] == "gather":
            return _gather_comm([self.slots[(job[1], k)] for k in job[2]])
        if job[0] == "presum":
            return _presum_comm([self.ready[k][1] for k in job[1]])
        if job[0] == "scatter":
            return _scatter_comm([self.summed[k][0] for k in job[1]])
        return _join_comm([self.half[k] for k in job[1]])

    def _done(self, job, res):
        if job[0] == "gather":
            for k, r in zip(job[2], res):
                self.gathered[(job[1], k)] = _use_gathered(k, r)
        elif job[0] == "presum":
            for k, r in zip(job[1], res):
                nm = f"presum_l{k[0]}_{k[1]}"
                self.summed[k] = self.run(nm, lambda cm: _sum_halves(self.ready.pop(k)[0], r, self.place, name=nm,
                                                                     comm=cm))
        elif job[0] == "scatter":
            for k, r in zip(job[1], res):
                self.half[k] = _sum4_into_half(self.summed.pop(k)[1], r, self.core, name=f"sum_l{k[0]}_{k[1]}")
        else:
            for k, r in zip(job[1], res):
                self.final[k] = r.reshape(-1, r.shape[-1])

    def run(self, name, fn):
        jobs = self.plan.get(name)
        if not jobs:
            return fn(None)
        comms = [self._comm(j) for j in jobs]
        outs, res = fn(_merge(comms))
        for job, r in zip(jobs, _split(comms, res)):
            self._done(job, r)
        return outs

    def alone(self, name):
        jobs = self.plan[name]
        comms = [self._comm(j) for j in jobs]
        for job, r in zip(jobs, _split(comms, _comm_call(_merge(comms), name=name))):
            self._done(job, r)


def _layer_fwd(x, ada, p, l, tr):
    sh1, sc1, gt1, sh2, sc2, gt2 = ada
    n = f"l{l}"
    z, ht = tr.run(f"{n}_in_proj", lambda cm: _mod_matmul(
        x, sc1, sh1, tr.weight(l, "w_in"), p["b_in"], ts=1024, tn=2304, name=f"{n}_in_proj", comm=cm))
    a, ya = _mix_a_fwd(z, p["conv_a"], tr.weight(l, "w_a_out"), ts=256, name=f"{n}_mix_a")
    sg, yb = tr.run(f"{n}_mix_b", lambda cm: _mix_b_fwd(
        z, p["ln_v_g"], p["ln_v_b"], p["w_spatial"], p["b_spatial_t"], tr.weight(l, "w_b_out"), ts=256,
        name=f"{n}_mix_b", comm=cm))
    dpool, yc = _mix_c_fwd(z, tr.weight(l, "w_pool"), p["pool_scale"], ts=256, name=f"{n}_mix_c")
    merged, o, x1 = tr.run(f"{n}_mix_o", lambda cm: _mix_o_fwd(
        x, z, ya, yb, yc, tr.weight(l, "w_o"), gt1, p["ln1_g"], p["ln1_b"], ts=256, name=f"{n}_mix_o", comm=cm))
    up, h2t = tr.run(f"{n}_up_proj", lambda cm: _mod_matmul(
        x1, sc2, sh2, tr.weight(l, "w_up"), p["b_up"], ts=1024, tn=1408, name=f"{n}_up_proj", comm=cm))
    ft, dn, x2 = tr.run(f"{n}_ffn", lambda cm: _ffn_fwd(
        up, x1, p["conv_ffn"], p["conv_ffn_b"], tr.weight(l, "w_down"), gt2, p["ln2_g"], p["ln2_b"], ts=256,
        name=f"{n}_ffn", comm=cm))
    saved = dict(x=x, z=z, ht=ht, a=a, ya=ya, sg=sg, yb=yb, dpool=dpool, yc=yc, merged=merged, o=o, x1=x1, h2t=h2t,
                 up=up, ft=ft, dn=dn)
    return x2, saved


def _layer_bwd(dx2, ada, p, sv, l, tr):
    sh1, sc1, gt1, sh2, sc2, gt2 = ada
    n = f"l{l}"
    ddn, dup, dx1, red_d, red_f, dbup = tr.run(f"{n}_ffn_bwd", lambda cm: _ffn_bwd(
        dx2, sv["x1"], sv["dn"], sv["up"], p["conv_ffn"], p["conv_ffn_b"], tr.weight(l, "w_down"),
        tr.weight(l, "w_up"), gt2, p["ln2_g"], sc2, ts=256, name=f"{n}_ffn_bwd", comm=cm))
    g = {}
    tr.add_grad(l, "w_down", *_grad_matmul_t(sv["ft"], ddn, tk=D_FF // N_CHIPS, tn=D_MODEL, name=f"{n}_dw_down"))
    tr.add_grad(l, "w_up", *tr.run(f"{n}_dw_up", lambda cm: _grad_matmul_t(
        sv["h2t"], dup, tk=D_MODEL, tn=FF_CHUNK, name=f"{n}_dw_up", by_chip=True, comm=cm)))
    g["ln2_g"], g["ln2_b"] = red_d[0], red_d[1]
    g["conv_ffn"], g["conv_ffn_b"], g["b_up"] = red_f[0:3], red_f[3], dbup[0]

    d_o, dxa, dz, dya, dyb, dyc, red_o = tr.run(f"{n}_mix_o_bwd", lambda cm: _mix_o_bwd(
        dx1, sv["x"], sv["o"], sv["z"], sv["ya"], sv["yb"], sv["yc"], tr.weight(l, "w_o"), gt1, p["ln1_g"], ts=256,
        name=f"{n}_mix_o_bwd", comm=cm))
    tr.add_grad(l, "w_o", *_grad_matmul(sv["merged"], d_o, ts=512, tn=1024, name=f"{n}_dw_o"))
    g["ln1_g"], g["ln1_b"] = red_o[0], red_o[1]

    dz, red_a = tr.run(f"{n}_mix_a_bwd", lambda cm: _mix_a_bwd(
        dya, sv["z"], dz, p["conv_a"], tr.weight(l, "w_a_out"), ts=256, name=f"{n}_mix_a_bwd", comm=cm))
    tr.add_grad(l, "w_a_out", *_grad_matmul(sv["a"], dya, ts=512, tn=1024, name=f"{n}_dw_a_out"))
    g["conv_a"] = red_a[0:3]

    dz, red_b, dws, dbst = tr.run(f"{n}_mix_b_bwd", lambda cm: _mix_b_bwd(
        dyb, sv["z"], dz, p["ln_v_g"], p["ln_v_b"], p["w_spatial"], p["b_spatial_t"], tr.weight(l, "w_b_out"), ts=256,
        name=f"{n}_mix_b_bwd", comm=cm))
    tr.add_grad(l, "w_b_out", *_grad_matmul(sv["sg"], dyb, ts=512, tn=1024, name=f"{n}_dw_b_out"))
    g["ln_v_g"], g["ln_v_b"], g["w_spatial"], g["b_spatial"] = red_b[0], red_b[1], dws, dbst.T

    dz, red_c, dwp = _mix_c_bwd(dyc, sv["z"], dz, tr.weight(l, "w_pool"), p["pool_scale"], ts=256,
                                name=f"{n}_mix_c_bwd")
    g["pool_scale"] = red_c[0]
    tr.add_grad(l, "w_pool", dwp, dwp.astype(BF16))

    tr.add_grad(l, "w_in", *tr.run(f"{n}_dw_in", lambda cm: _grad_matmul_t(
        sv["ht"], dz, tk=D_MODEL, tn=1152, name=f"{n}_dw_in", by_chip=True, comm=cm)))
    if f"{n}_presum_tail" in tr.plan:
        tr.alone(f"{n}_presum_tail")
    dx, red_i, dbin = tr.run(f"{n}_in_proj_bwd", lambda cm: _in_proj_bwd(
        dz, tr.weight(l, "w_in"), dxa, sv["x"], sc1, ts=256, name=f"{n}_in_proj_bwd", comm=cm))
    g["b_in"] = dbin[0]
    dada = jnp.stack([red_i[1], red_i[0], red_o[2], red_d[4], red_d[3], red_d[2]])
    return dx, g, dada


def _traffic_plan():
    plan = {
        "gather_l0": [("gather", 0, ("w_in",) + _LATE)],
        "l0_in_proj": [("gather", 1, ("w_in",))],
        "l0_mix_b": [("gather", 0, ("w_down",))],
        "l0_mix_o": [("gather", 0, ("w_up",))],
        "l0_up_proj": [("gather", 1, _LATE)],
        "l0_ffn": [("gather", 1, ("w_down",))],
        "l1_in_proj": [("gather", 1, ("w_up",))],
    }
    for l in reversed(range(DEPTH)):
        late = [(l, k) for k in _LATE]
        plan.update({
            f"l{l}_dw_up": [("presum", [(l, "w_down")])],
            f"l{l}_mix_o_bwd": [("presum", [(l, "w_up")]), ("scatter", [(l, "w_down")])],
            f"l{l}_mix_b_bwd": [("scatter", [(l, "w_up")])],
            f"l{l}_dw_in": [("presum", late), ("join", [(l, "w_down"), (l, "w_up")])],
        })
    late0, late1 = [(0, k) for k in _LATE], [(1, k) for k in _LATE]
    plan["l1_in_proj_bwd"] = [("presum", [(1, "w_in")]), ("scatter", late1)]
    plan["l0_ffn_bwd"] = [("scatter", [(1, "w_in")]), ("join", late1)]
    plan["l0_dw_up"] = plan["l0_dw_up"] + [("join", [(1, "w_in")])]
    plan["l0_presum_tail"] = [("presum", [(0, "w_in")])]
    plan["presum_l0_w_in"] = [("scatter", late0)]
    plan["l0_in_proj_bwd"] = [("scatter", [(0, "w_in")])]
    plan["join_tail"] = [("join", [(0, "w_in")] + late0)]
    return plan


def kernel(x, c, w_ada, b_ada, w_in, b_in, conv_a, w_a_out, ln_v_g, ln_v_b, w_spatial, b_spatial, w_b_out, w_pool, pool_scale, w_o, ln1_g, ln1_b, w_up, b_up, conv_ffn, conv_ffn_b, w_down, ln2_g, ln2_b, loss_target, m_w_ada, m_b_ada, m_w_in, m_b_in, m_conv_a, m_w_a_out, m_ln_v_g, m_ln_v_b, m_w_spatial, m_b_spatial, m_w_b_out, m_w_pool, m_pool_scale, m_w_o, m_ln1_g, m_ln1_b, m_w_up, m_b_up, m_conv_ffn, m_conv_ffn_b, m_w_down, m_ln2_g, m_ln2_b, v_w_ada, v_b_ada, v_w_in, v_b_in, v_conv_a, v_w_a_out, v_ln_v_g, v_ln_v_b, v_w_spatial, v_b_spatial, v_w_b_out, v_w_pool, v_pool_scale, v_w_o, v_ln1_g, v_ln1_b, v_w_up, v_b_up, v_conv_ffn, v_conv_ffn_b, v_w_down, v_ln2_g, v_ln2_b):
    args = locals()
    w = {k: args[k] for k in _WEIGHTS}
    m = {k: args["m_" + k] for k in _WEIGHTS}
    v = {k: args["v_" + k] for k in _WEIGHTS}
    d = D_MODEL
    mx, my, mc = _my_coords()
    chip = 2 * mx + my
    me = 4 * mx + 2 * my + mc

    small_shapes = [c.shape, conv_a.shape, conv_ffn.shape]
    small_all = _all_gather8(_pack_small([c, conv_a, conv_ffn]), name="gather_small")
    c_all, conv_a_st, conv_ffn_st = _unpack_small(small_all, small_shapes)
    c_all = c_all.reshape(N_DEV, d)
    conv_full = {"conv_a": jnp.concatenate([conv_a_st[2 * j] for j in range(N_CHIPS)], axis=-1),
                 "conv_ffn": jnp.concatenate([conv_ffn_st[2 * j] for j in range(N_CHIPS)], axis=-1)}

    chip_idx = jnp.reshape(chip, (1,)).astype(jnp.int32)
    slots = {}
    for l in range(DEPTH):
        bufs = _cast_into_slots([_shard3(w[k]) for k in _BIG], l, chip_idx, name=f"cast_l{l}")
        slots.update({(l, k): b for k, b in zip(_BIG, bufs)})
    tr = _Traffic(slots, _traffic_plan(), jnp.reshape(mc, (1,)).astype(jnp.int32), chip_idx)
    tr.alone("gather_l0")

    n_ada = w_ada.shape[2]
    b_ada_mine = lax.dynamic_slice_in_dim(b_ada, chip * n_ada, n_ada, axis=1)
    ada_part = _ada_fwd(c_all, w_ada, b_ada_mine.reshape(DEPTH, 1, n_ada), name="ada_fwd")
    ada_all = _all_gather8(_pack_small([ada_part]), name="gather_ada")
    ada_st = _unpack_small(ada_all, [ada_part.shape])[0][0::2]
    ada_rows = jnp.concatenate([ada_st[j] for j in range(N_CHIPS)], axis=-1)
    ada_mine = lax.dynamic_index_in_dim(ada_rows, me, axis=1, keepdims=False)

    def layer_params(l):
        p = {k: conv_full[k][l] for k in _SMALL_SHARDED}
        for k in ("b_in", "ln_v_g", "ln_v_b", "pool_scale", "ln1_g", "ln1_b", "b_up", "conv_ffn_b", "ln2_g", "ln2_b"):
            p[k] = w[k][l].reshape(1, -1)
        p["w_spatial"] = w_spatial[l]
        p["b_spatial_t"] = b_spatial[l].T
        return p

    xs = x[0]
    saved, adas, params = [], [], []
    for l in range(DEPTH):
        ada = [ada_mine[l, k * d:(k + 1) * d].reshape(1, d) for k in range(6)]
        p = layer_params(l)
        xs, sv = _layer_fwd(xs, ada, p, l, tr)
        saved.append(sv), adas.append(ada), params.append(p)
    dx, loss_blk = _loss_fwd(xs, loss_target[0], ts=512, name="loss")

    grads, dadas = [None] * DEPTH, [None] * DEPTH
    for l in reversed(range(DEPTH)):
        dx, grads[l], dadas[l] = _layer_bwd(dx, adas[l], params[l], saved[l], l, tr)
    tr.alone("join_tail")
    dada = jnp.stack(dadas).reshape(DEPTH, 6 * d)

    small_names = _SMALL_REPL + _SMALL_SHARDED
    small_g = [jnp.stack([grads[l][k] for l in range(DEPTH)]) for k in small_names]
    gsum = dict(zip(small_names, _unpack_small(_all_reduce_small(_pack_small(small_g), name="reduce_small"),
                                               [a.shape for a in small_g])))
    tail_g = [dada, loss_blk[0:1, 0:1]]
    tail_all, tail_sum = _all_gather8(_pack_small(tail_g), name="gather_dada", with_sum=True)
    gsum["b_ada"], loss_sum = _unpack_small(tail_sum, [a.shape for a in tail_g])
    loss = loss_sum[0, 0]
    dada_all = _unpack_small(tail_all, [a.shape for a in tail_g])[0]
    for k in _SMALL_SHARDED:
        wd = gsum[k].shape[-1] // N_CHIPS
        gsum[k] = lax.dynamic_slice_in_dim(gsum[k], chip * wd, wd, axis=gsum[k].ndim - 1)

    dada_cols = lax.dynamic_slice_in_dim(dada_all, chip * n_ada, n_ada, axis=2)
    dada_cols = jnp.pad(jnp.swapaxes(dada_cols, 0, 1), ((0, 0), (0, N_DEV), (0, 0)))
    gsum["w_ada"] = _ada_bwd(jnp.pad(c_all, ((0, N_DEV), (0, 0))), dada_cols, name="ada_bwd")

    out_g, out_d, out_m, out_v = {}, {}, {}, {}
    for k in _WEIGHTS:
        shp = w[k].shape
        if k in _BIG:
            res = tr.run(f"adamw_{k}", lambda cm: _adamw_sharded(
                _shard3(w[k]), _shard3(m[k]), _shard3(v[k]), [tr.final[(l, k)] for l in range(DEPTH)],
                name=f"adamw_{k}", comm=cm))
        else:
            gk = gsum[k].reshape(shp)
            res = [gk] + list(_adamw(_as2d(w[k]), _as2d(gk), _as2d(m[k]), _as2d(v[k]), name=f"adamw_{k}"))
        out_g[k], out_d[k], out_m[k], out_v[k] = [r.reshape(shp) for r in res]

    return (loss, dx[None], *[out_g[k] for k in _WEIGHTS], *[out_d[k] for k in _WEIGHTS],
            *[out_m[k] for k in _WEIGHTS], *[out_v[k] for k in _WEIGHTS])
```

```python
import math
from typing import Callable, NamedTuple

import jax
import jax.numpy as jnp
from jax import lax
from jax.experimental import pallas as pl
from jax.experimental.pallas import tpu as pltpu

F32 = jnp.float32
BF16 = jnp.bfloat16

D_MODEL = 1024
D_Z = 9216
D_FF = 2816
N_GROUPS = 8
GBLK = 128
CHUNK = 64
POOL_WINDOWS = (2, 4, 8, 16)
POOL_GROUP = 256
POOL_HALO = 16
CONV_HALO = 8
HALO_ROWS = 16
DEPTH = 2
ALPHA = (2 * DEPTH) ** 0.25
LN_EPS = 1e-5
ADAM_LR, ADAM_B1, ADAM_B2, ADAM_EPS, ADAM_WD, ADAM_STEP = 0.001, 0.9, 0.999, 1e-08, 0.01, 10
N_CHIPS = 4
N_DEV = 8
FF_CHUNK = 1408
MESH = pl.DeviceIdType.MESH
VMEM_LIMIT = 56 * 1024 * 1024
HBM = pl.BlockSpec(memory_space=pl.ANY)


def _dot(a, b):
    return jnp.dot(a, b, preferred_element_type=F32)


def _dot_nt(a, b):
    return lax.dot_general(a, b, (((1,), (1,)), ((), ())), preferred_element_type=F32)


def _dot_tn(a, b):
    return lax.dot_general(a, b, (((0,), (0,)), ((), ())), preferred_element_type=F32)


_GELU_C = math.sqrt(2.0 / math.pi)


def _gelu_and_grad(x):
    x2 = x * x
    t = jnp.tanh(_GELU_C * (x + 0.044715 * x * x2))
    g = 0.5 * x * (1.0 + t)
    dg = 0.5 * (1.0 + t) + 0.5 * x * (1.0 - t * t) * (_GELU_C * (1.0 + 3 * 0.044715 * x2))
    return g, dg


def _gelu(x):
    return 0.5 * x * (1.0 + jnp.tanh(_GELU_C * (x + 0.044715 * x * x * x)))


def _sigmoid(x):
    return 1.0 / (1.0 + jnp.exp(-x))


def _ln_fwd(r):
    mu = jnp.mean(r, axis=-1, keepdims=True)
    xc = r - mu
    var = jnp.mean(xc * xc, axis=-1, keepdims=True)
    rstd = lax.rsqrt(var + LN_EPS)
    return xc * rstd, rstd


def _ln_bwd(dy, g, xhat, rstd):
    dxh = dy * g
    m1 = jnp.mean(dxh, axis=-1, keepdims=True)
    m2 = jnp.mean(dxh * xhat, axis=-1, keepdims=True)
    return rstd * (dxh - m1 - xhat * m2)


def _rows_before(ext, k, halo):
    return pltpu.roll(ext, k, 0)[halo:]


def _rows_after(ext, k, n):
    return pltpu.roll(ext, ext.shape[0] - k, 0)[:n]


def _colsum(v):
    return jnp.sum(v, axis=0, keepdims=True)


def _spatial_mask():
    i = lax.broadcasted_iota(jnp.int32, (GBLK, GBLK), 0)
    j = lax.broadcasted_iota(jnp.int32, (GBLK, GBLK), 1)
    return (j // CHUNK) <= (i // CHUNK)


def _const(shape):
    n = len(shape)
    return pl.BlockSpec(shape, lambda *_: (0,) * n)


def _resident(shape):
    n = len(shape)
    return pl.BlockSpec(shape, lambda *_: (0,) * n, pipeline_mode=pl.Buffered(1))


def _tile(ts, s):
    return min(ts, s)


class _Comm(NamedTuple):
    srcs: tuple
    dsts: tuple
    n_remote: int
    n_local: int
    build: Callable
    alias: tuple = ()


def _my_coords():
    return lax.axis_index("x"), lax.axis_index("y"), lax.axis_index("c")


def _chip_peer(k):
    mx, my, mc = _my_coords()
    return (mx ^ ((k >> 1) & 1), my ^ (k & 1), mc)


def _sem_scratch(comm):
    return [pltpu.SemaphoreType.DMA((max(comm.n_remote, 1),)), pltpu.SemaphoreType.DMA((max(comm.n_remote, 1),)),
            pltpu.SemaphoreType.DMA((max(comm.n_local, 1),))]


def _run(body, *, name, grid, in_specs, out_specs, out_shape, args, scratch_shapes=(), comm=None, aliases=None,
         prefetch=None):
    sem = ("arbitrary",) * len(grid)
    cparams = pltpu.CompilerParams(dimension_semantics=sem, vmem_limit_bytes=VMEM_LIMIT)
    aliases = dict(aliases or {})
    n_pre = 0 if prefetch is None else 1

    def call(fn, in_specs, out_specs, out_shape, scratch_shapes, args):
        if prefetch is None:
            return pl.pallas_call(fn, name=name, grid=grid, in_specs=in_specs, out_specs=out_specs, out_shape=out_shape,
                                  scratch_shapes=scratch_shapes, compiler_params=cparams,
                                  input_output_aliases=aliases)(*args)
        spec = pltpu.PrefetchScalarGridSpec(num_scalar_prefetch=1, grid=grid, in_specs=in_specs, out_specs=out_specs,
                                            scratch_shapes=scratch_shapes)
        return pl.pallas_call(fn, name=name, grid_spec=spec, out_shape=out_shape, compiler_params=cparams,
                              input_output_aliases={k + 1: v for k, v in aliases.items()})(prefetch, *args)

    if comm is None:
        return call(body, list(in_specs), list(out_specs), list(out_shape), list(scratch_shapes), args)
    n_in, n_cs, n_out, n_cd, n_scr = len(in_specs), len(comm.srcs), len(out_specs), len(comm.dsts), len(scratch_shapes)
    aliases.update({n_in + si: n_out + di for si, di in comm.alias})
    total = math.prod(grid)
    mid_step = min(total - 1, int(total * 0.7))

    def wrapped(*refs):
        pre, refs = refs[:n_pre], refs[n_pre:]
        ins, refs = refs[:n_in], refs[n_in:]
        csrc, refs = refs[:n_cs], refs[n_cs:]
        outs, refs = refs[:n_out], refs[n_out:]
        cdst, refs = refs[:n_cd], refs[n_cd:]
        scr, sems = refs[:n_scr], refs[n_scr:]
        step = pl.program_id(0)
        for ax in range(1, len(grid)):
            step = step * grid[ax] + pl.program_id(ax)
        first, mid, last = comm.build(csrc, cdst, *sems, 0, 0)
        pl.when(step == 0)(first)
        if mid is not None:
            pl.when(step == mid_step)(mid)
        body(*pre, *ins, *outs, *scr)
        pl.when(step == total - 1)(last)

    res = call(wrapped, list(in_specs) + [HBM] * n_cs, list(out_specs) + [HBM] * n_cd,
               list(out_shape) + list(comm.dsts), list(scratch_shapes) + _sem_scratch(comm), (*args, *comm.srcs))
    return res[:n_out], res[n_out:]


def _comm_call(comm, *, name):
    def body(*refs):
        n_cs, n_cd = len(comm.srcs), len(comm.dsts)
        first, mid, last = comm.build(refs[:n_cs], refs[n_cs:n_cs + n_cd], *refs[n_cs + n_cd:], 0, 0)
        first()
        if mid is not None:
            mid()
        last()

    return pl.pallas_call(body, name=name, in_specs=[HBM] * len(comm.srcs), out_specs=[HBM] * len(comm.dsts),
                          out_shape=list(comm.dsts), scratch_shapes=_sem_scratch(comm),
                          input_output_aliases=dict(comm.alias))(*comm.srcs)


def _gather_comm(bufs):
    dsts = tuple(jax.ShapeDtypeStruct(b.shape, b.dtype) for b in bufs)
    nw = len(bufs)

    def build(srcs, outs, send_sems, recv_sems, local_sems, r0, l0):
        mx, my, mc = _my_coords()
        me = 2 * mx + my
        sibling = (mx, my, 1 - mc)

        def rdma(src, dst, idx, peer):
            return pltpu.make_async_remote_copy(src_ref=src, dst_ref=dst, send_sem=send_sems.at[r0 + idx],
                                                recv_sem=recv_sems.at[r0 + idx], device_id=peer, device_id_type=MESH)

        def ici(w, k, slot):
            return rdma(outs[w].at[me, mc], outs[w].at[slot, mc], 6 * w + k - 1, _chip_peer(k))

        def fwd(w, k, half):
            return rdma(outs[w].at[me ^ k, mc], outs[w].at[me ^ k, half], 6 * w + 2 + k, sibling)

        def first():
            for w in range(nw):
                for k in range(1, N_CHIPS):
                    ici(w, k, me).start()

        def mid():
            for w in range(nw):
                for k in range(1, N_CHIPS):
                    ici(w, k, me ^ k).wait_recv()
                    fwd(w, k, mc).start()

        def last():
            for w in range(nw):
                for k in range(1, N_CHIPS):
                    fwd(w, k, 1 - mc).wait_recv()
                    ici(w, k, me).wait_send()
                    fwd(w, k, mc).wait_send()

        return first, mid, last

    return _Comm(tuple(bufs), dsts, 6 * nw, 0, build, tuple((w, w) for w in range(nw)))


def _symmetric(make_remote, make_local, make_incoming=None):
    def first():
        for cp in make_remote() + make_local():
            cp.start()

    def last():
        for cp in (make_incoming or make_remote)():
            cp.wait_recv()
        for cp in make_remote():
            cp.wait_send()
        for cp in make_local():
            cp.wait()

    return first, None, last


def _presum_comm(g_bf16):
    nw = len(g_bf16)
    dsts = tuple(jax.ShapeDtypeStruct((N_CHIPS,) + g.shape[2:], BF16) for g in g_bf16)

    def build(srcs, outs, send_sems, recv_sems, local_sems, r0, l0):
        mx, my, mc = _my_coords()

        def remote():
            return [pltpu.make_async_remote_copy(
                src_ref=srcs[w].at[j, 1 - mc], dst_ref=outs[w].at[j], send_sem=send_sems.at[r0 + N_CHIPS * w + j],
                recv_sem=recv_sems.at[r0 + N_CHIPS * w + j], device_id=(mx, my, 1 - mc), device_id_type=MESH)
                for w in range(nw) for j in range(N_CHIPS)]

        return _symmetric(remote, lambda: [])

    return _Comm(tuple(g_bf16), dsts, N_CHIPS * nw, 0, build)


def _scatter_comm(h_bf16):
    nw = len(h_bf16)
    dsts = tuple(jax.ShapeDtypeStruct((N_CHIPS - 1,) + h.shape[1:], BF16) for h in h_bf16)

    def build(srcs, outs, send_sems, recv_sems, local_sems, r0, l0):
        mx, my, _ = _my_coords()
        me = 2 * mx + my

        def remote():
            return [pltpu.make_async_remote_copy(
                src_ref=srcs[w].at[me ^ k], dst_ref=outs[w].at[k - 1], send_sem=send_sems.at[r0 + 3 * w + k - 1],
                recv_sem=recv_sems.at[r0 + 3 * w + k - 1], device_id=_chip_peer(k), device_id_type=MESH)
                for w in range(nw) for k in range(1, N_CHIPS)]

        return _symmetric(remote, lambda: [])

    return _Comm(tuple(h_bf16), dsts, 3 * nw, 0, build)


def _join_comm(bufs):
    nw = len(bufs)
    dsts = tuple(jax.ShapeDtypeStruct(b.shape, b.dtype) for b in bufs)

    def build(srcs, outs, send_sems, recv_sems, local_sems, r0, l0):
        mx, my, mc = _my_coords()

        def remote(half=mc):
            return [pltpu.make_async_remote_copy(
                src_ref=outs[w].at[mc], dst_ref=outs[w].at[half], send_sem=send_sems.at[r0 + w],
                recv_sem=recv_sems.at[r0 + w], device_id=(mx, my, 1 - mc), device_id_type=MESH) for w in range(nw)]

        return _symmetric(remote, lambda: [], lambda: remote(1 - mc))

    return _Comm(tuple(bufs), dsts, nw, 0, build, tuple((w, w) for w in range(nw)))


def _merge(comms):
    comms = list(comms)
    if len(comms) == 1:
        return comms[0]

    def build(srcs, outs, send_sems, recv_sems, local_sems, r0, l0):
        phases, s0, d0 = [], 0, 0
        for cm in comms:
            phases.append(cm.build(srcs[s0:s0 + len(cm.srcs)], outs[d0:d0 + len(cm.dsts)], send_sems, recv_sems,
                                   local_sems, r0, l0))
            s0, d0, r0, l0 = s0 + len(cm.srcs), d0 + len(cm.dsts), r0 + cm.n_remote, l0 + cm.n_local

        def run(idx):
            fns = [ph[idx] for ph in phases if ph[idx] is not None]
            if not fns:
                return None

            def go():
                for fn in fns:
                    fn()
            return go

        return run(0), run(1), run(2)

    alias, s0, d0 = [], 0, 0
    for cm in comms:
        alias += [(s0 + si, d0 + di) for si, di in cm.alias]
        s0, d0 = s0 + len(cm.srcs), d0 + len(cm.dsts)
    return _Comm(sum((cm.srcs for cm in comms), ()), sum((cm.dsts for cm in comms), ()),
                 sum(cm.n_remote for cm in comms), sum(cm.n_local for cm in comms), build, tuple(alias))


def _split(comms, res):
    out, d0 = [], 0
    for cm in comms:
        out.append(list(res[d0:d0 + len(cm.dsts)]))
        d0 += len(cm.dsts)
    return out


def _all_reduce_small(x, *, name):
    r, lanes = x.shape
    half = r // 2
    assert half % 8 == 0

    def body(x_ref, out_ref, sib_ref, slots_ref, send_sems, recv_sems):
        mx, my, mc = _my_coords()
        me = 2 * mx + my
        sibling = (mx, my, 1 - mc)
        mine = pl.ds(pl.multiple_of(mc * half, 8), half)
        theirs = pl.ds(pl.multiple_of((1 - mc) * half, 8), half)

        def to_sibling(src, dst, idx):
            return pltpu.make_async_remote_copy(src_ref=src, dst_ref=dst, send_sem=send_sems.at[idx],
                                                recv_sem=recv_sems.at[idx], device_id=sibling, device_id_type=MESH)

        swap = to_sibling(x_ref.at[theirs], sib_ref, 0)
        swap.start()
        swap.wait_recv()
        swap.wait_send()
        slots_ref[me] = x_ref[mine, :] + sib_ref[...]

        def copy(k, slot):
            return pltpu.make_async_remote_copy(
                src_ref=slots_ref.at[me], dst_ref=slots_ref.at[slot], send_sem=send_sems.at[k], recv_sem=recv_sems.at[k],
                device_id=_chip_peer(k), device_id_type=MESH)

        sends = [copy(k, me) for k in range(1, N_CHIPS)]
        for cp in sends:
            cp.start()
        for k in range(1, N_CHIPS):
            copy(k, me ^ k).wait_recv()
        for cp in sends:
            cp.wait_send()
        acc = slots_ref[0]
        for j in range(1, N_CHIPS):
            acc = acc + slots_ref[j]
        out_ref[mine, :] = acc
        join = to_sibling(out_ref.at[mine], out_ref.at[mine], N_CHIPS)
        join.start()
        to_sibling(out_ref.at[mine], out_ref.at[theirs], N_CHIPS).wait_recv()
        join.wait_send()

    vmem = pl.BlockSpec(memory_space=pltpu.VMEM)
    return pl.pallas_call(
        body, name=name, in_specs=[vmem], out_specs=vmem, out_shape=jax.ShapeDtypeStruct((r, lanes), F32),
        scratch_shapes=[pltpu.VMEM((half, lanes), F32), pltpu.VMEM((N_CHIPS, half, lanes), F32),
                        pltpu.SemaphoreType.DMA((N_CHIPS + 1,)), pltpu.SemaphoreType.DMA((N_CHIPS + 1,))],
        compiler_params=pltpu.CompilerParams(vmem_limit_bytes=VMEM_LIMIT),
    )(x)


def _all_gather8(x, *, name, with_sum=False):
    r, lanes = x.shape

    def body(x_ref, out_ref, *rest):
        if with_sum:
            sum_ref, send_sems, recv_sems, local_sem = rest
        else:
            send_sems, recv_sems, local_sem = rest
        mx, my, mc = _my_coords()
        me = 4 * mx + 2 * my + mc

        def peer(k):
            return (mx ^ ((k >> 2) & 1), my ^ ((k >> 1) & 1), mc ^ (k & 1))

        def copy(k, slot):
            return pltpu.make_async_remote_copy(
                src_ref=x_ref, dst_ref=out_ref.at[slot], send_sem=send_sems.at[k - 1], recv_sem=recv_sems.at[k - 1],
                device_id=peer(k), device_id_type=MESH)

        mine = pltpu.make_async_copy(x_ref, out_ref.at[me], local_sem)
        mine.start()
        sends = [copy(k, me) for k in range(1, N_DEV)]
        for cp in sends:
            cp.start()
        for k in range(1, N_DEV):
            copy(k, me ^ k).wait_recv()
        for cp in sends:
            cp.wait_send()
        mine.wait()
        if with_sum:
            acc = out_ref[0]
            for k in range(1, N_DEV):
                acc = acc + out_ref[k]
            sum_ref[...] = acc

    vmem = pl.BlockSpec(memory_space=pltpu.VMEM)
    out_shape = [jax.ShapeDtypeStruct((N_DEV, r, lanes), F32)]
    if with_sum:
        out_shape.append(jax.ShapeDtypeStruct((r, lanes), F32))
    res = pl.pallas_call(
        body, name=name, in_specs=[vmem], out_specs=[vmem] * len(out_shape), out_shape=out_shape,
        scratch_shapes=[pltpu.SemaphoreType.DMA((N_DEV - 1,)), pltpu.SemaphoreType.DMA((N_DEV - 1,)),
                        pltpu.SemaphoreType.DMA],
        compiler_params=pltpu.CompilerParams(vmem_limit_bytes=VMEM_LIMIT),
    )(x)
    return res if with_sum else res[0]


def _mod_matmul(x, sc, sh, w4, b, *, ts, tn, name, comm=None):
    s, d = x.shape
    wd = w4.shape[2]
    n = N_CHIPS * wd
    per = wd // tn
    ts = _tile(ts, s)

    def body(x_ref, sc_ref, sh_ref, w_ref, b_ref, o_ref, ht_ref, h_scr):
        @pl.when(pl.program_id(1) == 0)
        def _():
            h = x_ref[...] * (1.0 + sc_ref[...]) + sh_ref[...]
            h_scr[...] = h.astype(BF16)
            ht_ref[...] = h.T.astype(BF16)
        o_ref[...] = (_dot(h_scr[...], w_ref[0]) + b_ref[...]).astype(BF16)

    return _run(
        body, name=name, grid=(s // ts, n // tn),
        in_specs=[pl.BlockSpec((ts, d), lambda i, j: (i, 0)), _const((1, d)), _const((1, d)),
                  pl.BlockSpec((1, d, tn), lambda i, j: (j // per, 0, j % per)),
                  pl.BlockSpec((1, tn), lambda i, j: (0, j))],
        out_specs=[pl.BlockSpec((ts, tn), lambda i, j: (i, j)), pl.BlockSpec((d, ts), lambda i, j: (0, i))],
        out_shape=[jax.ShapeDtypeStruct((s, n), BF16), jax.ShapeDtypeStruct((d, s), BF16)],
        scratch_shapes=[pltpu.VMEM((ts, d), BF16)],
        args=(x, sc, sh, w4, b), comm=comm)


def _conv3(q, ext, cw):
    return cw[2:3] * q + cw[1:2] * _rows_before(ext, 1, CONV_HALO) + cw[0:1] * _rows_before(ext, 2, CONV_HALO)


def _mix_a_fwd(z, cw, w_out, *, ts, name):
    s = z.shape[0]
    d = D_MODEL
    ts = _tile(ts, s)

    def body(zb_ref, zc_ref, zx_ref, cw_ref, w_ref, a_ref, y_ref, carry):
        @pl.when(pl.program_id(0) == 0)
        def _():
            carry[...] = jnp.zeros_like(carry)
        q = zc_ref[...].astype(F32) * zx_ref[...].astype(F32)
        ext = jnp.concatenate([carry[...], q], axis=0)
        a = (zb_ref[...].astype(F32) * _conv3(q, ext, cw_ref[...])).astype(BF16)
        carry[...] = q[ts - CONV_HALO:]
        a_ref[...] = a
        y_ref[...] = _dot(a, w_ref[...])

    zspec = lambda k: pl.BlockSpec((ts, d), lambda i, k=k: (i, k))
    return _run(
        body, name=name, grid=(s // ts,),
        in_specs=[zspec(0), zspec(1), zspec(2), _const((3, d)), _const((d, d))],
        out_specs=[pl.BlockSpec((ts, d), lambda i: (i, 0))] * 2,
        out_shape=[jax.ShapeDtypeStruct((s, d), BF16), jax.ShapeDtypeStruct((s, d), F32)],
        scratch_shapes=[pltpu.VMEM((CONV_HALO, d), F32)],
        args=(z, z, z, cw, w_out))


def _spatial_mix(vn_b, ws_ref, bst_ref, mixed_scr, ts):
    nblk = ts // GBLK
    mask = _spatial_mask()
    for g in range(N_GROUPS):
        cols = slice(g * GBLK, (g + 1) * GBLK)
        wm = jnp.where(mask, ws_ref[g], 0.0).astype(BF16)
        cat = jnp.concatenate([vn_b[n * GBLK:(n + 1) * GBLK, cols] for n in range(nblk)], axis=1)
        res = _dot(wm, cat) + bst_ref[:, g:g + 1]
        for n in range(nblk):
            mixed_scr[n * GBLK:(n + 1) * GBLK, cols] = res[:, n * GBLK:(n + 1) * GBLK]


def _mix_b_fwd(z, ln_g, ln_b, ws, bst, w_out, *, ts, name, comm=None):
    s = z.shape[0]
    d = D_MODEL
    ts = _tile(ts, s)

    def body(zu_ref, zv_ref, g_ref, b_ref, ws_ref, bst_ref, w_ref, sg_ref, y_ref, mixed_scr):
        xhat, _ = _ln_fwd(_gelu(zv_ref[...].astype(F32)))
        vn = (xhat * g_ref[...] + b_ref[...]).astype(BF16)
        _spatial_mix(vn, ws_ref, bst_ref, mixed_scr, ts)
        sg = (_gelu(zu_ref[...].astype(F32)) * mixed_scr[...]).astype(BF16)
        sg_ref[...] = sg
        y_ref[...] = _dot(sg, w_ref[...])

    zspec = lambda k: pl.BlockSpec((ts, d), lambda i, k=k: (i, k))
    return _run(
        body, name=name, grid=(s // ts,),
        in_specs=[zspec(3), zspec(4), _const((1, d)), _const((1, d)), _const((N_GROUPS, GBLK, GBLK)),
                  _const((GBLK, N_GROUPS)), _const((d, d))],
        out_specs=[pl.BlockSpec((ts, d), lambda i: (i, 0))] * 2,
        out_shape=[jax.ShapeDtypeStruct((s, d), BF16), jax.ShapeDtypeStruct((s, d), F32)],
        scratch_shapes=[pltpu.VMEM((ts, d), F32)],
        args=(z, z, ln_g, ln_b, ws, bst, w_out), comm=comm)


def _pool_denoms(tile_idx, ts):
    t1 = (tile_idx * ts + 1 + lax.broadcasted_iota(jnp.int32, (ts, 1), 0)).astype(F32)
    return [jnp.minimum(t1, float(w)) for w in POOL_WINDOWS]


def _pool_diff(p, ext, denoms, k):
    cols = slice(k * POOL_GROUP, (k + 1) * POOL_GROUP)
    acc = ext[:, cols]
    step = 1
    while step < POOL_WINDOWS[k]:
        acc = acc + pltpu.roll(acc, step, 0)
        step *= 2
    return acc[POOL_HALO:] / denoms[k] - p[:, cols]


def _mix_c_fwd(z, w_pool, scale, *, ts, name):
    s = z.shape[0]
    d = D_MODEL
    ts = _tile(ts, s)

    def body(zp_ref, w_ref, sc_ref, d_ref, y_ref, carry):
        i = pl.program_id(0)

        @pl.when(i == 0)
        def _():
            carry[...] = jnp.zeros_like(carry)
        p = zp_ref[...].astype(F32)
        ext = jnp.concatenate([carry[...], p], axis=0)
        carry[...] = p[ts - POOL_HALO:]
        denoms = _pool_denoms(i, ts)
        for k in range(len(POOL_WINDOWS)):
            cols = slice(k * POOL_GROUP, (k + 1) * POOL_GROUP)
            dk = _pool_diff(p, ext, denoms, k).astype(BF16)
            d_ref[:, cols] = dk
            y_ref[:, cols] = _dot(dk, w_ref[k]) * sc_ref[:, cols]

    return _run(
        body, name=name, grid=(s // ts,),
        in_specs=[pl.BlockSpec((ts, d), lambda i: (i, 5)), _const((4, POOL_GROUP, POOL_GROUP)), _const((1, d))],
        out_specs=[pl.BlockSpec((ts, d), lambda i: (i, 0))] * 2,
        out_shape=[jax.ShapeDtypeStruct((s, d), BF16), jax.ShapeDtypeStruct((s, d), F32)],
        scratch_shapes=[pltpu.VMEM((POOL_HALO, d), F32)],
        args=(z, w_pool, scale))


def _mix_o_fwd(x, z, ya, yb, yc, w_o, gt, ln_g, ln_b, *, ts, name, comm=None):
    s, d = x.shape
    ts = _tile(ts, s)

    def body(x_ref, ga_ref, gb_ref, gc_ref, ya_ref, yb_ref, yc_ref, w_ref, gt_ref, g_ref, b_ref,
             m_ref, o_ref, x1_ref):
        merged = (_sigmoid(ga_ref[...].astype(F32)) * ya_ref[...] + _sigmoid(gb_ref[...].astype(F32)) * yb_ref[...]
                  + _sigmoid(gc_ref[...].astype(F32)) * yc_ref[...]).astype(BF16)
        m_ref[...] = merged
        o = _dot(merged, w_ref[...])
        o_ref[...] = o
        xhat, _ = _ln_fwd(ALPHA * x_ref[...] + gt_ref[...] * o)
        x1_ref[...] = xhat * g_ref[...] + b_ref[...]

    row = pl.BlockSpec((ts, d), lambda i: (i, 0))
    zspec = lambda k: pl.BlockSpec((ts, d), lambda i, k=k: (i, k))
    return _run(
        body, name=name, grid=(s // ts,),
        in_specs=[row, zspec(6), zspec(7), zspec(8), row, row, row, _const((d, d)),
                  _const((1, d)), _const((1, d)), _const((1, d))],
        out_specs=[row] * 3,
        out_shape=[jax.ShapeDtypeStruct((s, d), BF16), jax.ShapeDtypeStruct((s, d), F32),
                   jax.ShapeDtypeStruct((s, d), F32)],
        args=(x, z, z, z, ya, yb, yc, w_o, gt, ln_g, ln_b), comm=comm)


def _ffn_fwd(up, x1, cw, cb, w_down, gt, ln_g, ln_b, *, ts, name, comm=None):
    s, d = x1.shape
    ts = _tile(ts, s)

    def body(up_ref, x1_ref, cw_ref, cb_ref, w_ref, gt_ref, g_ref, b_ref, ft_ref, dn_ref, x2_ref, carry, f_ref):
        @pl.when(pl.program_id(0) == 0)
        def _():
            carry[...] = jnp.zeros_like(carry)
        for c in range(D_FF // FF_CHUNK):
            ca = slice(c * FF_CHUNK, (c + 1) * FF_CHUNK)
            cg = slice(D_FF + c * FF_CHUNK, D_FF + (c + 1) * FF_CHUNK)
            ua = up_ref[:, ca].astype(F32)
            ext = jnp.concatenate([carry[:, ca], ua], axis=0)
            carry[:, ca] = ua[ts - CONV_HALO:]
            cf = _conv3(ua, ext, cw_ref[:, ca]) + cb_ref[:, ca]
            f = _gelu(cf) * up_ref[:, cg].astype(F32)
            f_ref[:, ca] = f.astype(BF16)
            ft_ref[ca, :] = f.T.astype(BF16)
        dn = _dot(f_ref[...], w_ref[...])
        dn_ref[...] = dn
        xhat, _ = _ln_fwd(ALPHA * x1_ref[...] + gt_ref[...] * dn)
        x2_ref[...] = xhat * g_ref[...] + b_ref[...]

    row = pl.BlockSpec((ts, d), lambda i: (i, 0))
    return _run(
        body, name=name, grid=(s // ts,),
        in_specs=[pl.BlockSpec((ts, 2 * D_FF), lambda i: (i, 0)), row, _const((3, D_FF)), _const((1, D_FF)),
                  _resident((D_FF, d)), _const((1, d)), _const((1, d)), _const((1, d))],
        out_specs=[pl.BlockSpec((D_FF, ts), lambda i: (0, i)), row, row],
        out_shape=[jax.ShapeDtypeStruct((D_FF, s), BF16), jax.ShapeDtypeStruct((s, d), F32),
                   jax.ShapeDtypeStruct((s, d), F32)],
        scratch_shapes=[pltpu.VMEM((CONV_HALO, D_FF), F32), pltpu.VMEM((ts, D_FF), BF16)],
        args=(up, x1, cw, cb, w_down, gt, ln_g, ln_b), comm=comm)


def _loss_fwd(y, tgt, *, ts, name):
    s, d = y.shape
    ts = _tile(ts, s)

    def body(y_ref, t_ref, dy_ref, l_ref):
        @pl.when(pl.program_id(0) == 0)
        def _():
            l_ref[...] = jnp.zeros_like(l_ref)
        e = y_ref[...] - t_ref[...]
        dy_ref[...] = e / float(d)
        l_ref[...] += 0.5 * jnp.sum(jnp.mean(e * e, axis=-1, keepdims=True), axis=0, keepdims=True)

    row = pl.BlockSpec((ts, d), lambda i: (i, 0))
    return _run(body, name=name, grid=(s // ts,), in_specs=[row, row], out_specs=[row, _const((8, 128))],
                out_shape=[jax.ShapeDtypeStruct((s, d), F32), jax.ShapeDtypeStruct((8, 128), F32)], args=(y, tgt))


def _rev(n_tiles):
    return lambda i: n_tiles - 1 - i


def _halo_spec(ts, n_tiles, halo, width, col):
    per = ts // halo
    return pl.BlockSpec((halo, width), lambda i: (jnp.maximum((n_tiles - 1 - i) * per - 1, 0), col))


def _ffn_bwd(dx2, x1, dn, up, cw, cb, w_down, w_up4, gt, ln_g, sc, *, ts, name, comm=None):
    s, d = x1.shape
    ts = _tile(ts, s)
    nt = s // ts
    rev = _rev(nt)
    wd = w_up4.shape[2]

    def w_up_cols(wu_ref, start):
        return wu_ref[start // wd, :, start % wd:start % wd + FF_CHUNK]

    def body(dx2_ref, x1_ref, dn_ref, up_ref, halo_ref, cw_ref, cb_ref, wd_ref, wu_ref, gt_ref, g_ref, sc_ref,
             ddn_ref, dup_ref, dx1_ref, redd_ref, redf_ref, dbup_ref, carry):
        i = pl.program_id(0)

        @pl.when(i == 0)
        def _():
            carry[...] = jnp.zeros_like(carry)
            redd_ref[...] = jnp.zeros_like(redd_ref)
            redf_ref[...] = jnp.zeros_like(redf_ref)
            dbup_ref[...] = jnp.zeros_like(dbup_ref)
        first_tile = i == nt - 1
        x1v, dnv, dyv = x1_ref[...], dn_ref[...], dx2_ref[...]
        xhat, rstd = _ln_fwd(ALPHA * x1v + gt_ref[...] * dnv)
        dr = _ln_bwd(dyv, g_ref[...], xhat, rstd)
        redd_ref[0:1, :] += _colsum(dyv * xhat)
        redd_ref[1:2, :] += _colsum(dyv)
        redd_ref[2:3, :] += _colsum(dr * dnv)
        ddn = (gt_ref[...] * dr).astype(BF16)
        ddn_ref[...] = ddn
        dh = jnp.zeros((ts, d), F32)
        for c in range(D_FF // FF_CHUNK):
            ca = slice(c * FF_CHUNK, (c + 1) * FF_CHUNK)
            cg = slice(D_FF + c * FF_CHUNK, D_FF + (c + 1) * FF_CHUNK)
            df = _dot_nt(ddn, wd_ref[ca, :])
            ua, ug = up_ref[:, ca].astype(F32), up_ref[:, cg].astype(F32)
            halo = jnp.where(first_tile, 0.0, halo_ref[:, ca].astype(F32)[HALO_ROWS - CONV_HALO:])
            ext = jnp.concatenate([halo, ua], axis=0)
            u1, u2 = _rows_before(ext, 1, CONV_HALO), _rows_before(ext, 2, CONV_HALO)
            cwc = cw_ref[:, ca]
            gl, dgl = _gelu_and_grad(cwc[2:3] * ua + cwc[1:2] * u1 + cwc[0:1] * u2 + cb_ref[:, ca])
            dug = df * gl
            dcf = df * ug * dgl
            redf_ref[0:1, ca] += _colsum(dcf * u2)
            redf_ref[1:2, ca] += _colsum(dcf * u1)
            redf_ref[2:3, ca] += _colsum(dcf * ua)
            redf_ref[3:4, ca] += _colsum(dcf)
            extd = jnp.concatenate([dcf, carry[:, ca]], axis=0)
            carry[:, ca] = dcf[:CONV_HALO]
            dua = cwc[2:3] * dcf + cwc[1:2] * _rows_after(extd, 1, ts) + cwc[0:1] * _rows_after(extd, 2, ts)
            dbup_ref[0:1, ca] += _colsum(dua)
            dbup_ref[0:1, cg] += _colsum(dug)
            dua_b, dug_b = dua.astype(BF16), dug.astype(BF16)
            dup_ref[:, ca] = dua_b
            dup_ref[:, cg] = dug_b
            dh = dh + _dot_nt(dua_b, w_up_cols(wu_ref, c * FF_CHUNK)) + _dot_nt(dug_b, w_up_cols(wu_ref, D_FF + c * FF_CHUNK))
        dx1_ref[...] = ALPHA * dr + dh * (1.0 + sc_ref[...])
        redd_ref[3:4, :] += _colsum(dh * x1v)
        redd_ref[4:5, :] += _colsum(dh)

    row = pl.BlockSpec((ts, d), lambda i: (rev(i), 0))
    return _run(
        body, name=name, grid=(nt,),
        in_specs=[row, row, row, pl.BlockSpec((ts, 2 * D_FF), lambda i: (rev(i), 0)),
                  _halo_spec(ts, nt, HALO_ROWS, D_FF, 0), _const((3, D_FF)), _const((1, D_FF)),
                  _resident((D_FF, d)), _resident((N_CHIPS, d, wd)), _const((1, d)), _const((1, d)), _const((1, d))],
        out_specs=[row, pl.BlockSpec((ts, 2 * D_FF), lambda i: (rev(i), 0)), row,
                   _const((8, d)), _const((8, D_FF)), _const((8, 2 * D_FF))],
        out_shape=[jax.ShapeDtypeStruct((s, d), BF16), jax.ShapeDtypeStruct((s, 2 * D_FF), BF16),
                   jax.ShapeDtypeStruct((s, d), F32), jax.ShapeDtypeStruct((8, d), F32),
                   jax.ShapeDtypeStruct((8, D_FF), F32), jax.ShapeDtypeStruct((8, 2 * D_FF), F32)],
        scratch_shapes=[pltpu.VMEM((CONV_HALO, D_FF), F32)],
        args=(dx2, x1, dn, up, up, cw, cb, w_down, w_up4, gt, ln_g, sc), comm=comm)


def _accumulate_dw(dw_ref, dwb_ref, xa, dy, first, last):
    @pl.when(first)
    def _():
        dw_ref[...] = jnp.zeros_like(dw_ref)
    dw_ref[...] += _dot_tn(xa, dy)

    @pl.when(last)
    def _():
        dwb_ref[...] = dw_ref[...].astype(BF16)


def _dw_out(d):
    return [_const((d, d))] * 2, [jax.ShapeDtypeStruct((d, d), F32), jax.ShapeDtypeStruct((d, d), BF16)]


def _grad_matmul_t(xt, dy, *, tk, tn, name, by_chip=False, comm=None):
    k, s = xt.shape
    n = dy.shape[1]

    def body(xt_ref, dy_ref, o_ref, ob_ref):
        o = _dot(xt_ref[...], dy_ref[...]).reshape(o_ref.shape)
        o_ref[...] = o
        ob_ref[...] = o.astype(BF16)

    if by_chip:
        assert tk == k
        per = n // N_CHIPS // tn
        ospec = pl.BlockSpec((1, k, tn), lambda j, i: (j // per, 0, j % per))
        shape = (N_CHIPS, k, n // N_CHIPS)
    else:
        ospec = pl.BlockSpec((tk, tn), lambda j, i: (i, j))
        shape = (k, n)
    xspec = _resident((k, s)) if tk == k else pl.BlockSpec((tk, s), lambda j, i: (i, 0))
    dspec = _resident((s, n)) if tn == n else pl.BlockSpec((s, tn), lambda j, i: (0, j))
    return _run(body, name=name, grid=(n // tn, k // tk), in_specs=[xspec, dspec], out_specs=[ospec, ospec],
                out_shape=[jax.ShapeDtypeStruct(shape, F32), jax.ShapeDtypeStruct(shape, BF16)], args=(xt, dy), comm=comm)


def _mix_o_bwd(dx1, x, o, z, ya, yb, yc, merged, w_o, gt, ln_g, *, ts, name, comm=None):
    s, d = x.shape
    ts = _tile(ts, s)
    nt = s // ts

    def body(dx1_ref, x_ref, o_ref, ga_ref, gb_ref, gc_ref, ya_ref, yb_ref, yc_ref, m_ref, w_ref, gt_ref, g_ref,
             dxa_ref, dzg_ref, dya_ref, dyb_ref, dyc_ref, red_ref, dw_ref, dwb_ref):
        i = pl.program_id(0)

        @pl.when(i == 0)
        def _():
            red_ref[...] = jnp.zeros_like(red_ref)
        dyv, ov = dx1_ref[...], o_ref[...]
        xhat, rstd = _ln_fwd(ALPHA * x_ref[...] + gt_ref[...] * ov)
        dr = _ln_bwd(dyv, g_ref[...], xhat, rstd)
        red_ref[0:1, :] += _colsum(dyv * xhat)
        red_ref[1:2, :] += _colsum(dyv)
        red_ref[2:3, :] += _colsum(dr * ov)
        dxa_ref[...] = ALPHA * dr
        d_o = (gt_ref[...] * dr).astype(BF16)
        _accumulate_dw(dw_ref, dwb_ref, m_ref[...], d_o, i == 0, i == nt - 1)
        dm = _dot_nt(d_o, w_ref[...])
        for k, (zg_ref, y_ref, dy_ref) in enumerate(((ga_ref, ya_ref, dya_ref), (gb_ref, yb_ref, dyb_ref),
                                                     (gc_ref, yc_ref, dyc_ref))):
            g = _sigmoid(zg_ref[...].astype(F32))
            dzg_ref[:, k * d:(k + 1) * d] = (dm * y_ref[...] * g * (1.0 - g)).astype(BF16)
            dy_ref[...] = (dm * g).astype(BF16)

    row = pl.BlockSpec((ts, d), lambda i: (i, 0))
    zspec = lambda k: pl.BlockSpec((ts, d), lambda i, k=k: (i, k))
    bf = jax.ShapeDtypeStruct((s, d), BF16)
    dw_specs, dw_shapes = _dw_out(d)
    return _run(
        body, name=name, grid=(nt,),
        in_specs=[row, row, row, zspec(6), zspec(7), zspec(8), row, row, row, row, _const((d, d)),
                  _const((1, d)), _const((1, d))],
        out_specs=[row, pl.BlockSpec((ts, 3 * d), lambda i: (i, 2)), row, row, row, _const((8, d))] + dw_specs,
        out_shape=[jax.ShapeDtypeStruct((s, d), F32), jax.ShapeDtypeStruct((s, D_Z), BF16), bf, bf, bf,
                   jax.ShapeDtypeStruct((8, d), F32)] + dw_shapes,
        args=(dx1, x, o, z, z, z, ya, yb, yc, merged, w_o, gt, ln_g), comm=comm)


def _mix_a_bwd(dya, a, z, dz, cw, w_out, *, ts, name, comm=None):
    s = z.shape[0]
    d = D_MODEL
    ts = _tile(ts, s)
    nt = s // ts
    rev = _rev(nt)

    def body(dya_ref, a_ref, zb_ref, zc_ref, zx_ref, hc_ref, hx_ref, cw_ref, w_ref, dz_in, dz_ref, red_ref,
             dw_ref, dwb_ref, carry):
        i = pl.program_id(0)

        @pl.when(i == 0)
        def _():
            carry[...] = jnp.zeros_like(carry)
            red_ref[...] = jnp.zeros_like(red_ref)
        _accumulate_dw(dw_ref, dwb_ref, a_ref[...], dya_ref[...], i == 0, i == nt - 1)
        zb, zc, zx = zb_ref[...].astype(F32), zc_ref[...].astype(F32), zx_ref[...].astype(F32)
        q = zc * zx
        halo = jnp.where(i == nt - 1, 0.0, (hc_ref[...].astype(F32) * hx_ref[...].astype(F32))[HALO_ROWS - CONV_HALO:])
        ext = jnp.concatenate([halo, q], axis=0)
        q1, q2 = _rows_before(ext, 1, CONV_HALO), _rows_before(ext, 2, CONV_HALO)
        cwv = cw_ref[...]
        cv = cwv[2:3] * q + cwv[1:2] * q1 + cwv[0:1] * q2
        da = _dot_nt(dya_ref[...], w_ref[...])
        dcv = da * zb
        red_ref[0:1, :] += _colsum(dcv * q2)
        red_ref[1:2, :] += _colsum(dcv * q1)
        red_ref[2:3, :] += _colsum(dcv * q)
        extd = jnp.concatenate([dcv, carry[...]], axis=0)
        carry[...] = dcv[:CONV_HALO]
        dq = cwv[2:3] * dcv + cwv[1:2] * _rows_after(extd, 1, ts) + cwv[0:1] * _rows_after(extd, 2, ts)
        dz_ref[:, 0:d] = (da * cv).astype(BF16)
        dz_ref[:, d:2 * d] = (dq * zx).astype(BF16)
        dz_ref[:, 2 * d:3 * d] = (dq * zc).astype(BF16)

    zspec = lambda k: pl.BlockSpec((ts, d), lambda i, k=k: (rev(i), k))
    row = pl.BlockSpec((ts, d), lambda i: (rev(i), 0))
    dw_specs, dw_shapes = _dw_out(d)
    return _run(
        body, name=name, grid=(nt,),
        in_specs=[row, row, zspec(0), zspec(1), zspec(2),
                  _halo_spec(ts, nt, HALO_ROWS, d, 1), _halo_spec(ts, nt, HALO_ROWS, d, 2),
                  _const((3, d)), _const((d, d)), HBM],
        out_specs=[pl.BlockSpec((ts, 3 * d), lambda i: (rev(i), 0)), _const((8, d))] + dw_specs,
        out_shape=[jax.ShapeDtypeStruct((s, D_Z), BF16), jax.ShapeDtypeStruct((8, d), F32)] + dw_shapes,
        scratch_shapes=[pltpu.VMEM((CONV_HALO, d), F32)],
        args=(dya, a, z, z, z, z, z, cw, w_out, dz), aliases={9: 0}, comm=comm)


def _mix_b_bwd(dyb, sg, z, dz, ln_g, ln_b, ws, bst, w_out, *, ts, name, comm=None):
    s = z.shape[0]
    d = D_MODEL
    ts = _tile(ts, s)
    nt = s // ts
    nblk = ts // GBLK

    def body(dyb_ref, sg_ref, zu_ref, zv_ref, g_ref, b_ref, ws_ref, bst_ref, w_ref, dz_in,
             dz_ref, red_ref, dws_ref, dbst_ref, dw_ref, dwb_ref, mixed_scr, dvn_scr, dzv_scr):
        first = (pl.program_id(0) == 0) & (pl.program_id(1) == 0)

        @pl.when(first)
        def _():
            red_ref[...] = jnp.zeros_like(red_ref)
            dws_ref[...] = jnp.zeros_like(dws_ref)
            dbst_ref[...] = jnp.zeros_like(dbst_ref)

        @pl.when(pl.program_id(1) == 0)
        def _():
            _accumulate_dw(dw_ref, dwb_ref, sg_ref[...], dyb_ref[...], first, pl.program_id(0) == nt - 1)
            u, du_dz = _gelu_and_grad(zu_ref[...].astype(F32))
            vg, dv_dz = _gelu_and_grad(zv_ref[...].astype(F32))
            xhat, rstd = _ln_fwd(vg)
            vn = (xhat * g_ref[...] + b_ref[...]).astype(BF16)
            _spatial_mix(vn, ws_ref, bst_ref, mixed_scr, ts)
            dsg = _dot_nt(dyb_ref[...], w_ref[...])
            dz_ref[...] = (dsg * mixed_scr[...] * du_dz).astype(BF16)
            dmix = dsg * u
            mask = _spatial_mask()
            for g in range(N_GROUPS):
                cols = slice(g * GBLK, (g + 1) * GBLK)
                wm = jnp.where(mask, ws_ref[g], 0.0).astype(BF16)
                dm_cat = jnp.concatenate([dmix[n * GBLK:(n + 1) * GBLK, cols] for n in range(nblk)], axis=1)
                vn_cat = jnp.concatenate([vn[n * GBLK:(n + 1) * GBLK, cols] for n in range(nblk)], axis=1)
                dm_b = dm_cat.astype(BF16)
                dbst_ref[:, g:g + 1] += jnp.sum(dm_cat, axis=1, keepdims=True)
                dws_ref[g] += jnp.where(mask, _dot_nt(dm_b, vn_cat), 0.0)
                dvn_cat = _dot_tn(wm, dm_b)
                for n in range(nblk):
                    dvn_scr[n * GBLK:(n + 1) * GBLK, cols] = dvn_cat[:, n * GBLK:(n + 1) * GBLK]
            dvn = dvn_scr[...]
            red_ref[0:1, :] += _colsum(dvn * xhat)
            red_ref[1:2, :] += _colsum(dvn)
            dzv_scr[...] = (_ln_bwd(dvn, g_ref[...], xhat, rstd) * dv_dz).astype(BF16)

        @pl.when(pl.program_id(1) == 1)
        def _():
            dz_ref[...] = dzv_scr[...]

    zspec = lambda k: pl.BlockSpec((ts, d), lambda i, h, k=k: (i, k))
    row = pl.BlockSpec((ts, d), lambda i, h: (i, 0))
    dw_specs, dw_shapes = _dw_out(d)
    return _run(
        body, name=name, grid=(nt, 2),
        in_specs=[row, row, zspec(3), zspec(4), _const((1, d)), _const((1, d)),
                  _const((N_GROUPS, GBLK, GBLK)), _const((GBLK, N_GROUPS)), _const((d, d)), HBM],
        out_specs=[pl.BlockSpec((ts, d), lambda i, h: (i, 3 + h)), _const((8, d)),
                   _const((N_GROUPS, GBLK, GBLK)), _const((GBLK, N_GROUPS))] + dw_specs,
        out_shape=[jax.ShapeDtypeStruct((s, D_Z), BF16), jax.ShapeDtypeStruct((8, d), F32),
                   jax.ShapeDtypeStruct((N_GROUPS, GBLK, GBLK), F32), jax.ShapeDtypeStruct((GBLK, N_GROUPS), F32)]
        + dw_shapes,
        scratch_shapes=[pltpu.VMEM((ts, d), F32), pltpu.VMEM((ts, d), F32), pltpu.VMEM((ts, d), BF16)],
        args=(dyb, sg, z, z, ln_g, ln_b, ws, bst, w_out, dz), aliases={9: 0}, comm=comm)


def _mix_c_bwd(dyc, z, dz, w_pool, scale, *, ts, name):
    s = z.shape[0]
    d = D_MODEL
    ts = _tile(ts, s)
    nt = s // ts
    rev = _rev(nt)

    def body(dyc_ref, zp_ref, halo_ref, w_ref, sc_ref, dz_in, dz_ref, red_ref, dw_ref, carry):
        i = pl.program_id(0)

        @pl.when(i == 0)
        def _():
            carry[...] = jnp.zeros_like(carry)
            red_ref[...] = jnp.zeros_like(red_ref)
            dw_ref[...] = jnp.zeros_like(dw_ref)
        p = zp_ref[...].astype(F32)
        ext = jnp.concatenate([jnp.where(i == nt - 1, 0.0, halo_ref[...].astype(F32)), p], axis=0)
        denoms = _pool_denoms(rev(i), ts)
        dyv = dyc_ref[...].astype(F32)
        for k in range(len(POOL_WINDOWS)):
            cols = slice(k * POOL_GROUP, (k + 1) * POOL_GROUP)
            dk = _pool_diff(p, ext, denoms, k).astype(BF16)
            red_ref[0:1, cols] += _colsum(dyv[:, cols] * _dot(dk, w_ref[k]))
            dpre = (dyv[:, cols] * sc_ref[:, cols]).astype(BF16)
            dw_ref[k] += _dot_tn(dk, dpre)
            dd = _dot_nt(dpre, w_ref[k])
            e = dd / denoms[k]
            acc = jnp.concatenate([e, carry[:, cols]], axis=0)
            carry[:, cols] = e[:POOL_HALO]
            step = 1
            while step < POOL_WINDOWS[k]:
                acc = acc + pltpu.roll(acc, acc.shape[0] - step, 0)
                step *= 2
            dz_ref[:, cols] = (acc[:ts] - dd).astype(BF16)

    return _run(
        body, name=name, grid=(nt,),
        in_specs=[pl.BlockSpec((ts, d), lambda i: (rev(i), 0)), pl.BlockSpec((ts, d), lambda i: (rev(i), 5)),
                  _halo_spec(ts, nt, POOL_HALO, d, 5), _const((4, POOL_GROUP, POOL_GROUP)), _const((1, d)), HBM],
        out_specs=[pl.BlockSpec((ts, d), lambda i: (rev(i), 5)), _const((8, d)), _const((4, POOL_GROUP, POOL_GROUP))],
        out_shape=[jax.ShapeDtypeStruct((s, D_Z), BF16), jax.ShapeDtypeStruct((8, d), F32),
                   jax.ShapeDtypeStruct((4, POOL_GROUP, POOL_GROUP), F32)],
        scratch_shapes=[pltpu.VMEM((POOL_HALO, d), F32)],
        args=(dyc, z, z, w_pool, scale, dz), aliases={5: 0})


def _in_proj_bwd(dz, w4, dxa, x, sc, *, ts, name, comm=None):
    s, d = x.shape
    ts = _tile(ts, s)
    wd = w4.shape[2]

    def body(dz_ref, w_ref, dxa_ref, x_ref, sc_ref, dx_ref, red_ref, db_ref):
        @pl.when(pl.program_id(0) == 0)
        def _():
            red_ref[...] = jnp.zeros_like(red_ref)
            db_ref[...] = jnp.zeros_like(db_ref)
        dh = jnp.zeros((ts, d), F32)
        for j in range(N_CHIPS):
            dzj = dz_ref[:, j * wd:(j + 1) * wd]
            db_ref[0:1, j * wd:(j + 1) * wd] += _colsum(dzj.astype(F32))
            dh = dh + _dot_nt(dzj, w_ref[j])
        dx_ref[...] = dxa_ref[...] + dh * (1.0 + sc_ref[...])
        red_ref[0:1, :] += _colsum(dh * x_ref[...])
        red_ref[1:2, :] += _colsum(dh)

    row = pl.BlockSpec((ts, d), lambda i: (i, 0))
    return _run(
        body, name=name, grid=(s // ts,),
        in_specs=[pl.BlockSpec((ts, D_Z), lambda i: (i, 0)), _resident((N_CHIPS, d, wd)), row, row, _const((1, d))],
        out_specs=[row, _const((8, d)), _const((8, D_Z))],
        out_shape=[jax.ShapeDtypeStruct((s, d), F32), jax.ShapeDtypeStruct((8, d), F32),
                   jax.ShapeDtypeStruct((8, D_Z), F32)],
        args=(dz, w4, dxa, x, sc), comm=comm)


def _ada_fwd(c_all, w_ada, b_ada, *, name):
    nl, d, n = w_ada.shape
    tn = n // 2

    def body(c_ref, w_ref, b_ref, o_ref):
        cv = c_ref[...]
        ca = (cv * _sigmoid(cv)).astype(BF16)
        o_ref[0] = _dot(ca, w_ref[0].astype(BF16)) + b_ref[0]

    return _run(
        body, name=name, grid=(nl, n // tn),
        in_specs=[_const((N_DEV, d)), pl.BlockSpec((1, d, tn), lambda l, j: (l, 0, j)),
                  pl.BlockSpec((1, 1, tn), lambda l, j: (l, 0, j))],
        out_specs=[pl.BlockSpec((1, N_DEV, tn), lambda l, j: (l, 0, j))],
        out_shape=[jax.ShapeDtypeStruct((nl, N_DEV, n), F32)], args=(c_all, w_ada, b_ada))[0]


def _ada_bwd(c_all, dada, *, name):
    nl, nb, n = dada.shape
    d = c_all.shape[1]
    tn = n // 2

    def body(c_ref, g_ref, o_ref):
        cv = c_ref[...]
        ca = (cv * _sigmoid(cv)).astype(BF16)
        o_ref[0] = _dot_tn(ca, g_ref[0].astype(BF16))

    return _run(
        body, name=name, grid=(nl, n // tn),
        in_specs=[_const((nb, d)), pl.BlockSpec((1, nb, tn), lambda l, j: (l, 0, j))],
        out_specs=[pl.BlockSpec((1, d, tn), lambda l, j: (l, 0, j))],
        out_shape=[jax.ShapeDtypeStruct((nl, d, n), F32)], args=(c_all, dada))[0]


def _sum4_into_half(own, recv, core, *, name):
    r, c = own.shape
    tr = _row_tile(r, c, 2)

    def body(core_ref, own_ref, recv_ref, o_ref):
        acc = own_ref[...]
        for k in range(N_CHIPS - 1):
            acc = acc + recv_ref[k].astype(F32)
        o_ref[0] = acc

    spec = pltpu.PrefetchScalarGridSpec(
        num_scalar_prefetch=1, grid=(r // tr,),
        in_specs=[pl.BlockSpec((tr, c), lambda i, core_ref: (i, 0)),
                  pl.BlockSpec((N_CHIPS - 1, tr, c), lambda i, core_ref: (0, i, 0))],
        out_specs=pl.BlockSpec((1, tr, c), lambda i, core_ref: (core_ref[0], i, 0)))
    return pl.pallas_call(
        body, name=name, grid_spec=spec, out_shape=jax.ShapeDtypeStruct((2, r, c), F32),
        compiler_params=pltpu.CompilerParams(dimension_semantics=("arbitrary",), vmem_limit_bytes=VMEM_LIMIT),
    )(core, own, recv)


def _cast_into_slots(shards, layer, chip, *, name):
    quarters = 4

    def body(chip_ref, *refs):
        ins, outs = refs[:len(shards)], refs[len(shards):]
        for i_ref, o_ref in zip(ins, outs):
            o_ref[0, 0] = i_ref[0].astype(BF16)

    in_specs, out_specs, out_shape = [], [], []
    for sh in shards:
        _, r, c = sh.shape
        in_specs.append(pl.BlockSpec((1, r // quarters, c), lambda t, chip_ref: (layer, t, 0)))
        out_specs.append(pl.BlockSpec((1, 1, r // quarters, c), lambda t, chip_ref: (chip_ref[0], t // 2, t % 2, 0)))
        out_shape.append(jax.ShapeDtypeStruct((N_CHIPS, 2, r // 2, c), BF16))
    spec = pltpu.PrefetchScalarGridSpec(num_scalar_prefetch=1, grid=(quarters,), in_specs=in_specs, out_specs=out_specs)
    return pl.pallas_call(
        body, name=name, grid_spec=spec, out_shape=out_shape,
        compiler_params=pltpu.CompilerParams(dimension_semantics=("arbitrary",), vmem_limit_bytes=VMEM_LIMIT),
    )(chip, *shards)


def _sum_halves(g_f32, theirs, place, *, name, comm=None):
    _, _, rh, c = g_f32.shape
    tr = _row_tile(rh, c, 2)

    def body(place_ref, g_ref, t_ref, hb_ref, own_ref):
        h = g_ref[0, 0] + t_ref[0].astype(F32)
        hb_ref[0] = h.astype(BF16)

        @pl.when(pl.program_id(1) == place_ref[1])
        def _():
            own_ref[...] = h

    return _run(
        body, name=name, grid=(rh // tr, N_CHIPS), prefetch=place,
        in_specs=[pl.BlockSpec((1, 1, tr, c), lambda i, j, place_ref: (j, place_ref[0], i, 0)),
                  pl.BlockSpec((1, tr, c), lambda i, j, place_ref: (j, i, 0))],
        out_specs=[pl.BlockSpec((1, tr, c), lambda i, j, place_ref: (j, i, 0)),
                   pl.BlockSpec((tr, c), lambda i, j, place_ref: (i, 0))],
        out_shape=[jax.ShapeDtypeStruct((N_CHIPS, rh, c), BF16), jax.ShapeDtypeStruct((rh, c), F32)],
        args=(g_f32, theirs), comm=comm)


def _row_tile(r, c, mib):
    limit = max(8, (mib << 20) // (4 * c))
    if r <= limit:
        return r
    best = 8
    for t in range(8, limit + 1, 8):
        if r % t == 0:
            best = t
    return best


def _adam_math(w, g, m, v):
    mn = ADAM_B1 * m + (1.0 - ADAM_B1) * g
    vn = ADAM_B2 * v + (1.0 - ADAM_B2) * (g * g)
    m_hat = mn / (1.0 - ADAM_B1 ** ADAM_STEP)
    v_hat = vn / (1.0 - ADAM_B2 ** ADAM_STEP)
    return -ADAM_LR * (m_hat / (jnp.sqrt(v_hat) + ADAM_EPS) + ADAM_WD * w), mn, vn


def _adamw(w, g, m, v, *, name):
    r, c = w.shape
    tr = _row_tile(r, c, 2)

    def body(w_ref, g_ref, m_ref, v_ref, d_ref, mo_ref, vo_ref):
        d_ref[...], mo_ref[...], vo_ref[...] = _adam_math(w_ref[...], g_ref[...], m_ref[...], v_ref[...])

    blk = pl.BlockSpec((tr, c), lambda i: (i, 0))
    return _run(body, name=name, grid=(r // tr,), in_specs=[blk] * 4, out_specs=[blk] * 3,
                out_shape=[jax.ShapeDtypeStruct((r, c), F32)] * 3, args=(w, g, m, v))


def _adamw_sharded(w, m, v, grads, *, name, comm=None):
    nl, r, c = w.shape
    tr = _row_tile(r, c, 1)
    nt = r // tr

    def body(w_ref, m_ref, v_ref, g0_ref, g1_ref, g_ref, d_ref, mo_ref, vo_ref):
        g = jnp.where(pl.program_id(0) == 0, g0_ref[...], g1_ref[...])
        g_ref[0] = g
        d_ref[0], mo_ref[0], vo_ref[0] = _adam_math(w_ref[0], g, m_ref[0], v_ref[0])

    blk = pl.BlockSpec((1, tr, c), lambda l, i: (l, i, 0))
    part0 = pl.BlockSpec((tr, c), lambda l, i: (jnp.where(l == 0, i, nt - 1), 0))
    part1 = pl.BlockSpec((tr, c), lambda l, i: (jnp.where(l == 1, i, 0), 0))
    return _run(body, name=name, grid=(nl, nt), in_specs=[blk] * 3 + [part0, part1],
                out_specs=[blk] * 4, out_shape=[jax.ShapeDtypeStruct((nl, r, c), F32)] * 4,
                args=(w, m, v, grads[0], grads[1]), comm=comm)


_BIG = ("w_in", "w_a_out", "w_b_out", "w_pool", "w_o", "w_up", "w_down")
_COL_SHARDED = ("w_in", "w_up")
_SMALL_SHARDED = ("conv_a", "conv_ffn")
_SMALL_REPL = ("b_in", "ln_v_g", "ln_v_b", "w_spatial", "b_spatial", "pool_scale", "ln1_g", "ln1_b", "b_up",
               "conv_ffn_b", "ln2_g", "ln2_b")
_WEIGHTS = ("w_ada", "b_ada", "w_in", "b_in", "conv_a", "w_a_out", "ln_v_g", "ln_v_b", "w_spatial", "b_spatial",
            "w_b_out", "w_pool", "pool_scale", "w_o", "ln1_g", "ln1_b", "w_up", "b_up", "conv_ffn", "conv_ffn_b",
            "w_down", "ln2_g", "ln2_b")


def _shard3(a):
    return a.reshape(a.shape[0], -1, a.shape[-1])


def _use_gathered(name, g):
    g = g.reshape(N_CHIPS, -1, g.shape[-1])
    if name in _COL_SHARDED:
        return g
    if name == "w_pool":
        return g.reshape(N_CHIPS, 4, POOL_GROUP // N_CHIPS, POOL_GROUP).transpose(1, 0, 2, 3).reshape(
            4, POOL_GROUP, POOL_GROUP)
    return g.reshape(-1, g.shape[-1])


def _grad_by_chip(name, g):
    if name in _COL_SHARDED:
        return g
    if name == "w_pool":
        return g.reshape(4, N_CHIPS, POOL_GROUP // N_CHIPS, POOL_GROUP).transpose(1, 0, 2, 3).reshape(
            N_CHIPS, POOL_GROUP, POOL_GROUP)
    return g.reshape(N_CHIPS, -1, g.shape[-1])


def _pack_small(arrs):
    parts = []
    for a in arrs:
        flat = a.reshape(-1).astype(F32)
        pad = (-flat.shape[0]) % 128
        parts.append(jnp.pad(flat, (0, pad)) if pad else flat)
    flat = jnp.concatenate(parts)
    pad = (-flat.shape[0]) % 2048
    if pad:
        flat = jnp.pad(flat, (0, pad))
    return flat.reshape(-1, 128)


def _unpack_small(buf, shapes):
    lead = buf.shape[:-2]
    flat = buf.reshape(lead + (-1,))
    out, off = [], 0
    for shp in shapes:
        n = math.prod(shp)
        out.append(flat[..., off:off + n].reshape(lead + tuple(shp)))
        off += n + ((-n) % 128)
    return out


def _as2d(a):
    return a.reshape(-1, a.shape[-1])


_LATE = ("w_a_out", "w_b_out", "w_pool", "w_o")


class _Traffic:
    def __init__(self, slots, plan, core, chip):
        self.slots = slots
        self.plan = plan
        self.core = core
        self.place = jnp.concatenate([core, chip])
        self.gathered = {}
        self.ready = {}
        self.summed = {}
        self.half = {}
        self.final = {}

    def weight(self, layer, name):
        return self.gathered[(layer, name)]

    def add_grad(self, layer, name, g_f32, g_bf16):
        def halves(g):
            g = _grad_by_chip(name, g)
            return g.reshape(N_CHIPS, 2, g.shape[1] // 2, g.shape[2])
        self.ready[(layer, name)] = (halves(g_f32), halves(g_bf16))

    def _comm(self, job):
        if job[0] == "gather":
            return _gather_comm([self.slots[(job[1], k)] for k in job[2]])
        if job[0] == "presum":
            return _presum_comm([self.ready[k][1] for k in job[1]])
        if job[0] == "scatter":
            return _scatter_comm([self.summed[k][0] for k in job[1]])
        return _join_comm([self.half[k] for k in job[1]])

    def _done(self, job, res):
        if job[0] == "gather":
            for k, r in zip(job[2], res):
                self.gathered[(job[1], k)] = _use_gathered(k, r)
        elif job[0] == "presum":
            for k, r in zip(job[1], res):
                nm = f"presum_l{k[0]}_{k[1]}"
                self.summed[k] = self.run(nm, lambda cm: _sum_halves(self.ready.pop(k)[0], r, self.place, name=nm,
                                                                     comm=cm))
        elif job[0] == "scatter":
            for k, r in zip(job[1], res):
                self.half[k] = _sum4_into_half(self.summed.pop(k)[1], r, self.core, name=f"sum_l{k[0]}_{k[1]}")
        else:
            for k, r in zip(job[1], res):
                self.final[k] = r.reshape(-1, r.shape[-1])

    def run(self, name, fn):
        jobs = self.plan.get(name)
        if not jobs:
            return fn(None)
        comms = [self._comm(j) for j in jobs]
        outs, res = fn(_merge(comms))
        for job, r in zip(jobs, _split(comms, res)):
            self._done(job, r)
        return outs

    def alone(self, name):
        jobs = self.plan[name]
        comms = [self._comm(j) for j in jobs]
        for job, r in zip(jobs, _split(comms, _comm_call(_merge(comms), name=name))):
            self._done(job, r)


def _layer_fwd(x, ada, p, l, tr):
    sh1, sc1, gt1, sh2, sc2, gt2 = ada
    n = f"l{l}"
    z, ht = tr.run(f"{n}_in_proj", lambda cm: _mod_matmul(
        x, sc1, sh1, tr.weight(l, "w_in"), p["b_in"], ts=1024, tn=2304, name=f"{n}_in_proj", comm=cm))
    a, ya = _mix_a_fwd(z, p["conv_a"], tr.weight(l, "w_a_out"), ts=256, name=f"{n}_mix_a")
    sg, yb = tr.run(f"{n}_mix_b", lambda cm: _mix_b_fwd(
        z, p["ln_v_g"], p["ln_v_b"], p["w_spatial"], p["b_spatial_t"], tr.weight(l, "w_b_out"), ts=256,
        name=f"{n}_mix_b", comm=cm))
    dpool, yc = _mix_c_fwd(z, tr.weight(l, "w_pool"), p["pool_scale"], ts=256, name=f"{n}_mix_c")
    merged, o, x1 = tr.run(f"{n}_mix_o", lambda cm: _mix_o_fwd(
        x, z, ya, yb, yc, tr.weight(l, "w_o"), gt1, p["ln1_g"], p["ln1_b"], ts=256, name=f"{n}_mix_o", comm=cm))
    up, h2t = tr.run(f"{n}_up_proj", lambda cm: _mod_matmul(
        x1, sc2, sh2, tr.weight(l, "w_up"), p["b_up"], ts=1024, tn=1408, name=f"{n}_up_proj", comm=cm))
    ft, dn, x2 = tr.run(f"{n}_ffn", lambda cm: _ffn_fwd(
        up, x1, p["conv_ffn"], p["conv_ffn_b"], tr.weight(l, "w_down"), gt2, p["ln2_g"], p["ln2_b"], ts=256,
        name=f"{n}_ffn", comm=cm))
    saved = dict(x=x, z=z, ht=ht, a=a, ya=ya, sg=sg, yb=yb, dpool=dpool, yc=yc, merged=merged, o=o, x1=x1, h2t=h2t,
                 up=up, ft=ft, dn=dn)
    return x2, saved


def _layer_bwd(dx2, ada, p, sv, l, tr):
    sh1, sc1, gt1, sh2, sc2, gt2 = ada
    n = f"l{l}"
    ddn, dup, dx1, red_d, red_f, dbup = tr.run(f"{n}_ffn_bwd", lambda cm: _ffn_bwd(
        dx2, sv["x1"], sv["dn"], sv["up"], p["conv_ffn"], p["conv_ffn_b"], tr.weight(l, "w_down"),
        tr.weight(l, "w_up"), gt2, p["ln2_g"], sc2, ts=256, name=f"{n}_ffn_bwd", comm=cm))
    g = {}
    tr.add_grad(l, "w_down", *_grad_matmul_t(sv["ft"], ddn, tk=D_FF // N_CHIPS, tn=D_MODEL, name=f"{n}_dw_down"))
    tr.add_grad(l, "w_up", *tr.run(f"{n}_dw_up", lambda cm: _grad_matmul_t(
        sv["h2t"], dup, tk=D_MODEL, tn=FF_CHUNK, name=f"{n}_dw_up", by_chip=True, comm=cm)))
    g["ln2_g"], g["ln2_b"] = red_d[0], red_d[1]
    g["conv_ffn"], g["conv_ffn_b"], g["b_up"] = red_f[0:3], red_f[3], dbup[0]

    dxa, dz, dya, dyb, dyc, red_o, dwo, dwo_b = tr.run(f"{n}_mix_o_bwd", lambda cm: _mix_o_bwd(
        dx1, sv["x"], sv["o"], sv["z"], sv["ya"], sv["yb"], sv["yc"], sv["merged"], tr.weight(l, "w_o"), gt1,
        p["ln1_g"], ts=256, name=f"{n}_mix_o_bwd", comm=cm))
    tr.add_grad(l, "w_o", dwo, dwo_b)
    g["ln1_g"], g["ln1_b"] = red_o[0], red_o[1]

    dz, red_a, dwa, dwa_b = tr.run(f"{n}_mix_a_bwd", lambda cm: _mix_a_bwd(
        dya, sv["a"], sv["z"], dz, p["conv_a"], tr.weight(l, "w_a_out"), ts=256, name=f"{n}_mix_a_bwd", comm=cm))
    tr.add_grad(l, "w_a_out", dwa, dwa_b)
    g["conv_a"] = red_a[0:3]

    dz, red_b, dws, dbst, dwb, dwb_b = tr.run(f"{n}_mix_b_bwd", lambda cm: _mix_b_bwd(
        dyb, sv["sg"], sv["z"], dz, p["ln_v_g"], p["ln_v_b"], p["w_spatial"], p["b_spatial_t"],
        tr.weight(l, "w_b_out"), ts=256, name=f"{n}_mix_b_bwd", comm=cm))
    tr.add_grad(l, "w_b_out", dwb, dwb_b)
    g["ln_v_g"], g["ln_v_b"], g["w_spatial"], g["b_spatial"] = red_b[0], red_b[1], dws, dbst.T

    dz, red_c, dwp = _mix_c_bwd(dyc, sv["z"], dz, tr.weight(l, "w_pool"), p["pool_scale"], ts=256,
                                name=f"{n}_mix_c_bwd")
    g["pool_scale"] = red_c[0]
    tr.add_grad(l, "w_pool", dwp, dwp.astype(BF16))

    tr.add_grad(l, "w_in", *tr.run(f"{n}_dw_in", lambda cm: _grad_matmul_t(
        sv["ht"], dz, tk=D_MODEL, tn=1152, name=f"{n}_dw_in", by_chip=True, comm=cm)))
    if f"{n}_presum_tail" in tr.plan:
        tr.alone(f"{n}_presum_tail")
    dx, red_i, dbin = tr.run(f"{n}_in_proj_bwd", lambda cm: _in_proj_bwd(
        dz, tr.weight(l, "w_in"), dxa, sv["x"], sc1, ts=256, name=f"{n}_in_proj_bwd", comm=cm))
    g["b_in"] = dbin[0]
    dada = jnp.stack([red_i[1], red_i[0], red_o[2], red_d[4], red_d[3], red_d[2]])
    return dx, g, dada


def _traffic_plan():
    plan = {
        "gather_l0": [("gather", 0, ("w_in",) + _LATE)],
        "l0_in_proj": [("gather", 1, ("w_in",))],
        "l0_mix_b": [("gather", 0, ("w_down",))],
        "l0_mix_o": [("gather", 0, ("w_up",))],
        "l0_up_proj": [("gather", 1, _LATE)],
        "l0_ffn": [("gather", 1, ("w_down",))],
        "l1_in_proj": [("gather", 1, ("w_up",))],
    }
    for l in reversed(range(DEPTH)):
        late = [(l, k) for k in _LATE]
        plan.update({
            f"l{l}_dw_up": [("presum", [(l, "w_down")])],
            f"l{l}_mix_o_bwd": [("presum", [(l, "w_up")]), ("scatter", [(l, "w_down")])],
            f"l{l}_mix_b_bwd": [("scatter", [(l, "w_up")])],
            f"l{l}_dw_in": [("presum", late), ("join", [(l, "w_down"), (l, "w_up")])],
        })
    late0, late1 = [(0, k) for k in _LATE], [(1, k) for k in _LATE]
    plan["l1_in_proj_bwd"] = [("presum", [(1, "w_in")]), ("scatter", late1)]
    plan["l0_ffn_bwd"] = [("scatter", [(1, "w_in")]), ("join", late1)]
    plan["l0_dw_up"] = plan["l0_dw_up"] + [("join", [(1, "w_in")])]
    plan["l0_presum_tail"] = [("presum", [(0, "w_in")])]
    plan["presum_l0_w_in"] = [("scatter", late0)]
    plan["l0_in_proj_bwd"] = [("scatter", [(0, "w_in")])]
    plan["join_tail"] = [("join", [(0, "w_in")] + late0)]
    return plan


def kernel(x, c, w_ada, b_ada, w_in, b_in, conv_a, w_a_out, ln_v_g, ln_v_b, w_spatial, b_spatial, w_b_out, w_pool, pool_scale, w_o, ln1_g, ln1_b, w_up, b_up, conv_ffn, conv_ffn_b, w_down, ln2_g, ln2_b, loss_target, m_w_ada, m_b_ada, m_w_in, m_b_in, m_conv_a, m_w_a_out, m_ln_v_g, m_ln_v_b, m_w_spatial, m_b_spatial, m_w_b_out, m_w_pool, m_pool_scale, m_w_o, m_ln1_g, m_ln1_b, m_w_up, m_b_up, m_conv_ffn, m_conv_ffn_b, m_w_down, m_ln2_g, m_ln2_b, v_w_ada, v_b_ada, v_w_in, v_b_in, v_conv_a, v_w_a_out, v_ln_v_g, v_ln_v_b, v_w_spatial, v_b_spatial, v_w_b_out, v_w_pool, v_pool_scale, v_w_o, v_ln1_g, v_ln1_b, v_w_up, v_b_up, v_conv_ffn, v_conv_ffn_b, v_w_down, v_ln2_g, v_ln2_b):
    args = locals()
    w = {k: args[k] for k in _WEIGHTS}
    m = {k: args["m_" + k] for k in _WEIGHTS}
    v = {k: args["v_" + k] for k in _WEIGHTS}
    d = D_MODEL
    mx, my, mc = _my_coords()
    chip = 2 * mx + my
    me = 4 * mx + 2 * my + mc

    small_shapes = [c.shape, conv_a.shape, conv_ffn.shape]
    small_all = _all_gather8(_pack_small([c, conv_a, conv_ffn]), name="gather_small")
    c_all, conv_a_st, conv_ffn_st = _unpack_small(small_all, small_shapes)
    c_all = c_all.reshape(N_DEV, d)
    conv_full = {"conv_a": jnp.concatenate([conv_a_st[2 * j] for j in range(N_CHIPS)], axis=-1),
                 "conv_ffn": jnp.concatenate([conv_ffn_st[2 * j] for j in range(N_CHIPS)], axis=-1)}

    chip_idx = jnp.reshape(chip, (1,)).astype(jnp.int32)
    slots = {}
    for l in range(DEPTH):
        bufs = _cast_into_slots([_shard3(w[k]) for k in _BIG], l, chip_idx, name=f"cast_l{l}")
        slots.update({(l, k): b for k, b in zip(_BIG, bufs)})
    tr = _Traffic(slots, _traffic_plan(), jnp.reshape(mc, (1,)).astype(jnp.int32), chip_idx)
    tr.alone("gather_l0")

    n_ada = w_ada.shape[2]
    b_ada_mine = lax.dynamic_slice_in_dim(b_ada, chip * n_ada, n_ada, axis=1)
    ada_part = _ada_fwd(c_all, w_ada, b_ada_mine.reshape(DEPTH, 1, n_ada), name="ada_fwd")
    ada_all = _all_gather8(_pack_small([ada_part]), name="gather_ada")
    ada_st = _unpack_small(ada_all, [ada_part.shape])[0][0::2]
    ada_rows = jnp.concatenate([ada_st[j] for j in range(N_CHIPS)], axis=-1)
    ada_mine = lax.dynamic_index_in_dim(ada_rows, me, axis=1, keepdims=False)

    def layer_params(l):
        p = {k: conv_full[k][l] for k in _SMALL_SHARDED}
        for k in ("b_in", "ln_v_g", "ln_v_b", "pool_scale", "ln1_g", "ln1_b", "b_up", "conv_ffn_b", "ln2_g", "ln2_b"):
            p[k] = w[k][l].reshape(1, -1)
        p["w_spatial"] = w_spatial[l]
        p["b_spatial_t"] = b_spatial[l].T
        return p

    xs = x[0]
    saved, adas, params = [], [], []
    for l in range(DEPTH):
        ada = [ada_mine[l, k * d:(k + 1) * d].reshape(1, d) for k in range(6)]
        p = layer_params(l)
        xs, sv = _layer_fwd(xs, ada, p, l, tr)
        saved.append(sv), adas.append(ada), params.append(p)
    dx, loss_blk = _loss_fwd(xs, loss_target[0], ts=512, name="loss")

    grads, dadas = [None] * DEPTH, [None] * DEPTH
    for l in reversed(range(DEPTH)):
        dx, grads[l], dadas[l] = _layer_bwd(dx, adas[l], params[l], saved[l], l, tr)
    tr.alone("join_tail")
    dada = jnp.stack(dadas).reshape(DEPTH, 6 * d)

    small_names = _SMALL_REPL + _SMALL_SHARDED
    small_g = [jnp.stack([grads[l][k] for l in range(DEPTH)]) for k in small_names]
    gsum = dict(zip(small_names, _unpack_small(_all_reduce_small(_pack_small(small_g), name="reduce_small"),
                                               [a.shape for a in small_g])))
    tail_g = [dada, loss_blk[0:1, 0:1]]
    tail_all, tail_sum = _all_gather8(_pack_small(tail_g), name="gather_dada", with_sum=True)
    gsum["b_ada"], loss_sum = _unpack_small(tail_sum, [a.shape for a in tail_g])
    loss = loss_sum[0, 0]
    dada_all = _unpack_small(tail_all, [a.shape for a in tail_g])[0]
    for k in _SMALL_SHARDED:
        wd = gsum[k].shape[-1] // N_CHIPS
        gsum[k] = lax.dynamic_slice_in_dim(gsum[k], chip * wd, wd, axis=gsum[k].ndim - 1)

    dada_cols = lax.dynamic_slice_in_dim(dada_all, chip * n_ada, n_ada, axis=2)
    dada_cols = jnp.pad(jnp.swapaxes(dada_cols, 0, 1), ((0, 0), (0, N_DEV), (0, 0)))
    gsum["w_ada"] = _ada_bwd(jnp.pad(c_all, ((0, N_DEV), (0, 0))), dada_cols, name="ada_bwd")

    out_g, out_d, out_m, out_v = {}, {}, {}, {}
    for k in _WEIGHTS:
        shp = w[k].shape
        if k in _BIG:
            res = tr.run(f"adamw_{k}", lambda cm: _adamw_sharded(
                _shard3(w[k]), _shard3(m[k]), _shard3(v[k]), [tr.final[(l, k)] for l in range(DEPTH)],
                name=f"adamw_{k}", comm=cm))
        else:
            gk = gsum[k].reshape(shp)
            res = [gk] + list(_adamw(_as2d(w[k]), _as2d(gk), _as2d(m[k]), _as2d(v[k]), name=f"adamw_{k}"))
        out_g[k], out_d[k], out_m[k], out_v[k] = [r.reshape(shp) for r in res]

    return (loss, dx[None], *[out_g[k] for k in _WEIGHTS], *[out_d[k] for k in _WEIGHTS],
            *[out_m[k] for k in _WEIGHTS], *[out_v[k] for k in _WEIGHTS])
```

```python
import math
from typing import Callable, NamedTuple

import jax
import jax.numpy as jnp
from jax import lax
from jax.experimental import pallas as pl
from jax.experimental.pallas import tpu as pltpu

F32 = jnp.float32
BF16 = jnp.bfloat16

D_MODEL = 1024
D_Z = 9216
D_FF = 2816
N_GROUPS = 8
GBLK = 128
CHUNK = 64
POOL_WINDOWS = (2, 4, 8, 16)
POOL_GROUP = 256
POOL_HALO = 16
CONV_HALO = 8
HALO_ROWS = 16
DEPTH = 2
ALPHA = (2 * DEPTH) ** 0.25
LN_EPS = 1e-5
ADAM_LR, ADAM_B1, ADAM_B2, ADAM_EPS, ADAM_WD, ADAM_STEP = 0.001, 0.9, 0.999, 1e-08, 0.01, 10
N_CHIPS = 4
N_DEV = 8
FF_CHUNK = 1408
MESH = pl.DeviceIdType.MESH
VMEM_LIMIT = 56 * 1024 * 1024
HBM = pl.BlockSpec(memory_space=pl.ANY)


def _dot(a, b):
    return jnp.dot(a, b, preferred_element_type=F32)


def _dot_nt(a, b):
    return lax.dot_general(a, b, (((1,), (1,)), ((), ())), preferred_element_type=F32)


def _dot_tn(a, b):
    return lax.dot_general(a, b, (((0,), (0,)), ((), ())), preferred_element_type=F32)


_GELU_C = math.sqrt(2.0 / math.pi)


def _gelu_and_grad(x):
    x2 = x * x
    t = jnp.tanh(_GELU_C * (x + 0.044715 * x * x2))
    g = 0.5 * x * (1.0 + t)
    dg = 0.5 * (1.0 + t) + 0.5 * x * (1.0 - t * t) * (_GELU_C * (1.0 + 3 * 0.044715 * x2))
    return g, dg


def _gelu(x):
    return 0.5 * x * (1.0 + jnp.tanh(_GELU_C * (x + 0.044715 * x * x * x)))


def _sigmoid(x):
    return 1.0 / (1.0 + jnp.exp(-x))


def _ln_fwd(r):
    mu = jnp.mean(r, axis=-1, keepdims=True)
    xc = r - mu
    var = jnp.mean(xc * xc, axis=-1, keepdims=True)
    rstd = lax.rsqrt(var + LN_EPS)
    return xc * rstd, rstd


def _ln_bwd(dy, g, xhat, rstd):
    dxh = dy * g
    m1 = jnp.mean(dxh, axis=-1, keepdims=True)
    m2 = jnp.mean(dxh * xhat, axis=-1, keepdims=True)
    return rstd * (dxh - m1 - xhat * m2)


def _rows_before(ext, k, halo):
    return pltpu.roll(ext, k, 0)[halo:]


def _rows_after(ext, k, n):
    return pltpu.roll(ext, ext.shape[0] - k, 0)[:n]


def _colsum(v):
    return jnp.sum(v, axis=0, keepdims=True)


def _spatial_mask():
    i = lax.broadcasted_iota(jnp.int32, (GBLK, GBLK), 0)
    j = lax.broadcasted_iota(jnp.int32, (GBLK, GBLK), 1)
    return (j // CHUNK) <= (i // CHUNK)


def _const(shape):
    n = len(shape)
    return pl.BlockSpec(shape, lambda *_: (0,) * n)


def _resident(shape):
    n = len(shape)
    return pl.BlockSpec(shape, lambda *_: (0,) * n, pipeline_mode=pl.Buffered(1))


def _tile(ts, s):
    return min(ts, s)


class _Comm(NamedTuple):
    srcs: tuple
    dsts: tuple
    n_remote: int
    n_local: int
    build: Callable
    alias: tuple = ()


def _my_coords():
    return lax.axis_index("x"), lax.axis_index("y"), lax.axis_index("c")


def _chip_peer(k):
    mx, my, mc = _my_coords()
    return (mx ^ ((k >> 1) & 1), my ^ (k & 1), mc)


def _sem_scratch(comm):
    return [pltpu.SemaphoreType.DMA((max(comm.n_remote, 1),)), pltpu.SemaphoreType.DMA((max(comm.n_remote, 1),)),
            pltpu.SemaphoreType.DMA((max(comm.n_local, 1),))]


def _run(body, *, name, grid, in_specs, out_specs, out_shape, args, scratch_shapes=(), comm=None, aliases=None,
         prefetch=None):
    sem = ("arbitrary",) * len(grid)
    cparams = pltpu.CompilerParams(dimension_semantics=sem, vmem_limit_bytes=VMEM_LIMIT)
    aliases = dict(aliases or {})
    n_pre = 0 if prefetch is None else 1

    def call(fn, in_specs, out_specs, out_shape, scratch_shapes, args):
        if prefetch is None:
            return pl.pallas_call(fn, name=name, grid=grid, in_specs=in_specs, out_specs=out_specs, out_shape=out_shape,
                                  scratch_shapes=scratch_shapes, compiler_params=cparams,
                                  input_output_aliases=aliases)(*args)
        spec = pltpu.PrefetchScalarGridSpec(num_scalar_prefetch=1, grid=grid, in_specs=in_specs, out_specs=out_specs,
                                            scratch_shapes=scratch_shapes)
        return pl.pallas_call(fn, name=name, grid_spec=spec, out_shape=out_shape, compiler_params=cparams,
                              input_output_aliases={k + 1: v for k, v in aliases.items()})(prefetch, *args)

    if comm is None:
        return call(body, list(in_specs), list(out_specs), list(out_shape), list(scratch_shapes), args)
    n_in, n_cs, n_out, n_cd, n_scr = len(in_specs), len(comm.srcs), len(out_specs), len(comm.dsts), len(scratch_shapes)
    aliases.update({n_in + si: n_out + di for si, di in comm.alias})
    total = math.prod(grid)
    mid_step = min(total - 1, int(total * 0.7))

    def wrapped(*refs):
        pre, refs = refs[:n_pre], refs[n_pre:]
        ins, refs = refs[:n_in], refs[n_in:]
        csrc, refs = refs[:n_cs], refs[n_cs:]
        outs, refs = refs[:n_out], refs[n_out:]
        cdst, refs = refs[:n_cd], refs[n_cd:]
        scr, sems = refs[:n_scr], refs[n_scr:]
        step = pl.program_id(0)
        for ax in range(1, len(grid)):
            step = step * grid[ax] + pl.program_id(ax)
        first, mid, last = comm.build(csrc, cdst, *sems, 0, 0)
        pl.when(step == 0)(first)
        if mid is not None:
            pl.when(step == mid_step)(mid)
        body(*pre, *ins, *outs, *scr)
        pl.when(step == total - 1)(last)

    res = call(wrapped, list(in_specs) + [HBM] * n_cs, list(out_specs) + [HBM] * n_cd,
               list(out_shape) + list(comm.dsts), list(scratch_shapes) + _sem_scratch(comm), (*args, *comm.srcs))
    return res[:n_out], res[n_out:]


def _comm_call(comm, *, name):
    def body(*refs):
        n_cs, n_cd = len(comm.srcs), len(comm.dsts)
        first, mid, last = comm.build(refs[:n_cs], refs[n_cs:n_cs + n_cd], *refs[n_cs + n_cd:], 0, 0)
        first()
        if mid is not None:
            mid()
        last()

    return pl.pallas_call(body, name=name, in_specs=[HBM] * len(comm.srcs), out_specs=[HBM] * len(comm.dsts),
                          out_shape=list(comm.dsts), scratch_shapes=_sem_scratch(comm),
                          input_output_aliases=dict(comm.alias))(*comm.srcs)


def _gather_comm(bufs, peers=(1, 2, 3)):
    dsts = tuple(jax.ShapeDtypeStruct(b.shape, b.dtype) for b in bufs)
    nw = len(bufs)

    def build(srcs, outs, send_sems, recv_sems, local_sems, r0, l0):
        mx, my, mc = _my_coords()
        me = 2 * mx + my
        sibling = (mx, my, 1 - mc)

        def rdma(src, dst, idx, peer):
            return pltpu.make_async_remote_copy(src_ref=src, dst_ref=dst, send_sem=send_sems.at[r0 + idx],
                                                recv_sem=recv_sems.at[r0 + idx], device_id=peer, device_id_type=MESH)

        def ici(w, k, slot):
            return rdma(outs[w].at[me, mc], outs[w].at[slot, mc], 6 * w + k - 1, _chip_peer(k))

        def fwd(w, k, half):
            return rdma(outs[w].at[me ^ k, mc], outs[w].at[me ^ k, half], 6 * w + 2 + k, sibling)

        def first():
            for w in range(nw):
                for k in peers:
                    ici(w, k, me).start()

        def mid():
            for w in range(nw):
                for k in peers:
                    ici(w, k, me ^ k).wait_recv()
                    fwd(w, k, mc).start()

        def last():
            for w in range(nw):
                for k in peers:
                    fwd(w, k, 1 - mc).wait_recv()
                    ici(w, k, me).wait_send()
                    fwd(w, k, mc).wait_send()

        return first, mid, last

    return _Comm(tuple(bufs), dsts, 6 * nw, 0, build, tuple((w, w) for w in range(nw)))


def _symmetric(make_remote, make_local, make_incoming=None):
    def first():
        for cp in make_remote() + make_local():
            cp.start()

    def last():
        for cp in (make_incoming or make_remote)():
            cp.wait_recv()
        for cp in make_remote():
            cp.wait_send()
        for cp in make_local():
            cp.wait()

    return first, None, last


def _presum_comm(g_bf16):
    nw = len(g_bf16)
    dsts = tuple(jax.ShapeDtypeStruct((N_CHIPS,) + g.shape[2:], BF16) for g in g_bf16)

    def build(srcs, outs, send_sems, recv_sems, local_sems, r0, l0):
        mx, my, mc = _my_coords()

        def remote():
            return [pltpu.make_async_remote_copy(
                src_ref=srcs[w].at[j, 1 - mc], dst_ref=outs[w].at[j], send_sem=send_sems.at[r0 + N_CHIPS * w + j],
                recv_sem=recv_sems.at[r0 + N_CHIPS * w + j], device_id=(mx, my, 1 - mc), device_id_type=MESH)
                for w in range(nw) for j in range(N_CHIPS)]

        return _symmetric(remote, lambda: [])

    return _Comm(tuple(g_bf16), dsts, N_CHIPS * nw, 0, build)


def _scatter_comm(h_bf16):
    nw = len(h_bf16)
    dsts = tuple(jax.ShapeDtypeStruct((N_CHIPS - 1,) + h.shape[1:], BF16) for h in h_bf16)

    def build(srcs, outs, send_sems, recv_sems, local_sems, r0, l0):
        mx, my, _ = _my_coords()
        me = 2 * mx + my

        def remote():
            return [pltpu.make_async_remote_copy(
                src_ref=srcs[w].at[me ^ k], dst_ref=outs[w].at[k - 1], send_sem=send_sems.at[r0 + 3 * w + k - 1],
                recv_sem=recv_sems.at[r0 + 3 * w + k - 1], device_id=_chip_peer(k), device_id_type=MESH)
                for w in range(nw) for k in range(1, N_CHIPS)]

        return _symmetric(remote, lambda: [])

    return _Comm(tuple(h_bf16), dsts, 3 * nw, 0, build)


def _join_comm(bufs):
    nw = len(bufs)
    dsts = tuple(jax.ShapeDtypeStruct(b.shape, b.dtype) for b in bufs)

    def build(srcs, outs, send_sems, recv_sems, local_sems, r0, l0):
        mx, my, mc = _my_coords()

        def remote(half=mc):
            return [pltpu.make_async_remote_copy(
                src_ref=outs[w].at[mc], dst_ref=outs[w].at[half], send_sem=send_sems.at[r0 + w],
                recv_sem=recv_sems.at[r0 + w], device_id=(mx, my, 1 - mc), device_id_type=MESH) for w in range(nw)]

        return _symmetric(remote, lambda: [], lambda: remote(1 - mc))

    return _Comm(tuple(bufs), dsts, nw, 0, build, tuple((w, w) for w in range(nw)))


def _merge(comms):
    comms = list(comms)
    if len(comms) == 1:
        return comms[0]

    def build(srcs, outs, send_sems, recv_sems, local_sems, r0, l0):
        phases, s0, d0 = [], 0, 0
        for cm in comms:
            phases.append(cm.build(srcs[s0:s0 + len(cm.srcs)], outs[d0:d0 + len(cm.dsts)], send_sems, recv_sems,
                                   local_sems, r0, l0))
            s0, d0, r0, l0 = s0 + len(cm.srcs), d0 + len(cm.dsts), r0 + cm.n_remote, l0 + cm.n_local

        def run(idx):
            fns = [ph[idx] for ph in phases if ph[idx] is not None]
            if not fns:
                return None

            def go():
                for fn in fns:
                    fn()
            return go

        return run(0), run(1), run(2)

    alias, s0, d0 = [], 0, 0
    for cm in comms:
        alias += [(s0 + si, d0 + di) for si, di in cm.alias]
        s0, d0 = s0 + len(cm.srcs), d0 + len(cm.dsts)
    return _Comm(sum((cm.srcs for cm in comms), ()), sum((cm.dsts for cm in comms), ()),
                 sum(cm.n_remote for cm in comms), sum(cm.n_local for cm in comms), build, tuple(alias))


def _split(comms, res):
    out, d0 = [], 0
    for cm in comms:
        out.append(list(res[d0:d0 + len(cm.dsts)]))
        d0 += len(cm.dsts)
    return out


def _all_reduce_small(x, *, name):
    r, lanes = x.shape
    half = r // 2
    assert half % 8 == 0

    def body(x_ref, out_ref, sib_ref, slots_ref, send_sems, recv_sems):
        mx, my, mc = _my_coords()
        me = 2 * mx + my
        sibling = (mx, my, 1 - mc)
        mine = pl.ds(pl.multiple_of(mc * half, 8), half)
        theirs = pl.ds(pl.multiple_of((1 - mc) * half, 8), half)

        def to_sibling(src, dst, idx):
            return pltpu.make_async_remote_copy(src_ref=src, dst_ref=dst, send_sem=send_sems.at[idx],
                                                recv_sem=recv_sems.at[idx], device_id=sibling, device_id_type=MESH)

        swap = to_sibling(x_ref.at[theirs], sib_ref, 0)
        swap.start()
        swap.wait_recv()
        swap.wait_send()
        slots_ref[me] = x_ref[mine, :] + sib_ref[...]

        def copy(k, slot):
            return pltpu.make_async_remote_copy(
                src_ref=slots_ref.at[me], dst_ref=slots_ref.at[slot], send_sem=send_sems.at[k], recv_sem=recv_sems.at[k],
                device_id=_chip_peer(k), device_id_type=MESH)

        sends = [copy(k, me) for k in range(1, N_CHIPS)]
        for cp in sends:
            cp.start()
        for k in range(1, N_CHIPS):
            copy(k, me ^ k).wait_recv()
        for cp in sends:
            cp.wait_send()
        acc = slots_ref[0]
        for j in range(1, N_CHIPS):
            acc = acc + slots_ref[j]
        out_ref[mine, :] = acc
        join = to_sibling(out_ref.at[mine], out_ref.at[mine], N_CHIPS)
        join.start()
        to_sibling(out_ref.at[mine], out_ref.at[theirs], N_CHIPS).wait_recv()
        join.wait_send()

    vmem = pl.BlockSpec(memory_space=pltpu.VMEM)
    return pl.pallas_call(
        body, name=name, in_specs=[vmem], out_specs=vmem, out_shape=jax.ShapeDtypeStruct((r, lanes), F32),
        scratch_shapes=[pltpu.VMEM((half, lanes), F32), pltpu.VMEM((N_CHIPS, half, lanes), F32),
                        pltpu.SemaphoreType.DMA((N_CHIPS + 1,)), pltpu.SemaphoreType.DMA((N_CHIPS + 1,))],
        compiler_params=pltpu.CompilerParams(vmem_limit_bytes=VMEM_LIMIT),
    )(x)


def _all_gather8(x, *, name, with_sum=False):
    r, lanes = x.shape

    def body(x_ref, out_ref, *rest):
        if with_sum:
            sum_ref, send_sems, recv_sems, local_sem = rest
        else:
            send_sems, recv_sems, local_sem = rest
        mx, my, mc = _my_coords()
        me = 4 * mx + 2 * my + mc

        def peer(k):
            return (mx ^ ((k >> 2) & 1), my ^ ((k >> 1) & 1), mc ^ (k & 1))

        def copy(k, slot):
            return pltpu.make_async_remote_copy(
                src_ref=x_ref, dst_ref=out_ref.at[slot], send_sem=send_sems.at[k - 1], recv_sem=recv_sems.at[k - 1],
                device_id=peer(k), device_id_type=MESH)

        mine = pltpu.make_async_copy(x_ref, out_ref.at[me], local_sem)
        mine.start()
        sends = [copy(k, me) for k in range(1, N_DEV)]
        for cp in sends:
            cp.start()
        for k in range(1, N_DEV):
            copy(k, me ^ k).wait_recv()
        for cp in sends:
            cp.wait_send()
        mine.wait()
        if with_sum:
            acc = out_ref[0]
            for k in range(1, N_DEV):
                acc = acc + out_ref[k]
            sum_ref[...] = acc

    vmem = pl.BlockSpec(memory_space=pltpu.VMEM)
    out_shape = [jax.ShapeDtypeStruct((N_DEV, r, lanes), F32)]
    if with_sum:
        out_shape.append(jax.ShapeDtypeStruct((r, lanes), F32))
    res = pl.pallas_call(
        body, name=name, in_specs=[vmem], out_specs=[vmem] * len(out_shape), out_shape=out_shape,
        scratch_shapes=[pltpu.SemaphoreType.DMA((N_DEV - 1,)), pltpu.SemaphoreType.DMA((N_DEV - 1,)),
                        pltpu.SemaphoreType.DMA],
        compiler_params=pltpu.CompilerParams(vmem_limit_bytes=VMEM_LIMIT),
    )(x)
    return res if with_sum else res[0]


def _mod_matmul(x, sc, sh, w4, b, *, ts, tn, name, comm=None):
    s, d = x.shape
    wd = w4.shape[2]
    n = N_CHIPS * wd
    per = wd // tn
    ts = _tile(ts, s)

    def body(x_ref, sc_ref, sh_ref, w_ref, b_ref, o_ref, ht_ref, h_scr):
        @pl.when(pl.program_id(1) == 0)
        def _():
            h = x_ref[...] * (1.0 + sc_ref[...]) + sh_ref[...]
            h_scr[...] = h.astype(BF16)
            ht_ref[...] = h.T.astype(BF16)
        o_ref[...] = (_dot(h_scr[...], w_ref[0]) + b_ref[...]).astype(BF16)

    return _run(
        body, name=name, grid=(s // ts, n // tn),
        in_specs=[pl.BlockSpec((ts, d), lambda i, j: (i, 0)), _const((1, d)), _const((1, d)),
                  pl.BlockSpec((1, d, tn), lambda i, j: (j // per, 0, j % per)),
                  pl.BlockSpec((1, tn), lambda i, j: (0, j))],
        out_specs=[pl.BlockSpec((ts, tn), lambda i, j: (i, j)), pl.BlockSpec((d, ts), lambda i, j: (0, i))],
        out_shape=[jax.ShapeDtypeStruct((s, n), BF16), jax.ShapeDtypeStruct((d, s), BF16)],
        scratch_shapes=[pltpu.VMEM((ts, d), BF16)],
        args=(x, sc, sh, w4, b), comm=comm)


def _conv3(q, ext, cw):
    return cw[2:3] * q + cw[1:2] * _rows_before(ext, 1, CONV_HALO) + cw[0:1] * _rows_before(ext, 2, CONV_HALO)


def _mix_a_fwd(z, cw, w_out, *, ts, name, comm=None):
    s = z.shape[0]
    d = D_MODEL
    ts = _tile(ts, s)

    def body(zb_ref, zc_ref, zx_ref, cw_ref, w_ref, a_ref, y_ref, carry):
        @pl.when(pl.program_id(0) == 0)
        def _():
            carry[...] = jnp.zeros_like(carry)
        q = zc_ref[...].astype(F32) * zx_ref[...].astype(F32)
        ext = jnp.concatenate([carry[...], q], axis=0)
        a = (zb_ref[...].astype(F32) * _conv3(q, ext, cw_ref[...])).astype(BF16)
        carry[...] = q[ts - CONV_HALO:]
        a_ref[...] = a
        y_ref[...] = _dot(a, w_ref[...])

    zspec = lambda k: pl.BlockSpec((ts, d), lambda i, k=k: (i, k))
    return _run(
        body, name=name, grid=(s // ts,),
        in_specs=[zspec(0), zspec(1), zspec(2), _const((3, d)), _const((d, d))],
        out_specs=[pl.BlockSpec((ts, d), lambda i: (i, 0))] * 2,
        out_shape=[jax.ShapeDtypeStruct((s, d), BF16), jax.ShapeDtypeStruct((s, d), F32)],
        scratch_shapes=[pltpu.VMEM((CONV_HALO, d), F32)],
        args=(z, z, z, cw, w_out), comm=comm)


def _spatial_mix(vn_b, ws_ref, bst_ref, mixed_scr, ts):
    nblk = ts // GBLK
    mask = _spatial_mask()
    for g in range(N_GROUPS):
        cols = slice(g * GBLK, (g + 1) * GBLK)
        wm = jnp.where(mask, ws_ref[g], 0.0).astype(BF16)
        cat = jnp.concatenate([vn_b[n * GBLK:(n + 1) * GBLK, cols] for n in range(nblk)], axis=1)
        res = _dot(wm, cat) + bst_ref[:, g:g + 1]
        for n in range(nblk):
            mixed_scr[n * GBLK:(n + 1) * GBLK, cols] = res[:, n * GBLK:(n + 1) * GBLK]


def _mix_b_fwd(z, ln_g, ln_b, ws, bst, w_out, *, ts, name, comm=None):
    s = z.shape[0]
    d = D_MODEL
    ts = _tile(ts, s)

    def body(zu_ref, zv_ref, g_ref, b_ref, ws_ref, bst_ref, w_ref, sg_ref, y_ref, mixed_scr):
        xhat, _ = _ln_fwd(_gelu(zv_ref[...].astype(F32)))
        vn = (xhat * g_ref[...] + b_ref[...]).astype(BF16)
        _spatial_mix(vn, ws_ref, bst_ref, mixed_scr, ts)
        sg = (_gelu(zu_ref[...].astype(F32)) * mixed_scr[...]).astype(BF16)
        sg_ref[...] = sg
        y_ref[...] = _dot(sg, w_ref[...])

    zspec = lambda k: pl.BlockSpec((ts, d), lambda i, k=k: (i, k))
    return _run(
        body, name=name, grid=(s // ts,),
        in_specs=[zspec(3), zspec(4), _const((1, d)), _const((1, d)), _const((N_GROUPS, GBLK, GBLK)),
                  _const((GBLK, N_GROUPS)), _const((d, d))],
        out_specs=[pl.BlockSpec((ts, d), lambda i: (i, 0))] * 2,
        out_shape=[jax.ShapeDtypeStruct((s, d), BF16), jax.ShapeDtypeStruct((s, d), F32)],
        scratch_shapes=[pltpu.VMEM((ts, d), F32)],
        args=(z, z, ln_g, ln_b, ws, bst, w_out), comm=comm)


def _pool_denoms(tile_idx, ts):
    t1 = (tile_idx * ts + 1 + lax.broadcasted_iota(jnp.int32, (ts, 1), 0)).astype(F32)
    return [jnp.minimum(t1, float(w)) for w in POOL_WINDOWS]


def _pool_diff(p, ext, denoms, k):
    cols = slice(k * POOL_GROUP, (k + 1) * POOL_GROUP)
    acc = ext[:, cols]
    step = 1
    while step < POOL_WINDOWS[k]:
        acc = acc + pltpu.roll(acc, step, 0)
        step *= 2
    return acc[POOL_HALO:] / denoms[k] - p[:, cols]


def _mix_c_fwd(z, w_pool, scale, *, ts, name, comm=None):
    s = z.shape[0]
    d = D_MODEL
    ts = _tile(ts, s)

    def body(zp_ref, w_ref, sc_ref, d_ref, y_ref, carry):
        i = pl.program_id(0)

        @pl.when(i == 0)
        def _():
            carry[...] = jnp.zeros_like(carry)
        p = zp_ref[...].astype(F32)
        ext = jnp.concatenate([carry[...], p], axis=0)
        carry[...] = p[ts - POOL_HALO:]
        denoms = _pool_denoms(i, ts)
        for k in range(len(POOL_WINDOWS)):
            cols = slice(k * POOL_GROUP, (k + 1) * POOL_GROUP)
            dk = _pool_diff(p, ext, denoms, k).astype(BF16)
            d_ref[:, cols] = dk
            y_ref[:, cols] = _dot(dk, w_ref[k]) * sc_ref[:, cols]

    return _run(
        body, name=name, grid=(s // ts,),
        in_specs=[pl.BlockSpec((ts, d), lambda i: (i, 5)), _const((4, POOL_GROUP, POOL_GROUP)), _const((1, d))],
        out_specs=[pl.BlockSpec((ts, d), lambda i: (i, 0))] * 2,
        out_shape=[jax.ShapeDtypeStruct((s, d), BF16), jax.ShapeDtypeStruct((s, d), F32)],
        scratch_shapes=[pltpu.VMEM((POOL_HALO, d), F32)],
        args=(z, w_pool, scale), comm=comm)


def _mix_o_fwd(x, z, ya, yb, yc, w_o, gt, ln_g, ln_b, *, ts, name, comm=None):
    s, d = x.shape
    ts = _tile(ts, s)

    def body(x_ref, ga_ref, gb_ref, gc_ref, ya_ref, yb_ref, yc_ref, w_ref, gt_ref, g_ref, b_ref,
             m_ref, o_ref, x1_ref):
        merged = (_sigmoid(ga_ref[...].astype(F32)) * ya_ref[...] + _sigmoid(gb_ref[...].astype(F32)) * yb_ref[...]
                  + _sigmoid(gc_ref[...].astype(F32)) * yc_ref[...]).astype(BF16)
        m_ref[...] = merged
        o = _dot(merged, w_ref[...])
        o_ref[...] = o
        xhat, _ = _ln_fwd(ALPHA * x_ref[...] + gt_ref[...] * o)
        x1_ref[...] = xhat * g_ref[...] + b_ref[...]

    row = pl.BlockSpec((ts, d), lambda i: (i, 0))
    zspec = lambda k: pl.BlockSpec((ts, d), lambda i, k=k: (i, k))
    return _run(
        body, name=name, grid=(s // ts,),
        in_specs=[row, zspec(6), zspec(7), zspec(8), row, row, row, _const((d, d)),
                  _const((1, d)), _const((1, d)), _const((1, d))],
        out_specs=[row] * 3,
        out_shape=[jax.ShapeDtypeStruct((s, d), BF16), jax.ShapeDtypeStruct((s, d), F32),
                   jax.ShapeDtypeStruct((s, d), F32)],
        args=(x, z, z, z, ya, yb, yc, w_o, gt, ln_g, ln_b), comm=comm)


def _ffn_fwd(up, x1, cw, cb, w_down, gt, ln_g, ln_b, *, ts, name, comm=None):
    s, d = x1.shape
    ts = _tile(ts, s)

    def body(up_ref, x1_ref, cw_ref, cb_ref, w_ref, gt_ref, g_ref, b_ref, ft_ref, dn_ref, x2_ref, carry, f_ref):
        @pl.when(pl.program_id(0) == 0)
        def _():
            carry[...] = jnp.zeros_like(carry)
        for c in range(D_FF // FF_CHUNK):
            ca = slice(c * FF_CHUNK, (c + 1) * FF_CHUNK)
            cg = slice(D_FF + c * FF_CHUNK, D_FF + (c + 1) * FF_CHUNK)
            ua = up_ref[:, ca].astype(F32)
            ext = jnp.concatenate([carry[:, ca], ua], axis=0)
            carry[:, ca] = ua[ts - CONV_HALO:]
            cf = _conv3(ua, ext, cw_ref[:, ca]) + cb_ref[:, ca]
            f = _gelu(cf) * up_ref[:, cg].astype(F32)
            f_ref[:, ca] = f.astype(BF16)
            ft_ref[ca, :] = f.T.astype(BF16)
        dn = _dot(f_ref[...], w_ref[...])
        dn_ref[...] = dn
        xhat, _ = _ln_fwd(ALPHA * x1_ref[...] + gt_ref[...] * dn)
        x2_ref[...] = xhat * g_ref[...] + b_ref[...]

    row = pl.BlockSpec((ts, d), lambda i: (i, 0))
    return _run(
        body, name=name, grid=(s // ts,),
        in_specs=[pl.BlockSpec((ts, 2 * D_FF), lambda i: (i, 0)), row, _const((3, D_FF)), _const((1, D_FF)),
                  _resident((D_FF, d)), _const((1, d)), _const((1, d)), _const((1, d))],
        out_specs=[pl.BlockSpec((D_FF, ts), lambda i: (0, i)), row, row],
        out_shape=[jax.ShapeDtypeStruct((D_FF, s), BF16), jax.ShapeDtypeStruct((s, d), F32),
                   jax.ShapeDtypeStruct((s, d), F32)],
        scratch_shapes=[pltpu.VMEM((CONV_HALO, D_FF), F32), pltpu.VMEM((ts, D_FF), BF16)],
        args=(up, x1, cw, cb, w_down, gt, ln_g, ln_b), comm=comm)


def _loss_fwd(y, tgt, *, ts, name):
    s, d = y.shape
    ts = _tile(ts, s)

    def body(y_ref, t_ref, dy_ref, l_ref):
        @pl.when(pl.program_id(0) == 0)
        def _():
            l_ref[...] = jnp.zeros_like(l_ref)
        e = y_ref[...] - t_ref[...]
        dy_ref[...] = e / float(d)
        l_ref[...] += 0.5 * jnp.sum(jnp.mean(e * e, axis=-1, keepdims=True), axis=0, keepdims=True)

    row = pl.BlockSpec((ts, d), lambda i: (i, 0))
    return _run(body, name=name, grid=(s // ts,), in_specs=[row, row], out_specs=[row, _const((8, 128))],
                out_shape=[jax.ShapeDtypeStruct((s, d), F32), jax.ShapeDtypeStruct((8, 128), F32)], args=(y, tgt))


def _rev(n_tiles):
    return lambda i: n_tiles - 1 - i


def _halo_spec(ts, n_tiles, halo, width, col):
    per = ts // halo
    return pl.BlockSpec((halo, width), lambda i: (jnp.maximum((n_tiles - 1 - i) * per - 1, 0), col))


def _ffn_bwd(dx2, x1, dn, up, cw, cb, w_down, w_up4, gt, ln_g, sc, *, ts, name, comm=None):
    s, d = x1.shape
    ts = _tile(ts, s)
    nt = s // ts
    rev = _rev(nt)
    wd = w_up4.shape[2]

    def w_up_cols(wu_ref, start):
        return wu_ref[start // wd, :, start % wd:start % wd + FF_CHUNK]

    def body(dx2_ref, x1_ref, dn_ref, up_ref, halo_ref, cw_ref, cb_ref, wd_ref, wu_ref, gt_ref, g_ref, sc_ref,
             ddn_ref, dup_ref, dx1_ref, redd_ref, redf_ref, dbup_ref, carry):
        i = pl.program_id(0)

        @pl.when(i == 0)
        def _():
            carry[...] = jnp.zeros_like(carry)
            redd_ref[...] = jnp.zeros_like(redd_ref)
            redf_ref[...] = jnp.zeros_like(redf_ref)
            dbup_ref[...] = jnp.zeros_like(dbup_ref)
        first_tile = i == nt - 1
        x1v, dnv, dyv = x1_ref[...], dn_ref[...], dx2_ref[...]
        xhat, rstd = _ln_fwd(ALPHA * x1v + gt_ref[...] * dnv)
        dr = _ln_bwd(dyv, g_ref[...], xhat, rstd)
        redd_ref[0:1, :] += _colsum(dyv * xhat)
        redd_ref[1:2, :] += _colsum(dyv)
        redd_ref[2:3, :] += _colsum(dr * dnv)
        ddn = (gt_ref[...] * dr).astype(BF16)
        ddn_ref[...] = ddn
        dh = jnp.zeros((ts, d), F32)
        for c in range(D_FF // FF_CHUNK):
            ca = slice(c * FF_CHUNK, (c + 1) * FF_CHUNK)
            cg = slice(D_FF + c * FF_CHUNK, D_FF + (c + 1) * FF_CHUNK)
            df = _dot_nt(ddn, wd_ref[ca, :])
            ua, ug = up_ref[:, ca].astype(F32), up_ref[:, cg].astype(F32)
            halo = jnp.where(first_tile, 0.0, halo_ref[:, ca].astype(F32)[HALO_ROWS - CONV_HALO:])
            ext = jnp.concatenate([halo, ua], axis=0)
            u1, u2 = _rows_before(ext, 1, CONV_HALO), _rows_before(ext, 2, CONV_HALO)
            cwc = cw_ref[:, ca]
            gl, dgl = _gelu_and_grad(cwc[2:3] * ua + cwc[1:2] * u1 + cwc[0:1] * u2 + cb_ref[:, ca])
            dug = df * gl
            dcf = df * ug * dgl
            redf_ref[0:1, ca] += _colsum(dcf * u2)
            redf_ref[1:2, ca] += _colsum(dcf * u1)
            redf_ref[2:3, ca] += _colsum(dcf * ua)
            redf_ref[3:4, ca] += _colsum(dcf)
            extd = jnp.concatenate([dcf, carry[:, ca]], axis=0)
            carry[:, ca] = dcf[:CONV_HALO]
            dua = cwc[2:3] * dcf + cwc[1:2] * _rows_after(extd, 1, ts) + cwc[0:1] * _rows_after(extd, 2, ts)
            dbup_ref[0:1, ca] += _colsum(dua)
            dbup_ref[0:1, cg] += _colsum(dug)
            dua_b, dug_b = dua.astype(BF16), dug.astype(BF16)
            dup_ref[:, ca] = dua_b
            dup_ref[:, cg] = dug_b
            dh = dh + _dot_nt(dua_b, w_up_cols(wu_ref, c * FF_CHUNK)) + _dot_nt(dug_b, w_up_cols(wu_ref, D_FF + c * FF_CHUNK))
        dx1_ref[...] = ALPHA * dr + dh * (1.0 + sc_ref[...])
        redd_ref[3:4, :] += _colsum(dh * x1v)
        redd_ref[4:5, :] += _colsum(dh)

    row = pl.BlockSpec((ts, d), lambda i: (rev(i), 0))
    return _run(
        body, name=name, grid=(nt,),
        in_specs=[row, row, row, pl.BlockSpec((ts, 2 * D_FF), lambda i: (rev(i), 0)),
                  _halo_spec(ts, nt, HALO_ROWS, D_FF, 0), _const((3, D_FF)), _const((1, D_FF)),
                  _resident((D_FF, d)), _resident((N_CHIPS, d, wd)), _const((1, d)), _const((1, d)), _const((1, d))],
        out_specs=[row, pl.BlockSpec((ts, 2 * D_FF), lambda i: (rev(i), 0)), row,
                   _const((8, d)), _const((8, D_FF)), _const((8, 2 * D_FF))],
        out_shape=[jax.ShapeDtypeStruct((s, d), BF16), jax.ShapeDtypeStruct((s, 2 * D_FF), BF16),
                   jax.ShapeDtypeStruct((s, d), F32), jax.ShapeDtypeStruct((8, d), F32),
                   jax.ShapeDtypeStruct((8, D_FF), F32), jax.ShapeDtypeStruct((8, 2 * D_FF), F32)],
        scratch_shapes=[pltpu.VMEM((CONV_HALO, D_FF), F32)],
        args=(dx2, x1, dn, up, up, cw, cb, w_down, w_up4, gt, ln_g, sc), comm=comm)


def _accumulate_dw(dw_ref, dwb_ref, xa, dy, first, last):
    @pl.when(first)
    def _():
        dw_ref[...] = jnp.zeros_like(dw_ref)
    dw_ref[...] += _dot_tn(xa, dy)

    @pl.when(last)
    def _():
        dwb_ref[...] = dw_ref[...].astype(BF16)


def _dw_out(d):
    return [_const((d, d))] * 2, [jax.ShapeDtypeStruct((d, d), F32), jax.ShapeDtypeStruct((d, d), BF16)]


def _grad_matmul_t(xt, dy, *, tk, tn, name, by_chip=False, comm=None):
    k, s = xt.shape
    n = dy.shape[1]

    def body(xt_ref, dy_ref, o_ref, ob_ref):
        o = _dot(xt_ref[...], dy_ref[...]).reshape(o_ref.shape)
        o_ref[...] = o
        ob_ref[...] = o.astype(BF16)

    if by_chip:
        assert tk == k
        per = n // N_CHIPS // tn
        ospec = pl.BlockSpec((1, k, tn), lambda j, i: (j // per, 0, j % per))
        shape = (N_CHIPS, k, n // N_CHIPS)
    else:
        ospec = pl.BlockSpec((tk, tn), lambda j, i: (i, j))
        shape = (k, n)
    xspec = _resident((k, s)) if tk == k else pl.BlockSpec((tk, s), lambda j, i: (i, 0))
    dspec = _resident((s, n)) if tn == n else pl.BlockSpec((s, tn), lambda j, i: (0, j))
    return _run(body, name=name, grid=(n // tn, k // tk), in_specs=[xspec, dspec], out_specs=[ospec, ospec],
                out_shape=[jax.ShapeDtypeStruct(shape, F32), jax.ShapeDtypeStruct(shape, BF16)], args=(xt, dy), comm=comm)


def _mix_o_bwd(dx1, x, o, z, ya, yb, yc, merged, w_o, gt, ln_g, *, ts, name, comm=None):
    s, d = x.shape
    ts = _tile(ts, s)
    nt = s // ts

    def body(dx1_ref, x_ref, o_ref, ga_ref, gb_ref, gc_ref, ya_ref, yb_ref, yc_ref, m_ref, w_ref, gt_ref, g_ref,
             dxa_ref, dzg_ref, dya_ref, dyb_ref, dyc_ref, red_ref, dw_ref, dwb_ref):
        i = pl.program_id(0)

        @pl.when(i == 0)
        def _():
            red_ref[...] = jnp.zeros_like(red_ref)
        dyv, ov = dx1_ref[...], o_ref[...]
        xhat, rstd = _ln_fwd(ALPHA * x_ref[...] + gt_ref[...] * ov)
        dr = _ln_bwd(dyv, g_ref[...], xhat, rstd)
        red_ref[0:1, :] += _colsum(dyv * xhat)
        red_ref[1:2, :] += _colsum(dyv)
        red_ref[2:3, :] += _colsum(dr * ov)
        dxa_ref[...] = ALPHA * dr
        d_o = (gt_ref[...] * dr).astype(BF16)
        _accumulate_dw(dw_ref, dwb_ref, m_ref[...], d_o, i == 0, i == nt - 1)
        dm = _dot_nt(d_o, w_ref[...])
        for k, (zg_ref, y_ref, dy_ref) in enumerate(((ga_ref, ya_ref, dya_ref), (gb_ref, yb_ref, dyb_ref),
                                                     (gc_ref, yc_ref, dyc_ref))):
            g = _sigmoid(zg_ref[...].astype(F32))
            dzg_ref[:, k * d:(k + 1) * d] = (dm * y_ref[...] * g * (1.0 - g)).astype(BF16)
            dy_ref[...] = (dm * g).astype(BF16)

    row = pl.BlockSpec((ts, d), lambda i: (i, 0))
    zspec = lambda k: pl.BlockSpec((ts, d), lambda i, k=k: (i, k))
    bf = jax.ShapeDtypeStruct((s, d), BF16)
    dw_specs, dw_shapes = _dw_out(d)
    return _run(
        body, name=name, grid=(nt,),
        in_specs=[row, row, row, zspec(6), zspec(7), zspec(8), row, row, row, row, _const((d, d)),
                  _const((1, d)), _const((1, d))],
        out_specs=[row, pl.BlockSpec((ts, 3 * d), lambda i: (i, 2)), row, row, row, _const((8, d))] + dw_specs,
        out_shape=[jax.ShapeDtypeStruct((s, d), F32), jax.ShapeDtypeStruct((s, D_Z), BF16), bf, bf, bf,
                   jax.ShapeDtypeStruct((8, d), F32)] + dw_shapes,
        args=(dx1, x, o, z, z, z, ya, yb, yc, merged, w_o, gt, ln_g), comm=comm)


def _mix_a_bwd(dya, a, z, dz, cw, w_out, *, ts, name, comm=None):
    s = z.shape[0]
    d = D_MODEL
    ts = _tile(ts, s)
    nt = s // ts
    rev = _rev(nt)

    def body(dya_ref, a_ref, zb_ref, zc_ref, zx_ref, hc_ref, hx_ref, cw_ref, w_ref, dz_in, dz_ref, red_ref,
             dw_ref, dwb_ref, carry):
        i = pl.program_id(0)

        @pl.when(i == 0)
        def _():
            carry[...] = jnp.zeros_like(carry)
            red_ref[...] = jnp.zeros_like(red_ref)
        _accumulate_dw(dw_ref, dwb_ref, a_ref[...], dya_ref[...], i == 0, i == nt - 1)
        zb, zc, zx = zb_ref[...].astype(F32), zc_ref[...].astype(F32), zx_ref[...].astype(F32)
        q = zc * zx
        halo = jnp.where(i == nt - 1, 0.0, (hc_ref[...].astype(F32) * hx_ref[...].astype(F32))[HALO_ROWS - CONV_HALO:])
        ext = jnp.concatenate([halo, q], axis=0)
        q1, q2 = _rows_before(ext, 1, CONV_HALO), _rows_before(ext, 2, CONV_HALO)
        cwv = cw_ref[...]
        cv = cwv[2:3] * q + cwv[1:2] * q1 + cwv[0:1] * q2
        da = _dot_nt(dya_ref[...], w_ref[...])
        dcv = da * zb
        red_ref[0:1, :] += _colsum(dcv * q2)
        red_ref[1:2, :] += _colsum(dcv * q1)
        red_ref[2:3, :] += _colsum(dcv * q)
        extd = jnp.concatenate([dcv, carry[...]], axis=0)
        carry[...] = dcv[:CONV_HALO]
        dq = cwv[2:3] * dcv + cwv[1:2] * _rows_after(extd, 1, ts) + cwv[0:1] * _rows_after(extd, 2, ts)
        dz_ref[:, 0:d] = (da * cv).astype(BF16)
        dz_ref[:, d:2 * d] = (dq * zx).astype(BF16)
        dz_ref[:, 2 * d:3 * d] = (dq * zc).astype(BF16)

    zspec = lambda k: pl.BlockSpec((ts, d), lambda i, k=k: (rev(i), k))
    row = pl.BlockSpec((ts, d), lambda i: (rev(i), 0))
    dw_specs, dw_shapes = _dw_out(d)
    return _run(
        body, name=name, grid=(nt,),
        in_specs=[row, row, zspec(0), zspec(1), zspec(2),
                  _halo_spec(ts, nt, HALO_ROWS, d, 1), _halo_spec(ts, nt, HALO_ROWS, d, 2),
                  _const((3, d)), _const((d, d)), HBM],
        out_specs=[pl.BlockSpec((ts, 3 * d), lambda i: (rev(i), 0)), _const((8, d))] + dw_specs,
        out_shape=[jax.ShapeDtypeStruct((s, D_Z), BF16), jax.ShapeDtypeStruct((8, d), F32)] + dw_shapes,
        scratch_shapes=[pltpu.VMEM((CONV_HALO, d), F32)],
        args=(dya, a, z, z, z, z, z, cw, w_out, dz), aliases={9: 0}, comm=comm)


def _mix_b_bwd(dyb, sg, z, dz, ln_g, ln_b, ws, bst, w_out, *, ts, name, comm=None):
    s = z.shape[0]
    d = D_MODEL
    ts = _tile(ts, s)
    nt = s // ts
    nblk = ts // GBLK

    def body(dyb_ref, sg_ref, zu_ref, zv_ref, g_ref, b_ref, ws_ref, bst_ref, w_ref, dz_in,
             dz_ref, red_ref, dws_ref, dbst_ref, dw_ref, dwb_ref, mixed_scr, dvn_scr, dzv_scr):
        first = (pl.program_id(0) == 0) & (pl.program_id(1) == 0)

        @pl.when(first)
        def _():
            red_ref[...] = jnp.zeros_like(red_ref)
            dws_ref[...] = jnp.zeros_like(dws_ref)
            dbst_ref[...] = jnp.zeros_like(dbst_ref)

        @pl.when(pl.program_id(1) == 0)
        def _():
            _accumulate_dw(dw_ref, dwb_ref, sg_ref[...], dyb_ref[...], first, pl.program_id(0) == nt - 1)
            u, du_dz = _gelu_and_grad(zu_ref[...].astype(F32))
            vg, dv_dz = _gelu_and_grad(zv_ref[...].astype(F32))
            xhat, rstd = _ln_fwd(vg)
            vn = (xhat * g_ref[...] + b_ref[...]).astype(BF16)
            _spatial_mix(vn, ws_ref, bst_ref, mixed_scr, ts)
            dsg = _dot_nt(dyb_ref[...], w_ref[...])
            dz_ref[...] = (dsg * mixed_scr[...] * du_dz).astype(BF16)
            dmix = dsg * u
            mask = _spatial_mask()
            for g in range(N_GROUPS):
                cols = slice(g * GBLK, (g + 1) * GBLK)
                wm = jnp.where(mask, ws_ref[g], 0.0).astype(BF16)
                dm_cat = jnp.concatenate([dmix[n * GBLK:(n + 1) * GBLK, cols] for n in range(nblk)], axis=1)
                vn_cat = jnp.concatenate([vn[n * GBLK:(n + 1) * GBLK, cols] for n in range(nblk)], axis=1)
                dm_b = dm_cat.astype(BF16)
                dbst_ref[:, g:g + 1] += jnp.sum(dm_cat, axis=1, keepdims=True)
                dws_ref[g] += jnp.where(mask, _dot_nt(dm_b, vn_cat), 0.0)
                dvn_cat = _dot_tn(wm, dm_b)
                for n in range(nblk):
                    dvn_scr[n * GBLK:(n + 1) * GBLK, cols] = dvn_cat[:, n * GBLK:(n + 1) * GBLK]
            dvn = dvn_scr[...]
            red_ref[0:1, :] += _colsum(dvn * xhat)
            red_ref[1:2, :] += _colsum(dvn)
            dzv_scr[...] = (_ln_bwd(dvn, g_ref[...], xhat, rstd) * dv_dz).astype(BF16)

        @pl.when(pl.program_id(1) == 1)
        def _():
            dz_ref[...] = dzv_scr[...]

    zspec = lambda k: pl.BlockSpec((ts, d), lambda i, h, k=k: (i, k))
    row = pl.BlockSpec((ts, d), lambda i, h: (i, 0))
    dw_specs, dw_shapes = _dw_out(d)
    return _run(
        body, name=name, grid=(nt, 2),
        in_specs=[row, row, zspec(3), zspec(4), _const((1, d)), _const((1, d)),
                  _const((N_GROUPS, GBLK, GBLK)), _const((GBLK, N_GROUPS)), _const((d, d)), HBM],
        out_specs=[pl.BlockSpec((ts, d), lambda i, h: (i, 3 + h)), _const((8, d)),
                   _const((N_GROUPS, GBLK, GBLK)), _const((GBLK, N_GROUPS))] + dw_specs,
        out_shape=[jax.ShapeDtypeStruct((s, D_Z), BF16), jax.ShapeDtypeStruct((8, d), F32),
                   jax.ShapeDtypeStruct((N_GROUPS, GBLK, GBLK), F32), jax.ShapeDtypeStruct((GBLK, N_GROUPS), F32)]
        + dw_shapes,
        scratch_shapes=[pltpu.VMEM((ts, d), F32), pltpu.VMEM((ts, d), F32), pltpu.VMEM((ts, d), BF16)],
        args=(dyb, sg, z, z, ln_g, ln_b, ws, bst, w_out, dz), aliases={9: 0}, comm=comm)


def _mix_c_bwd(dyc, z, dz, w_pool, scale, *, ts, name):
    s = z.shape[0]
    d = D_MODEL
    ts = _tile(ts, s)
    nt = s // ts
    rev = _rev(nt)

    def body(dyc_ref, zp_ref, halo_ref, w_ref, sc_ref, dz_in, dz_ref, red_ref, dw_ref, carry):
        i = pl.program_id(0)

        @pl.when(i == 0)
        def _():
            carry[...] = jnp.zeros_like(carry)
            red_ref[...] = jnp.zeros_like(red_ref)
            dw_ref[...] = jnp.zeros_like(dw_ref)
        p = zp_ref[...].astype(F32)
        ext = jnp.concatenate([jnp.where(i == nt - 1, 0.0, halo_ref[...].astype(F32)), p], axis=0)
        denoms = _pool_denoms(rev(i), ts)
        dyv = dyc_ref[...].astype(F32)
        for k in range(len(POOL_WINDOWS)):
            cols = slice(k * POOL_GROUP, (k + 1) * POOL_GROUP)
            dk = _pool_diff(p, ext, denoms, k).astype(BF16)
            red_ref[0:1, cols] += _colsum(dyv[:, cols] * _dot(dk, w_ref[k]))
            dpre = (dyv[:, cols] * sc_ref[:, cols]).astype(BF16)
            dw_ref[k] += _dot_tn(dk, dpre)
            dd = _dot_nt(dpre, w_ref[k])
            e = dd / denoms[k]
            acc = jnp.concatenate([e, carry[:, cols]], axis=0)
            carry[:, cols] = e[:POOL_HALO]
            step = 1
            while step < POOL_WINDOWS[k]:
                acc = acc + pltpu.roll(acc, acc.shape[0] - step, 0)
                step *= 2
            dz_ref[:, cols] = (acc[:ts] - dd).astype(BF16)

    return _run(
        body, name=name, grid=(nt,),
        in_specs=[pl.BlockSpec((ts, d), lambda i: (rev(i), 0)), pl.BlockSpec((ts, d), lambda i: (rev(i), 5)),
                  _halo_spec(ts, nt, POOL_HALO, d, 5), _const((4, POOL_GROUP, POOL_GROUP)), _const((1, d)), HBM],
        out_specs=[pl.BlockSpec((ts, d), lambda i: (rev(i), 5)), _const((8, d)), _const((4, POOL_GROUP, POOL_GROUP))],
        out_shape=[jax.ShapeDtypeStruct((s, D_Z), BF16), jax.ShapeDtypeStruct((8, d), F32),
                   jax.ShapeDtypeStruct((4, POOL_GROUP, POOL_GROUP), F32)],
        scratch_shapes=[pltpu.VMEM((POOL_HALO, d), F32)],
        args=(dyc, z, z, w_pool, scale, dz), aliases={5: 0})


def _in_proj_bwd(dz, w4, dxa, x, sc, *, ts, name, comm=None):
    s, d = x.shape
    ts = _tile(ts, s)
    wd = w4.shape[2]

    def body(dz_ref, w_ref, dxa_ref, x_ref, sc_ref, dx_ref, red_ref, db_ref):
        @pl.when(pl.program_id(0) == 0)
        def _():
            red_ref[...] = jnp.zeros_like(red_ref)
            db_ref[...] = jnp.zeros_like(db_ref)
        dh = jnp.zeros((ts, d), F32)
        for j in range(N_CHIPS):
            dzj = dz_ref[:, j * wd:(j + 1) * wd]
            db_ref[0:1, j * wd:(j + 1) * wd] += _colsum(dzj.astype(F32))
            dh = dh + _dot_nt(dzj, w_ref[j])
        dx_ref[...] = dxa_ref[...] + dh * (1.0 + sc_ref[...])
        red_ref[0:1, :] += _colsum(dh * x_ref[...])
        red_ref[1:2, :] += _colsum(dh)

    row = pl.BlockSpec((ts, d), lambda i: (i, 0))
    return _run(
        body, name=name, grid=(s // ts,),
        in_specs=[pl.BlockSpec((ts, D_Z), lambda i: (i, 0)), _resident((N_CHIPS, d, wd)), row, row, _const((1, d))],
        out_specs=[row, _const((8, d)), _const((8, D_Z))],
        out_shape=[jax.ShapeDtypeStruct((s, d), F32), jax.ShapeDtypeStruct((8, d), F32),
                   jax.ShapeDtypeStruct((8, D_Z), F32)],
        args=(dz, w4, dxa, x, sc), comm=comm)


def _ada_fwd(c_all, w_ada, b_ada, *, name):
    nl, d, n = w_ada.shape
    tn = n // 2

    def body(c_ref, w_ref, b_ref, o_ref):
        cv = c_ref[...]
        ca = (cv * _sigmoid(cv)).astype(BF16)
        o_ref[0] = _dot(ca, w_ref[0].astype(BF16)) + b_ref[0]

    return _run(
        body, name=name, grid=(nl, n // tn),
        in_specs=[_const((N_DEV, d)), pl.BlockSpec((1, d, tn), lambda l, j: (l, 0, j)),
                  pl.BlockSpec((1, 1, tn), lambda l, j: (l, 0, j))],
        out_specs=[pl.BlockSpec((1, N_DEV, tn), lambda l, j: (l, 0, j))],
        out_shape=[jax.ShapeDtypeStruct((nl, N_DEV, n), F32)], args=(c_all, w_ada, b_ada))[0]


def _ada_bwd(c_all, dada, *, name):
    nl, nb, n = dada.shape
    d = c_all.shape[1]
    tn = n // 2

    def body(c_ref, g_ref, o_ref):
        cv = c_ref[...]
        ca = (cv * _sigmoid(cv)).astype(BF16)
        o_ref[0] = _dot_tn(ca, g_ref[0].astype(BF16))

    return _run(
        body, name=name, grid=(nl, n // tn),
        in_specs=[_const((nb, d)), pl.BlockSpec((1, nb, tn), lambda l, j: (l, 0, j))],
        out_specs=[pl.BlockSpec((1, d, tn), lambda l, j: (l, 0, j))],
        out_shape=[jax.ShapeDtypeStruct((nl, d, n), F32)], args=(c_all, dada))[0]


def _sum4_into_half(own, recv, core, *, name):
    r, c = own.shape
    tr = _row_tile(r, c, 2)

    def body(core_ref, own_ref, recv_ref, o_ref):
        acc = own_ref[...]
        for k in range(N_CHIPS - 1):
            acc = acc + recv_ref[k].astype(F32)
        o_ref[0] = acc

    spec = pltpu.PrefetchScalarGridSpec(
        num_scalar_prefetch=1, grid=(r // tr,),
        in_specs=[pl.BlockSpec((tr, c), lambda i, core_ref: (i, 0)),
                  pl.BlockSpec((N_CHIPS - 1, tr, c), lambda i, core_ref: (0, i, 0))],
        out_specs=pl.BlockSpec((1, tr, c), lambda i, core_ref: (core_ref[0], i, 0)))
    return pl.pallas_call(
        body, name=name, grid_spec=spec, out_shape=jax.ShapeDtypeStruct((2, r, c), F32),
        compiler_params=pltpu.CompilerParams(dimension_semantics=("arbitrary",), vmem_limit_bytes=VMEM_LIMIT),
    )(core, own, recv)


def _cast_into_slots(shards, layer, chip, *, name):
    quarters = 4

    def body(chip_ref, *refs):
        ins, outs = refs[:len(shards)], refs[len(shards):]
        for i_ref, o_ref in zip(ins, outs):
            o_ref[0, 0] = i_ref[0].astype(BF16)

    in_specs, out_specs, out_shape = [], [], []
    for sh in shards:
        _, r, c = sh.shape
        in_specs.append(pl.BlockSpec((1, r // quarters, c), lambda t, chip_ref: (layer, t, 0)))
        out_specs.append(pl.BlockSpec((1, 1, r // quarters, c), lambda t, chip_ref: (chip_ref[0], t // 2, t % 2, 0)))
        out_shape.append(jax.ShapeDtypeStruct((N_CHIPS, 2, r // 2, c), BF16))
    spec = pltpu.PrefetchScalarGridSpec(num_scalar_prefetch=1, grid=(quarters,), in_specs=in_specs, out_specs=out_specs)
    return pl.pallas_call(
        body, name=name, grid_spec=spec, out_shape=out_shape,
        compiler_params=pltpu.CompilerParams(dimension_semantics=("arbitrary",), vmem_limit_bytes=VMEM_LIMIT),
    )(chip, *shards)


def _sum_halves(g_f32, theirs, place, *, name, comm=None):
    _, _, rh, c = g_f32.shape
    tr = _row_tile(rh, c, 2)

    def body(place_ref, g_ref, t_ref, hb_ref, own_ref):
        h = g_ref[0, 0] + t_ref[0].astype(F32)
        hb_ref[0] = h.astype(BF16)

        @pl.when(pl.program_id(1) == place_ref[1])
        def _():
            own_ref[...] = h

    return _run(
        body, name=name, grid=(rh // tr, N_CHIPS), prefetch=place,
        in_specs=[pl.BlockSpec((1, 1, tr, c), lambda i, j, place_ref: (j, place_ref[0], i, 0)),
                  pl.BlockSpec((1, tr, c), lambda i, j, place_ref: (j, i, 0))],
        out_specs=[pl.BlockSpec((1, tr, c), lambda i, j, place_ref: (j, i, 0)),
                   pl.BlockSpec((tr, c), lambda i, j, place_ref: (i, 0))],
        out_shape=[jax.ShapeDtypeStruct((N_CHIPS, rh, c), BF16), jax.ShapeDtypeStruct((rh, c), F32)],
        args=(g_f32, theirs), comm=comm)


def _row_tile(r, c, mib):
    limit = max(8, (mib << 20) // (4 * c))
    if r <= limit:
        return r
    best = 8
    for t in range(8, limit + 1, 8):
        if r % t == 0:
            best = t
    return best


def _adam_math(w, g, m, v):
    mn = ADAM_B1 * m + (1.0 - ADAM_B1) * g
    vn = ADAM_B2 * v + (1.0 - ADAM_B2) * (g * g)
    m_hat = mn / (1.0 - ADAM_B1 ** ADAM_STEP)
    v_hat = vn / (1.0 - ADAM_B2 ** ADAM_STEP)
    return -ADAM_LR * (m_hat / (jnp.sqrt(v_hat) + ADAM_EPS) + ADAM_WD * w), mn, vn


def _adamw(w, g, m, v, *, name):
    r, c = w.shape
    tr = _row_tile(r, c, 2)

    def body(w_ref, g_ref, m_ref, v_ref, d_ref, mo_ref, vo_ref):
        d_ref[...], mo_ref[...], vo_ref[...] = _adam_math(w_ref[...], g_ref[...], m_ref[...], v_ref[...])

    blk = pl.BlockSpec((tr, c), lambda i: (i, 0))
    return _run(body, name=name, grid=(r // tr,), in_specs=[blk] * 4, out_specs=[blk] * 3,
                out_shape=[jax.ShapeDtypeStruct((r, c), F32)] * 3, args=(w, g, m, v))


def _adamw_sharded(w, m, v, grads, *, name, comm=None):
    nl, r, c = w.shape
    tr = _row_tile(r, c, 1)
    nt = r // tr

    def body(w_ref, m_ref, v_ref, g0_ref, g1_ref, g_ref, d_ref, mo_ref, vo_ref):
        g = jnp.where(pl.program_id(0) == 0, g0_ref[...], g1_ref[...])
        g_ref[0] = g
        d_ref[0], mo_ref[0], vo_ref[0] = _adam_math(w_ref[0], g, m_ref[0], v_ref[0])

    blk = pl.BlockSpec((1, tr, c), lambda l, i: (l, i, 0))
    part0 = pl.BlockSpec((tr, c), lambda l, i: (jnp.where(l == 0, i, nt - 1), 0))
    part1 = pl.BlockSpec((tr, c), lambda l, i: (jnp.where(l == 1, i, 0), 0))
    return _run(body, name=name, grid=(nl, nt), in_specs=[blk] * 3 + [part0, part1],
                out_specs=[blk] * 4, out_shape=[jax.ShapeDtypeStruct((nl, r, c), F32)] * 4,
                args=(w, m, v, grads[0], grads[1]), comm=comm)


_BIG = ("w_in", "w_a_out", "w_b_out", "w_pool", "w_o", "w_up", "w_down")
_COL_SHARDED = ("w_in", "w_up")
_SMALL_SHARDED = ("conv_a", "conv_ffn")
_SMALL_REPL = ("b_in", "ln_v_g", "ln_v_b", "w_spatial", "b_spatial", "pool_scale", "ln1_g", "ln1_b", "b_up",
               "conv_ffn_b", "ln2_g", "ln2_b")
_WEIGHTS = ("w_ada", "b_ada", "w_in", "b_in", "conv_a", "w_a_out", "ln_v_g", "ln_v_b", "w_spatial", "b_spatial",
            "w_b_out", "w_pool", "pool_scale", "w_o", "ln1_g", "ln1_b", "w_up", "b_up", "conv_ffn", "conv_ffn_b",
            "w_down", "ln2_g", "ln2_b")


def _shard3(a):
    return a.reshape(a.shape[0], -1, a.shape[-1])


def _use_gathered(name, g):
    g = g.reshape(N_CHIPS, -1, g.shape[-1])
    if name in _COL_SHARDED:
        return g
    if name == "w_pool":
        return g.reshape(N_CHIPS, 4, POOL_GROUP // N_CHIPS, POOL_GROUP).transpose(1, 0, 2, 3).reshape(
            4, POOL_GROUP, POOL_GROUP)
    return g.reshape(-1, g.shape[-1])


def _grad_by_chip(name, g):
    if name in _COL_SHARDED:
        return g
    if name == "w_pool":
        return g.reshape(4, N_CHIPS, POOL_GROUP // N_CHIPS, POOL_GROUP).transpose(1, 0, 2, 3).reshape(
            N_CHIPS, POOL_GROUP, POOL_GROUP)
    return g.reshape(N_CHIPS, -1, g.shape[-1])


def _pack_small(arrs):
    parts = []
    for a in arrs:
        flat = a.reshape(-1).astype(F32)
        pad = (-flat.shape[0]) % 128
        parts.append(jnp.pad(flat, (0, pad)) if pad else flat)
    flat = jnp.concatenate(parts)
    pad = (-flat.shape[0]) % 2048
    if pad:
        flat = jnp.pad(flat, (0, pad))
    return flat.reshape(-1, 128)


def _unpack_small(buf, shapes):
    lead = buf.shape[:-2]
    flat = buf.reshape(lead + (-1,))
    out, off = [], 0
    for shp in shapes:
        n = math.prod(shp)
        out.append(flat[..., off:off + n].reshape(lead + tuple(shp)))
        off += n + ((-n) % 128)
    return out


def _as2d(a):
    return a.reshape(-1, a.shape[-1])


_LATE = ("w_a_out", "w_b_out", "w_pool", "w_o")


class _Traffic:
    def __init__(self, slots, plan, core, chip):
        self.slots = slots
        self.plan = plan
        self.core = core
        self.place = jnp.concatenate([core, chip])
        self.gathered = {}
        self.ready = {}
        self.summed = {}
        self.half = {}
        self.final = {}

    def weight(self, layer, name):
        return self.gathered[(layer, name)]

    def add_grad(self, layer, name, g_f32, g_bf16):
        def halves(g):
            g = _grad_by_chip(name, g)
            return g.reshape(N_CHIPS, 2, g.shape[1] // 2, g.shape[2])
        self.ready[(layer, name)] = (halves(g_f32), halves(g_bf16))

    def _comm(self, job):
        if job[0] == "gather":
            return _gather_comm([self.slots[(job[1], k)] for k in job[2]], *job[3:])
        if job[0] == "presum":
            return _presum_comm([self.ready[k][1] for k in job[1]])
        if job[0] == "scatter":
            return _scatter_comm([self.summed[k][0] for k in job[1]])
        return _join_comm([self.half[k] for k in job[1]])

    def _done(self, job, res):
        if job[0] == "gather":
            for k, r in zip(job[2], res):
                self.slots[(job[1], k)] = r
                self.gathered[(job[1], k)] = _use_gathered(k, r)
        elif job[0] == "presum":
            for k, r in zip(job[1], res):
                nm = f"presum_l{k[0]}_{k[1]}"
                self.summed[k] = self.run(nm, lambda cm: _sum_halves(self.ready.pop(k)[0], r, self.place, name=nm,
                                                                     comm=cm))
        elif job[0] == "scatter":
            for k, r in zip(job[1], res):
                self.half[k] = _sum4_into_half(self.summed.pop(k)[1], r, self.core, name=f"sum_l{k[0]}_{k[1]}")
        else:
            for k, r in zip(job[1], res):
                self.final[k] = r.reshape(-1, r.shape[-1])

    def run(self, name, fn):
        jobs = self.plan.get(name)
        if not jobs:
            return fn(None)
        comms = [self._comm(j) for j in jobs]
        outs, res = fn(_merge(comms))
        for job, r in zip(jobs, _split(comms, res)):
            self._done(job, r)
        return outs

    def alone(self, name):
        jobs = self.plan[name]
        comms = [self._comm(j) for j in jobs]
        for job, r in zip(jobs, _split(comms, _comm_call(_merge(comms), name=name))):
            self._done(job, r)


def _layer_fwd(x, ada, p, l, tr):
    sh1, sc1, gt1, sh2, sc2, gt2 = ada
    n = f"l{l}"
    z, ht = tr.run(f"{n}_in_proj", lambda cm: _mod_matmul(
        x, sc1, sh1, tr.weight(l, "w_in"), p["b_in"], ts=1024, tn=2304, name=f"{n}_in_proj", comm=cm))
    a, ya = tr.run(f"{n}_mix_a", lambda cm: _mix_a_fwd(z, p["conv_a"], tr.weight(l, "w_a_out"), ts=256,
                                                      name=f"{n}_mix_a", comm=cm))
    sg, yb = tr.run(f"{n}_mix_b", lambda cm: _mix_b_fwd(
        z, p["ln_v_g"], p["ln_v_b"], p["w_spatial"], p["b_spatial_t"], tr.weight(l, "w_b_out"), ts=256,
        name=f"{n}_mix_b", comm=cm))
    dpool, yc = tr.run(f"{n}_mix_c", lambda cm: _mix_c_fwd(z, tr.weight(l, "w_pool"), p["pool_scale"], ts=256,
                                                          name=f"{n}_mix_c", comm=cm))
    merged, o, x1 = tr.run(f"{n}_mix_o", lambda cm: _mix_o_fwd(
        x, z, ya, yb, yc, tr.weight(l, "w_o"), gt1, p["ln1_g"], p["ln1_b"], ts=256, name=f"{n}_mix_o", comm=cm))
    up, h2t = tr.run(f"{n}_up_proj", lambda cm: _mod_matmul(
        x1, sc2, sh2, tr.weight(l, "w_up"), p["b_up"], ts=1024, tn=1408, name=f"{n}_up_proj", comm=cm))
    ft, dn, x2 = tr.run(f"{n}_ffn", lambda cm: _ffn_fwd(
        up, x1, p["conv_ffn"], p["conv_ffn_b"], tr.weight(l, "w_down"), gt2, p["ln2_g"], p["ln2_b"], ts=256,
        name=f"{n}_ffn", comm=cm))
    saved = dict(x=x, z=z, ht=ht, a=a, ya=ya, sg=sg, yb=yb, dpool=dpool, yc=yc, merged=merged, o=o, x1=x1, h2t=h2t,
                 up=up, ft=ft, dn=dn)
    return x2, saved


def _layer_bwd(dx2, ada, p, sv, l, tr):
    sh1, sc1, gt1, sh2, sc2, gt2 = ada
    n = f"l{l}"
    ddn, dup, dx1, red_d, red_f, dbup = tr.run(f"{n}_ffn_bwd", lambda cm: _ffn_bwd(
        dx2, sv["x1"], sv["dn"], sv["up"], p["conv_ffn"], p["conv_ffn_b"], tr.weight(l, "w_down"),
        tr.weight(l, "w_up"), gt2, p["ln2_g"], sc2, ts=256, name=f"{n}_ffn_bwd", comm=cm))
    g = {}
    tr.add_grad(l, "w_down", *_grad_matmul_t(sv["ft"], ddn, tk=D_FF // N_CHIPS, tn=D_MODEL, name=f"{n}_dw_down"))
    tr.add_grad(l, "w_up", *tr.run(f"{n}_dw_up", lambda cm: _grad_matmul_t(
        sv["h2t"], dup, tk=D_MODEL, tn=FF_CHUNK, name=f"{n}_dw_up", by_chip=True, comm=cm)))
    g["ln2_g"], g["ln2_b"] = red_d[0], red_d[1]
    g["conv_ffn"], g["conv_ffn_b"], g["b_up"] = red_f[0:3], red_f[3], dbup[0]

    dxa, dz, dya, dyb, dyc, red_o, dwo, dwo_b = tr.run(f"{n}_mix_o_bwd", lambda cm: _mix_o_bwd(
        dx1, sv["x"], sv["o"], sv["z"], sv["ya"], sv["yb"], sv["yc"], sv["merged"], tr.weight(l, "w_o"), gt1,
        p["ln1_g"], ts=256, name=f"{n}_mix_o_bwd", comm=cm))
    tr.add_grad(l, "w_o", dwo, dwo_b)
    g["ln1_g"], g["ln1_b"] = red_o[0], red_o[1]

    dz, red_a, dwa, dwa_b = tr.run(f"{n}_mix_a_bwd", lambda cm: _mix_a_bwd(
        dya, sv["a"], sv["z"], dz, p["conv_a"], tr.weight(l, "w_a_out"), ts=256, name=f"{n}_mix_a_bwd", comm=cm))
    tr.add_grad(l, "w_a_out", dwa, dwa_b)
    g["conv_a"] = red_a[0:3]

    dz, red_b, dws, dbst, dwb, dwb_b = tr.run(f"{n}_mix_b_bwd", lambda cm: _mix_b_bwd(
        dyb, sv["sg"], sv["z"], dz, p["ln_v_g"], p["ln_v_b"], p["w_spatial"], p["b_spatial_t"],
        tr.weight(l, "w_b_out"), ts=256, name=f"{n}_mix_b_bwd", comm=cm))
    tr.add_grad(l, "w_b_out", dwb, dwb_b)
    g["ln_v_g"], g["ln_v_b"], g["w_spatial"], g["b_spatial"] = red_b[0], red_b[1], dws, dbst.T

    dz, red_c, dwp = _mix_c_bwd(dyc, sv["z"], dz, tr.weight(l, "w_pool"), p["pool_scale"], ts=256,
                                name=f"{n}_mix_c_bwd")
    g["pool_scale"] = red_c[0]
    tr.add_grad(l, "w_pool", dwp, dwp.astype(BF16))

    tr.add_grad(l, "w_in", *tr.run(f"{n}_dw_in", lambda cm: _grad_matmul_t(
        sv["ht"], dz, tk=D_MODEL, tn=1152, name=f"{n}_dw_in", by_chip=True, comm=cm)))
    if f"{n}_presum_tail" in tr.plan:
        tr.alone(f"{n}_presum_tail")
    dx, red_i, dbin = tr.run(f"{n}_in_proj_bwd", lambda cm: _in_proj_bwd(
        dz, tr.weight(l, "w_in"), dxa, sv["x"], sc1, ts=256, name=f"{n}_in_proj_bwd", comm=cm))
    g["b_in"] = dbin[0]
    dada = jnp.stack([red_i[1], red_i[0], red_o[2], red_d[4], red_d[3], red_d[2]])
    return dx, g, dada


def _traffic_plan():
    plan = {
        "gather_l0": [("gather", 0, ("w_in",) + _LATE)],
        "l0_in_proj": [("gather", 1, ("w_in",), (1, 2))],
        "l0_mix_a": [("gather", 0, ("w_up",), (1,))],
        "l0_mix_b": [("gather", 0, ("w_up",), (2,)), ("gather", 0, ("w_down",), (1,))],
        "l0_mix_c": [("gather", 0, ("w_up",), (3,))],
        "l0_mix_o": [("gather", 0, ("w_down",), (2, 3)), ("gather", 1, _LATE, (1,))],
        "l0_up_proj": [("gather", 1, ("w_in",), (3,)), ("gather", 1, _LATE, (2, 3))],
        "l0_ffn": [("gather", 1, ("w_down",)), ("gather", 1, ("w_up",), (1,))],
        "l1_in_proj": [("gather", 1, ("w_up",), (2, 3))],
    }
    for l in reversed(range(DEPTH)):
        late = [(l, k) for k in _LATE]
        plan.update({
            f"l{l}_mix_o_bwd": [("presum", [(l, "w_down"), (l, "w_up")])],
            f"l{l}_mix_b_bwd": [("scatter", [(l, "w_down"), (l, "w_up")])],
            f"l{l}_dw_in": [("presum", late), ("join", [(l, "w_down"), (l, "w_up")])],
        })
    late0, late1 = [(0, k) for k in _LATE], [(1, k) for k in _LATE]
    plan["l1_in_proj_bwd"] = [("presum", [(1, "w_in")]), ("scatter", late1)]
    plan["l0_ffn_bwd"] = [("scatter", [(1, "w_in")]), ("join", late1)]
    plan["l0_dw_up"] = [("join", [(1, "w_in")])]
    plan["l0_presum_tail"] = [("presum", [(0, "w_in")])]
    plan["presum_l0_w_in"] = [("scatter", late0)]
    plan["l0_in_proj_bwd"] = [("scatter", [(0, "w_in")])]
    plan["join_tail"] = [("join", [(0, "w_in")] + late0)]
    return plan


def kernel(x, c, w_ada, b_ada, w_in, b_in, conv_a, w_a_out, ln_v_g, ln_v_b, w_spatial, b_spatial, w_b_out, w_pool, pool_scale, w_o, ln1_g, ln1_b, w_up, b_up, conv_ffn, conv_ffn_b, w_down, ln2_g, ln2_b, loss_target, m_w_ada, m_b_ada, m_w_in, m_b_in, m_conv_a, m_w_a_out, m_ln_v_g, m_ln_v_b, m_w_spatial, m_b_spatial, m_w_b_out, m_w_pool, m_pool_scale, m_w_o, m_ln1_g, m_ln1_b, m_w_up, m_b_up, m_conv_ffn, m_conv_ffn_b, m_w_down, m_ln2_g, m_ln2_b, v_w_ada, v_b_ada, v_w_in, v_b_in, v_conv_a, v_w_a_out, v_ln_v_g, v_ln_v_b, v_w_spatial, v_b_spatial, v_w_b_out, v_w_pool, v_pool_scale, v_w_o, v_ln1_g, v_ln1_b, v_w_up, v_b_up, v_conv_ffn, v_conv_ffn_b, v_w_down, v_ln2_g, v_ln2_b):
    args = locals()
    w = {k: args[k] for k in _WEIGHTS}
    m = {k: args["m_" + k] for k in _WEIGHTS}
    v = {k: args["v_" + k] for k in _WEIGHTS}
    d = D_MODEL
    mx, my, mc = _my_coords()
    chip = 2 * mx + my
    me = 4 * mx + 2 * my + mc

    small_shapes = [c.shape, conv_a.shape, conv_ffn.shape]
    small_all = _all_gather8(_pack_small([c, conv_a, conv_ffn]), name="gather_small")
    c_all, conv_a_st, conv_ffn_st = _unpack_small(small_all, small_shapes)
    c_all = c_all.reshape(N_DEV, d)
    conv_full = {"conv_a": jnp.concatenate([conv_a_st[2 * j] for j in range(N_CHIPS)], axis=-1),
                 "conv_ffn": jnp.concatenate([conv_ffn_st[2 * j] for j in range(N_CHIPS)], axis=-1)}

    chip_idx = jnp.reshape(chip, (1,)).astype(jnp.int32)
    slots = {}
    for l in range(DEPTH):
        bufs = _cast_into_slots([_shard3(w[k]) for k in _BIG], l, chip_idx, name=f"cast_l{l}")
        slots.update({(l, k): b for k, b in zip(_BIG, bufs)})
    tr = _Traffic(slots, _traffic_plan(), jnp.reshape(mc, (1,)).astype(jnp.int32), chip_idx)
    tr.alone("gather_l0")

    n_ada = w_ada.shape[2]
    b_ada_mine = lax.dynamic_slice_in_dim(b_ada, chip * n_ada, n_ada, axis=1)
    ada_part = _ada_fwd(c_all, w_ada, b_ada_mine.reshape(DEPTH, 1, n_ada), name="ada_fwd")
    ada_all = _all_gather8(_pack_small([ada_part]), name="gather_ada")
    ada_st = _unpack_small(ada_all, [ada_part.shape])[0][0::2]
    ada_rows = jnp.concatenate([ada_st[j] for j in range(N_CHIPS)], axis=-1)
    ada_mine = lax.dynamic_index_in_dim(ada_rows, me, axis=1, keepdims=False)

    def layer_params(l):
        p = {k: conv_full[k][l] for k in _SMALL_SHARDED}
        for k in ("b_in", "ln_v_g", "ln_v_b", "pool_scale", "ln1_g", "ln1_b", "b_up", "conv_ffn_b", "ln2_g", "ln2_b"):
            p[k] = w[k][l].reshape(1, -1)
        p["w_spatial"] = w_spatial[l]
        p["b_spatial_t"] = b_spatial[l].T
        return p

    xs = x[0]
    saved, adas, params = [], [], []
    for l in range(DEPTH):
        ada = [ada_mine[l, k * d:(k + 1) * d].reshape(1, d) for k in range(6)]
        p = layer_params(l)
        xs, sv = _layer_fwd(xs, ada, p, l, tr)
        saved.append(sv), adas.append(ada), params.append(p)
    dx, loss_blk = _loss_fwd(xs, loss_target[0], ts=512, name="loss")

    grads, dadas = [None] * DEPTH, [None] * DEPTH
    for l in reversed(range(DEPTH)):
        dx, grads[l], dadas[l] = _layer_bwd(dx, adas[l], params[l], saved[l], l, tr)
    tr.alone("join_tail")
    dada = jnp.stack(dadas).reshape(DEPTH, 6 * d)

    small_names = _SMALL_REPL + _SMALL_SHARDED
    small_g = [jnp.stack([grads[l][k] for l in range(DEPTH)]) for k in small_names]
    gsum = dict(zip(small_names, _unpack_small(_all_reduce_small(_pack_small(small_g), name="reduce_small"),
                                               [a.shape for a in small_g])))
    tail_g = [dada, loss_blk[0:1, 0:1]]
    tail_all, tail_sum = _all_gather8(_pack_small(tail_g), name="gather_dada", with_sum=True)
    gsum["b_ada"], loss_sum = _unpack_small(tail_sum, [a.shape for a in tail_g])
    loss = loss_sum[0, 0]
    dada_all = _unpack_small(tail_all, [a.shape for a in tail_g])[0]
    for k in _SMALL_SHARDED:
        wd = gsum[k].shape[-1] // N_CHIPS
        gsum[k] = lax.dynamic_slice_in_dim(gsum[k], chip * wd, wd, axis=gsum[k].ndim - 1)

    dada_cols = lax.dynamic_slice_in_dim(dada_all, chip * n_ada, n_ada, axis=2)
    dada_cols = jnp.pad(jnp.swapaxes(dada_cols, 0, 1), ((0, 0), (0, N_DEV), (0, 0)))
    gsum["w_ada"] = _ada_bwd(jnp.pad(c_all, ((0, N_DEV), (0, 0))), dada_cols, name="ada_bwd")

    out_g, out_d, out_m, out_v = {}, {}, {}, {}
    for k in _WEIGHTS:
        shp = w[k].shape
        if k in _BIG:
            res = tr.run(f"adamw_{k}", lambda cm: _adamw_sharded(
                _shard3(w[k]), _shard3(m[k]), _shard3(v[k]), [tr.final[(l, k)] for l in range(DEPTH)],
                name=f"adamw_{k}", comm=cm))
        else:
            gk = gsum[k].reshape(shp)
            res = [gk] + list(_adamw(_as2d(w[k]), _as2d(gk), _as2d(m[k]), _as2d(v[k]), name=f"adamw_{k}"))
        out_g[k], out_d[k], out_m[k], out_v[k] = [r.reshape(shp) for r in res]

    return (loss, dx[None], *[out_g[k] for k in _WEIGHTS], *[out_d[k] for k in _WEIGHTS],
            *[out_m[k] for k in _WEIGHTS], *[out_v[k] for k in _WEIGHTS])
```

```python
import math
from typing import Callable, NamedTuple

import jax
import jax.numpy as jnp
from jax import lax
from jax.experimental import pallas as pl
from jax.experimental.pallas import tpu as pltpu

F32 = jnp.float32
BF16 = jnp.bfloat16

D_MODEL = 1024
D_Z = 9216
D_FF = 2816
N_GROUPS = 8
GBLK = 128
CHUNK = 64
POOL_WINDOWS = (2, 4, 8, 16)
POOL_GROUP = 256
POOL_HALO = 16
CONV_HALO = 8
HALO_ROWS = 16
DEPTH = 2
ALPHA = (2 * DEPTH) ** 0.25
LN_EPS = 1e-5
ADAM_LR, ADAM_B1, ADAM_B2, ADAM_EPS, ADAM_WD, ADAM_STEP = 0.001, 0.9, 0.999, 1e-08, 0.01, 10
N_CHIPS = 4
N_DEV = 8
FF_CHUNK = 1408
MESH = pl.DeviceIdType.MESH
VMEM_LIMIT = 56 * 1024 * 1024
HBM = pl.BlockSpec(memory_space=pl.ANY)


def _dot(a, b):
    return jnp.dot(a, b, preferred_element_type=F32)


def _dot_nt(a, b):
    return lax.dot_general(a, b, (((1,), (1,)), ((), ())), preferred_element_type=F32)


def _dot_tn(a, b):
    return lax.dot_general(a, b, (((0,), (0,)), ((), ())), preferred_element_type=F32)


_GELU_C = math.sqrt(2.0 / math.pi)


def _gelu_and_grad(x):
    x2 = x * x
    t = jnp.tanh(_GELU_C * (x + 0.044715 * x * x2))
    g = 0.5 * x * (1.0 + t)
    dg = 0.5 * (1.0 + t) + 0.5 * x * (1.0 - t * t) * (_GELU_C * (1.0 + 3 * 0.044715 * x2))
    return g, dg


def _gelu(x):
    return 0.5 * x * (1.0 + jnp.tanh(_GELU_C * (x + 0.044715 * x * x * x)))


def _sigmoid(x):
    return 1.0 / (1.0 + jnp.exp(-x))


def _ln_fwd(r):
    mu = jnp.mean(r, axis=-1, keepdims=True)
    xc = r - mu
    var = jnp.mean(xc * xc, axis=-1, keepdims=True)
    rstd = lax.rsqrt(var + LN_EPS)
    return xc * rstd, rstd


def _ln_bwd(dy, g, xhat, rstd):
    dxh = dy * g
    m1 = jnp.mean(dxh, axis=-1, keepdims=True)
    m2 = jnp.mean(dxh * xhat, axis=-1, keepdims=True)
    return rstd * (dxh - m1 - xhat * m2)


def _rows_before(ext, k, halo):
    return pltpu.roll(ext, k, 0)[halo:]


def _rows_after(ext, k, n):
    return pltpu.roll(ext, ext.shape[0] - k, 0)[:n]


def _colsum(v):
    return jnp.sum(v, axis=0, keepdims=True)


def _spatial_mask():
    i = lax.broadcasted_iota(jnp.int32, (GBLK, GBLK), 0)
    j = lax.broadcasted_iota(jnp.int32, (GBLK, GBLK), 1)
    return (j // CHUNK) <= (i // CHUNK)


def _const(shape):
    n = len(shape)
    return pl.BlockSpec(shape, lambda *_: (0,) * n)


def _resident(shape):
    n = len(shape)
    return pl.BlockSpec(shape, lambda *_: (0,) * n, pipeline_mode=pl.Buffered(1))


def _tile(ts, s):
    return min(ts, s)


class _Comm(NamedTuple):
    srcs: tuple
    dsts: tuple
    n_remote: int
    n_local: int
    build: Callable
    alias: tuple = ()


def _my_coords():
    return lax.axis_index("x"), lax.axis_index("y"), lax.axis_index("c")


def _chip_peer(k):
    mx, my, mc = _my_coords()
    return (mx ^ ((k >> 1) & 1), my ^ (k & 1), mc)


def _sem_scratch(comm):
    return [pltpu.SemaphoreType.DMA((max(comm.n_remote, 1),)), pltpu.SemaphoreType.DMA((max(comm.n_remote, 1),)),
            pltpu.SemaphoreType.DMA((max(comm.n_local, 1),))]


def _run(body, *, name, grid, in_specs, out_specs, out_shape, args, scratch_shapes=(), comm=None, aliases=None,
         prefetch=None):
    sem = ("arbitrary",) * len(grid)
    cparams = pltpu.CompilerParams(dimension_semantics=sem, vmem_limit_bytes=VMEM_LIMIT)
    aliases = dict(aliases or {})
    n_pre = 0 if prefetch is None else 1

    def call(fn, in_specs, out_specs, out_shape, scratch_shapes, args):
        if prefetch is None:
            return pl.pallas_call(fn, name=name, grid=grid, in_specs=in_specs, out_specs=out_specs, out_shape=out_shape,
                                  scratch_shapes=scratch_shapes, compiler_params=cparams,
                                  input_output_aliases=aliases)(*args)
        spec = pltpu.PrefetchScalarGridSpec(num_scalar_prefetch=1, grid=grid, in_specs=in_specs, out_specs=out_specs,
                                            scratch_shapes=scratch_shapes)
        return pl.pallas_call(fn, name=name, grid_spec=spec, out_shape=out_shape, compiler_params=cparams,
                              input_output_aliases={k + 1: v for k, v in aliases.items()})(prefetch, *args)

    if comm is None:
        return call(body, list(in_specs), list(out_specs), list(out_shape), list(scratch_shapes), args)
    n_in, n_cs, n_out, n_cd, n_scr = len(in_specs), len(comm.srcs), len(out_specs), len(comm.dsts), len(scratch_shapes)
    aliases.update({n_in + si: n_out + di for si, di in comm.alias})
    total = math.prod(grid)
    mid_step = min(total - 1, int(total * 0.9))

    def wrapped(*refs):
        pre, refs = refs[:n_pre], refs[n_pre:]
        ins, refs = refs[:n_in], refs[n_in:]
        csrc, refs = refs[:n_cs], refs[n_cs:]
        outs, refs = refs[:n_out], refs[n_out:]
        cdst, refs = refs[:n_cd], refs[n_cd:]
        scr, sems = refs[:n_scr], refs[n_scr:]
        step = pl.program_id(0)
        for ax in range(1, len(grid)):
            step = step * grid[ax] + pl.program_id(ax)
        first, mid, last = comm.build(csrc, cdst, *sems, 0, 0)
        pl.when(step == 0)(first)
        if mid is not None:
            pl.when(step == mid_step)(mid)
        body(*pre, *ins, *outs, *scr)
        pl.when(step == total - 1)(last)

    res = call(wrapped, list(in_specs) + [HBM] * n_cs, list(out_specs) + [HBM] * n_cd,
               list(out_shape) + list(comm.dsts), list(scratch_shapes) + _sem_scratch(comm), (*args, *comm.srcs))
    return res[:n_out], res[n_out:]


def _comm_call(comm, *, name):
    def body(*refs):
        n_cs, n_cd = len(comm.srcs), len(comm.dsts)
        first, mid, last = comm.build(refs[:n_cs], refs[n_cs:n_cs + n_cd], *refs[n_cs + n_cd:], 0, 0)
        first()
        if mid is not None:
            mid()
        last()

    return pl.pallas_call(body, name=name, in_specs=[HBM] * len(comm.srcs), out_specs=[HBM] * len(comm.dsts),
                          out_shape=list(comm.dsts), scratch_shapes=_sem_scratch(comm),
                          input_output_aliases=dict(comm.alias))(*comm.srcs)


def _gather_comm(bufs, peers=(1, 2, 3)):
    dsts = tuple(jax.ShapeDtypeStruct(b.shape, b.dtype) for b in bufs)
    nw = len(bufs)

    def build(srcs, outs, send_sems, recv_sems, local_sems, r0, l0):
        mx, my, mc = _my_coords()
        me = 2 * mx + my
        sibling = (mx, my, 1 - mc)

        def rdma(src, dst, idx, peer):
            return pltpu.make_async_remote_copy(src_ref=src, dst_ref=dst, send_sem=send_sems.at[r0 + idx],
                                                recv_sem=recv_sems.at[r0 + idx], device_id=peer, device_id_type=MESH)

        def ici(w, k, slot):
            return rdma(outs[w].at[me, mc], outs[w].at[slot, mc], 6 * w + k - 1, _chip_peer(k))

        def fwd(w, k, half):
            return rdma(outs[w].at[me ^ k, mc], outs[w].at[me ^ k, half], 6 * w + 2 + k, sibling)

        def first():
            for w in range(nw):
                for k in peers:
                    ici(w, k, me).start()

        def mid():
            for w in range(nw):
                for k in peers:
                    ici(w, k, me ^ k).wait_recv()
                    fwd(w, k, mc).start()

        def last():
            for w in range(nw):
                for k in peers:
                    fwd(w, k, 1 - mc).wait_recv()
                    ici(w, k, me).wait_send()
                    fwd(w, k, mc).wait_send()

        return first, mid, last

    return _Comm(tuple(bufs), dsts, 6 * nw, 0, build, tuple((w, w) for w in range(nw)))


def _symmetric(make_remote, make_local, make_incoming=None):
    def first():
        for cp in make_remote() + make_local():
            cp.start()

    def last():
        for cp in (make_incoming or make_remote)():
            cp.wait_recv()
        for cp in make_remote():
            cp.wait_send()
        for cp in make_local():
            cp.wait()

    return first, None, last


def _presum_comm(g_bf16):
    nw = len(g_bf16)
    dsts = tuple(jax.ShapeDtypeStruct((N_CHIPS,) + g.shape[2:], BF16) for g in g_bf16)

    def build(srcs, outs, send_sems, recv_sems, local_sems, r0, l0):
        mx, my, mc = _my_coords()

        def remote():
            return [pltpu.make_async_remote_copy(
                src_ref=srcs[w].at[j, 1 - mc], dst_ref=outs[w].at[j], send_sem=send_sems.at[r0 + N_CHIPS * w + j],
                recv_sem=recv_sems.at[r0 + N_CHIPS * w + j], device_id=(mx, my, 1 - mc), device_id_type=MESH)
                for w in range(nw) for j in range(N_CHIPS)]

        return _symmetric(remote, lambda: [])

    return _Comm(tuple(g_bf16), dsts, N_CHIPS * nw, 0, build)


def _scatter_comm(h_bf16):
    nw = len(h_bf16)
    dsts = tuple(jax.ShapeDtypeStruct((N_CHIPS - 1,) + h.shape[1:], BF16) for h in h_bf16)

    def build(srcs, outs, send_sems, recv_sems, local_sems, r0, l0):
        mx, my, _ = _my_coords()
        me = 2 * mx + my

        def remote():
            return [pltpu.make_async_remote_copy(
                src_ref=srcs[w].at[me ^ k], dst_ref=outs[w].at[k - 1], send_sem=send_sems.at[r0 + 3 * w + k - 1],
                recv_sem=recv_sems.at[r0 + 3 * w + k - 1], device_id=_chip_peer(k), device_id_type=MESH)
                for w in range(nw) for k in range(1, N_CHIPS)]

        return _symmetric(remote, lambda: [])

    return _Comm(tuple(h_bf16), dsts, 3 * nw, 0, build)


def _join_comm(bufs):
    nw = len(bufs)
    dsts = tuple(jax.ShapeDtypeStruct(b.shape, b.dtype) for b in bufs)

    def build(srcs, outs, send_sems, recv_sems, local_sems, r0, l0):
        mx, my, mc = _my_coords()

        def remote(half=mc):
            return [pltpu.make_async_remote_copy(
                src_ref=outs[w].at[mc], dst_ref=outs[w].at[half], send_sem=send_sems.at[r0 + w],
                recv_sem=recv_sems.at[r0 + w], device_id=(mx, my, 1 - mc), device_id_type=MESH) for w in range(nw)]

        return _symmetric(remote, lambda: [], lambda: remote(1 - mc))

    return _Comm(tuple(bufs), dsts, nw, 0, build, tuple((w, w) for w in range(nw)))


def _merge(comms):
    comms = list(comms)
    if len(comms) == 1:
        return comms[0]

    def build(srcs, outs, send_sems, recv_sems, local_sems, r0, l0):
        phases, s0, d0 = [], 0, 0
        for cm in comms:
            phases.append(cm.build(srcs[s0:s0 + len(cm.srcs)], outs[d0:d0 + len(cm.dsts)], send_sems, recv_sems,
                                   local_sems, r0, l0))
            s0, d0, r0, l0 = s0 + len(cm.srcs), d0 + len(cm.dsts), r0 + cm.n_remote, l0 + cm.n_local

        def run(idx):
            fns = [ph[idx] for ph in phases if ph[idx] is not None]
            if not fns:
                return None

            def go():
                for fn in fns:
                    fn()
            return go

        return run(0), run(1), run(2)

    alias, s0, d0 = [], 0, 0
    for cm in comms:
        alias += [(s0 + si, d0 + di) for si, di in cm.alias]
        s0, d0 = s0 + len(cm.srcs), d0 + len(cm.dsts)
    return _Comm(sum((cm.srcs for cm in comms), ()), sum((cm.dsts for cm in comms), ()),
                 sum(cm.n_remote for cm in comms), sum(cm.n_local for cm in comms), build, tuple(alias))


def _split(comms, res):
    out, d0 = [], 0
    for cm in comms:
        out.append(list(res[d0:d0 + len(cm.dsts)]))
        d0 += len(cm.dsts)
    return out


def _all_reduce_small(x, *, name):
    r, lanes = x.shape
    half = r // 2
    assert half % 8 == 0

    def body(x_ref, out_ref, sib_ref, slots_ref, send_sems, recv_sems):
        mx, my, mc = _my_coords()
        me = 2 * mx + my
        sibling = (mx, my, 1 - mc)
        mine = pl.ds(pl.multiple_of(mc * half, 8), half)
        theirs = pl.ds(pl.multiple_of((1 - mc) * half, 8), half)

        def to_sibling(src, dst, idx):
            return pltpu.make_async_remote_copy(src_ref=src, dst_ref=dst, send_sem=send_sems.at[idx],
                                                recv_sem=recv_sems.at[idx], device_id=sibling, device_id_type=MESH)

        swap = to_sibling(x_ref.at[theirs], sib_ref, 0)
        swap.start()
        swap.wait_recv()
        swap.wait_send()
        slots_ref[me] = x_ref[mine, :] + sib_ref[...]

        def copy(k, slot):
            return pltpu.make_async_remote_copy(
                src_ref=slots_ref.at[me], dst_ref=slots_ref.at[slot], send_sem=send_sems.at[k], recv_sem=recv_sems.at[k],
                device_id=_chip_peer(k), device_id_type=MESH)

        sends = [copy(k, me) for k in range(1, N_CHIPS)]
        for cp in sends:
            cp.start()
        for k in range(1, N_CHIPS):
            copy(k, me ^ k).wait_recv()
        for cp in sends:
            cp.wait_send()
        acc = slots_ref[0]
        for j in range(1, N_CHIPS):
            acc = acc + slots_ref[j]
        out_ref[mine, :] = acc
        join = to_sibling(out_ref.at[mine], out_ref.at[mine], N_CHIPS)
        join.start()
        to_sibling(out_ref.at[mine], out_ref.at[theirs], N_CHIPS).wait_recv()
        join.wait_send()

    vmem = pl.BlockSpec(memory_space=pltpu.VMEM)
    return pl.pallas_call(
        body, name=name, in_specs=[vmem], out_specs=vmem, out_shape=jax.ShapeDtypeStruct((r, lanes), F32),
        scratch_shapes=[pltpu.VMEM((half, lanes), F32), pltpu.VMEM((N_CHIPS, half, lanes), F32),
                        pltpu.SemaphoreType.DMA((N_CHIPS + 1,)), pltpu.SemaphoreType.DMA((N_CHIPS + 1,))],
        compiler_params=pltpu.CompilerParams(vmem_limit_bytes=VMEM_LIMIT),
    )(x)


def _all_gather8(x, *, name, with_sum=False):
    r, lanes = x.shape

    def body(x_ref, out_ref, *rest):
        if with_sum:
            sum_ref, send_sems, recv_sems, local_sem = rest
        else:
            send_sems, recv_sems, local_sem = rest
        mx, my, mc = _my_coords()
        me = 4 * mx + 2 * my + mc

        def peer(k):
            return (mx ^ ((k >> 2) & 1), my ^ ((k >> 1) & 1), mc ^ (k & 1))

        def copy(k, slot):
            return pltpu.make_async_remote_copy(
                src_ref=x_ref, dst_ref=out_ref.at[slot], send_sem=send_sems.at[k - 1], recv_sem=recv_sems.at[k - 1],
                device_id=peer(k), device_id_type=MESH)

        mine = pltpu.make_async_copy(x_ref, out_ref.at[me], local_sem)
        mine.start()
        sends = [copy(k, me) for k in range(1, N_DEV)]
        for cp in sends:
            cp.start()
        for k in range(1, N_DEV):
            copy(k, me ^ k).wait_recv()
        for cp in sends:
            cp.wait_send()
        mine.wait()
        if with_sum:
            acc = out_ref[0]
            for k in range(1, N_DEV):
                acc = acc + out_ref[k]
            sum_ref[...] = acc

    vmem = pl.BlockSpec(memory_space=pltpu.VMEM)
    out_shape = [jax.ShapeDtypeStruct((N_DEV, r, lanes), F32)]
    if with_sum:
        out_shape.append(jax.ShapeDtypeStruct((r, lanes), F32))
    res = pl.pallas_call(
        body, name=name, in_specs=[vmem], out_specs=[vmem] * len(out_shape), out_shape=out_shape,
        scratch_shapes=[pltpu.SemaphoreType.DMA((N_DEV - 1,)), pltpu.SemaphoreType.DMA((N_DEV - 1,)),
                        pltpu.SemaphoreType.DMA],
        compiler_params=pltpu.CompilerParams(vmem_limit_bytes=VMEM_LIMIT),
    )(x)
    return res if with_sum else res[0]


def _mod_matmul(x, sc, sh, w4, b, *, ts, tn, name, comm=None):
    s, d = x.shape
    wd = w4.shape[2]
    n = N_CHIPS * wd
    per = wd // tn
    ts = _tile(ts, s)

    def body(x_ref, sc_ref, sh_ref, w_ref, b_ref, o_ref, ht_ref, h_scr):
        @pl.when(pl.program_id(1) == 0)
        def _():
            h = x_ref[...] * (1.0 + sc_ref[...]) + sh_ref[...]
            h_scr[...] = h.astype(BF16)
            ht_ref[...] = h.T.astype(BF16)
        o_ref[...] = (_dot(h_scr[...], w_ref[0]) + b_ref[...]).astype(BF16)

    return _run(
        body, name=name, grid=(s // ts, n // tn),
        in_specs=[pl.BlockSpec((ts, d), lambda i, j: (i, 0)), _const((1, d)), _const((1, d)),
                  pl.BlockSpec((1, d, tn), lambda i, j: (j // per, 0, j % per)),
                  pl.BlockSpec((1, tn), lambda i, j: (0, j))],
        out_specs=[pl.BlockSpec((ts, tn), lambda i, j: (i, j)), pl.BlockSpec((d, ts), lambda i, j: (0, i))],
        out_shape=[jax.ShapeDtypeStruct((s, n), BF16), jax.ShapeDtypeStruct((d, s), BF16)],
        scratch_shapes=[pltpu.VMEM((ts, d), BF16)],
        args=(x, sc, sh, w4, b), comm=comm)


def _conv3(q, ext, cw):
    return cw[2:3] * q + cw[1:2] * _rows_before(ext, 1, CONV_HALO) + cw[0:1] * _rows_before(ext, 2, CONV_HALO)


def _mix_a_fwd(z, cw, w_out, *, ts, name, comm=None):
    s = z.shape[0]
    d = D_MODEL
    ts = _tile(ts, s)

    def body(zb_ref, zc_ref, zx_ref, cw_ref, w_ref, a_ref, y_ref, carry):
        @pl.when(pl.program_id(0) == 0)
        def _():
            carry[...] = jnp.zeros_like(carry)
        q = zc_ref[...].astype(F32) * zx_ref[...].astype(F32)
        ext = jnp.concatenate([carry[...], q], axis=0)
        a = (zb_ref[...].astype(F32) * _conv3(q, ext, cw_ref[...])).astype(BF16)
        carry[...] = q[ts - CONV_HALO:]
        a_ref[...] = a
        y_ref[...] = _dot(a, w_ref[...])

    zspec = lambda k: pl.BlockSpec((ts, d), lambda i, k=k: (i, k))
    return _run(
        body, name=name, grid=(s // ts,),
        in_specs=[zspec(0), zspec(1), zspec(2), _const((3, d)), _const((d, d))],
        out_specs=[pl.BlockSpec((ts, d), lambda i: (i, 0))] * 2,
        out_shape=[jax.ShapeDtypeStruct((s, d), BF16), jax.ShapeDtypeStruct((s, d), F32)],
        scratch_shapes=[pltpu.VMEM((CONV_HALO, d), F32)],
        args=(z, z, z, cw, w_out), comm=comm)


def _spatial_mix(vn_b, ws_ref, bst_ref, mixed_scr, ts):
    nblk = ts // GBLK
    mask = _spatial_mask()
    for g in range(N_GROUPS):
        cols = slice(g * GBLK, (g + 1) * GBLK)
        wm = jnp.where(mask, ws_ref[g], 0.0).astype(BF16)
        cat = jnp.concatenate([vn_b[n * GBLK:(n + 1) * GBLK, cols] for n in range(nblk)], axis=1)
        res = _dot(wm, cat) + bst_ref[:, g:g + 1]
        for n in range(nblk):
            mixed_scr[n * GBLK:(n + 1) * GBLK, cols] = res[:, n * GBLK:(n + 1) * GBLK]


def _mix_b_fwd(z, ln_g, ln_b, ws, bst, w_out, *, ts, name, comm=None):
    s = z.shape[0]
    d = D_MODEL
    ts = _tile(ts, s)

    def body(zu_ref, zv_ref, g_ref, b_ref, ws_ref, bst_ref, w_ref, sg_ref, y_ref, mixed_scr):
        xhat, _ = _ln_fwd(_gelu(zv_ref[...].astype(F32)))
        vn = (xhat * g_ref[...] + b_ref[...]).astype(BF16)
        _spatial_mix(vn, ws_ref, bst_ref, mixed_scr, ts)
        sg = (_gelu(zu_ref[...].astype(F32)) * mixed_scr[...]).astype(BF16)
        sg_ref[...] = sg
        y_ref[...] = _dot(sg, w_ref[...])

    zspec = lambda k: pl.BlockSpec((ts, d), lambda i, k=k: (i, k))
    return _run(
        body, name=name, grid=(s // ts,),
        in_specs=[zspec(3), zspec(4), _const((1, d)), _const((1, d)), _const((N_GROUPS, GBLK, GBLK)),
                  _const((GBLK, N_GROUPS)), _const((d, d))],
        out_specs=[pl.BlockSpec((ts, d), lambda i: (i, 0))] * 2,
        out_shape=[jax.ShapeDtypeStruct((s, d), BF16), jax.ShapeDtypeStruct((s, d), F32)],
        scratch_shapes=[pltpu.VMEM((ts, d), F32)],
        args=(z, z, ln_g, ln_b, ws, bst, w_out), comm=comm)


def _pool_denoms(tile_idx, ts):
    t1 = (tile_idx * ts + 1 + lax.broadcasted_iota(jnp.int32, (ts, 1), 0)).astype(F32)
    return [jnp.minimum(t1, float(w)) for w in POOL_WINDOWS]


def _pool_diff(p, ext, denoms, k):
    cols = slice(k * POOL_GROUP, (k + 1) * POOL_GROUP)
    acc = ext[:, cols]
    step = 1
    while step < POOL_WINDOWS[k]:
        acc = acc + pltpu.roll(acc, step, 0)
        step *= 2
    return acc[POOL_HALO:] / denoms[k] - p[:, cols]


def _mix_c_fwd(z, w_pool, scale, *, ts, name, comm=None):
    s = z.shape[0]
    d = D_MODEL
    ts = _tile(ts, s)

    def body(zp_ref, w_ref, sc_ref, d_ref, y_ref, carry):
        i = pl.program_id(0)

        @pl.when(i == 0)
        def _():
            carry[...] = jnp.zeros_like(carry)
        p = zp_ref[...].astype(F32)
        ext = jnp.concatenate([carry[...], p], axis=0)
        carry[...] = p[ts - POOL_HALO:]
        denoms = _pool_denoms(i, ts)
        for k in range(len(POOL_WINDOWS)):
            cols = slice(k * POOL_GROUP, (k + 1) * POOL_GROUP)
            dk = _pool_diff(p, ext, denoms, k).astype(BF16)
            d_ref[:, cols] = dk
            y_ref[:, cols] = _dot(dk, w_ref[k]) * sc_ref[:, cols]

    return _run(
        body, name=name, grid=(s // ts,),
        in_specs=[pl.BlockSpec((ts, d), lambda i: (i, 5)), _const((4, POOL_GROUP, POOL_GROUP)), _const((1, d))],
        out_specs=[pl.BlockSpec((ts, d), lambda i: (i, 0))] * 2,
        out_shape=[jax.ShapeDtypeStruct((s, d), BF16), jax.ShapeDtypeStruct((s, d), F32)],
        scratch_shapes=[pltpu.VMEM((POOL_HALO, d), F32)],
        args=(z, w_pool, scale), comm=comm)


def _mix_o_fwd(x, z, ya, yb, yc, w_o, gt, ln_g, ln_b, *, ts, name, comm=None):
    s, d = x.shape
    ts = _tile(ts, s)

    def body(x_ref, ga_ref, gb_ref, gc_ref, ya_ref, yb_ref, yc_ref, w_ref, gt_ref, g_ref, b_ref,
             m_ref, o_ref, x1_ref):
        merged = (_sigmoid(ga_ref[...].astype(F32)) * ya_ref[...] + _sigmoid(gb_ref[...].astype(F32)) * yb_ref[...]
                  + _sigmoid(gc_ref[...].astype(F32)) * yc_ref[...]).astype(BF16)
        m_ref[...] = merged
        o = _dot(merged, w_ref[...])
        o_ref[...] = o
        xhat, _ = _ln_fwd(ALPHA * x_ref[...] + gt_ref[...] * o)
        x1_ref[...] = xhat * g_ref[...] + b_ref[...]

    row = pl.BlockSpec((ts, d), lambda i: (i, 0))
    zspec = lambda k: pl.BlockSpec((ts, d), lambda i, k=k: (i, k))
    return _run(
        body, name=name, grid=(s // ts,),
        in_specs=[row, zspec(6), zspec(7), zspec(8), row, row, row, _const((d, d)),
                  _const((1, d)), _const((1, d)), _const((1, d))],
        out_specs=[row] * 3,
        out_shape=[jax.ShapeDtypeStruct((s, d), BF16), jax.ShapeDtypeStruct((s, d), F32),
                   jax.ShapeDtypeStruct((s, d), F32)],
        args=(x, z, z, z, ya, yb, yc, w_o, gt, ln_g, ln_b), comm=comm)


def _ffn_fwd(up, x1, cw, cb, w_down, gt, ln_g, ln_b, *, ts, name, comm=None):
    s, d = x1.shape
    ts = _tile(ts, s)

    def body(up_ref, x1_ref, cw_ref, cb_ref, w_ref, gt_ref, g_ref, b_ref, ft_ref, dn_ref, x2_ref, carry, f_ref):
        @pl.when(pl.program_id(0) == 0)
        def _():
            carry[...] = jnp.zeros_like(carry)
        for c in range(D_FF // FF_CHUNK):
            ca = slice(c * FF_CHUNK, (c + 1) * FF_CHUNK)
            cg = slice(D_FF + c * FF_CHUNK, D_FF + (c + 1) * FF_CHUNK)
            ua = up_ref[:, ca].astype(F32)
            ext = jnp.concatenate([carry[:, ca], ua], axis=0)
            carry[:, ca] = ua[ts - CONV_HALO:]
            cf = _conv3(ua, ext, cw_ref[:, ca]) + cb_ref[:, ca]
            f = _gelu(cf) * up_ref[:, cg].astype(F32)
            f_ref[:, ca] = f.astype(BF16)
            ft_ref[ca, :] = f.T.astype(BF16)
        dn = _dot(f_ref[...], w_ref[...])
        dn_ref[...] = dn
        xhat, _ = _ln_fwd(ALPHA * x1_ref[...] + gt_ref[...] * dn)
        x2_ref[...] = xhat * g_ref[...] + b_ref[...]

    row = pl.BlockSpec((ts, d), lambda i: (i, 0))
    return _run(
        body, name=name, grid=(s // ts,),
        in_specs=[pl.BlockSpec((ts, 2 * D_FF), lambda i: (i, 0)), row, _const((3, D_FF)), _const((1, D_FF)),
                  _resident((D_FF, d)), _const((1, d)), _const((1, d)), _const((1, d))],
        out_specs=[pl.BlockSpec((D_FF, ts), lambda i: (0, i)), row, row],
        out_shape=[jax.ShapeDtypeStruct((D_FF, s), BF16), jax.ShapeDtypeStruct((s, d), F32),
                   jax.ShapeDtypeStruct((s, d), F32)],
        scratch_shapes=[pltpu.VMEM((CONV_HALO, D_FF), F32), pltpu.VMEM((ts, D_FF), BF16)],
        args=(up, x1, cw, cb, w_down, gt, ln_g, ln_b), comm=comm)


def _loss_fwd(y, tgt, *, ts, name):
    s, d = y.shape
    ts = _tile(ts, s)

    def body(y_ref, t_ref, dy_ref, l_ref):
        @pl.when(pl.program_id(0) == 0)
        def _():
            l_ref[...] = jnp.zeros_like(l_ref)
        e = y_ref[...] - t_ref[...]
        dy_ref[...] = e / float(d)
        l_ref[...] += 0.5 * jnp.sum(jnp.mean(e * e, axis=-1, keepdims=True), axis=0, keepdims=True)

    row = pl.BlockSpec((ts, d), lambda i: (i, 0))
    return _run(body, name=name, grid=(s // ts,), in_specs=[row, row], out_specs=[row, _const((8, 128))],
                out_shape=[jax.ShapeDtypeStruct((s, d), F32), jax.ShapeDtypeStruct((8, 128), F32)], args=(y, tgt))


def _rev(n_tiles):
    return lambda i: n_tiles - 1 - i


def _halo_spec(ts, n_tiles, halo, width, col):
    per = ts // halo
    return pl.BlockSpec((halo, width), lambda i: (jnp.maximum((n_tiles - 1 - i) * per - 1, 0), col))


def _ffn_bwd(dx2, x1, dn, up, cw, cb, w_down, w_up4, gt, ln_g, sc, *, ts, name, comm=None):
    s, d = x1.shape
    ts = _tile(ts, s)
    nt = s // ts
    rev = _rev(nt)
    wd = w_up4.shape[2]

    def w_up_cols(wu_ref, start):
        return wu_ref[start // wd, :, start % wd:start % wd + FF_CHUNK]

    def body(dx2_ref, x1_ref, dn_ref, up_ref, halo_ref, cw_ref, cb_ref, wd_ref, wu_ref, gt_ref, g_ref, sc_ref,
             ddn_ref, dup_ref, dx1_ref, redd_ref, redf_ref, dbup_ref, carry):
        i = pl.program_id(0)

        @pl.when(i == 0)
        def _():
            carry[...] = jnp.zeros_like(carry)
            redd_ref[...] = jnp.zeros_like(redd_ref)
            redf_ref[...] = jnp.zeros_like(redf_ref)
            dbup_ref[...] = jnp.zeros_like(dbup_ref)
        first_tile = i == nt - 1
        x1v, dnv, dyv = x1_ref[...], dn_ref[...], dx2_ref[...]
        xhat, rstd = _ln_fwd(ALPHA * x1v + gt_ref[...] * dnv)
        dr = _ln_bwd(dyv, g_ref[...], xhat, rstd)
        redd_ref[0:1, :] += _colsum(dyv * xhat)
        redd_ref[1:2, :] += _colsum(dyv)
        redd_ref[2:3, :] += _colsum(dr * dnv)
        ddn = (gt_ref[...] * dr).astype(BF16)
        ddn_ref[...] = ddn
        dh = jnp.zeros((ts, d), F32)
        for c in range(D_FF // FF_CHUNK):
            ca = slice(c * FF_CHUNK, (c + 1) * FF_CHUNK)
            cg = slice(D_FF + c * FF_CHUNK, D_FF + (c + 1) * FF_CHUNK)
            df = _dot_nt(ddn, wd_ref[ca, :])
            ua, ug = up_ref[:, ca].astype(F32), up_ref[:, cg].astype(F32)
            halo = jnp.where(first_tile, 0.0, halo_ref[:, ca].astype(F32)[HALO_ROWS - CONV_HALO:])
            ext = jnp.concatenate([halo, ua], axis=0)
            u1, u2 = _rows_before(ext, 1, CONV_HALO), _rows_before(ext, 2, CONV_HALO)
            cwc = cw_ref[:, ca]
            gl, dgl = _gelu_and_grad(cwc[2:3] * ua + cwc[1:2] * u1 + cwc[0:1] * u2 + cb_ref[:, ca])
            dug = df * gl
            dcf = df * ug * dgl
            redf_ref[0:1, ca] += _colsum(dcf * u2)
            redf_ref[1:2, ca] += _colsum(dcf * u1)
            redf_ref[2:3, ca] += _colsum(dcf * ua)
            redf_ref[3:4, ca] += _colsum(dcf)
            extd = jnp.concatenate([dcf, carry[:, ca]], axis=0)
            carry[:, ca] = dcf[:CONV_HALO]
            dua = cwc[2:3] * dcf + cwc[1:2] * _rows_after(extd, 1, ts) + cwc[0:1] * _rows_after(extd, 2, ts)
            dbup_ref[0:1, ca] += _colsum(dua)
            dbup_ref[0:1, cg] += _colsum(dug)
            dua_b, dug_b = dua.astype(BF16), dug.astype(BF16)
            dup_ref[:, ca] = dua_b
            dup_ref[:, cg] = dug_b
            dh = dh + _dot_nt(dua_b, w_up_cols(wu_ref, c * FF_CHUNK)) + _dot_nt(dug_b, w_up_cols(wu_ref, D_FF + c * FF_CHUNK))
        dx1_ref[...] = ALPHA * dr + dh * (1.0 + sc_ref[...])
        redd_ref[3:4, :] += _colsum(dh * x1v)
        redd_ref[4:5, :] += _colsum(dh)

    row = pl.BlockSpec((ts, d), lambda i: (rev(i), 0))
    return _run(
        body, name=name, grid=(nt,),
        in_specs=[row, row, row, pl.BlockSpec((ts, 2 * D_FF), lambda i: (rev(i), 0)),
                  _halo_spec(ts, nt, HALO_ROWS, D_FF, 0), _const((3, D_FF)), _const((1, D_FF)),
                  _resident((D_FF, d)), _resident((N_CHIPS, d, wd)), _const((1, d)), _const((1, d)), _const((1, d))],
        out_specs=[row, pl.BlockSpec((ts, 2 * D_FF), lambda i: (rev(i), 0)), row,
                   _const((8, d)), _const((8, D_FF)), _const((8, 2 * D_FF))],
        out_shape=[jax.ShapeDtypeStruct((s, d), BF16), jax.ShapeDtypeStruct((s, 2 * D_FF), BF16),
                   jax.ShapeDtypeStruct((s, d), F32), jax.ShapeDtypeStruct((8, d), F32),
                   jax.ShapeDtypeStruct((8, D_FF), F32), jax.ShapeDtypeStruct((8, 2 * D_FF), F32)],
        scratch_shapes=[pltpu.VMEM((CONV_HALO, D_FF), F32)],
        args=(dx2, x1, dn, up, up, cw, cb, w_down, w_up4, gt, ln_g, sc), comm=comm)


def _accumulate_dw(dw_ref, dwb_ref, xa, dy, first, last):
    @pl.when(first)
    def _():
        dw_ref[...] = jnp.zeros_like(dw_ref)
    dw_ref[...] += _dot_tn(xa, dy)

    @pl.when(last)
    def _():
        dwb_ref[...] = dw_ref[...].astype(BF16)


def _dw_out(d):
    return [_const((d, d))] * 2, [jax.ShapeDtypeStruct((d, d), F32), jax.ShapeDtypeStruct((d, d), BF16)]


def _grad_matmul_t(xt, dy, *, tk, tn, name, by_chip=False, comm=None):
    k, s = xt.shape
    n = dy.shape[1]

    def body(xt_ref, dy_ref, o_ref, ob_ref):
        o = _dot(xt_ref[...], dy_ref[...]).reshape(o_ref.shape)
        o_ref[...] = o
        ob_ref[...] = o.astype(BF16)

    if by_chip:
        assert tk == k
        per = n // N_CHIPS // tn
        ospec = pl.BlockSpec((1, k, tn), lambda j, i: (j // per, 0, j % per))
        shape = (N_CHIPS, k, n // N_CHIPS)
    else:
        ospec = pl.BlockSpec((tk, tn), lambda j, i: (i, j))
        shape = (k, n)
    xspec = _resident((k, s)) if tk == k else pl.BlockSpec((tk, s), lambda j, i: (i, 0))
    dspec = _resident((s, n)) if tn == n else pl.BlockSpec((s, tn), lambda j, i: (0, j))
    return _run(body, name=name, grid=(n // tn, k // tk), in_specs=[xspec, dspec], out_specs=[ospec, ospec],
                out_shape=[jax.ShapeDtypeStruct(shape, F32), jax.ShapeDtypeStruct(shape, BF16)], args=(xt, dy), comm=comm)


def _mix_o_bwd(dx1, x, o, z, ya, yb, yc, merged, w_o, gt, ln_g, *, ts, name, comm=None):
    s, d = x.shape
    ts = _tile(ts, s)
    nt = s // ts

    def body(dx1_ref, x_ref, o_ref, ga_ref, gb_ref, gc_ref, ya_ref, yb_ref, yc_ref, m_ref, w_ref, gt_ref, g_ref,
             dxa_ref, dzg_ref, dya_ref, dyb_ref, dyc_ref, red_ref, dw_ref, dwb_ref):
        i = pl.program_id(0)

        @pl.when(i == 0)
        def _():
            red_ref[...] = jnp.zeros_like(red_ref)
        dyv, ov = dx1_ref[...], o_ref[...]
        xhat, rstd = _ln_fwd(ALPHA * x_ref[...] + gt_ref[...] * ov)
        dr = _ln_bwd(dyv, g_ref[...], xhat, rstd)
        red_ref[0:1, :] += _colsum(dyv * xhat)
        red_ref[1:2, :] += _colsum(dyv)
        red_ref[2:3, :] += _colsum(dr * ov)
        dxa_ref[...] = ALPHA * dr
        d_o = (gt_ref[...] * dr).astype(BF16)
        _accumulate_dw(dw_ref, dwb_ref, m_ref[...], d_o, i == 0, i == nt - 1)
        dm = _dot_nt(d_o, w_ref[...])
        for k, (zg_ref, y_ref, dy_ref) in enumerate(((ga_ref, ya_ref, dya_ref), (gb_ref, yb_ref, dyb_ref),
                                                     (gc_ref, yc_ref, dyc_ref))):
            g = _sigmoid(zg_ref[...].astype(F32))
            dzg_ref[:, k * d:(k + 1) * d] = (dm * y_ref[...] * g * (1.0 - g)).astype(BF16)
            dy_ref[...] = (dm * g).astype(BF16)

    row = pl.BlockSpec((ts, d), lambda i: (i, 0))
    zspec = lambda k: pl.BlockSpec((ts, d), lambda i, k=k: (i, k))
    bf = jax.ShapeDtypeStruct((s, d), BF16)
    dw_specs, dw_shapes = _dw_out(d)
    return _run(
        body, name=name, grid=(nt,),
        in_specs=[row, row, row, zspec(6), zspec(7), zspec(8), row, row, row, row, _const((d, d)),
                  _const((1, d)), _const((1, d))],
        out_specs=[row, pl.BlockSpec((ts, 3 * d), lambda i: (i, 2)), row, row, row, _const((8, d))] + dw_specs,
        out_shape=[jax.ShapeDtypeStruct((s, d), F32), jax.ShapeDtypeStruct((s, D_Z), BF16), bf, bf, bf,
                   jax.ShapeDtypeStruct((8, d), F32)] + dw_shapes,
        args=(dx1, x, o, z, z, z, ya, yb, yc, merged, w_o, gt, ln_g), comm=comm)


def _mix_a_bwd(dya, a, z, dz, cw, w_out, *, ts, name, comm=None):
    s = z.shape[0]
    d = D_MODEL
    ts = _tile(ts, s)
    nt = s // ts
    rev = _rev(nt)

    def body(dya_ref, a_ref, zb_ref, zc_ref, zx_ref, hc_ref, hx_ref, cw_ref, w_ref, dz_in, dz_ref, red_ref,
             dw_ref, dwb_ref, carry):
        i = pl.program_id(0)

        @pl.when(i == 0)
        def _():
            carry[...] = jnp.zeros_like(carry)
            red_ref[...] = jnp.zeros_like(red_ref)
        _accumulate_dw(dw_ref, dwb_ref, a_ref[...], dya_ref[...], i == 0, i == nt - 1)
        zb, zc, zx = zb_ref[...].astype(F32), zc_ref[...].astype(F32), zx_ref[...].astype(F32)
        q = zc * zx
        halo = jnp.where(i == nt - 1, 0.0, (hc_ref[...].astype(F32) * hx_ref[...].astype(F32))[HALO_ROWS - CONV_HALO:])
        ext = jnp.concatenate([halo, q], axis=0)
        q1, q2 = _rows_before(ext, 1, CONV_HALO), _rows_before(ext, 2, CONV_HALO)
        cwv = cw_ref[...]
        cv = cwv[2:3] * q + cwv[1:2] * q1 + cwv[0:1] * q2
        da = _dot_nt(dya_ref[...], w_ref[...])
        dcv = da * zb
        red_ref[0:1, :] += _colsum(dcv * q2)
        red_ref[1:2, :] += _colsum(dcv * q1)
        red_ref[2:3, :] += _colsum(dcv * q)
        extd = jnp.concatenate([dcv, carry[...]], axis=0)
        carry[...] = dcv[:CONV_HALO]
        dq = cwv[2:3] * dcv + cwv[1:2] * _rows_after(extd, 1, ts) + cwv[0:1] * _rows_after(extd, 2, ts)
        dz_ref[:, 0:d] = (da * cv).astype(BF16)
        dz_ref[:, d:2 * d] = (dq * zx).astype(BF16)
        dz_ref[:, 2 * d:3 * d] = (dq * zc).astype(BF16)

    zspec = lambda k: pl.BlockSpec((ts, d), lambda i, k=k: (rev(i), k))
    row = pl.BlockSpec((ts, d), lambda i: (rev(i), 0))
    dw_specs, dw_shapes = _dw_out(d)
    return _run(
        body, name=name, grid=(nt,),
        in_specs=[row, row, zspec(0), zspec(1), zspec(2),
                  _halo_spec(ts, nt, HALO_ROWS, d, 1), _halo_spec(ts, nt, HALO_ROWS, d, 2),
                  _const((3, d)), _const((d, d)), HBM],
        out_specs=[pl.BlockSpec((ts, 3 * d), lambda i: (rev(i), 0)), _const((8, d))] + dw_specs,
        out_shape=[jax.ShapeDtypeStruct((s, D_Z), BF16), jax.ShapeDtypeStruct((8, d), F32)] + dw_shapes,
        scratch_shapes=[pltpu.VMEM((CONV_HALO, d), F32)],
        args=(dya, a, z, z, z, z, z, cw, w_out, dz), aliases={9: 0}, comm=comm)


def _mix_b_bwd(dyb, sg, z, dz, ln_g, ln_b, ws, bst, w_out, *, ts, name, comm=None):
    s = z.shape[0]
    d = D_MODEL
    ts = _tile(ts, s)
    nt = s // ts
    nblk = ts // GBLK

    def body(dyb_ref, sg_ref, zu_ref, zv_ref, g_ref, b_ref, ws_ref, bst_ref, w_ref, dz_in,
             dz_ref, red_ref, dws_ref, dbst_ref, dw_ref, dwb_ref, mixed_scr, dvn_scr, dzv_scr):
        first = (pl.program_id(0) == 0) & (pl.program_id(1) == 0)

        @pl.when(first)
        def _():
            red_ref[...] = jnp.zeros_like(red_ref)
            dws_ref[...] = jnp.zeros_like(dws_ref)
            dbst_ref[...] = jnp.zeros_like(dbst_ref)

        @pl.when(pl.program_id(1) == 0)
        def _():
            _accumulate_dw(dw_ref, dwb_ref, sg_ref[...], dyb_ref[...], first, pl.program_id(0) == nt - 1)
            u, du_dz = _gelu_and_grad(zu_ref[...].astype(F32))
            vg, dv_dz = _gelu_and_grad(zv_ref[...].astype(F32))
            xhat, rstd = _ln_fwd(vg)
            vn = (xhat * g_ref[...] + b_ref[...]).astype(BF16)
            _spatial_mix(vn, ws_ref, bst_ref, mixed_scr, ts)
            dsg = _dot_nt(dyb_ref[...], w_ref[...])
            dz_ref[...] = (dsg * mixed_scr[...] * du_dz).astype(BF16)
            dmix = dsg * u
            mask = _spatial_mask()
            for g in range(N_GROUPS):
                cols = slice(g * GBLK, (g + 1) * GBLK)
                wm = jnp.where(mask, ws_ref[g], 0.0).astype(BF16)
                dm_cat = jnp.concatenate([dmix[n * GBLK:(n + 1) * GBLK, cols] for n in range(nblk)], axis=1)
                vn_cat = jnp.concatenate([vn[n * GBLK:(n + 1) * GBLK, cols] for n in range(nblk)], axis=1)
                dm_b = dm_cat.astype(BF16)
                dbst_ref[:, g:g + 1] += jnp.sum(dm_cat, axis=1, keepdims=True)
                dws_ref[g] += jnp.where(mask, _dot_nt(dm_b, vn_cat), 0.0)
                dvn_cat = _dot_tn(wm, dm_b)
                for n in range(nblk):
                    dvn_scr[n * GBLK:(n + 1) * GBLK, cols] = dvn_cat[:, n * GBLK:(n + 1) * GBLK]
            dvn = dvn_scr[...]
            red_ref[0:1, :] += _colsum(dvn * xhat)
            red_ref[1:2, :] += _colsum(dvn)
            dzv_scr[...] = (_ln_bwd(dvn, g_ref[...], xhat, rstd) * dv_dz).astype(BF16)

        @pl.when(pl.program_id(1) == 1)
        def _():
            dz_ref[...] = dzv_scr[...]

    zspec = lambda k: pl.BlockSpec((ts, d), lambda i, h, k=k: (i, k))
    row = pl.BlockSpec((ts, d), lambda i, h: (i, 0))
    dw_specs, dw_shapes = _dw_out(d)
    return _run(
        body, name=name, grid=(nt, 2),
        in_specs=[row, row, zspec(3), zspec(4), _const((1, d)), _const((1, d)),
                  _const((N_GROUPS, GBLK, GBLK)), _const((GBLK, N_GROUPS)), _const((d, d)), HBM],
        out_specs=[pl.BlockSpec((ts, d), lambda i, h: (i, 3 + h)), _const((8, d)),
                   _const((N_GROUPS, GBLK, GBLK)), _const((GBLK, N_GROUPS))] + dw_specs,
        out_shape=[jax.ShapeDtypeStruct((s, D_Z), BF16), jax.ShapeDtypeStruct((8, d), F32),
                   jax.ShapeDtypeStruct((N_GROUPS, GBLK, GBLK), F32), jax.ShapeDtypeStruct((GBLK, N_GROUPS), F32)]
        + dw_shapes,
        scratch_shapes=[pltpu.VMEM((ts, d), F32), pltpu.VMEM((ts, d), F32), pltpu.VMEM((ts, d), BF16)],
        args=(dyb, sg, z, z, ln_g, ln_b, ws, bst, w_out, dz), aliases={9: 0}, comm=comm)


def _mix_c_bwd(dyc, z, dz, w_pool, scale, *, ts, name):
    s = z.shape[0]
    d = D_MODEL
    ts = _tile(ts, s)
    nt = s // ts
    rev = _rev(nt)

    def body(dyc_ref, zp_ref, halo_ref, w_ref, sc_ref, dz_in, dz_ref, red_ref, dw_ref, carry):
        i = pl.program_id(0)

        @pl.when(i == 0)
        def _():
            carry[...] = jnp.zeros_like(carry)
            red_ref[...] = jnp.zeros_like(red_ref)
            dw_ref[...] = jnp.zeros_like(dw_ref)
        p = zp_ref[...].astype(F32)
        ext = jnp.concatenate([jnp.where(i == nt - 1, 0.0, halo_ref[...].astype(F32)), p], axis=0)
        denoms = _pool_denoms(rev(i), ts)
        dyv = dyc_ref[...].astype(F32)
        for k in range(len(POOL_WINDOWS)):
            cols = slice(k * POOL_GROUP, (k + 1) * POOL_GROUP)
            dk = _pool_diff(p, ext, denoms, k).astype(BF16)
            red_ref[0:1, cols] += _colsum(dyv[:, cols] * _dot(dk, w_ref[k]))
            dpre = (dyv[:, cols] * sc_ref[:, cols]).astype(BF16)
            dw_ref[k] += _dot_tn(dk, dpre)
            dd = _dot_nt(dpre, w_ref[k])
            e = dd / denoms[k]
            acc = jnp.concatenate([e, carry[:, cols]], axis=0)
            carry[:, cols] = e[:POOL_HALO]
            step = 1
            while step < POOL_WINDOWS[k]:
                acc = acc + pltpu.roll(acc, acc.shape[0] - step, 0)
                step *= 2
            dz_ref[:, cols] = (acc[:ts] - dd).astype(BF16)

    return _run(
        body, name=name, grid=(nt,),
        in_specs=[pl.BlockSpec((ts, d), lambda i: (rev(i), 0)), pl.BlockSpec((ts, d), lambda i: (rev(i), 5)),
                  _halo_spec(ts, nt, POOL_HALO, d, 5), _const((4, POOL_GROUP, POOL_GROUP)), _const((1, d)), HBM],
        out_specs=[pl.BlockSpec((ts, d), lambda i: (rev(i), 5)), _const((8, d)), _const((4, POOL_GROUP, POOL_GROUP))],
        out_shape=[jax.ShapeDtypeStruct((s, D_Z), BF16), jax.ShapeDtypeStruct((8, d), F32),
                   jax.ShapeDtypeStruct((4, POOL_GROUP, POOL_GROUP), F32)],
        scratch_shapes=[pltpu.VMEM((POOL_HALO, d), F32)],
        args=(dyc, z, z, w_pool, scale, dz), aliases={5: 0})


def _in_proj_bwd(dz, w4, dxa, x, sc, *, ts, name, comm=None):
    s, d = x.shape
    ts = _tile(ts, s)
    wd = w4.shape[2]

    def body(dz_ref, w_ref, dxa_ref, x_ref, sc_ref, dx_ref, red_ref, db_ref):
        @pl.when(pl.program_id(0) == 0)
        def _():
            red_ref[...] = jnp.zeros_like(red_ref)
            db_ref[...] = jnp.zeros_like(db_ref)
        dh = jnp.zeros((ts, d), F32)
        for j in range(N_CHIPS):
            dzj = dz_ref[:, j * wd:(j + 1) * wd]
            db_ref[0:1, j * wd:(j + 1) * wd] += _colsum(dzj.astype(F32))
            dh = dh + _dot_nt(dzj, w_ref[j])
        dx_ref[...] = dxa_ref[...] + dh * (1.0 + sc_ref[...])
        red_ref[0:1, :] += _colsum(dh * x_ref[...])
        red_ref[1:2, :] += _colsum(dh)

    row = pl.BlockSpec((ts, d), lambda i: (i, 0))
    return _run(
        body, name=name, grid=(s // ts,),
        in_specs=[pl.BlockSpec((ts, D_Z), lambda i: (i, 0)), _resident((N_CHIPS, d, wd)), row, row, _const((1, d))],
        out_specs=[row, _const((8, d)), _const((8, D_Z))],
        out_shape=[jax.ShapeDtypeStruct((s, d), F32), jax.ShapeDtypeStruct((8, d), F32),
                   jax.ShapeDtypeStruct((8, D_Z), F32)],
        args=(dz, w4, dxa, x, sc), comm=comm)


def _ada_fwd(c_all, w_ada, b_ada, *, name):
    nl, d, n = w_ada.shape
    tn = n // 2

    def body(c_ref, w_ref, b_ref, o_ref):
        cv = c_ref[...]
        ca = (cv * _sigmoid(cv)).astype(BF16)
        o_ref[0] = _dot(ca, w_ref[0].astype(BF16)) + b_ref[0]

    return _run(
        body, name=name, grid=(nl, n // tn),
        in_specs=[_const((N_DEV, d)), pl.BlockSpec((1, d, tn), lambda l, j: (l, 0, j)),
                  pl.BlockSpec((1, 1, tn), lambda l, j: (l, 0, j))],
        out_specs=[pl.BlockSpec((1, N_DEV, tn), lambda l, j: (l, 0, j))],
        out_shape=[jax.ShapeDtypeStruct((nl, N_DEV, n), F32)], args=(c_all, w_ada, b_ada))[0]


def _ada_bwd(c_all, dada, *, name):
    nl, nb, n = dada.shape
    d = c_all.shape[1]
    tn = n // 2

    def body(c_ref, g_ref, o_ref):
        cv = c_ref[...]
        ca = (cv * _sigmoid(cv)).astype(BF16)
        o_ref[0] = _dot_tn(ca, g_ref[0].astype(BF16))

    return _run(
        body, name=name, grid=(nl, n // tn),
        in_specs=[_const((nb, d)), pl.BlockSpec((1, nb, tn), lambda l, j: (l, 0, j))],
        out_specs=[pl.BlockSpec((1, d, tn), lambda l, j: (l, 0, j))],
        out_shape=[jax.ShapeDtypeStruct((nl, d, n), F32)], args=(c_all, dada))[0]


def _sum4_into_half(own, recv, core, *, name):
    r, c = own.shape
    tr = _row_tile(r, c, 2)

    def body(core_ref, own_ref, recv_ref, o_ref):
        acc = own_ref[...]
        for k in range(N_CHIPS - 1):
            acc = acc + recv_ref[k].astype(F32)
        o_ref[0] = acc

    spec = pltpu.PrefetchScalarGridSpec(
        num_scalar_prefetch=1, grid=(r // tr,),
        in_specs=[pl.BlockSpec((tr, c), lambda i, core_ref: (i, 0)),
                  pl.BlockSpec((N_CHIPS - 1, tr, c), lambda i, core_ref: (0, i, 0))],
        out_specs=pl.BlockSpec((1, tr, c), lambda i, core_ref: (core_ref[0], i, 0)))
    return pl.pallas_call(
        body, name=name, grid_spec=spec, out_shape=jax.ShapeDtypeStruct((2, r, c), F32),
        compiler_params=pltpu.CompilerParams(dimension_semantics=("arbitrary",), vmem_limit_bytes=VMEM_LIMIT),
    )(core, own, recv)


def _cast_into_slots(shards, layer, chip, *, name):
    quarters = 4

    def body(chip_ref, *refs):
        ins, outs = refs[:len(shards)], refs[len(shards):]
        for i_ref, o_ref in zip(ins, outs):
            o_ref[0, 0] = i_ref[0].astype(BF16)

    in_specs, out_specs, out_shape = [], [], []
    for sh in shards:
        _, r, c = sh.shape
        in_specs.append(pl.BlockSpec((1, r // quarters, c), lambda t, chip_ref: (layer, t, 0)))
        out_specs.append(pl.BlockSpec((1, 1, r // quarters, c), lambda t, chip_ref: (chip_ref[0], t // 2, t % 2, 0)))
        out_shape.append(jax.ShapeDtypeStruct((N_CHIPS, 2, r // 2, c), BF16))
    spec = pltpu.PrefetchScalarGridSpec(num_scalar_prefetch=1, grid=(quarters,), in_specs=in_specs, out_specs=out_specs)
    return pl.pallas_call(
        body, name=name, grid_spec=spec, out_shape=out_shape,
        compiler_params=pltpu.CompilerParams(dimension_semantics=("arbitrary",), vmem_limit_bytes=VMEM_LIMIT),
    )(chip, *shards)


def _sum_halves(g_f32, theirs, place, *, name, comm=None):
    _, _, rh, c = g_f32.shape
    tr = _row_tile(rh, c, 2)

    def body(place_ref, g_ref, t_ref, hb_ref, own_ref):
        h = g_ref[0, 0] + t_ref[0].astype(F32)
        hb_ref[0] = h.astype(BF16)

        @pl.when(pl.program_id(1) == place_ref[1])
        def _():
            own_ref[...] = h

    return _run(
        body, name=name, grid=(rh // tr, N_CHIPS), prefetch=place,
        in_specs=[pl.BlockSpec((1, 1, tr, c), lambda i, j, place_ref: (j, place_ref[0], i, 0)),
                  pl.BlockSpec((1, tr, c), lambda i, j, place_ref: (j, i, 0))],
        out_specs=[pl.BlockSpec((1, tr, c), lambda i, j, place_ref: (j, i, 0)),
                   pl.BlockSpec((tr, c), lambda i, j, place_ref: (i, 0))],
        out_shape=[jax.ShapeDtypeStruct((N_CHIPS, rh, c), BF16), jax.ShapeDtypeStruct((rh, c), F32)],
        args=(g_f32, theirs), comm=comm)


def _row_tile(r, c, mib):
    limit = max(8, (mib << 20) // (4 * c))
    if r <= limit:
        return r
    best = 8
    for t in range(8, limit + 1, 8):
        if r % t == 0:
            best = t
    return best


def _adam_math(w, g, m, v):
    mn = ADAM_B1 * m + (1.0 - ADAM_B1) * g
    vn = ADAM_B2 * v + (1.0 - ADAM_B2) * (g * g)
    m_hat = mn / (1.0 - ADAM_B1 ** ADAM_STEP)
    v_hat = vn / (1.0 - ADAM_B2 ** ADAM_STEP)
    return -ADAM_LR * (m_hat / (jnp.sqrt(v_hat) + ADAM_EPS) + ADAM_WD * w), mn, vn


def _adamw(w, g, m, v, *, name):
    r, c = w.shape
    tr = _row_tile(r, c, 2)

    def body(w_ref, g_ref, m_ref, v_ref, d_ref, mo_ref, vo_ref):
        d_ref[...], mo_ref[...], vo_ref[...] = _adam_math(w_ref[...], g_ref[...], m_ref[...], v_ref[...])

    blk = pl.BlockSpec((tr, c), lambda i: (i, 0))
    return _run(body, name=name, grid=(r // tr,), in_specs=[blk] * 4, out_specs=[blk] * 3,
                out_shape=[jax.ShapeDtypeStruct((r, c), F32)] * 3, args=(w, g, m, v))


def _adamw_sharded(w, m, v, grads, *, name, comm=None):
    nl, r, c = w.shape
    tr = _row_tile(r, c, 1)
    nt = r // tr

    def body(w_ref, m_ref, v_ref, g0_ref, g1_ref, g_ref, d_ref, mo_ref, vo_ref):
        g = jnp.where(pl.program_id(0) == 0, g0_ref[...], g1_ref[...])
        g_ref[0] = g
        d_ref[0], mo_ref[0], vo_ref[0] = _adam_math(w_ref[0], g, m_ref[0], v_ref[0])

    blk = pl.BlockSpec((1, tr, c), lambda l, i: (l, i, 0))
    part0 = pl.BlockSpec((tr, c), lambda l, i: (jnp.where(l == 0, i, nt - 1), 0))
    part1 = pl.BlockSpec((tr, c), lambda l, i: (jnp.where(l == 1, i, 0), 0))
    return _run(body, name=name, grid=(nl, nt), in_specs=[blk] * 3 + [part0, part1],
                out_specs=[blk] * 4, out_shape=[jax.ShapeDtypeStruct((nl, r, c), F32)] * 4,
                args=(w, m, v, grads[0], grads[1]), comm=comm)


_BIG = ("w_in", "w_a_out", "w_b_out", "w_pool", "w_o", "w_up", "w_down")
_COL_SHARDED = ("w_in", "w_up")
_SMALL_SHARDED = ("conv_a", "conv_ffn")
_SMALL_REPL = ("b_in", "ln_v_g", "ln_v_b", "w_spatial", "b_spatial", "pool_scale", "ln1_g", "ln1_b", "b_up",
               "conv_ffn_b", "ln2_g", "ln2_b")
_WEIGHTS = ("w_ada", "b_ada", "w_in", "b_in", "conv_a", "w_a_out", "ln_v_g", "ln_v_b", "w_spatial", "b_spatial",
            "w_b_out", "w_pool", "pool_scale", "w_o", "ln1_g", "ln1_b", "w_up", "b_up", "conv_ffn", "conv_ffn_b",
            "w_down", "ln2_g", "ln2_b")


def _shard3(a):
    return a.reshape(a.shape[0], -1, a.shape[-1])


def _use_gathered(name, g):
    g = g.reshape(N_CHIPS, -1, g.shape[-1])
    if name in _COL_SHARDED:
        return g
    if name == "w_pool":
        return g.reshape(N_CHIPS, 4, POOL_GROUP // N_CHIPS, POOL_GROUP).transpose(1, 0, 2, 3).reshape(
            4, POOL_GROUP, POOL_GROUP)
    return g.reshape(-1, g.shape[-1])


def _grad_by_chip(name, g):
    if name in _COL_SHARDED:
        return g
    if name == "w_pool":
        return g.reshape(4, N_CHIPS, POOL_GROUP // N_CHIPS, POOL_GROUP).transpose(1, 0, 2, 3).reshape(
            N_CHIPS, POOL_GROUP, POOL_GROUP)
    return g.reshape(N_CHIPS, -1, g.shape[-1])


def _pack_small(arrs):
    parts = []
    for a in arrs:
        flat = a.reshape(-1).astype(F32)
        pad = (-flat.shape[0]) % 128
        parts.append(jnp.pad(flat, (0, pad)) if pad else flat)
    flat = jnp.concatenate(parts)
    pad = (-flat.shape[0]) % 2048
    if pad:
        flat = jnp.pad(flat, (0, pad))
    return flat.reshape(-1, 128)


def _unpack_small(buf, shapes):
    lead = buf.shape[:-2]
    flat = buf.reshape(lead + (-1,))
    out, off = [], 0
    for shp in shapes:
        n = math.prod(shp)
        out.append(flat[..., off:off + n].reshape(lead + tuple(shp)))
        off += n + ((-n) % 128)
    return out


def _as2d(a):
    return a.reshape(-1, a.shape[-1])


_LATE = ("w_a_out", "w_b_out", "w_pool", "w_o")


class _Traffic:
    def __init__(self, slots, plan, core, chip):
        self.slots = slots
        self.plan = plan
        self.core = core
        self.place = jnp.concatenate([core, chip])
        self.gathered = {}
        self.ready = {}
        self.summed = {}
        self.half = {}
        self.final = {}

    def weight(self, layer, name):
        return self.gathered[(layer, name)]

    def add_grad(self, layer, name, g_f32, g_bf16):
        def halves(g):
            g = _grad_by_chip(name, g)
            return g.reshape(N_CHIPS, 2, g.shape[1] // 2, g.shape[2])
        self.ready[(layer, name)] = (halves(g_f32), halves(g_bf16))

    def _comm(self, job):
        if job[0] == "gather":
            return _gather_comm([self.slots[(job[1], k)] for k in job[2]], *job[3:])
        if job[0] == "presum":
            return _presum_comm([self.ready[k][1] for k in job[1]])
        if job[0] == "scatter":
            return _scatter_comm([self.summed[k][0] for k in job[1]])
        return _join_comm([self.half[k] for k in job[1]])

    def _done(self, job, res):
        if job[0] == "gather":
            for k, r in zip(job[2], res):
                self.slots[(job[1], k)] = r
                self.gathered[(job[1], k)] = _use_gathered(k, r)
        elif job[0] == "presum":
            for k, r in zip(job[1], res):
                nm = f"presum_l{k[0]}_{k[1]}"
                self.summed[k] = self.run(nm, lambda cm: _sum_halves(self.ready.pop(k)[0], r, self.place, name=nm,
                                                                     comm=cm))
        elif job[0] == "scatter":
            for k, r in zip(job[1], res):
                self.half[k] = _sum4_into_half(self.summed.pop(k)[1], r, self.core, name=f"sum_l{k[0]}_{k[1]}")
        else:
            for k, r in zip(job[1], res):
                self.final[k] = r.reshape(-1, r.shape[-1])

    def run(self, name, fn):
        jobs = self.plan.get(name)
        if not jobs:
            return fn(None)
        comms = [self._comm(j) for j in jobs]
        outs, res = fn(_merge(comms))
        for job, r in zip(jobs, _split(comms, res)):
            self._done(job, r)
        return outs

    def alone(self, name):
        jobs = self.plan[name]
        comms = [self._comm(j) for j in jobs]
        for job, r in zip(jobs, _split(comms, _comm_call(_merge(comms), name=name))):
            self._done(job, r)


def _layer_fwd(x, ada, p, l, tr):
    sh1, sc1, gt1, sh2, sc2, gt2 = ada
    n = f"l{l}"
    z, ht = tr.run(f"{n}_in_proj", lambda cm: _mod_matmul(
        x, sc1, sh1, tr.weight(l, "w_in"), p["b_in"], ts=1024, tn=2304, name=f"{n}_in_proj", comm=cm))
    a, ya = tr.run(f"{n}_mix_a", lambda cm: _mix_a_fwd(z, p["conv_a"], tr.weight(l, "w_a_out"), ts=256,
                                                      name=f"{n}_mix_a", comm=cm))
    sg, yb = tr.run(f"{n}_mix_b", lambda cm: _mix_b_fwd(
        z, p["ln_v_g"], p["ln_v_b"], p["w_spatial"], p["b_spatial_t"], tr.weight(l, "w_b_out"), ts=256,
        name=f"{n}_mix_b", comm=cm))
    dpool, yc = tr.run(f"{n}_mix_c", lambda cm: _mix_c_fwd(z, tr.weight(l, "w_pool"), p["pool_scale"], ts=256,
                                                          name=f"{n}_mix_c", comm=cm))
    merged, o, x1 = tr.run(f"{n}_mix_o", lambda cm: _mix_o_fwd(
        x, z, ya, yb, yc, tr.weight(l, "w_o"), gt1, p["ln1_g"], p["ln1_b"], ts=256, name=f"{n}_mix_o", comm=cm))
    up, h2t = tr.run(f"{n}_up_proj", lambda cm: _mod_matmul(
        x1, sc2, sh2, tr.weight(l, "w_up"), p["b_up"], ts=1024, tn=1408, name=f"{n}_up_proj", comm=cm))
    ft, dn, x2 = tr.run(f"{n}_ffn", lambda cm: _ffn_fwd(
        up, x1, p["conv_ffn"], p["conv_ffn_b"], tr.weight(l, "w_down"), gt2, p["ln2_g"], p["ln2_b"], ts=256,
        name=f"{n}_ffn", comm=cm))
    saved = dict(x=x, z=z, ht=ht, a=a, ya=ya, sg=sg, yb=yb, dpool=dpool, yc=yc, merged=merged, o=o, x1=x1, h2t=h2t,
                 up=up, ft=ft, dn=dn)
    return x2, saved


def _layer_bwd(dx2, ada, p, sv, l, tr):
    sh1, sc1, gt1, sh2, sc2, gt2 = ada
    n = f"l{l}"
    ddn, dup, dx1, red_d, red_f, dbup = tr.run(f"{n}_ffn_bwd", lambda cm: _ffn_bwd(
        dx2, sv["x1"], sv["dn"], sv["up"], p["conv_ffn"], p["conv_ffn_b"], tr.weight(l, "w_down"),
        tr.weight(l, "w_up"), gt2, p["ln2_g"], sc2, ts=256, name=f"{n}_ffn_bwd", comm=cm))
    g = {}
    tr.add_grad(l, "w_down", *_grad_matmul_t(sv["ft"], ddn, tk=D_FF // N_CHIPS, tn=D_MODEL, name=f"{n}_dw_down"))
    tr.add_grad(l, "w_up", *tr.run(f"{n}_dw_up", lambda cm: _grad_matmul_t(
        sv["h2t"], dup, tk=D_MODEL, tn=FF_CHUNK, name=f"{n}_dw_up", by_chip=True, comm=cm)))
    g["ln2_g"], g["ln2_b"] = red_d[0], red_d[1]
    g["conv_ffn"], g["conv_ffn_b"], g["b_up"] = red_f[0:3], red_f[3], dbup[0]

    dxa, dz, dya, dyb, dyc, red_o, dwo, dwo_b = tr.run(f"{n}_mix_o_bwd", lambda cm: _mix_o_bwd(
        dx1, sv["x"], sv["o"], sv["z"], sv["ya"], sv["yb"], sv["yc"], sv["merged"], tr.weight(l, "w_o"), gt1,
        p["ln1_g"], ts=256, name=f"{n}_mix_o_bwd", comm=cm))
    tr.add_grad(l, "w_o", dwo, dwo_b)
    g["ln1_g"], g["ln1_b"] = red_o[0], red_o[1]

    dz, red_a, dwa, dwa_b = tr.run(f"{n}_mix_a_bwd", lambda cm: _mix_a_bwd(
        dya, sv["a"], sv["z"], dz, p["conv_a"], tr.weight(l, "w_a_out"), ts=256, name=f"{n}_mix_a_bwd", comm=cm))
    tr.add_grad(l, "w_a_out", dwa, dwa_b)
    g["conv_a"] = red_a[0:3]

    dz, red_b, dws, dbst, dwb, dwb_b = tr.run(f"{n}_mix_b_bwd", lambda cm: _mix_b_bwd(
        dyb, sv["sg"], sv["z"], dz, p["ln_v_g"], p["ln_v_b"], p["w_spatial"], p["b_spatial_t"],
        tr.weight(l, "w_b_out"), ts=256, name=f"{n}_mix_b_bwd", comm=cm))
    tr.add_grad(l, "w_b_out", dwb, dwb_b)
    g["ln_v_g"], g["ln_v_b"], g["w_spatial"], g["b_spatial"] = red_b[0], red_b[1], dws, dbst.T

    dz, red_c, dwp = _mix_c_bwd(dyc, sv["z"], dz, tr.weight(l, "w_pool"), p["pool_scale"], ts=256,
                                name=f"{n}_mix_c_bwd")
    g["pool_scale"] = red_c[0]
    tr.add_grad(l, "w_pool", dwp, dwp.astype(BF16))

    tr.add_grad(l, "w_in", *tr.run(f"{n}_dw_in", lambda cm: _grad_matmul_t(
        sv["ht"], dz, tk=D_MODEL, tn=1152, name=f"{n}_dw_in", by_chip=True, comm=cm)))
    if f"{n}_presum_tail" in tr.plan:
        tr.alone(f"{n}_presum_tail")
    dx, red_i, dbin = tr.run(f"{n}_in_proj_bwd", lambda cm: _in_proj_bwd(
        dz, tr.weight(l, "w_in"), dxa, sv["x"], sc1, ts=256, name=f"{n}_in_proj_bwd", comm=cm))
    g["b_in"] = dbin[0]
    dada = jnp.stack([red_i[1], red_i[0], red_o[2], red_d[4], red_d[3], red_d[2]])
    return dx, g, dada


def _traffic_plan():
    plan = {
        "gather_l0": [("gather", 0, ("w_in",) + _LATE)],
        "l0_in_proj": [("gather", 0, ("w_up", "w_down"))],
        "l0_mix_o": [("gather", 1, _LATE)],
        "l0_up_proj": [("gather", 1, ("w_in",), (1, 2))],
        "l0_ffn": [("gather", 1, ("w_in",), (3,))],
        "l1_in_proj": [("gather", 1, ("w_up", "w_down"))],
    }
    for l in reversed(range(DEPTH)):
        late = [(l, k) for k in _LATE]
        plan.update({
            f"l{l}_mix_o_bwd": [("presum", [(l, "w_down"), (l, "w_up")])],
            f"l{l}_mix_b_bwd": [("scatter", [(l, "w_down"), (l, "w_up")])],
            f"l{l}_dw_in": [("presum", late), ("join", [(l, "w_down"), (l, "w_up")])],
        })
    late0, late1 = [(0, k) for k in _LATE], [(1, k) for k in _LATE]
    plan["l1_in_proj_bwd"] = [("presum", [(1, "w_in")]), ("scatter", late1)]
    plan["l0_ffn_bwd"] = [("scatter", [(1, "w_in")]), ("join", late1)]
    plan["l0_dw_up"] = [("join", [(1, "w_in")])]
    plan["l0_presum_tail"] = [("presum", [(0, "w_in")])]
    plan["presum_l0_w_in"] = [("scatter", late0)]
    plan["l0_in_proj_bwd"] = [("scatter", [(0, "w_in")])]
    plan["join_tail"] = [("join", [(0, "w_in")] + late0)]
    return plan


def kernel(x, c, w_ada, b_ada, w_in, b_in, conv_a, w_a_out, ln_v_g, ln_v_b, w_spatial, b_spatial, w_b_out, w_pool, pool_scale, w_o, ln1_g, ln1_b, w_up, b_up, conv_ffn, conv_ffn_b, w_down, ln2_g, ln2_b, loss_target, m_w_ada, m_b_ada, m_w_in, m_b_in, m_conv_a, m_w_a_out, m_ln_v_g, m_ln_v_b, m_w_spatial, m_b_spatial, m_w_b_out, m_w_pool, m_pool_scale, m_w_o, m_ln1_g, m_ln1_b, m_w_up, m_b_up, m_conv_ffn, m_conv_ffn_b, m_w_down, m_ln2_g, m_ln2_b, v_w_ada, v_b_ada, v_w_in, v_b_in, v_conv_a, v_w_a_out, v_ln_v_g, v_ln_v_b, v_w_spatial, v_b_spatial, v_w_b_out, v_w_pool, v_pool_scale, v_w_o, v_ln1_g, v_ln1_b, v_w_up, v_b_up, v_conv_ffn, v_conv_ffn_b, v_w_down, v_ln2_g, v_ln2_b):
    args = locals()
    w = {k: args[k] for k in _WEIGHTS}
    m = {k: args["m_" + k] for k in _WEIGHTS}
    v = {k: args["v_" + k] for k in _WEIGHTS}
    d = D_MODEL
    mx, my, mc = _my_coords()
    chip = 2 * mx + my
    me = 4 * mx + 2 * my + mc

    small_shapes = [c.shape, conv_a.shape, conv_ffn.shape]
    small_all = _all_gather8(_pack_small([c, conv_a, conv_ffn]), name="gather_small")
    c_all, conv_a_st, conv_ffn_st = _unpack_small(small_all, small_shapes)
    c_all = c_all.reshape(N_DEV, d)
    conv_full = {"conv_a": jnp.concatenate([conv_a_st[2 * j] for j in range(N_CHIPS)], axis=-1),
                 "conv_ffn": jnp.concatenate([conv_ffn_st[2 * j] for j in range(N_CHIPS)], axis=-1)}

    chip_idx = jnp.reshape(chip, (1,)).astype(jnp.int32)
    slots = {}
    for l in range(DEPTH):
        bufs = _cast_into_slots([_shard3(w[k]) for k in _BIG], l, chip_idx, name=f"cast_l{l}")
        slots.update({(l, k): b for k, b in zip(_BIG, bufs)})
    tr = _Traffic(slots, _traffic_plan(), jnp.reshape(mc, (1,)).astype(jnp.int32), chip_idx)
    tr.alone("gather_l0")

    n_ada = w_ada.shape[2]
    b_ada_mine = lax.dynamic_slice_in_dim(b_ada, chip * n_ada, n_ada, axis=1)
    ada_part = _ada_fwd(c_all, w_ada, b_ada_mine.reshape(DEPTH, 1, n_ada), name="ada_fwd")
    ada_all = _all_gather8(_pack_small([ada_part]), name="gather_ada")
    ada_st = _unpack_small(ada_all, [ada_part.shape])[0][0::2]
    ada_rows = jnp.concatenate([ada_st[j] for j in range(N_CHIPS)], axis=-1)
    ada_mine = lax.dynamic_index_in_dim(ada_rows, me, axis=1, keepdims=False)

    def layer_params(l):
        p = {k: conv_full[k][l] for k in _SMALL_SHARDED}
        for k in ("b_in", "ln_v_g", "ln_v_b", "pool_scale", "ln1_g", "ln1_b", "b_up", "conv_ffn_b", "ln2_g", "ln2_b"):
            p[k] = w[k][l].reshape(1, -1)
        p["w_spatial"] = w_spatial[l]
        p["b_spatial_t"] = b_spatial[l].T
        return p

    xs = x[0]
    saved, adas, params = [], [], []
    for l in range(DEPTH):
        ada = [ada_mine[l, k * d:(k + 1) * d].reshape(1, d) for k in range(6)]
        p = layer_params(l)
        xs, sv = _layer_fwd(xs, ada, p, l, tr)
        saved.append(sv), adas.append(ada), params.append(p)
    dx, loss_blk = _loss_fwd(xs, loss_target[0], ts=512, name="loss")

    grads, dadas = [None] * DEPTH, [None] * DEPTH
    for l in reversed(range(DEPTH)):
        dx, grads[l], dadas[l] = _layer_bwd(dx, adas[l], params[l], saved[l], l, tr)
    tr.alone("join_tail")
    dada = jnp.stack(dadas).reshape(DEPTH, 6 * d)

    small_names = _SMALL_REPL + _SMALL_SHARDED
    small_g = [jnp.stack([grads[l][k] for l in range(DEPTH)]) for k in small_names]
    gsum = dict(zip(small_names, _unpack_small(_all_reduce_small(_pack_small(small_g), name="reduce_small"),
                                               [a.shape for a in small_g])))
    tail_g = [dada, loss_blk[0:1, 0:1]]
    tail_all, tail_sum = _all_gather8(_pack_small(tail_g), name="gather_dada", with_sum=True)
    gsum["b_ada"], loss_sum = _unpack_small(tail_sum, [a.shape for a in tail_g])
    loss = loss_sum[0, 0]
    dada_all = _unpack_small(tail_all, [a.shape for a in tail_g])[0]
    for k in _SMALL_SHARDED:
        wd = gsum[k].shape[-1] // N_CHIPS
        gsum[k] = lax.dynamic_slice_in_dim(gsum[k], chip * wd, wd, axis=gsum[k].ndim - 1)

    dada_cols = lax.dynamic_slice_in_dim(dada_all, chip * n_ada, n_ada, axis=2)
    dada_cols = jnp.pad(jnp.swapaxes(dada_cols, 0, 1), ((0, 0), (0, N_DEV), (0, 0)))
    gsum["w_ada"] = _ada_bwd(jnp.pad(c_all, ((0, N_DEV), (0, 0))), dada_cols, name="ada_bwd")

    out_g, out_d, out_m, out_v = {}, {}, {}, {}
    for k in _WEIGHTS:
        shp = w[k].shape
        if k in _BIG:
            res = tr.run(f"adamw_{k}", lambda cm: _adamw_sharded(
                _shard3(w[k]), _shard3(m[k]), _shard3(v[k]), [tr.final[(l, k)] for l in range(DEPTH)],
                name=f"adamw_{k}", comm=cm))
        else:
            gk = gsum[k].reshape(shp)
            res = [gk] + list(_adamw(_as2d(w[k]), _as2d(gk), _as2d(m[k]), _as2d(v[k]), name=f"adamw_{k}"))
        out_g[k], out_d[k], out_m[k], out_v[k] = [r.reshape(shp) for r in res]

    return (loss, dx[None], *[out_g[k] for k in _WEIGHTS], *[out_d[k] for k in _WEIGHTS],
            *[out_m[k] for k in _WEIGHTS], *[out_v[k] for k in _WEIGHTS])
```

```python
import math
from typing import Callable, NamedTuple

import jax
import jax.numpy as jnp
from jax import lax
from jax.experimental import pallas as pl
from jax.experimental.pallas import tpu as pltpu

F32 = jnp.float32
BF16 = jnp.bfloat16

D_MODEL = 1024
D_Z = 9216
D_FF = 2816
N_GROUPS = 8
GBLK = 128
CHUNK = 64
POOL_WINDOWS = (2, 4, 8, 16)
POOL_GROUP = 256
POOL_HALO = 16
CONV_HALO = 8
HALO_ROWS = 16
DEPTH = 2
ALPHA = (2 * DEPTH) ** 0.25
LN_EPS = 1e-5
ADAM_LR, ADAM_B1, ADAM_B2, ADAM_EPS, ADAM_WD, ADAM_STEP = 0.001, 0.9, 0.999, 1e-08, 0.01, 10
N_CHIPS = 4
N_DEV = 8
FF_CHUNK = 1408
MESH = pl.DeviceIdType.MESH
VMEM_LIMIT = 56 * 1024 * 1024
HBM = pl.BlockSpec(memory_space=pl.ANY)


def _dot(a, b):
    return jnp.dot(a, b, preferred_element_type=F32)


def _dot_nt(a, b):
    return lax.dot_general(a, b, (((1,), (1,)), ((), ())), preferred_element_type=F32)


def _dot_tn(a, b):
    return lax.dot_general(a, b, (((0,), (0,)), ((), ())), preferred_element_type=F32)


_GELU_C = math.sqrt(2.0 / math.pi)


def _gelu_and_grad(x):
    x2 = x * x
    t = jnp.tanh(_GELU_C * (x + 0.044715 * x * x2))
    g = 0.5 * x * (1.0 + t)
    dg = 0.5 * (1.0 + t) + 0.5 * x * (1.0 - t * t) * (_GELU_C * (1.0 + 3 * 0.044715 * x2))
    return g, dg


def _gelu(x):
    return 0.5 * x * (1.0 + jnp.tanh(_GELU_C * (x + 0.044715 * x * x * x)))


def _sigmoid(x):
    return 1.0 / (1.0 + jnp.exp(-x))


def _ln_fwd(r):
    mu = jnp.mean(r, axis=-1, keepdims=True)
    xc = r - mu
    var = jnp.mean(xc * xc, axis=-1, keepdims=True)
    rstd = lax.rsqrt(var + LN_EPS)
    return xc * rstd, rstd


def _ln_bwd(dy, g, xhat, rstd):
    dxh = dy * g
    m1 = jnp.mean(dxh, axis=-1, keepdims=True)
    m2 = jnp.mean(dxh * xhat, axis=-1, keepdims=True)
    return rstd * (dxh - m1 - xhat * m2)


def _rows_before(ext, k, halo):
    return pltpu.roll(ext, k, 0)[halo:]


def _rows_after(ext, k, n):
    return pltpu.roll(ext, ext.shape[0] - k, 0)[:n]


def _colsum(v):
    return jnp.sum(v, axis=0, keepdims=True)


def _spatial_mask():
    i = lax.broadcasted_iota(jnp.int32, (GBLK, GBLK), 0)
    j = lax.broadcasted_iota(jnp.int32, (GBLK, GBLK), 1)
    return (j // CHUNK) <= (i // CHUNK)


def _const(shape):
    n = len(shape)
    return pl.BlockSpec(shape, lambda *_: (0,) * n)


def _resident(shape):
    n = len(shape)
    return pl.BlockSpec(shape, lambda *_: (0,) * n, pipeline_mode=pl.Buffered(1))


def _tile(ts, s):
    return min(ts, s)


class _Comm(NamedTuple):
    srcs: tuple
    dsts: tuple
    n_remote: int
    n_local: int
    build: Callable
    alias: tuple = ()


def _my_coords():
    return lax.axis_index("x"), lax.axis_index("y"), lax.axis_index("c")


def _chip_peer(k):
    mx, my, mc = _my_coords()
    return (mx ^ ((k >> 1) & 1), my ^ (k & 1), mc)


def _sem_scratch(comm):
    return [pltpu.SemaphoreType.DMA((max(comm.n_remote, 1),)), pltpu.SemaphoreType.DMA((max(comm.n_remote, 1),)),
            pltpu.SemaphoreType.DMA((max(comm.n_local, 1),))]


def _run(body, *, name, grid, in_specs, out_specs, out_shape, args, scratch_shapes=(), comm=None, aliases=None,
         prefetch=None):
    sem = ("arbitrary",) * len(grid)
    cparams = pltpu.CompilerParams(dimension_semantics=sem, vmem_limit_bytes=VMEM_LIMIT)
    aliases = dict(aliases or {})
    n_pre = 0 if prefetch is None else 1

    def call(fn, in_specs, out_specs, out_shape, scratch_shapes, args):
        if prefetch is None:
            return pl.pallas_call(fn, name=name, grid=grid, in_specs=in_specs, out_specs=out_specs, out_shape=out_shape,
                                  scratch_shapes=scratch_shapes, compiler_params=cparams,
                                  input_output_aliases=aliases)(*args)
        spec = pltpu.PrefetchScalarGridSpec(num_scalar_prefetch=1, grid=grid, in_specs=in_specs, out_specs=out_specs,
                                            scratch_shapes=scratch_shapes)
        return pl.pallas_call(fn, name=name, grid_spec=spec, out_shape=out_shape, compiler_params=cparams,
                              input_output_aliases={k + 1: v for k, v in aliases.items()})(prefetch, *args)

    if comm is None:
        return call(body, list(in_specs), list(out_specs), list(out_shape), list(scratch_shapes), args)
    n_in, n_cs, n_out, n_cd, n_scr = len(in_specs), len(comm.srcs), len(out_specs), len(comm.dsts), len(scratch_shapes)
    aliases.update({n_in + si: n_out + di for si, di in comm.alias})
    total = math.prod(grid)
    mid_step = min(total - 1, int(total * 0.9))

    def wrapped(*refs):
        pre, refs = refs[:n_pre], refs[n_pre:]
        ins, refs = refs[:n_in], refs[n_in:]
        csrc, refs = refs[:n_cs], refs[n_cs:]
        outs, refs = refs[:n_out], refs[n_out:]
        cdst, refs = refs[:n_cd], refs[n_cd:]
        scr, sems = refs[:n_scr], refs[n_scr:]
        step = pl.program_id(0)
        for ax in range(1, len(grid)):
            step = step * grid[ax] + pl.program_id(ax)
        first, mid, last = comm.build(csrc, cdst, *sems, 0, 0)
        pl.when(step == 0)(first)
        if mid is not None:
            pl.when(step == mid_step)(mid)
        body(*pre, *ins, *outs, *scr)
        pl.when(step == total - 1)(last)

    res = call(wrapped, list(in_specs) + [HBM] * n_cs, list(out_specs) + [HBM] * n_cd,
               list(out_shape) + list(comm.dsts), list(scratch_shapes) + _sem_scratch(comm), (*args, *comm.srcs))
    return res[:n_out], res[n_out:]


def _comm_call(comm, *, name):
    def body(*refs):
        n_cs, n_cd = len(comm.srcs), len(comm.dsts)
        first, mid, last = comm.build(refs[:n_cs], refs[n_cs:n_cs + n_cd], *refs[n_cs + n_cd:], 0, 0)
        first()
        if mid is not None:
            mid()
        last()

    return pl.pallas_call(body, name=name, in_specs=[HBM] * len(comm.srcs), out_specs=[HBM] * len(comm.dsts),
                          out_shape=list(comm.dsts), scratch_shapes=_sem_scratch(comm),
                          input_output_aliases=dict(comm.alias))(*comm.srcs)


def _gather_comm(bufs, peers=(1, 2, 3)):
    dsts = tuple(jax.ShapeDtypeStruct(b.shape, b.dtype) for b in bufs)
    nw = len(bufs)

    def build(srcs, outs, send_sems, recv_sems, local_sems, r0, l0):
        mx, my, mc = _my_coords()
        me = 2 * mx + my
        sibling = (mx, my, 1 - mc)

        def rdma(src, dst, idx, peer):
            return pltpu.make_async_remote_copy(src_ref=src, dst_ref=dst, send_sem=send_sems.at[r0 + idx],
                                                recv_sem=recv_sems.at[r0 + idx], device_id=peer, device_id_type=MESH)

        def ici(w, k, slot):
            return rdma(outs[w].at[me, mc], outs[w].at[slot, mc], 6 * w + k - 1, _chip_peer(k))

        def fwd(w, k, half):
            return rdma(outs[w].at[me ^ k, mc], outs[w].at[me ^ k, half], 6 * w + 2 + k, sibling)

        def first():
            for w in range(nw):
                for k in peers:
                    ici(w, k, me).start()

        def mid():
            for w in range(nw):
                for k in peers:
                    ici(w, k, me ^ k).wait_recv()
                    fwd(w, k, mc).start()

        def last():
            for w in range(nw):
                for k in peers:
                    fwd(w, k, 1 - mc).wait_recv()
                    ici(w, k, me).wait_send()
                    fwd(w, k, mc).wait_send()

        return first, mid, last

    return _Comm(tuple(bufs), dsts, 6 * nw, 0, build, tuple((w, w) for w in range(nw)))


def _symmetric(make_remote, make_local, make_incoming=None):
    def first():
        for cp in make_remote() + make_local():
            cp.start()

    def last():
        for cp in (make_incoming or make_remote)():
            cp.wait_recv()
        for cp in make_remote():
            cp.wait_send()
        for cp in make_local():
            cp.wait()

    return first, None, last


def _presum_comm(g_bf16):
    nw = len(g_bf16)
    dsts = tuple(jax.ShapeDtypeStruct((N_CHIPS,) + g.shape[2:], BF16) for g in g_bf16)

    def build(srcs, outs, send_sems, recv_sems, local_sems, r0, l0):
        mx, my, mc = _my_coords()

        def remote():
            return [pltpu.make_async_remote_copy(
                src_ref=srcs[w].at[j, 1 - mc], dst_ref=outs[w].at[j], send_sem=send_sems.at[r0 + N_CHIPS * w + j],
                recv_sem=recv_sems.at[r0 + N_CHIPS * w + j], device_id=(mx, my, 1 - mc), device_id_type=MESH)
                for w in range(nw) for j in range(N_CHIPS)]

        return _symmetric(remote, lambda: [])

    return _Comm(tuple(g_bf16), dsts, N_CHIPS * nw, 0, build)


def _scatter_comm(h_bf16):
    nw = len(h_bf16)
    dsts = tuple(jax.ShapeDtypeStruct((N_CHIPS - 1,) + h.shape[1:], BF16) for h in h_bf16)

    def build(srcs, outs, send_sems, recv_sems, local_sems, r0, l0):
        mx, my, _ = _my_coords()
        me = 2 * mx + my

        def remote():
            return [pltpu.make_async_remote_copy(
                src_ref=srcs[w].at[me ^ k], dst_ref=outs[w].at[k - 1], send_sem=send_sems.at[r0 + 3 * w + k - 1],
                recv_sem=recv_sems.at[r0 + 3 * w + k - 1], device_id=_chip_peer(k), device_id_type=MESH)
                for w in range(nw) for k in range(1, N_CHIPS)]

        return _symmetric(remote, lambda: [])

    return _Comm(tuple(h_bf16), dsts, 3 * nw, 0, build)


def _join_comm(bufs):
    nw = len(bufs)
    dsts = tuple(jax.ShapeDtypeStruct(b.shape, b.dtype) for b in bufs)

    def build(srcs, outs, send_sems, recv_sems, local_sems, r0, l0):
        mx, my, mc = _my_coords()

        def remote(half=mc):
            return [pltpu.make_async_remote_copy(
                src_ref=outs[w].at[mc], dst_ref=outs[w].at[half], send_sem=send_sems.at[r0 + w],
                recv_sem=recv_sems.at[r0 + w], device_id=(mx, my, 1 - mc), device_id_type=MESH) for w in range(nw)]

        return _symmetric(remote, lambda: [], lambda: remote(1 - mc))

    return _Comm(tuple(bufs), dsts, nw, 0, build, tuple((w, w) for w in range(nw)))


def _merge(comms):
    comms = list(comms)
    if len(comms) == 1:
        return comms[0]

    def build(srcs, outs, send_sems, recv_sems, local_sems, r0, l0):
        phases, s0, d0 = [], 0, 0
        for cm in comms:
            phases.append(cm.build(srcs[s0:s0 + len(cm.srcs)], outs[d0:d0 + len(cm.dsts)], send_sems, recv_sems,
                                   local_sems, r0, l0))
            s0, d0, r0, l0 = s0 + len(cm.srcs), d0 + len(cm.dsts), r0 + cm.n_remote, l0 + cm.n_local

        def run(idx):
            fns = [ph[idx] for ph in phases if ph[idx] is not None]
            if not fns:
                return None

            def go():
                for fn in fns:
                    fn()
            return go

        return run(0), run(1), run(2)

    alias, s0, d0 = [], 0, 0
    for cm in comms:
        alias += [(s0 + si, d0 + di) for si, di in cm.alias]
        s0, d0 = s0 + len(cm.srcs), d0 + len(cm.dsts)
    return _Comm(sum((cm.srcs for cm in comms), ()), sum((cm.dsts for cm in comms), ()),
                 sum(cm.n_remote for cm in comms), sum(cm.n_local for cm in comms), build, tuple(alias))


def _split(comms, res):
    out, d0 = [], 0
    for cm in comms:
        out.append(list(res[d0:d0 + len(cm.dsts)]))
        d0 += len(cm.dsts)
    return out


def _all_reduce_small(x, *, name):
    r, lanes = x.shape
    half = r // 2
    assert half % 8 == 0

    def body(x_ref, out_ref, sib_ref, slots_ref, send_sems, recv_sems):
        mx, my, mc = _my_coords()
        me = 2 * mx + my
        sibling = (mx, my, 1 - mc)
        mine = pl.ds(pl.multiple_of(mc * half, 8), half)
        theirs = pl.ds(pl.multiple_of((1 - mc) * half, 8), half)

        def to_sibling(src, dst, idx):
            return pltpu.make_async_remote_copy(src_ref=src, dst_ref=dst, send_sem=send_sems.at[idx],
                                                recv_sem=recv_sems.at[idx], device_id=sibling, device_id_type=MESH)

        swap = to_sibling(x_ref.at[theirs], sib_ref, 0)
        swap.start()
        swap.wait_recv()
        swap.wait_send()
        slots_ref[me] = x_ref[mine, :] + sib_ref[...]

        def copy(k, slot):
            return pltpu.make_async_remote_copy(
                src_ref=slots_ref.at[me], dst_ref=slots_ref.at[slot], send_sem=send_sems.at[k], recv_sem=recv_sems.at[k],
                device_id=_chip_peer(k), device_id_type=MESH)

        sends = [copy(k, me) for k in range(1, N_CHIPS)]
        for cp in sends:
            cp.start()
        for k in range(1, N_CHIPS):
            copy(k, me ^ k).wait_recv()
        for cp in sends:
            cp.wait_send()
        acc = slots_ref[0]
        for j in range(1, N_CHIPS):
            acc = acc + slots_ref[j]
        out_ref[mine, :] = acc
        join = to_sibling(out_ref.at[mine], out_ref.at[mine], N_CHIPS)
        join.start()
        to_sibling(out_ref.at[mine], out_ref.at[theirs], N_CHIPS).wait_recv()
        join.wait_send()

    vmem = pl.BlockSpec(memory_space=pltpu.VMEM)
    return pl.pallas_call(
        body, name=name, in_specs=[vmem], out_specs=vmem, out_shape=jax.ShapeDtypeStruct((r, lanes), F32),
        scratch_shapes=[pltpu.VMEM((half, lanes), F32), pltpu.VMEM((N_CHIPS, half, lanes), F32),
                        pltpu.SemaphoreType.DMA((N_CHIPS + 1,)), pltpu.SemaphoreType.DMA((N_CHIPS + 1,))],
        compiler_params=pltpu.CompilerParams(vmem_limit_bytes=VMEM_LIMIT),
    )(x)


def _all_gather8(x, *, name, with_sum=False):
    r, lanes = x.shape

    def body(x_ref, out_ref, *rest):
        if with_sum:
            sum_ref, send_sems, recv_sems, local_sem = rest
        else:
            send_sems, recv_sems, local_sem = rest
        mx, my, mc = _my_coords()
        me = 4 * mx + 2 * my + mc

        def peer(k):
            return (mx ^ ((k >> 2) & 1), my ^ ((k >> 1) & 1), mc ^ (k & 1))

        def copy(k, slot):
            return pltpu.make_async_remote_copy(
                src_ref=x_ref, dst_ref=out_ref.at[slot], send_sem=send_sems.at[k - 1], recv_sem=recv_sems.at[k - 1],
                device_id=peer(k), device_id_type=MESH)

        mine = pltpu.make_async_copy(x_ref, out_ref.at[me], local_sem)
        mine.start()
        sends = [copy(k, me) for k in range(1, N_DEV)]
        for cp in sends:
            cp.start()
        for k in range(1, N_DEV):
            copy(k, me ^ k).wait_recv()
        for cp in sends:
            cp.wait_send()
        mine.wait()
        if with_sum:
            acc = out_ref[0]
            for k in range(1, N_DEV):
                acc = acc + out_ref[k]
            sum_ref[...] = acc

    vmem = pl.BlockSpec(memory_space=pltpu.VMEM)
    out_shape = [jax.ShapeDtypeStruct((N_DEV, r, lanes), F32)]
    if with_sum:
        out_shape.append(jax.ShapeDtypeStruct((r, lanes), F32))
    res = pl.pallas_call(
        body, name=name, in_specs=[vmem], out_specs=[vmem] * len(out_shape), out_shape=out_shape,
        scratch_shapes=[pltpu.SemaphoreType.DMA((N_DEV - 1,)), pltpu.SemaphoreType.DMA((N_DEV - 1,)),
                        pltpu.SemaphoreType.DMA],
        compiler_params=pltpu.CompilerParams(vmem_limit_bytes=VMEM_LIMIT),
    )(x)
    return res if with_sum else res[0]


def _mod_matmul(x, sc, sh, w4, b, *, ts, tn, name, comm=None):
    s, d = x.shape
    wd = w4.shape[2]
    n = N_CHIPS * wd
    per = wd // tn
    ts = _tile(ts, s)

    def body(x_ref, sc_ref, sh_ref, w_ref, b_ref, o_ref, ht_ref, h_scr):
        @pl.when(pl.program_id(1) == 0)
        def _():
            h = x_ref[...] * (1.0 + sc_ref[...]) + sh_ref[...]
            h_scr[...] = h.astype(BF16)
            ht_ref[...] = h.T.astype(BF16)
        o_ref[...] = (_dot(h_scr[...], w_ref[0]) + b_ref[...]).astype(BF16)

    return _run(
        body, name=name, grid=(s // ts, n // tn),
        in_specs=[pl.BlockSpec((ts, d), lambda i, j: (i, 0)), _const((1, d)), _const((1, d)),
                  pl.BlockSpec((1, d, tn), lambda i, j: (j // per, 0, j % per)),
                  pl.BlockSpec((1, tn), lambda i, j: (0, j))],
        out_specs=[pl.BlockSpec((ts, tn), lambda i, j: (i, j)), pl.BlockSpec((d, ts), lambda i, j: (0, i))],
        out_shape=[jax.ShapeDtypeStruct((s, n), BF16), jax.ShapeDtypeStruct((d, s), BF16)],
        scratch_shapes=[pltpu.VMEM((ts, d), BF16)],
        args=(x, sc, sh, w4, b), comm=comm)


def _in_proj_own(x, sc, sh, w_shard, layer, b, chip, *, ts, name, comm=None):
    s, d = x.shape
    wd = w_shard.shape[2]
    ts = _tile(ts, s)

    def body(chip_ref, x_ref, sc_ref, sh_ref, w_ref, b_ref, o_ref, ht_ref, w_scr):
        @pl.when(pl.program_id(0) == 0)
        def _():
            w_scr[...] = w_ref[0].astype(BF16)
        h = x_ref[...] * (1.0 + sc_ref[...]) + sh_ref[...]
        ht_ref[...] = h.T.astype(BF16)
        o_ref[...] = (_dot(h.astype(BF16), w_scr[...]) + b_ref[...]).astype(BF16)

    return _run(
        body, name=name, grid=(s // ts,), prefetch=chip,
        in_specs=[pl.BlockSpec((ts, d), lambda i, c: (i, 0)), _const((1, d)), _const((1, d)),
                  pl.BlockSpec((1, d, wd), lambda i, c: (layer, 0, 0), pipeline_mode=pl.Buffered(1)),
                  pl.BlockSpec((1, wd), lambda i, c: (0, c[0]))],
        out_specs=[pl.BlockSpec((ts, wd), lambda i, c: (i, c[0])), pl.BlockSpec((d, ts), lambda i, c: (0, i))],
        out_shape=[jax.ShapeDtypeStruct((s, N_CHIPS * wd), BF16), jax.ShapeDtypeStruct((d, s), BF16)],
        scratch_shapes=[pltpu.VMEM((d, wd), BF16)],
        args=(x, sc, sh, w_shard, b), comm=comm)


def _in_proj_rest(x, sc, sh, w4, b, chip, z_own, *, ts, name, comm=None):
    s, d = x.shape
    wd = w4.shape[2]
    ts = _tile(ts, s)

    def body(chip_ref, x_ref, sc_ref, sh_ref, w_ref, b_ref, z_in, o_ref, h_scr):
        @pl.when(pl.program_id(1) == 0)
        def _():
            h_scr[...] = (x_ref[...] * (1.0 + sc_ref[...]) + sh_ref[...]).astype(BF16)
        o_ref[...] = (_dot(h_scr[...], w_ref[0]) + b_ref[...]).astype(BF16)

    return _run(
        body, name=name, grid=(s // ts, N_CHIPS - 1), prefetch=chip,
        in_specs=[pl.BlockSpec((ts, d), lambda i, j, c: (i, 0)), _const((1, d)), _const((1, d)),
                  pl.BlockSpec((1, d, wd), lambda i, j, c: (c[0] ^ (j + 1), 0, 0)),
                  pl.BlockSpec((1, wd), lambda i, j, c: (0, c[0] ^ (j + 1))), HBM],
        out_specs=[pl.BlockSpec((ts, wd), lambda i, j, c: (i, c[0] ^ (j + 1)))],
        out_shape=[jax.ShapeDtypeStruct(z_own.shape, BF16)],
        scratch_shapes=[pltpu.VMEM((ts, d), BF16)],
        args=(x, sc, sh, w4, b, z_own), aliases={5: 0}, comm=comm)


def _conv3(q, ext, cw):
    return cw[2:3] * q + cw[1:2] * _rows_before(ext, 1, CONV_HALO) + cw[0:1] * _rows_before(ext, 2, CONV_HALO)


def _mix_a_fwd(z, cw, w_out, *, ts, name, comm=None):
    s = z.shape[0]
    d = D_MODEL
    ts = _tile(ts, s)

    def body(zb_ref, zc_ref, zx_ref, cw_ref, w_ref, a_ref, y_ref, carry):
        @pl.when(pl.program_id(0) == 0)
        def _():
            carry[...] = jnp.zeros_like(carry)
        q = zc_ref[...].astype(F32) * zx_ref[...].astype(F32)
        ext = jnp.concatenate([carry[...], q], axis=0)
        a = (zb_ref[...].astype(F32) * _conv3(q, ext, cw_ref[...])).astype(BF16)
        carry[...] = q[ts - CONV_HALO:]
        a_ref[...] = a
        y_ref[...] = _dot(a, w_ref[...])

    zspec = lambda k: pl.BlockSpec((ts, d), lambda i, k=k: (i, k))
    return _run(
        body, name=name, grid=(s // ts,),
        in_specs=[zspec(0), zspec(1), zspec(2), _const((3, d)), _const((d, d))],
        out_specs=[pl.BlockSpec((ts, d), lambda i: (i, 0))] * 2,
        out_shape=[jax.ShapeDtypeStruct((s, d), BF16), jax.ShapeDtypeStruct((s, d), F32)],
        scratch_shapes=[pltpu.VMEM((CONV_HALO, d), F32)],
        args=(z, z, z, cw, w_out), comm=comm)


def _spatial_mix(vn_b, ws_ref, bst_ref, mixed_scr, ts):
    nblk = ts // GBLK
    mask = _spatial_mask()
    for g in range(N_GROUPS):
        cols = slice(g * GBLK, (g + 1) * GBLK)
        wm = jnp.where(mask, ws_ref[g], 0.0).astype(BF16)
        cat = jnp.concatenate([vn_b[n * GBLK:(n + 1) * GBLK, cols] for n in range(nblk)], axis=1)
        res = _dot(wm, cat) + bst_ref[:, g:g + 1]
        for n in range(nblk):
            mixed_scr[n * GBLK:(n + 1) * GBLK, cols] = res[:, n * GBLK:(n + 1) * GBLK]


def _mix_b_fwd(z, ln_g, ln_b, ws, bst, w_out, *, ts, name, comm=None):
    s = z.shape[0]
    d = D_MODEL
    ts = _tile(ts, s)

    def body(zu_ref, zv_ref, g_ref, b_ref, ws_ref, bst_ref, w_ref, sg_ref, y_ref, mixed_scr):
        xhat, _ = _ln_fwd(_gelu(zv_ref[...].astype(F32)))
        vn = (xhat * g_ref[...] + b_ref[...]).astype(BF16)
        _spatial_mix(vn, ws_ref, bst_ref, mixed_scr, ts)
        sg = (_gelu(zu_ref[...].astype(F32)) * mixed_scr[...]).astype(BF16)
        sg_ref[...] = sg
        y_ref[...] = _dot(sg, w_ref[...])

    zspec = lambda k: pl.BlockSpec((ts, d), lambda i, k=k: (i, k))
    return _run(
        body, name=name, grid=(s // ts,),
        in_specs=[zspec(3), zspec(4), _const((1, d)), _const((1, d)), _const((N_GROUPS, GBLK, GBLK)),
                  _const((GBLK, N_GROUPS)), _const((d, d))],
        out_specs=[pl.BlockSpec((ts, d), lambda i: (i, 0))] * 2,
        out_shape=[jax.ShapeDtypeStruct((s, d), BF16), jax.ShapeDtypeStruct((s, d), F32)],
        scratch_shapes=[pltpu.VMEM((ts, d), F32)],
        args=(z, z, ln_g, ln_b, ws, bst, w_out), comm=comm)


def _pool_denoms(tile_idx, ts):
    t1 = (tile_idx * ts + 1 + lax.broadcasted_iota(jnp.int32, (ts, 1), 0)).astype(F32)
    return [jnp.minimum(t1, float(w)) for w in POOL_WINDOWS]


def _pool_diff(p, ext, denoms, k):
    cols = slice(k * POOL_GROUP, (k + 1) * POOL_GROUP)
    acc = ext[:, cols]
    step = 1
    while step < POOL_WINDOWS[k]:
        acc = acc + pltpu.roll(acc, step, 0)
        step *= 2
    return acc[POOL_HALO:] / denoms[k] - p[:, cols]


def _mix_c_fwd(z, w_pool, scale, *, ts, name, comm=None):
    s = z.shape[0]
    d = D_MODEL
    ts = _tile(ts, s)

    def body(zp_ref, w_ref, sc_ref, d_ref, y_ref, carry):
        i = pl.program_id(0)

        @pl.when(i == 0)
        def _():
            carry[...] = jnp.zeros_like(carry)
        p = zp_ref[...].astype(F32)
        ext = jnp.concatenate([carry[...], p], axis=0)
        carry[...] = p[ts - POOL_HALO:]
        denoms = _pool_denoms(i, ts)
        for k in range(len(POOL_WINDOWS)):
            cols = slice(k * POOL_GROUP, (k + 1) * POOL_GROUP)
            dk = _pool_diff(p, ext, denoms, k).astype(BF16)
            d_ref[:, cols] = dk
            y_ref[:, cols] = _dot(dk, w_ref[k]) * sc_ref[:, cols]

    return _run(
        body, name=name, grid=(s // ts,),
        in_specs=[pl.BlockSpec((ts, d), lambda i: (i, 5)), _const((4, POOL_GROUP, POOL_GROUP)), _const((1, d))],
        out_specs=[pl.BlockSpec((ts, d), lambda i: (i, 0))] * 2,
        out_shape=[jax.ShapeDtypeStruct((s, d), BF16), jax.ShapeDtypeStruct((s, d), F32)],
        scratch_shapes=[pltpu.VMEM((POOL_HALO, d), F32)],
        args=(z, w_pool, scale), comm=comm)


def _mix_o_fwd(x, z, ya, yb, yc, w_o, gt, ln_g, ln_b, *, ts, name, comm=None):
    s, d = x.shape
    ts = _tile(ts, s)

    def body(x_ref, ga_ref, gb_ref, gc_ref, ya_ref, yb_ref, yc_ref, w_ref, gt_ref, g_ref, b_ref,
             m_ref, o_ref, x1_ref):
        merged = (_sigmoid(ga_ref[...].astype(F32)) * ya_ref[...] + _sigmoid(gb_ref[...].astype(F32)) * yb_ref[...]
                  + _sigmoid(gc_ref[...].astype(F32)) * yc_ref[...]).astype(BF16)
        m_ref[...] = merged
        o = _dot(merged, w_ref[...])
        o_ref[...] = o
        xhat, _ = _ln_fwd(ALPHA * x_ref[...] + gt_ref[...] * o)
        x1_ref[...] = xhat * g_ref[...] + b_ref[...]

    row = pl.BlockSpec((ts, d), lambda i: (i, 0))
    zspec = lambda k: pl.BlockSpec((ts, d), lambda i, k=k: (i, k))
    return _run(
        body, name=name, grid=(s // ts,),
        in_specs=[row, zspec(6), zspec(7), zspec(8), row, row, row, _const((d, d)),
                  _const((1, d)), _const((1, d)), _const((1, d))],
        out_specs=[row] * 3,
        out_shape=[jax.ShapeDtypeStruct((s, d), BF16), jax.ShapeDtypeStruct((s, d), F32),
                   jax.ShapeDtypeStruct((s, d), F32)],
        args=(x, z, z, z, ya, yb, yc, w_o, gt, ln_g, ln_b), comm=comm)


def _ffn_fwd(up, x1, cw, cb, w_down, gt, ln_g, ln_b, *, ts, name, comm=None):
    s, d = x1.shape
    ts = _tile(ts, s)

    def body(up_ref, x1_ref, cw_ref, cb_ref, w_ref, gt_ref, g_ref, b_ref, ft_ref, dn_ref, x2_ref, carry, f_ref):
        @pl.when(pl.program_id(0) == 0)
        def _():
            carry[...] = jnp.zeros_like(carry)
        for c in range(D_FF // FF_CHUNK):
            ca = slice(c * FF_CHUNK, (c + 1) * FF_CHUNK)
            cg = slice(D_FF + c * FF_CHUNK, D_FF + (c + 1) * FF_CHUNK)
            ua = up_ref[:, ca].astype(F32)
            ext = jnp.concatenate([carry[:, ca], ua], axis=0)
            carry[:, ca] = ua[ts - CONV_HALO:]
            cf = _conv3(ua, ext, cw_ref[:, ca]) + cb_ref[:, ca]
            f = _gelu(cf) * up_ref[:, cg].astype(F32)
            f_ref[:, ca] = f.astype(BF16)
            ft_ref[ca, :] = f.T.astype(BF16)
        dn = _dot(f_ref[...], w_ref[...])
        dn_ref[...] = dn
        xhat, _ = _ln_fwd(ALPHA * x1_ref[...] + gt_ref[...] * dn)
        x2_ref[...] = xhat * g_ref[...] + b_ref[...]

    row = pl.BlockSpec((ts, d), lambda i: (i, 0))
    return _run(
        body, name=name, grid=(s // ts,),
        in_specs=[pl.BlockSpec((ts, 2 * D_FF), lambda i: (i, 0)), row, _const((3, D_FF)), _const((1, D_FF)),
                  _resident((D_FF, d)), _const((1, d)), _const((1, d)), _const((1, d))],
        out_specs=[pl.BlockSpec((D_FF, ts), lambda i: (0, i)), row, row],
        out_shape=[jax.ShapeDtypeStruct((D_FF, s), BF16), jax.ShapeDtypeStruct((s, d), F32),
                   jax.ShapeDtypeStruct((s, d), F32)],
        scratch_shapes=[pltpu.VMEM((CONV_HALO, D_FF), F32), pltpu.VMEM((ts, D_FF), BF16)],
        args=(up, x1, cw, cb, w_down, gt, ln_g, ln_b), comm=comm)


def _loss_fwd(y, tgt, *, ts, name):
    s, d = y.shape
    ts = _tile(ts, s)

    def body(y_ref, t_ref, dy_ref, l_ref):
        @pl.when(pl.program_id(0) == 0)
        def _():
            l_ref[...] = jnp.zeros_like(l_ref)
        e = y_ref[...] - t_ref[...]
        dy_ref[...] = e / float(d)
        l_ref[...] += 0.5 * jnp.sum(jnp.mean(e * e, axis=-1, keepdims=True), axis=0, keepdims=True)

    row = pl.BlockSpec((ts, d), lambda i: (i, 0))
    return _run(body, name=name, grid=(s // ts,), in_specs=[row, row], out_specs=[row, _const((8, 128))],
                out_shape=[jax.ShapeDtypeStruct((s, d), F32), jax.ShapeDtypeStruct((8, 128), F32)], args=(y, tgt))


def _rev(n_tiles):
    return lambda i: n_tiles - 1 - i


def _halo_spec(ts, n_tiles, halo, width, col):
    per = ts // halo
    return pl.BlockSpec((halo, width), lambda i: (jnp.maximum((n_tiles - 1 - i) * per - 1, 0), col))


def _ffn_bwd(dx2, x1, dn, up, cw, cb, w_down, w_up4, gt, ln_g, sc, *, ts, name, comm=None):
    s, d = x1.shape
    ts = _tile(ts, s)
    nt = s // ts
    rev = _rev(nt)
    wd = w_up4.shape[2]

    def w_up_cols(wu_ref, start):
        return wu_ref[start // wd, :, start % wd:start % wd + FF_CHUNK]

    def body(dx2_ref, x1_ref, dn_ref, up_ref, halo_ref, cw_ref, cb_ref, wd_ref, wu_ref, gt_ref, g_ref, sc_ref,
             ddn_ref, dup_ref, dx1_ref, redd_ref, redf_ref, dbup_ref, carry):
        i = pl.program_id(0)

        @pl.when(i == 0)
        def _():
            carry[...] = jnp.zeros_like(carry)
            redd_ref[...] = jnp.zeros_like(redd_ref)
            redf_ref[...] = jnp.zeros_like(redf_ref)
            dbup_ref[...] = jnp.zeros_like(dbup_ref)
        first_tile = i == nt - 1
        x1v, dnv, dyv = x1_ref[...], dn_ref[...], dx2_ref[...]
        xhat, rstd = _ln_fwd(ALPHA * x1v + gt_ref[...] * dnv)
        dr = _ln_bwd(dyv, g_ref[...], xhat, rstd)
        redd_ref[0:1, :] += _colsum(dyv * xhat)
        redd_ref[1:2, :] += _colsum(dyv)
        redd_ref[2:3, :] += _colsum(dr * dnv)
        ddn = (gt_ref[...] * dr).astype(BF16)
        ddn_ref[...] = ddn
        dh = jnp.zeros((ts, d), F32)
        for c in range(D_FF // FF_CHUNK):
            ca = slice(c * FF_CHUNK, (c + 1) * FF_CHUNK)
            cg = slice(D_FF + c * FF_CHUNK, D_FF + (c + 1) * FF_CHUNK)
            df = _dot_nt(ddn, wd_ref[ca, :])
            ua, ug = up_ref[:, ca].astype(F32), up_ref[:, cg].astype(F32)
            halo = jnp.where(first_tile, 0.0, halo_ref[:, ca].astype(F32)[HALO_ROWS - CONV_HALO:])
            ext = jnp.concatenate([halo, ua], axis=0)
            u1, u2 = _rows_before(ext, 1, CONV_HALO), _rows_before(ext, 2, CONV_HALO)
            cwc = cw_ref[:, ca]
            gl, dgl = _gelu_and_grad(cwc[2:3] * ua + cwc[1:2] * u1 + cwc[0:1] * u2 + cb_ref[:, ca])
            dug = df * gl
            dcf = df * ug * dgl
            redf_ref[0:1, ca] += _colsum(dcf * u2)
            redf_ref[1:2, ca] += _colsum(dcf * u1)
            redf_ref[2:3, ca] += _colsum(dcf * ua)
            redf_ref[3:4, ca] += _colsum(dcf)
            extd = jnp.concatenate([dcf, carry[:, ca]], axis=0)
            carry[:, ca] = dcf[:CONV_HALO]
            dua = cwc[2:3] * dcf + cwc[1:2] * _rows_after(extd, 1, ts) + cwc[0:1] * _rows_after(extd, 2, ts)
            dbup_ref[0:1, ca] += _colsum(dua)
            dbup_ref[0:1, cg] += _colsum(dug)
            dua_b, dug_b = dua.astype(BF16), dug.astype(BF16)
            dup_ref[:, ca] = dua_b
            dup_ref[:, cg] = dug_b
            dh = dh + _dot_nt(dua_b, w_up_cols(wu_ref, c * FF_CHUNK)) + _dot_nt(dug_b, w_up_cols(wu_ref, D_FF + c * FF_CHUNK))
        dx1_ref[...] = ALPHA * dr + dh * (1.0 + sc_ref[...])
        redd_ref[3:4, :] += _colsum(dh * x1v)
        redd_ref[4:5, :] += _colsum(dh)

    row = pl.BlockSpec((ts, d), lambda i: (rev(i), 0))
    return _run(
        body, name=name, grid=(nt,),
        in_specs=[row, row, row, pl.BlockSpec((ts, 2 * D_FF), lambda i: (rev(i), 0)),
                  _halo_spec(ts, nt, HALO_ROWS, D_FF, 0), _const((3, D_FF)), _const((1, D_FF)),
                  _resident((D_FF, d)), _resident((N_CHIPS, d, wd)), _const((1, d)), _const((1, d)), _const((1, d))],
        out_specs=[row, pl.BlockSpec((ts, 2 * D_FF), lambda i: (rev(i), 0)), row,
                   _const((8, d)), _const((8, D_FF)), _const((8, 2 * D_FF))],
        out_shape=[jax.ShapeDtypeStruct((s, d), BF16), jax.ShapeDtypeStruct((s, 2 * D_FF), BF16),
                   jax.ShapeDtypeStruct((s, d), F32), jax.ShapeDtypeStruct((8, d), F32),
                   jax.ShapeDtypeStruct((8, D_FF), F32), jax.ShapeDtypeStruct((8, 2 * D_FF), F32)],
        scratch_shapes=[pltpu.VMEM((CONV_HALO, D_FF), F32)],
        args=(dx2, x1, dn, up, up, cw, cb, w_down, w_up4, gt, ln_g, sc), comm=comm)


def _accumulate_dw(dw_ref, dwb_ref, xa, dy, first, last):
    @pl.when(first)
    def _():
        dw_ref[...] = jnp.zeros_like(dw_ref)
    dw_ref[...] += _dot_tn(xa, dy)

    @pl.when(last)
    def _():
        dwb_ref[...] = dw_ref[...].astype(BF16)


def _dw_out(d):
    return [_const((d, d))] * 2, [jax.ShapeDtypeStruct((d, d), F32), jax.ShapeDtypeStruct((d, d), BF16)]


def _grad_matmul_t(xt, dy, *, tk, tn, name, by_chip=False, comm=None):
    k, s = xt.shape
    n = dy.shape[1]

    def body(xt_ref, dy_ref, o_ref, ob_ref):
        o = _dot(xt_ref[...], dy_ref[...]).reshape(o_ref.shape)
        o_ref[...] = o
        ob_ref[...] = o.astype(BF16)

    if by_chip:
        assert tk == k
        per = n // N_CHIPS // tn
        ospec = pl.BlockSpec((1, k, tn), lambda j, i: (j // per, 0, j % per))
        shape = (N_CHIPS, k, n // N_CHIPS)
    else:
        ospec = pl.BlockSpec((tk, tn), lambda j, i: (i, j))
        shape = (k, n)
    xspec = _resident((k, s)) if tk == k else pl.BlockSpec((tk, s), lambda j, i: (i, 0))
    dspec = _resident((s, n)) if tn == n else pl.BlockSpec((s, tn), lambda j, i: (0, j))
    return _run(body, name=name, grid=(n // tn, k // tk), in_specs=[xspec, dspec], out_specs=[ospec, ospec],
                out_shape=[jax.ShapeDtypeStruct(shape, F32), jax.ShapeDtypeStruct(shape, BF16)], args=(xt, dy), comm=comm)


def _mix_o_bwd(dx1, x, o, z, ya, yb, yc, merged, w_o, gt, ln_g, *, ts, name, comm=None):
    s, d = x.shape
    ts = _tile(ts, s)
    nt = s // ts

    def body(dx1_ref, x_ref, o_ref, ga_ref, gb_ref, gc_ref, ya_ref, yb_ref, yc_ref, m_ref, w_ref, gt_ref, g_ref,
             dxa_ref, dzg_ref, dya_ref, dyb_ref, dyc_ref, red_ref, dw_ref, dwb_ref):
        i = pl.program_id(0)

        @pl.when(i == 0)
        def _():
            red_ref[...] = jnp.zeros_like(red_ref)
        dyv, ov = dx1_ref[...], o_ref[...]
        xhat, rstd = _ln_fwd(ALPHA * x_ref[...] + gt_ref[...] * ov)
        dr = _ln_bwd(dyv, g_ref[...], xhat, rstd)
        red_ref[0:1, :] += _colsum(dyv * xhat)
        red_ref[1:2, :] += _colsum(dyv)
        red_ref[2:3, :] += _colsum(dr * ov)
        dxa_ref[...] = ALPHA * dr
        d_o = (gt_ref[...] * dr).astype(BF16)
        _accumulate_dw(dw_ref, dwb_ref, m_ref[...], d_o, i == 0, i == nt - 1)
        dm = _dot_nt(d_o, w_ref[...])
        for k, (zg_ref, y_ref, dy_ref) in enumerate(((ga_ref, ya_ref, dya_ref), (gb_ref, yb_ref, dyb_ref),
                                                     (gc_ref, yc_ref, dyc_ref))):
            g = _sigmoid(zg_ref[...].astype(F32))
            dzg_ref[:, k * d:(k + 1) * d] = (dm * y_ref[...] * g * (1.0 - g)).astype(BF16)
            dy_ref[...] = (dm * g).astype(BF16)

    row = pl.BlockSpec((ts, d), lambda i: (i, 0))
    zspec = lambda k: pl.BlockSpec((ts, d), lambda i, k=k: (i, k))
    bf = jax.ShapeDtypeStruct((s, d), BF16)
    dw_specs, dw_shapes = _dw_out(d)
    return _run(
        body, name=name, grid=(nt,),
        in_specs=[row, row, row, zspec(6), zspec(7), zspec(8), row, row, row, row, _const((d, d)),
                  _const((1, d)), _const((1, d))],
        out_specs=[row, pl.BlockSpec((ts, 3 * d), lambda i: (i, 2)), row, row, row, _const((8, d))] + dw_specs,
        out_shape=[jax.ShapeDtypeStruct((s, d), F32), jax.ShapeDtypeStruct((s, D_Z), BF16), bf, bf, bf,
                   jax.ShapeDtypeStruct((8, d), F32)] + dw_shapes,
        args=(dx1, x, o, z, z, z, ya, yb, yc, merged, w_o, gt, ln_g), comm=comm)


def _mix_a_bwd(dya, a, z, dz, cw, w_out, *, ts, name, comm=None):
    s = z.shape[0]
    d = D_MODEL
    ts = _tile(ts, s)
    nt = s // ts
    rev = _rev(nt)

    def body(dya_ref, a_ref, zb_ref, zc_ref, zx_ref, hc_ref, hx_ref, cw_ref, w_ref, dz_in, dz_ref, red_ref,
             dw_ref, dwb_ref, carry):
        i = pl.program_id(0)

        @pl.when(i == 0)
        def _():
            carry[...] = jnp.zeros_like(carry)
            red_ref[...] = jnp.zeros_like(red_ref)
        _accumulate_dw(dw_ref, dwb_ref, a_ref[...], dya_ref[...], i == 0, i == nt - 1)
        zb, zc, zx = zb_ref[...].astype(F32), zc_ref[...].astype(F32), zx_ref[...].astype(F32)
        q = zc * zx
        halo = jnp.where(i == nt - 1, 0.0, (hc_ref[...].astype(F32) * hx_ref[...].astype(F32))[HALO_ROWS - CONV_HALO:])
        ext = jnp.concatenate([halo, q], axis=0)
        q1, q2 = _rows_before(ext, 1, CONV_HALO), _rows_before(ext, 2, CONV_HALO)
        cwv = cw_ref[...]
        cv = cwv[2:3] * q + cwv[1:2] * q1 + cwv[0:1] * q2
        da = _dot_nt(dya_ref[...], w_ref[...])
        dcv = da * zb
        red_ref[0:1, :] += _colsum(dcv * q2)
        red_ref[1:2, :] += _colsum(dcv * q1)
        red_ref[2:3, :] += _colsum(dcv * q)
        extd = jnp.concatenate([dcv, carry[...]], axis=0)
        carry[...] = dcv[:CONV_HALO]
        dq = cwv[2:3] * dcv + cwv[1:2] * _rows_after(extd, 1, ts) + cwv[0:1] * _rows_after(extd, 2, ts)
        dz_ref[:, 0:d] = (da * cv).astype(BF16)
        dz_ref[:, d:2 * d] = (dq * zx).astype(BF16)
        dz_ref[:, 2 * d:3 * d] = (dq * zc).astype(BF16)

    zspec = lambda k: pl.BlockSpec((ts, d), lambda i, k=k: (rev(i), k))
    row = pl.BlockSpec((ts, d), lambda i: (rev(i), 0))
    dw_specs, dw_shapes = _dw_out(d)
    return _run(
        body, name=name, grid=(nt,),
        in_specs=[row, row, zspec(0), zspec(1), zspec(2),
                  _halo_spec(ts, nt, HALO_ROWS, d, 1), _halo_spec(ts, nt, HALO_ROWS, d, 2),
                  _const((3, d)), _const((d, d)), HBM],
        out_specs=[pl.BlockSpec((ts, 3 * d), lambda i: (rev(i), 0)), _const((8, d))] + dw_specs,
        out_shape=[jax.ShapeDtypeStruct((s, D_Z), BF16), jax.ShapeDtypeStruct((8, d), F32)] + dw_shapes,
        scratch_shapes=[pltpu.VMEM((CONV_HALO, d), F32)],
        args=(dya, a, z, z, z, z, z, cw, w_out, dz), aliases={9: 0}, comm=comm)


def _mix_b_bwd(dyb, sg, z, dz, ln_g, ln_b, ws, bst, w_out, *, ts, name, comm=None):
    s = z.shape[0]
    d = D_MODEL
    ts = _tile(ts, s)
    nt = s // ts
    nblk = ts // GBLK

    def body(dyb_ref, sg_ref, zu_ref, zv_ref, g_ref, b_ref, ws_ref, bst_ref, w_ref, dz_in,
             dz_ref, red_ref, dws_ref, dbst_ref, dw_ref, dwb_ref, mixed_scr, dvn_scr, dzv_scr):
        first = (pl.program_id(0) == 0) & (pl.program_id(1) == 0)

        @pl.when(first)
        def _():
            red_ref[...] = jnp.zeros_like(red_ref)
            dws_ref[...] = jnp.zeros_like(dws_ref)
            dbst_ref[...] = jnp.zeros_like(dbst_ref)

        @pl.when(pl.program_id(1) == 0)
        def _():
            _accumulate_dw(dw_ref, dwb_ref, sg_ref[...], dyb_ref[...], first, pl.program_id(0) == nt - 1)
            u, du_dz = _gelu_and_grad(zu_ref[...].astype(F32))
            vg, dv_dz = _gelu_and_grad(zv_ref[...].astype(F32))
            xhat, rstd = _ln_fwd(vg)
            vn = (xhat * g_ref[...] + b_ref[...]).astype(BF16)
            _spatial_mix(vn, ws_ref, bst_ref, mixed_scr, ts)
            dsg = _dot_nt(dyb_ref[...], w_ref[...])
            dz_ref[...] = (dsg * mixed_scr[...] * du_dz).astype(BF16)
            dmix = dsg * u
            mask = _spatial_mask()
            for g in range(N_GROUPS):
                cols = slice(g * GBLK, (g + 1) * GBLK)
                wm = jnp.where(mask, ws_ref[g], 0.0).astype(BF16)
                dm_cat = jnp.concatenate([dmix[n * GBLK:(n + 1) * GBLK, cols] for n in range(nblk)], axis=1)
                vn_cat = jnp.concatenate([vn[n * GBLK:(n + 1) * GBLK, cols] for n in range(nblk)], axis=1)
                dm_b = dm_cat.astype(BF16)
                dbst_ref[:, g:g + 1] += jnp.sum(dm_cat, axis=1, keepdims=True)
                dws_ref[g] += jnp.where(mask, _dot_nt(dm_b, vn_cat), 0.0)
                dvn_cat = _dot_tn(wm, dm_b)
                for n in range(nblk):
                    dvn_scr[n * GBLK:(n + 1) * GBLK, cols] = dvn_cat[:, n * GBLK:(n + 1) * GBLK]
            dvn = dvn_scr[...]
            red_ref[0:1, :] += _colsum(dvn * xhat)
            red_ref[1:2, :] += _colsum(dvn)
            dzv_scr[...] = (_ln_bwd(dvn, g_ref[...], xhat, rstd) * dv_dz).astype(BF16)

        @pl.when(pl.program_id(1) == 1)
        def _():
            dz_ref[...] = dzv_scr[...]

    zspec = lambda k: pl.BlockSpec((ts, d), lambda i, h, k=k: (i, k))
    row = pl.BlockSpec((ts, d), lambda i, h: (i, 0))
    dw_specs, dw_shapes = _dw_out(d)
    return _run(
        body, name=name, grid=(nt, 2),
        in_specs=[row, row, zspec(3), zspec(4), _const((1, d)), _const((1, d)),
                  _const((N_GROUPS, GBLK, GBLK)), _const((GBLK, N_GROUPS)), _const((d, d)), HBM],
        out_specs=[pl.BlockSpec((ts, d), lambda i, h: (i, 3 + h)), _const((8, d)),
                   _const((N_GROUPS, GBLK, GBLK)), _const((GBLK, N_GROUPS))] + dw_specs,
        out_shape=[jax.ShapeDtypeStruct((s, D_Z), BF16), jax.ShapeDtypeStruct((8, d), F32),
                   jax.ShapeDtypeStruct((N_GROUPS, GBLK, GBLK), F32), jax.ShapeDtypeStruct((GBLK, N_GROUPS), F32)]
        + dw_shapes,
        scratch_shapes=[pltpu.VMEM((ts, d), F32), pltpu.VMEM((ts, d), F32), pltpu.VMEM((ts, d), BF16)],
        args=(dyb, sg, z, z, ln_g, ln_b, ws, bst, w_out, dz), aliases={9: 0}, comm=comm)


def _mix_c_bwd(dyc, z, dz, w_pool, scale, *, ts, name):
    s = z.shape[0]
    d = D_MODEL
    ts = _tile(ts, s)
    nt = s // ts
    rev = _rev(nt)

    def body(dyc_ref, zp_ref, halo_ref, w_ref, sc_ref, dz_in, dz_ref, red_ref, dw_ref, carry):
        i = pl.program_id(0)

        @pl.when(i == 0)
        def _():
            carry[...] = jnp.zeros_like(carry)
            red_ref[...] = jnp.zeros_like(red_ref)
            dw_ref[...] = jnp.zeros_like(dw_ref)
        p = zp_ref[...].astype(F32)
        ext = jnp.concatenate([jnp.where(i == nt - 1, 0.0, halo_ref[...].astype(F32)), p], axis=0)
        denoms = _pool_denoms(rev(i), ts)
        dyv = dyc_ref[...].astype(F32)
        for k in range(len(POOL_WINDOWS)):
            cols = slice(k * POOL_GROUP, (k + 1) * POOL_GROUP)
            dk = _pool_diff(p, ext, denoms, k).astype(BF16)
            red_ref[0:1, cols] += _colsum(dyv[:, cols] * _dot(dk, w_ref[k]))
            dpre = (dyv[:, cols] * sc_ref[:, cols]).astype(BF16)
            dw_ref[k] += _dot_tn(dk, dpre)
            dd = _dot_nt(dpre, w_ref[k])
            e = dd / denoms[k]
            acc = jnp.concatenate([e, carry[:, cols]], axis=0)
            carry[:, cols] = e[:POOL_HALO]
            step = 1
            while step < POOL_WINDOWS[k]:
                acc = acc + pltpu.roll(acc, acc.shape[0] - step, 0)
                step *= 2
            dz_ref[:, cols] = (acc[:ts] - dd).astype(BF16)

    return _run(
        body, name=name, grid=(nt,),
        in_specs=[pl.BlockSpec((ts, d), lambda i: (rev(i), 0)), pl.BlockSpec((ts, d), lambda i: (rev(i), 5)),
                  _halo_spec(ts, nt, POOL_HALO, d, 5), _const((4, POOL_GROUP, POOL_GROUP)), _const((1, d)), HBM],
        out_specs=[pl.BlockSpec((ts, d), lambda i: (rev(i), 5)), _const((8, d)), _const((4, POOL_GROUP, POOL_GROUP))],
        out_shape=[jax.ShapeDtypeStruct((s, D_Z), BF16), jax.ShapeDtypeStruct((8, d), F32),
                   jax.ShapeDtypeStruct((4, POOL_GROUP, POOL_GROUP), F32)],
        scratch_shapes=[pltpu.VMEM((POOL_HALO, d), F32)],
        args=(dyc, z, z, w_pool, scale, dz), aliases={5: 0})


def _in_proj_bwd(dz, w4, dxa, x, sc, *, ts, name, comm=None):
    s, d = x.shape
    ts = _tile(ts, s)
    wd = w4.shape[2]

    def body(dz_ref, w_ref, dxa_ref, x_ref, sc_ref, dx_ref, red_ref, db_ref):
        @pl.when(pl.program_id(0) == 0)
        def _():
            red_ref[...] = jnp.zeros_like(red_ref)
            db_ref[...] = jnp.zeros_like(db_ref)
        dh = jnp.zeros((ts, d), F32)
        for j in range(N_CHIPS):
            dzj = dz_ref[:, j * wd:(j + 1) * wd]
            db_ref[0:1, j * wd:(j + 1) * wd] += _colsum(dzj.astype(F32))
            dh = dh + _dot_nt(dzj, w_ref[j])
        dx_ref[...] = dxa_ref[...] + dh * (1.0 + sc_ref[...])
        red_ref[0:1, :] += _colsum(dh * x_ref[...])
        red_ref[1:2, :] += _colsum(dh)

    row = pl.BlockSpec((ts, d), lambda i: (i, 0))
    return _run(
        body, name=name, grid=(s // ts,),
        in_specs=[pl.BlockSpec((ts, D_Z), lambda i: (i, 0)), _resident((N_CHIPS, d, wd)), row, row, _const((1, d))],
        out_specs=[row, _const((8, d)), _const((8, D_Z))],
        out_shape=[jax.ShapeDtypeStruct((s, d), F32), jax.ShapeDtypeStruct((8, d), F32),
                   jax.ShapeDtypeStruct((8, D_Z), F32)],
        args=(dz, w4, dxa, x, sc), comm=comm)


def _ada_fwd(c_all, w_ada, b_ada, *, name):
    nl, d, n = w_ada.shape
    tn = n // 2

    def body(c_ref, w_ref, b_ref, o_ref):
        cv = c_ref[...]
        ca = (cv * _sigmoid(cv)).astype(BF16)
        o_ref[0] = _dot(ca, w_ref[0].astype(BF16)) + b_ref[0]

    return _run(
        body, name=name, grid=(nl, n // tn),
        in_specs=[_const((N_DEV, d)), pl.BlockSpec((1, d, tn), lambda l, j: (l, 0, j)),
                  pl.BlockSpec((1, 1, tn), lambda l, j: (l, 0, j))],
        out_specs=[pl.BlockSpec((1, N_DEV, tn), lambda l, j: (l, 0, j))],
        out_shape=[jax.ShapeDtypeStruct((nl, N_DEV, n), F32)], args=(c_all, w_ada, b_ada))[0]


def _ada_bwd(c_all, dada, *, name):
    nl, nb, n = dada.shape
    d = c_all.shape[1]
    tn = n // 2

    def body(c_ref, g_ref, o_ref):
        cv = c_ref[...]
        ca = (cv * _sigmoid(cv)).astype(BF16)
        o_ref[0] = _dot_tn(ca, g_ref[0].astype(BF16))

    return _run(
        body, name=name, grid=(nl, n // tn),
        in_specs=[_const((nb, d)), pl.BlockSpec((1, nb, tn), lambda l, j: (l, 0, j))],
        out_specs=[pl.BlockSpec((1, d, tn), lambda l, j: (l, 0, j))],
        out_shape=[jax.ShapeDtypeStruct((nl, d, n), F32)], args=(c_all, dada))[0]


def _sum4_into_half(own, recv, core, *, name):
    r, c = own.shape
    tr = _row_tile(r, c, 2)

    def body(core_ref, own_ref, recv_ref, o_ref):
        acc = own_ref[...]
        for k in range(N_CHIPS - 1):
            acc = acc + recv_ref[k].astype(F32)
        o_ref[0] = acc

    spec = pltpu.PrefetchScalarGridSpec(
        num_scalar_prefetch=1, grid=(r // tr,),
        in_specs=[pl.BlockSpec((tr, c), lambda i, core_ref: (i, 0)),
                  pl.BlockSpec((N_CHIPS - 1, tr, c), lambda i, core_ref: (0, i, 0))],
        out_specs=pl.BlockSpec((1, tr, c), lambda i, core_ref: (core_ref[0], i, 0)))
    return pl.pallas_call(
        body, name=name, grid_spec=spec, out_shape=jax.ShapeDtypeStruct((2, r, c), F32),
        compiler_params=pltpu.CompilerParams(dimension_semantics=("arbitrary",), vmem_limit_bytes=VMEM_LIMIT),
    )(core, own, recv)


def _cast_into_slots(shards, layer, chip, *, name):
    quarters = 4

    def body(chip_ref, *refs):
        ins, outs = refs[:len(shards)], refs[len(shards):]
        for i_ref, o_ref in zip(ins, outs):
            o_ref[0, 0] = i_ref[0].astype(BF16)

    in_specs, out_specs, out_shape = [], [], []
    for sh in shards:
        _, r, c = sh.shape
        in_specs.append(pl.BlockSpec((1, r // quarters, c), lambda t, chip_ref: (layer, t, 0)))
        out_specs.append(pl.BlockSpec((1, 1, r // quarters, c), lambda t, chip_ref: (chip_ref[0], t // 2, t % 2, 0)))
        out_shape.append(jax.ShapeDtypeStruct((N_CHIPS, 2, r // 2, c), BF16))
    spec = pltpu.PrefetchScalarGridSpec(num_scalar_prefetch=1, grid=(quarters,), in_specs=in_specs, out_specs=out_specs)
    return pl.pallas_call(
        body, name=name, grid_spec=spec, out_shape=out_shape,
        compiler_params=pltpu.CompilerParams(dimension_semantics=("arbitrary",), vmem_limit_bytes=VMEM_LIMIT),
    )(chip, *shards)


def _sum_halves(g_f32, theirs, place, *, name, comm=None):
    _, _, rh, c = g_f32.shape
    tr = _row_tile(rh, c, 2)

    def body(place_ref, g_ref, t_ref, hb_ref, own_ref):
        h = g_ref[0, 0] + t_ref[0].astype(F32)
        hb_ref[0] = h.astype(BF16)

        @pl.when(pl.program_id(1) == place_ref[1])
        def _():
            own_ref[...] = h

    return _run(
        body, name=name, grid=(rh // tr, N_CHIPS), prefetch=place,
        in_specs=[pl.BlockSpec((1, 1, tr, c), lambda i, j, place_ref: (j, place_ref[0], i, 0)),
                  pl.BlockSpec((1, tr, c), lambda i, j, place_ref: (j, i, 0))],
        out_specs=[pl.BlockSpec((1, tr, c), lambda i, j, place_ref: (j, i, 0)),
                   pl.BlockSpec((tr, c), lambda i, j, place_ref: (i, 0))],
        out_shape=[jax.ShapeDtypeStruct((N_CHIPS, rh, c), BF16), jax.ShapeDtypeStruct((rh, c), F32)],
        args=(g_f32, theirs), comm=comm)


def _row_tile(r, c, mib):
    limit = max(8, (mib << 20) // (4 * c))
    if r <= limit:
        return r
    best = 8
    for t in range(8, limit + 1, 8):
        if r % t == 0:
            best = t
    return best


def _adam_math(w, g, m, v):
    mn = ADAM_B1 * m + (1.0 - ADAM_B1) * g
    vn = ADAM_B2 * v + (1.0 - ADAM_B2) * (g * g)
    m_hat = mn / (1.0 - ADAM_B1 ** ADAM_STEP)
    v_hat = vn / (1.0 - ADAM_B2 ** ADAM_STEP)
    return -ADAM_LR * (m_hat / (jnp.sqrt(v_hat) + ADAM_EPS) + ADAM_WD * w), mn, vn


def _adamw(w, g, m, v, *, name):
    r, c = w.shape
    tr = _row_tile(r, c, 2)

    def body(w_ref, g_ref, m_ref, v_ref, d_ref, mo_ref, vo_ref):
        d_ref[...], mo_ref[...], vo_ref[...] = _adam_math(w_ref[...], g_ref[...], m_ref[...], v_ref[...])

    blk = pl.BlockSpec((tr, c), lambda i: (i, 0))
    return _run(body, name=name, grid=(r // tr,), in_specs=[blk] * 4, out_specs=[blk] * 3,
                out_shape=[jax.ShapeDtypeStruct((r, c), F32)] * 3, args=(w, g, m, v))


def _adamw_sharded(w, m, v, grads, *, name, comm=None):
    nl, r, c = w.shape
    tr = _row_tile(r, c, 1)
    nt = r // tr

    def body(w_ref, m_ref, v_ref, g0_ref, g1_ref, g_ref, d_ref, mo_ref, vo_ref):
        g = jnp.where(pl.program_id(0) == 0, g0_ref[...], g1_ref[...])
        g_ref[0] = g
        d_ref[0], mo_ref[0], vo_ref[0] = _adam_math(w_ref[0], g, m_ref[0], v_ref[0])

    blk = pl.BlockSpec((1, tr, c), lambda l, i: (l, i, 0))
    part0 = pl.BlockSpec((tr, c), lambda l, i: (jnp.where(l == 0, i, nt - 1), 0))
    part1 = pl.BlockSpec((tr, c), lambda l, i: (jnp.where(l == 1, i, 0), 0))
    return _run(body, name=name, grid=(nl, nt), in_specs=[blk] * 3 + [part0, part1],
                out_specs=[blk] * 4, out_shape=[jax.ShapeDtypeStruct((nl, r, c), F32)] * 4,
                args=(w, m, v, grads[0], grads[1]), comm=comm)


_BIG = ("w_in", "w_a_out", "w_b_out", "w_pool", "w_o", "w_up", "w_down")
_COL_SHARDED = ("w_in", "w_up")
_SMALL_SHARDED = ("conv_a", "conv_ffn")
_SMALL_REPL = ("b_in", "ln_v_g", "ln_v_b", "w_spatial", "b_spatial", "pool_scale", "ln1_g", "ln1_b", "b_up",
               "conv_ffn_b", "ln2_g", "ln2_b")
_WEIGHTS = ("w_ada", "b_ada", "w_in", "b_in", "conv_a", "w_a_out", "ln_v_g", "ln_v_b", "w_spatial", "b_spatial",
            "w_b_out", "w_pool", "pool_scale", "w_o", "ln1_g", "ln1_b", "w_up", "b_up", "conv_ffn", "conv_ffn_b",
            "w_down", "ln2_g", "ln2_b")


def _shard3(a):
    return a.reshape(a.shape[0], -1, a.shape[-1])


def _use_gathered(name, g):
    g = g.reshape(N_CHIPS, -1, g.shape[-1])
    if name in _COL_SHARDED:
        return g
    if name == "w_pool":
        return g.reshape(N_CHIPS, 4, POOL_GROUP // N_CHIPS, POOL_GROUP).transpose(1, 0, 2, 3).reshape(
            4, POOL_GROUP, POOL_GROUP)
    return g.reshape(-1, g.shape[-1])


def _grad_by_chip(name, g):
    if name in _COL_SHARDED:
        return g
    if name == "w_pool":
        return g.reshape(4, N_CHIPS, POOL_GROUP // N_CHIPS, POOL_GROUP).transpose(1, 0, 2, 3).reshape(
            N_CHIPS, POOL_GROUP, POOL_GROUP)
    return g.reshape(N_CHIPS, -1, g.shape[-1])


def _pack_small(arrs):
    parts = []
    for a in arrs:
        flat = a.reshape(-1).astype(F32)
        pad = (-flat.shape[0]) % 128
        parts.append(jnp.pad(flat, (0, pad)) if pad else flat)
    flat = jnp.concatenate(parts)
    pad = (-flat.shape[0]) % 2048
    if pad:
        flat = jnp.pad(flat, (0, pad))
    return flat.reshape(-1, 128)


def _unpack_small(buf, shapes):
    lead = buf.shape[:-2]
    flat = buf.reshape(lead + (-1,))
    out, off = [], 0
    for shp in shapes:
        n = math.prod(shp)
        out.append(flat[..., off:off + n].reshape(lead + tuple(shp)))
        off += n + ((-n) % 128)
    return out


def _as2d(a):
    return a.reshape(-1, a.shape[-1])


_LATE = ("w_a_out", "w_b_out", "w_pool", "w_o")


class _Traffic:
    def __init__(self, slots, plan, core, chip):
        self.slots = slots
        self.plan = plan
        self.core = core
        self.chip = chip
        self.place = jnp.concatenate([core, chip])
        self.gathered = {}
        self.ready = {}
        self.summed = {}
        self.half = {}
        self.final = {}

    def weight(self, layer, name):
        return self.gathered[(layer, name)]

    def add_grad(self, layer, name, g_f32, g_bf16):
        def halves(g):
            g = _grad_by_chip(name, g)
            return g.reshape(N_CHIPS, 2, g.shape[1] // 2, g.shape[2])
        self.ready[(layer, name)] = (halves(g_f32), halves(g_bf16))

    def _comm(self, job):
        if job[0] == "gather":
            return _gather_comm([self.slots[(job[1], k)] for k in job[2]], *job[3:])
        if job[0] == "presum":
            return _presum_comm([self.ready[k][1] for k in job[1]])
        if job[0] == "scatter":
            return _scatter_comm([self.summed[k][0] for k in job[1]])
        return _join_comm([self.half[k] for k in job[1]])

    def _done(self, job, res):
        if job[0] == "gather":
            for k, r in zip(job[2], res):
                self.slots[(job[1], k)] = r
                self.gathered[(job[1], k)] = _use_gathered(k, r)
        elif job[0] == "presum":
            for k, r in zip(job[1], res):
                nm = f"presum_l{k[0]}_{k[1]}"
                self.summed[k] = self.run(nm, lambda cm: _sum_halves(self.ready.pop(k)[0], r, self.place, name=nm,
                                                                     comm=cm))
        elif job[0] == "scatter":
            for k, r in zip(job[1], res):
                self.half[k] = _sum4_into_half(self.summed.pop(k)[1], r, self.core, name=f"sum_l{k[0]}_{k[1]}")
        else:
            for k, r in zip(job[1], res):
                self.final[k] = r.reshape(-1, r.shape[-1])

    def run(self, name, fn):
        jobs = self.plan.get(name)
        if not jobs:
            return fn(None)
        comms = [self._comm(j) for j in jobs]
        outs, res = fn(_merge(comms))
        for job, r in zip(jobs, _split(comms, res)):
            self._done(job, r)
        return outs

    def alone(self, name):
        jobs = self.plan[name]
        comms = [self._comm(j) for j in jobs]
        for job, r in zip(jobs, _split(comms, _comm_call(_merge(comms), name=name))):
            self._done(job, r)


def _layer_fwd(x, ada, p, l, tr):
    sh1, sc1, gt1, sh2, sc2, gt2 = ada
    n = f"l{l}"
    if f"{n}_in_proj_own" in tr.plan:
        z, ht = tr.run(f"{n}_in_proj_own", lambda cm: _in_proj_own(
            x, sc1, sh1, p["w_in_shards"], l, p["b_in"], tr.chip, ts=1024, name=f"{n}_in_proj_own", comm=cm))
        z, = tr.run(f"{n}_in_proj", lambda cm: _in_proj_rest(
            x, sc1, sh1, tr.weight(l, "w_in"), p["b_in"], tr.chip, z, ts=1024, name=f"{n}_in_proj", comm=cm))
    else:
        z, ht = tr.run(f"{n}_in_proj", lambda cm: _mod_matmul(
            x, sc1, sh1, tr.weight(l, "w_in"), p["b_in"], ts=1024, tn=2304, name=f"{n}_in_proj", comm=cm))
    a, ya = tr.run(f"{n}_mix_a", lambda cm: _mix_a_fwd(z, p["conv_a"], tr.weight(l, "w_a_out"), ts=256,
                                                      name=f"{n}_mix_a", comm=cm))
    sg, yb = tr.run(f"{n}_mix_b", lambda cm: _mix_b_fwd(
        z, p["ln_v_g"], p["ln_v_b"], p["w_spatial"], p["b_spatial_t"], tr.weight(l, "w_b_out"), ts=256,
        name=f"{n}_mix_b", comm=cm))
    dpool, yc = tr.run(f"{n}_mix_c", lambda cm: _mix_c_fwd(z, tr.weight(l, "w_pool"), p["pool_scale"], ts=256,
                                                          name=f"{n}_mix_c", comm=cm))
    merged, o, x1 = tr.run(f"{n}_mix_o", lambda cm: _mix_o_fwd(
        x, z, ya, yb, yc, tr.weight(l, "w_o"), gt1, p["ln1_g"], p["ln1_b"], ts=256, name=f"{n}_mix_o", comm=cm))
    up, h2t = tr.run(f"{n}_up_proj", lambda cm: _mod_matmul(
        x1, sc2, sh2, tr.weight(l, "w_up"), p["b_up"], ts=1024, tn=1408, name=f"{n}_up_proj", comm=cm))
    ft, dn, x2 = tr.run(f"{n}_ffn", lambda cm: _ffn_fwd(
        up, x1, p["conv_ffn"], p["conv_ffn_b"], tr.weight(l, "w_down"), gt2, p["ln2_g"], p["ln2_b"], ts=256,
        name=f"{n}_ffn", comm=cm))
    saved = dict(x=x, z=z, ht=ht, a=a, ya=ya, sg=sg, yb=yb, dpool=dpool, yc=yc, merged=merged, o=o, x1=x1, h2t=h2t,
                 up=up, ft=ft, dn=dn)
    return x2, saved


def _layer_bwd(dx2, ada, p, sv, l, tr):
    sh1, sc1, gt1, sh2, sc2, gt2 = ada
    n = f"l{l}"
    ddn, dup, dx1, red_d, red_f, dbup = tr.run(f"{n}_ffn_bwd", lambda cm: _ffn_bwd(
        dx2, sv["x1"], sv["dn"], sv["up"], p["conv_ffn"], p["conv_ffn_b"], tr.weight(l, "w_down"),
        tr.weight(l, "w_up"), gt2, p["ln2_g"], sc2, ts=256, name=f"{n}_ffn_bwd", comm=cm))
    g = {}
    tr.add_grad(l, "w_down", *_grad_matmul_t(sv["ft"], ddn, tk=D_FF // N_CHIPS, tn=D_MODEL, name=f"{n}_dw_down"))
    tr.add_grad(l, "w_up", *tr.run(f"{n}_dw_up", lambda cm: _grad_matmul_t(
        sv["h2t"], dup, tk=D_MODEL, tn=FF_CHUNK, name=f"{n}_dw_up", by_chip=True, comm=cm)))
    g["ln2_g"], g["ln2_b"] = red_d[0], red_d[1]
    g["conv_ffn"], g["conv_ffn_b"], g["b_up"] = red_f[0:3], red_f[3], dbup[0]

    dxa, dz, dya, dyb, dyc, red_o, dwo, dwo_b = tr.run(f"{n}_mix_o_bwd", lambda cm: _mix_o_bwd(
        dx1, sv["x"], sv["o"], sv["z"], sv["ya"], sv["yb"], sv["yc"], sv["merged"], tr.weight(l, "w_o"), gt1,
        p["ln1_g"], ts=256, name=f"{n}_mix_o_bwd", comm=cm))
    tr.add_grad(l, "w_o", dwo, dwo_b)
    g["ln1_g"], g["ln1_b"] = red_o[0], red_o[1]

    dz, red_a, dwa, dwa_b = tr.run(f"{n}_mix_a_bwd", lambda cm: _mix_a_bwd(
        dya, sv["a"], sv["z"], dz, p["conv_a"], tr.weight(l, "w_a_out"), ts=256, name=f"{n}_mix_a_bwd", comm=cm))
    tr.add_grad(l, "w_a_out", dwa, dwa_b)
    g["conv_a"] = red_a[0:3]

    dz, red_b, dws, dbst, dwb, dwb_b = tr.run(f"{n}_mix_b_bwd", lambda cm: _mix_b_bwd(
        dyb, sv["sg"], sv["z"], dz, p["ln_v_g"], p["ln_v_b"], p["w_spatial"], p["b_spatial_t"],
        tr.weight(l, "w_b_out"), ts=256, name=f"{n}_mix_b_bwd", comm=cm))
    tr.add_grad(l, "w_b_out", dwb, dwb_b)
    g["ln_v_g"], g["ln_v_b"], g["w_spatial"], g["b_spatial"] = red_b[0], red_b[1], dws, dbst.T

    dz, red_c, dwp = _mix_c_bwd(dyc, sv["z"], dz, tr.weight(l, "w_pool"), p["pool_scale"], ts=256,
                                name=f"{n}_mix_c_bwd")
    g["pool_scale"] = red_c[0]
    tr.add_grad(l, "w_pool", dwp, dwp.astype(BF16))

    tr.add_grad(l, "w_in", *tr.run(f"{n}_dw_in", lambda cm: _grad_matmul_t(
        sv["ht"], dz, tk=D_MODEL, tn=1152, name=f"{n}_dw_in", by_chip=True, comm=cm)))
    if f"{n}_presum_tail" in tr.plan:
        tr.alone(f"{n}_presum_tail")
    dx, red_i, dbin = tr.run(f"{n}_in_proj_bwd", lambda cm: _in_proj_bwd(
        dz, tr.weight(l, "w_in"), dxa, sv["x"], sc1, ts=256, name=f"{n}_in_proj_bwd", comm=cm))
    g["b_in"] = dbin[0]
    dada = jnp.stack([red_i[1], red_i[0], red_o[2], red_d[4], red_d[3], red_d[2]])
    return dx, g, dada


def _traffic_plan():
    plan = {
        "l0_in_proj_own": [("gather", 0, ("w_in",) + _LATE)],
        "l0_in_proj": [("gather", 0, ("w_up", "w_down"))],
        "l0_mix_o": [("gather", 1, _LATE)],
        "l0_up_proj": [("gather", 1, ("w_in",), (1, 2))],
        "l0_ffn": [("gather", 1, ("w_in",), (3,))],
        "l1_in_proj": [("gather", 1, ("w_up", "w_down"))],
    }
    for l in reversed(range(DEPTH)):
        late = [(l, k) for k in _LATE]
        plan.update({
            f"l{l}_mix_o_bwd": [("presum", [(l, "w_down"), (l, "w_up")])],
            f"l{l}_mix_b_bwd": [("scatter", [(l, "w_down"), (l, "w_up")])],
            f"l{l}_dw_in": [("presum", late), ("join", [(l, "w_down"), (l, "w_up")])],
        })
    late0, late1 = [(0, k) for k in _LATE], [(1, k) for k in _LATE]
    plan["l1_in_proj_bwd"] = [("presum", [(1, "w_in")]), ("scatter", late1)]
    plan["l0_ffn_bwd"] = [("scatter", [(1, "w_in")]), ("join", late1)]
    plan["l0_dw_up"] = [("join", [(1, "w_in")])]
    plan["l0_presum_tail"] = [("presum", [(0, "w_in")])]
    plan["presum_l0_w_in"] = [("scatter", late0)]
    plan["l0_in_proj_bwd"] = [("scatter", [(0, "w_in")])]
    plan["join_tail"] = [("join", [(0, "w_in")] + late0)]
    return plan


def kernel(x, c, w_ada, b_ada, w_in, b_in, conv_a, w_a_out, ln_v_g, ln_v_b, w_spatial, b_spatial, w_b_out, w_pool, pool_scale, w_o, ln1_g, ln1_b, w_up, b_up, conv_ffn, conv_ffn_b, w_down, ln2_g, ln2_b, loss_target, m_w_ada, m_b_ada, m_w_in, m_b_in, m_conv_a, m_w_a_out, m_ln_v_g, m_ln_v_b, m_w_spatial, m_b_spatial, m_w_b_out, m_w_pool, m_pool_scale, m_w_o, m_ln1_g, m_ln1_b, m_w_up, m_b_up, m_conv_ffn, m_conv_ffn_b, m_w_down, m_ln2_g, m_ln2_b, v_w_ada, v_b_ada, v_w_in, v_b_in, v_conv_a, v_w_a_out, v_ln_v_g, v_ln_v_b, v_w_spatial, v_b_spatial, v_w_b_out, v_w_pool, v_pool_scale, v_w_o, v_ln1_g, v_ln1_b, v_w_up, v_b_up, v_conv_ffn, v_conv_ffn_b, v_w_down, v_ln2_g, v_ln2_b):
    args = locals()
    w = {k: args[k] for k in _WEIGHTS}
    m = {k: args["m_" + k] for k in _WEIGHTS}
    v = {k: args["v_" + k] for k in _WEIGHTS}
    d = D_MODEL
    mx, my, mc = _my_coords()
    chip = 2 * mx + my
    me = 4 * mx + 2 * my + mc

    small_shapes = [c.shape, conv_a.shape, conv_ffn.shape]
    small_all = _all_gather8(_pack_small([c, conv_a, conv_ffn]), name="gather_small")
    c_all, conv_a_st, conv_ffn_st = _unpack_small(small_all, small_shapes)
    c_all = c_all.reshape(N_DEV, d)
    conv_full = {"conv_a": jnp.concatenate([conv_a_st[2 * j] for j in range(N_CHIPS)], axis=-1),
                 "conv_ffn": jnp.concatenate([conv_ffn_st[2 * j] for j in range(N_CHIPS)], axis=-1)}

    chip_idx = jnp.reshape(chip, (1,)).astype(jnp.int32)
    slots = {}
    for l in range(DEPTH):
        bufs = _cast_into_slots([_shard3(w[k]) for k in _BIG], l, chip_idx, name=f"cast_l{l}")
        slots.update({(l, k): b for k, b in zip(_BIG, bufs)})
    tr = _Traffic(slots, _traffic_plan(), jnp.reshape(mc, (1,)).astype(jnp.int32), chip_idx)

    n_ada = w_ada.shape[2]
    b_ada_mine = lax.dynamic_slice_in_dim(b_ada, chip * n_ada, n_ada, axis=1)
    ada_part = _ada_fwd(c_all, w_ada, b_ada_mine.reshape(DEPTH, 1, n_ada), name="ada_fwd")
    ada_all = _all_gather8(_pack_small([ada_part]), name="gather_ada")
    ada_st = _unpack_small(ada_all, [ada_part.shape])[0][0::2]
    ada_rows = jnp.concatenate([ada_st[j] for j in range(N_CHIPS)], axis=-1)
    ada_mine = lax.dynamic_index_in_dim(ada_rows, me, axis=1, keepdims=False)

    def layer_params(l):
        p = {k: conv_full[k][l] for k in _SMALL_SHARDED}
        p["w_in_shards"] = w_in
        for k in ("b_in", "ln_v_g", "ln_v_b", "pool_scale", "ln1_g", "ln1_b", "b_up", "conv_ffn_b", "ln2_g", "ln2_b"):
            p[k] = w[k][l].reshape(1, -1)
        p["w_spatial"] = w_spatial[l]
        p["b_spatial_t"] = b_spatial[l].T
        return p

    xs = x[0]
    saved, adas, params = [], [], []
    for l in range(DEPTH):
        ada = [ada_mine[l, k * d:(k + 1) * d].reshape(1, d) for k in range(6)]
        p = layer_params(l)
        xs, sv = _layer_fwd(xs, ada, p, l, tr)
        saved.append(sv), adas.append(ada), params.append(p)
    dx, loss_blk = _loss_fwd(xs, loss_target[0], ts=512, name="loss")

    grads, dadas = [None] * DEPTH, [None] * DEPTH
    for l in reversed(range(DEPTH)):
        dx, grads[l], dadas[l] = _layer_bwd(dx, adas[l], params[l], saved[l], l, tr)
    tr.alone("join_tail")
    dada = jnp.stack(dadas).reshape(DEPTH, 6 * d)

    small_names = _SMALL_REPL + _SMALL_SHARDED
    small_g = [jnp.stack([grads[l][k] for l in range(DEPTH)]) for k in small_names]
    gsum = dict(zip(small_names, _unpack_small(_all_reduce_small(_pack_small(small_g), name="reduce_small"),
                                               [a.shape for a in small_g])))
    tail_g = [dada, loss_blk[0:1, 0:1]]
    tail_all, tail_sum = _all_gather8(_pack_small(tail_g), name="gather_dada", with_sum=True)
    gsum["b_ada"], loss_sum = _unpack_small(tail_sum, [a.shape for a in tail_g])
    loss = loss_sum[0, 0]
    dada_all = _unpack_small(tail_all, [a.shape for a in tail_g])[0]
    for k in _SMALL_SHARDED:
        wd = gsum[k].shape[-1] // N_CHIPS
        gsum[k] = lax.dynamic_slice_in_dim(gsum[k], chip * wd, wd, axis=gsum[k].ndim - 1)

    dada_cols = lax.dynamic_slice_in_dim(dada_all, chip * n_ada, n_ada, axis=2)
    dada_cols = jnp.pad(jnp.swapaxes(dada_cols, 0, 1), ((0, 0), (0, N_DEV), (0, 0)))
    gsum["w_ada"] = _ada_bwd(jnp.pad(c_all, ((0, N_DEV), (0, 0))), dada_cols, name="ada_bwd")

    out_g, out_d, out_m, out_v = {}, {}, {}, {}
    for k in _WEIGHTS:
        shp = w[k].shape
        if k in _BIG:
            res = tr.run(f"adamw_{k}", lambda cm: _adamw_sharded(
                _shard3(w[k]), _shard3(m[k]), _shard3(v[k]), [tr.final[(l, k)] for l in range(DEPTH)],
                name=f"adamw_{k}", comm=cm))
        else:
            gk = gsum[k].reshape(shp)
            res = [gk] + list(_adamw(_as2d(w[k]), _as2d(gk), _as2d(m[k]), _as2d(v[k]), name=f"adamw_{k}"))
        out_g[k], out_d[k], out_m[k], out_v[k] = [r.reshape(shp) for r in res]

    return (loss, dx[None], *[out_g[k] for k in _WEIGHTS], *[out_d[k] for k in _WEIGHTS],
            *[out_m[k] for k in _WEIGHTS], *[out_v[k] for k in _WEIGHTS])
```

```python
import math
from typing import Callable, NamedTuple

import jax
import jax.numpy as jnp
from jax import lax
from jax.experimental import pallas as pl
from jax.experimental.pallas import tpu as pltpu

F32 = jnp.float32
BF16 = jnp.bfloat16

D_MODEL = 1024
D_Z = 9216
D_FF = 2816
N_GROUPS = 8
GBLK = 128
CHUNK = 64
POOL_WINDOWS = (2, 4, 8, 16)
POOL_GROUP = 256
POOL_HALO = 16
CONV_HALO = 8
HALO_ROWS = 16
DEPTH = 2
ALPHA = (2 * DEPTH) ** 0.25
LN_EPS = 1e-5
ADAM_LR, ADAM_B1, ADAM_B2, ADAM_EPS, ADAM_WD, ADAM_STEP = 0.001, 0.9, 0.999, 1e-08, 0.01, 10
N_CHIPS = 4
N_DEV = 8
FF_CHUNK = 1408
MESH = pl.DeviceIdType.MESH
VMEM_LIMIT = 56 * 1024 * 1024
HBM = pl.BlockSpec(memory_space=pl.ANY)


def _dot(a, b):
    return jnp.dot(a, b, preferred_element_type=F32)


def _dot_nt(a, b):
    return lax.dot_general(a, b, (((1,), (1,)), ((), ())), preferred_element_type=F32)


def _dot_tn(a, b):
    return lax.dot_general(a, b, (((0,), (0,)), ((), ())), preferred_element_type=F32)


_GELU_C = math.sqrt(2.0 / math.pi)


def _gelu_and_grad(x):
    x2 = x * x
    t = jnp.tanh(_GELU_C * (x + 0.044715 * x * x2))
    g = 0.5 * x * (1.0 + t)
    dg = 0.5 * (1.0 + t) + 0.5 * x * (1.0 - t * t) * (_GELU_C * (1.0 + 3 * 0.044715 * x2))
    return g, dg


def _gelu(x):
    return 0.5 * x * (1.0 + jnp.tanh(_GELU_C * (x + 0.044715 * x * x * x)))


def _sigmoid(x):
    return 1.0 / (1.0 + jnp.exp(-x))


def _ln_fwd(r):
    mu = jnp.mean(r, axis=-1, keepdims=True)
    xc = r - mu
    var = jnp.mean(xc * xc, axis=-1, keepdims=True)
    rstd = lax.rsqrt(var + LN_EPS)
    return xc * rstd, rstd


def _ln_bwd(dy, g, xhat, rstd):
    dxh = dy * g
    m1 = jnp.mean(dxh, axis=-1, keepdims=True)
    m2 = jnp.mean(dxh * xhat, axis=-1, keepdims=True)
    return rstd * (dxh - m1 - xhat * m2)


def _rows_before(ext, k, halo):
    return pltpu.roll(ext, k, 0)[halo:]


def _rows_after(ext, k, n):
    return pltpu.roll(ext, ext.shape[0] - k, 0)[:n]


def _colsum(v):
    return jnp.sum(v, axis=0, keepdims=True)


def _spatial_mask():
    i = lax.broadcasted_iota(jnp.int32, (GBLK, GBLK), 0)
    j = lax.broadcasted_iota(jnp.int32, (GBLK, GBLK), 1)
    return (j // CHUNK) <= (i // CHUNK)


def _const(shape):
    n = len(shape)
    return pl.BlockSpec(shape, lambda *_: (0,) * n)


def _resident(shape):
    n = len(shape)
    return pl.BlockSpec(shape, lambda *_: (0,) * n, pipeline_mode=pl.Buffered(1))


def _tile(ts, s):
    return min(ts, s)


class _Comm(NamedTuple):
    srcs: tuple
    dsts: tuple
    n_remote: int
    n_local: int
    build: Callable
    alias: tuple = ()


def _my_coords():
    return lax.axis_index("x"), lax.axis_index("y"), lax.axis_index("c")


def _chip_peer(k):
    mx, my, mc = _my_coords()
    return (mx ^ ((k >> 1) & 1), my ^ (k & 1), mc)


def _sem_scratch(comm):
    return [pltpu.SemaphoreType.DMA((max(comm.n_remote, 1),)), pltpu.SemaphoreType.DMA((max(comm.n_remote, 1),)),
            pltpu.SemaphoreType.DMA((max(comm.n_local, 1),))]


def _run(body, *, name, grid, in_specs, out_specs, out_shape, args, scratch_shapes=(), comm=None, aliases=None,
         prefetch=None):
    sem = ("arbitrary",) * len(grid)
    cparams = pltpu.CompilerParams(dimension_semantics=sem, vmem_limit_bytes=VMEM_LIMIT)
    aliases = dict(aliases or {})
    n_pre = 0 if prefetch is None else 1

    def call(fn, in_specs, out_specs, out_shape, scratch_shapes, args):
        if prefetch is None:
            return pl.pallas_call(fn, name=name, grid=grid, in_specs=in_specs, out_specs=out_specs, out_shape=out_shape,
                                  scratch_shapes=scratch_shapes, compiler_params=cparams,
                                  input_output_aliases=aliases)(*args)
        spec = pltpu.PrefetchScalarGridSpec(num_scalar_prefetch=1, grid=grid, in_specs=in_specs, out_specs=out_specs,
                                            scratch_shapes=scratch_shapes)
        return pl.pallas_call(fn, name=name, grid_spec=spec, out_shape=out_shape, compiler_params=cparams,
                              input_output_aliases={k + 1: v for k, v in aliases.items()})(prefetch, *args)

    if comm is None:
        return call(body, list(in_specs), list(out_specs), list(out_shape), list(scratch_shapes), args)
    n_in, n_cs, n_out, n_cd, n_scr = len(in_specs), len(comm.srcs), len(out_specs), len(comm.dsts), len(scratch_shapes)
    aliases.update({n_in + si: n_out + di for si, di in comm.alias})
    total = math.prod(grid)
    mid_step = min(total - 1, int(total * 0.9))

    def wrapped(*refs):
        pre, refs = refs[:n_pre], refs[n_pre:]
        ins, refs = refs[:n_in], refs[n_in:]
        csrc, refs = refs[:n_cs], refs[n_cs:]
        outs, refs = refs[:n_out], refs[n_out:]
        cdst, refs = refs[:n_cd], refs[n_cd:]
        scr, sems = refs[:n_scr], refs[n_scr:]
        step = pl.program_id(0)
        for ax in range(1, len(grid)):
            step = step * grid[ax] + pl.program_id(ax)
        first, mid, last = comm.build(csrc, cdst, *sems, 0, 0)
        pl.when(step == 0)(first)
        if mid is not None:
            pl.when(step == mid_step)(mid)
        body(*pre, *ins, *outs, *scr)
        pl.when(step == total - 1)(last)

    res = call(wrapped, list(in_specs) + [HBM] * n_cs, list(out_specs) + [HBM] * n_cd,
               list(out_shape) + list(comm.dsts), list(scratch_shapes) + _sem_scratch(comm), (*args, *comm.srcs))
    return res[:n_out], res[n_out:]


def _comm_call(comm, *, name):
    def body(*refs):
        n_cs, n_cd = len(comm.srcs), len(comm.dsts)
        first, mid, last = comm.build(refs[:n_cs], refs[n_cs:n_cs + n_cd], *refs[n_cs + n_cd:], 0, 0)
        first()
        if mid is not None:
            mid()
        last()

    return pl.pallas_call(body, name=name, in_specs=[HBM] * len(comm.srcs), out_specs=[HBM] * len(comm.dsts),
                          out_shape=list(comm.dsts), scratch_shapes=_sem_scratch(comm),
                          input_output_aliases=dict(comm.alias))(*comm.srcs)


def _gather_comm(bufs, peers=(1, 2, 3)):
    dsts = tuple(jax.ShapeDtypeStruct(b.shape, b.dtype) for b in bufs)
    nw = len(bufs)

    def build(srcs, outs, send_sems, recv_sems, local_sems, r0, l0):
        mx, my, mc = _my_coords()
        me = 2 * mx + my
        sibling = (mx, my, 1 - mc)

        def rdma(src, dst, idx, peer):
            return pltpu.make_async_remote_copy(src_ref=src, dst_ref=dst, send_sem=send_sems.at[r0 + idx],
                                                recv_sem=recv_sems.at[r0 + idx], device_id=peer, device_id_type=MESH)

        def ici(w, k, slot):
            return rdma(outs[w].at[me, mc], outs[w].at[slot, mc], 6 * w + k - 1, _chip_peer(k))

        def fwd(w, k, half):
            return rdma(outs[w].at[me ^ k, mc], outs[w].at[me ^ k, half], 6 * w + 2 + k, sibling)

        def first():
            for w in range(nw):
                for k in peers:
                    ici(w, k, me).start()

        def mid():
            for w in range(nw):
                for k in peers:
                    ici(w, k, me ^ k).wait_recv()
                    fwd(w, k, mc).start()

        def last():
            for w in range(nw):
                for k in peers:
                    fwd(w, k, 1 - mc).wait_recv()
                    ici(w, k, me).wait_send()
                    fwd(w, k, mc).wait_send()

        return first, mid, last

    return _Comm(tuple(bufs), dsts, 6 * nw, 0, build, tuple((w, w) for w in range(nw)))


def _symmetric(make_remote, make_local, make_incoming=None):
    def first():
        for cp in make_remote() + make_local():
            cp.start()

    def last():
        for cp in (make_incoming or make_remote)():
            cp.wait_recv()
        for cp in make_remote():
            cp.wait_send()
        for cp in make_local():
            cp.wait()

    return first, None, last


def _presum_comm(g_bf16):
    nw = len(g_bf16)
    dsts = tuple(jax.ShapeDtypeStruct((N_CHIPS,) + g.shape[2:], BF16) for g in g_bf16)

    def build(srcs, outs, send_sems, recv_sems, local_sems, r0, l0):
        mx, my, mc = _my_coords()

        def remote():
            return [pltpu.make_async_remote_copy(
                src_ref=srcs[w].at[j, 1 - mc], dst_ref=outs[w].at[j], send_sem=send_sems.at[r0 + N_CHIPS * w + j],
                recv_sem=recv_sems.at[r0 + N_CHIPS * w + j], device_id=(mx, my, 1 - mc), device_id_type=MESH)
                for w in range(nw) for j in range(N_CHIPS)]

        return _symmetric(remote, lambda: [])

    return _Comm(tuple(g_bf16), dsts, N_CHIPS * nw, 0, build)


def _scatter_comm(h_bf16):
    nw = len(h_bf16)
    dsts = tuple(jax.ShapeDtypeStruct((N_CHIPS - 1,) + h.shape[1:], BF16) for h in h_bf16)

    def build(srcs, outs, send_sems, recv_sems, local_sems, r0, l0):
        mx, my, _ = _my_coords()
        me = 2 * mx + my

        def remote():
            return [pltpu.make_async_remote_copy(
                src_ref=srcs[w].at[me ^ k], dst_ref=outs[w].at[k - 1], send_sem=send_sems.at[r0 + 3 * w + k - 1],
                recv_sem=recv_sems.at[r0 + 3 * w + k - 1], device_id=_chip_peer(k), device_id_type=MESH)
                for w in range(nw) for k in range(1, N_CHIPS)]

        return _symmetric(remote, lambda: [])

    return _Comm(tuple(h_bf16), dsts, 3 * nw, 0, build)


def _join_comm(bufs):
    nw = len(bufs)
    dsts = tuple(jax.ShapeDtypeStruct(b.shape, b.dtype) for b in bufs)

    def build(srcs, outs, send_sems, recv_sems, local_sems, r0, l0):
        mx, my, mc = _my_coords()

        def remote(half=mc):
            return [pltpu.make_async_remote_copy(
                src_ref=outs[w].at[mc], dst_ref=outs[w].at[half], send_sem=send_sems.at[r0 + w],
                recv_sem=recv_sems.at[r0 + w], device_id=(mx, my, 1 - mc), device_id_type=MESH) for w in range(nw)]

        return _symmetric(remote, lambda: [], lambda: remote(1 - mc))

    return _Comm(tuple(bufs), dsts, nw, 0, build, tuple((w, w) for w in range(nw)))


def _merge(comms):
    comms = list(comms)
    if len(comms) == 1:
        return comms[0]

    def build(srcs, outs, send_sems, recv_sems, local_sems, r0, l0):
        phases, s0, d0 = [], 0, 0
        for cm in comms:
            phases.append(cm.build(srcs[s0:s0 + len(cm.srcs)], outs[d0:d0 + len(cm.dsts)], send_sems, recv_sems,
                                   local_sems, r0, l0))
            s0, d0, r0, l0 = s0 + len(cm.srcs), d0 + len(cm.dsts), r0 + cm.n_remote, l0 + cm.n_local

        def run(idx):
            fns = [ph[idx] for ph in phases if ph[idx] is not None]
            if not fns:
                return None

            def go():
                for fn in fns:
                    fn()
            return go

        return run(0), run(1), run(2)

    alias, s0, d0 = [], 0, 0
    for cm in comms:
        alias += [(s0 + si, d0 + di) for si, di in cm.alias]
        s0, d0 = s0 + len(cm.srcs), d0 + len(cm.dsts)
    return _Comm(sum((cm.srcs for cm in comms), ()), sum((cm.dsts for cm in comms), ()),
                 sum(cm.n_remote for cm in comms), sum(cm.n_local for cm in comms), build, tuple(alias))


def _split(comms, res):
    out, d0 = [], 0
    for cm in comms:
        out.append(list(res[d0:d0 + len(cm.dsts)]))
        d0 += len(cm.dsts)
    return out


def _all_reduce_small(x, *, name):
    r, lanes = x.shape
    half = r // 2
    assert half % 8 == 0

    def body(x_ref, out_ref, sib_ref, slots_ref, send_sems, recv_sems):
        mx, my, mc = _my_coords()
        me = 2 * mx + my
        sibling = (mx, my, 1 - mc)
        mine = pl.ds(pl.multiple_of(mc * half, 8), half)
        theirs = pl.ds(pl.multiple_of((1 - mc) * half, 8), half)

        def to_sibling(src, dst, idx):
            return pltpu.make_async_remote_copy(src_ref=src, dst_ref=dst, send_sem=send_sems.at[idx],
                                                recv_sem=recv_sems.at[idx], device_id=sibling, device_id_type=MESH)

        swap = to_sibling(x_ref.at[theirs], sib_ref, 0)
        swap.start()
        swap.wait_recv()
        swap.wait_send()
        slots_ref[me] = x_ref[mine, :] + sib_ref[...]

        def copy(k, slot):
            return pltpu.make_async_remote_copy(
                src_ref=slots_ref.at[me], dst_ref=slots_ref.at[slot], send_sem=send_sems.at[k], recv_sem=recv_sems.at[k],
                device_id=_chip_peer(k), device_id_type=MESH)

        sends = [copy(k, me) for k in range(1, N_CHIPS)]
        for cp in sends:
            cp.start()
        for k in range(1, N_CHIPS):
            copy(k, me ^ k).wait_recv()
        for cp in sends:
            cp.wait_send()
        acc = slots_ref[0]
        for j in range(1, N_CHIPS):
            acc = acc + slots_ref[j]
        out_ref[mine, :] = acc
        join = to_sibling(out_ref.at[mine], out_ref.at[mine], N_CHIPS)
        join.start()
        to_sibling(out_ref.at[mine], out_ref.at[theirs], N_CHIPS).wait_recv()
        join.wait_send()

    vmem = pl.BlockSpec(memory_space=pltpu.VMEM)
    return pl.pallas_call(
        body, name=name, in_specs=[vmem], out_specs=vmem, out_shape=jax.ShapeDtypeStruct((r, lanes), F32),
        scratch_shapes=[pltpu.VMEM((half, lanes), F32), pltpu.VMEM((N_CHIPS, half, lanes), F32),
                        pltpu.SemaphoreType.DMA((N_CHIPS + 1,)), pltpu.SemaphoreType.DMA((N_CHIPS + 1,))],
        compiler_params=pltpu.CompilerParams(vmem_limit_bytes=VMEM_LIMIT),
    )(x)


def _all_gather8(x, *, name, with_sum=False):
    r, lanes = x.shape

    def body(x_ref, out_ref, *rest):
        if with_sum:
            sum_ref, send_sems, recv_sems, local_sem = rest
        else:
            send_sems, recv_sems, local_sem = rest
        mx, my, mc = _my_coords()
        me = 4 * mx + 2 * my + mc

        def peer(k):
            return (mx ^ ((k >> 2) & 1), my ^ ((k >> 1) & 1), mc ^ (k & 1))

        def copy(k, slot):
            return pltpu.make_async_remote_copy(
                src_ref=x_ref, dst_ref=out_ref.at[slot], send_sem=send_sems.at[k - 1], recv_sem=recv_sems.at[k - 1],
                device_id=peer(k), device_id_type=MESH)

        mine = pltpu.make_async_copy(x_ref, out_ref.at[me], local_sem)
        mine.start()
        sends = [copy(k, me) for k in range(1, N_DEV)]
        for cp in sends:
            cp.start()
        for k in range(1, N_DEV):
            copy(k, me ^ k).wait_recv()
        for cp in sends:
            cp.wait_send()
        mine.wait()
        if with_sum:
            acc = out_ref[0]
            for k in range(1, N_DEV):
                acc = acc + out_ref[k]
            sum_ref[...] = acc

    vmem = pl.BlockSpec(memory_space=pltpu.VMEM)
    out_shape = [jax.ShapeDtypeStruct((N_DEV, r, lanes), F32)]
    if with_sum:
        out_shape.append(jax.ShapeDtypeStruct((r, lanes), F32))
    res = pl.pallas_call(
        body, name=name, in_specs=[vmem], out_specs=[vmem] * len(out_shape), out_shape=out_shape,
        scratch_shapes=[pltpu.SemaphoreType.DMA((N_DEV - 1,)), pltpu.SemaphoreType.DMA((N_DEV - 1,)),
                        pltpu.SemaphoreType.DMA],
        compiler_params=pltpu.CompilerParams(vmem_limit_bytes=VMEM_LIMIT),
    )(x)
    return res if with_sum else res[0]


def _mod_matmul(x, sc, sh, w4, b, *, ts, tn, name, comm=None):
    s, d = x.shape
    wd = w4.shape[2]
    n = N_CHIPS * wd
    per = wd // tn
    ts = _tile(ts, s)

    def body(x_ref, sc_ref, sh_ref, w_ref, b_ref, o_ref, ht_ref, h_scr):
        @pl.when(pl.program_id(1) == 0)
        def _():
            h = x_ref[...] * (1.0 + sc_ref[...]) + sh_ref[...]
            h_scr[...] = h.astype(BF16)
            ht_ref[...] = h.T.astype(BF16)
        o_ref[...] = (_dot(h_scr[...], w_ref[0]) + b_ref[...]).astype(BF16)

    return _run(
        body, name=name, grid=(s // ts, n // tn),
        in_specs=[pl.BlockSpec((ts, d), lambda i, j: (i, 0)), _const((1, d)), _const((1, d)),
                  pl.BlockSpec((1, d, tn), lambda i, j: (j // per, 0, j % per)),
                  pl.BlockSpec((1, tn), lambda i, j: (0, j))],
        out_specs=[pl.BlockSpec((ts, tn), lambda i, j: (i, j)), pl.BlockSpec((d, ts), lambda i, j: (0, i))],
        out_shape=[jax.ShapeDtypeStruct((s, n), BF16), jax.ShapeDtypeStruct((d, s), BF16)],
        scratch_shapes=[pltpu.VMEM((ts, d), BF16)],
        args=(x, sc, sh, w4, b), comm=comm)


def _conv3(q, ext, cw):
    return cw[2:3] * q + cw[1:2] * _rows_before(ext, 1, CONV_HALO) + cw[0:1] * _rows_before(ext, 2, CONV_HALO)


def _mix_a_fwd(z, cw, w_out, *, ts, name, comm=None):
    s = z.shape[0]
    d = D_MODEL
    ts = _tile(ts, s)

    def body(zb_ref, zc_ref, zx_ref, cw_ref, w_ref, a_ref, y_ref, carry):
        @pl.when(pl.program_id(0) == 0)
        def _():
            carry[...] = jnp.zeros_like(carry)
        q = zc_ref[...].astype(F32) * zx_ref[...].astype(F32)
        ext = jnp.concatenate([carry[...], q], axis=0)
        a = (zb_ref[...].astype(F32) * _conv3(q, ext, cw_ref[...])).astype(BF16)
        carry[...] = q[ts - CONV_HALO:]
        a_ref[...] = a
        y_ref[...] = _dot(a, w_ref[...])

    zspec = lambda k: pl.BlockSpec((ts, d), lambda i, k=k: (i, k))
    return _run(
        body, name=name, grid=(s // ts,),
        in_specs=[zspec(0), zspec(1), zspec(2), _const((3, d)), _const((d, d))],
        out_specs=[pl.BlockSpec((ts, d), lambda i: (i, 0))] * 2,
        out_shape=[jax.ShapeDtypeStruct((s, d), BF16), jax.ShapeDtypeStruct((s, d), F32)],
        scratch_shapes=[pltpu.VMEM((CONV_HALO, d), F32)],
        args=(z, z, z, cw, w_out), comm=comm)


def _spatial_mix(vn_b, ws_ref, bst_ref, mixed_scr, ts):
    nblk = ts // GBLK
    mask = _spatial_mask()
    for g in range(N_GROUPS):
        cols = slice(g * GBLK, (g + 1) * GBLK)
        wm = jnp.where(mask, ws_ref[g], 0.0).astype(BF16)
        cat = jnp.concatenate([vn_b[n * GBLK:(n + 1) * GBLK, cols] for n in range(nblk)], axis=1)
        res = _dot(wm, cat) + bst_ref[:, g:g + 1]
        for n in range(nblk):
            mixed_scr[n * GBLK:(n + 1) * GBLK, cols] = res[:, n * GBLK:(n + 1) * GBLK]


def _mix_b_fwd(z, ln_g, ln_b, ws, bst, w_out, *, ts, name, comm=None):
    s = z.shape[0]
    d = D_MODEL
    ts = _tile(ts, s)

    def body(zu_ref, zv_ref, g_ref, b_ref, ws_ref, bst_ref, w_ref, sg_ref, y_ref, mixed_scr):
        xhat, _ = _ln_fwd(_gelu(zv_ref[...].astype(F32)))
        vn = (xhat * g_ref[...] + b_ref[...]).astype(BF16)
        _spatial_mix(vn, ws_ref, bst_ref, mixed_scr, ts)
        sg = (_gelu(zu_ref[...].astype(F32)) * mixed_scr[...]).astype(BF16)
        sg_ref[...] = sg
        y_ref[...] = _dot(sg, w_ref[...])

    zspec = lambda k: pl.BlockSpec((ts, d), lambda i, k=k: (i, k))
    return _run(
        body, name=name, grid=(s // ts,),
        in_specs=[zspec(3), zspec(4), _const((1, d)), _const((1, d)), _const((N_GROUPS, GBLK, GBLK)),
                  _const((GBLK, N_GROUPS)), _const((d, d))],
        out_specs=[pl.BlockSpec((ts, d), lambda i: (i, 0))] * 2,
        out_shape=[jax.ShapeDtypeStruct((s, d), BF16), jax.ShapeDtypeStruct((s, d), F32)],
        scratch_shapes=[pltpu.VMEM((ts, d), F32)],
        args=(z, z, ln_g, ln_b, ws, bst, w_out), comm=comm)


def _pool_denoms(tile_idx, ts):
    t1 = (tile_idx * ts + 1 + lax.broadcasted_iota(jnp.int32, (ts, 1), 0)).astype(F32)
    return [jnp.minimum(t1, float(w)) for w in POOL_WINDOWS]


def _pool_diff(p, ext, denoms, k):
    cols = slice(k * POOL_GROUP, (k + 1) * POOL_GROUP)
    acc = ext[:, cols]
    step = 1
    while step < POOL_WINDOWS[k]:
        acc = acc + pltpu.roll(acc, step, 0)
        step *= 2
    return acc[POOL_HALO:] / denoms[k] - p[:, cols]


def _mix_c_fwd(z, w_pool, scale, *, ts, name, comm=None):
    s = z.shape[0]
    d = D_MODEL
    ts = _tile(ts, s)

    def body(zp_ref, w_ref, sc_ref, d_ref, y_ref, carry):
        i = pl.program_id(0)

        @pl.when(i == 0)
        def _():
            carry[...] = jnp.zeros_like(carry)
        p = zp_ref[...].astype(F32)
        ext = jnp.concatenate([carry[...], p], axis=0)
        carry[...] = p[ts - POOL_HALO:]
        denoms = _pool_denoms(i, ts)
        for k in range(len(POOL_WINDOWS)):
            cols = slice(k * POOL_GROUP, (k + 1) * POOL_GROUP)
            dk = _pool_diff(p, ext, denoms, k).astype(BF16)
            d_ref[:, cols] = dk
            y_ref[:, cols] = _dot(dk, w_ref[k]) * sc_ref[:, cols]

    return _run(
        body, name=name, grid=(s // ts,),
        in_specs=[pl.BlockSpec((ts, d), lambda i: (i, 5)), _const((4, POOL_GROUP, POOL_GROUP)), _const((1, d))],
        out_specs=[pl.BlockSpec((ts, d), lambda i: (i, 0))] * 2,
        out_shape=[jax.ShapeDtypeStruct((s, d), BF16), jax.ShapeDtypeStruct((s, d), F32)],
        scratch_shapes=[pltpu.VMEM((POOL_HALO, d), F32)],
        args=(z, w_pool, scale), comm=comm)


def _mix_o_fwd(x, z, ya, yb, yc, w_o, gt, ln_g, ln_b, *, ts, name, comm=None):
    s, d = x.shape
    ts = _tile(ts, s)

    def body(x_ref, ga_ref, gb_ref, gc_ref, ya_ref, yb_ref, yc_ref, w_ref, gt_ref, g_ref, b_ref,
             m_ref, o_ref, x1_ref):
        merged = (_sigmoid(ga_ref[...].astype(F32)) * ya_ref[...] + _sigmoid(gb_ref[...].astype(F32)) * yb_ref[...]
                  + _sigmoid(gc_ref[...].astype(F32)) * yc_ref[...]).astype(BF16)
        m_ref[...] = merged
        o = _dot(merged, w_ref[...])
        o_ref[...] = o
        xhat, _ = _ln_fwd(ALPHA * x_ref[...] + gt_ref[...] * o)
        x1_ref[...] = xhat * g_ref[...] + b_ref[...]

    row = pl.BlockSpec((ts, d), lambda i: (i, 0))
    zspec = lambda k: pl.BlockSpec((ts, d), lambda i, k=k: (i, k))
    return _run(
        body, name=name, grid=(s // ts,),
        in_specs=[row, zspec(6), zspec(7), zspec(8), row, row, row, _const((d, d)),
                  _const((1, d)), _const((1, d)), _const((1, d))],
        out_specs=[row] * 3,
        out_shape=[jax.ShapeDtypeStruct((s, d), BF16), jax.ShapeDtypeStruct((s, d), F32),
                   jax.ShapeDtypeStruct((s, d), F32)],
        args=(x, z, z, z, ya, yb, yc, w_o, gt, ln_g, ln_b), comm=comm)


def _ffn_fwd(up, x1, cw, cb, w_down, gt, ln_g, ln_b, *, ts, name, comm=None):
    s, d = x1.shape
    ts = _tile(ts, s)

    def body(up_ref, x1_ref, cw_ref, cb_ref, w_ref, gt_ref, g_ref, b_ref, ft_ref, dn_ref, x2_ref, carry, f_ref):
        @pl.when(pl.program_id(0) == 0)
        def _():
            carry[...] = jnp.zeros_like(carry)
        for c in range(D_FF // FF_CHUNK):
            ca = slice(c * FF_CHUNK, (c + 1) * FF_CHUNK)
            cg = slice(D_FF + c * FF_CHUNK, D_FF + (c + 1) * FF_CHUNK)
            ua = up_ref[:, ca].astype(F32)
            ext = jnp.concatenate([carry[:, ca], ua], axis=0)
            carry[:, ca] = ua[ts - CONV_HALO:]
            cf = _conv3(ua, ext, cw_ref[:, ca]) + cb_ref[:, ca]
            f = _gelu(cf) * up_ref[:, cg].astype(F32)
            f_ref[:, ca] = f.astype(BF16)
            ft_ref[ca, :] = f.T.astype(BF16)
        dn = _dot(f_ref[...], w_ref[...])
        dn_ref[...] = dn
        xhat, _ = _ln_fwd(ALPHA * x1_ref[...] + gt_ref[...] * dn)
        x2_ref[...] = xhat * g_ref[...] + b_ref[...]

    row = pl.BlockSpec((ts, d), lambda i: (i, 0))
    return _run(
        body, name=name, grid=(s // ts,),
        in_specs=[pl.BlockSpec((ts, 2 * D_FF), lambda i: (i, 0)), row, _const((3, D_FF)), _const((1, D_FF)),
                  _resident((D_FF, d)), _const((1, d)), _const((1, d)), _const((1, d))],
        out_specs=[pl.BlockSpec((D_FF, ts), lambda i: (0, i)), row, row],
        out_shape=[jax.ShapeDtypeStruct((D_FF, s), BF16), jax.ShapeDtypeStruct((s, d), F32),
                   jax.ShapeDtypeStruct((s, d), F32)],
        scratch_shapes=[pltpu.VMEM((CONV_HALO, D_FF), F32), pltpu.VMEM((ts, D_FF), BF16)],
        args=(up, x1, cw, cb, w_down, gt, ln_g, ln_b), comm=comm)


def _loss_fwd(y, tgt, *, ts, name):
    s, d = y.shape
    ts = _tile(ts, s)

    def body(y_ref, t_ref, dy_ref, l_ref):
        @pl.when(pl.program_id(0) == 0)
        def _():
            l_ref[...] = jnp.zeros_like(l_ref)
        e = y_ref[...] - t_ref[...]
        dy_ref[...] = e / float(d)
        l_ref[...] += 0.5 * jnp.sum(jnp.mean(e * e, axis=-1, keepdims=True), axis=0, keepdims=True)

    row = pl.BlockSpec((ts, d), lambda i: (i, 0))
    return _run(body, name=name, grid=(s // ts,), in_specs=[row, row], out_specs=[row, _const((8, 128))],
                out_shape=[jax.ShapeDtypeStruct((s, d), F32), jax.ShapeDtypeStruct((8, 128), F32)], args=(y, tgt))


def _rev(n_tiles):
    return lambda i: n_tiles - 1 - i


def _halo_spec(ts, n_tiles, halo, width, col):
    per = ts // halo
    return pl.BlockSpec((halo, width), lambda i: (jnp.maximum((n_tiles - 1 - i) * per - 1, 0), col))


def _ffn_bwd(dx2, x1, dn, up, cw, cb, w_down, w_up4, gt, ln_g, sc, *, ts, name, comm=None):
    s, d = x1.shape
    ts = _tile(ts, s)
    nt = s // ts
    rev = _rev(nt)
    wd = w_up4.shape[2]

    def w_up_cols(wu_ref, start):
        return wu_ref[start // wd, :, start % wd:start % wd + FF_CHUNK]

    def body(dx2_ref, x1_ref, dn_ref, up_ref, halo_ref, cw_ref, cb_ref, wd_ref, wu_ref, gt_ref, g_ref, sc_ref,
             ddn_ref, dup_ref, dx1_ref, redd_ref, redf_ref, dbup_ref, carry):
        i = pl.program_id(0)

        @pl.when(i == 0)
        def _():
            carry[...] = jnp.zeros_like(carry)
            redd_ref[...] = jnp.zeros_like(redd_ref)
            redf_ref[...] = jnp.zeros_like(redf_ref)
            dbup_ref[...] = jnp.zeros_like(dbup_ref)
        first_tile = i == nt - 1
        x1v, dnv, dyv = x1_ref[...], dn_ref[...], dx2_ref[...]
        xhat, rstd = _ln_fwd(ALPHA * x1v + gt_ref[...] * dnv)
        dr = _ln_bwd(dyv, g_ref[...], xhat, rstd)
        redd_ref[0:1, :] += _colsum(dyv * xhat)
        redd_ref[1:2, :] += _colsum(dyv)
        redd_ref[2:3, :] += _colsum(dr * dnv)
        ddn = (gt_ref[...] * dr).astype(BF16)
        ddn_ref[...] = ddn
        dh = jnp.zeros((ts, d), F32)
        for c in range(D_FF // FF_CHUNK):
            ca = slice(c * FF_CHUNK, (c + 1) * FF_CHUNK)
            cg = slice(D_FF + c * FF_CHUNK, D_FF + (c + 1) * FF_CHUNK)
            df = _dot_nt(ddn, wd_ref[ca, :])
            ua, ug = up_ref[:, ca].astype(F32), up_ref[:, cg].astype(F32)
            halo = jnp.where(first_tile, 0.0, halo_ref[:, ca].astype(F32)[HALO_ROWS - CONV_HALO:])
            ext = jnp.concatenate([halo, ua], axis=0)
            u1, u2 = _rows_before(ext, 1, CONV_HALO), _rows_before(ext, 2, CONV_HALO)
            cwc = cw_ref[:, ca]
            gl, dgl = _gelu_and_grad(cwc[2:3] * ua + cwc[1:2] * u1 + cwc[0:1] * u2 + cb_ref[:, ca])
            dug = df * gl
            dcf = df * ug * dgl
            redf_ref[0:1, ca] += _colsum(dcf * u2)
            redf_ref[1:2, ca] += _colsum(dcf * u1)
            redf_ref[2:3, ca] += _colsum(dcf * ua)
            redf_ref[3:4, ca] += _colsum(dcf)
            extd = jnp.concatenate([dcf, carry[:, ca]], axis=0)
            carry[:, ca] = dcf[:CONV_HALO]
            dua = cwc[2:3] * dcf + cwc[1:2] * _rows_after(extd, 1, ts) + cwc[0:1] * _rows_after(extd, 2, ts)
            dbup_ref[0:1, ca] += _colsum(dua)
            dbup_ref[0:1, cg] += _colsum(dug)
            dua_b, dug_b = dua.astype(BF16), dug.astype(BF16)
            dup_ref[:, ca] = dua_b
            dup_ref[:, cg] = dug_b
            dh = dh + _dot_nt(dua_b, w_up_cols(wu_ref, c * FF_CHUNK)) + _dot_nt(dug_b, w_up_cols(wu_ref, D_FF + c * FF_CHUNK))
        dx1_ref[...] = ALPHA * dr + dh * (1.0 + sc_ref[...])
        redd_ref[3:4, :] += _colsum(dh * x1v)
        redd_ref[4:5, :] += _colsum(dh)

    row = pl.BlockSpec((ts, d), lambda i: (rev(i), 0))
    return _run(
        body, name=name, grid=(nt,),
        in_specs=[row, row, row, pl.BlockSpec((ts, 2 * D_FF), lambda i: (rev(i), 0)),
                  _halo_spec(ts, nt, HALO_ROWS, D_FF, 0), _const((3, D_FF)), _const((1, D_FF)),
                  _resident((D_FF, d)), _resident((N_CHIPS, d, wd)), _const((1, d)), _const((1, d)), _const((1, d))],
        out_specs=[row, pl.BlockSpec((ts, 2 * D_FF), lambda i: (rev(i), 0)), row,
                   _const((8, d)), _const((8, D_FF)), _const((8, 2 * D_FF))],
        out_shape=[jax.ShapeDtypeStruct((s, d), BF16), jax.ShapeDtypeStruct((s, 2 * D_FF), BF16),
                   jax.ShapeDtypeStruct((s, d), F32), jax.ShapeDtypeStruct((8, d), F32),
                   jax.ShapeDtypeStruct((8, D_FF), F32), jax.ShapeDtypeStruct((8, 2 * D_FF), F32)],
        scratch_shapes=[pltpu.VMEM((CONV_HALO, D_FF), F32)],
        args=(dx2, x1, dn, up, up, cw, cb, w_down, w_up4, gt, ln_g, sc), comm=comm)


def _accumulate_dw(dw_ref, dwb_ref, xa, dy, first, last):
    @pl.when(first)
    def _():
        dw_ref[...] = jnp.zeros_like(dw_ref)
    dw_ref[...] += _dot_tn(xa, dy)

    @pl.when(last)
    def _():
        dwb_ref[...] = dw_ref[...].astype(BF16)


def _dw_out(d):
    return [_const((d, d))] * 2, [jax.ShapeDtypeStruct((d, d), F32), jax.ShapeDtypeStruct((d, d), BF16)]


def _grad_matmul_t(xt, dy, *, tk, tn, name, by_chip=False, comm=None):
    k, s = xt.shape
    n = dy.shape[1]

    def body(xt_ref, dy_ref, o_ref, ob_ref):
        o = _dot(xt_ref[...], dy_ref[...]).reshape(o_ref.shape)
        o_ref[...] = o
        ob_ref[...] = o.astype(BF16)

    if by_chip:
        assert tk == k
        per = n // N_CHIPS // tn
        ospec = pl.BlockSpec((1, k, tn), lambda j, i: (j // per, 0, j % per))
        shape = (N_CHIPS, k, n // N_CHIPS)
    else:
        ospec = pl.BlockSpec((tk, tn), lambda j, i: (i, j))
        shape = (k, n)
    xspec = _resident((k, s)) if tk == k else pl.BlockSpec((tk, s), lambda j, i: (i, 0))
    dspec = _resident((s, n)) if tn == n else pl.BlockSpec((s, tn), lambda j, i: (0, j))
    return _run(body, name=name, grid=(n // tn, k // tk), in_specs=[xspec, dspec], out_specs=[ospec, ospec],
                out_shape=[jax.ShapeDtypeStruct(shape, F32), jax.ShapeDtypeStruct(shape, BF16)], args=(xt, dy), comm=comm)


def _mix_o_bwd(dx1, x, o, z, ya, yb, yc, merged, w_o, gt, ln_g, *, ts, name, comm=None):
    s, d = x.shape
    ts = _tile(ts, s)
    nt = s // ts

    def body(dx1_ref, x_ref, o_ref, ga_ref, gb_ref, gc_ref, ya_ref, yb_ref, yc_ref, m_ref, w_ref, gt_ref, g_ref,
             dxa_ref, dzg_ref, dya_ref, dyb_ref, dyc_ref, red_ref, dw_ref, dwb_ref):
        i = pl.program_id(0)

        @pl.when(i == 0)
        def _():
            red_ref[...] = jnp.zeros_like(red_ref)
        dyv, ov = dx1_ref[...], o_ref[...]
        xhat, rstd = _ln_fwd(ALPHA * x_ref[...] + gt_ref[...] * ov)
        dr = _ln_bwd(dyv, g_ref[...], xhat, rstd)
        red_ref[0:1, :] += _colsum(dyv * xhat)
        red_ref[1:2, :] += _colsum(dyv)
        red_ref[2:3, :] += _colsum(dr * ov)
        dxa_ref[...] = ALPHA * dr
        d_o = (gt_ref[...] * dr).astype(BF16)
        _accumulate_dw(dw_ref, dwb_ref, m_ref[...], d_o, i == 0, i == nt - 1)
        dm = _dot_nt(d_o, w_ref[...])
        for k, (zg_ref, y_ref, dy_ref) in enumerate(((ga_ref, ya_ref, dya_ref), (gb_ref, yb_ref, dyb_ref),
                                                     (gc_ref, yc_ref, dyc_ref))):
            g = _sigmoid(zg_ref[...].astype(F32))
            dzg_ref[:, k * d:(k + 1) * d] = (dm * y_ref[...] * g * (1.0 - g)).astype(BF16)
            dy_ref[...] = (dm * g).astype(BF16)

    row = pl.BlockSpec((ts, d), lambda i: (i, 0))
    zspec = lambda k: pl.BlockSpec((ts, d), lambda i, k=k: (i, k))
    bf = jax.ShapeDtypeStruct((s, d), BF16)
    dw_specs, dw_shapes = _dw_out(d)
    return _run(
        body, name=name, grid=(nt,),
        in_specs=[row, row, row, zspec(6), zspec(7), zspec(8), row, row, row, row, _const((d, d)),
                  _const((1, d)), _const((1, d))],
        out_specs=[row, pl.BlockSpec((ts, 3 * d), lambda i: (i, 2)), row, row, row, _const((8, d))] + dw_specs,
        out_shape=[jax.ShapeDtypeStruct((s, d), F32), jax.ShapeDtypeStruct((s, D_Z), BF16), bf, bf, bf,
                   jax.ShapeDtypeStruct((8, d), F32)] + dw_shapes,
        args=(dx1, x, o, z, z, z, ya, yb, yc, merged, w_o, gt, ln_g), comm=comm)


def _mix_a_bwd(dya, a, z, dz, cw, w_out, *, ts, name, comm=None):
    s = z.shape[0]
    d = D_MODEL
    ts = _tile(ts, s)
    nt = s // ts
    rev = _rev(nt)

    def body(dya_ref, a_ref, zb_ref, zc_ref, zx_ref, hc_ref, hx_ref, cw_ref, w_ref, dz_in, dz_ref, red_ref,
             dw_ref, dwb_ref, carry):
        i = pl.program_id(0)

        @pl.when(i == 0)
        def _():
            carry[...] = jnp.zeros_like(carry)
            red_ref[...] = jnp.zeros_like(red_ref)
        _accumulate_dw(dw_ref, dwb_ref, a_ref[...], dya_ref[...], i == 0, i == nt - 1)
        zb, zc, zx = zb_ref[...].astype(F32), zc_ref[...].astype(F32), zx_ref[...].astype(F32)
        q = zc * zx
        halo = jnp.where(i == nt - 1, 0.0, (hc_ref[...].astype(F32) * hx_ref[...].astype(F32))[HALO_ROWS - CONV_HALO:])
        ext = jnp.concatenate([halo, q], axis=0)
        q1, q2 = _rows_before(ext, 1, CONV_HALO), _rows_before(ext, 2, CONV_HALO)
        cwv = cw_ref[...]
        cv = cwv[2:3] * q + cwv[1:2] * q1 + cwv[0:1] * q2
        da = _dot_nt(dya_ref[...], w_ref[...])
        dcv = da * zb
        red_ref[0:1, :] += _colsum(dcv * q2)
        red_ref[1:2, :] += _colsum(dcv * q1)
        red_ref[2:3, :] += _colsum(dcv * q)
        extd = jnp.concatenate([dcv, carry[...]], axis=0)
        carry[...] = dcv[:CONV_HALO]
        dq = cwv[2:3] * dcv + cwv[1:2] * _rows_after(extd, 1, ts) + cwv[0:1] * _rows_after(extd, 2, ts)
        dz_ref[:, 0:d] = (da * cv).astype(BF16)
        dz_ref[:, d:2 * d] = (dq * zx).astype(BF16)
        dz_ref[:, 2 * d:3 * d] = (dq * zc).astype(BF16)

    zspec = lambda k: pl.BlockSpec((ts, d), lambda i, k=k: (rev(i), k))
    row = pl.BlockSpec((ts, d), lambda i: (rev(i), 0))
    dw_specs, dw_shapes = _dw_out(d)
    return _run(
        body, name=name, grid=(nt,),
        in_specs=[row, row, zspec(0), zspec(1), zspec(2),
                  _halo_spec(ts, nt, HALO_ROWS, d, 1), _halo_spec(ts, nt, HALO_ROWS, d, 2),
                  _const((3, d)), _const((d, d)), HBM],
        out_specs=[pl.BlockSpec((ts, 3 * d), lambda i: (rev(i), 0)), _const((8, d))] + dw_specs,
        out_shape=[jax.ShapeDtypeStruct((s, D_Z), BF16), jax.ShapeDtypeStruct((8, d), F32)] + dw_shapes,
        scratch_shapes=[pltpu.VMEM((CONV_HALO, d), F32)],
        args=(dya, a, z, z, z, z, z, cw, w_out, dz), aliases={9: 0}, comm=comm)


def _mix_b_bwd(dyb, sg, z, dz, ln_g, ln_b, ws, bst, w_out, *, ts, name, comm=None):
    s = z.shape[0]
    d = D_MODEL
    ts = _tile(ts, s)
    nt = s // ts
    nblk = ts // GBLK

    def body(dyb_ref, sg_ref, zu_ref, zv_ref, g_ref, b_ref, ws_ref, bst_ref, w_ref, dz_in,
             dz_ref, red_ref, dws_ref, dbst_ref, dw_ref, dwb_ref, mixed_scr, dvn_scr, dzv_scr):
        first = (pl.program_id(0) == 0) & (pl.program_id(1) == 0)

        @pl.when(first)
        def _():
            red_ref[...] = jnp.zeros_like(red_ref)
            dws_ref[...] = jnp.zeros_like(dws_ref)
            dbst_ref[...] = jnp.zeros_like(dbst_ref)

        @pl.when(pl.program_id(1) == 0)
        def _():
            _accumulate_dw(dw_ref, dwb_ref, sg_ref[...], dyb_ref[...], first, pl.program_id(0) == nt - 1)
            u, du_dz = _gelu_and_grad(zu_ref[...].astype(F32))
            vg, dv_dz = _gelu_and_grad(zv_ref[...].astype(F32))
            xhat, rstd = _ln_fwd(vg)
            vn = (xhat * g_ref[...] + b_ref[...]).astype(BF16)
            _spatial_mix(vn, ws_ref, bst_ref, mixed_scr, ts)
            dsg = _dot_nt(dyb_ref[...], w_ref[...])
            dz_ref[...] = (dsg * mixed_scr[...] * du_dz).astype(BF16)
            dmix = dsg * u
            mask = _spatial_mask()
            for g in range(N_GROUPS):
                cols = slice(g * GBLK, (g + 1) * GBLK)
                wm = jnp.where(mask, ws_ref[g], 0.0).astype(BF16)
                dm_cat = jnp.concatenate([dmix[n * GBLK:(n + 1) * GBLK, cols] for n in range(nblk)], axis=1)
                vn_cat = jnp.concatenate([vn[n * GBLK:(n + 1) * GBLK, cols] for n in range(nblk)], axis=1)
                dm_b = dm_cat.astype(BF16)
                dbst_ref[:, g:g + 1] += jnp.sum(dm_cat, axis=1, keepdims=True)
                dws_ref[g] += jnp.where(mask, _dot_nt(dm_b, vn_cat), 0.0)
                dvn_cat = _dot_tn(wm, dm_b)
                for n in range(nblk):
                    dvn_scr[n * GBLK:(n + 1) * GBLK, cols] = dvn_cat[:, n * GBLK:(n + 1) * GBLK]
            dvn = dvn_scr[...]
            red_ref[0:1, :] += _colsum(dvn * xhat)
            red_ref[1:2, :] += _colsum(dvn)
            dzv_scr[...] = (_ln_bwd(dvn, g_ref[...], xhat, rstd) * dv_dz).astype(BF16)

        @pl.when(pl.program_id(1) == 1)
        def _():
            dz_ref[...] = dzv_scr[...]

    zspec = lambda k: pl.BlockSpec((ts, d), lambda i, h, k=k: (i, k))
    row = pl.BlockSpec((ts, d), lambda i, h: (i, 0))
    dw_specs, dw_shapes = _dw_out(d)
    return _run(
        body, name=name, grid=(nt, 2),
        in_specs=[row, row, zspec(3), zspec(4), _const((1, d)), _const((1, d)),
                  _const((N_GROUPS, GBLK, GBLK)), _const((GBLK, N_GROUPS)), _const((d, d)), HBM],
        out_specs=[pl.BlockSpec((ts, d), lambda i, h: (i, 3 + h)), _const((8, d)),
                   _const((N_GROUPS, GBLK, GBLK)), _const((GBLK, N_GROUPS))] + dw_specs,
        out_shape=[jax.ShapeDtypeStruct((s, D_Z), BF16), jax.ShapeDtypeStruct((8, d), F32),
                   jax.ShapeDtypeStruct((N_GROUPS, GBLK, GBLK), F32), jax.ShapeDtypeStruct((GBLK, N_GROUPS), F32)]
        + dw_shapes,
        scratch_shapes=[pltpu.VMEM((ts, d), F32), pltpu.VMEM((ts, d), F32), pltpu.VMEM((ts, d), BF16)],
        args=(dyb, sg, z, z, ln_g, ln_b, ws, bst, w_out, dz), aliases={9: 0}, comm=comm)


def _mix_c_bwd(dyc, z, dz, w_pool, scale, *, ts, name):
    s = z.shape[0]
    d = D_MODEL
    ts = _tile(ts, s)
    nt = s // ts
    rev = _rev(nt)

    def body(dyc_ref, zp_ref, halo_ref, w_ref, sc_ref, dz_in, dz_ref, red_ref, dw_ref, carry):
        i = pl.program_id(0)

        @pl.when(i == 0)
        def _():
            carry[...] = jnp.zeros_like(carry)
            red_ref[...] = jnp.zeros_like(red_ref)
            dw_ref[...] = jnp.zeros_like(dw_ref)
        p = zp_ref[...].astype(F32)
        ext = jnp.concatenate([jnp.where(i == nt - 1, 0.0, halo_ref[...].astype(F32)), p], axis=0)
        denoms = _pool_denoms(rev(i), ts)
        dyv = dyc_ref[...].astype(F32)
        for k in range(len(POOL_WINDOWS)):
            cols = slice(k * POOL_GROUP, (k + 1) * POOL_GROUP)
            dk = _pool_diff(p, ext, denoms, k).astype(BF16)
            red_ref[0:1, cols] += _colsum(dyv[:, cols] * _dot(dk, w_ref[k]))
            dpre = (dyv[:, cols] * sc_ref[:, cols]).astype(BF16)
            dw_ref[k] += _dot_tn(dk, dpre)
            dd = _dot_nt(dpre, w_ref[k])
            e = dd / denoms[k]
            acc = jnp.concatenate([e, carry[:, cols]], axis=0)
            carry[:, cols] = e[:POOL_HALO]
            step = 1
            while step < POOL_WINDOWS[k]:
                acc = acc + pltpu.roll(acc, acc.shape[0] - step, 0)
                step *= 2
            dz_ref[:, cols] = (acc[:ts] - dd).astype(BF16)

    return _run(
        body, name=name, grid=(nt,),
        in_specs=[pl.BlockSpec((ts, d), lambda i: (rev(i), 0)), pl.BlockSpec((ts, d), lambda i: (rev(i), 5)),
                  _halo_spec(ts, nt, POOL_HALO, d, 5), _const((4, POOL_GROUP, POOL_GROUP)), _const((1, d)), HBM],
        out_specs=[pl.BlockSpec((ts, d), lambda i: (rev(i), 5)), _const((8, d)), _const((4, POOL_GROUP, POOL_GROUP))],
        out_shape=[jax.ShapeDtypeStruct((s, D_Z), BF16), jax.ShapeDtypeStruct((8, d), F32),
                   jax.ShapeDtypeStruct((4, POOL_GROUP, POOL_GROUP), F32)],
        scratch_shapes=[pltpu.VMEM((POOL_HALO, d), F32)],
        args=(dyc, z, z, w_pool, scale, dz), aliases={5: 0})


def _in_proj_bwd(dz, w4, dxa, x, sc, *, ts, name, comm=None):
    s, d = x.shape
    ts = _tile(ts, s)
    wd = w4.shape[2]

    def body(dz_ref, w_ref, dxa_ref, x_ref, sc_ref, dx_ref, red_ref, db_ref):
        @pl.when(pl.program_id(0) == 0)
        def _():
            red_ref[...] = jnp.zeros_like(red_ref)
            db_ref[...] = jnp.zeros_like(db_ref)
        dh = jnp.zeros((ts, d), F32)
        for j in range(N_CHIPS):
            dzj = dz_ref[:, j * wd:(j + 1) * wd]
            db_ref[0:1, j * wd:(j + 1) * wd] += _colsum(dzj.astype(F32))
            dh = dh + _dot_nt(dzj, w_ref[j])
        dx_ref[...] = dxa_ref[...] + dh * (1.0 + sc_ref[...])
        red_ref[0:1, :] += _colsum(dh * x_ref[...])
        red_ref[1:2, :] += _colsum(dh)

    row = pl.BlockSpec((ts, d), lambda i: (i, 0))
    return _run(
        body, name=name, grid=(s // ts,),
        in_specs=[pl.BlockSpec((ts, D_Z), lambda i: (i, 0)), _resident((N_CHIPS, d, wd)), row, row, _const((1, d))],
        out_specs=[row, _const((8, d)), _const((8, D_Z))],
        out_shape=[jax.ShapeDtypeStruct((s, d), F32), jax.ShapeDtypeStruct((8, d), F32),
                   jax.ShapeDtypeStruct((8, D_Z), F32)],
        args=(dz, w4, dxa, x, sc), comm=comm)


def _ada_fwd(c_all, w_ada, b_ada, *, name):
    nl, d, n = w_ada.shape
    tn = n // 2

    def body(c_ref, w_ref, b_ref, o_ref):
        cv = c_ref[...]
        ca = (cv * _sigmoid(cv)).astype(BF16)
        o_ref[0] = _dot(ca, w_ref[0].astype(BF16)) + b_ref[0]

    return _run(
        body, name=name, grid=(nl, n // tn),
        in_specs=[_const((N_DEV, d)), pl.BlockSpec((1, d, tn), lambda l, j: (l, 0, j)),
                  pl.BlockSpec((1, 1, tn), lambda l, j: (l, 0, j))],
        out_specs=[pl.BlockSpec((1, N_DEV, tn), lambda l, j: (l, 0, j))],
        out_shape=[jax.ShapeDtypeStruct((nl, N_DEV, n), F32)], args=(c_all, w_ada, b_ada))[0]


def _ada_bwd(c_all, dada, *, name):
    nl, nb, n = dada.shape
    d = c_all.shape[1]
    tn = n // 2

    def body(c_ref, g_ref, o_ref):
        cv = c_ref[...]
        ca = (cv * _sigmoid(cv)).astype(BF16)
        o_ref[0] = _dot_tn(ca, g_ref[0].astype(BF16))

    return _run(
        body, name=name, grid=(nl, n // tn),
        in_specs=[_const((nb, d)), pl.BlockSpec((1, nb, tn), lambda l, j: (l, 0, j))],
        out_specs=[pl.BlockSpec((1, d, tn), lambda l, j: (l, 0, j))],
        out_shape=[jax.ShapeDtypeStruct((nl, d, n), F32)], args=(c_all, dada))[0]


def _sum4_into_half(owns, recvs, core, *, name):
    nw = len(owns)
    r = owns[0].shape[0]
    tr = _row_tile(r, max(o.shape[1] for o in owns), 2)

    def body(core_ref, *refs):
        for own_ref, recv_ref, o_ref in zip(refs[:nw], refs[nw:2 * nw], refs[2 * nw:]):
            acc = own_ref[...]
            for k in range(N_CHIPS - 1):
                acc = acc + recv_ref[k].astype(F32)
            o_ref[0] = acc

    cols = [o.shape[1] for o in owns]
    spec = pltpu.PrefetchScalarGridSpec(
        num_scalar_prefetch=1, grid=(r // tr,),
        in_specs=[pl.BlockSpec((tr, c), lambda i, core_ref: (i, 0)) for c in cols]
        + [pl.BlockSpec((N_CHIPS - 1, tr, c), lambda i, core_ref: (0, i, 0)) for c in cols],
        out_specs=[pl.BlockSpec((1, tr, c), lambda i, core_ref: (core_ref[0], i, 0)) for c in cols])
    return pl.pallas_call(
        body, name=name, grid_spec=spec, out_shape=[jax.ShapeDtypeStruct((2, r, c), F32) for c in cols],
        compiler_params=pltpu.CompilerParams(dimension_semantics=("arbitrary",), vmem_limit_bytes=VMEM_LIMIT),
    )(core, *owns, *recvs)


def _cast_into_slots(shards, layer, chip, *, name):
    quarters = 4

    def body(chip_ref, *refs):
        ins, outs = refs[:len(shards)], refs[len(shards):]
        for i_ref, o_ref in zip(ins, outs):
            o_ref[0, 0] = i_ref[0].astype(BF16)

    in_specs, out_specs, out_shape = [], [], []
    for sh in shards:
        _, r, c = sh.shape
        in_specs.append(pl.BlockSpec((1, r // quarters, c), lambda t, chip_ref: (layer, t, 0)))
        out_specs.append(pl.BlockSpec((1, 1, r // quarters, c), lambda t, chip_ref: (chip_ref[0], t // 2, t % 2, 0)))
        out_shape.append(jax.ShapeDtypeStruct((N_CHIPS, 2, r // 2, c), BF16))
    spec = pltpu.PrefetchScalarGridSpec(num_scalar_prefetch=1, grid=(quarters,), in_specs=in_specs, out_specs=out_specs)
    return pl.pallas_call(
        body, name=name, grid_spec=spec, out_shape=out_shape,
        compiler_params=pltpu.CompilerParams(dimension_semantics=("arbitrary",), vmem_limit_bytes=VMEM_LIMIT),
    )(chip, *shards)


def _sum_halves(g_f32s, theirs, place, *, name, comm=None):
    nw = len(g_f32s)
    rh = g_f32s[0].shape[2]
    cols = [g.shape[3] for g in g_f32s]
    tr = _row_tile(rh, max(cols), 2)

    def body(place_ref, *refs):
        for g_ref, t_ref, hb_ref, own_ref in zip(refs[:nw], refs[nw:2 * nw], refs[2 * nw:3 * nw], refs[3 * nw:]):
            h = g_ref[0, 0] + t_ref[0].astype(F32)
            hb_ref[0] = h.astype(BF16)

            @pl.when(pl.program_id(1) == place_ref[1])
            def _():
                own_ref[...] = h

    return _run(
        body, name=name, grid=(rh // tr, N_CHIPS), prefetch=place,
        in_specs=[pl.BlockSpec((1, 1, tr, c), lambda i, j, place_ref: (j, place_ref[0], i, 0)) for c in cols]
        + [pl.BlockSpec((1, tr, c), lambda i, j, place_ref: (j, i, 0)) for c in cols],
        out_specs=[pl.BlockSpec((1, tr, c), lambda i, j, place_ref: (j, i, 0)) for c in cols]
        + [pl.BlockSpec((tr, c), lambda i, j, place_ref: (i, 0)) for c in cols],
        out_shape=[jax.ShapeDtypeStruct((N_CHIPS, rh, c), BF16) for c in cols]
        + [jax.ShapeDtypeStruct((rh, c), F32) for c in cols],
        args=(*g_f32s, *theirs), comm=comm)


def _row_tile(r, c, mib):
    limit = max(8, (mib << 20) // (4 * c))
    if r <= limit:
        return r
    best = 8
    for t in range(8, limit + 1, 8):
        if r % t == 0:
            best = t
    return best


def _adam_math(w, g, m, v):
    mn = ADAM_B1 * m + (1.0 - ADAM_B1) * g
    vn = ADAM_B2 * v + (1.0 - ADAM_B2) * (g * g)
    m_hat = mn / (1.0 - ADAM_B1 ** ADAM_STEP)
    v_hat = vn / (1.0 - ADAM_B2 ** ADAM_STEP)
    return -ADAM_LR * (m_hat / (jnp.sqrt(v_hat) + ADAM_EPS) + ADAM_WD * w), mn, vn


def _adamw(w, g, m, v, *, name):
    r, c = w.shape
    tr = _row_tile(r, c, 2)

    def body(w_ref, g_ref, m_ref, v_ref, d_ref, mo_ref, vo_ref):
        d_ref[...], mo_ref[...], vo_ref[...] = _adam_math(w_ref[...], g_ref[...], m_ref[...], v_ref[...])

    blk = pl.BlockSpec((tr, c), lambda i: (i, 0))
    return _run(body, name=name, grid=(r // tr,), in_specs=[blk] * 4, out_specs=[blk] * 3,
                out_shape=[jax.ShapeDtypeStruct((r, c), F32)] * 3, args=(w, g, m, v))


def _adamw_sharded(w, m, v, grads, *, name, comm=None):
    nl, r, c = w.shape
    tr = _row_tile(r, c, 1)
    nt = r // tr

    def body(w_ref, m_ref, v_ref, g0_ref, g1_ref, g_ref, d_ref, mo_ref, vo_ref):
        g = jnp.where(pl.program_id(0) == 0, g0_ref[...], g1_ref[...])
        g_ref[0] = g
        d_ref[0], mo_ref[0], vo_ref[0] = _adam_math(w_ref[0], g, m_ref[0], v_ref[0])

    blk = pl.BlockSpec((1, tr, c), lambda l, i: (l, i, 0))
    part0 = pl.BlockSpec((tr, c), lambda l, i: (jnp.where(l == 0, i, nt - 1), 0))
    part1 = pl.BlockSpec((tr, c), lambda l, i: (jnp.where(l == 1, i, 0), 0))
    return _run(body, name=name, grid=(nl, nt), in_specs=[blk] * 3 + [part0, part1],
                out_specs=[blk] * 4, out_shape=[jax.ShapeDtypeStruct((nl, r, c), F32)] * 4,
                args=(w, m, v, grads[0], grads[1]), comm=comm)


_BIG = ("w_in", "w_a_out", "w_b_out", "w_pool", "w_o", "w_up", "w_down")
_COL_SHARDED = ("w_in", "w_up")
_SMALL_SHARDED = ("conv_a", "conv_ffn")
_SMALL_REPL = ("b_in", "ln_v_g", "ln_v_b", "w_spatial", "b_spatial", "pool_scale", "ln1_g", "ln1_b", "b_up",
               "conv_ffn_b", "ln2_g", "ln2_b")
_WEIGHTS = ("w_ada", "b_ada", "w_in", "b_in", "conv_a", "w_a_out", "ln_v_g", "ln_v_b", "w_spatial", "b_spatial",
            "w_b_out", "w_pool", "pool_scale", "w_o", "ln1_g", "ln1_b", "w_up", "b_up", "conv_ffn", "conv_ffn_b",
            "w_down", "ln2_g", "ln2_b")


def _shard3(a):
    return a.reshape(a.shape[0], -1, a.shape[-1])


def _use_gathered(name, g):
    g = g.reshape(N_CHIPS, -1, g.shape[-1])
    if name in _COL_SHARDED:
        return g
    if name == "w_pool":
        return g.reshape(N_CHIPS, 4, POOL_GROUP // N_CHIPS, POOL_GROUP).transpose(1, 0, 2, 3).reshape(
            4, POOL_GROUP, POOL_GROUP)
    return g.reshape(-1, g.shape[-1])


def _grad_by_chip(name, g):
    if name in _COL_SHARDED:
        return g
    if name == "w_pool":
        return g.reshape(4, N_CHIPS, POOL_GROUP // N_CHIPS, POOL_GROUP).transpose(1, 0, 2, 3).reshape(
            N_CHIPS, POOL_GROUP, POOL_GROUP)
    return g.reshape(N_CHIPS, -1, g.shape[-1])


def _pack_small(arrs):
    parts = []
    for a in arrs:
        flat = a.reshape(-1).astype(F32)
        pad = (-flat.shape[0]) % 128
        parts.append(jnp.pad(flat, (0, pad)) if pad else flat)
    flat = jnp.concatenate(parts)
    pad = (-flat.shape[0]) % 2048
    if pad:
        flat = jnp.pad(flat, (0, pad))
    return flat.reshape(-1, 128)


def _unpack_small(buf, shapes):
    lead = buf.shape[:-2]
    flat = buf.reshape(lead + (-1,))
    out, off = [], 0
    for shp in shapes:
        n = math.prod(shp)
        out.append(flat[..., off:off + n].reshape(lead + tuple(shp)))
        off += n + ((-n) % 128)
    return out


def _as2d(a):
    return a.reshape(-1, a.shape[-1])


_LATE = ("w_a_out", "w_b_out", "w_pool", "w_o")


class _Traffic:
    def __init__(self, slots, plan, core, chip):
        self.slots = slots
        self.plan = plan
        self.core = core
        self.place = jnp.concatenate([core, chip])
        self.gathered = {}
        self.ready = {}
        self.summed = {}
        self.half = {}
        self.final = {}

    def weight(self, layer, name):
        return self.gathered[(layer, name)]

    @staticmethod
    def _same_rows(keys, arrays):
        groups = []
        for k, a in zip(keys, arrays):
            if groups and groups[-1][1][-1].shape[-2] == a.shape[-2]:
                groups[-1][0].append(k)
                groups[-1][1].append(a)
            else:
                groups.append(([k], [a]))
        return groups

    def add_grad(self, layer, name, g_f32, g_bf16):
        def halves(g):
            g = _grad_by_chip(name, g)
            return g.reshape(N_CHIPS, 2, g.shape[1] // 2, g.shape[2])
        self.ready[(layer, name)] = (halves(g_f32), halves(g_bf16))

    def _comm(self, job):
        if job[0] == "gather":
            return _gather_comm([self.slots[(job[1], k)] for k in job[2]], *job[3:])
        if job[0] == "presum":
            return _presum_comm([self.ready[k][1] for k in job[1]])
        if job[0] == "scatter":
            return _scatter_comm([self.summed[k][0] for k in job[1]])
        return _join_comm([self.half[k] for k in job[1]])

    def _done(self, job, res):
        if job[0] == "gather":
            for k, r in zip(job[2], res):
                self.slots[(job[1], k)] = r
                self.gathered[(job[1], k)] = _use_gathered(k, r)
        elif job[0] == "presum":
            for keys, rs in self._same_rows(job[1], res):
                nm = f"presum_l{keys[0][0]}_{keys[0][1] if len(keys) == 1 else 'late'}"
                outs = self.run(nm, lambda cm: _sum_halves([self.ready.pop(k)[0] for k in keys], rs, self.place,
                                                           name=nm, comm=cm))
                for i, k in enumerate(keys):
                    self.summed[k] = (outs[i], outs[len(keys) + i])
        elif job[0] == "scatter":
            for keys, rs in self._same_rows(job[1], res):
                nm = f"sum_l{keys[0][0]}_{keys[0][1] if len(keys) == 1 else 'late'}"
                outs = _sum4_into_half([self.summed.pop(k)[1] for k in keys], rs, self.core, name=nm)
                self.half.update(zip(keys, outs))
        else:
            for k, r in zip(job[1], res):
                self.final[k] = r.reshape(-1, r.shape[-1])

    def run(self, name, fn):
        jobs = self.plan.get(name)
        if not jobs:
            return fn(None)
        comms = [self._comm(j) for j in jobs]
        outs, res = fn(_merge(comms))
        for job, r in zip(jobs, _split(comms, res)):
            self._done(job, r)
        return outs

    def alone(self, name):
        jobs = self.plan[name]
        comms = [self._comm(j) for j in jobs]
        for job, r in zip(jobs, _split(comms, _comm_call(_merge(comms), name=name))):
            self._done(job, r)


def _layer_fwd(x, ada, p, l, tr):
    sh1, sc1, gt1, sh2, sc2, gt2 = ada
    n = f"l{l}"
    z, ht = tr.run(f"{n}_in_proj", lambda cm: _mod_matmul(
        x, sc1, sh1, tr.weight(l, "w_in"), p["b_in"], ts=1024, tn=2304, name=f"{n}_in_proj", comm=cm))
    a, ya = tr.run(f"{n}_mix_a", lambda cm: _mix_a_fwd(z, p["conv_a"], tr.weight(l, "w_a_out"), ts=256,
                                                      name=f"{n}_mix_a", comm=cm))
    sg, yb = tr.run(f"{n}_mix_b", lambda cm: _mix_b_fwd(
        z, p["ln_v_g"], p["ln_v_b"], p["w_spatial"], p["b_spatial_t"], tr.weight(l, "w_b_out"), ts=256,
        name=f"{n}_mix_b", comm=cm))
    dpool, yc = tr.run(f"{n}_mix_c", lambda cm: _mix_c_fwd(z, tr.weight(l, "w_pool"), p["pool_scale"], ts=256,
                                                          name=f"{n}_mix_c", comm=cm))
    merged, o, x1 = tr.run(f"{n}_mix_o", lambda cm: _mix_o_fwd(
        x, z, ya, yb, yc, tr.weight(l, "w_o"), gt1, p["ln1_g"], p["ln1_b"], ts=256, name=f"{n}_mix_o", comm=cm))
    up, h2t = tr.run(f"{n}_up_proj", lambda cm: _mod_matmul(
        x1, sc2, sh2, tr.weight(l, "w_up"), p["b_up"], ts=1024, tn=1408, name=f"{n}_up_proj", comm=cm))
    ft, dn, x2 = tr.run(f"{n}_ffn", lambda cm: _ffn_fwd(
        up, x1, p["conv_ffn"], p["conv_ffn_b"], tr.weight(l, "w_down"), gt2, p["ln2_g"], p["ln2_b"], ts=256,
        name=f"{n}_ffn", comm=cm))
    saved = dict(x=x, z=z, ht=ht, a=a, ya=ya, sg=sg, yb=yb, dpool=dpool, yc=yc, merged=merged, o=o, x1=x1, h2t=h2t,
                 up=up, ft=ft, dn=dn)
    return x2, saved


def _layer_bwd(dx2, ada, p, sv, l, tr):
    sh1, sc1, gt1, sh2, sc2, gt2 = ada
    n = f"l{l}"
    ddn, dup, dx1, red_d, red_f, dbup = tr.run(f"{n}_ffn_bwd", lambda cm: _ffn_bwd(
        dx2, sv["x1"], sv["dn"], sv["up"], p["conv_ffn"], p["conv_ffn_b"], tr.weight(l, "w_down"),
        tr.weight(l, "w_up"), gt2, p["ln2_g"], sc2, ts=256, name=f"{n}_ffn_bwd", comm=cm))
    g = {}
    tr.add_grad(l, "w_down", *_grad_matmul_t(sv["ft"], ddn, tk=D_FF // N_CHIPS, tn=D_MODEL, name=f"{n}_dw_down"))
    tr.add_grad(l, "w_up", *tr.run(f"{n}_dw_up", lambda cm: _grad_matmul_t(
        sv["h2t"], dup, tk=D_MODEL, tn=FF_CHUNK, name=f"{n}_dw_up", by_chip=True, comm=cm)))
    g["ln2_g"], g["ln2_b"] = red_d[0], red_d[1]
    g["conv_ffn"], g["conv_ffn_b"], g["b_up"] = red_f[0:3], red_f[3], dbup[0]

    dxa, dz, dya, dyb, dyc, red_o, dwo, dwo_b = tr.run(f"{n}_mix_o_bwd", lambda cm: _mix_o_bwd(
        dx1, sv["x"], sv["o"], sv["z"], sv["ya"], sv["yb"], sv["yc"], sv["merged"], tr.weight(l, "w_o"), gt1,
        p["ln1_g"], ts=256, name=f"{n}_mix_o_bwd", comm=cm))
    tr.add_grad(l, "w_o", dwo, dwo_b)
    g["ln1_g"], g["ln1_b"] = red_o[0], red_o[1]

    dz, red_a, dwa, dwa_b = tr.run(f"{n}_mix_a_bwd", lambda cm: _mix_a_bwd(
        dya, sv["a"], sv["z"], dz, p["conv_a"], tr.weight(l, "w_a_out"), ts=256, name=f"{n}_mix_a_bwd", comm=cm))
    tr.add_grad(l, "w_a_out", dwa, dwa_b)
    g["conv_a"] = red_a[0:3]

    dz, red_b, dws, dbst, dwb, dwb_b = tr.run(f"{n}_mix_b_bwd", lambda cm: _mix_b_bwd(
        dyb, sv["sg"], sv["z"], dz, p["ln_v_g"], p["ln_v_b"], p["w_spatial"], p["b_spatial_t"],
        tr.weight(l, "w_b_out"), ts=256, name=f"{n}_mix_b_bwd", comm=cm))
    tr.add_grad(l, "w_b_out", dwb, dwb_b)
    g["ln_v_g"], g["ln_v_b"], g["w_spatial"], g["b_spatial"] = red_b[0], red_b[1], dws, dbst.T

    dz, red_c, dwp = _mix_c_bwd(dyc, sv["z"], dz, tr.weight(l, "w_pool"), p["pool_scale"], ts=256,
                                name=f"{n}_mix_c_bwd")
    g["pool_scale"] = red_c[0]
    tr.add_grad(l, "w_pool", dwp, dwp.astype(BF16))

    tr.add_grad(l, "w_in", *tr.run(f"{n}_dw_in", lambda cm: _grad_matmul_t(
        sv["ht"], dz, tk=D_MODEL, tn=1152, name=f"{n}_dw_in", by_chip=True, comm=cm)))
    if f"{n}_presum_tail" in tr.plan:
        tr.alone(f"{n}_presum_tail")
    dx, red_i, dbin = tr.run(f"{n}_in_proj_bwd", lambda cm: _in_proj_bwd(
        dz, tr.weight(l, "w_in"), dxa, sv["x"], sc1, ts=256, name=f"{n}_in_proj_bwd", comm=cm))
    g["b_in"] = dbin[0]
    dada = jnp.stack([red_i[1], red_i[0], red_o[2], red_d[4], red_d[3], red_d[2]])
    return dx, g, dada


def _traffic_plan():
    plan = {
        "gather_l0": [("gather", 0, ("w_in",) + _LATE)],
        "l0_in_proj": [("gather", 0, ("w_up", "w_down"))],
        "l0_mix_o": [("gather", 1, _LATE)],
        "l0_up_proj": [("gather", 1, ("w_in",), (1, 2))],
        "l0_ffn": [("gather", 1, ("w_in",), (3,))],
        "l1_in_proj": [("gather", 1, ("w_up", "w_down"))],
    }
    for l in reversed(range(DEPTH)):
        late = [(l, k) for k in _LATE]
        plan.update({
            f"l{l}_mix_o_bwd": [("presum", [(l, "w_down"), (l, "w_up")])],
            f"l{l}_mix_b_bwd": [("scatter", [(l, "w_down"), (l, "w_up")])],
            f"l{l}_dw_in": [("presum", late), ("join", [(l, "w_down"), (l, "w_up")])],
        })
    late0, late1 = [(0, k) for k in _LATE], [(1, k) for k in _LATE]
    plan["l1_in_proj_bwd"] = [("presum", [(1, "w_in")]), ("scatter", late1)]
    plan["l0_ffn_bwd"] = [("scatter", [(1, "w_in")]), ("join", late1)]
    plan["l0_dw_up"] = [("join", [(1, "w_in")])]
    plan["l0_presum_tail"] = [("presum", [(0, "w_in")])]
    plan["presum_l0_w_in"] = [("scatter", late0)]
    plan["l0_in_proj_bwd"] = [("scatter", [(0, "w_in")])]
    plan["join_tail"] = [("join", [(0, "w_in")] + late0)]
    return plan


def kernel(x, c, w_ada, b_ada, w_in, b_in, conv_a, w_a_out, ln_v_g, ln_v_b, w_spatial, b_spatial, w_b_out, w_pool, pool_scale, w_o, ln1_g, ln1_b, w_up, b_up, conv_ffn, conv_ffn_b, w_down, ln2_g, ln2_b, loss_target, m_w_ada, m_b_ada, m_w_in, m_b_in, m_conv_a, m_w_a_out, m_ln_v_g, m_ln_v_b, m_w_spatial, m_b_spatial, m_w_b_out, m_w_pool, m_pool_scale, m_w_o, m_ln1_g, m_ln1_b, m_w_up, m_b_up, m_conv_ffn, m_conv_ffn_b, m_w_down, m_ln2_g, m_ln2_b, v_w_ada, v_b_ada, v_w_in, v_b_in, v_conv_a, v_w_a_out, v_ln_v_g, v_ln_v_b, v_w_spatial, v_b_spatial, v_w_b_out, v_w_pool, v_pool_scale, v_w_o, v_ln1_g, v_ln1_b, v_w_up, v_b_up, v_conv_ffn, v_conv_ffn_b, v_w_down, v_ln2_g, v_ln2_b):
    args = locals()
    w = {k: args[k] for k in _WEIGHTS}
    m = {k: args["m_" + k] for k in _WEIGHTS}
    v = {k: args["v_" + k] for k in _WEIGHTS}
    d = D_MODEL
    mx, my, mc = _my_coords()
    chip = 2 * mx + my
    me = 4 * mx + 2 * my + mc

    small_shapes = [c.shape, conv_a.shape, conv_ffn.shape]
    small_all = _all_gather8(_pack_small([c, conv_a, conv_ffn]), name="gather_small")
    c_all, conv_a_st, conv_ffn_st = _unpack_small(small_all, small_shapes)
    c_all = c_all.reshape(N_DEV, d)
    conv_full = {"conv_a": jnp.concatenate([conv_a_st[2 * j] for j in range(N_CHIPS)], axis=-1),
                 "conv_ffn": jnp.concatenate([conv_ffn_st[2 * j] for j in range(N_CHIPS)], axis=-1)}

    chip_idx = jnp.reshape(chip, (1,)).astype(jnp.int32)
    slots = {}
    for l in range(DEPTH):
        bufs = _cast_into_slots([_shard3(w[k]) for k in _BIG], l, chip_idx, name=f"cast_l{l}")
        slots.update({(l, k): b for k, b in zip(_BIG, bufs)})
    tr = _Traffic(slots, _traffic_plan(), jnp.reshape(mc, (1,)).astype(jnp.int32), chip_idx)
    tr.alone("gather_l0")

    n_ada = w_ada.shape[2]
    b_ada_mine = lax.dynamic_slice_in_dim(b_ada, chip * n_ada, n_ada, axis=1)
    ada_part = _ada_fwd(c_all, w_ada, b_ada_mine.reshape(DEPTH, 1, n_ada), name="ada_fwd")
    ada_all = _all_gather8(_pack_small([ada_part]), name="gather_ada")
    ada_st = _unpack_small(ada_all, [ada_part.shape])[0][0::2]
    ada_rows = jnp.concatenate([ada_st[j] for j in range(N_CHIPS)], axis=-1)
    ada_mine = lax.dynamic_index_in_dim(ada_rows, me, axis=1, keepdims=False)

    def layer_params(l):
        p = {k: conv_full[k][l] for k in _SMALL_SHARDED}
        for k in ("b_in", "ln_v_g", "ln_v_b", "pool_scale", "ln1_g", "ln1_b", "b_up", "conv_ffn_b", "ln2_g", "ln2_b"):
            p[k] = w[k][l].reshape(1, -1)
        p["w_spatial"] = w_spatial[l]
        p["b_spatial_t"] = b_spatial[l].T
        return p

    xs = x[0]
    saved, adas, params = [], [], []
    for l in range(DEPTH):
        ada = [ada_mine[l, k * d:(k + 1) * d].reshape(1, d) for k in range(6)]
        p = layer_params(l)
        xs, sv = _layer_fwd(xs, ada, p, l, tr)
        saved.append(sv), adas.append(ada), params.append(p)
    dx, loss_blk = _loss_fwd(xs, loss_target[0], ts=512, name="loss")

    grads, dadas = [None] * DEPTH, [None] * DEPTH
    for l in reversed(range(DEPTH)):
        dx, grads[l], dadas[l] = _layer_bwd(dx, adas[l], params[l], saved[l], l, tr)
    tr.alone("join_tail")
    dada = jnp.stack(dadas).reshape(DEPTH, 6 * d)

    small_names = _SMALL_REPL + _SMALL_SHARDED
    small_g = [jnp.stack([grads[l][k] for l in range(DEPTH)]) for k in small_names]
    gsum = dict(zip(small_names, _unpack_small(_all_reduce_small(_pack_small(small_g), name="reduce_small"),
                                               [a.shape for a in small_g])))
    tail_g = [dada, loss_blk[0:1, 0:1]]
    tail_all, tail_sum = _all_gather8(_pack_small(tail_g), name="gather_dada", with_sum=True)
    gsum["b_ada"], loss_sum = _unpack_small(tail_sum, [a.shape for a in tail_g])
    loss = loss_sum[0, 0]
    dada_all = _unpack_small(tail_all, [a.shape for a in tail_g])[0]
    for k in _SMALL_SHARDED:
        wd = gsum[k].shape[-1] // N_CHIPS
        gsum[k] = lax.dynamic_slice_in_dim(gsum[k], chip * wd, wd, axis=gsum[k].ndim - 1)

    dada_cols = lax.dynamic_slice_in_dim(dada_all, chip * n_ada, n_ada, axis=2)
    dada_cols = jnp.pad(jnp.swapaxes(dada_cols, 0, 1), ((0, 0), (0, N_DEV), (0, 0)))
    gsum["w_ada"] = _ada_bwd(jnp.pad(c_all, ((0, N_DEV), (0, 0))), dada_cols, name="ada_bwd")

    out_g, out_d, out_m, out_v = {}, {}, {}, {}
    for k in _WEIGHTS:
        shp = w[k].shape
        if k in _BIG:
            res = tr.run(f"adamw_{k}", lambda cm: _adamw_sharded(
                _shard3(w[k]), _shard3(m[k]), _shard3(v[k]), [tr.final[(l, k)] for l in range(DEPTH)],
                name=f"adamw_{k}", comm=cm))
        else:
            gk = gsum[k].reshape(shp)
            res = [gk] + list(_adamw(_as2d(w[k]), _as2d(gk), _as2d(m[k]), _as2d(v[k]), name=f"adamw_{k}"))
        out_g[k], out_d[k], out_m[k], out_v[k] = [r.reshape(shp) for r in res]

    return (loss, dx[None], *[out_g[k] for k in _WEIGHTS], *[out_d[k] for k in _WEIGHTS],
            *[out_m[k] for k in _WEIGHTS], *[out_v[k] for k in _WEIGHTS])
```

```python
import math
from typing import Callable, NamedTuple

import jax
import jax.numpy as jnp
from jax import lax
from jax.experimental import pallas as pl
from jax.experimental.pallas import tpu as pltpu

F32 = jnp.float32
BF16 = jnp.bfloat16

D_MODEL = 1024
D_Z = 9216
D_FF = 2816
N_GROUPS = 8
GBLK = 128
CHUNK = 64
POOL_WINDOWS = (2, 4, 8, 16)
POOL_GROUP = 256
POOL_HALO = 16
CONV_HALO = 8
HALO_ROWS = 16
DEPTH = 2
ALPHA = (2 * DEPTH) ** 0.25
LN_EPS = 1e-5
ADAM_LR, ADAM_B1, ADAM_B2, ADAM_EPS, ADAM_WD, ADAM_STEP = 0.001, 0.9, 0.999, 1e-08, 0.01, 10
N_CHIPS = 4
N_DEV = 8
FF_CHUNK = 1408
MESH = pl.DeviceIdType.MESH
VMEM_LIMIT = 56 * 1024 * 1024
HBM = pl.BlockSpec(memory_space=pl.ANY)


def _dot(a, b):
    return jnp.dot(a, b, preferred_element_type=F32)


def _dot_nt(a, b):
    return lax.dot_general(a, b, (((1,), (1,)), ((), ())), preferred_element_type=F32)


def _dot_tn(a, b):
    return lax.dot_general(a, b, (((0,), (0,)), ((), ())), preferred_element_type=F32)


_GELU_C = math.sqrt(2.0 / math.pi)


def _gelu_and_grad(x):
    x2 = x * x
    t = jnp.tanh(_GELU_C * (x + 0.044715 * x * x2))
    g = 0.5 * x * (1.0 + t)
    dg = 0.5 * (1.0 + t) + 0.5 * x * (1.0 - t * t) * (_GELU_C * (1.0 + 3 * 0.044715 * x2))
    return g, dg


def _gelu(x):
    return 0.5 * x * (1.0 + jnp.tanh(_GELU_C * (x + 0.044715 * x * x * x)))


def _sigmoid(x):
    return 1.0 / (1.0 + jnp.exp(-x))


def _ln_fwd(r):
    mu = jnp.mean(r, axis=-1, keepdims=True)
    xc = r - mu
    var = jnp.mean(xc * xc, axis=-1, keepdims=True)
    rstd = lax.rsqrt(var + LN_EPS)
    return xc * rstd, rstd


def _ln_bwd(dy, g, xhat, rstd):
    dxh = dy * g
    m1 = jnp.mean(dxh, axis=-1, keepdims=True)
    m2 = jnp.mean(dxh * xhat, axis=-1, keepdims=True)
    return rstd * (dxh - m1 - xhat * m2)


def _rows_before(ext, k, halo):
    return pltpu.roll(ext, k, 0)[halo:]


def _rows_after(ext, k, n):
    return pltpu.roll(ext, ext.shape[0] - k, 0)[:n]


def _colsum(v):
    return jnp.sum(v, axis=0, keepdims=True)


def _spatial_mask():
    i = lax.broadcasted_iota(jnp.int32, (GBLK, GBLK), 0)
    j = lax.broadcasted_iota(jnp.int32, (GBLK, GBLK), 1)
    return (j // CHUNK) <= (i // CHUNK)


def _const(shape):
    n = len(shape)
    return pl.BlockSpec(shape, lambda *_: (0,) * n)


def _resident(shape):
    n = len(shape)
    return pl.BlockSpec(shape, lambda *_: (0,) * n, pipeline_mode=pl.Buffered(1))


def _tile(ts, s):
    return min(ts, s)


class _Comm(NamedTuple):
    srcs: tuple
    dsts: tuple
    n_remote: int
    n_local: int
    build: Callable
    alias: tuple = ()


def _my_coords():
    return lax.axis_index("x"), lax.axis_index("y"), lax.axis_index("c")


def _chip_peer(k):
    mx, my, mc = _my_coords()
    return (mx ^ ((k >> 1) & 1), my ^ (k & 1), mc)


def _sem_scratch(comm):
    return [pltpu.SemaphoreType.DMA((max(comm.n_remote, 1),)), pltpu.SemaphoreType.DMA((max(comm.n_remote, 1),)),
            pltpu.SemaphoreType.DMA((max(comm.n_local, 1),))]


def _run(body, *, name, grid, in_specs, out_specs, out_shape, args, scratch_shapes=(), comm=None, aliases=None,
         prefetch=None):
    sem = ("arbitrary",) * len(grid)
    cparams = pltpu.CompilerParams(dimension_semantics=sem, vmem_limit_bytes=VMEM_LIMIT)
    aliases = dict(aliases or {})
    n_pre = 0 if prefetch is None else 1

    def call(fn, in_specs, out_specs, out_shape, scratch_shapes, args):
        if prefetch is None:
            return pl.pallas_call(fn, name=name, grid=grid, in_specs=in_specs, out_specs=out_specs, out_shape=out_shape,
                                  scratch_shapes=scratch_shapes, compiler_params=cparams,
                                  input_output_aliases=aliases)(*args)
        spec = pltpu.PrefetchScalarGridSpec(num_scalar_prefetch=1, grid=grid, in_specs=in_specs, out_specs=out_specs,
                                            scratch_shapes=scratch_shapes)
        return pl.pallas_call(fn, name=name, grid_spec=spec, out_shape=out_shape, compiler_params=cparams,
                              input_output_aliases={k + 1: v for k, v in aliases.items()})(prefetch, *args)

    if comm is None:
        return call(body, list(in_specs), list(out_specs), list(out_shape), list(scratch_shapes), args)
    n_in, n_cs, n_out, n_cd, n_scr = len(in_specs), len(comm.srcs), len(out_specs), len(comm.dsts), len(scratch_shapes)
    aliases.update({n_in + si: n_out + di for si, di in comm.alias})
    total = math.prod(grid)
    mid_step = min(total - 1, int(total * 0.9))

    def wrapped(*refs):
        pre, refs = refs[:n_pre], refs[n_pre:]
        ins, refs = refs[:n_in], refs[n_in:]
        csrc, refs = refs[:n_cs], refs[n_cs:]
        outs, refs = refs[:n_out], refs[n_out:]
        cdst, refs = refs[:n_cd], refs[n_cd:]
        scr, sems = refs[:n_scr], refs[n_scr:]
        step = pl.program_id(0)
        for ax in range(1, len(grid)):
            step = step * grid[ax] + pl.program_id(ax)
        first, mid, last = comm.build(csrc, cdst, *sems, 0, 0)
        pl.when(step == 0)(first)
        if mid is not None:
            pl.when(step == mid_step)(mid)
        body(*pre, *ins, *outs, *scr)
        pl.when(step == total - 1)(last)

    res = call(wrapped, list(in_specs) + [HBM] * n_cs, list(out_specs) + [HBM] * n_cd,
               list(out_shape) + list(comm.dsts), list(scratch_shapes) + _sem_scratch(comm), (*args, *comm.srcs))
    return res[:n_out], res[n_out:]


def _comm_call(comm, *, name):
    def body(*refs):
        n_cs, n_cd = len(comm.srcs), len(comm.dsts)
        first, mid, last = comm.build(refs[:n_cs], refs[n_cs:n_cs + n_cd], *refs[n_cs + n_cd:], 0, 0)
        first()
        if mid is not None:
            mid()
        last()

    return pl.pallas_call(body, name=name, in_specs=[HBM] * len(comm.srcs), out_specs=[HBM] * len(comm.dsts),
                          out_shape=list(comm.dsts), scratch_shapes=_sem_scratch(comm),
                          input_output_aliases=dict(comm.alias))(*comm.srcs)


def _gather_comm(bufs, peers=(1, 2, 3)):
    dsts = tuple(jax.ShapeDtypeStruct(b.shape, b.dtype) for b in bufs)
    nw = len(bufs)

    def build(srcs, outs, send_sems, recv_sems, local_sems, r0, l0):
        mx, my, mc = _my_coords()
        me = 2 * mx + my
        sibling = (mx, my, 1 - mc)

        def rdma(src, dst, idx, peer):
            return pltpu.make_async_remote_copy(src_ref=src, dst_ref=dst, send_sem=send_sems.at[r0 + idx],
                                                recv_sem=recv_sems.at[r0 + idx], device_id=peer, device_id_type=MESH)

        def ici(w, k, slot):
            return rdma(outs[w].at[me, mc], outs[w].at[slot, mc], 6 * w + k - 1, _chip_peer(k))

        def fwd(w, k, half):
            return rdma(outs[w].at[me ^ k, mc], outs[w].at[me ^ k, half], 6 * w + 2 + k, sibling)

        def first():
            for w in range(nw):
                for k in peers:
                    ici(w, k, me).start()

        def mid():
            for w in range(nw):
                for k in peers:
                    ici(w, k, me ^ k).wait_recv()
                    fwd(w, k, mc).start()

        def last():
            for w in range(nw):
                for k in peers:
                    fwd(w, k, 1 - mc).wait_recv()
                    ici(w, k, me).wait_send()
                    fwd(w, k, mc).wait_send()

        return first, mid, last

    return _Comm(tuple(bufs), dsts, 6 * nw, 0, build, tuple((w, w) for w in range(nw)))


def _symmetric(make_remote, make_local, make_incoming=None):
    def first():
        for cp in make_remote() + make_local():
            cp.start()

    def last():
        for cp in (make_incoming or make_remote)():
            cp.wait_recv()
        for cp in make_remote():
            cp.wait_send()
        for cp in make_local():
            cp.wait()

    return first, None, last


def _presum_comm(g_bf16):
    nw = len(g_bf16)
    dsts = tuple(jax.ShapeDtypeStruct((N_CHIPS,) + g.shape[2:], BF16) for g in g_bf16)

    def build(srcs, outs, send_sems, recv_sems, local_sems, r0, l0):
        mx, my, mc = _my_coords()

        def remote():
            return [pltpu.make_async_remote_copy(
                src_ref=srcs[w].at[j, 1 - mc], dst_ref=outs[w].at[j], send_sem=send_sems.at[r0 + N_CHIPS * w + j],
                recv_sem=recv_sems.at[r0 + N_CHIPS * w + j], device_id=(mx, my, 1 - mc), device_id_type=MESH)
                for w in range(nw) for j in range(N_CHIPS)]

        return _symmetric(remote, lambda: [])

    return _Comm(tuple(g_bf16), dsts, N_CHIPS * nw, 0, build)


def _scatter_comm(h_bf16):
    nw = len(h_bf16)
    dsts = tuple(jax.ShapeDtypeStruct((N_CHIPS - 1,) + h.shape[1:], BF16) for h in h_bf16)

    def build(srcs, outs, send_sems, recv_sems, local_sems, r0, l0):
        mx, my, _ = _my_coords()
        me = 2 * mx + my

        def remote():
            return [pltpu.make_async_remote_copy(
                src_ref=srcs[w].at[me ^ k], dst_ref=outs[w].at[k - 1], send_sem=send_sems.at[r0 + 3 * w + k - 1],
                recv_sem=recv_sems.at[r0 + 3 * w + k - 1], device_id=_chip_peer(k), device_id_type=MESH)
                for w in range(nw) for k in range(1, N_CHIPS)]

        return _symmetric(remote, lambda: [])

    return _Comm(tuple(h_bf16), dsts, 3 * nw, 0, build)


def _join_comm(bufs):
    nw = len(bufs)
    dsts = tuple(jax.ShapeDtypeStruct(b.shape, b.dtype) for b in bufs)

    def build(srcs, outs, send_sems, recv_sems, local_sems, r0, l0):
        mx, my, mc = _my_coords()

        def remote(half=mc):
            return [pltpu.make_async_remote_copy(
                src_ref=outs[w].at[mc], dst_ref=outs[w].at[half], send_sem=send_sems.at[r0 + w],
                recv_sem=recv_sems.at[r0 + w], device_id=(mx, my, 1 - mc), device_id_type=MESH) for w in range(nw)]

        return _symmetric(remote, lambda: [], lambda: remote(1 - mc))

    return _Comm(tuple(bufs), dsts, nw, 0, build, tuple((w, w) for w in range(nw)))


def _merge(comms):
    comms = list(comms)
    if len(comms) == 1:
        return comms[0]

    def build(srcs, outs, send_sems, recv_sems, local_sems, r0, l0):
        phases, s0, d0 = [], 0, 0
        for cm in comms:
            phases.append(cm.build(srcs[s0:s0 + len(cm.srcs)], outs[d0:d0 + len(cm.dsts)], send_sems, recv_sems,
                                   local_sems, r0, l0))
            s0, d0, r0, l0 = s0 + len(cm.srcs), d0 + len(cm.dsts), r0 + cm.n_remote, l0 + cm.n_local

        def run(idx):
            fns = [ph[idx] for ph in phases if ph[idx] is not None]
            if not fns:
                return None

            def go():
                for fn in fns:
                    fn()
            return go

        return run(0), run(1), run(2)

    alias, s0, d0 = [], 0, 0
    for cm in comms:
        alias += [(s0 + si, d0 + di) for si, di in cm.alias]
        s0, d0 = s0 + len(cm.srcs), d0 + len(cm.dsts)
    return _Comm(sum((cm.srcs for cm in comms), ()), sum((cm.dsts for cm in comms), ()),
                 sum(cm.n_remote for cm in comms), sum(cm.n_local for cm in comms), build, tuple(alias))


def _split(comms, res):
    out, d0 = [], 0
    for cm in comms:
        out.append(list(res[d0:d0 + len(cm.dsts)]))
        d0 += len(cm.dsts)
    return out


def _all_reduce_small(x, *, name):
    r, lanes = x.shape
    half = r // 2
    assert half % 8 == 0

    def body(x_ref, out_ref, sib_ref, slots_ref, send_sems, recv_sems):
        mx, my, mc = _my_coords()
        me = 2 * mx + my
        sibling = (mx, my, 1 - mc)
        mine = pl.ds(pl.multiple_of(mc * half, 8), half)
        theirs = pl.ds(pl.multiple_of((1 - mc) * half, 8), half)

        def to_sibling(src, dst, idx):
            return pltpu.make_async_remote_copy(src_ref=src, dst_ref=dst, send_sem=send_sems.at[idx],
                                                recv_sem=recv_sems.at[idx], device_id=sibling, device_id_type=MESH)

        swap = to_sibling(x_ref.at[theirs], sib_ref, 0)
        swap.start()
        swap.wait_recv()
        swap.wait_send()
        slots_ref[me] = x_ref[mine, :] + sib_ref[...]

        def copy(k, slot):
            return pltpu.make_async_remote_copy(
                src_ref=slots_ref.at[me], dst_ref=slots_ref.at[slot], send_sem=send_sems.at[k], recv_sem=recv_sems.at[k],
                device_id=_chip_peer(k), device_id_type=MESH)

        sends = [copy(k, me) for k in range(1, N_CHIPS)]
        for cp in sends:
            cp.start()
        for k in range(1, N_CHIPS):
            copy(k, me ^ k).wait_recv()
        for cp in sends:
            cp.wait_send()
        acc = slots_ref[0]
        for j in range(1, N_CHIPS):
            acc = acc + slots_ref[j]
        out_ref[mine, :] = acc
        join = to_sibling(out_ref.at[mine], out_ref.at[mine], N_CHIPS)
        join.start()
        to_sibling(out_ref.at[mine], out_ref.at[theirs], N_CHIPS).wait_recv()
        join.wait_send()

    vmem = pl.BlockSpec(memory_space=pltpu.VMEM)
    return pl.pallas_call(
        body, name=name, in_specs=[vmem], out_specs=vmem, out_shape=jax.ShapeDtypeStruct((r, lanes), F32),
        scratch_shapes=[pltpu.VMEM((half, lanes), F32), pltpu.VMEM((N_CHIPS, half, lanes), F32),
                        pltpu.SemaphoreType.DMA((N_CHIPS + 1,)), pltpu.SemaphoreType.DMA((N_CHIPS + 1,))],
        compiler_params=pltpu.CompilerParams(vmem_limit_bytes=VMEM_LIMIT),
    )(x)


def _all_gather8(x, *, name, with_sum=False):
    r, lanes = x.shape

    def body(x_ref, out_ref, *rest):
        if with_sum:
            sum_ref, send_sems, recv_sems, local_sem = rest
        else:
            send_sems, recv_sems, local_sem = rest
        mx, my, mc = _my_coords()
        me = 4 * mx + 2 * my + mc

        def peer(k):
            return (mx ^ ((k >> 2) & 1), my ^ ((k >> 1) & 1), mc ^ (k & 1))

        def copy(k, slot):
            return pltpu.make_async_remote_copy(
                src_ref=x_ref, dst_ref=out_ref.at[slot], send_sem=send_sems.at[k - 1], recv_sem=recv_sems.at[k - 1],
                device_id=peer(k), device_id_type=MESH)

        mine = pltpu.make_async_copy(x_ref, out_ref.at[me], local_sem)
        mine.start()
        sends = [copy(k, me) for k in range(1, N_DEV)]
        for cp in sends:
            cp.start()
        for k in range(1, N_DEV):
            copy(k, me ^ k).wait_recv()
        for cp in sends:
            cp.wait_send()
        mine.wait()
        if with_sum:
            acc = out_ref[0]
            for k in range(1, N_DEV):
                acc = acc + out_ref[k]
            sum_ref[...] = acc

    vmem = pl.BlockSpec(memory_space=pltpu.VMEM)
    out_shape = [jax.ShapeDtypeStruct((N_DEV, r, lanes), F32)]
    if with_sum:
        out_shape.append(jax.ShapeDtypeStruct((r, lanes), F32))
    res = pl.pallas_call(
        body, name=name, in_specs=[vmem], out_specs=[vmem] * len(out_shape), out_shape=out_shape,
        scratch_shapes=[pltpu.SemaphoreType.DMA((N_DEV - 1,)), pltpu.SemaphoreType.DMA((N_DEV - 1,)),
                        pltpu.SemaphoreType.DMA],
        compiler_params=pltpu.CompilerParams(vmem_limit_bytes=VMEM_LIMIT),
    )(x)
    return res if with_sum else res[0]


def _mod_matmul(x, sc, sh, w4, b, *, ts, tn, name, comm=None):
    s, d = x.shape
    wd = w4.shape[2]
    n = N_CHIPS * wd
    per = wd // tn
    ts = _tile(ts, s)

    def body(x_ref, sc_ref, sh_ref, w_ref, b_ref, o_ref, ht_ref, h_scr):
        @pl.when(pl.program_id(1) == 0)
        def _():
            h = x_ref[...] * (1.0 + sc_ref[...]) + sh_ref[...]
            h_scr[...] = h.astype(BF16)
            ht_ref[...] = h.T.astype(BF16)
        o_ref[...] = (_dot(h_scr[...], w_ref[0]) + b_ref[...]).astype(BF16)

    return _run(
        body, name=name, grid=(s // ts, n // tn),
        in_specs=[pl.BlockSpec((ts, d), lambda i, j: (i, 0)), _const((1, d)), _const((1, d)),
                  pl.BlockSpec((1, d, tn), lambda i, j: (j // per, 0, j % per)),
                  pl.BlockSpec((1, tn), lambda i, j: (0, j))],
        out_specs=[pl.BlockSpec((ts, tn), lambda i, j: (i, j)), pl.BlockSpec((d, ts), lambda i, j: (0, i))],
        out_shape=[jax.ShapeDtypeStruct((s, n), BF16), jax.ShapeDtypeStruct((d, s), BF16)],
        scratch_shapes=[pltpu.VMEM((ts, d), BF16)],
        args=(x, sc, sh, w4, b), comm=comm)


def _conv3(q, ext, cw):
    return cw[2:3] * q + cw[1:2] * _rows_before(ext, 1, CONV_HALO) + cw[0:1] * _rows_before(ext, 2, CONV_HALO)


def _mix_a_fwd(z, cw, w_out, *, ts, name, comm=None):
    s = z.shape[0]
    d = D_MODEL
    ts = _tile(ts, s)

    def body(zb_ref, zc_ref, zx_ref, cw_ref, w_ref, a_ref, y_ref, carry):
        @pl.when(pl.program_id(0) == 0)
        def _():
            carry[...] = jnp.zeros_like(carry)
        q = zc_ref[...].astype(F32) * zx_ref[...].astype(F32)
        ext = jnp.concatenate([carry[...], q], axis=0)
        a = (zb_ref[...].astype(F32) * _conv3(q, ext, cw_ref[...])).astype(BF16)
        carry[...] = q[ts - CONV_HALO:]
        a_ref[...] = a
        y_ref[...] = _dot(a, w_ref[...])

    zspec = lambda k: pl.BlockSpec((ts, d), lambda i, k=k: (i, k))
    return _run(
        body, name=name, grid=(s // ts,),
        in_specs=[zspec(0), zspec(1), zspec(2), _const((3, d)), _const((d, d))],
        out_specs=[pl.BlockSpec((ts, d), lambda i: (i, 0))] * 2,
        out_shape=[jax.ShapeDtypeStruct((s, d), BF16), jax.ShapeDtypeStruct((s, d), F32)],
        scratch_shapes=[pltpu.VMEM((CONV_HALO, d), F32)],
        args=(z, z, z, cw, w_out), comm=comm)


def _spatial_mix(vn_b, ws_ref, bst_ref, mixed_scr, ts):
    nblk = ts // GBLK
    mask = _spatial_mask()
    for g in range(N_GROUPS):
        cols = slice(g * GBLK, (g + 1) * GBLK)
        wm = jnp.where(mask, ws_ref[g], 0.0).astype(BF16)
        cat = jnp.concatenate([vn_b[n * GBLK:(n + 1) * GBLK, cols] for n in range(nblk)], axis=1)
        res = _dot(wm, cat) + bst_ref[:, g:g + 1]
        for n in range(nblk):
            mixed_scr[n * GBLK:(n + 1) * GBLK, cols] = res[:, n * GBLK:(n + 1) * GBLK]


def _mix_b_fwd(z, ln_g, ln_b, ws, bst, w_out, *, ts, name, comm=None):
    s = z.shape[0]
    d = D_MODEL
    ts = _tile(ts, s)

    def body(zu_ref, zv_ref, g_ref, b_ref, ws_ref, bst_ref, w_ref, sg_ref, y_ref, mixed_scr):
        xhat, _ = _ln_fwd(_gelu(zv_ref[...].astype(F32)))
        vn = (xhat * g_ref[...] + b_ref[...]).astype(BF16)
        _spatial_mix(vn, ws_ref, bst_ref, mixed_scr, ts)
        sg = (_gelu(zu_ref[...].astype(F32)) * mixed_scr[...]).astype(BF16)
        sg_ref[...] = sg
        y_ref[...] = _dot(sg, w_ref[...])

    zspec = lambda k: pl.BlockSpec((ts, d), lambda i, k=k: (i, k))
    return _run(
        body, name=name, grid=(s // ts,),
        in_specs=[zspec(3), zspec(4), _const((1, d)), _const((1, d)), _const((N_GROUPS, GBLK, GBLK)),
                  _const((GBLK, N_GROUPS)), _const((d, d))],
        out_specs=[pl.BlockSpec((ts, d), lambda i: (i, 0))] * 2,
        out_shape=[jax.ShapeDtypeStruct((s, d), BF16), jax.ShapeDtypeStruct((s, d), F32)],
        scratch_shapes=[pltpu.VMEM((ts, d), F32)],
        args=(z, z, ln_g, ln_b, ws, bst, w_out), comm=comm)


def _pool_denoms(tile_idx, ts):
    t1 = (tile_idx * ts + 1 + lax.broadcasted_iota(jnp.int32, (ts, 1), 0)).astype(F32)
    return [jnp.minimum(t1, float(w)) for w in POOL_WINDOWS]


def _pool_diff(p, ext, denoms, k):
    cols = slice(k * POOL_GROUP, (k + 1) * POOL_GROUP)
    acc = ext[:, cols]
    step = 1
    while step < POOL_WINDOWS[k]:
        acc = acc + pltpu.roll(acc, step, 0)
        step *= 2
    return acc[POOL_HALO:] / denoms[k] - p[:, cols]


def _mix_c_fwd(z, w_pool, scale, *, ts, name, comm=None):
    s = z.shape[0]
    d = D_MODEL
    ts = _tile(ts, s)

    def body(zp_ref, w_ref, sc_ref, d_ref, y_ref, carry):
        i = pl.program_id(0)

        @pl.when(i == 0)
        def _():
            carry[...] = jnp.zeros_like(carry)
        p = zp_ref[...].astype(F32)
        ext = jnp.concatenate([carry[...], p], axis=0)
        carry[...] = p[ts - POOL_HALO:]
        denoms = _pool_denoms(i, ts)
        for k in range(len(POOL_WINDOWS)):
            cols = slice(k * POOL_GROUP, (k + 1) * POOL_GROUP)
            dk = _pool_diff(p, ext, denoms, k).astype(BF16)
            d_ref[:, cols] = dk
            y_ref[:, cols] = _dot(dk, w_ref[k]) * sc_ref[:, cols]

    return _run(
        body, name=name, grid=(s // ts,),
        in_specs=[pl.BlockSpec((ts, d), lambda i: (i, 5)), _const((4, POOL_GROUP, POOL_GROUP)), _const((1, d))],
        out_specs=[pl.BlockSpec((ts, d), lambda i: (i, 0))] * 2,
        out_shape=[jax.ShapeDtypeStruct((s, d), BF16), jax.ShapeDtypeStruct((s, d), F32)],
        scratch_shapes=[pltpu.VMEM((POOL_HALO, d), F32)],
        args=(z, w_pool, scale), comm=comm)


def _mix_o_fwd(x, z, ya, yb, yc, w_o, gt, ln_g, ln_b, *, ts, name, comm=None):
    s, d = x.shape
    ts = _tile(ts, s)

    def body(x_ref, ga_ref, gb_ref, gc_ref, ya_ref, yb_ref, yc_ref, w_ref, gt_ref, g_ref, b_ref,
             m_ref, o_ref, x1_ref):
        merged = (_sigmoid(ga_ref[...].astype(F32)) * ya_ref[...] + _sigmoid(gb_ref[...].astype(F32)) * yb_ref[...]
                  + _sigmoid(gc_ref[...].astype(F32)) * yc_ref[...]).astype(BF16)
        m_ref[...] = merged
        o = _dot(merged, w_ref[...])
        o_ref[...] = o
        xhat, _ = _ln_fwd(ALPHA * x_ref[...] + gt_ref[...] * o)
        x1_ref[...] = xhat * g_ref[...] + b_ref[...]

    row = pl.BlockSpec((ts, d), lambda i: (i, 0))
    zspec = lambda k: pl.BlockSpec((ts, d), lambda i, k=k: (i, k))
    return _run(
        body, name=name, grid=(s // ts,),
        in_specs=[row, zspec(6), zspec(7), zspec(8), row, row, row, _const((d, d)),
                  _const((1, d)), _const((1, d)), _const((1, d))],
        out_specs=[row] * 3,
        out_shape=[jax.ShapeDtypeStruct((s, d), BF16), jax.ShapeDtypeStruct((s, d), F32),
                   jax.ShapeDtypeStruct((s, d), F32)],
        args=(x, z, z, z, ya, yb, yc, w_o, gt, ln_g, ln_b), comm=comm)


def _ffn_fwd(up, x1, cw, cb, w_down, gt, ln_g, ln_b, *, ts, name, comm=None):
    s, d = x1.shape
    ts = _tile(ts, s)

    def body(up_ref, x1_ref, cw_ref, cb_ref, w_ref, gt_ref, g_ref, b_ref, ft_ref, dn_ref, x2_ref, carry, f_ref):
        @pl.when(pl.program_id(0) == 0)
        def _():
            carry[...] = jnp.zeros_like(carry)
        for c in range(D_FF // FF_CHUNK):
            ca = slice(c * FF_CHUNK, (c + 1) * FF_CHUNK)
            cg = slice(D_FF + c * FF_CHUNK, D_FF + (c + 1) * FF_CHUNK)
            ua = up_ref[:, ca].astype(F32)
            ext = jnp.concatenate([carry[:, ca], ua], axis=0)
            carry[:, ca] = ua[ts - CONV_HALO:]
            cf = _conv3(ua, ext, cw_ref[:, ca]) + cb_ref[:, ca]
            f = _gelu(cf) * up_ref[:, cg].astype(F32)
            f_ref[:, ca] = f.astype(BF16)
            ft_ref[ca, :] = f.T.astype(BF16)
        dn = _dot(f_ref[...], w_ref[...])
        dn_ref[...] = dn
        xhat, _ = _ln_fwd(ALPHA * x1_ref[...] + gt_ref[...] * dn)
        x2_ref[...] = xhat * g_ref[...] + b_ref[...]

    row = pl.BlockSpec((ts, d), lambda i: (i, 0))
    return _run(
        body, name=name, grid=(s // ts,),
        in_specs=[pl.BlockSpec((ts, 2 * D_FF), lambda i: (i, 0)), row, _const((3, D_FF)), _const((1, D_FF)),
                  _resident((D_FF, d)), _const((1, d)), _const((1, d)), _const((1, d))],
        out_specs=[pl.BlockSpec((D_FF, ts), lambda i: (0, i)), row, row],
        out_shape=[jax.ShapeDtypeStruct((D_FF, s), BF16), jax.ShapeDtypeStruct((s, d), F32),
                   jax.ShapeDtypeStruct((s, d), F32)],
        scratch_shapes=[pltpu.VMEM((CONV_HALO, D_FF), F32), pltpu.VMEM((ts, D_FF), BF16)],
        args=(up, x1, cw, cb, w_down, gt, ln_g, ln_b), comm=comm)


def _loss_fwd(y, tgt, *, ts, name):
    s, d = y.shape
    ts = _tile(ts, s)

    def body(y_ref, t_ref, dy_ref, l_ref):
        @pl.when(pl.program_id(0) == 0)
        def _():
            l_ref[...] = jnp.zeros_like(l_ref)
        e = y_ref[...] - t_ref[...]
        dy_ref[...] = e / float(d)
        l_ref[...] += 0.5 * jnp.sum(jnp.mean(e * e, axis=-1, keepdims=True), axis=0, keepdims=True)

    row = pl.BlockSpec((ts, d), lambda i: (i, 0))
    return _run(body, name=name, grid=(s // ts,), in_specs=[row, row], out_specs=[row, _const((8, 128))],
                out_shape=[jax.ShapeDtypeStruct((s, d), F32), jax.ShapeDtypeStruct((8, 128), F32)], args=(y, tgt))


def _rev(n_tiles):
    return lambda i: n_tiles - 1 - i


def _halo_spec(ts, n_tiles, halo, width, col):
    per = ts // halo
    return pl.BlockSpec((halo, width), lambda i: (jnp.maximum((n_tiles - 1 - i) * per - 1, 0), col))


def _ffn_bwd(dx2, x1, dn, up, cw, cb, w_down, w_up4, gt, ln_g, sc, *, ts, name, comm=None):
    s, d = x1.shape
    ts = _tile(ts, s)
    nt = s // ts
    rev = _rev(nt)
    wd = w_up4.shape[2]

    def w_up_cols(wu_ref, start):
        return wu_ref[start // wd, :, start % wd:start % wd + FF_CHUNK]

    def body(dx2_ref, x1_ref, dn_ref, up_ref, halo_ref, cw_ref, cb_ref, wd_ref, wu_ref, gt_ref, g_ref, sc_ref,
             ddn_ref, dup_ref, dx1_ref, redd_ref, redf_ref, dbup_ref, carry):
        i = pl.program_id(0)

        @pl.when(i == 0)
        def _():
            carry[...] = jnp.zeros_like(carry)
            redd_ref[...] = jnp.zeros_like(redd_ref)
            redf_ref[...] = jnp.zeros_like(redf_ref)
            dbup_ref[...] = jnp.zeros_like(dbup_ref)
        first_tile = i == nt - 1
        x1v, dnv, dyv = x1_ref[...], dn_ref[...], dx2_ref[...]
        xhat, rstd = _ln_fwd(ALPHA * x1v + gt_ref[...] * dnv)
        dr = _ln_bwd(dyv, g_ref[...], xhat, rstd)
        redd_ref[0:1, :] += _colsum(dyv * xhat)
        redd_ref[1:2, :] += _colsum(dyv)
        redd_ref[2:3, :] += _colsum(dr * dnv)
        ddn = (gt_ref[...] * dr).astype(BF16)
        ddn_ref[...] = ddn
        dh = jnp.zeros((ts, d), F32)
        for c in range(D_FF // FF_CHUNK):
            ca = slice(c * FF_CHUNK, (c + 1) * FF_CHUNK)
            cg = slice(D_FF + c * FF_CHUNK, D_FF + (c + 1) * FF_CHUNK)
            df = _dot_nt(ddn, wd_ref[ca, :])
            ua, ug = up_ref[:, ca].astype(F32), up_ref[:, cg].astype(F32)
            halo = jnp.where(first_tile, 0.0, halo_ref[:, ca].astype(F32)[HALO_ROWS - CONV_HALO:])
            ext = jnp.concatenate([halo, ua], axis=0)
            u1, u2 = _rows_before(ext, 1, CONV_HALO), _rows_before(ext, 2, CONV_HALO)
            cwc = cw_ref[:, ca]
            gl, dgl = _gelu_and_grad(cwc[2:3] * ua + cwc[1:2] * u1 + cwc[0:1] * u2 + cb_ref[:, ca])
            dug = df * gl
            dcf = df * ug * dgl
            redf_ref[0:1, ca] += _colsum(dcf * u2)
            redf_ref[1:2, ca] += _colsum(dcf * u1)
            redf_ref[2:3, ca] += _colsum(dcf * ua)
            redf_ref[3:4, ca] += _colsum(dcf)
            extd = jnp.concatenate([dcf, carry[:, ca]], axis=0)
            carry[:, ca] = dcf[:CONV_HALO]
            dua = cwc[2:3] * dcf + cwc[1:2] * _rows_after(extd, 1, ts) + cwc[0:1] * _rows_after(extd, 2, ts)
            dbup_ref[0:1, ca] += _colsum(dua)
            dbup_ref[0:1, cg] += _colsum(dug)
            dua_b, dug_b = dua.astype(BF16), dug.astype(BF16)
            dup_ref[:, ca] = dua_b
            dup_ref[:, cg] = dug_b
            dh = dh + _dot_nt(dua_b, w_up_cols(wu_ref, c * FF_CHUNK)) + _dot_nt(dug_b, w_up_cols(wu_ref, D_FF + c * FF_CHUNK))
        dx1_ref[...] = ALPHA * dr + dh * (1.0 + sc_ref[...])
        redd_ref[3:4, :] += _colsum(dh * x1v)
        redd_ref[4:5, :] += _colsum(dh)

    row = pl.BlockSpec((ts, d), lambda i: (rev(i), 0))
    return _run(
        body, name=name, grid=(nt,),
        in_specs=[row, row, row, pl.BlockSpec((ts, 2 * D_FF), lambda i: (rev(i), 0)),
                  _halo_spec(ts, nt, HALO_ROWS, D_FF, 0), _const((3, D_FF)), _const((1, D_FF)),
                  _resident((D_FF, d)), _resident((N_CHIPS, d, wd)), _const((1, d)), _const((1, d)), _const((1, d))],
        out_specs=[row, pl.BlockSpec((ts, 2 * D_FF), lambda i: (rev(i), 0)), row,
                   _const((8, d)), _const((8, D_FF)), _const((8, 2 * D_FF))],
        out_shape=[jax.ShapeDtypeStruct((s, d), BF16), jax.ShapeDtypeStruct((s, 2 * D_FF), BF16),
                   jax.ShapeDtypeStruct((s, d), F32), jax.ShapeDtypeStruct((8, d), F32),
                   jax.ShapeDtypeStruct((8, D_FF), F32), jax.ShapeDtypeStruct((8, 2 * D_FF), F32)],
        scratch_shapes=[pltpu.VMEM((CONV_HALO, D_FF), F32)],
        args=(dx2, x1, dn, up, up, cw, cb, w_down, w_up4, gt, ln_g, sc), comm=comm)


def _accumulate_dw(dw_ref, dwb_ref, xa, dy, first, last):
    @pl.when(first)
    def _():
        dw_ref[...] = jnp.zeros_like(dw_ref)
    dw_ref[...] += _dot_tn(xa, dy)

    @pl.when(last)
    def _():
        dwb_ref[...] = dw_ref[...].astype(BF16)


def _dw_out(d):
    return [_const((d, d))] * 2, [jax.ShapeDtypeStruct((d, d), F32), jax.ShapeDtypeStruct((d, d), BF16)]


def _grad_matmul_t(xt, dy, *, tk, tn, name, by_chip=False, comm=None):
    k, s = xt.shape
    n = dy.shape[1]

    def body(xt_ref, dy_ref, o_ref, ob_ref):
        o = _dot(xt_ref[...], dy_ref[...]).reshape(o_ref.shape)
        o_ref[...] = o
        ob_ref[...] = o.astype(BF16)

    if by_chip:
        assert tk == k
        per = n // N_CHIPS // tn
        ospec = pl.BlockSpec((1, k, tn), lambda j, i: (j // per, 0, j % per))
        shape = (N_CHIPS, k, n // N_CHIPS)
    else:
        ospec = pl.BlockSpec((tk, tn), lambda j, i: (i, j))
        shape = (k, n)
    xspec = _resident((k, s)) if tk == k else pl.BlockSpec((tk, s), lambda j, i: (i, 0))
    dspec = _resident((s, n)) if tn == n else pl.BlockSpec((s, tn), lambda j, i: (0, j))
    return _run(body, name=name, grid=(n // tn, k // tk), in_specs=[xspec, dspec], out_specs=[ospec, ospec],
                out_shape=[jax.ShapeDtypeStruct(shape, F32), jax.ShapeDtypeStruct(shape, BF16)], args=(xt, dy), comm=comm)


def _mix_o_bwd(dx1, x, o, z, ya, yb, yc, merged, w_o, gt, ln_g, *, ts, name, comm=None):
    s, d = x.shape
    ts = _tile(ts, s)
    nt = s // ts

    def body(dx1_ref, x_ref, o_ref, ga_ref, gb_ref, gc_ref, ya_ref, yb_ref, yc_ref, m_ref, w_ref, gt_ref, g_ref,
             dxa_ref, dzg_ref, dya_ref, dyb_ref, dyc_ref, red_ref, dw_ref, dwb_ref):
        i = pl.program_id(0)

        @pl.when(i == 0)
        def _():
            red_ref[...] = jnp.zeros_like(red_ref)
        dyv, ov = dx1_ref[...], o_ref[...]
        xhat, rstd = _ln_fwd(ALPHA * x_ref[...] + gt_ref[...] * ov)
        dr = _ln_bwd(dyv, g_ref[...], xhat, rstd)
        red_ref[0:1, :] += _colsum(dyv * xhat)
        red_ref[1:2, :] += _colsum(dyv)
        red_ref[2:3, :] += _colsum(dr * ov)
        dxa_ref[...] = ALPHA * dr
        d_o = (gt_ref[...] * dr).astype(BF16)
        _accumulate_dw(dw_ref, dwb_ref, m_ref[...], d_o, i == 0, i == nt - 1)
        dm = _dot_nt(d_o, w_ref[...])
        for k, (zg_ref, y_ref, dy_ref) in enumerate(((ga_ref, ya_ref, dya_ref), (gb_ref, yb_ref, dyb_ref),
                                                     (gc_ref, yc_ref, dyc_ref))):
            g = _sigmoid(zg_ref[...].astype(F32))
            dzg_ref[:, k * d:(k + 1) * d] = (dm * y_ref[...] * g * (1.0 - g)).astype(BF16)
            dy_ref[...] = (dm * g).astype(BF16)

    row = pl.BlockSpec((ts, d), lambda i: (i, 0))
    zspec = lambda k: pl.BlockSpec((ts, d), lambda i, k=k: (i, k))
    bf = jax.ShapeDtypeStruct((s, d), BF16)
    dw_specs, dw_shapes = _dw_out(d)
    return _run(
        body, name=name, grid=(nt,),
        in_specs=[row, row, row, zspec(6), zspec(7), zspec(8), row, row, row, row, _const((d, d)),
                  _const((1, d)), _const((1, d))],
        out_specs=[row, pl.BlockSpec((ts, 3 * d), lambda i: (i, 2)), row, row, row, _const((8, d))] + dw_specs,
        out_shape=[jax.ShapeDtypeStruct((s, d), F32), jax.ShapeDtypeStruct((s, D_Z), BF16), bf, bf, bf,
                   jax.ShapeDtypeStruct((8, d), F32)] + dw_shapes,
        args=(dx1, x, o, z, z, z, ya, yb, yc, merged, w_o, gt, ln_g), comm=comm)


def _mix_a_bwd(dya, a, z, dz, cw, w_out, *, ts, name, comm=None):
    s = z.shape[0]
    d = D_MODEL
    ts = _tile(ts, s)
    nt = s // ts
    rev = _rev(nt)

    def body(dya_ref, a_ref, zb_ref, zc_ref, zx_ref, hc_ref, hx_ref, cw_ref, w_ref, dz_in, dz_ref, red_ref,
             dw_ref, dwb_ref, carry):
        i = pl.program_id(0)

        @pl.when(i == 0)
        def _():
            carry[...] = jnp.zeros_like(carry)
            red_ref[...] = jnp.zeros_like(red_ref)
        _accumulate_dw(dw_ref, dwb_ref, a_ref[...], dya_ref[...], i == 0, i == nt - 1)
        zb, zc, zx = zb_ref[...].astype(F32), zc_ref[...].astype(F32), zx_ref[...].astype(F32)
        q = zc * zx
        halo = jnp.where(i == nt - 1, 0.0, (hc_ref[...].astype(F32) * hx_ref[...].astype(F32))[HALO_ROWS - CONV_HALO:])
        ext = jnp.concatenate([halo, q], axis=0)
        q1, q2 = _rows_before(ext, 1, CONV_HALO), _rows_before(ext, 2, CONV_HALO)
        cwv = cw_ref[...]
        cv = cwv[2:3] * q + cwv[1:2] * q1 + cwv[0:1] * q2
        da = _dot_nt(dya_ref[...], w_ref[...])
        dcv = da * zb
        red_ref[0:1, :] += _colsum(dcv * q2)
        red_ref[1:2, :] += _colsum(dcv * q1)
        red_ref[2:3, :] += _colsum(dcv * q)
        extd = jnp.concatenate([dcv, carry[...]], axis=0)
        carry[...] = dcv[:CONV_HALO]
        dq = cwv[2:3] * dcv + cwv[1:2] * _rows_after(extd, 1, ts) + cwv[0:1] * _rows_after(extd, 2, ts)
        dz_ref[:, 0:d] = (da * cv).astype(BF16)
        dz_ref[:, d:2 * d] = (dq * zx).astype(BF16)
        dz_ref[:, 2 * d:3 * d] = (dq * zc).astype(BF16)

    zspec = lambda k: pl.BlockSpec((ts, d), lambda i, k=k: (rev(i), k))
    row = pl.BlockSpec((ts, d), lambda i: (rev(i), 0))
    dw_specs, dw_shapes = _dw_out(d)
    return _run(
        body, name=name, grid=(nt,),
        in_specs=[row, row, zspec(0), zspec(1), zspec(2),
                  _halo_spec(ts, nt, HALO_ROWS, d, 1), _halo_spec(ts, nt, HALO_ROWS, d, 2),
                  _const((3, d)), _const((d, d)), HBM],
        out_specs=[pl.BlockSpec((ts, 3 * d), lambda i: (rev(i), 0)), _const((8, d))] + dw_specs,
        out_shape=[jax.ShapeDtypeStruct((s, D_Z), BF16), jax.ShapeDtypeStruct((8, d), F32)] + dw_shapes,
        scratch_shapes=[pltpu.VMEM((CONV_HALO, d), F32)],
        args=(dya, a, z, z, z, z, z, cw, w_out, dz), aliases={9: 0}, comm=comm)


def _mix_b_bwd(dyb, sg, z, dz, ln_g, ln_b, ws, bst, w_out, *, ts, name, comm=None):
    s = z.shape[0]
    d = D_MODEL
    ts = _tile(ts, s)
    nt = s // ts
    nblk = ts // GBLK

    def body(dyb_ref, sg_ref, zu_ref, zv_ref, g_ref, b_ref, ws_ref, bst_ref, w_ref, dz_in,
             dz_ref, red_ref, dws_ref, dbst_ref, dw_ref, dwb_ref, mixed_scr, dvn_scr, dzv_scr):
        first = (pl.program_id(0) == 0) & (pl.program_id(1) == 0)

        @pl.when(first)
        def _():
            red_ref[...] = jnp.zeros_like(red_ref)
            dws_ref[...] = jnp.zeros_like(dws_ref)
            dbst_ref[...] = jnp.zeros_like(dbst_ref)

        @pl.when(pl.program_id(1) == 0)
        def _():
            _accumulate_dw(dw_ref, dwb_ref, sg_ref[...], dyb_ref[...], first, pl.program_id(0) == nt - 1)
            u, du_dz = _gelu_and_grad(zu_ref[...].astype(F32))
            vg, dv_dz = _gelu_and_grad(zv_ref[...].astype(F32))
            xhat, rstd = _ln_fwd(vg)
            vn = (xhat * g_ref[...] + b_ref[...]).astype(BF16)
            _spatial_mix(vn, ws_ref, bst_ref, mixed_scr, ts)
            dsg = _dot_nt(dyb_ref[...], w_ref[...])
            dz_ref[...] = (dsg * mixed_scr[...] * du_dz).astype(BF16)
            dmix = dsg * u
            mask = _spatial_mask()
            for g in range(N_GROUPS):
                cols = slice(g * GBLK, (g + 1) * GBLK)
                wm = jnp.where(mask, ws_ref[g], 0.0).astype(BF16)
                dm_cat = jnp.concatenate([dmix[n * GBLK:(n + 1) * GBLK, cols] for n in range(nblk)], axis=1)
                vn_cat = jnp.concatenate([vn[n * GBLK:(n + 1) * GBLK, cols] for n in range(nblk)], axis=1)
                dm_b = dm_cat.astype(BF16)
                dbst_ref[:, g:g + 1] += jnp.sum(dm_cat, axis=1, keepdims=True)
                dws_ref[g] += jnp.where(mask, _dot_nt(dm_b, vn_cat), 0.0)
                dvn_cat = _dot_tn(wm, dm_b)
                for n in range(nblk):
                    dvn_scr[n * GBLK:(n + 1) * GBLK, cols] = dvn_cat[:, n * GBLK:(n + 1) * GBLK]
            dvn = dvn_scr[...]
            red_ref[0:1, :] += _colsum(dvn * xhat)
            red_ref[1:2, :] += _colsum(dvn)
            dzv_scr[...] = (_ln_bwd(dvn, g_ref[...], xhat, rstd) * dv_dz).astype(BF16)

        @pl.when(pl.program_id(1) == 1)
        def _():
            dz_ref[...] = dzv_scr[...]

    zspec = lambda k: pl.BlockSpec((ts, d), lambda i, h, k=k: (i, k))
    row = pl.BlockSpec((ts, d), lambda i, h: (i, 0))
    dw_specs, dw_shapes = _dw_out(d)
    return _run(
        body, name=name, grid=(nt, 2),
        in_specs=[row, row, zspec(3), zspec(4), _const((1, d)), _const((1, d)),
                  _const((N_GROUPS, GBLK, GBLK)), _const((GBLK, N_GROUPS)), _const((d, d)), HBM],
        out_specs=[pl.BlockSpec((ts, d), lambda i, h: (i, 3 + h)), _const((8, d)),
                   _const((N_GROUPS, GBLK, GBLK)), _const((GBLK, N_GROUPS))] + dw_specs,
        out_shape=[jax.ShapeDtypeStruct((s, D_Z), BF16), jax.ShapeDtypeStruct((8, d), F32),
                   jax.ShapeDtypeStruct((N_GROUPS, GBLK, GBLK), F32), jax.ShapeDtypeStruct((GBLK, N_GROUPS), F32)]
        + dw_shapes,
        scratch_shapes=[pltpu.VMEM((ts, d), F32), pltpu.VMEM((ts, d), F32), pltpu.VMEM((ts, d), BF16)],
        args=(dyb, sg, z, z, ln_g, ln_b, ws, bst, w_out, dz), aliases={9: 0}, comm=comm)


def _mix_c_bwd(dyc, z, dz, w_pool, scale, *, ts, name):
    s = z.shape[0]
    d = D_MODEL
    ts = _tile(ts, s)
    nt = s // ts
    rev = _rev(nt)

    def body(dyc_ref, zp_ref, halo_ref, w_ref, sc_ref, dz_in, dz_ref, red_ref, dw_ref, carry):
        i = pl.program_id(0)

        @pl.when(i == 0)
        def _():
            carry[...] = jnp.zeros_like(carry)
            red_ref[...] = jnp.zeros_like(red_ref)
            dw_ref[...] = jnp.zeros_like(dw_ref)
        p = zp_ref[...].astype(F32)
        ext = jnp.concatenate([jnp.where(i == nt - 1, 0.0, halo_ref[...].astype(F32)), p], axis=0)
        denoms = _pool_denoms(rev(i), ts)
        dyv = dyc_ref[...].astype(F32)
        for k in range(len(POOL_WINDOWS)):
            cols = slice(k * POOL_GROUP, (k + 1) * POOL_GROUP)
            dk = _pool_diff(p, ext, denoms, k).astype(BF16)
            red_ref[0:1, cols] += _colsum(dyv[:, cols] * _dot(dk, w_ref[k]))
            dpre = (dyv[:, cols] * sc_ref[:, cols]).astype(BF16)
            dw_ref[k] += _dot_tn(dk, dpre)
            dd = _dot_nt(dpre, w_ref[k])
            e = dd / denoms[k]
            acc = jnp.concatenate([e, carry[:, cols]], axis=0)
            carry[:, cols] = e[:POOL_HALO]
            step = 1
            while step < POOL_WINDOWS[k]:
                acc = acc + pltpu.roll(acc, acc.shape[0] - step, 0)
                step *= 2
            dz_ref[:, cols] = (acc[:ts] - dd).astype(BF16)

    return _run(
        body, name=name, grid=(nt,),
        in_specs=[pl.BlockSpec((ts, d), lambda i: (rev(i), 0)), pl.BlockSpec((ts, d), lambda i: (rev(i), 5)),
                  _halo_spec(ts, nt, POOL_HALO, d, 5), _const((4, POOL_GROUP, POOL_GROUP)), _const((1, d)), HBM],
        out_specs=[pl.BlockSpec((ts, d), lambda i: (rev(i), 5)), _const((8, d)), _const((4, POOL_GROUP, POOL_GROUP))],
        out_shape=[jax.ShapeDtypeStruct((s, D_Z), BF16), jax.ShapeDtypeStruct((8, d), F32),
                   jax.ShapeDtypeStruct((4, POOL_GROUP, POOL_GROUP), F32)],
        scratch_shapes=[pltpu.VMEM((POOL_HALO, d), F32)],
        args=(dyc, z, z, w_pool, scale, dz), aliases={5: 0})


def _in_proj_bwd(dz, w4, dxa, x, sc, *, ts, name, comm=None):
    s, d = x.shape
    ts = _tile(ts, s)
    wd = w4.shape[2]

    def body(dz_ref, w_ref, dxa_ref, x_ref, sc_ref, dx_ref, red_ref, db_ref):
        @pl.when(pl.program_id(0) == 0)
        def _():
            red_ref[...] = jnp.zeros_like(red_ref)
            db_ref[...] = jnp.zeros_like(db_ref)
        dh = jnp.zeros((ts, d), F32)
        for j in range(N_CHIPS):
            dzj = dz_ref[:, j * wd:(j + 1) * wd]
            db_ref[0:1, j * wd:(j + 1) * wd] += _colsum(dzj.astype(F32))
            dh = dh + _dot_nt(dzj, w_ref[j])
        dx_ref[...] = dxa_ref[...] + dh * (1.0 + sc_ref[...])
        red_ref[0:1, :] += _colsum(dh * x_ref[...])
        red_ref[1:2, :] += _colsum(dh)

    row = pl.BlockSpec((ts, d), lambda i: (i, 0))
    return _run(
        body, name=name, grid=(s // ts,),
        in_specs=[pl.BlockSpec((ts, D_Z), lambda i: (i, 0)), _resident((N_CHIPS, d, wd)), row, row, _const((1, d))],
        out_specs=[row, _const((8, d)), _const((8, D_Z))],
        out_shape=[jax.ShapeDtypeStruct((s, d), F32), jax.ShapeDtypeStruct((8, d), F32),
                   jax.ShapeDtypeStruct((8, D_Z), F32)],
        args=(dz, w4, dxa, x, sc), comm=comm)


def _ada_fwd(c_all, w_ada, b_ada, *, name):
    nl, d, n = w_ada.shape
    tn = n // 2

    def body(c_ref, w_ref, b_ref, o_ref):
        cv = c_ref[...]
        ca = (cv * _sigmoid(cv)).astype(BF16)
        o_ref[0] = _dot(ca, w_ref[0].astype(BF16)) + b_ref[0]

    return _run(
        body, name=name, grid=(nl, n // tn),
        in_specs=[_const((N_DEV, d)), pl.BlockSpec((1, d, tn), lambda l, j: (l, 0, j)),
                  pl.BlockSpec((1, 1, tn), lambda l, j: (l, 0, j))],
        out_specs=[pl.BlockSpec((1, N_DEV, tn), lambda l, j: (l, 0, j))],
        out_shape=[jax.ShapeDtypeStruct((nl, N_DEV, n), F32)], args=(c_all, w_ada, b_ada))[0]


def _ada_bwd(c_all, dada, *, name):
    nl, nb, n = dada.shape
    d = c_all.shape[1]
    tn = n // 2

    def body(c_ref, g_ref, o_ref):
        cv = c_ref[...]
        ca = (cv * _sigmoid(cv)).astype(BF16)
        o_ref[0] = _dot_tn(ca, g_ref[0].astype(BF16))

    return _run(
        body, name=name, grid=(nl, n // tn),
        in_specs=[_const((nb, d)), pl.BlockSpec((1, nb, tn), lambda l, j: (l, 0, j))],
        out_specs=[pl.BlockSpec((1, d, tn), lambda l, j: (l, 0, j))],
        out_shape=[jax.ShapeDtypeStruct((nl, d, n), F32)], args=(c_all, dada))[0]


def _sum4_into_half(owns, recvs, core, *, name):
    nw = len(owns)
    r = owns[0].shape[0]
    tr = _row_tile(r, max(o.shape[1] for o in owns), 2)

    def body(core_ref, *refs):
        for own_ref, recv_ref, o_ref in zip(refs[:nw], refs[nw:2 * nw], refs[2 * nw:]):
            acc = own_ref[...]
            for k in range(N_CHIPS - 1):
                acc = acc + recv_ref[k].astype(F32)
            o_ref[0] = acc

    cols = [o.shape[1] for o in owns]
    spec = pltpu.PrefetchScalarGridSpec(
        num_scalar_prefetch=1, grid=(r // tr,),
        in_specs=[pl.BlockSpec((tr, c), lambda i, core_ref: (i, 0)) for c in cols]
        + [pl.BlockSpec((N_CHIPS - 1, tr, c), lambda i, core_ref: (0, i, 0)) for c in cols],
        out_specs=[pl.BlockSpec((1, tr, c), lambda i, core_ref: (core_ref[0], i, 0)) for c in cols])
    return pl.pallas_call(
        body, name=name, grid_spec=spec, out_shape=[jax.ShapeDtypeStruct((2, r, c), F32) for c in cols],
        compiler_params=pltpu.CompilerParams(dimension_semantics=("arbitrary",), vmem_limit_bytes=VMEM_LIMIT),
    )(core, *owns, *recvs)


def _cast_into_slots(shards, layer, chip, *, name):
    quarters = 4

    def body(chip_ref, *refs):
        ins, outs = refs[:len(shards)], refs[len(shards):]
        for i_ref, o_ref in zip(ins, outs):
            o_ref[0, 0] = i_ref[0].astype(BF16)

    in_specs, out_specs, out_shape = [], [], []
    for sh in shards:
        _, r, c = sh.shape
        in_specs.append(pl.BlockSpec((1, r // quarters, c), lambda t, chip_ref: (layer, t, 0)))
        out_specs.append(pl.BlockSpec((1, 1, r // quarters, c), lambda t, chip_ref: (chip_ref[0], t // 2, t % 2, 0)))
        out_shape.append(jax.ShapeDtypeStruct((N_CHIPS, 2, r // 2, c), BF16))
    spec = pltpu.PrefetchScalarGridSpec(num_scalar_prefetch=1, grid=(quarters,), in_specs=in_specs, out_specs=out_specs)
    return pl.pallas_call(
        body, name=name, grid_spec=spec, out_shape=out_shape,
        compiler_params=pltpu.CompilerParams(dimension_semantics=("arbitrary",), vmem_limit_bytes=VMEM_LIMIT),
    )(chip, *shards)


def _sum_halves(g_f32s, theirs, place, *, name, comm=None):
    nw = len(g_f32s)
    rh = g_f32s[0].shape[2]
    cols = [g.shape[3] for g in g_f32s]
    tr = _row_tile(rh, max(cols), 2)

    def body(place_ref, *refs):
        for g_ref, t_ref, hb_ref, own_ref in zip(refs[:nw], refs[nw:2 * nw], refs[2 * nw:3 * nw], refs[3 * nw:]):
            h = g_ref[0, 0] + t_ref[0].astype(F32)
            hb_ref[0] = h.astype(BF16)

            @pl.when(pl.program_id(1) == place_ref[1])
            def _():
                own_ref[...] = h

    return _run(
        body, name=name, grid=(rh // tr, N_CHIPS), prefetch=place,
        in_specs=[pl.BlockSpec((1, 1, tr, c), lambda i, j, place_ref: (j, place_ref[0], i, 0)) for c in cols]
        + [pl.BlockSpec((1, tr, c), lambda i, j, place_ref: (j, i, 0)) for c in cols],
        out_specs=[pl.BlockSpec((1, tr, c), lambda i, j, place_ref: (j, i, 0)) for c in cols]
        + [pl.BlockSpec((tr, c), lambda i, j, place_ref: (i, 0)) for c in cols],
        out_shape=[jax.ShapeDtypeStruct((N_CHIPS, rh, c), BF16) for c in cols]
        + [jax.ShapeDtypeStruct((rh, c), F32) for c in cols],
        args=(*g_f32s, *theirs), comm=comm)


def _row_tile(r, c, mib):
    limit = max(8, (mib << 20) // (4 * c))
    if r <= limit:
        return r
    best = 8
    for t in range(8, limit + 1, 8):
        if r % t == 0:
            best = t
    return best


def _adam_math(w, g, m, v):
    mn = ADAM_B1 * m + (1.0 - ADAM_B1) * g
    vn = ADAM_B2 * v + (1.0 - ADAM_B2) * (g * g)
    m_hat = mn / (1.0 - ADAM_B1 ** ADAM_STEP)
    v_hat = vn / (1.0 - ADAM_B2 ** ADAM_STEP)
    return -ADAM_LR * (m_hat / (jnp.sqrt(v_hat) + ADAM_EPS) + ADAM_WD * w), mn, vn


def _adamw(w, g, m, v, *, name):
    r, c = w.shape
    tr = _row_tile(r, c, 2)

    def body(w_ref, g_ref, m_ref, v_ref, d_ref, mo_ref, vo_ref):
        d_ref[...], mo_ref[...], vo_ref[...] = _adam_math(w_ref[...], g_ref[...], m_ref[...], v_ref[...])

    blk = pl.BlockSpec((tr, c), lambda i: (i, 0))
    return _run(body, name=name, grid=(r // tr,), in_specs=[blk] * 4, out_specs=[blk] * 3,
                out_shape=[jax.ShapeDtypeStruct((r, c), F32)] * 3, args=(w, g, m, v))


def _adamw_sharded(w, m, v, grads, *, name, comm=None):
    nl, r, c = w.shape
    tr = _row_tile(r, c, 2)
    nt = r // tr

    def body(w_ref, m_ref, v_ref, g0_ref, g1_ref, g_ref, d_ref, mo_ref, vo_ref):
        g = jnp.where(pl.program_id(0) == 0, g0_ref[...], g1_ref[...])
        g_ref[0] = g
        d_ref[0], mo_ref[0], vo_ref[0] = _adam_math(w_ref[0], g, m_ref[0], v_ref[0])

    blk = pl.BlockSpec((1, tr, c), lambda l, i: (l, i, 0))
    part0 = pl.BlockSpec((tr, c), lambda l, i: (jnp.where(l == 0, i, nt - 1), 0))
    part1 = pl.BlockSpec((tr, c), lambda l, i: (jnp.where(l == 1, i, 0), 0))
    return _run(body, name=name, grid=(nl, nt), in_specs=[blk] * 3 + [part0, part1],
                out_specs=[blk] * 4, out_shape=[jax.ShapeDtypeStruct((nl, r, c), F32)] * 4,
                args=(w, m, v, grads[0], grads[1]), comm=comm)


_BIG = ("w_in", "w_a_out", "w_b_out", "w_pool", "w_o", "w_up", "w_down")
_COL_SHARDED = ("w_in", "w_up")
_SMALL_SHARDED = ("conv_a", "conv_ffn")
_SMALL_REPL = ("b_in", "ln_v_g", "ln_v_b", "w_spatial", "b_spatial", "pool_scale", "ln1_g", "ln1_b", "b_up",
               "conv_ffn_b", "ln2_g", "ln2_b")
_WEIGHTS = ("w_ada", "b_ada", "w_in", "b_in", "conv_a", "w_a_out", "ln_v_g", "ln_v_b", "w_spatial", "b_spatial",
            "w_b_out", "w_pool", "pool_scale", "w_o", "ln1_g", "ln1_b", "w_up", "b_up", "conv_ffn", "conv_ffn_b",
            "w_down", "ln2_g", "ln2_b")


def _shard3(a):
    return a.reshape(a.shape[0], -1, a.shape[-1])


def _use_gathered(name, g):
    g = g.reshape(N_CHIPS, -1, g.shape[-1])
    if name in _COL_SHARDED:
        return g
    if name == "w_pool":
        return g.reshape(N_CHIPS, 4, POOL_GROUP // N_CHIPS, POOL_GROUP).transpose(1, 0, 2, 3).reshape(
            4, POOL_GROUP, POOL_GROUP)
    return g.reshape(-1, g.shape[-1])


def _grad_by_chip(name, g):
    if name in _COL_SHARDED:
        return g
    if name == "w_pool":
        return g.reshape(4, N_CHIPS, POOL_GROUP // N_CHIPS, POOL_GROUP).transpose(1, 0, 2, 3).reshape(
            N_CHIPS, POOL_GROUP, POOL_GROUP)
    return g.reshape(N_CHIPS, -1, g.shape[-1])


def _pack_small(arrs):
    parts = []
    for a in arrs:
        flat = a.reshape(-1).astype(F32)
        pad = (-flat.shape[0]) % 128
        parts.append(jnp.pad(flat, (0, pad)) if pad else flat)
    flat = jnp.concatenate(parts)
    pad = (-flat.shape[0]) % 2048
    if pad:
        flat = jnp.pad(flat, (0, pad))
    return flat.reshape(-1, 128)


def _unpack_small(buf, shapes):
    lead = buf.shape[:-2]
    flat = buf.reshape(lead + (-1,))
    out, off = [], 0
    for shp in shapes:
        n = math.prod(shp)
        out.append(flat[..., off:off + n].reshape(lead + tuple(shp)))
        off += n + ((-n) % 128)
    return out


def _as2d(a):
    return a.reshape(-1, a.shape[-1])


_LATE = ("w_a_out", "w_b_out", "w_pool", "w_o")


class _Traffic:
    def __init__(self, slots, plan, core, chip):
        self.slots = slots
        self.plan = plan
        self.core = core
        self.place = jnp.concatenate([core, chip])
        self.gathered = {}
        self.ready = {}
        self.summed = {}
        self.half = {}
        self.final = {}

    def weight(self, layer, name):
        return self.gathered[(layer, name)]

    @staticmethod
    def _same_rows(keys, arrays):
        groups = []
        for k, a in zip(keys, arrays):
            if groups and groups[-1][1][-1].shape[-2] == a.shape[-2]:
                groups[-1][0].append(k)
                groups[-1][1].append(a)
            else:
                groups.append(([k], [a]))
        return groups

    def add_grad(self, layer, name, g_f32, g_bf16):
        def halves(g):
            g = _grad_by_chip(name, g)
            return g.reshape(N_CHIPS, 2, g.shape[1] // 2, g.shape[2])
        self.ready[(layer, name)] = (halves(g_f32), halves(g_bf16))

    def _comm(self, job):
        if job[0] == "gather":
            return _gather_comm([self.slots[(job[1], k)] for k in job[2]], *job[3:])
        if job[0] == "presum":
            return _presum_comm([self.ready[k][1] for k in job[1]])
        if job[0] == "scatter":
            return _scatter_comm([self.summed[k][0] for k in job[1]])
        return _join_comm([self.half[k] for k in job[1]])

    def _done(self, job, res):
        if job[0] == "gather":
            for k, r in zip(job[2], res):
                self.slots[(job[1], k)] = r
                self.gathered[(job[1], k)] = _use_gathered(k, r)
        elif job[0] == "presum":
            for keys, rs in self._same_rows(job[1], res):
                nm = f"presum_l{keys[0][0]}_{keys[0][1] if len(keys) == 1 else 'late'}"
                outs = self.run(nm, lambda cm: _sum_halves([self.ready.pop(k)[0] for k in keys], rs, self.place,
                                                           name=nm, comm=cm))
                for i, k in enumerate(keys):
                    self.summed[k] = (outs[i], outs[len(keys) + i])
        elif job[0] == "scatter":
            for keys, rs in self._same_rows(job[1], res):
                nm = f"sum_l{keys[0][0]}_{keys[0][1] if len(keys) == 1 else 'late'}"
                outs = _sum4_into_half([self.summed.pop(k)[1] for k in keys], rs, self.core, name=nm)
                self.half.update(zip(keys, outs))
        else:
            for k, r in zip(job[1], res):
                self.final[k] = r.reshape(-1, r.shape[-1])

    def run(self, name, fn):
        jobs = self.plan.get(name)
        if not jobs:
            return fn(None)
        comms = [self._comm(j) for j in jobs]
        outs, res = fn(_merge(comms))
        for job, r in zip(jobs, _split(comms, res)):
            self._done(job, r)
        return outs

    def alone(self, name):
        jobs = self.plan[name]
        comms = [self._comm(j) for j in jobs]
        for job, r in zip(jobs, _split(comms, _comm_call(_merge(comms), name=name))):
            self._done(job, r)


def _layer_fwd(x, ada, p, l, tr):
    sh1, sc1, gt1, sh2, sc2, gt2 = ada
    n = f"l{l}"
    z, ht = tr.run(f"{n}_in_proj", lambda cm: _mod_matmul(
        x, sc1, sh1, tr.weight(l, "w_in"), p["b_in"], ts=1024, tn=2304, name=f"{n}_in_proj", comm=cm))
    a, ya = tr.run(f"{n}_mix_a", lambda cm: _mix_a_fwd(z, p["conv_a"], tr.weight(l, "w_a_out"), ts=256,
                                                      name=f"{n}_mix_a", comm=cm))
    sg, yb = tr.run(f"{n}_mix_b", lambda cm: _mix_b_fwd(
        z, p["ln_v_g"], p["ln_v_b"], p["w_spatial"], p["b_spatial_t"], tr.weight(l, "w_b_out"), ts=256,
        name=f"{n}_mix_b", comm=cm))
    dpool, yc = tr.run(f"{n}_mix_c", lambda cm: _mix_c_fwd(z, tr.weight(l, "w_pool"), p["pool_scale"], ts=256,
                                                          name=f"{n}_mix_c", comm=cm))
    merged, o, x1 = tr.run(f"{n}_mix_o", lambda cm: _mix_o_fwd(
        x, z, ya, yb, yc, tr.weight(l, "w_o"), gt1, p["ln1_g"], p["ln1_b"], ts=256, name=f"{n}_mix_o", comm=cm))
    up, h2t = tr.run(f"{n}_up_proj", lambda cm: _mod_matmul(
        x1, sc2, sh2, tr.weight(l, "w_up"), p["b_up"], ts=1024, tn=1408, name=f"{n}_up_proj", comm=cm))
    ft, dn, x2 = tr.run(f"{n}_ffn", lambda cm: _ffn_fwd(
        up, x1, p["conv_ffn"], p["conv_ffn_b"], tr.weight(l, "w_down"), gt2, p["ln2_g"], p["ln2_b"], ts=256,
        name=f"{n}_ffn", comm=cm))
    saved = dict(x=x, z=z, ht=ht, a=a, ya=ya, sg=sg, yb=yb, dpool=dpool, yc=yc, merged=merged, o=o, x1=x1, h2t=h2t,
                 up=up, ft=ft, dn=dn)
    return x2, saved


def _layer_bwd(dx2, ada, p, sv, l, tr):
    sh1, sc1, gt1, sh2, sc2, gt2 = ada
    n = f"l{l}"
    ddn, dup, dx1, red_d, red_f, dbup = tr.run(f"{n}_ffn_bwd", lambda cm: _ffn_bwd(
        dx2, sv["x1"], sv["dn"], sv["up"], p["conv_ffn"], p["conv_ffn_b"], tr.weight(l, "w_down"),
        tr.weight(l, "w_up"), gt2, p["ln2_g"], sc2, ts=256, name=f"{n}_ffn_bwd", comm=cm))
    g = {}
    tr.add_grad(l, "w_down", *_grad_matmul_t(sv["ft"], ddn, tk=D_FF // N_CHIPS, tn=D_MODEL, name=f"{n}_dw_down"))
    tr.add_grad(l, "w_up", *tr.run(f"{n}_dw_up", lambda cm: _grad_matmul_t(
        sv["h2t"], dup, tk=D_MODEL, tn=FF_CHUNK, name=f"{n}_dw_up", by_chip=True, comm=cm)))
    g["ln2_g"], g["ln2_b"] = red_d[0], red_d[1]
    g["conv_ffn"], g["conv_ffn_b"], g["b_up"] = red_f[0:3], red_f[3], dbup[0]

    dxa, dz, dya, dyb, dyc, red_o, dwo, dwo_b = tr.run(f"{n}_mix_o_bwd", lambda cm: _mix_o_bwd(
        dx1, sv["x"], sv["o"], sv["z"], sv["ya"], sv["yb"], sv["yc"], sv["merged"], tr.weight(l, "w_o"), gt1,
        p["ln1_g"], ts=256, name=f"{n}_mix_o_bwd", comm=cm))
    tr.add_grad(l, "w_o", dwo, dwo_b)
    g["ln1_g"], g["ln1_b"] = red_o[0], red_o[1]

    dz, red_a, dwa, dwa_b = tr.run(f"{n}_mix_a_bwd", lambda cm: _mix_a_bwd(
        dya, sv["a"], sv["z"], dz, p["conv_a"], tr.weight(l, "w_a_out"), ts=256, name=f"{n}_mix_a_bwd", comm=cm))
    tr.add_grad(l, "w_a_out", dwa, dwa_b)
    g["conv_a"] = red_a[0:3]

    dz, red_b, dws, dbst, dwb, dwb_b = tr.run(f"{n}_mix_b_bwd", lambda cm: _mix_b_bwd(
        dyb, sv["sg"], sv["z"], dz, p["ln_v_g"], p["ln_v_b"], p["w_spatial"], p["b_spatial_t"],
        tr.weight(l, "w_b_out"), ts=256, name=f"{n}_mix_b_bwd", comm=cm))
    tr.add_grad(l, "w_b_out", dwb, dwb_b)
    g["ln_v_g"], g["ln_v_b"], g["w_spatial"], g["b_spatial"] = red_b[0], red_b[1], dws, dbst.T

    dz, red_c, dwp = _mix_c_bwd(dyc, sv["z"], dz, tr.weight(l, "w_pool"), p["pool_scale"], ts=256,
                                name=f"{n}_mix_c_bwd")
    g["pool_scale"] = red_c[0]
    tr.add_grad(l, "w_pool", dwp, dwp.astype(BF16))

    tr.add_grad(l, "w_in", *tr.run(f"{n}_dw_in", lambda cm: _grad_matmul_t(
        sv["ht"], dz, tk=D_MODEL, tn=1152, name=f"{n}_dw_in", by_chip=True, comm=cm)))
    if f"{n}_presum_tail" in tr.plan:
        tr.alone(f"{n}_presum_tail")
    dx, red_i, dbin = tr.run(f"{n}_in_proj_bwd", lambda cm: _in_proj_bwd(
        dz, tr.weight(l, "w_in"), dxa, sv["x"], sc1, ts=256, name=f"{n}_in_proj_bwd", comm=cm))
    g["b_in"] = dbin[0]
    dada = jnp.stack([red_i[1], red_i[0], red_o[2], red_d[4], red_d[3], red_d[2]])
    return dx, g, dada


def _traffic_plan():
    plan = {
        "gather_l0": [("gather", 0, ("w_in",) + _LATE)],
        "l0_in_proj": [("gather", 0, ("w_up", "w_down"))],
        "l0_mix_o": [("gather", 1, _LATE)],
        "l0_up_proj": [("gather", 1, ("w_in",), (1, 2))],
        "l0_ffn": [("gather", 1, ("w_in",), (3,))],
        "l1_in_proj": [("gather", 1, ("w_up", "w_down"))],
    }
    for l in reversed(range(DEPTH)):
        late = [(l, k) for k in _LATE]
        plan.update({
            f"l{l}_mix_o_bwd": [("presum", [(l, "w_down"), (l, "w_up")])],
            f"l{l}_mix_b_bwd": [("scatter", [(l, "w_down"), (l, "w_up")])],
            f"l{l}_dw_in": [("presum", late), ("join", [(l, "w_down"), (l, "w_up")])],
        })
    late0, late1 = [(0, k) for k in _LATE], [(1, k) for k in _LATE]
    plan["l1_in_proj_bwd"] = [("presum", [(1, "w_in")]), ("scatter", late1)]
    plan["l0_ffn_bwd"] = [("scatter", [(1, "w_in")]), ("join", late1)]
    plan["l0_dw_up"] = [("join", [(1, "w_in")])]
    plan["l0_presum_tail"] = [("presum", [(0, "w_in")])]
    plan["presum_l0_w_in"] = [("scatter", late0)]
    plan["l0_in_proj_bwd"] = [("scatter", [(0, "w_in")])]
    plan["join_tail"] = [("join", [(0, "w_in")] + late0)]
    return plan


def kernel(x, c, w_ada, b_ada, w_in, b_in, conv_a, w_a_out, ln_v_g, ln_v_b, w_spatial, b_spatial, w_b_out, w_pool, pool_scale, w_o, ln1_g, ln1_b, w_up, b_up, conv_ffn, conv_ffn_b, w_down, ln2_g, ln2_b, loss_target, m_w_ada, m_b_ada, m_w_in, m_b_in, m_conv_a, m_w_a_out, m_ln_v_g, m_ln_v_b, m_w_spatial, m_b_spatial, m_w_b_out, m_w_pool, m_pool_scale, m_w_o, m_ln1_g, m_ln1_b, m_w_up, m_b_up, m_conv_ffn, m_conv_ffn_b, m_w_down, m_ln2_g, m_ln2_b, v_w_ada, v_b_ada, v_w_in, v_b_in, v_conv_a, v_w_a_out, v_ln_v_g, v_ln_v_b, v_w_spatial, v_b_spatial, v_w_b_out, v_w_pool, v_pool_scale, v_w_o, v_ln1_g, v_ln1_b, v_w_up, v_b_up, v_conv_ffn, v_conv_ffn_b, v_w_down, v_ln2_g, v_ln2_b):
    args = locals()
    w = {k: args[k] for k in _WEIGHTS}
    m = {k: args["m_" + k] for k in _WEIGHTS}
    v = {k: args["v_" + k] for k in _WEIGHTS}
    d = D_MODEL
    mx, my, mc = _my_coords()
    chip = 2 * mx + my
    me = 4 * mx + 2 * my + mc

    small_shapes = [c.shape, conv_a.shape, conv_ffn.shape]
    small_all = _all_gather8(_pack_small([c, conv_a, conv_ffn]), name="gather_small")
    c_all, conv_a_st, conv_ffn_st = _unpack_small(small_all, small_shapes)
    c_all = c_all.reshape(N_DEV, d)
    conv_full = {"conv_a": jnp.concatenate([conv_a_st[2 * j] for j in range(N_CHIPS)], axis=-1),
                 "conv_ffn": jnp.concatenate([conv_ffn_st[2 * j] for j in range(N_CHIPS)], axis=-1)}

    chip_idx = jnp.reshape(chip, (1,)).astype(jnp.int32)
    slots = {}
    for l in range(DEPTH):
        bufs = _cast_into_slots([_shard3(w[k]) for k in _BIG], l, chip_idx, name=f"cast_l{l}")
        slots.update({(l, k): b for k, b in zip(_BIG, bufs)})
    tr = _Traffic(slots, _traffic_plan(), jnp.reshape(mc, (1,)).astype(jnp.int32), chip_idx)
    tr.alone("gather_l0")

    n_ada = w_ada.shape[2]
    b_ada_mine = lax.dynamic_slice_in_dim(b_ada, chip * n_ada, n_ada, axis=1)
    ada_part = _ada_fwd(c_all, w_ada, b_ada_mine.reshape(DEPTH, 1, n_ada), name="ada_fwd")
    ada_all = _all_gather8(_pack_small([ada_part]), name="gather_ada")
    ada_st = _unpack_small(ada_all, [ada_part.shape])[0][0::2]
    ada_rows = jnp.concatenate([ada_st[j] for j in range(N_CHIPS)], axis=-1)
    ada_mine = lax.dynamic_index_in_dim(ada_rows, me, axis=1, keepdims=False)

    def layer_params(l):
        p = {k: conv_full[k][l] for k in _SMALL_SHARDED}
        for k in ("b_in", "ln_v_g", "ln_v_b", "pool_scale", "ln1_g", "ln1_b", "b_up", "conv_ffn_b", "ln2_g", "ln2_b"):
            p[k] = w[k][l].reshape(1, -1)
        p["w_spatial"] = w_spatial[l]
        p["b_spatial_t"] = b_spatial[l].T
        return p

    xs = x[0]
    saved, adas, params = [], [], []
    for l in range(DEPTH):
        ada = [ada_mine[l, k * d:(k + 1) * d].reshape(1, d) for k in range(6)]
        p = layer_params(l)
        xs, sv = _layer_fwd(xs, ada, p, l, tr)
        saved.append(sv), adas.append(ada), params.append(p)
    dx, loss_blk = _loss_fwd(xs, loss_target[0], ts=512, name="loss")

    grads, dadas = [None] * DEPTH, [None] * DEPTH
    for l in reversed(range(DEPTH)):
        dx, grads[l], dadas[l] = _layer_bwd(dx, adas[l], params[l], saved[l], l, tr)
    tr.alone("join_tail")
    dada = jnp.stack(dadas).reshape(DEPTH, 6 * d)

    small_names = _SMALL_REPL + _SMALL_SHARDED
    small_g = [jnp.stack([grads[l][k] for l in range(DEPTH)]) for k in small_names]
    gsum = dict(zip(small_names, _unpack_small(_all_reduce_small(_pack_small(small_g), name="reduce_small"),
                                               [a.shape for a in small_g])))
    tail_g = [dada, loss_blk[0:1, 0:1]]
    tail_all, tail_sum = _all_gather8(_pack_small(tail_g), name="gather_dada", with_sum=True)
    gsum["b_ada"], loss_sum = _unpack_small(tail_sum, [a.shape for a in tail_g])
    loss = loss_sum[0, 0]
    dada_all = _unpack_small(tail_all, [a.shape for a in tail_g])[0]
    for k in _SMALL_SHARDED:
        wd = gsum[k].shape[-1] // N_CHIPS
        gsum[k] = lax.dynamic_slice_in_dim(gsum[k], chip * wd, wd, axis=gsum[k].ndim - 1)

    dada_cols = lax.dynamic_slice_in_dim(dada_all, chip * n_ada, n_ada, axis=2)
    dada_cols = jnp.pad(jnp.swapaxes(dada_cols, 0, 1), ((0, 0), (0, N_DEV), (0, 0)))
    gsum["w_ada"] = _ada_bwd(jnp.pad(c_all, ((0, N_DEV), (0, 0))), dada_cols, name="ada_bwd")

    out_g, out_d, out_m, out_v = {}, {}, {}, {}
    for k in _WEIGHTS:
        shp = w[k].shape
        if k in _BIG:
            res = tr.run(f"adamw_{k}", lambda cm: _adamw_sharded(
                _shard3(w[k]), _shard3(m[k]), _shard3(v[k]), [tr.final[(l, k)] for l in range(DEPTH)],
                name=f"adamw_{k}", comm=cm))
        else:
            gk = gsum[k].reshape(shp)
            res = [gk] + list(_adamw(_as2d(w[k]), _as2d(gk), _as2d(m[k]), _as2d(v[k]), name=f"adamw_{k}"))
        out_g[k], out_d[k], out_m[k], out_v[k] = [r.reshape(shp) for r in res]

    return (loss, dx[None], *[out_g[k] for k in _WEIGHTS], *[out_d[k] for k in _WEIGHTS],
            *[out_m[k] for k in _WEIGHTS], *[out_v[k] for k in _WEIGHTS])
```

```python
import math
from typing import Callable, NamedTuple

import jax
import jax.numpy as jnp
from jax import lax
from jax.experimental import pallas as pl
from jax.experimental.pallas import tpu as pltpu

F32 = jnp.float32
BF16 = jnp.bfloat16

D_MODEL = 1024
D_Z = 9216
D_FF = 2816
N_GROUPS = 8
GBLK = 128
CHUNK = 64
POOL_WINDOWS = (2, 4, 8, 16)
POOL_GROUP = 256
POOL_HALO = 16
CONV_HALO = 8
HALO_ROWS = 16
DEPTH = 2
ALPHA = (2 * DEPTH) ** 0.25
LN_EPS = 1e-5
ADAM_LR, ADAM_B1, ADAM_B2, ADAM_EPS, ADAM_WD, ADAM_STEP = 0.001, 0.9, 0.999, 1e-08, 0.01, 10
N_CHIPS = 4
N_DEV = 8
FF_CHUNK = 1408
MESH = pl.DeviceIdType.MESH
VMEM_LIMIT = 56 * 1024 * 1024
HBM = pl.BlockSpec(memory_space=pl.ANY)


def _dot(a, b):
    return jnp.dot(a, b, preferred_element_type=F32)


def _dot_nt(a, b):
    return lax.dot_general(a, b, (((1,), (1,)), ((), ())), preferred_element_type=F32)


def _dot_tn(a, b):
    return lax.dot_general(a, b, (((0,), (0,)), ((), ())), preferred_element_type=F32)


_GELU_C = math.sqrt(2.0 / math.pi)


def _gelu_and_grad(x):
    x2 = x * x
    t = jnp.tanh(_GELU_C * (x + 0.044715 * x * x2))
    g = 0.5 * x * (1.0 + t)
    dg = 0.5 * (1.0 + t) + 0.5 * x * (1.0 - t * t) * (_GELU_C * (1.0 + 3 * 0.044715 * x2))
    return g, dg


def _gelu(x):
    return 0.5 * x * (1.0 + jnp.tanh(_GELU_C * (x + 0.044715 * x * x * x)))


def _sigmoid(x):
    return 1.0 / (1.0 + jnp.exp(-x))


def _ln_fwd(r):
    mu = jnp.mean(r, axis=-1, keepdims=True)
    xc = r - mu
    var = jnp.mean(xc * xc, axis=-1, keepdims=True)
    rstd = lax.rsqrt(var + LN_EPS)
    return xc * rstd, rstd


def _ln_bwd(dy, g, xhat, rstd):
    dxh = dy * g
    m1 = jnp.mean(dxh, axis=-1, keepdims=True)
    m2 = jnp.mean(dxh * xhat, axis=-1, keepdims=True)
    return rstd * (dxh - m1 - xhat * m2)


def _rows_before(ext, k, halo):
    return pltpu.roll(ext, k, 0)[halo:]


def _rows_after(ext, k, n):
    return pltpu.roll(ext, ext.shape[0] - k, 0)[:n]


def _colsum(v):
    return jnp.sum(v, axis=0, keepdims=True)


def _spatial_mask():
    i = lax.broadcasted_iota(jnp.int32, (GBLK, GBLK), 0)
    j = lax.broadcasted_iota(jnp.int32, (GBLK, GBLK), 1)
    return (j // CHUNK) <= (i // CHUNK)


def _const(shape):
    n = len(shape)
    return pl.BlockSpec(shape, lambda *_: (0,) * n)


def _resident(shape):
    n = len(shape)
    return pl.BlockSpec(shape, lambda *_: (0,) * n, pipeline_mode=pl.Buffered(1))


def _tile(ts, s):
    return min(ts, s)


class _Comm(NamedTuple):
    srcs: tuple
    dsts: tuple
    n_remote: int
    n_local: int
    build: Callable
    alias: tuple = ()


def _my_coords():
    return lax.axis_index("x"), lax.axis_index("y"), lax.axis_index("c")


def _chip_peer(k):
    mx, my, mc = _my_coords()
    return (mx ^ ((k >> 1) & 1), my ^ (k & 1), mc)


def _sem_scratch(comm):
    return [pltpu.SemaphoreType.DMA((max(comm.n_remote, 1),)), pltpu.SemaphoreType.DMA((max(comm.n_remote, 1),)),
            pltpu.SemaphoreType.DMA((max(comm.n_local, 1),))]


def _run(body, *, name, grid, in_specs, out_specs, out_shape, args, scratch_shapes=(), comm=None, aliases=None,
         prefetch=None):
    sem = ("arbitrary",) * len(grid)
    cparams = pltpu.CompilerParams(dimension_semantics=sem, vmem_limit_bytes=VMEM_LIMIT)
    aliases = dict(aliases or {})
    n_pre = 0 if prefetch is None else 1

    def call(fn, in_specs, out_specs, out_shape, scratch_shapes, args):
        if prefetch is None:
            return pl.pallas_call(fn, name=name, grid=grid, in_specs=in_specs, out_specs=out_specs, out_shape=out_shape,
                                  scratch_shapes=scratch_shapes, compiler_params=cparams,
                                  input_output_aliases=aliases)(*args)
        spec = pltpu.PrefetchScalarGridSpec(num_scalar_prefetch=1, grid=grid, in_specs=in_specs, out_specs=out_specs,
                                            scratch_shapes=scratch_shapes)
        return pl.pallas_call(fn, name=name, grid_spec=spec, out_shape=out_shape, compiler_params=cparams,
                              input_output_aliases={k + 1: v for k, v in aliases.items()})(prefetch, *args)

    if comm is None:
        return call(body, list(in_specs), list(out_specs), list(out_shape), list(scratch_shapes), args)
    n_in, n_cs, n_out, n_cd, n_scr = len(in_specs), len(comm.srcs), len(out_specs), len(comm.dsts), len(scratch_shapes)
    aliases.update({n_in + si: n_out + di for si, di in comm.alias})
    total = math.prod(grid)
    mid_step = min(total - 1, int(total * 0.9))

    def wrapped(*refs):
        pre, refs = refs[:n_pre], refs[n_pre:]
        ins, refs = refs[:n_in], refs[n_in:]
        csrc, refs = refs[:n_cs], refs[n_cs:]
        outs, refs = refs[:n_out], refs[n_out:]
        cdst, refs = refs[:n_cd], refs[n_cd:]
        scr, sems = refs[:n_scr], refs[n_scr:]
        step = pl.program_id(0)
        for ax in range(1, len(grid)):
            step = step * grid[ax] + pl.program_id(ax)
        first, mid, last = comm.build(csrc, cdst, *sems, 0, 0)
        pl.when(step == 0)(first)
        if mid is not None:
            pl.when(step == mid_step)(mid)
        body(*pre, *ins, *outs, *scr)
        pl.when(step == total - 1)(last)

    res = call(wrapped, list(in_specs) + [HBM] * n_cs, list(out_specs) + [HBM] * n_cd,
               list(out_shape) + list(comm.dsts), list(scratch_shapes) + _sem_scratch(comm), (*args, *comm.srcs))
    return res[:n_out], res[n_out:]


def _comm_call(comm, *, name):
    def body(*refs):
        n_cs, n_cd = len(comm.srcs), len(comm.dsts)
        first, mid, last = comm.build(refs[:n_cs], refs[n_cs:n_cs + n_cd], *refs[n_cs + n_cd:], 0, 0)
        first()
        if mid is not None:
            mid()
        last()

    return pl.pallas_call(body, name=name, in_specs=[HBM] * len(comm.srcs), out_specs=[HBM] * len(comm.dsts),
                          out_shape=list(comm.dsts), scratch_shapes=_sem_scratch(comm),
                          input_output_aliases=dict(comm.alias))(*comm.srcs)


def _gather_comm(bufs, peers=(1, 2, 3)):
    dsts = tuple(jax.ShapeDtypeStruct(b.shape, b.dtype) for b in bufs)
    nw = len(bufs)

    def build(srcs, outs, send_sems, recv_sems, local_sems, r0, l0):
        mx, my, mc = _my_coords()
        me = 2 * mx + my
        sibling = (mx, my, 1 - mc)

        def rdma(src, dst, idx, peer):
            return pltpu.make_async_remote_copy(src_ref=src, dst_ref=dst, send_sem=send_sems.at[r0 + idx],
                                                recv_sem=recv_sems.at[r0 + idx], device_id=peer, device_id_type=MESH)

        def ici(w, k, slot):
            return rdma(outs[w].at[me, mc], outs[w].at[slot, mc], 6 * w + k - 1, _chip_peer(k))

        def fwd(w, k, half):
            return rdma(outs[w].at[me ^ k, mc], outs[w].at[me ^ k, half], 6 * w + 2 + k, sibling)

        def first():
            for w in range(nw):
                for k in peers:
                    ici(w, k, me).start()

        def mid():
            for w in range(nw):
                for k in peers:
                    ici(w, k, me ^ k).wait_recv()
                    fwd(w, k, mc).start()

        def last():
            for w in range(nw):
                for k in peers:
                    fwd(w, k, 1 - mc).wait_recv()
                    ici(w, k, me).wait_send()
                    fwd(w, k, mc).wait_send()

        return first, mid, last

    return _Comm(tuple(bufs), dsts, 6 * nw, 0, build, tuple((w, w) for w in range(nw)))


def _symmetric(make_remote, make_local, make_incoming=None):
    def first():
        for cp in make_remote() + make_local():
            cp.start()

    def last():
        for cp in (make_incoming or make_remote)():
            cp.wait_recv()
        for cp in make_remote():
            cp.wait_send()
        for cp in make_local():
            cp.wait()

    return first, None, last


def _presum_comm(g_bf16):
    nw = len(g_bf16)
    dsts = tuple(jax.ShapeDtypeStruct((N_CHIPS,) + g.shape[2:], BF16) for g in g_bf16)

    def build(srcs, outs, send_sems, recv_sems, local_sems, r0, l0):
        mx, my, mc = _my_coords()

        def remote():
            return [pltpu.make_async_remote_copy(
                src_ref=srcs[w].at[j, 1 - mc], dst_ref=outs[w].at[j], send_sem=send_sems.at[r0 + N_CHIPS * w + j],
                recv_sem=recv_sems.at[r0 + N_CHIPS * w + j], device_id=(mx, my, 1 - mc), device_id_type=MESH)
                for w in range(nw) for j in range(N_CHIPS)]

        return _symmetric(remote, lambda: [])

    return _Comm(tuple(g_bf16), dsts, N_CHIPS * nw, 0, build)


def _scatter_comm(h_bf16):
    nw = len(h_bf16)
    dsts = tuple(jax.ShapeDtypeStruct((N_CHIPS - 1,) + h.shape[1:], BF16) for h in h_bf16)

    def build(srcs, outs, send_sems, recv_sems, local_sems, r0, l0):
        mx, my, _ = _my_coords()
        me = 2 * mx + my

        def remote():
            return [pltpu.make_async_remote_copy(
                src_ref=srcs[w].at[me ^ k], dst_ref=outs[w].at[k - 1], send_sem=send_sems.at[r0 + 3 * w + k - 1],
                recv_sem=recv_sems.at[r0 + 3 * w + k - 1], device_id=_chip_peer(k), device_id_type=MESH)
                for w in range(nw) for k in range(1, N_CHIPS)]

        return _symmetric(remote, lambda: [])

    return _Comm(tuple(h_bf16), dsts, 3 * nw, 0, build)


def _join_comm(bufs):
    nw = len(bufs)
    dsts = tuple(jax.ShapeDtypeStruct(b.shape, b.dtype) for b in bufs)

    def build(srcs, outs, send_sems, recv_sems, local_sems, r0, l0):
        mx, my, mc = _my_coords()

        def remote(half=mc):
            return [pltpu.make_async_remote_copy(
                src_ref=outs[w].at[mc], dst_ref=outs[w].at[half], send_sem=send_sems.at[r0 + w],
                recv_sem=recv_sems.at[r0 + w], device_id=(mx, my, 1 - mc), device_id_type=MESH) for w in range(nw)]

        return _symmetric(remote, lambda: [], lambda: remote(1 - mc))

    return _Comm(tuple(bufs), dsts, nw, 0, build, tuple((w, w) for w in range(nw)))


def _merge(comms):
    comms = list(comms)
    if len(comms) == 1:
        return comms[0]

    def build(srcs, outs, send_sems, recv_sems, local_sems, r0, l0):
        phases, s0, d0 = [], 0, 0
        for cm in comms:
            phases.append(cm.build(srcs[s0:s0 + len(cm.srcs)], outs[d0:d0 + len(cm.dsts)], send_sems, recv_sems,
                                   local_sems, r0, l0))
            s0, d0, r0, l0 = s0 + len(cm.srcs), d0 + len(cm.dsts), r0 + cm.n_remote, l0 + cm.n_local

        def run(idx):
            fns = [ph[idx] for ph in phases if ph[idx] is not None]
            if not fns:
                return None

            def go():
                for fn in fns:
                    fn()
            return go

        return run(0), run(1), run(2)

    alias, s0, d0 = [], 0, 0
    for cm in comms:
        alias += [(s0 + si, d0 + di) for si, di in cm.alias]
        s0, d0 = s0 + len(cm.srcs), d0 + len(cm.dsts)
    return _Comm(sum((cm.srcs for cm in comms), ()), sum((cm.dsts for cm in comms), ()),
                 sum(cm.n_remote for cm in comms), sum(cm.n_local for cm in comms), build, tuple(alias))


def _split(comms, res):
    out, d0 = [], 0
    for cm in comms:
        out.append(list(res[d0:d0 + len(cm.dsts)]))
        d0 += len(cm.dsts)
    return out


def _all_reduce_small(x, *, name):
    r, lanes = x.shape
    half = r // 2
    assert half % 8 == 0

    def body(x_ref, out_ref, sib_ref, slots_ref, send_sems, recv_sems):
        mx, my, mc = _my_coords()
        me = 2 * mx + my
        sibling = (mx, my, 1 - mc)
        mine = pl.ds(pl.multiple_of(mc * half, 8), half)
        theirs = pl.ds(pl.multiple_of((1 - mc) * half, 8), half)

        def to_sibling(src, dst, idx):
            return pltpu.make_async_remote_copy(src_ref=src, dst_ref=dst, send_sem=send_sems.at[idx],
                                                recv_sem=recv_sems.at[idx], device_id=sibling, device_id_type=MESH)

        swap = to_sibling(x_ref.at[theirs], sib_ref, 0)
        swap.start()
        swap.wait_recv()
        swap.wait_send()
        slots_ref[me] = x_ref[mine, :] + sib_ref[...]

        def copy(k, slot):
            return pltpu.make_async_remote_copy(
                src_ref=slots_ref.at[me], dst_ref=slots_ref.at[slot], send_sem=send_sems.at[k], recv_sem=recv_sems.at[k],
                device_id=_chip_peer(k), device_id_type=MESH)

        sends = [copy(k, me) for k in range(1, N_CHIPS)]
        for cp in sends:
            cp.start()
        for k in range(1, N_CHIPS):
            copy(k, me ^ k).wait_recv()
        for cp in sends:
            cp.wait_send()
        acc = slots_ref[0]
        for j in range(1, N_CHIPS):
            acc = acc + slots_ref[j]
        out_ref[mine, :] = acc
        join = to_sibling(out_ref.at[mine], out_ref.at[mine], N_CHIPS)
        join.start()
        to_sibling(out_ref.at[mine], out_ref.at[theirs], N_CHIPS).wait_recv()
        join.wait_send()

    vmem = pl.BlockSpec(memory_space=pltpu.VMEM)
    return pl.pallas_call(
        body, name=name, in_specs=[vmem], out_specs=vmem, out_shape=jax.ShapeDtypeStruct((r, lanes), F32),
        scratch_shapes=[pltpu.VMEM((half, lanes), F32), pltpu.VMEM((N_CHIPS, half, lanes), F32),
                        pltpu.SemaphoreType.DMA((N_CHIPS + 1,)), pltpu.SemaphoreType.DMA((N_CHIPS + 1,))],
        compiler_params=pltpu.CompilerParams(vmem_limit_bytes=VMEM_LIMIT),
    )(x)


def _all_gather8(x, *, name, with_sum=False):
    r, lanes = x.shape

    def body(x_ref, out_ref, *rest):
        if with_sum:
            sum_ref, send_sems, recv_sems, local_sem = rest
        else:
            send_sems, recv_sems, local_sem = rest
        mx, my, mc = _my_coords()
        me = 4 * mx + 2 * my + mc

        def peer(k):
            return (mx ^ ((k >> 2) & 1), my ^ ((k >> 1) & 1), mc ^ (k & 1))

        def copy(k, slot):
            return pltpu.make_async_remote_copy(
                src_ref=x_ref, dst_ref=out_ref.at[slot], send_sem=send_sems.at[k - 1], recv_sem=recv_sems.at[k - 1],
                device_id=peer(k), device_id_type=MESH)

        mine = pltpu.make_async_copy(x_ref, out_ref.at[me], local_sem)
        mine.start()
        sends = [copy(k, me) for k in range(1, N_DEV)]
        for cp in sends:
            cp.start()
        for k in range(1, N_DEV):
            copy(k, me ^ k).wait_recv()
        for cp in sends:
            cp.wait_send()
        mine.wait()
        if with_sum:
            acc = out_ref[0]
            for k in range(1, N_DEV):
                acc = acc + out_ref[k]
            sum_ref[...] = acc

    vmem = pl.BlockSpec(memory_space=pltpu.VMEM)
    out_shape = [jax.ShapeDtypeStruct((N_DEV, r, lanes), F32)]
    if with_sum:
        out_shape.append(jax.ShapeDtypeStruct((r, lanes), F32))
    res = pl.pallas_call(
        body, name=name, in_specs=[vmem], out_specs=[vmem] * len(out_shape), out_shape=out_shape,
        scratch_shapes=[pltpu.SemaphoreType.DMA((N_DEV - 1,)), pltpu.SemaphoreType.DMA((N_DEV - 1,)),
                        pltpu.SemaphoreType.DMA],
        compiler_params=pltpu.CompilerParams(vmem_limit_bytes=VMEM_LIMIT),
    )(x)
    return res if with_sum else res[0]


def _mod_matmul(x, sc, sh, w4, b, *, ts, tn, name, comm=None):
    s, d = x.shape
    wd = w4.shape[2]
    n = N_CHIPS * wd
    per = wd // tn
    ts = _tile(ts, s)

    def body(x_ref, sc_ref, sh_ref, w_ref, b_ref, o_ref, ht_ref, h_scr):
        @pl.when(pl.program_id(1) == 0)
        def _():
            h = x_ref[...] * (1.0 + sc_ref[...]) + sh_ref[...]
            h_scr[...] = h.astype(BF16)
            ht_ref[...] = h.T.astype(BF16)
        o_ref[...] = (_dot(h_scr[...], w_ref[0]) + b_ref[...]).astype(BF16)

    return _run(
        body, name=name, grid=(s // ts, n // tn),
        in_specs=[pl.BlockSpec((ts, d), lambda i, j: (i, 0)), _const((1, d)), _const((1, d)),
                  pl.BlockSpec((1, d, tn), lambda i, j: (j // per, 0, j % per)),
                  pl.BlockSpec((1, tn), lambda i, j: (0, j))],
        out_specs=[pl.BlockSpec((ts, tn), lambda i, j: (i, j)), pl.BlockSpec((d, ts), lambda i, j: (0, i))],
        out_shape=[jax.ShapeDtypeStruct((s, n), BF16), jax.ShapeDtypeStruct((d, s), BF16)],
        scratch_shapes=[pltpu.VMEM((ts, d), BF16)],
        args=(x, sc, sh, w4, b), comm=comm)


def _conv3(q, ext, cw):
    return cw[2:3] * q + cw[1:2] * _rows_before(ext, 1, CONV_HALO) + cw[0:1] * _rows_before(ext, 2, CONV_HALO)


def _mix_a_fwd(z, cw, w_out, *, ts, name, comm=None):
    s = z.shape[0]
    d = D_MODEL
    ts = _tile(ts, s)

    def body(zb_ref, zc_ref, zx_ref, cw_ref, w_ref, a_ref, y_ref, carry):
        @pl.when(pl.program_id(0) == 0)
        def _():
            carry[...] = jnp.zeros_like(carry)
        q = zc_ref[...].astype(F32) * zx_ref[...].astype(F32)
        ext = jnp.concatenate([carry[...], q], axis=0)
        a = (zb_ref[...].astype(F32) * _conv3(q, ext, cw_ref[...])).astype(BF16)
        carry[...] = q[ts - CONV_HALO:]
        a_ref[...] = a
        y_ref[...] = _dot(a, w_ref[...])

    zspec = lambda k: pl.BlockSpec((ts, d), lambda i, k=k: (i, k))
    return _run(
        body, name=name, grid=(s // ts,),
        in_specs=[zspec(0), zspec(1), zspec(2), _const((3, d)), _const((d, d))],
        out_specs=[pl.BlockSpec((ts, d), lambda i: (i, 0))] * 2,
        out_shape=[jax.ShapeDtypeStruct((s, d), BF16), jax.ShapeDtypeStruct((s, d), F32)],
        scratch_shapes=[pltpu.VMEM((CONV_HALO, d), F32)],
        args=(z, z, z, cw, w_out), comm=comm)


def _spatial_mix(vn_b, ws_ref, bst_ref, mixed_scr, ts):
    nblk = ts // GBLK
    mask = _spatial_mask()
    for g in range(N_GROUPS):
        cols = slice(g * GBLK, (g + 1) * GBLK)
        wm = jnp.where(mask, ws_ref[g], 0.0).astype(BF16)
        cat = jnp.concatenate([vn_b[n * GBLK:(n + 1) * GBLK, cols] for n in range(nblk)], axis=1)
        res = _dot(wm, cat) + bst_ref[:, g:g + 1]
        for n in range(nblk):
            mixed_scr[n * GBLK:(n + 1) * GBLK, cols] = res[:, n * GBLK:(n + 1) * GBLK]


def _mix_b_fwd(z, ln_g, ln_b, ws, bst, w_out, *, ts, name, comm=None):
    s = z.shape[0]
    d = D_MODEL
    ts = _tile(ts, s)

    def body(zu_ref, zv_ref, g_ref, b_ref, ws_ref, bst_ref, w_ref, sg_ref, y_ref, mixed_scr):
        xhat, _ = _ln_fwd(_gelu(zv_ref[...].astype(F32)))
        vn = (xhat * g_ref[...] + b_ref[...]).astype(BF16)
        _spatial_mix(vn, ws_ref, bst_ref, mixed_scr, ts)
        sg = (_gelu(zu_ref[...].astype(F32)) * mixed_scr[...]).astype(BF16)
        sg_ref[...] = sg
        y_ref[...] = _dot(sg, w_ref[...])

    zspec = lambda k: pl.BlockSpec((ts, d), lambda i, k=k: (i, k))
    return _run(
        body, name=name, grid=(s // ts,),
        in_specs=[zspec(3), zspec(4), _const((1, d)), _const((1, d)), _const((N_GROUPS, GBLK, GBLK)),
                  _const((GBLK, N_GROUPS)), _const((d, d))],
        out_specs=[pl.BlockSpec((ts, d), lambda i: (i, 0))] * 2,
        out_shape=[jax.ShapeDtypeStruct((s, d), BF16), jax.ShapeDtypeStruct((s, d), F32)],
        scratch_shapes=[pltpu.VMEM((ts, d), F32)],
        args=(z, z, ln_g, ln_b, ws, bst, w_out), comm=comm)


def _pool_denoms(tile_idx, ts):
    t1 = (tile_idx * ts + 1 + lax.broadcasted_iota(jnp.int32, (ts, 1), 0)).astype(F32)
    return [jnp.minimum(t1, float(w)) for w in POOL_WINDOWS]


def _pool_diff(p, ext, denoms, k):
    cols = slice(k * POOL_GROUP, (k + 1) * POOL_GROUP)
    acc = ext[:, cols]
    step = 1
    while step < POOL_WINDOWS[k]:
        acc = acc + pltpu.roll(acc, step, 0)
        step *= 2
    return acc[POOL_HALO:] / denoms[k] - p[:, cols]


def _mix_c_fwd(z, w_pool, scale, *, ts, name, comm=None):
    s = z.shape[0]
    d = D_MODEL
    ts = _tile(ts, s)

    def body(zp_ref, w_ref, sc_ref, d_ref, y_ref, carry):
        i = pl.program_id(0)

        @pl.when(i == 0)
        def _():
            carry[...] = jnp.zeros_like(carry)
        p = zp_ref[...].astype(F32)
        ext = jnp.concatenate([carry[...], p], axis=0)
        carry[...] = p[ts - POOL_HALO:]
        denoms = _pool_denoms(i, ts)
        for k in range(len(POOL_WINDOWS)):
            cols = slice(k * POOL_GROUP, (k + 1) * POOL_GROUP)
            dk = _pool_diff(p, ext, denoms, k).astype(BF16)
            d_ref[:, cols] = dk
            y_ref[:, cols] = _dot(dk, w_ref[k]) * sc_ref[:, cols]

    return _run(
        body, name=name, grid=(s // ts,),
        in_specs=[pl.BlockSpec((ts, d), lambda i: (i, 5)), _const((4, POOL_GROUP, POOL_GROUP)), _const((1, d))],
        out_specs=[pl.BlockSpec((ts, d), lambda i: (i, 0))] * 2,
        out_shape=[jax.ShapeDtypeStruct((s, d), BF16), jax.ShapeDtypeStruct((s, d), F32)],
        scratch_shapes=[pltpu.VMEM((POOL_HALO, d), F32)],
        args=(z, w_pool, scale), comm=comm)


def _mix_o_fwd(x, z, ya, yb, yc, w_o, gt, ln_g, ln_b, *, ts, name, comm=None):
    s, d = x.shape
    ts = _tile(ts, s)

    def body(x_ref, ga_ref, gb_ref, gc_ref, ya_ref, yb_ref, yc_ref, w_ref, gt_ref, g_ref, b_ref,
             m_ref, o_ref, x1_ref):
        merged = (_sigmoid(ga_ref[...].astype(F32)) * ya_ref[...] + _sigmoid(gb_ref[...].astype(F32)) * yb_ref[...]
                  + _sigmoid(gc_ref[...].astype(F32)) * yc_ref[...]).astype(BF16)
        m_ref[...] = merged
        o = _dot(merged, w_ref[...])
        o_ref[...] = o
        xhat, _ = _ln_fwd(ALPHA * x_ref[...] + gt_ref[...] * o)
        x1_ref[...] = xhat * g_ref[...] + b_ref[...]

    row = pl.BlockSpec((ts, d), lambda i: (i, 0))
    zspec = lambda k: pl.BlockSpec((ts, d), lambda i, k=k: (i, k))
    return _run(
        body, name=name, grid=(s // ts,),
        in_specs=[row, zspec(6), zspec(7), zspec(8), row, row, row, _const((d, d)),
                  _const((1, d)), _const((1, d)), _const((1, d))],
        out_specs=[row] * 3,
        out_shape=[jax.ShapeDtypeStruct((s, d), BF16), jax.ShapeDtypeStruct((s, d), F32),
                   jax.ShapeDtypeStruct((s, d), F32)],
        args=(x, z, z, z, ya, yb, yc, w_o, gt, ln_g, ln_b), comm=comm)


def _ffn_fwd(up, x1, cw, cb, w_down, gt, ln_g, ln_b, *, ts, name, comm=None, tgt=None):
    s, d = x1.shape
    ts = _tile(ts, s)

    def body(*refs):
        if tgt is None:
            up_ref, x1_ref, cw_ref, cb_ref, w_ref, gt_ref, g_ref, b_ref, ft_ref, dn_ref, x2_ref, carry, f_ref = refs
        else:
            (up_ref, x1_ref, cw_ref, cb_ref, w_ref, gt_ref, g_ref, b_ref, t_ref, ft_ref, dn_ref, x2_ref, l_ref,
             carry, f_ref) = refs

        @pl.when(pl.program_id(0) == 0)
        def _():
            carry[...] = jnp.zeros_like(carry)
            if tgt is not None:
                l_ref[...] = jnp.zeros_like(l_ref)
        for c in range(D_FF // FF_CHUNK):
            ca = slice(c * FF_CHUNK, (c + 1) * FF_CHUNK)
            cg = slice(D_FF + c * FF_CHUNK, D_FF + (c + 1) * FF_CHUNK)
            ua = up_ref[:, ca].astype(F32)
            ext = jnp.concatenate([carry[:, ca], ua], axis=0)
            carry[:, ca] = ua[ts - CONV_HALO:]
            cf = _conv3(ua, ext, cw_ref[:, ca]) + cb_ref[:, ca]
            f = _gelu(cf) * up_ref[:, cg].astype(F32)
            f_ref[:, ca] = f.astype(BF16)
            ft_ref[ca, :] = f.T.astype(BF16)
        dn = _dot(f_ref[...], w_ref[...])
        dn_ref[...] = dn
        xhat, _ = _ln_fwd(ALPHA * x1_ref[...] + gt_ref[...] * dn)
        x2 = xhat * g_ref[...] + b_ref[...]
        if tgt is None:
            x2_ref[...] = x2
        else:
            e = x2 - t_ref[...]
            x2_ref[...] = e / float(d)
            l_ref[...] += 0.5 * jnp.sum(jnp.mean(e * e, axis=-1, keepdims=True), axis=0, keepdims=True)

    row = pl.BlockSpec((ts, d), lambda i: (i, 0))
    head = tgt is not None
    return _run(
        body, name=name, grid=(s // ts,),
        in_specs=[pl.BlockSpec((ts, 2 * D_FF), lambda i: (i, 0)), row, _const((3, D_FF)), _const((1, D_FF)),
                  _resident((D_FF, d)), _const((1, d)), _const((1, d)), _const((1, d))] + [row] * head,
        out_specs=[pl.BlockSpec((D_FF, ts), lambda i: (0, i)), row, row] + [_const((8, 128))] * head,
        out_shape=[jax.ShapeDtypeStruct((D_FF, s), BF16), jax.ShapeDtypeStruct((s, d), F32),
                   jax.ShapeDtypeStruct((s, d), F32)] + [jax.ShapeDtypeStruct((8, 128), F32)] * head,
        scratch_shapes=[pltpu.VMEM((CONV_HALO, D_FF), F32), pltpu.VMEM((ts, D_FF), BF16)],
        args=(up, x1, cw, cb, w_down, gt, ln_g, ln_b) + ((tgt,) if head else ()), comm=comm)


def _rev(n_tiles):
    return lambda i: n_tiles - 1 - i


def _halo_spec(ts, n_tiles, halo, width, col):
    per = ts // halo
    return pl.BlockSpec((halo, width), lambda i: (jnp.maximum((n_tiles - 1 - i) * per - 1, 0), col))


def _ffn_bwd(dx2, x1, dn, up, cw, cb, w_down, w_up4, gt, ln_g, sc, *, ts, name, comm=None):
    s, d = x1.shape
    ts = _tile(ts, s)
    nt = s // ts
    rev = _rev(nt)
    wd = w_up4.shape[2]

    def w_up_cols(wu_ref, start):
        return wu_ref[start // wd, :, start % wd:start % wd + FF_CHUNK]

    def body(dx2_ref, x1_ref, dn_ref, up_ref, halo_ref, cw_ref, cb_ref, wd_ref, wu_ref, gt_ref, g_ref, sc_ref,
             ddn_ref, dup_ref, dx1_ref, redd_ref, redf_ref, dbup_ref, carry):
        i = pl.program_id(0)

        @pl.when(i == 0)
        def _():
            carry[...] = jnp.zeros_like(carry)
            redd_ref[...] = jnp.zeros_like(redd_ref)
            redf_ref[...] = jnp.zeros_like(redf_ref)
            dbup_ref[...] = jnp.zeros_like(dbup_ref)
        first_tile = i == nt - 1
        x1v, dnv, dyv = x1_ref[...], dn_ref[...], dx2_ref[...]
        xhat, rstd = _ln_fwd(ALPHA * x1v + gt_ref[...] * dnv)
        dr = _ln_bwd(dyv, g_ref[...], xhat, rstd)
        redd_ref[0:1, :] += _colsum(dyv * xhat)
        redd_ref[1:2, :] += _colsum(dyv)
        redd_ref[2:3, :] += _colsum(dr * dnv)
        ddn = (gt_ref[...] * dr).astype(BF16)
        ddn_ref[...] = ddn
        dh = jnp.zeros((ts, d), F32)
        for c in range(D_FF // FF_CHUNK):
            ca = slice(c * FF_CHUNK, (c + 1) * FF_CHUNK)
            cg = slice(D_FF + c * FF_CHUNK, D_FF + (c + 1) * FF_CHUNK)
            df = _dot_nt(ddn, wd_ref[ca, :])
            ua, ug = up_ref[:, ca].astype(F32), up_ref[:, cg].astype(F32)
            halo = jnp.where(first_tile, 0.0, halo_ref[:, ca].astype(F32)[HALO_ROWS - CONV_HALO:])
            ext = jnp.concatenate([halo, ua], axis=0)
            u1, u2 = _rows_before(ext, 1, CONV_HALO), _rows_before(ext, 2, CONV_HALO)
            cwc = cw_ref[:, ca]
            gl, dgl = _gelu_and_grad(cwc[2:3] * ua + cwc[1:2] * u1 + cwc[0:1] * u2 + cb_ref[:, ca])
            dug = df * gl
            dcf = df * ug * dgl
            redf_ref[0:1, ca] += _colsum(dcf * u2)
            redf_ref[1:2, ca] += _colsum(dcf * u1)
            redf_ref[2:3, ca] += _colsum(dcf * ua)
            redf_ref[3:4, ca] += _colsum(dcf)
            extd = jnp.concatenate([dcf, carry[:, ca]], axis=0)
            carry[:, ca] = dcf[:CONV_HALO]
            dua = cwc[2:3] * dcf + cwc[1:2] * _rows_after(extd, 1, ts) + cwc[0:1] * _rows_after(extd, 2, ts)
            dbup_ref[0:1, ca] += _colsum(dua)
            dbup_ref[0:1, cg] += _colsum(dug)
            dua_b, dug_b = dua.astype(BF16), dug.astype(BF16)
            dup_ref[:, ca] = dua_b
            dup_ref[:, cg] = dug_b
            dh = dh + _dot_nt(dua_b, w_up_cols(wu_ref, c * FF_CHUNK)) + _dot_nt(dug_b, w_up_cols(wu_ref, D_FF + c * FF_CHUNK))
        dx1_ref[...] = ALPHA * dr + dh * (1.0 + sc_ref[...])
        redd_ref[3:4, :] += _colsum(dh * x1v)
        redd_ref[4:5, :] += _colsum(dh)

    row = pl.BlockSpec((ts, d), lambda i: (rev(i), 0))
    return _run(
        body, name=name, grid=(nt,),
        in_specs=[row, row, row, pl.BlockSpec((ts, 2 * D_FF), lambda i: (rev(i), 0)),
                  _halo_spec(ts, nt, HALO_ROWS, D_FF, 0), _const((3, D_FF)), _const((1, D_FF)),
                  _resident((D_FF, d)), _resident((N_CHIPS, d, wd)), _const((1, d)), _const((1, d)), _const((1, d))],
        out_specs=[row, pl.BlockSpec((ts, 2 * D_FF), lambda i: (rev(i), 0)), row,
                   _const((8, d)), _const((8, D_FF)), _const((8, 2 * D_FF))],
        out_shape=[jax.ShapeDtypeStruct((s, d), BF16), jax.ShapeDtypeStruct((s, 2 * D_FF), BF16),
                   jax.ShapeDtypeStruct((s, d), F32), jax.ShapeDtypeStruct((8, d), F32),
                   jax.ShapeDtypeStruct((8, D_FF), F32), jax.ShapeDtypeStruct((8, 2 * D_FF), F32)],
        scratch_shapes=[pltpu.VMEM((CONV_HALO, D_FF), F32)],
        args=(dx2, x1, dn, up, up, cw, cb, w_down, w_up4, gt, ln_g, sc), comm=comm)


def _accumulate_dw(dw_ref, dwb_ref, xa, dy, first, last):
    @pl.when(first)
    def _():
        dw_ref[...] = jnp.zeros_like(dw_ref)
    dw_ref[...] += _dot_tn(xa, dy)

    @pl.when(last)
    def _():
        dwb_ref[...] = dw_ref[...].astype(BF16)


def _dw_out(d):
    return [_const((d, d))] * 2, [jax.ShapeDtypeStruct((d, d), F32), jax.ShapeDtypeStruct((d, d), BF16)]


def _grad_matmul_t(xt, dy, *, tk, tn, name, by_chip=False, comm=None):
    k, s = xt.shape
    n = dy.shape[1]

    def body(xt_ref, dy_ref, o_ref, ob_ref):
        o = _dot(xt_ref[...], dy_ref[...]).reshape(o_ref.shape)
        o_ref[...] = o
        ob_ref[...] = o.astype(BF16)

    if by_chip:
        assert tk == k
        per = n // N_CHIPS // tn
        ospec = pl.BlockSpec((1, k, tn), lambda j, i: (j // per, 0, j % per))
        shape = (N_CHIPS, k, n // N_CHIPS)
    else:
        ospec = pl.BlockSpec((tk, tn), lambda j, i: (i, j))
        shape = (k, n)
    xspec = _resident((k, s)) if tk == k else pl.BlockSpec((tk, s), lambda j, i: (i, 0))
    dspec = _resident((s, n)) if tn == n else pl.BlockSpec((s, tn), lambda j, i: (0, j))
    return _run(body, name=name, grid=(n // tn, k // tk), in_specs=[xspec, dspec], out_specs=[ospec, ospec],
                out_shape=[jax.ShapeDtypeStruct(shape, F32), jax.ShapeDtypeStruct(shape, BF16)], args=(xt, dy), comm=comm)


def _mix_o_bwd(dx1, x, o, z, ya, yb, yc, merged, w_o, gt, ln_g, *, ts, name, comm=None):
    s, d = x.shape
    ts = _tile(ts, s)
    nt = s // ts

    def body(dx1_ref, x_ref, o_ref, ga_ref, gb_ref, gc_ref, ya_ref, yb_ref, yc_ref, m_ref, w_ref, gt_ref, g_ref,
             dxa_ref, dzg_ref, dya_ref, dyb_ref, dyc_ref, red_ref, dw_ref, dwb_ref):
        i = pl.program_id(0)

        @pl.when(i == 0)
        def _():
            red_ref[...] = jnp.zeros_like(red_ref)
        dyv, ov = dx1_ref[...], o_ref[...]
        xhat, rstd = _ln_fwd(ALPHA * x_ref[...] + gt_ref[...] * ov)
        dr = _ln_bwd(dyv, g_ref[...], xhat, rstd)
        red_ref[0:1, :] += _colsum(dyv * xhat)
        red_ref[1:2, :] += _colsum(dyv)
        red_ref[2:3, :] += _colsum(dr * ov)
        dxa_ref[...] = ALPHA * dr
        d_o = (gt_ref[...] * dr).astype(BF16)
        _accumulate_dw(dw_ref, dwb_ref, m_ref[...], d_o, i == 0, i == nt - 1)
        dm = _dot_nt(d_o, w_ref[...])
        for k, (zg_ref, y_ref, dy_ref) in enumerate(((ga_ref, ya_ref, dya_ref), (gb_ref, yb_ref, dyb_ref),
                                                     (gc_ref, yc_ref, dyc_ref))):
            g = _sigmoid(zg_ref[...].astype(F32))
            dzg_ref[:, k * d:(k + 1) * d] = (dm * y_ref[...] * g * (1.0 - g)).astype(BF16)
            dy_ref[...] = (dm * g).astype(BF16)

    row = pl.BlockSpec((ts, d), lambda i: (i, 0))
    zspec = lambda k: pl.BlockSpec((ts, d), lambda i, k=k: (i, k))
    bf = jax.ShapeDtypeStruct((s, d), BF16)
    dw_specs, dw_shapes = _dw_out(d)
    return _run(
        body, name=name, grid=(nt,),
        in_specs=[row, row, row, zspec(6), zspec(7), zspec(8), row, row, row, row, _const((d, d)),
                  _const((1, d)), _const((1, d))],
        out_specs=[row, pl.BlockSpec((ts, 3 * d), lambda i: (i, 2)), row, row, row, _const((8, d))] + dw_specs,
        out_shape=[jax.ShapeDtypeStruct((s, d), F32), jax.ShapeDtypeStruct((s, D_Z), BF16), bf, bf, bf,
                   jax.ShapeDtypeStruct((8, d), F32)] + dw_shapes,
        args=(dx1, x, o, z, z, z, ya, yb, yc, merged, w_o, gt, ln_g), comm=comm)


def _mix_a_bwd(dya, a, z, dz, cw, w_out, *, ts, name, comm=None):
    s = z.shape[0]
    d = D_MODEL
    ts = _tile(ts, s)
    nt = s // ts
    rev = _rev(nt)

    def body(dya_ref, a_ref, zb_ref, zc_ref, zx_ref, hc_ref, hx_ref, cw_ref, w_ref, dz_in, dz_ref, red_ref,
             dw_ref, dwb_ref, carry):
        i = pl.program_id(0)

        @pl.when(i == 0)
        def _():
            carry[...] = jnp.zeros_like(carry)
            red_ref[...] = jnp.zeros_like(red_ref)
        _accumulate_dw(dw_ref, dwb_ref, a_ref[...], dya_ref[...], i == 0, i == nt - 1)
        zb, zc, zx = zb_ref[...].astype(F32), zc_ref[...].astype(F32), zx_ref[...].astype(F32)
        q = zc * zx
        halo = jnp.where(i == nt - 1, 0.0, (hc_ref[...].astype(F32) * hx_ref[...].astype(F32))[HALO_ROWS - CONV_HALO:])
        ext = jnp.concatenate([halo, q], axis=0)
        q1, q2 = _rows_before(ext, 1, CONV_HALO), _rows_before(ext, 2, CONV_HALO)
        cwv = cw_ref[...]
        cv = cwv[2:3] * q + cwv[1:2] * q1 + cwv[0:1] * q2
        da = _dot_nt(dya_ref[...], w_ref[...])
        dcv = da * zb
        red_ref[0:1, :] += _colsum(dcv * q2)
        red_ref[1:2, :] += _colsum(dcv * q1)
        red_ref[2:3, :] += _colsum(dcv * q)
        extd = jnp.concatenate([dcv, carry[...]], axis=0)
        carry[...] = dcv[:CONV_HALO]
        dq = cwv[2:3] * dcv + cwv[1:2] * _rows_after(extd, 1, ts) + cwv[0:1] * _rows_after(extd, 2, ts)
        dz_ref[:, 0:d] = (da * cv).astype(BF16)
        dz_ref[:, d:2 * d] = (dq * zx).astype(BF16)
        dz_ref[:, 2 * d:3 * d] = (dq * zc).astype(BF16)

    zspec = lambda k: pl.BlockSpec((ts, d), lambda i, k=k: (rev(i), k))
    row = pl.BlockSpec((ts, d), lambda i: (rev(i), 0))
    dw_specs, dw_shapes = _dw_out(d)
    return _run(
        body, name=name, grid=(nt,),
        in_specs=[row, row, zspec(0), zspec(1), zspec(2),
                  _halo_spec(ts, nt, HALO_ROWS, d, 1), _halo_spec(ts, nt, HALO_ROWS, d, 2),
                  _const((3, d)), _const((d, d)), HBM],
        out_specs=[pl.BlockSpec((ts, 3 * d), lambda i: (rev(i), 0)), _const((8, d))] + dw_specs,
        out_shape=[jax.ShapeDtypeStruct((s, D_Z), BF16), jax.ShapeDtypeStruct((8, d), F32)] + dw_shapes,
        scratch_shapes=[pltpu.VMEM((CONV_HALO, d), F32)],
        args=(dya, a, z, z, z, z, z, cw, w_out, dz), aliases={9: 0}, comm=comm)


def _mix_b_bwd(dyb, sg, z, dz, ln_g, ln_b, ws, bst, w_out, *, ts, name, comm=None):
    s = z.shape[0]
    d = D_MODEL
    ts = _tile(ts, s)
    nt = s // ts
    nblk = ts // GBLK

    def body(dyb_ref, sg_ref, zu_ref, zv_ref, g_ref, b_ref, ws_ref, bst_ref, w_ref, dz_in,
             dz_ref, red_ref, dws_ref, dbst_ref, dw_ref, dwb_ref, mixed_scr, dvn_scr, dzv_scr):
        first = (pl.program_id(0) == 0) & (pl.program_id(1) == 0)

        @pl.when(first)
        def _():
            red_ref[...] = jnp.zeros_like(red_ref)
            dws_ref[...] = jnp.zeros_like(dws_ref)
            dbst_ref[...] = jnp.zeros_like(dbst_ref)

        @pl.when(pl.program_id(1) == 0)
        def _():
            _accumulate_dw(dw_ref, dwb_ref, sg_ref[...], dyb_ref[...], first, pl.program_id(0) == nt - 1)
            u, du_dz = _gelu_and_grad(zu_ref[...].astype(F32))
            vg, dv_dz = _gelu_and_grad(zv_ref[...].astype(F32))
            xhat, rstd = _ln_fwd(vg)
            vn = (xhat * g_ref[...] + b_ref[...]).astype(BF16)
            _spatial_mix(vn, ws_ref, bst_ref, mixed_scr, ts)
            dsg = _dot_nt(dyb_ref[...], w_ref[...])
            dz_ref[...] = (dsg * mixed_scr[...] * du_dz).astype(BF16)
            dmix = dsg * u
            mask = _spatial_mask()
            for g in range(N_GROUPS):
                cols = slice(g * GBLK, (g + 1) * GBLK)
                wm = jnp.where(mask, ws_ref[g], 0.0).astype(BF16)
                dm_cat = jnp.concatenate([dmix[n * GBLK:(n + 1) * GBLK, cols] for n in range(nblk)], axis=1)
                vn_cat = jnp.concatenate([vn[n * GBLK:(n + 1) * GBLK, cols] for n in range(nblk)], axis=1)
                dm_b = dm_cat.astype(BF16)
                dbst_ref[:, g:g + 1] += jnp.sum(dm_cat, axis=1, keepdims=True)
                dws_ref[g] += jnp.where(mask, _dot_nt(dm_b, vn_cat), 0.0)
                dvn_cat = _dot_tn(wm, dm_b)
                for n in range(nblk):
                    dvn_scr[n * GBLK:(n + 1) * GBLK, cols] = dvn_cat[:, n * GBLK:(n + 1) * GBLK]
            dvn = dvn_scr[...]
            red_ref[0:1, :] += _colsum(dvn * xhat)
            red_ref[1:2, :] += _colsum(dvn)
            dzv_scr[...] = (_ln_bwd(dvn, g_ref[...], xhat, rstd) * dv_dz).astype(BF16)

        @pl.when(pl.program_id(1) == 1)
        def _():
            dz_ref[...] = dzv_scr[...]

    zspec = lambda k: pl.BlockSpec((ts, d), lambda i, h, k=k: (i, k))
    row = pl.BlockSpec((ts, d), lambda i, h: (i, 0))
    dw_specs, dw_shapes = _dw_out(d)
    return _run(
        body, name=name, grid=(nt, 2),
        in_specs=[row, row, zspec(3), zspec(4), _const((1, d)), _const((1, d)),
                  _const((N_GROUPS, GBLK, GBLK)), _const((GBLK, N_GROUPS)), _const((d, d)), HBM],
        out_specs=[pl.BlockSpec((ts, d), lambda i, h: (i, 3 + h)), _const((8, d)),
                   _const((N_GROUPS, GBLK, GBLK)), _const((GBLK, N_GROUPS))] + dw_specs,
        out_shape=[jax.ShapeDtypeStruct((s, D_Z), BF16), jax.ShapeDtypeStruct((8, d), F32),
                   jax.ShapeDtypeStruct((N_GROUPS, GBLK, GBLK), F32), jax.ShapeDtypeStruct((GBLK, N_GROUPS), F32)]
        + dw_shapes,
        scratch_shapes=[pltpu.VMEM((ts, d), F32), pltpu.VMEM((ts, d), F32), pltpu.VMEM((ts, d), BF16)],
        args=(dyb, sg, z, z, ln_g, ln_b, ws, bst, w_out, dz), aliases={9: 0}, comm=comm)


def _mix_c_bwd(dyc, z, dz, w_pool, scale, *, ts, name):
    s = z.shape[0]
    d = D_MODEL
    ts = _tile(ts, s)
    nt = s // ts
    rev = _rev(nt)

    def body(dyc_ref, zp_ref, halo_ref, w_ref, sc_ref, dz_in, dz_ref, red_ref, dw_ref, carry):
        i = pl.program_id(0)

        @pl.when(i == 0)
        def _():
            carry[...] = jnp.zeros_like(carry)
            red_ref[...] = jnp.zeros_like(red_ref)
            dw_ref[...] = jnp.zeros_like(dw_ref)
        p = zp_ref[...].astype(F32)
        ext = jnp.concatenate([jnp.where(i == nt - 1, 0.0, halo_ref[...].astype(F32)), p], axis=0)
        denoms = _pool_denoms(rev(i), ts)
        dyv = dyc_ref[...].astype(F32)
        for k in range(len(POOL_WINDOWS)):
            cols = slice(k * POOL_GROUP, (k + 1) * POOL_GROUP)
            dk = _pool_diff(p, ext, denoms, k).astype(BF16)
            red_ref[0:1, cols] += _colsum(dyv[:, cols] * _dot(dk, w_ref[k]))
            dpre = (dyv[:, cols] * sc_ref[:, cols]).astype(BF16)
            dw_ref[k] += _dot_tn(dk, dpre)
            dd = _dot_nt(dpre, w_ref[k])
            e = dd / denoms[k]
            acc = jnp.concatenate([e, carry[:, cols]], axis=0)
            carry[:, cols] = e[:POOL_HALO]
            step = 1
            while step < POOL_WINDOWS[k]:
                acc = acc + pltpu.roll(acc, acc.shape[0] - step, 0)
                step *= 2
            dz_ref[:, cols] = (acc[:ts] - dd).astype(BF16)

    return _run(
        body, name=name, grid=(nt,),
        in_specs=[pl.BlockSpec((ts, d), lambda i: (rev(i), 0)), pl.BlockSpec((ts, d), lambda i: (rev(i), 5)),
                  _halo_spec(ts, nt, POOL_HALO, d, 5), _const((4, POOL_GROUP, POOL_GROUP)), _const((1, d)), HBM],
        out_specs=[pl.BlockSpec((ts, d), lambda i: (rev(i), 5)), _const((8, d)), _const((4, POOL_GROUP, POOL_GROUP))],
        out_shape=[jax.ShapeDtypeStruct((s, D_Z), BF16), jax.ShapeDtypeStruct((8, d), F32),
                   jax.ShapeDtypeStruct((4, POOL_GROUP, POOL_GROUP), F32)],
        scratch_shapes=[pltpu.VMEM((POOL_HALO, d), F32)],
        args=(dyc, z, z, w_pool, scale, dz), aliases={5: 0})


def _in_proj_bwd(dz, w4, dxa, x, sc, *, ts, name, comm=None):
    s, d = x.shape
    ts = _tile(ts, s)
    wd = w4.shape[2]

    def body(dz_ref, w_ref, dxa_ref, x_ref, sc_ref, dx_ref, red_ref, db_ref):
        @pl.when(pl.program_id(0) == 0)
        def _():
            red_ref[...] = jnp.zeros_like(red_ref)
            db_ref[...] = jnp.zeros_like(db_ref)
        dh = jnp.zeros((ts, d), F32)
        for j in range(N_CHIPS):
            dzj = dz_ref[:, j * wd:(j + 1) * wd]
            db_ref[0:1, j * wd:(j + 1) * wd] += _colsum(dzj.astype(F32))
            dh = dh + _dot_nt(dzj, w_ref[j])
        dx_ref[...] = dxa_ref[...] + dh * (1.0 + sc_ref[...])
        red_ref[0:1, :] += _colsum(dh * x_ref[...])
        red_ref[1:2, :] += _colsum(dh)

    row = pl.BlockSpec((ts, d), lambda i: (i, 0))
    return _run(
        body, name=name, grid=(s // ts,),
        in_specs=[pl.BlockSpec((ts, D_Z), lambda i: (i, 0)), _resident((N_CHIPS, d, wd)), row, row, _const((1, d))],
        out_specs=[row, _const((8, d)), _const((8, D_Z))],
        out_shape=[jax.ShapeDtypeStruct((s, d), F32), jax.ShapeDtypeStruct((8, d), F32),
                   jax.ShapeDtypeStruct((8, D_Z), F32)],
        args=(dz, w4, dxa, x, sc), comm=comm)


def _ada_fwd(c_all, w_ada, b_ada, *, name):
    nl, d, n = w_ada.shape
    tn = n // 2

    def body(c_ref, w_ref, b_ref, o_ref):
        cv = c_ref[...]
        ca = (cv * _sigmoid(cv)).astype(BF16)
        o_ref[0] = _dot(ca, w_ref[0].astype(BF16)) + b_ref[0]

    return _run(
        body, name=name, grid=(nl, n // tn),
        in_specs=[_const((N_DEV, d)), pl.BlockSpec((1, d, tn), lambda l, j: (l, 0, j)),
                  pl.BlockSpec((1, 1, tn), lambda l, j: (l, 0, j))],
        out_specs=[pl.BlockSpec((1, N_DEV, tn), lambda l, j: (l, 0, j))],
        out_shape=[jax.ShapeDtypeStruct((nl, N_DEV, n), F32)], args=(c_all, w_ada, b_ada))[0]


def _ada_bwd(c_all, dada, *, name):
    nl, nb, n = dada.shape
    d = c_all.shape[1]
    tn = n // 2

    def body(c_ref, g_ref, o_ref):
        cv = c_ref[...]
        ca = (cv * _sigmoid(cv)).astype(BF16)
        o_ref[0] = _dot_tn(ca, g_ref[0].astype(BF16))

    return _run(
        body, name=name, grid=(nl, n // tn),
        in_specs=[_const((nb, d)), pl.BlockSpec((1, nb, tn), lambda l, j: (l, 0, j))],
        out_specs=[pl.BlockSpec((1, d, tn), lambda l, j: (l, 0, j))],
        out_shape=[jax.ShapeDtypeStruct((nl, d, n), F32)], args=(c_all, dada))[0]


def _sum4_into_half(owns, recvs, core, *, name):
    nw = len(owns)
    r = owns[0].shape[0]
    tr = _row_tile(r, max(o.shape[1] for o in owns), 2)

    def body(core_ref, *refs):
        for own_ref, recv_ref, o_ref in zip(refs[:nw], refs[nw:2 * nw], refs[2 * nw:]):
            acc = own_ref[...]
            for k in range(N_CHIPS - 1):
                acc = acc + recv_ref[k].astype(F32)
            o_ref[0] = acc

    cols = [o.shape[1] for o in owns]
    spec = pltpu.PrefetchScalarGridSpec(
        num_scalar_prefetch=1, grid=(r // tr,),
        in_specs=[pl.BlockSpec((tr, c), lambda i, core_ref: (i, 0)) for c in cols]
        + [pl.BlockSpec((N_CHIPS - 1, tr, c), lambda i, core_ref: (0, i, 0)) for c in cols],
        out_specs=[pl.BlockSpec((1, tr, c), lambda i, core_ref: (core_ref[0], i, 0)) for c in cols])
    return pl.pallas_call(
        body, name=name, grid_spec=spec, out_shape=[jax.ShapeDtypeStruct((2, r, c), F32) for c in cols],
        compiler_params=pltpu.CompilerParams(dimension_semantics=("arbitrary",), vmem_limit_bytes=VMEM_LIMIT),
    )(core, *owns, *recvs)


def _cast_into_slots(shards, layer, chip, *, name):
    quarters = 4

    def body(chip_ref, *refs):
        ins, outs = refs[:len(shards)], refs[len(shards):]
        for i_ref, o_ref in zip(ins, outs):
            o_ref[0, 0] = i_ref[0].astype(BF16)

    in_specs, out_specs, out_shape = [], [], []
    for sh in shards:
        _, r, c = sh.shape
        in_specs.append(pl.BlockSpec((1, r // quarters, c), lambda t, chip_ref: (layer, t, 0)))
        out_specs.append(pl.BlockSpec((1, 1, r // quarters, c), lambda t, chip_ref: (chip_ref[0], t // 2, t % 2, 0)))
        out_shape.append(jax.ShapeDtypeStruct((N_CHIPS, 2, r // 2, c), BF16))
    spec = pltpu.PrefetchScalarGridSpec(num_scalar_prefetch=1, grid=(quarters,), in_specs=in_specs, out_specs=out_specs)
    return pl.pallas_call(
        body, name=name, grid_spec=spec, out_shape=out_shape,
        compiler_params=pltpu.CompilerParams(dimension_semantics=("arbitrary",), vmem_limit_bytes=VMEM_LIMIT),
    )(chip, *shards)


def _sum_halves(g_f32s, theirs, place, *, name, comm=None):
    nw = len(g_f32s)
    rh = g_f32s[0].shape[2]
    cols = [g.shape[3] for g in g_f32s]
    tr = _row_tile(rh, max(cols), 2)

    def body(place_ref, *refs):
        for g_ref, t_ref, hb_ref, own_ref in zip(refs[:nw], refs[nw:2 * nw], refs[2 * nw:3 * nw], refs[3 * nw:]):
            h = g_ref[0, 0] + t_ref[0].astype(F32)
            hb_ref[0] = h.astype(BF16)

            @pl.when(pl.program_id(1) == place_ref[1])
            def _():
                own_ref[...] = h

    return _run(
        body, name=name, grid=(rh // tr, N_CHIPS), prefetch=place,
        in_specs=[pl.BlockSpec((1, 1, tr, c), lambda i, j, place_ref: (j, place_ref[0], i, 0)) for c in cols]
        + [pl.BlockSpec((1, tr, c), lambda i, j, place_ref: (j, i, 0)) for c in cols],
        out_specs=[pl.BlockSpec((1, tr, c), lambda i, j, place_ref: (j, i, 0)) for c in cols]
        + [pl.BlockSpec((tr, c), lambda i, j, place_ref: (i, 0)) for c in cols],
        out_shape=[jax.ShapeDtypeStruct((N_CHIPS, rh, c), BF16) for c in cols]
        + [jax.ShapeDtypeStruct((rh, c), F32) for c in cols],
        args=(*g_f32s, *theirs), comm=comm)


def _row_tile(r, c, mib):
    limit = max(8, (mib << 20) // (4 * c))
    if r <= limit:
        return r
    best = 8
    for t in range(8, limit + 1, 8):
        if r % t == 0:
            best = t
    return best


def _adam_math(w, g, m, v):
    mn = ADAM_B1 * m + (1.0 - ADAM_B1) * g
    vn = ADAM_B2 * v + (1.0 - ADAM_B2) * (g * g)
    m_hat = mn / (1.0 - ADAM_B1 ** ADAM_STEP)
    v_hat = vn / (1.0 - ADAM_B2 ** ADAM_STEP)
    return -ADAM_LR * (m_hat / (jnp.sqrt(v_hat) + ADAM_EPS) + ADAM_WD * w), mn, vn


def _adamw(w, g, m, v, *, name):
    r, c = w.shape
    tr = _row_tile(r, c, 2)

    def body(w_ref, g_ref, m_ref, v_ref, d_ref, mo_ref, vo_ref):
        d_ref[...], mo_ref[...], vo_ref[...] = _adam_math(w_ref[...], g_ref[...], m_ref[...], v_ref[...])

    blk = pl.BlockSpec((tr, c), lambda i: (i, 0))
    return _run(body, name=name, grid=(r // tr,), in_specs=[blk] * 4, out_specs=[blk] * 3,
                out_shape=[jax.ShapeDtypeStruct((r, c), F32)] * 3, args=(w, g, m, v))


def _adamw_sharded(w, m, v, grads, *, name, comm=None):
    nl, r, c = w.shape
    tr = _row_tile(r, c, 1)
    nt = r // tr

    def body(w_ref, m_ref, v_ref, g0_ref, g1_ref, g_ref, d_ref, mo_ref, vo_ref):
        g = jnp.where(pl.program_id(0) == 0, g0_ref[...], g1_ref[...])
        g_ref[0] = g
        d_ref[0], mo_ref[0], vo_ref[0] = _adam_math(w_ref[0], g, m_ref[0], v_ref[0])

    blk = pl.BlockSpec((1, tr, c), lambda l, i: (l, i, 0))
    part0 = pl.BlockSpec((tr, c), lambda l, i: (jnp.where(l == 0, i, nt - 1), 0))
    part1 = pl.BlockSpec((tr, c), lambda l, i: (jnp.where(l == 1, i, 0), 0))
    return _run(body, name=name, grid=(nl, nt), in_specs=[blk] * 3 + [part0, part1],
                out_specs=[blk] * 4, out_shape=[jax.ShapeDtypeStruct((nl, r, c), F32)] * 4,
                args=(w, m, v, grads[0], grads[1]), comm=comm)


_BIG = ("w_in", "w_a_out", "w_b_out", "w_pool", "w_o", "w_up", "w_down")
_COL_SHARDED = ("w_in", "w_up")
_SMALL_SHARDED = ("conv_a", "conv_ffn")
_SMALL_REPL = ("b_in", "ln_v_g", "ln_v_b", "w_spatial", "b_spatial", "pool_scale", "ln1_g", "ln1_b", "b_up",
               "conv_ffn_b", "ln2_g", "ln2_b")
_WEIGHTS = ("w_ada", "b_ada", "w_in", "b_in", "conv_a", "w_a_out", "ln_v_g", "ln_v_b", "w_spatial", "b_spatial",
            "w_b_out", "w_pool", "pool_scale", "w_o", "ln1_g", "ln1_b", "w_up", "b_up", "conv_ffn", "conv_ffn_b",
            "w_down", "ln2_g", "ln2_b")


def _shard3(a):
    return a.reshape(a.shape[0], -1, a.shape[-1])


def _use_gathered(name, g):
    g = g.reshape(N_CHIPS, -1, g.shape[-1])
    if name in _COL_SHARDED:
        return g
    if name == "w_pool":
        return g.reshape(N_CHIPS, 4, POOL_GROUP // N_CHIPS, POOL_GROUP).transpose(1, 0, 2, 3).reshape(
            4, POOL_GROUP, POOL_GROUP)
    return g.reshape(-1, g.shape[-1])


def _grad_by_chip(name, g):
    if name in _COL_SHARDED:
        return g
    if name == "w_pool":
        return g.reshape(4, N_CHIPS, POOL_GROUP // N_CHIPS, POOL_GROUP).transpose(1, 0, 2, 3).reshape(
            N_CHIPS, POOL_GROUP, POOL_GROUP)
    return g.reshape(N_CHIPS, -1, g.shape[-1])


def _pack_small(arrs):
    parts = []
    for a in arrs:
        flat = a.reshape(-1).astype(F32)
        pad = (-flat.shape[0]) % 128
        parts.append(jnp.pad(flat, (0, pad)) if pad else flat)
    flat = jnp.concatenate(parts)
    pad = (-flat.shape[0]) % 2048
    if pad:
        flat = jnp.pad(flat, (0, pad))
    return flat.reshape(-1, 128)


def _unpack_small(buf, shapes):
    lead = buf.shape[:-2]
    flat = buf.reshape(lead + (-1,))
    out, off = [], 0
    for shp in shapes:
        n = math.prod(shp)
        out.append(flat[..., off:off + n].reshape(lead + tuple(shp)))
        off += n + ((-n) % 128)
    return out


def _as2d(a):
    return a.reshape(-1, a.shape[-1])


_LATE = ("w_a_out", "w_b_out", "w_pool", "w_o")


class _Traffic:
    def __init__(self, slots, plan, core, chip):
        self.slots = slots
        self.plan = plan
        self.core = core
        self.place = jnp.concatenate([core, chip])
        self.gathered = {}
        self.ready = {}
        self.summed = {}
        self.half = {}
        self.final = {}

    def weight(self, layer, name):
        return self.gathered[(layer, name)]

    @staticmethod
    def _same_rows(keys, arrays):
        groups = []
        for k, a in zip(keys, arrays):
            if groups and groups[-1][1][-1].shape[-2] == a.shape[-2]:
                groups[-1][0].append(k)
                groups[-1][1].append(a)
            else:
                groups.append(([k], [a]))
        return groups

    def add_grad(self, layer, name, g_f32, g_bf16):
        def halves(g):
            g = _grad_by_chip(name, g)
            return g.reshape(N_CHIPS, 2, g.shape[1] // 2, g.shape[2])
        self.ready[(layer, name)] = (halves(g_f32), halves(g_bf16))

    def _comm(self, job):
        if job[0] == "gather":
            return _gather_comm([self.slots[(job[1], k)] for k in job[2]], *job[3:])
        if job[0] == "presum":
            return _presum_comm([self.ready[k][1] for k in job[1]])
        if job[0] == "scatter":
            return _scatter_comm([self.summed[k][0] for k in job[1]])
        return _join_comm([self.half[k] for k in job[1]])

    def _done(self, job, res):
        if job[0] == "gather":
            for k, r in zip(job[2], res):
                self.slots[(job[1], k)] = r
                self.gathered[(job[1], k)] = _use_gathered(k, r)
        elif job[0] == "presum":
            for keys, rs in self._same_rows(job[1], res):
                nm = f"presum_l{keys[0][0]}_{keys[0][1] if len(keys) == 1 else 'late'}"
                outs = self.run(nm, lambda cm: _sum_halves([self.ready.pop(k)[0] for k in keys], rs, self.place,
                                                           name=nm, comm=cm))
                for i, k in enumerate(keys):
                    self.summed[k] = (outs[i], outs[len(keys) + i])
        elif job[0] == "scatter":
            for keys, rs in self._same_rows(job[1], res):
                nm = f"sum_l{keys[0][0]}_{keys[0][1] if len(keys) == 1 else 'late'}"
                outs = _sum4_into_half([self.summed.pop(k)[1] for k in keys], rs, self.core, name=nm)
                self.half.update(zip(keys, outs))
        else:
            for k, r in zip(job[1], res):
                self.final[k] = r.reshape(-1, r.shape[-1])

    def run(self, name, fn):
        jobs = self.plan.get(name)
        if not jobs:
            return fn(None)
        comms = [self._comm(j) for j in jobs]
        outs, res = fn(_merge(comms))
        for job, r in zip(jobs, _split(comms, res)):
            self._done(job, r)
        return outs

    def alone(self, name):
        jobs = self.plan[name]
        comms = [self._comm(j) for j in jobs]
        for job, r in zip(jobs, _split(comms, _comm_call(_merge(comms), name=name))):
            self._done(job, r)


def _layer_fwd(x, ada, p, l, tr, tgt=None):
    sh1, sc1, gt1, sh2, sc2, gt2 = ada
    n = f"l{l}"
    z, ht = tr.run(f"{n}_in_proj", lambda cm: _mod_matmul(
        x, sc1, sh1, tr.weight(l, "w_in"), p["b_in"], ts=1024, tn=2304, name=f"{n}_in_proj", comm=cm))
    a, ya = tr.run(f"{n}_mix_a", lambda cm: _mix_a_fwd(z, p["conv_a"], tr.weight(l, "w_a_out"), ts=256,
                                                      name=f"{n}_mix_a", comm=cm))
    sg, yb = tr.run(f"{n}_mix_b", lambda cm: _mix_b_fwd(
        z, p["ln_v_g"], p["ln_v_b"], p["w_spatial"], p["b_spatial_t"], tr.weight(l, "w_b_out"), ts=256,
        name=f"{n}_mix_b", comm=cm))
    dpool, yc = tr.run(f"{n}_mix_c", lambda cm: _mix_c_fwd(z, tr.weight(l, "w_pool"), p["pool_scale"], ts=256,
                                                          name=f"{n}_mix_c", comm=cm))
    merged, o, x1 = tr.run(f"{n}_mix_o", lambda cm: _mix_o_fwd(
        x, z, ya, yb, yc, tr.weight(l, "w_o"), gt1, p["ln1_g"], p["ln1_b"], ts=256, name=f"{n}_mix_o", comm=cm))
    up, h2t = tr.run(f"{n}_up_proj", lambda cm: _mod_matmul(
        x1, sc2, sh2, tr.weight(l, "w_up"), p["b_up"], ts=1024, tn=1408, name=f"{n}_up_proj", comm=cm))
    ft, dn, *x2 = tr.run(f"{n}_ffn", lambda cm: _ffn_fwd(
        up, x1, p["conv_ffn"], p["conv_ffn_b"], tr.weight(l, "w_down"), gt2, p["ln2_g"], p["ln2_b"], ts=256,
        name=f"{n}_ffn", comm=cm, tgt=tgt))
    saved = dict(x=x, z=z, ht=ht, a=a, ya=ya, sg=sg, yb=yb, dpool=dpool, yc=yc, merged=merged, o=o, x1=x1, h2t=h2t,
                 up=up, ft=ft, dn=dn)
    return (x2[0] if tgt is None else tuple(x2)), saved


def _layer_bwd(dx2, ada, p, sv, l, tr):
    sh1, sc1, gt1, sh2, sc2, gt2 = ada
    n = f"l{l}"
    ddn, dup, dx1, red_d, red_f, dbup = tr.run(f"{n}_ffn_bwd", lambda cm: _ffn_bwd(
        dx2, sv["x1"], sv["dn"], sv["up"], p["conv_ffn"], p["conv_ffn_b"], tr.weight(l, "w_down"),
        tr.weight(l, "w_up"), gt2, p["ln2_g"], sc2, ts=256, name=f"{n}_ffn_bwd", comm=cm))
    g = {}
    tr.add_grad(l, "w_down", *_grad_matmul_t(sv["ft"], ddn, tk=D_FF // N_CHIPS, tn=D_MODEL, name=f"{n}_dw_down"))
    tr.add_grad(l, "w_up", *tr.run(f"{n}_dw_up", lambda cm: _grad_matmul_t(
        sv["h2t"], dup, tk=D_MODEL, tn=FF_CHUNK, name=f"{n}_dw_up", by_chip=True, comm=cm)))
    g["ln2_g"], g["ln2_b"] = red_d[0], red_d[1]
    g["conv_ffn"], g["conv_ffn_b"], g["b_up"] = red_f[0:3], red_f[3], dbup[0]

    dxa, dz, dya, dyb, dyc, red_o, dwo, dwo_b = tr.run(f"{n}_mix_o_bwd", lambda cm: _mix_o_bwd(
        dx1, sv["x"], sv["o"], sv["z"], sv["ya"], sv["yb"], sv["yc"], sv["merged"], tr.weight(l, "w_o"), gt1,
        p["ln1_g"], ts=256, name=f"{n}_mix_o_bwd", comm=cm))
    tr.add_grad(l, "w_o", dwo, dwo_b)
    g["ln1_g"], g["ln1_b"] = red_o[0], red_o[1]

    dz, red_a, dwa, dwa_b = tr.run(f"{n}_mix_a_bwd", lambda cm: _mix_a_bwd(
        dya, sv["a"], sv["z"], dz, p["conv_a"], tr.weight(l, "w_a_out"), ts=256, name=f"{n}_mix_a_bwd", comm=cm))
    tr.add_grad(l, "w_a_out", dwa, dwa_b)
    g["conv_a"] = red_a[0:3]

    dz, red_b, dws, dbst, dwb, dwb_b = tr.run(f"{n}_mix_b_bwd", lambda cm: _mix_b_bwd(
        dyb, sv["sg"], sv["z"], dz, p["ln_v_g"], p["ln_v_b"], p["w_spatial"], p["b_spatial_t"],
        tr.weight(l, "w_b_out"), ts=256, name=f"{n}_mix_b_bwd", comm=cm))
    tr.add_grad(l, "w_b_out", dwb, dwb_b)
    g["ln_v_g"], g["ln_v_b"], g["w_spatial"], g["b_spatial"] = red_b[0], red_b[1], dws, dbst.T

    dz, red_c, dwp = _mix_c_bwd(dyc, sv["z"], dz, tr.weight(l, "w_pool"), p["pool_scale"], ts=256,
                                name=f"{n}_mix_c_bwd")
    g["pool_scale"] = red_c[0]
    tr.add_grad(l, "w_pool", dwp, dwp.astype(BF16))

    tr.add_grad(l, "w_in", *tr.run(f"{n}_dw_in", lambda cm: _grad_matmul_t(
        sv["ht"], dz, tk=D_MODEL, tn=1152, name=f"{n}_dw_in", by_chip=True, comm=cm)))
    if f"{n}_presum_tail" in tr.plan:
        tr.alone(f"{n}_presum_tail")
    dx, red_i, dbin = tr.run(f"{n}_in_proj_bwd", lambda cm: _in_proj_bwd(
        dz, tr.weight(l, "w_in"), dxa, sv["x"], sc1, ts=256, name=f"{n}_in_proj_bwd", comm=cm))
    g["b_in"] = dbin[0]
    dada = jnp.stack([red_i[1], red_i[0], red_o[2], red_d[4], red_d[3], red_d[2]])
    return dx, g, dada


def _traffic_plan():
    plan = {
        "gather_l0": [("gather", 0, ("w_in",) + _LATE)],
        "l0_in_proj": [("gather", 0, ("w_up", "w_down"))],
        "l0_mix_o": [("gather", 1, _LATE)],
        "l0_up_proj": [("gather", 1, ("w_in",), (1, 2))],
        "l0_ffn": [("gather", 1, ("w_in",), (3,))],
        "l1_in_proj": [("gather", 1, ("w_up", "w_down"))],
    }
    for l in reversed(range(DEPTH)):
        late = [(l, k) for k in _LATE]
        plan.update({
            f"l{l}_mix_o_bwd": [("presum", [(l, "w_down"), (l, "w_up")])],
            f"l{l}_mix_b_bwd": [("scatter", [(l, "w_down"), (l, "w_up")])],
            f"l{l}_dw_in": [("presum", late), ("join", [(l, "w_down"), (l, "w_up")])],
        })
    late0, late1 = [(0, k) for k in _LATE], [(1, k) for k in _LATE]
    plan["l1_in_proj_bwd"] = [("presum", [(1, "w_in")]), ("scatter", late1)]
    plan["l0_ffn_bwd"] = [("scatter", [(1, "w_in")]), ("join", late1)]
    plan["l0_dw_up"] = [("join", [(1, "w_in")])]
    plan["l0_presum_tail"] = [("presum", [(0, "w_in")])]
    plan["presum_l0_w_in"] = [("scatter", late0)]
    plan["l0_in_proj_bwd"] = [("scatter", [(0, "w_in")])]
    plan["join_tail"] = [("join", [(0, "w_in")] + late0)]
    return plan


def kernel(x, c, w_ada, b_ada, w_in, b_in, conv_a, w_a_out, ln_v_g, ln_v_b, w_spatial, b_spatial, w_b_out, w_pool, pool_scale, w_o, ln1_g, ln1_b, w_up, b_up, conv_ffn, conv_ffn_b, w_down, ln2_g, ln2_b, loss_target, m_w_ada, m_b_ada, m_w_in, m_b_in, m_conv_a, m_w_a_out, m_ln_v_g, m_ln_v_b, m_w_spatial, m_b_spatial, m_w_b_out, m_w_pool, m_pool_scale, m_w_o, m_ln1_g, m_ln1_b, m_w_up, m_b_up, m_conv_ffn, m_conv_ffn_b, m_w_down, m_ln2_g, m_ln2_b, v_w_ada, v_b_ada, v_w_in, v_b_in, v_conv_a, v_w_a_out, v_ln_v_g, v_ln_v_b, v_w_spatial, v_b_spatial, v_w_b_out, v_w_pool, v_pool_scale, v_w_o, v_ln1_g, v_ln1_b, v_w_up, v_b_up, v_conv_ffn, v_conv_ffn_b, v_w_down, v_ln2_g, v_ln2_b):
    args = locals()
    w = {k: args[k] for k in _WEIGHTS}
    m = {k: args["m_" + k] for k in _WEIGHTS}
    v = {k: args["v_" + k] for k in _WEIGHTS}
    d = D_MODEL
    mx, my, mc = _my_coords()
    chip = 2 * mx + my
    me = 4 * mx + 2 * my + mc

    small_shapes = [c.shape, conv_a.shape, conv_ffn.shape]
    small_all = _all_gather8(_pack_small([c, conv_a, conv_ffn]), name="gather_small")
    c_all, conv_a_st, conv_ffn_st = _unpack_small(small_all, small_shapes)
    c_all = c_all.reshape(N_DEV, d)
    conv_full = {"conv_a": jnp.concatenate([conv_a_st[2 * j] for j in range(N_CHIPS)], axis=-1),
                 "conv_ffn": jnp.concatenate([conv_ffn_st[2 * j] for j in range(N_CHIPS)], axis=-1)}

    chip_idx = jnp.reshape(chip, (1,)).astype(jnp.int32)
    slots = {}
    for l in range(DEPTH):
        bufs = _cast_into_slots([_shard3(w[k]) for k in _BIG], l, chip_idx, name=f"cast_l{l}")
        slots.update({(l, k): b for k, b in zip(_BIG, bufs)})
    tr = _Traffic(slots, _traffic_plan(), jnp.reshape(mc, (1,)).astype(jnp.int32), chip_idx)
    tr.alone("gather_l0")

    n_ada = w_ada.shape[2]
    b_ada_mine = lax.dynamic_slice_in_dim(b_ada, chip * n_ada, n_ada, axis=1)
    ada_part = _ada_fwd(c_all, w_ada, b_ada_mine.reshape(DEPTH, 1, n_ada), name="ada_fwd")
    ada_all = _all_gather8(_pack_small([ada_part]), name="gather_ada")
    ada_st = _unpack_small(ada_all, [ada_part.shape])[0][0::2]
    ada_rows = jnp.concatenate([ada_st[j] for j in range(N_CHIPS)], axis=-1)
    ada_mine = lax.dynamic_index_in_dim(ada_rows, me, axis=1, keepdims=False)

    def layer_params(l):
        p = {k: conv_full[k][l] for k in _SMALL_SHARDED}
        for k in ("b_in", "ln_v_g", "ln_v_b", "pool_scale", "ln1_g", "ln1_b", "b_up", "conv_ffn_b", "ln2_g", "ln2_b"):
            p[k] = w[k][l].reshape(1, -1)
        p["w_spatial"] = w_spatial[l]
        p["b_spatial_t"] = b_spatial[l].T
        return p

    xs = x[0]
    saved, adas, params = [], [], []
    for l in range(DEPTH):
        ada = [ada_mine[l, k * d:(k + 1) * d].reshape(1, d) for k in range(6)]
        p = layer_params(l)
        xs, sv = _layer_fwd(xs, ada, p, l, tr, tgt=loss_target[0] if l == DEPTH - 1 else None)
        saved.append(sv), adas.append(ada), params.append(p)
    dx, loss_blk = xs

    grads, dadas = [None] * DEPTH, [None] * DEPTH
    for l in reversed(range(DEPTH)):
        dx, grads[l], dadas[l] = _layer_bwd(dx, adas[l], params[l], saved[l], l, tr)
    tr.alone("join_tail")
    dada = jnp.stack(dadas).reshape(DEPTH, 6 * d)

    small_names = _SMALL_REPL + _SMALL_SHARDED
    small_g = [jnp.stack([grads[l][k] for l in range(DEPTH)]) for k in small_names]
    gsum = dict(zip(small_names, _unpack_small(_all_reduce_small(_pack_small(small_g), name="reduce_small"),
                                               [a.shape for a in small_g])))
    tail_g = [dada, loss_blk[0:1, 0:1]]
    tail_all, tail_sum = _all_gather8(_pack_small(tail_g), name="gather_dada", with_sum=True)
    gsum["b_ada"], loss_sum = _unpack_small(tail_sum, [a.shape for a in tail_g])
    loss = loss_sum[0, 0]
    dada_all = _unpack_small(tail_all, [a.shape for a in tail_g])[0]
    for k in _SMALL_SHARDED:
        wd = gsum[k].shape[-1] // N_CHIPS
        gsum[k] = lax.dynamic_slice_in_dim(gsum[k], chip * wd, wd, axis=gsum[k].ndim - 1)

    dada_cols = lax.dynamic_slice_in_dim(dada_all, chip * n_ada, n_ada, axis=2)
    dada_cols = jnp.pad(jnp.swapaxes(dada_cols, 0, 1), ((0, 0), (0, N_DEV), (0, 0)))
    gsum["w_ada"] = _ada_bwd(jnp.pad(c_all, ((0, N_DEV), (0, 0))), dada_cols, name="ada_bwd")

    out_g, out_d, out_m, out_v = {}, {}, {}, {}
    for k in _WEIGHTS:
        shp = w[k].shape
        if k in _BIG:
            res = tr.run(f"adamw_{k}", lambda cm: _adamw_sharded(
                _shard3(w[k]), _shard3(m[k]), _shard3(v[k]), [tr.final[(l, k)] for l in range(DEPTH)],
                name=f"adamw_{k}", comm=cm))
        else:
            gk = gsum[k].reshape(shp)
            res = [gk] + list(_adamw(_as2d(w[k]), _as2d(gk), _as2d(m[k]), _as2d(v[k]), name=f"adamw_{k}"))
        out_g[k], out_d[k], out_m[k], out_v[k] = [r.reshape(shp) for r in res]

    return (loss, dx[None], *[out_g[k] for k in _WEIGHTS], *[out_d[k] for k in _WEIGHTS],
            *[out_m[k] for k in _WEIGHTS], *[out_v[k] for k in _WEIGHTS])
```
